```python
import math
import jax, jax.numpy as jnp
from jax import lax
import numpy as np

D_MODEL = 1024
BATCH = 8
SEQ = 8192
DEPTH = 1

SSD_INNER = D_MODEL
SSD_HEADS = 16
SSD_HEAD_DIM = SSD_INNER // SSD_HEADS
SSD_GROUPS = 2
SSD_STATE = 128
SSD_CONV = 4
SSD_CHUNK = 128
SSD_XBC = SSD_INNER + 2 * SSD_GROUPS * SSD_STATE

SB_HEADS = 16
SB_HEAD_DIM = 64
SB_WIDTH = SB_HEADS * SB_HEAD_DIM
SB_BLOCK = 128

MIX_WIDTH = SSD_INNER + SB_WIDTH
IN_PROJ = SSD_INNER + SSD_XBC + SSD_HEADS + 3 * SB_WIDTH

MEM_TOKENS = 256
MEM_HEADS = 4
MEM_HEAD_DIM = D_MODEL // MEM_HEADS

D_FF = 2816
FFN_CONV = 3

EPS = 1e-6

kernel_name = "hymba_ssd_stickbreaking_hybrid_layer"


def rmsnorm(x, w):
    xf = x.astype(jnp.float32)
    y = xf * lax.rsqrt(jnp.mean(xf * xf, axis=-1, keepdims=True) + EPS)
    return y.astype(x.dtype) * w


def causal_dwconv(x, w, b):
    K, C = w.shape
    y = lax.conv_general_dilated(
        x, w[:, None, :].astype(x.dtype), window_strides=(1,), padding=[(K - 1, 0)],
        dimension_numbers=("NWC", "WIO", "NWC"), feature_group_count=C)
    return y + b


def ssd_chunked(xh, dt, A, Bh, Ch):
    b, l, h, p = xh.shape
    n = Bh.shape[-1]
    c = l // SSD_CHUNK
    L = SSD_CHUNK
    dtype = xh.dtype
    a = dt * A
    xc = (xh * dt[..., None].astype(dtype)).reshape(b, c, L, h, p)
    Bc = Bh.reshape(b, c, L, h, n)
    Cc = Ch.reshape(b, c, L, h, n)
    a_cs = jnp.cumsum(a.reshape(b, c, L, h).transpose(0, 3, 1, 2), axis=-1)
    seg = a_cs[..., :, None] - a_cs[..., None, :]
    tril = jnp.tril(jnp.ones((L, L), dtype=bool))
    Lmat = jnp.where(tril, jnp.exp(jnp.where(tril, seg, 0.0)), 0.0).astype(dtype)
    cb = jnp.einsum("bclhn,bcshn->bhcls", Cc, Bc) * Lmat
    y_diag = jnp.einsum("bhcls,bcshp->bclhp", cb, xc)
    decay_states = jnp.exp(a_cs[..., -1:] - a_cs).astype(dtype)
    states = jnp.einsum("bclhn,bhcl,bclhp->bchpn", Bc, decay_states, xc)
    chunk_decay = jnp.exp(a_cs[..., -1]).astype(dtype)

    def step(carry, inp):
        s, d = inp
        return carry * d[..., None, None] + s, carry

    init = jnp.zeros((b, h, p, n), dtype=states.dtype)
    _, prev = lax.scan(step, init, (states.transpose(1, 0, 2, 3, 4), chunk_decay.transpose(2, 0, 1)))
    prev = prev.transpose(1, 0, 2, 3, 4)
    y_off = jnp.einsum("bclhn,bchpn,bhcl->bclhp", Cc, prev, jnp.exp(a_cs).astype(dtype))
    return (y_diag + y_off).reshape(b, l, h, p)


def ssd_mixer(z, xbc, dt_raw, conv_w, conv_b, dt_bias, a_log, d_skip, norm_w):
    b, l, _ = xbc.shape
    xbc = jax.nn.silu(causal_dwconv(xbc, conv_w, conv_b))
    xs = xbc[..., :SSD_INNER]
    Bm = xbc[..., SSD_INNER:SSD_INNER + SSD_GROUPS * SSD_STATE].reshape(b, l, SSD_GROUPS, SSD_STATE)
    Cm = xbc[..., SSD_INNER + SSD_GROUPS * SSD_STATE:].reshape(b, l, SSD_GROUPS, SSD_STATE)
    rep = SSD_HEADS // SSD_GROUPS
    Bh = jnp.repeat(Bm, rep, axis=2)
    Ch = jnp.repeat(Cm, rep, axis=2)
    xh = xs.reshape(b, l, SSD_HEADS, SSD_HEAD_DIM)
    dt = jax.nn.softplus((dt_raw + dt_bias).astype(jnp.float32))
    A = -jnp.exp(a_log.astype(jnp.float32))
    y = ssd_chunked(xh, dt, A, Bh, Ch)
    y = y + xh * d_skip[:, None]
    y = y.reshape(b, l, SSD_INNER) * jax.nn.silu(z)
    yg = y.reshape(b, l, SSD_GROUPS, SSD_INNER // SSD_GROUPS)
    yg = rmsnorm(yg, norm_w.reshape(SSD_GROUPS, SSD_INNER // SSD_GROUPS))
    return yg.reshape(b, l, SSD_INNER)


def stick_breaking_attention(q, k, v, norm_w):
    b, S, _ = q.shape
    nb = S // SB_BLOCK
    qh = q.reshape(b, S, SB_HEADS, SB_HEAD_DIM).transpose(0, 2, 1, 3)
    kh = k.reshape(b, S, SB_HEADS, SB_HEAD_DIM).transpose(0, 2, 1, 3)
    vh = v.reshape(b, S, SB_HEADS, SB_HEAD_DIM).transpose(0, 2, 1, 3)
    qb = qh.reshape(b, SB_HEADS, nb, SB_BLOCK, SB_HEAD_DIM).transpose(2, 0, 1, 3, 4)
    starts = jnp.arange(nb, dtype=jnp.int32) * SB_BLOCK
    scale = 1.0 / math.sqrt(SB_HEAD_DIM)
    s_idx = jnp.arange(S, dtype=jnp.int32)

    def block(args):
        qblk, start = args
        zz = jnp.einsum("bhqd,bhkd->bhqk", qblk, kh).astype(jnp.float32) * scale
        t_idx = start + jnp.arange(SB_BLOCK, dtype=jnp.int32)
        causal = s_idx[None, :] < t_idx[:, None]
        lnb = jnp.where(causal, jax.nn.log_sigmoid(-zz), 0.0)
        excl = lax.cumsum(lnb, axis=3, reverse=True) - lnb
        A = jnp.where(causal, jnp.exp(jax.nn.log_sigmoid(zz) + excl), 0.0)
        return jnp.einsum("bhqk,bhkd->bhqd", A.astype(vh.dtype), vh)

    out = lax.map(block, (qb, starts))
    out = out.transpose(1, 0, 3, 2, 4).reshape(b, S, SB_HEADS, SB_HEAD_DIM)
    out = rmsnorm(out, norm_w.reshape(SB_HEADS, SB_HEAD_DIM))
    return out.reshape(b, S, SB_WIDTH)


def memory_cross_attention(h, m, w_q, w_k, w_v, w_o):
    b, S, _ = h.shape
    M = m.shape[1]
    q = (h @ w_q).reshape(b, S, MEM_HEADS, MEM_HEAD_DIM)
    k = (m @ w_k).reshape(b, M, MEM_HEADS, MEM_HEAD_DIM)
    v = (m @ w_v).reshape(b, M, MEM_HEADS, MEM_HEAD_DIM)
    sc = jnp.einsum("bshd,bmhd->bhsm", q, k).astype(jnp.float32) / math.sqrt(MEM_HEAD_DIM)
    p = jax.nn.softmax(sc, axis=-1).astype(v.dtype)
    o = jnp.einsum("bhsm,bmhd->bshd", p, v).reshape(b, S, D_MODEL)
    return o @ w_o


def conv_glu_ffn(h, w_up, conv_w, conv_b, w_down):
    u = causal_dwconv(h @ w_up, conv_w, conv_b)
    g, val = u[..., :D_FF], u[..., D_FF:]
    return (jax.nn.silu(g) * val) @ w_down


def _fwd_setup_inputs(seed: int = 0) -> dict:
    key = jax.random.key(seed)
    ks = jax.random.split(key, 32)
    f32 = jnp.float32

    def nrm(k, shape, fan_in):
        return jax.random.normal(k, shape, f32) * (fan_in ** -0.5)

    def gain(k, shape):
        return 1.0 + 0.02 * jax.random.normal(k, shape, f32)

    Ld = DEPTH
    dt0 = jnp.exp(jax.random.uniform(ks[8], (Ld, SSD_HEADS), f32, math.log(1e-3), math.log(1e-1)))
    dt_bias = dt0 + jnp.log(-jnp.expm1(-dt0))
    return {
        "x": jax.random.normal(ks[0], (BATCH, SEQ, D_MODEL), f32),
        "mem": jax.random.normal(ks[1], (BATCH, MEM_TOKENS, D_MODEL), f32),
        "norm_mix_w": gain(ks[2], (Ld, D_MODEL)),
        "w_in": nrm(ks[3], (Ld, D_MODEL, IN_PROJ), D_MODEL),
        "conv_ssd_w": nrm(ks[4], (Ld, SSD_CONV, SSD_XBC), SSD_CONV),
        "conv_ssd_b": 0.02 * jax.random.normal(ks[5], (Ld, SSD_XBC), f32),
        "dt_bias": dt_bias,
        "a_log": jnp.log(jax.random.uniform(ks[6], (Ld, SSD_HEADS), f32, 1.0, 16.0)),
        "d_skip": gain(ks[7], (Ld, SSD_HEADS)),
        "ssd_norm_w": gain(ks[9], (Ld, SSD_INNER)),
        "sb_norm_w": gain(ks[10], (Ld, SB_WIDTH)),
        "w_out": nrm(ks[11], (Ld, MIX_WIDTH, D_MODEL), MIX_WIDTH),
        "norm_mem_w": gain(ks[12], (Ld, D_MODEL)),
        "norm_memkv_w": gain(ks[13], (Ld, D_MODEL)),
        "w_mq": nrm(ks[14], (Ld, D_MODEL, D_MODEL), D_MODEL),
        "w_mk": nrm(ks[15], (Ld, D_MODEL, D_MODEL), D_MODEL),
        "w_mv": nrm(ks[16], (Ld, D_MODEL, D_MODEL), D_MODEL),
        "w_mo": nrm(ks[17], (Ld, D_MODEL, D_MODEL), D_MODEL),
        "norm_ffn_w": gain(ks[18], (Ld, D_MODEL)),
        "w_up": nrm(ks[19], (Ld, D_MODEL, 2 * D_FF), D_MODEL),
        "conv_ffn_w": nrm(ks[20], (Ld, FFN_CONV, 2 * D_FF), FFN_CONV),
        "conv_ffn_b": 0.02 * jax.random.normal(ks[21], (Ld, 2 * D_FF), f32),
        "w_down": nrm(ks[22], (Ld, D_FF, D_MODEL), D_FF),
        "norm_final_w": gain(ks[23], (D_MODEL,)),
    }


def _fwd_reference(x, mem, norm_mix_w, w_in, conv_ssd_w, conv_ssd_b, dt_bias, a_log, d_skip,
              ssd_norm_w, sb_norm_w, w_out, norm_mem_w, norm_memkv_w, w_mq, w_mk, w_mv, w_mo,
              norm_ffn_w, w_up, conv_ffn_w, conv_ffn_b, w_down, norm_final_w):
    o1 = SSD_INNER
    o2 = o1 + SSD_XBC
    o3 = o2 + SSD_HEADS
    o4 = o3 + SB_WIDTH
    o5 = o4 + SB_WIDTH
    for l in range(DEPTH):
        h = rmsnorm(x, norm_mix_w[l])
        proj = h @ w_in[l]
        z, xbc, dt_raw = proj[..., :o1], proj[..., o1:o2], proj[..., o2:o3]
        q, k, v = proj[..., o3:o4], proj[..., o4:o5], proj[..., o5:]
        y_ssd = ssd_mixer(z, xbc, dt_raw, conv_ssd_w[l], conv_ssd_b[l], dt_bias[l],
                          a_log[l], d_skip[l], ssd_norm_w[l])
        y_sb = stick_breaking_attention(q, k, v, sb_norm_w[l])
        x = x + jnp.concatenate([y_ssd, y_sb], axis=-1) @ w_out[l]
        h = rmsnorm(x, norm_mem_w[l])
        m = rmsnorm(mem, norm_memkv_w[l])
        x = x + memory_cross_attention(h, m, w_mq[l], w_mk[l], w_mv[l], w_mo[l])
        h = rmsnorm(x, norm_ffn_w[l])
        x = x + conv_glu_ffn(h, w_up[l], conv_ffn_w[l], conv_ffn_b[l], w_down[l])
    return rmsnorm(x, norm_final_w)


import jax as _jax
import jax.numpy as _jnp

TWIN_FORMAT = 'train_step'
FWD_PARAMS = ['x', 'mem', 'norm_mix_w', 'w_in', 'conv_ssd_w', 'conv_ssd_b', 'dt_bias', 'a_log', 'd_skip', 'ssd_norm_w', 'sb_norm_w', 'w_out', 'norm_mem_w', 'norm_memkv_w', 'w_mq', 'w_mk', 'w_mv', 'w_mo', 'norm_ffn_w', 'w_up', 'conv_ffn_w', 'conv_ffn_b', 'w_down', 'norm_final_w']
TWIN_WEIGHTS = ['norm_mix_w', 'w_in', 'conv_ssd_w', 'conv_ssd_b', 'dt_bias', 'a_log', 'd_skip', 'ssd_norm_w', 'sb_norm_w', 'w_out', 'norm_mem_w', 'norm_memkv_w', 'w_mq', 'w_mk', 'w_mv', 'w_mo', 'norm_ffn_w', 'w_up', 'conv_ffn_w', 'conv_ffn_b', 'w_down', 'norm_final_w']
TWIN_DIFF_INPUT = 'x'
TWIN_INPUTS = ['x', 'mem', 'norm_mix_w', 'w_in', 'conv_ssd_w', 'conv_ssd_b', 'dt_bias', 'a_log', 'd_skip', 'ssd_norm_w', 'sb_norm_w', 'w_out', 'norm_mem_w', 'norm_memkv_w', 'w_mq', 'w_mk', 'w_mv', 'w_mo', 'norm_ffn_w', 'w_up', 'conv_ffn_w', 'conv_ffn_b', 'w_down', 'norm_final_w', 'loss_target', 'm_norm_mix_w', 'm_w_in', 'm_conv_ssd_w', 'm_conv_ssd_b', 'm_dt_bias', 'm_a_log', 'm_d_skip', 'm_ssd_norm_w', 'm_sb_norm_w', 'm_w_out', 'm_norm_mem_w', 'm_norm_memkv_w', 'm_w_mq', 'm_w_mk', 'm_w_mv', 'm_w_mo', 'm_norm_ffn_w', 'm_w_up', 'm_conv_ffn_w', 'm_conv_ffn_b', 'm_w_down', 'm_norm_final_w', 'v_norm_mix_w', 'v_w_in', 'v_conv_ssd_w', 'v_conv_ssd_b', 'v_dt_bias', 'v_a_log', 'v_d_skip', 'v_ssd_norm_w', 'v_sb_norm_w', 'v_w_out', 'v_norm_mem_w', 'v_norm_memkv_w', 'v_w_mq', 'v_w_mk', 'v_w_mv', 'v_w_mo', 'v_norm_ffn_w', 'v_w_up', 'v_conv_ffn_w', 'v_conv_ffn_b', 'v_w_down', 'v_norm_final_w']
TWIN_OUTPUTS = ['loss', 'grad_x', 'grad_norm_mix_w', 'grad_w_in', 'grad_conv_ssd_w', 'grad_conv_ssd_b', 'grad_dt_bias', 'grad_a_log', 'grad_d_skip', 'grad_ssd_norm_w', 'grad_sb_norm_w', 'grad_w_out', 'grad_norm_mem_w', 'grad_norm_memkv_w', 'grad_w_mq', 'grad_w_mk', 'grad_w_mv', 'grad_w_mo', 'grad_norm_ffn_w', 'grad_w_up', 'grad_conv_ffn_w', 'grad_conv_ffn_b', 'grad_w_down', 'grad_norm_final_w', 'delta_norm_mix_w', 'delta_w_in', 'delta_conv_ssd_w', 'delta_conv_ssd_b', 'delta_dt_bias', 'delta_a_log', 'delta_d_skip', 'delta_ssd_norm_w', 'delta_sb_norm_w', 'delta_w_out', 'delta_norm_mem_w', 'delta_norm_memkv_w', 'delta_w_mq', 'delta_w_mk', 'delta_w_mv', 'delta_w_mo', 'delta_norm_ffn_w', 'delta_w_up', 'delta_conv_ffn_w', 'delta_conv_ffn_b', 'delta_w_down', 'delta_norm_final_w', 'new_m_norm_mix_w', 'new_m_w_in', 'new_m_conv_ssd_w', 'new_m_conv_ssd_b', 'new_m_dt_bias', 'new_m_a_log', 'new_m_d_skip', 'new_m_ssd_norm_w', 'new_m_sb_norm_w', 'new_m_w_out', 'new_m_norm_mem_w', 'new_m_norm_memkv_w', 'new_m_w_mq', 'new_m_w_mk', 'new_m_w_mv', 'new_m_w_mo', 'new_m_norm_ffn_w', 'new_m_w_up', 'new_m_conv_ffn_w', 'new_m_conv_ffn_b', 'new_m_w_down', 'new_m_norm_final_w', 'new_v_norm_mix_w', 'new_v_w_in', 'new_v_conv_ssd_w', 'new_v_conv_ssd_b', 'new_v_dt_bias', 'new_v_a_log', 'new_v_d_skip', 'new_v_ssd_norm_w', 'new_v_sb_norm_w', 'new_v_w_out', 'new_v_norm_mem_w', 'new_v_norm_memkv_w', 'new_v_w_mq', 'new_v_w_mk', 'new_v_w_mv', 'new_v_w_mo', 'new_v_norm_ffn_w', 'new_v_w_up', 'new_v_conv_ffn_w', 'new_v_conv_ffn_b', 'new_v_w_down', 'new_v_norm_final_w']
TWIN_LEAF_KINDS = {'loss': 'loss', 'grad_x': 'grad_x', 'grad_norm_mix_w': 'grad_w', 'grad_w_in': 'grad_w', 'grad_conv_ssd_w': 'grad_w', 'grad_conv_ssd_b': 'grad_w', 'grad_dt_bias': 'grad_w', 'grad_a_log': 'grad_w', 'grad_d_skip': 'grad_w', 'grad_ssd_norm_w': 'grad_w', 'grad_sb_norm_w': 'grad_w', 'grad_w_out': 'grad_w', 'grad_norm_mem_w': 'grad_w', 'grad_norm_memkv_w': 'grad_w', 'grad_w_mq': 'grad_w', 'grad_w_mk': 'grad_w', 'grad_w_mv': 'grad_w', 'grad_w_mo': 'grad_w', 'grad_norm_ffn_w': 'grad_w', 'grad_w_up': 'grad_w', 'grad_conv_ffn_w': 'grad_w', 'grad_conv_ffn_b': 'grad_w', 'grad_w_down': 'grad_w', 'grad_norm_final_w': 'grad_w', 'delta_norm_mix_w': 'delta_w', 'delta_w_in': 'delta_w', 'delta_conv_ssd_w': 'delta_w', 'delta_conv_ssd_b': 'delta_w', 'delta_dt_bias': 'delta_w', 'delta_a_log': 'delta_w', 'delta_d_skip': 'delta_w', 'delta_ssd_norm_w': 'delta_w', 'delta_sb_norm_w': 'delta_w', 'delta_w_out': 'delta_w', 'delta_norm_mem_w': 'delta_w', 'delta_norm_memkv_w': 'delta_w', 'delta_w_mq': 'delta_w', 'delta_w_mk': 'delta_w', 'delta_w_mv': 'delta_w', 'delta_w_mo': 'delta_w', 'delta_norm_ffn_w': 'delta_w', 'delta_w_up': 'delta_w', 'delta_conv_ffn_w': 'delta_w', 'delta_conv_ffn_b': 'delta_w', 'delta_w_down': 'delta_w', 'delta_norm_final_w': 'delta_w', 'new_m_norm_mix_w': 'new_m', 'new_m_w_in': 'new_m', 'new_m_conv_ssd_w': 'new_m', 'new_m_conv_ssd_b': 'new_m', 'new_m_dt_bias': 'new_m', 'new_m_a_log': 'new_m', 'new_m_d_skip': 'new_m', 'new_m_ssd_norm_w': 'new_m', 'new_m_sb_norm_w': 'new_m', 'new_m_w_out': 'new_m', 'new_m_norm_mem_w': 'new_m', 'new_m_norm_memkv_w': 'new_m', 'new_m_w_mq': 'new_m', 'new_m_w_mk': 'new_m', 'new_m_w_mv': 'new_m', 'new_m_w_mo': 'new_m', 'new_m_norm_ffn_w': 'new_m', 'new_m_w_up': 'new_m', 'new_m_conv_ffn_w': 'new_m', 'new_m_conv_ffn_b': 'new_m', 'new_m_w_down': 'new_m', 'new_m_norm_final_w': 'new_m', 'new_v_norm_mix_w': 'new_v', 'new_v_w_in': 'new_v', 'new_v_conv_ssd_w': 'new_v', 'new_v_conv_ssd_b': 'new_v', 'new_v_dt_bias': 'new_v', 'new_v_a_log': 'new_v', 'new_v_d_skip': 'new_v', 'new_v_ssd_norm_w': 'new_v', 'new_v_sb_norm_w': 'new_v', 'new_v_w_out': 'new_v', 'new_v_norm_mem_w': 'new_v', 'new_v_norm_memkv_w': 'new_v', 'new_v_w_mq': 'new_v', 'new_v_w_mk': 'new_v', 'new_v_w_mv': 'new_v', 'new_v_w_mo': 'new_v', 'new_v_norm_ffn_w': 'new_v', 'new_v_w_up': 'new_v', 'new_v_conv_ffn_w': 'new_v', 'new_v_conv_ffn_b': 'new_v', 'new_v_w_down': 'new_v', 'new_v_norm_final_w': 'new_v'}


def _forward(args):
    return _fwd_reference(*[args[k] for k in FWD_PARAMS])


def _output_shape():
    def fwd():
        inp = _fwd_setup_inputs(0)
        return _fwd_reference(*[inp[k] for k in FWD_PARAMS])
    out = _jax.eval_shape(fwd)
    return out.shape, out.dtype

N_MICROBATCH = 1
ADAM_LR = 0.001
ADAM_B1 = 0.9
ADAM_B2 = 0.999
ADAM_EPS = 1e-08
ADAM_WD = 0.01
ADAM_STEP = 10
PER_EXAMPLE_BATCH_AXIS = {'x': 0, 'mem': 0, 'loss_target': 0}
SHARED_INPUTS = []
_WEIGHT_DTYPES = {'norm_mix_w': _jnp.float32, 'w_in': _jnp.float32, 'conv_ssd_w': _jnp.float32, 'conv_ssd_b': _jnp.float32, 'dt_bias': _jnp.float32, 'a_log': _jnp.float32, 'd_skip': _jnp.float32, 'ssd_norm_w': _jnp.float32, 'sb_norm_w': _jnp.float32, 'w_out': _jnp.float32, 'norm_mem_w': _jnp.float32, 'norm_memkv_w': _jnp.float32, 'w_mq': _jnp.float32, 'w_mk': _jnp.float32, 'w_mv': _jnp.float32, 'w_mo': _jnp.float32, 'norm_ffn_w': _jnp.float32, 'w_up': _jnp.float32, 'conv_ffn_w': _jnp.float32, 'conv_ffn_b': _jnp.float32, 'w_down': _jnp.float32, 'norm_final_w': _jnp.float32}
MOMENT_SCALE = {'norm_mix_w': 2.622171e-01, 'w_in': 1.108796e-01, 'conv_ssd_w': 1.167022e-01, 'conv_ssd_b': 1.533431e-01, 'dt_bias': 6.267829e-01, 'a_log': 3.196169e-01, 'd_skip': 1.168047e+00, 'ssd_norm_w': 1.370443e-01, 'sb_norm_w': 1.337071e-01, 'w_out': 1.927783e-01, 'norm_mem_w': 2.029144e-02, 'norm_memkv_w': 3.390645e-02, 'w_mq': 2.001206e-02, 'w_mk': 2.005877e-02, 'w_mv': 2.059232e-02, 'w_mo': 2.043640e-02, 'norm_ffn_w': 1.422747e-01, 'w_up': 6.043765e-02, 'conv_ffn_w': 6.147552e-02, 'conv_ffn_b': 5.966561e-02, 'w_down': 9.859030e-02, 'norm_final_w': 6.400172e+01}


def _to_microbatches(a, axis):
    t = _jnp.moveaxis(a, axis, 0)
    t = t.reshape((N_MICROBATCH, t.shape[0] // N_MICROBATCH) + t.shape[1:])
    return _jnp.moveaxis(t, 1, axis + 1)


def setup_inputs(seed: int = 0) -> dict:
    inp = _fwd_setup_inputs(seed)
    key = _jax.random.fold_in(_jax.random.key(seed), 7919)
    shape, _ = _output_shape()
    out = dict(inp)
    out["loss_target"] = _jax.random.normal(_jax.random.fold_in(key, 0), shape, _jnp.float32)
    for i, name in enumerate(TWIN_WEIGHTS):
        w = inp[name].astype(_jnp.float32)
        if MOMENT_SCALE is None:
            s = _jnp.sqrt(_jnp.mean(_jnp.square(w)) + 1e-30)
        else:
            s = MOMENT_SCALE[name]
        km, kv = _jax.random.split(_jax.random.fold_in(key, i + 1))
        out[name] = w
        out["m_" + name] = s * _jax.random.normal(km, w.shape, _jnp.float32)
        out["v_" + name] = (s * s) * _jax.random.uniform(kv, w.shape, _jnp.float32, 0.5, 1.5)
    if N_MICROBATCH > 1:
        for name, axis in PER_EXAMPLE_BATCH_AXIS.items():
            out[name] = _to_microbatches(out[name], axis)
    return {'x': out['x'], 'mem': out['mem'], 'norm_mix_w': out['norm_mix_w'], 'w_in': out['w_in'], 'conv_ssd_w': out['conv_ssd_w'], 'conv_ssd_b': out['conv_ssd_b'], 'dt_bias': out['dt_bias'], 'a_log': out['a_log'], 'd_skip': out['d_skip'], 'ssd_norm_w': out['ssd_norm_w'], 'sb_norm_w': out['sb_norm_w'], 'w_out': out['w_out'], 'norm_mem_w': out['norm_mem_w'], 'norm_memkv_w': out['norm_memkv_w'], 'w_mq': out['w_mq'], 'w_mk': out['w_mk'], 'w_mv': out['w_mv'], 'w_mo': out['w_mo'], 'norm_ffn_w': out['norm_ffn_w'], 'w_up': out['w_up'], 'conv_ffn_w': out['conv_ffn_w'], 'conv_ffn_b': out['conv_ffn_b'], 'w_down': out['w_down'], 'norm_final_w': out['norm_final_w'], 'loss_target': out['loss_target'], 'm_norm_mix_w': out['m_norm_mix_w'], 'm_w_in': out['m_w_in'], 'm_conv_ssd_w': out['m_conv_ssd_w'], 'm_conv_ssd_b': out['m_conv_ssd_b'], 'm_dt_bias': out['m_dt_bias'], 'm_a_log': out['m_a_log'], 'm_d_skip': out['m_d_skip'], 'm_ssd_norm_w': out['m_ssd_norm_w'], 'm_sb_norm_w': out['m_sb_norm_w'], 'm_w_out': out['m_w_out'], 'm_norm_mem_w': out['m_norm_mem_w'], 'm_norm_memkv_w': out['m_norm_memkv_w'], 'm_w_mq': out['m_w_mq'], 'm_w_mk': out['m_w_mk'], 'm_w_mv': out['m_w_mv'], 'm_w_mo': out['m_w_mo'], 'm_norm_ffn_w': out['m_norm_ffn_w'], 'm_w_up': out['m_w_up'], 'm_conv_ffn_w': out['m_conv_ffn_w'], 'm_conv_ffn_b': out['m_conv_ffn_b'], 'm_w_down': out['m_w_down'], 'm_norm_final_w': out['m_norm_final_w'], 'v_norm_mix_w': out['v_norm_mix_w'], 'v_w_in': out['v_w_in'], 'v_conv_ssd_w': out['v_conv_ssd_w'], 'v_conv_ssd_b': out['v_conv_ssd_b'], 'v_dt_bias': out['v_dt_bias'], 'v_a_log': out['v_a_log'], 'v_d_skip': out['v_d_skip'], 'v_ssd_norm_w': out['v_ssd_norm_w'], 'v_sb_norm_w': out['v_sb_norm_w'], 'v_w_out': out['v_w_out'], 'v_norm_mem_w': out['v_norm_mem_w'], 'v_norm_memkv_w': out['v_norm_memkv_w'], 'v_w_mq': out['v_w_mq'], 'v_w_mk': out['v_w_mk'], 'v_w_mv': out['v_w_mv'], 'v_w_mo': out['v_w_mo'], 'v_norm_ffn_w': out['v_norm_ffn_w'], 'v_w_up': out['v_w_up'], 'v_conv_ffn_w': out['v_conv_ffn_w'], 'v_conv_ffn_b': out['v_conv_ffn_b'], 'v_w_down': out['v_w_down'], 'v_norm_final_w': out['v_norm_final_w']}


def _loss(weights, diff, rest, loss_target):
    with _jax.named_scope("forward"):
        args = {**rest, TWIN_DIFF_INPUT: diff, **{k: w.astype(_WEIGHT_DTYPES[k]) for k, w in weights.items()}}
        y = _forward(args)
    with _jax.named_scope("loss_head"):
        err = _jnp.square(y.astype(_jnp.float32) - loss_target)
        return 0.5 * _jnp.sum(_jnp.mean(err, axis=-1)) if err.ndim else 0.5 * err


def _adamw(w, g, m, v):
    m = ADAM_B1 * m + (1.0 - ADAM_B1) * g
    v = ADAM_B2 * v + (1.0 - ADAM_B2) * _jnp.square(g)
    m_hat = m / (1.0 - ADAM_B1 ** ADAM_STEP)
    v_hat = v / (1.0 - ADAM_B2 ** ADAM_STEP)
    delta = -ADAM_LR * (m_hat / (_jnp.sqrt(v_hat) + ADAM_EPS) + ADAM_WD * w)
    return delta, m, v


def reference(x, mem, norm_mix_w, w_in, conv_ssd_w, conv_ssd_b, dt_bias, a_log, d_skip, ssd_norm_w, sb_norm_w, w_out, norm_mem_w, norm_memkv_w, w_mq, w_mk, w_mv, w_mo, norm_ffn_w, w_up, conv_ffn_w, conv_ffn_b, w_down, norm_final_w, loss_target, m_norm_mix_w, m_w_in, m_conv_ssd_w, m_conv_ssd_b, m_dt_bias, m_a_log, m_d_skip, m_ssd_norm_w, m_sb_norm_w, m_w_out, m_norm_mem_w, m_norm_memkv_w, m_w_mq, m_w_mk, m_w_mv, m_w_mo, m_norm_ffn_w, m_w_up, m_conv_ffn_w, m_conv_ffn_b, m_w_down, m_norm_final_w, v_norm_mix_w, v_w_in, v_conv_ssd_w, v_conv_ssd_b, v_dt_bias, v_a_log, v_d_skip, v_ssd_norm_w, v_sb_norm_w, v_w_out, v_norm_mem_w, v_norm_memkv_w, v_w_mq, v_w_mk, v_w_mv, v_w_mo, v_norm_ffn_w, v_w_up, v_conv_ffn_w, v_conv_ffn_b, v_w_down, v_norm_final_w):
    given = dict(x=x, mem=mem, norm_mix_w=norm_mix_w, w_in=w_in, conv_ssd_w=conv_ssd_w, conv_ssd_b=conv_ssd_b, dt_bias=dt_bias, a_log=a_log, d_skip=d_skip, ssd_norm_w=ssd_norm_w, sb_norm_w=sb_norm_w, w_out=w_out, norm_mem_w=norm_mem_w, norm_memkv_w=norm_memkv_w, w_mq=w_mq, w_mk=w_mk, w_mv=w_mv, w_mo=w_mo, norm_ffn_w=norm_ffn_w, w_up=w_up, conv_ffn_w=conv_ffn_w, conv_ffn_b=conv_ffn_b, w_down=w_down, norm_final_w=norm_final_w, loss_target=loss_target, m_norm_mix_w=m_norm_mix_w, m_w_in=m_w_in, m_conv_ssd_w=m_conv_ssd_w, m_conv_ssd_b=m_conv_ssd_b, m_dt_bias=m_dt_bias, m_a_log=m_a_log, m_d_skip=m_d_skip, m_ssd_norm_w=m_ssd_norm_w, m_sb_norm_w=m_sb_norm_w, m_w_out=m_w_out, m_norm_mem_w=m_norm_mem_w, m_norm_memkv_w=m_norm_memkv_w, m_w_mq=m_w_mq, m_w_mk=m_w_mk, m_w_mv=m_w_mv, m_w_mo=m_w_mo, m_norm_ffn_w=m_norm_ffn_w, m_w_up=m_w_up, m_conv_ffn_w=m_conv_ffn_w, m_conv_ffn_b=m_conv_ffn_b, m_w_down=m_w_down, m_norm_final_w=m_norm_final_w, v_norm_mix_w=v_norm_mix_w, v_w_in=v_w_in, v_conv_ssd_w=v_conv_ssd_w, v_conv_ssd_b=v_conv_ssd_b, v_dt_bias=v_dt_bias, v_a_log=v_a_log, v_d_skip=v_d_skip, v_ssd_norm_w=v_ssd_norm_w, v_sb_norm_w=v_sb_norm_w, v_w_out=v_w_out, v_norm_mem_w=v_norm_mem_w, v_norm_memkv_w=v_norm_memkv_w, v_w_mq=v_w_mq, v_w_mk=v_w_mk, v_w_mv=v_w_mv, v_w_mo=v_w_mo, v_norm_ffn_w=v_norm_ffn_w, v_w_up=v_w_up, v_conv_ffn_w=v_conv_ffn_w, v_conv_ffn_b=v_conv_ffn_b, v_w_down=v_w_down, v_norm_final_w=v_norm_final_w)
    weights = {n: given[n] for n in TWIN_WEIGHTS}
    shared = {n: given[n] for n in SHARED_INPUTS}
    per_example = {n: given[n] for n in ['x', 'mem']}
    grad_fn = _jax.value_and_grad(_loss, argnums=(0, 1))

    def one_microbatch(ex, loss_target):
        ex = dict(ex)
        diff = ex.pop(TWIN_DIFF_INPUT)
        return grad_fn(weights, diff, {**shared, **ex}, loss_target)

    if N_MICROBATCH == 1:
        loss, (grad_w, grad_x) = one_microbatch(per_example, given["loss_target"])
    else:
        def body(carry, xs):
            loss_sum, grad_sum = carry
            l_k, (gw_k, gx_k) = one_microbatch(xs[0], xs[1])
            with _jax.named_scope("update"):
                return (loss_sum + l_k, _jax.tree.map(_jnp.add, grad_sum, gw_k)), gx_k

        init = (_jnp.zeros((), _jnp.float32), _jax.tree.map(_jnp.zeros_like, weights))
        (loss, grad_w), grad_x = _jax.lax.scan(body, init, (per_example, given["loss_target"]))
    with _jax.named_scope("update"):
        delta_w, new_m, new_v = {}, {}, {}
        for n in TWIN_WEIGHTS:
            delta_w[n], new_m[n], new_v[n] = _adamw(weights[n], grad_w[n], given["m_" + n], given["v_" + n])
    return (loss, grad_x, *[grad_w[n] for n in TWIN_WEIGHTS], *[delta_w[n] for n in TWIN_WEIGHTS],
            *[new_m[n] for n in TWIN_WEIGHTS], *[new_v[n] for n in TWIN_WEIGHTS])
```

```python
import functools
import math

import jax
import jax.numpy as jnp
from jax import lax
from jax.experimental import pallas as pl
from jax.experimental.pallas import tpu as pltpu

F32 = jnp.float32
BF16 = jnp.bfloat16

D = 1024
NDEV = 8
EPS = 1e-6
SSD_CHUNK = 128
HALO = 8
VMEM_LIMIT = 56 * 2**20

ADAM_LR, ADAM_B1, ADAM_B2, ADAM_EPS, ADAM_WD, ADAM_STEP = 0.001, 0.9, 0.999, 1e-08, 0.01, 10


def _cp(*sem):
    return pltpu.CompilerParams(dimension_semantics=sem, vmem_limit_bytes=VMEM_LIMIT)


def _tile(n, cap, mult):
    if n <= cap:
        return n
    for d in range(cap - cap % mult, 0, -mult):
        if n % d == 0:
            return d
    raise ValueError(f"no tile for {n}")


def _sigmoid(x):
    return 1.0 / (1.0 + jnp.exp(-x))


def _silu(x):
    return x * _sigmoid(x)


def _softplus(x):
    return jnp.maximum(x, 0.0) + jnp.log1p(jnp.exp(-jnp.abs(x)))


def _terms(x, n):
    out = []
    r = x.astype(F32)
    for i in range(n):
        h = r.astype(BF16)
        out.append(h)
        if i + 1 < n:
            r = r - h.astype(F32)
    return out


_DIMS = {"nn": ((1,), (0,)), "nt": ((1,), (1,)), "tn": ((0,), (0,))}


def _dot_raw(form, a, b, ta, tb):
    acc = None
    for ai in _terms(a, ta):
        for bi in _terms(b, tb):
            d = lax.dot_general(ai, bi, (_DIMS[form], ((), ())), preferred_element_type=F32)
            acc = d if acc is None else acc + d
    return acc


@functools.lru_cache(maxsize=None)
def _dot_fn(form, ta, tb):
    @jax.custom_vjp
    def f(a, b):
        return _dot_raw(form, a, b, ta, tb)

    def fwd(a, b):
        return f(a, b), (a, b)

    def bwd(res, ct):
        a, b = res
        if form == "nn":
            return _dot_fn("nt", ta, tb)(ct, b), _dot_fn("tn", ta, tb)(a, ct)
        if form == "nt":
            return _dot_fn("nn", ta, tb)(ct, b), _dot_fn("tn", tb, ta)(ct, a)
        return _dot_fn("nt", tb, ta)(b, ct), _dot_fn("nn", ta, tb)(a, ct)

    f.defvjp(fwd, bwd)
    return f


def _dot(form, a, b, ta=1, tb=1):
    return _dot_fn(form, ta, tb)(a, b)


@functools.lru_cache(maxsize=None)
def _take_fn(axis, idx):
    @jax.custom_vjp
    def f(x):
        return x[:, idx:idx + 1] if axis == 1 else x[idx:idx + 1, :]

    def fwd(x):
        return f(x), x.shape

    def bwd(shape, ct):
        io = lax.broadcasted_iota(jnp.int32, shape, axis)
        return (jnp.where(io == idx, jnp.broadcast_to(ct, shape), 0.0),)

    f.defvjp(fwd, bwd)
    return f


@functools.lru_cache(maxsize=None)
def _split_fn(width, n):
    @jax.custom_vjp
    def f(x):
        return tuple(x[:, i * width:(i + 1) * width] for i in range(n))

    def fwd(x):
        return f(x), None

    def bwd(_, cts):
        return (jnp.concatenate(list(cts), axis=1),)

    f.defvjp(fwd, bwd)
    return f


def _split(x, width):
    return _split_fn(width, x.shape[1] // width)(x)


def _iota(shape, axis):
    return lax.broadcasted_iota(jnp.int32, shape, axis)


def _mm(a, b, *, name, add=None, trans_a=False, out_dtype=F32):
    if trans_a:
        kt, m = a.shape
    else:
        m, kt = a.shape
    kt2, n = b.shape
    assert kt == kt2, (a.shape, b.shape)
    tm = _tile(m, 512, 128 if trans_a else 8)
    tn = _tile(n, 1536, 128)
    tk = _tile(kt, 1024 if trans_a else 1536, 128)
    nk = kt // tk

    def body(*refs):
        if add is None:
            a_ref, b_ref, o_ref, acc = refs
        else:
            a_ref, b_ref, add_ref, o_ref, acc = refs
        k = pl.program_id(2)

        @pl.when(k == 0)
        def _():
            acc[...] = jnp.zeros_like(acc)

        av = a_ref[...].astype(BF16)
        bv = b_ref[...].astype(BF16)
        dims = _DIMS["tn" if trans_a else "nn"]
        acc[...] += lax.dot_general(av, bv, (dims, ((), ())), preferred_element_type=F32)

        @pl.when(k == nk - 1)
        def _():
            r = acc[...]
            if add is not None:
                r = r + add_ref[...]
            o_ref[...] = r.astype(out_dtype)

    a_spec = (pl.BlockSpec((tk, tm), lambda i, j, k: (k, i)) if trans_a
              else pl.BlockSpec((tm, tk), lambda i, j, k: (i, k)))
    in_specs = [a_spec, pl.BlockSpec((tk, tn), lambda i, j, k: (k, j))]
    args = [a, b]
    if add is not None:
        in_specs.append(pl.BlockSpec((tm, tn), lambda i, j, k: (i, j)))
        args.append(add)
    return pl.pallas_call(
        body, name=name, grid=(m // tm, n // tn, nk),
        in_specs=in_specs, out_specs=pl.BlockSpec((tm, tn), lambda i, j, k: (i, j)),
        out_shape=jax.ShapeDtypeStruct((m, n), out_dtype),
        scratch_shapes=[pltpu.VMEM((tm, tn), F32)],
        compiler_params=_cp("parallel", "parallel", "arbitrary"),
    )(*args)


def _rstd(x):
    return lax.rsqrt(jnp.mean(x * x, axis=-1, keepdims=True) + EPS)


def _norm_fwd(x, w, *, name):
    s = x.shape[0]
    tm = _tile(s, 512, 8)

    def body(x_ref, w_ref, o_ref):
        xv = x_ref[...]
        o_ref[...] = (xv * _rstd(xv) * w_ref[...]).astype(BF16)

    return pl.pallas_call(
        body, name=name, grid=(s // tm,),
        in_specs=[pl.BlockSpec((tm, D), lambda i: (i, 0)), pl.BlockSpec((1, D), lambda i: (0, 0))],
        out_specs=pl.BlockSpec((tm, D), lambda i: (i, 0)),
        out_shape=jax.ShapeDtypeStruct((s, D), BF16), compiler_params=_cp("parallel"),
    )(x, w)


def _norm_bwd_math(xv, wv, dy):
    r = _rstd(xv)
    xh = xv * r
    dxh = dy * wv
    dx = r * (dxh - xh * jnp.mean(dxh * xh, axis=-1, keepdims=True))
    dw = jnp.sum(dy * xh, axis=0, keepdims=True)
    return dx, dw


def _norm_bwd(x, w, dy, add, *, name):
    s = x.shape[0]
    tm = _tile(s, 256, 8)

    def body(*refs):
        if add is None:
            x_ref, w_ref, dy_ref, dx_ref, dw_ref = refs
        else:
            x_ref, w_ref, dy_ref, add_ref, dx_ref, dw_ref = refs

        @pl.when(pl.program_id(0) == 0)
        def _():
            dw_ref[...] = jnp.zeros_like(dw_ref)

        dx, dw = _norm_bwd_math(x_ref[...], w_ref[...], dy_ref[...])
        if add is not None:
            dx = dx + add_ref[...]
        dx_ref[...] = dx
        dw_ref[...] += dw

    row = pl.BlockSpec((tm, D), lambda i: (i, 0))
    vec = pl.BlockSpec((1, D), lambda i: (0, 0))
    in_specs = [row, vec, row] + ([row] if add is not None else [])
    args = [x, w, dy] + ([add] if add is not None else [])
    return pl.pallas_call(
        body, name=name, grid=(s // tm,), in_specs=in_specs, out_specs=[row, vec],
        out_shape=[jax.ShapeDtypeStruct((s, D), F32), jax.ShapeDtypeStruct((1, D), F32)],
        compiler_params=_cp("arbitrary"),
    )(*args)


def _final(x3, w, target, *, name):
    s = x3.shape[0]
    tm = _tile(s, 256, 8)

    def body(x_ref, w_ref, t_ref, dx_ref, dw_ref, loss_ref):
        @pl.when(pl.program_id(0) == 0)
        def _():
            dw_ref[...] = jnp.zeros_like(dw_ref)
            loss_ref[...] = jnp.zeros_like(loss_ref)

        xv = x_ref[...]
        wv = w_ref[...]
        y = xv * _rstd(xv) * wv
        err = y - t_ref[...]
        loss_ref[...] += 0.5 * jnp.sum(jnp.mean(err * err, axis=-1, keepdims=True))
        dx, dw = _norm_bwd_math(xv, wv, err * (1.0 / D))
        dx_ref[...] = dx
        dw_ref[...] += dw

    row = pl.BlockSpec((tm, D), lambda i: (i, 0))
    vec = pl.BlockSpec((1, D), lambda i: (0, 0))
    return pl.pallas_call(
        body, name=name, grid=(s // tm,), in_specs=[row, vec, row], out_specs=[row, vec, vec],
        out_shape=[jax.ShapeDtypeStruct((s, D), F32), jax.ShapeDtypeStruct((1, D), F32),
                   jax.ShapeDtypeStruct((1, D), F32)],
        compiler_params=_cp("arbitrary"),
    )(x3, w, target)


def _head_norm_math(o, w):
    lane = _iota((128, 128), 0) // 64
    bd = (lane == _iota((128, 128), 1) // 64).astype(F32)
    outs = []
    for op in _split(o, 128):
        ms = _dot("nn", op * op, bd, 2, 1) * (1.0 / 64)
        outs.append(op * lax.rsqrt(ms + EPS))
    return jnp.concatenate(outs, axis=1) * w


def _head_norm_fwd(o, w, *, name):
    s = o.shape[0]
    tm = _tile(s, 256, 8)

    def body(o_ref, w_ref, y_ref):
        y_ref[...] = _head_norm_math(o_ref[...], w_ref[...]).astype(BF16)

    row = pl.BlockSpec((tm, D), lambda i: (i, 0))
    vec = pl.BlockSpec((1, D), lambda i: (0, 0))
    return pl.pallas_call(
        body, name=name, grid=(s // tm,), in_specs=[row, vec], out_specs=row,
        out_shape=jax.ShapeDtypeStruct((s, D), BF16), compiler_params=_cp("parallel"),
    )(o, w)


def _head_norm_bwd(o, w, dymix, *, name):
    s = o.shape[0]
    tm = _tile(s, 256, 8)

    def body(o_ref, w_ref, dy_ref, do_ref, dw_ref):
        @pl.when(pl.program_id(0) == 0)
        def _():
            dw_ref[...] = jnp.zeros_like(dw_ref)

        _, vjp = jax.vjp(_head_norm_math, o_ref[...], w_ref[...])
        do, dw = vjp(dy_ref[...])
        do_ref[...] = do
        dw_ref[...] += dw

    row = pl.BlockSpec((tm, D), lambda i: (i, 0))
    vec = pl.BlockSpec((1, D), lambda i: (0, 0))
    return pl.pallas_call(
        body, name=name, grid=(s // tm,),
        in_specs=[row, vec, pl.BlockSpec((tm, D), lambda i: (i, 1))], out_specs=[row, vec],
        out_shape=[jax.ShapeDtypeStruct((s, D), F32), jax.ShapeDtypeStruct((1, D), F32)],
        compiler_params=_cp("arbitrary"),
    )(o, w, dymix)


SB_BQ = 256
SB_BK = 256


def _sb_consts():
    r = _iota((SB_BK, SB_BK), 0)
    c = _iota((SB_BK, SB_BK), 1)
    u_excl = (r > c).astype(BF16)
    u_incl = (r >= c).astype(BF16)
    return u_excl, u_incl


def _sb_tile(qs, kb, mask):
    z = lax.dot_general(qs, kb, (_DIMS["nt"], ((), ())), preferred_element_type=F32)
    e = jnp.exp(-jnp.abs(z))
    l = -(jnp.maximum(z, 0.0) + jnp.log1p(e))
    if mask is not None:
        l = jnp.where(mask, l, 0.0)
    return z, e, l


def _sb_fwd(qkv, *, name):
    s = qkv.shape[0]
    nq = s // SB_BQ
    npair = D // 128
    assert SB_BQ == SB_BK

    def body(q_ref, k_ref, v_ref, o_ref):
        i = pl.program_id(1)
        u_excl, _ = _sb_consts()
        lane_a = _iota((SB_BQ, 128), 1) < 64
        q = q_ref[...] * 0.125
        qa = jnp.where(lane_a, q, 0).astype(BF16)
        qb = jnp.where(lane_a, 0, q).astype(BF16)
        diag = _iota((SB_BQ, SB_BK), 1) < _iota((SB_BQ, SB_BK), 0)

        def tile(j, carry, mask):
            acc, la, lb = carry
            off = pl.multiple_of(j * SB_BK, SB_BK)
            kb = k_ref[pl.ds(off, SB_BK), :]
            vb = v_ref[pl.ds(off, SB_BK), :]
            outs = []
            lsum = []
            for qs, lrun in ((qa, la), (qb, lb)):
                z, _, l = _sb_tile(qs, kb, mask)
                hi, lo = _terms(l, 2)
                ein = (jnp.dot(hi, u_excl, preferred_element_type=F32)
                       + jnp.dot(lo, u_excl, preferred_element_type=F32))
                a = jnp.exp(z + l + ein + lrun)
                if mask is not None:
                    a = jnp.where(mask, a, 0.0)
                outs.append(jnp.dot(a.astype(BF16), vb, preferred_element_type=F32))
                lsum.append(lrun + ein[:, 0:1] + l[:, 0:1])
            acc = acc + jnp.where(lane_a, outs[0], outs[1])
            return acc, lsum[0], lsum[1]

        zero = jnp.zeros((SB_BQ, 1), F32)
        carry = tile(i, (jnp.zeros((SB_BQ, 128), F32), zero, zero), diag)
        carry = lax.fori_loop(0, i, lambda jj, c: tile(i - 1 - jj, c, None), carry)
        o_ref[...] = carry[0]

    return pl.pallas_call(
        body, name=name, grid=(npair, nq),
        in_specs=[pl.BlockSpec((SB_BQ, 128), lambda p, i: (i, p)),
                  pl.BlockSpec((s, 128), lambda p, i: (0, npair + p)),
                  pl.BlockSpec((s, 128), lambda p, i: (0, 2 * npair + p))],
        out_specs=pl.BlockSpec((SB_BQ, 128), lambda p, i: (i, p)),
        out_shape=jax.ShapeDtypeStruct((s, D), F32),
        compiler_params=_cp("parallel", "arbitrary"),
    )(qkv, qkv, qkv)


def _sb_bwd(qkv, o, do, *, name):
    s = qkv.shape[0]
    nq = s // SB_BQ
    npair = D // 128

    def body(q_ref, k_ref, v_ref, o_ref, do_ref, dq_ref, dk_ref, dv_ref):
        i = pl.program_id(1)

        @pl.when(i == 0)
        def _():
            dk_ref[...] = jnp.zeros_like(dk_ref)
            dv_ref[...] = jnp.zeros_like(dv_ref)

        u_excl, u_incl = _sb_consts()
        lane_a = _iota((SB_BQ, 128), 1) < 64
        q = q_ref[...]
        qs8 = q * 0.125
        qa = jnp.where(lane_a, qs8, 0).astype(BF16)
        qb = jnp.where(lane_a, 0, qs8).astype(BF16)
        dov = do_ref[...]
        dob16 = dov.astype(BF16)
        prod = dob16.astype(F32) * o_ref[...]
        delta_a = jnp.sum(jnp.where(lane_a, prod, 0.0), axis=1, keepdims=True)
        delta_b = jnp.sum(jnp.where(lane_a, 0.0, prod), axis=1, keepdims=True)
        doa = jnp.where(lane_a, dov, 0.0).astype(BF16)
        dob = jnp.where(lane_a, 0.0, dov).astype(BF16)
        diag = _iota((SB_BQ, SB_BK), 1) < _iota((SB_BQ, SB_BK), 0)

        def tile(j, carry, mask):
            dq, la, lb, ga, gb = carry
            off = pl.multiple_of(j * SB_BK, SB_BK)
            kb = k_ref[pl.ds(off, SB_BK), :]
            vb = v_ref[pl.ds(off, SB_BK), :]
            dqs, dks, dvs, lnew, gnew = [], [], [], [], []
            for qs, dos, lrun, grun, delta in ((qa, doa, la, ga, delta_a), (qb, dob, lb, gb, delta_b)):
                z, e, l = _sb_tile(qs, kb, mask)
                hi, lo = _terms(l, 2)
                ein = (jnp.dot(hi, u_excl, preferred_element_type=F32)
                       + jnp.dot(lo, u_excl, preferred_element_type=F32))
                a = jnp.exp(z + l + ein + lrun)
                if mask is not None:
                    a = jnp.where(mask, a, 0.0)
                da = lax.dot_general(dos, vb, (_DIMS["nt"], ((), ())), preferred_element_type=F32)
                a16 = a.astype(BF16)
                g = a16.astype(F32) * da
                ghi, glo = _terms(g, 2)
                rg = (jnp.dot(ghi, u_incl, preferred_element_type=F32)
                      + jnp.dot(glo, u_incl, preferred_element_type=F32))
                cs = delta - grun - rg
                inv = 1.0 / (1.0 + e)
                sig = jnp.where(z >= 0, inv, e * inv)
                dz = g * (1.0 - sig) - cs * sig
                if mask is not None:
                    dz = jnp.where(mask, dz, 0.0)
                dz16 = dz.astype(BF16)
                dqs.append(jnp.dot(dz16, kb, preferred_element_type=F32))
                dks.append(lax.dot_general(dz16, q, (_DIMS["tn"], ((), ())), preferred_element_type=F32))
                dvs.append(lax.dot_general(a16, dob16, (_DIMS["tn"], ((), ())), preferred_element_type=F32))
                lnew.append(lrun + ein[:, 0:1] + l[:, 0:1])
                gnew.append(grun + rg[:, 0:1])
            lane_k = _iota((SB_BK, 128), 1) < 64
            dk_ref[pl.ds(off, SB_BK), :] += 0.125 * jnp.where(lane_k, dks[0], dks[1])
            dv_ref[pl.ds(off, SB_BK), :] += jnp.where(lane_k, dvs[0], dvs[1])
            dq = dq + jnp.where(lane_a, dqs[0], dqs[1])
            return dq, lnew[0], lnew[1], gnew[0], gnew[1]

        zero = jnp.zeros((SB_BQ, 1), F32)
        carry = tile(i, (jnp.zeros((SB_BQ, 128), F32), zero, zero, zero, zero), diag)
        carry = lax.fori_loop(0, i, lambda jj, c: tile(i - 1 - jj, c, None), carry)
        dq_ref[...] = 0.125 * carry[0]

    qblk = pl.BlockSpec((SB_BQ, 128), lambda p, i: (i, p))
    slab = pl.BlockSpec((s, 128), lambda p, i: (0, p))
    return pl.pallas_call(
        body, name=name, grid=(npair, nq),
        in_specs=[qblk, pl.BlockSpec((s, 128), lambda p, i: (0, npair + p)),
                  pl.BlockSpec((s, 128), lambda p, i: (0, 2 * npair + p)), qblk, qblk],
        out_specs=[qblk, slab, slab],
        out_shape=[jax.ShapeDtypeStruct((s, D), F32)] * 3,
        compiler_params=_cp("parallel", "arbitrary"),
    )(qkv, qkv, qkv, o, do)


def _ssd_core(z, xpre, dtr, state, dtb, alog, dsk, nw):
    L = SSD_CHUNK
    xa = _silu(xpre)
    pieces = _split(xa, 128)
    xs = jnp.concatenate(pieces[:8], axis=1)
    bm, cm = pieces[8:10], pieces[10:12]
    dt = _softplus(dtr + dtb)
    a = dt * (-jnp.exp(alog))
    tri = (_iota((L, L), 0) >= _iota((L, L), 1)).astype(F32)
    a_cs = _dot("nn", tri, a, 1, 3)
    xc = xs * dt
    tril = _iota((L, L), 0) >= _iota((L, L), 1)
    lane_a = _iota((L, 128), 1) < 64
    acs_p = _split(a_cs, 128)
    xc_p = _split(xc, 128)
    ys, new_states = [], []
    for g in range(2):
        cb = _dot("nt", cm[g], bm[g])
        for pp in range(4):
            pair = 4 * g + pp
            acs = acs_p[pair]
            acs_t = acs.T
            xcp = xc_p[pair]
            st = state[pair]
            heads = []
            for hh in range(2):
                col = _take_fn(1, 64 * hh)(acs)
                row = _take_fn(0, 64 * hh)(acs_t)
                seg = col - row
                lm = jnp.where(tril, jnp.exp(jnp.where(tril, seg, 0.0)), 0.0)
                heads.append(_dot("nn", cb * lm, xcp))
            ydiag = jnp.where(lane_a, heads[0], heads[1])
            last = _take_fn(0, L - 1)(acs)
            snew = _dot("tn", xcp * jnp.exp(last - acs), bm[g])
            new_states.append(st * jnp.exp(_take_fn(1, L - 1)(acs_t)) + snew)
            yoff = _dot("nt", cm[g], st) * jnp.exp(acs)
            ys.append(ydiag + yoff)
    y = jnp.concatenate(ys, axis=1) + xs * dsk
    yg = y * _silu(z)
    outs = []
    for v in _split(yg, 512):
        outs.append(v * lax.rsqrt(jnp.mean(v * v, axis=-1, keepdims=True) + EPS))
    return jnp.concatenate(outs, axis=1) * nw, tuple(new_states)


XBC = 1536


def _ssd_conv(ext_ref, cw, cb):
    acc = cb
    for k in range(4):
        acc = acc + cw[k:k + 1, :] * ext_ref[pl.ds(HALO - 3 + k, SSD_CHUNK), :]
    return acc


def _ssd_fwd(z, xbc, dtr, cw, cb, lanes, *, name):
    s = z.shape[0]
    L = SSD_CHUNK
    nc = s // L

    def body(z_ref, x_ref, h_ref, dtr_ref, cw_ref, cb_ref, ln_ref, y_ref, st_ref, state, ext):
        c = pl.program_id(0)

        @pl.when(c == 0)
        def _():
            state[...] = jnp.zeros_like(state)

        ext[0:HALO, :] = jnp.where(c == 0, 0.0, h_ref[...])
        ext[HALO:, :] = x_ref[...]
        xpre = _ssd_conv(ext, cw_ref[...], cb_ref[...])
        st_ref[0] = state[...]
        st_in = tuple(state[p] for p in range(8))
        yn, st_out = _ssd_core(z_ref[...], xpre, dtr_ref[...], st_in,
                               ln_ref[0:1, :], ln_ref[1:2, :], ln_ref[2:3, :], ln_ref[3:4, :])
        y_ref[...] = yn.astype(BF16)
        for p in range(8):
            state[p] = st_out[p]

    return pl.pallas_call(
        body, name=name, grid=(nc,),
        in_specs=[pl.BlockSpec((L, D), lambda c: (c, 0)),
                  pl.BlockSpec((L, XBC), lambda c: (c, 0)),
                  pl.BlockSpec((HALO, XBC), lambda c: (jnp.maximum(c * (L // HALO) - 1, 0), 0)),
                  pl.BlockSpec((L, D), lambda c: (c, 0)),
                  pl.BlockSpec((4, XBC), lambda c: (0, 0)),
                  pl.BlockSpec((1, XBC), lambda c: (0, 0)),
                  pl.BlockSpec((8, D), lambda c: (0, 0))],
        out_specs=[pl.BlockSpec((L, D), lambda c: (c, 0)),
                   pl.BlockSpec((1, 8, 128, 128), lambda c: (c, 0, 0, 0))],
        out_shape=[jax.ShapeDtypeStruct((s, D), BF16), jax.ShapeDtypeStruct((nc, 8, 128, 128), F32)],
        scratch_shapes=[pltpu.VMEM((8, 128, 128), F32), pltpu.VMEM((L + HALO, XBC), F32)],
        compiler_params=_cp("arbitrary"),
    )(z, xbc, xbc, dtr, cw, cb, lanes)


def _ssd_bwd(z, xbc, dtr, states, dymix, cw, cb, lanes, *, name):
    s = z.shape[0]
    L = SSD_CHUNK
    nc = s // L

    def body(z_ref, x_ref, h_ref, dtr_ref, st_ref, dy_ref, cw_ref, cb_ref, ln_ref,
             dz_ref, dx_ref, ddt_ref, dln_ref, dcv_ref, dstate, ext, dext):
        i = pl.program_id(0)
        c = nc - 1 - i

        @pl.when(i == 0)
        def _():
            dstate[...] = jnp.zeros_like(dstate)
            dext[...] = jnp.zeros_like(dext)
            dln_ref[...] = jnp.zeros_like(dln_ref)
            dcv_ref[...] = jnp.zeros_like(dcv_ref)

        ext[0:HALO, :] = jnp.where(c == 0, 0.0, h_ref[...])
        ext[HALO:, :] = x_ref[...]
        cwv = cw_ref[...]
        xpre = _ssd_conv(ext, cwv, cb_ref[...])
        st_in = tuple(st_ref[0, p] for p in range(8))
        _, vjp = jax.vjp(_ssd_core, z_ref[...], xpre, dtr_ref[...], st_in,
                         ln_ref[0:1, :], ln_ref[1:2, :], ln_ref[2:3, :], ln_ref[3:4, :])
        dz, dxpre, ddtr, dst, d0, d1, d2, d3 = vjp((dy_ref[...], tuple(dstate[p] for p in range(8))))
        for p in range(8):
            dstate[p] = dst[p]
        dz_ref[...] = dz.astype(BF16)
        ddt_ref[...] = ddtr.astype(BF16)
        dln_ref[0:4, :] += jnp.concatenate([d0, d1, d2, d3], axis=0)
        dext[0:L, :] = dxpre
        dx = jnp.zeros((L, XBC), F32)
        rows = []
        for k in range(4):
            dx = dx + cwv[k:k + 1, :] * dext[pl.ds(3 - k, L), :]
            rows.append(jnp.sum(dxpre * ext[pl.ds(HALO - 3 + k, L), :], axis=0, keepdims=True))
        rows.append(jnp.sum(dxpre, axis=0, keepdims=True))
        dx_ref[...] = dx.astype(BF16)
        dcv_ref[0:5, :] += jnp.concatenate(rows, axis=0)
        dext[L:L + HALO, :] = dxpre[0:HALO, :]

    rev = lambda i: (nc - 1 - i, 0)
    return pl.pallas_call(
        body, name=name, grid=(nc,),
        in_specs=[pl.BlockSpec((L, D), rev),
                  pl.BlockSpec((L, XBC), rev),
                  pl.BlockSpec((HALO, XBC), lambda i: (jnp.maximum((nc - 1 - i) * (L // HALO) - 1, 0), 0)),
                  pl.BlockSpec((L, D), rev),
                  pl.BlockSpec((1, 8, 128, 128), lambda i: (nc - 1 - i, 0, 0, 0)),
                  pl.BlockSpec((L, D), rev),
                  pl.BlockSpec((4, XBC), lambda i: (0, 0)),
                  pl.BlockSpec((1, XBC), lambda i: (0, 0)),
                  pl.BlockSpec((8, D), lambda i: (0, 0))],
        out_specs=[pl.BlockSpec((L, D), rev), pl.BlockSpec((L, XBC), rev), pl.BlockSpec((L, D), rev),
                   pl.BlockSpec((8, D), lambda i: (0, 0)), pl.BlockSpec((8, XBC), lambda i: (0, 0))],
        out_shape=[jax.ShapeDtypeStruct((s, D), BF16), jax.ShapeDtypeStruct((s, XBC), BF16),
                   jax.ShapeDtypeStruct((s, D), BF16), jax.ShapeDtypeStruct((8, D), F32),
                   jax.ShapeDtypeStruct((8, XBC), F32)],
        scratch_shapes=[pltpu.VMEM((8, 128, 128), F32), pltpu.VMEM((L + HALO, XBC), F32),
                        pltpu.VMEM((L + HALO, XBC), F32)],
        compiler_params=_cp("arbitrary"),
    )(z, xbc, xbc, dtr, states, dymix, cw, cb, lanes)


def _mem_attn_math(q, k, v):
    outs = []
    for qh, kh, vh in zip(_split(q, 256), _split(k, 256), _split(v, 256)):
        sc = _dot("nt", qh, kh) * (1.0 / 16.0)
        e = jnp.exp(sc - lax.stop_gradient(jnp.max(sc, axis=-1, keepdims=True)))
        p = e / jnp.sum(e, axis=-1, keepdims=True)
        outs.append(_dot("nn", p, vh))
    return jnp.concatenate(outs, axis=1)


def _mem_attn_fwd(q, k, v, *, name):
    s, m = q.shape[0], k.shape[0]
    tm = _tile(s, 256, 8)

    def body(q_ref, k_ref, v_ref, o_ref):
        o_ref[...] = _mem_attn_math(q_ref[...].astype(F32), k_ref[...].astype(F32),
                                    v_ref[...].astype(F32)).astype(BF16)

    row = pl.BlockSpec((tm, D), lambda i: (i, 0))
    kv = pl.BlockSpec((m, D), lambda i: (0, 0))
    return pl.pallas_call(
        body, name=name, grid=(s // tm,), in_specs=[row, kv, kv], out_specs=row,
        out_shape=jax.ShapeDtypeStruct((s, D), BF16), compiler_params=_cp("parallel"),
    )(q, k, v)


def _mem_attn_bwd(q, k, v, do, *, name):
    s, m = q.shape[0], k.shape[0]
    tm = _tile(s, 256, 8)

    def body(q_ref, k_ref, v_ref, do_ref, dq_ref, dk_ref, dv_ref):
        @pl.when(pl.program_id(0) == 0)
        def _():
            dk_ref[...] = jnp.zeros_like(dk_ref)
            dv_ref[...] = jnp.zeros_like(dv_ref)

        _, vjp = jax.vjp(_mem_attn_math, q_ref[...].astype(F32), k_ref[...].astype(F32),
                         v_ref[...].astype(F32))
        dq, dk, dv = vjp(do_ref[...])
        dq_ref[...] = dq.astype(BF16)
        dk_ref[...] += dk
        dv_ref[...] += dv

    row = pl.BlockSpec((tm, D), lambda i: (i, 0))
    kv = pl.BlockSpec((m, D), lambda i: (0, 0))
    return pl.pallas_call(
        body, name=name, grid=(s // tm,), in_specs=[row, kv, kv, row], out_specs=[row, kv, kv],
        out_shape=[jax.ShapeDtypeStruct((s, D), BF16), jax.ShapeDtypeStruct((m, D), F32),
                   jax.ShapeDtypeStruct((m, D), F32)],
        compiler_params=_cp("arbitrary"),
    )(q, k, v, do)


DFF = 2816
FFN_TC = 1408
FFN_TM = 256


def _ffn_conv(ext_ref, cw, cb, tm):
    acc = cb
    for k in range(3):
        acc = acc + cw[k:k + 1, :] * ext_ref[pl.ds(HALO - 2 + k, tm), :]
    return acc


def _ffn_specs(s):
    tm, tc = FFN_TM, FFN_TC
    blk = pl.BlockSpec((tm, tc), lambda i, j: (i, j))
    halo = pl.BlockSpec((HALO, tc), lambda i, j: (jnp.maximum(i * (tm // HALO) - 1, 0), j))
    cw = pl.BlockSpec((3, tc), lambda i, j: (0, j))
    cb = pl.BlockSpec((1, tc), lambda i, j: (0, j))
    return tm, tc, blk, halo, cw, cb


def _glu_fwd(ug, uv, cwg, cwv, cbg, cbv, *, name):
    s = ug.shape[0]
    tm, tc, blk, halo, cw, cb = _ffn_specs(s)

    def body(g_ref, gh_ref, v_ref, vh_ref, cwg_ref, cwv_ref, cbg_ref, cbv_ref, f_ref, eg, ev):
        first = pl.program_id(0) == 0
        eg[0:HALO, :] = jnp.where(first, 0.0, gh_ref[...])
        eg[HALO:, :] = g_ref[...]
        ev[0:HALO, :] = jnp.where(first, 0.0, vh_ref[...])
        ev[HALO:, :] = v_ref[...]
        g = _ffn_conv(eg, cwg_ref[...], cbg_ref[...], tm)
        v = _ffn_conv(ev, cwv_ref[...], cbv_ref[...], tm)
        f_ref[...] = (_silu(g) * v).astype(BF16)

    return pl.pallas_call(
        body, name=name, grid=(s // tm, DFF // tc),
        in_specs=[blk, halo, blk, halo, cw, cw, cb, cb], out_specs=blk,
        out_shape=jax.ShapeDtypeStruct((s, DFF), BF16),
        scratch_shapes=[pltpu.VMEM((tm + HALO, tc), F32)] * 2,
        compiler_params=_cp("parallel", "parallel"),
    )(ug, ug, uv, uv, cwg, cwv, cbg, cbv)


def _glu_bwd(ug, uv, df, cwg, cwv, cbg, cbv, *, name):
    s = ug.shape[0]
    tm, tc, blk, halo, cw, cb = _ffn_specs(s)

    def body(g_ref, gh_ref, v_ref, vh_ref, df_ref, cwg_ref, cwv_ref, cbg_ref, cbv_ref, dg_ref, dv_ref, eg, ev):
        first = pl.program_id(0) == 0
        eg[0:HALO, :] = jnp.where(first, 0.0, gh_ref[...])
        eg[HALO:, :] = g_ref[...]
        ev[0:HALO, :] = jnp.where(first, 0.0, vh_ref[...])
        ev[HALO:, :] = v_ref[...]
        g = _ffn_conv(eg, cwg_ref[...], cbg_ref[...], tm)
        v = _ffn_conv(ev, cwv_ref[...], cbv_ref[...], tm)
        dfv = df_ref[...]
        sg = _sigmoid(g)
        dv_ref[...] = dfv * g * sg
        dg_ref[...] = dfv * v * sg * (1.0 + g * (1.0 - sg))

    return pl.pallas_call(
        body, name=name, grid=(s // tm, DFF // tc),
        in_specs=[blk, halo, blk, halo, blk, cw, cw, cb, cb], out_specs=[blk, blk],
        out_shape=[jax.ShapeDtypeStruct((s, DFF), F32)] * 2,
        scratch_shapes=[pltpu.VMEM((tm + HALO, tc), F32)] * 2,
        compiler_params=_cp("parallel", "parallel"),
    )(ug, ug, uv, uv, df, cwg, cwv, cbg, cbv)


def _ffn_conv_bwd(du, u, cwt, *, name):
    s = du.shape[0]
    tm, tc = FFN_TM, FFN_TC
    nb = s // tm

    def body(du_ref, duh_ref, u_ref, uh_ref, cw_ref, dx_ref, dc_ref, edu, eu):
        i = pl.program_id(1)

        @pl.when(i == 0)
        def _():
            dc_ref[...] = jnp.zeros_like(dc_ref)

        duv = du_ref[...]
        edu[0:tm, :] = duv
        edu[tm:, :] = jnp.where(i == nb - 1, 0.0, duh_ref[...])
        eu[0:HALO, :] = jnp.where(i == 0, 0.0, uh_ref[...])
        eu[HALO:, :] = u_ref[...]
        cwv = cw_ref[...]
        dx = jnp.zeros((tm, tc), F32)
        rows = []
        for k in range(3):
            dx = dx + cwv[k:k + 1, :] * edu[pl.ds(2 - k, tm), :]
            rows.append(jnp.sum(duv * eu[pl.ds(HALO - 2 + k, tm), :], axis=0, keepdims=True))
        rows.append(jnp.sum(duv, axis=0, keepdims=True))
        dx_ref[...] = dx.astype(BF16)
        dc_ref[0:4, :] += jnp.concatenate(rows, axis=0)

    blk = pl.BlockSpec((tm, tc), lambda j, i: (i, j))
    nxt = pl.BlockSpec((HALO, tc), lambda j, i: (jnp.minimum((i + 1) * (tm // HALO), s // HALO - 1), j))
    prv = pl.BlockSpec((HALO, tc), lambda j, i: (jnp.maximum(i * (tm // HALO) - 1, 0), j))
    return pl.pallas_call(
        body, name=name, grid=(DFF // tc, nb),
        in_specs=[blk, nxt, blk, prv, pl.BlockSpec((3, tc), lambda j, i: (0, j))],
        out_specs=[blk, pl.BlockSpec((8, tc), lambda j, i: (0, j))],
        out_shape=[jax.ShapeDtypeStruct((s, DFF), BF16), jax.ShapeDtypeStruct((8, DFF), F32)],
        scratch_shapes=[pltpu.VMEM((tm + HALO, tc), F32)] * 2,
        compiler_params=_cp("parallel", "arbitrary"),
    )(du, du, u, u, cwt)


MESH = pl.DeviceIdType.MESH


def _all_gather(arrs, *, name):
    n = len(arrs)

    def body(*refs):
        x_refs, out_refs = refs[:n], refs[n:2 * n]
        send_sems, recv_sems, local_sems = refs[2 * n:]
        x, y, c = lax.axis_index("x"), lax.axis_index("y"), lax.axis_index("c")
        me, sibling = (x, y, c), (x, y, 1 - c)
        chips = [(1 - x, y), (x, 1 - y), (1 - x, 1 - y)]

        def blk(a, dev):
            return out_refs[a].at[4 * dev[0] + 2 * dev[1] + dev[2]]

        def copy(a, k, block, to, src=None):
            return pltpu.make_async_remote_copy(
                src_ref=blk(a, block) if src is None else src, dst_ref=blk(a, block),
                send_sem=send_sems.at[7 * a + k], recv_sem=recv_sems.at[7 * a + k],
                device_id=to, device_id_type=MESH)

        started = []
        mine = []
        for a in range(n):
            cp = pltpu.make_async_copy(x_refs[a], blk(a, me), local_sems.at[a])
            cp.start()
            mine.append(cp)
            first = [copy(a, 0, me, sibling, src=x_refs[a])]
            first += [copy(a, 1 + j, me, (*chip, c), src=x_refs[a]) for j, chip in enumerate(chips)]
            for cp in first:
                cp.start()
            started += first
        for a in range(n):
            for j, chip in enumerate(chips):
                copy(a, 1 + j, (*chip, c), me).wait_recv()
                fwd = copy(a, 4 + j, (*chip, c), sibling)
                fwd.start()
                started.append(fwd)
        for a in range(n):
            copy(a, 0, sibling, me).wait_recv()
            for j, chip in enumerate(chips):
                copy(a, 4 + j, (*chip, 1 - c), me).wait_recv()
        for cp in started:
            cp.wait_send()
        for cp in mine:
            cp.wait()

    any_spec = pl.BlockSpec(memory_space=pl.ANY)
    return pl.pallas_call(
        body, name=name,
        in_specs=[any_spec] * n, out_specs=[any_spec] * n,
        out_shape=[jax.ShapeDtypeStruct((NDEV,) + a.shape, a.dtype) for a in arrs],
        scratch_shapes=[pltpu.SemaphoreType.DMA((7 * n,)), pltpu.SemaphoreType.DMA((7 * n,)),
                        pltpu.SemaphoreType.DMA((n,))],
    )(*arrs)


def _exchange(send, *, name):
    def body(send_ref, recv_ref, send_sems, recv_sems, local_sem):
        x, y, c = lax.axis_index("x"), lax.axis_index("y"), lax.axis_index("c")
        me = 4 * x + 2 * y + c
        mine = pltpu.make_async_copy(send_ref.at[me], recv_ref.at[me], local_sem)
        mine.start()
        copies = []
        for k in range(1, NDEV):
            px = 1 - x if k & 4 else x
            py = 1 - y if k & 2 else y
            pc = 1 - c if k & 1 else c
            cp = pltpu.make_async_remote_copy(
                src_ref=send_ref.at[4 * px + 2 * py + pc], dst_ref=recv_ref.at[me],
                send_sem=send_sems.at[k - 1], recv_sem=recv_sems.at[k - 1],
                device_id=(px, py, pc), device_id_type=MESH)
            cp.start()
            copies.append(cp)
        for cp in copies:
            cp.wait_recv()
        for cp in copies:
            cp.wait_send()
        mine.wait()

    any_spec = pl.BlockSpec(memory_space=pl.ANY)
    return pl.pallas_call(
        body, name=name, in_specs=[any_spec], out_specs=any_spec,
        out_shape=jax.ShapeDtypeStruct(send.shape, send.dtype),
        scratch_shapes=[pltpu.SemaphoreType.DMA((7,)), pltpu.SemaphoreType.DMA((7,)),
                        pltpu.SemaphoreType.DMA],
    )(send)


def _adamw(parts, w, m, v, *, name):
    r = w.shape[0]
    tm = _tile(r, 256, 8)
    c1 = 1.0 - ADAM_B1 ** ADAM_STEP
    c2 = 1.0 - ADAM_B2 ** ADAM_STEP

    def body(p_ref, w_ref, m_ref, v_ref, g_ref, d_ref, nm_ref, nv_ref):
        g = p_ref[0]
        for i in range(1, NDEV):
            g = g + p_ref[i]
        nm = ADAM_B1 * m_ref[...] + (1.0 - ADAM_B1) * g
        nv = ADAM_B2 * v_ref[...] + (1.0 - ADAM_B2) * (g * g)
        d_ref[...] = -ADAM_LR * ((nm / c1) / (jnp.sqrt(nv / c2) + ADAM_EPS) + ADAM_WD * w_ref[...])
        g_ref[...] = g
        nm_ref[...] = nm
        nv_ref[...] = nv

    row = pl.BlockSpec((tm, D), lambda i: (i, 0))
    return pl.pallas_call(
        body, name=name, grid=(r // tm,),
        in_specs=[pl.BlockSpec((NDEV, tm, D), lambda i: (0, i, 0)), row, row, row],
        out_specs=[row] * 4, out_shape=[jax.ShapeDtypeStruct((r, D), F32)] * 4,
        compiler_params=_cp("parallel"),
    )(parts, w, m, v)


def _rows(a):
    flat = a.reshape(-1)
    pad = (-flat.shape[0]) % D
    if pad:
        flat = jnp.concatenate([flat, jnp.zeros((pad,), flat.dtype)])
    return flat.reshape(-1, D)


def _pack(parts, total_rows):
    rows = [_rows(p) for p in parts]
    used = sum(r.shape[0] for r in rows)
    if total_rows > used:
        rows.append(jnp.zeros((total_rows - used, D), rows[0].dtype))
    return jnp.concatenate(rows, axis=0)


def _unpack(buf, shapes):
    out, r0 = [], 0
    for shp in shapes:
        n = math.prod(shp)
        nr = -(-n // D)
        out.append(buf[r0:r0 + nr].reshape(-1)[:n].reshape(shp))
        r0 += nr
    return out


BIG = [("w_in", (D, 706)), ("w_out", (256, D)), ("w_mq", (128, D)), ("w_mk", (128, D)), ("w_mv", (128, D)),
       ("w_mo", (128, D)), ("w_up", (D, 704)), ("w_down", (352, D)), ("conv_ssd_w", (4, 192)),
       ("conv_ffn_w", (3, 704))]
BIG_ROWS = 2560
CONV_ROW0 = 706 + 256 + 4 * 128 + 704 + 352
SMALL = [("norm_mix_w", (1, D)), ("conv_ssd_b", (1, 1536)), ("dt_bias", (1, 16)), ("a_log", (1, 16)),
         ("d_skip", (1, 16)), ("ssd_norm_w", (1, D)), ("sb_norm_w", (1, D)), ("norm_mem_w", (1, D)),
         ("norm_memkv_w", (1, D)), ("norm_ffn_w", (1, D)), ("conv_ffn_b", (1, 5632)), ("norm_final_w", (D,))]
SMALL_ROWS = 24
LOSS_ROW = 18
ORDER = ["norm_mix_w", "w_in", "conv_ssd_w", "conv_ssd_b", "dt_bias", "a_log", "d_skip", "ssd_norm_w",
         "sb_norm_w", "w_out", "norm_mem_w", "norm_memkv_w", "w_mq", "w_mk", "w_mv", "w_mo", "norm_ffn_w",
         "w_up", "conv_ffn_w", "conv_ffn_b", "w_down", "norm_final_w"]


def _gather_cols(g, r0, nr, rows, cols):
    t = g[:, r0:r0 + nr].reshape(NDEV, rows, cols)
    return t.transpose(1, 0, 2).reshape(rows, NDEV * cols)


def _scatter_cols(full, cols):
    rows = full.shape[0]
    return full.reshape(rows, NDEV, cols).transpose(1, 0, 2).reshape(NDEV, -1, D)


def _pad_rows(a, nr):
    n = a.shape[1]
    return jnp.concatenate([a, jnp.zeros((NDEV, nr * D - n), a.dtype)], axis=1).reshape(NDEV, nr, D)


def _group_sum(lanes):
    return lanes.reshape(16, 64).sum(axis=1).reshape(1, 16)


def kernel(x, mem, norm_mix_w, w_in, conv_ssd_w, conv_ssd_b, dt_bias, a_log, d_skip, ssd_norm_w, sb_norm_w, w_out, norm_mem_w, norm_memkv_w, w_mq, w_mk, w_mv, w_mo, norm_ffn_w, w_up, conv_ffn_w, conv_ffn_b, w_down, norm_final_w, loss_target, m_norm_mix_w, m_w_in, m_conv_ssd_w, m_conv_ssd_b, m_dt_bias, m_a_log, m_d_skip, m_ssd_norm_w, m_sb_norm_w, m_w_out, m_norm_mem_w, m_norm_memkv_w, m_w_mq, m_w_mk, m_w_mv, m_w_mo, m_norm_ffn_w, m_w_up, m_conv_ffn_w, m_conv_ffn_b, m_w_down, m_norm_final_w, v_norm_mix_w, v_w_in, v_conv_ssd_w, v_conv_ssd_b, v_dt_bias, v_a_log, v_d_skip, v_ssd_norm_w, v_sb_norm_w, v_w_out, v_norm_mem_w, v_norm_memkv_w, v_w_mq, v_w_mk, v_w_mv, v_w_mo, v_norm_ffn_w, v_w_up, v_conv_ffn_w, v_conv_ffn_b, v_w_down, v_norm_final_w):
    P = dict(norm_mix_w=norm_mix_w, w_in=w_in, conv_ssd_w=conv_ssd_w, conv_ssd_b=conv_ssd_b, dt_bias=dt_bias, a_log=a_log, d_skip=d_skip, ssd_norm_w=ssd_norm_w, sb_norm_w=sb_norm_w, w_out=w_out, norm_mem_w=norm_mem_w, norm_memkv_w=norm_memkv_w, w_mq=w_mq, w_mk=w_mk, w_mv=w_mv, w_mo=w_mo, norm_ffn_w=norm_ffn_w, w_up=w_up, conv_ffn_w=conv_ffn_w, conv_ffn_b=conv_ffn_b, w_down=w_down, norm_final_w=norm_final_w)
    M = dict(norm_mix_w=m_norm_mix_w, w_in=m_w_in, conv_ssd_w=m_conv_ssd_w, conv_ssd_b=m_conv_ssd_b, dt_bias=m_dt_bias, a_log=m_a_log, d_skip=m_d_skip, ssd_norm_w=m_ssd_norm_w, sb_norm_w=m_sb_norm_w, w_out=m_w_out, norm_mem_w=m_norm_mem_w, norm_memkv_w=m_norm_memkv_w, w_mq=m_w_mq, w_mk=m_w_mk, w_mv=m_w_mv, w_mo=m_w_mo, norm_ffn_w=m_norm_ffn_w, w_up=m_w_up, conv_ffn_w=m_conv_ffn_w, conv_ffn_b=m_conv_ffn_b, w_down=m_w_down, norm_final_w=m_norm_final_w)
    V = dict(norm_mix_w=v_norm_mix_w, w_in=v_w_in, conv_ssd_w=v_conv_ssd_w, conv_ssd_b=v_conv_ssd_b, dt_bias=v_dt_bias, a_log=v_a_log, d_skip=v_d_skip, ssd_norm_w=v_ssd_norm_w, sb_norm_w=v_sb_norm_w, w_out=v_w_out, norm_mem_w=v_norm_mem_w, norm_memkv_w=v_norm_memkv_w, w_mq=v_w_mq, w_mk=v_w_mk, w_mv=v_w_mv, w_mo=v_w_mo, norm_ffn_w=v_norm_ffn_w, w_up=v_w_up, conv_ffn_w=v_conv_ffn_w, conv_ffn_b=v_conv_ffn_b, w_down=v_w_down, norm_final_w=v_norm_final_w)
    big_shapes = [shp for _, shp in BIG]
    small_shapes = [shp for _, shp in SMALL]

    w_big = _pack([P[n][0] for n, _ in BIG], BIG_ROWS)
    g16, g32 = _all_gather([w_big.astype(BF16), w_big[CONV_ROW0:CONV_ROW0 + 8]], name="gather_weights")
    r = 0
    W_in = _gather_cols(g16, r, 706, D, 706); r += 706
    W_out = g16[:, r:r + 256].reshape(2 * D, D); r += 256
    W_mq, W_mk, W_mv, W_mo = [g16[:, r + 128 * i:r + 128 * (i + 1)].reshape(D, D) for i in range(4)]; r += 512
    W_up = _gather_cols(g16, r, 704, D, 704); r += 704
    W_down = g16[:, r:r + 352].reshape(DFF, D)
    cw_ssd = g32[:, 0].reshape(NDEV, -1)[:, :768].reshape(NDEV, 4, 192).transpose(1, 0, 2).reshape(4, XBC)
    cw_ffn = g32[:, 1:4].reshape(NDEV, -1)[:, :2112].reshape(NDEV, 3, 704).transpose(1, 0, 2).reshape(3, 2 * DFF)
    W_z, W_xbc, W_dt, W_qkv = W_in[:, :D], W_in[:, D:D + XBC], W_in[:, D + XBC:D + XBC + 16], W_in[:, D + XBC + 16:]
    W_dtr = jnp.repeat(W_dt, 64, axis=1)
    W_upg, W_upv = W_up[:, :DFF], W_up[:, DFF:]
    cwg, cwv = cw_ffn[:, :DFF], cw_ffn[:, DFF:]
    cbg, cbv = conv_ffn_b[:, :DFF], conv_ffn_b[:, DFF:]
    rep = lambda p: jnp.repeat(p, 64, axis=1)
    lanes = jnp.concatenate([rep(dt_bias), rep(a_log), rep(d_skip), ssd_norm_w, jnp.zeros((4, D), F32)], axis=0)

    xs, tgt, mm = x[0], loss_target[0], mem[0]

    h1 = _norm_fwd(xs, norm_mix_w, name="norm_mix")
    z = _mm(h1, W_z, name="proj_z")
    xbc = _mm(h1, W_xbc, name="proj_xbc")
    dtr = _mm(h1, W_dtr, name="proj_dt")
    qkv = _mm(h1, W_qkv, name="proj_qkv", out_dtype=BF16)
    y_ssd, states = _ssd_fwd(z, xbc, dtr, cw_ssd, conv_ssd_b, lanes, name="ssd_fwd")
    o_sb = _sb_fwd(qkv, name="sb_fwd")
    y_sb = _head_norm_fwd(o_sb, sb_norm_w, name="sb_norm")
    ymix = jnp.concatenate([y_ssd, y_sb], axis=1)
    x1 = _mm(ymix, W_out, add=xs, name="proj_out")
    h2 = _norm_fwd(x1, norm_mem_w, name="norm_mem")
    mn = _norm_fwd(mm, norm_memkv_w, name="norm_memkv")
    qm = _mm(h2, W_mq, name="mem_q", out_dtype=BF16)
    km = _mm(mn, W_mk, name="mem_k", out_dtype=BF16)
    vm = _mm(mn, W_mv, name="mem_v", out_dtype=BF16)
    om = _mem_attn_fwd(qm, km, vm, name="mem_attn")
    x2 = _mm(om, W_mo, add=x1, name="mem_o")
    h3 = _norm_fwd(x2, norm_ffn_w, name="norm_ffn")
    ug = _mm(h3, W_upg, name="ffn_up_g")
    uv = _mm(h3, W_upv, name="ffn_up_v")
    f = _glu_fwd(ug, uv, cwg, cwv, cbg, cbv, name="ffn_glu")
    x3 = _mm(f, W_down, add=x2, name="ffn_down")
    dx3, g_nfinal, loss_part = _final(x3, norm_final_w.reshape(1, D), tgt, name="final_loss")

    G = {}
    G["w_down"] = _mm(f, dx3, trans_a=True, name="g_w_down")
    df = _mm(dx3, W_down.T, name="d_f")
    dug, duv = _glu_bwd(ug, uv, df, cwg, cwv, cbg, cbv, name="ffn_glu_bwd")
    dupg, dcg = _ffn_conv_bwd(dug, ug, cwg, name="ffn_conv_bwd_g")
    dupv, dcv = _ffn_conv_bwd(duv, uv, cwv, name="ffn_conv_bwd_v")
    G["w_up"] = jnp.concatenate([_mm(h3, dupg, trans_a=True, name="g_w_up_g"),
                                 _mm(h3, dupv, trans_a=True, name="g_w_up_v")], axis=1)
    G["conv_ffn_w"] = jnp.concatenate([dcg[0:3], dcv[0:3]], axis=1)
    G["conv_ffn_b"] = jnp.concatenate([dcg[3:4], dcv[3:4]], axis=1)
    dh3 = _mm(dupg, W_upg.T, name="d_h3_g")
    dh3 = _mm(dupv, W_upv.T, add=dh3, name="d_h3_v")
    dx2, G["norm_ffn_w"] = _norm_bwd(x2, norm_ffn_w, dh3, dx3, name="norm_ffn_bwd")
    G["w_mo"] = _mm(om, dx2, trans_a=True, name="g_w_mo")
    dom = _mm(dx2, W_mo.T, name="d_om")
    dqm, dkm, dvm = _mem_attn_bwd(qm, km, vm, dom, name="mem_attn_bwd")
    G["w_mq"] = _mm(h2, dqm, trans_a=True, name="g_w_mq")
    G["w_mk"] = _mm(mn, dkm, trans_a=True, name="g_w_mk")
    G["w_mv"] = _mm(mn, dvm, trans_a=True, name="g_w_mv")
    dh2 = _mm(dqm, W_mq.T, name="d_h2")
    dmn = _mm(dkm, W_mk.T, name="d_mn_k")
    dmn = _mm(dvm, W_mv.T, add=dmn, name="d_mn_v")
    _, G["norm_memkv_w"] = _norm_bwd(mm, norm_memkv_w, dmn, None, name="norm_memkv_bwd")
    dx1, G["norm_mem_w"] = _norm_bwd(x1, norm_mem_w, dh2, dx2, name="norm_mem_bwd")
    G["w_out"] = _mm(ymix, dx1, trans_a=True, name="g_w_out")
    dymix = _mm(dx1, W_out.T, name="d_ymix")
    do_sb, G["sb_norm_w"] = _head_norm_bwd(o_sb, sb_norm_w, dymix, name="sb_norm_bwd")
    dq, dk, dv = _sb_bwd(qkv, o_sb, do_sb, name="sb_bwd")
    dqkv = jnp.concatenate([dq, dk, dv], axis=1)
    dz, dxbc, ddtr, dlanes, dconv = _ssd_bwd(z, xbc, dtr, states, dymix, cw_ssd, conv_ssd_b, lanes, name="ssd_bwd")
    G["dt_bias"], G["a_log"], G["d_skip"] = [_group_sum(dlanes[i:i + 1]) for i in range(3)]
    G["ssd_norm_w"] = dlanes[3:4]
    G["conv_ssd_w"], G["conv_ssd_b"] = dconv[0:4], dconv[4:5]
    g_wdt = _mm(h1, ddtr, trans_a=True, name="g_w_dt").reshape(D, 16, 64).sum(axis=2)
    G["w_in"] = jnp.concatenate([_mm(h1, dz, trans_a=True, name="g_w_z"),
                                 _mm(h1, dxbc, trans_a=True, name="g_w_xbc"), g_wdt,
                                 _mm(h1, dqkv, trans_a=True, name="g_w_qkv")], axis=1)
    dh1 = _mm(dz, W_z.T, name="d_h1_z")
    dh1 = _mm(dxbc, W_xbc.T, add=dh1, name="d_h1_xbc")
    dh1 = _mm(ddtr, W_dtr.T, add=dh1, name="d_h1_dt")
    dh1 = _mm(dqkv, W_qkv.T, add=dh1, name="d_h1_qkv")
    dx, G["norm_mix_w"] = _norm_bwd(xs, norm_mix_w, dh1, dx1, name="norm_mix_bwd")
    G["norm_final_w"] = g_nfinal.reshape(D)

    send = jnp.concatenate([
        _scatter_cols(G["w_in"], 706), G["w_out"].reshape(NDEV, 256, D),
        G["w_mq"].reshape(NDEV, 128, D), G["w_mk"].reshape(NDEV, 128, D), G["w_mv"].reshape(NDEV, 128, D),
        G["w_mo"].reshape(NDEV, 128, D), _scatter_cols(G["w_up"], 704), G["w_down"].reshape(NDEV, 352, D),
        _pad_rows(G["conv_ssd_w"].reshape(4, NDEV, 192).transpose(1, 0, 2).reshape(NDEV, 768), 1),
        _pad_rows(G["conv_ffn_w"].reshape(3, NDEV, 704).transpose(1, 0, 2).reshape(NDEV, 2112), 3),
        jnp.zeros((NDEV, BIG_ROWS - CONV_ROW0 - 4, D), F32)], axis=1)
    recv = _exchange(send, name="exchange_grads")
    outs_big = _adamw(recv, w_big, _pack([M[n][0] for n, _ in BIG], BIG_ROWS),
                      _pack([V[n][0] for n, _ in BIG], BIG_ROWS), name="adamw_sharded")
    small_g = _pack([G[n] for n, _ in SMALL] + [loss_part], SMALL_ROWS)
    (parts_small,) = _all_gather([small_g], name="gather_small_grads")
    outs_small = _adamw(parts_small, _pack([P[n] for n, _ in SMALL], SMALL_ROWS),
                        _pack([M[n] for n, _ in SMALL], SMALL_ROWS),
                        _pack([V[n] for n, _ in SMALL], SMALL_ROWS), name="adamw_replicated")

    res = {}
    for kind, ob, osm in zip(("grad", "delta", "new_m", "new_v"), outs_big, outs_small):
        for (n, shp), val in zip(BIG, _unpack(ob, big_shapes)):
            res[kind, n] = val.reshape((1,) + shp)
        for (n, shp), val in zip(SMALL, _unpack(osm, small_shapes)):
            res[kind, n] = val
    loss = outs_small[0][LOSS_ROW, 0]
    out = [loss, dx.reshape(1, -1, D)]
    for kind in ("grad", "delta", "new_m", "new_v"):
        out += [res[kind, n] for n in ORDER]
    return tuple(out)
```

```python
import functools
import math

import jax
import jax.numpy as jnp
from jax import lax
from jax.experimental import pallas as pl
from jax.experimental.pallas import tpu as pltpu

F32 = jnp.float32
BF16 = jnp.bfloat16

D = 1024
NDEV = 8
EPS = 1e-6
SSD_CHUNK = 128
HALO = 8
VMEM_LIMIT = 56 * 2**20

ADAM_LR, ADAM_B1, ADAM_B2, ADAM_EPS, ADAM_WD, ADAM_STEP = 0.001, 0.9, 0.999, 1e-08, 0.01, 10


def _cp(*sem):
    return pltpu.CompilerParams(dimension_semantics=sem, vmem_limit_bytes=VMEM_LIMIT)


def _tile(n, cap, mult):
    if n <= cap:
        return n
    for d in range(cap - cap % mult, 0, -mult):
        if n % d == 0:
            return d
    raise ValueError(f"no tile for {n}")


def _sigmoid(x):
    return 1.0 / (1.0 + jnp.exp(-x))


def _silu(x):
    return x * _sigmoid(x)


def _softplus(x):
    return jnp.maximum(x, 0.0) + jnp.log1p(jnp.exp(-jnp.abs(x)))


def _terms(x, n):
    out = []
    r = x.astype(F32)
    for i in range(n):
        h = r.astype(BF16)
        out.append(h)
        if i + 1 < n:
            r = r - h.astype(F32)
    return out


_DIMS = {"nn": ((1,), (0,)), "nt": ((1,), (1,)), "tn": ((0,), (0,))}


def _dot_raw(form, a, b, ta, tb):
    acc = None
    for ai in _terms(a, ta):
        for bi in _terms(b, tb):
            d = lax.dot_general(ai, bi, (_DIMS[form], ((), ())), preferred_element_type=F32)
            acc = d if acc is None else acc + d
    return acc


@functools.lru_cache(maxsize=None)
def _dot_fn(form, ta, tb):
    @jax.custom_vjp
    def f(a, b):
        return _dot_raw(form, a, b, ta, tb)

    def fwd(a, b):
        return f(a, b), (a, b)

    def bwd(res, ct):
        a, b = res
        if form == "nn":
            return _dot_fn("nt", ta, tb)(ct, b), _dot_fn("tn", ta, tb)(a, ct)
        if form == "nt":
            return _dot_fn("nn", ta, tb)(ct, b), _dot_fn("tn", tb, ta)(ct, a)
        return _dot_fn("nt", tb, ta)(b, ct), _dot_fn("nn", ta, tb)(a, ct)

    f.defvjp(fwd, bwd)
    return f


def _dot(form, a, b, ta=1, tb=1):
    return _dot_fn(form, ta, tb)(a, b)


@functools.lru_cache(maxsize=None)
def _take_fn(axis, idx):
    @jax.custom_vjp
    def f(x):
        return x[:, idx:idx + 1] if axis == 1 else x[idx:idx + 1, :]

    def fwd(x):
        return f(x), x.shape

    def bwd(shape, ct):
        io = lax.broadcasted_iota(jnp.int32, shape, axis)
        return (jnp.where(io == idx, jnp.broadcast_to(ct, shape), 0.0),)

    f.defvjp(fwd, bwd)
    return f


@functools.lru_cache(maxsize=None)
def _split_fn(width, n):
    @jax.custom_vjp
    def f(x):
        return tuple(x[:, i * width:(i + 1) * width] for i in range(n))

    def fwd(x):
        return f(x), None

    def bwd(_, cts):
        return (jnp.concatenate(list(cts), axis=1),)

    f.defvjp(fwd, bwd)
    return f


def _split(x, width):
    return _split_fn(width, x.shape[1] // width)(x)


def _iota(shape, axis):
    return lax.broadcasted_iota(jnp.int32, shape, axis)


def _mm(a, b, *, name, add=None, trans_a=False, out_dtype=F32):
    if trans_a:
        kt, m = a.shape
    else:
        m, kt = a.shape
    kt2, n = b.shape
    assert kt == kt2, (a.shape, b.shape)
    tm = _tile(m, 512, 128 if trans_a else 8)
    tn = _tile(n, 1536, 128)
    tk = _tile(kt, 1024 if trans_a else 1536, 128)
    nk = kt // tk

    def body(*refs):
        if add is None:
            a_ref, b_ref, o_ref, acc = refs
        else:
            a_ref, b_ref, add_ref, o_ref, acc = refs
        k = pl.program_id(2)

        @pl.when(k == 0)
        def _():
            acc[...] = jnp.zeros_like(acc)

        av = a_ref[...].astype(BF16)
        bv = b_ref[...].astype(BF16)
        dims = _DIMS["tn" if trans_a else "nn"]
        acc[...] += lax.dot_general(av, bv, (dims, ((), ())), preferred_element_type=F32)

        @pl.when(k == nk - 1)
        def _():
            r = acc[...]
            if add is not None:
                r = r + add_ref[...]
            o_ref[...] = r.astype(out_dtype)

    a_spec = (pl.BlockSpec((tk, tm), lambda i, j, k: (k, i)) if trans_a
              else pl.BlockSpec((tm, tk), lambda i, j, k: (i, k)))
    in_specs = [a_spec, pl.BlockSpec((tk, tn), lambda i, j, k: (k, j))]
    args = [a, b]
    if add is not None:
        in_specs.append(pl.BlockSpec((tm, tn), lambda i, j, k: (i, j)))
        args.append(add)
    return pl.pallas_call(
        body, name=name, grid=(m // tm, n // tn, nk),
        in_specs=in_specs, out_specs=pl.BlockSpec((tm, tn), lambda i, j, k: (i, j)),
        out_shape=jax.ShapeDtypeStruct((m, n), out_dtype),
        scratch_shapes=[pltpu.VMEM((tm, tn), F32)],
        compiler_params=_cp("parallel", "parallel", "arbitrary"),
    )(*args)


def _rstd(x):
    return lax.rsqrt(jnp.mean(x * x, axis=-1, keepdims=True) + EPS)


def _norm_fwd(x, w, *, name):
    s = x.shape[0]
    tm = _tile(s, 512, 8)

    def body(x_ref, w_ref, o_ref):
        xv = x_ref[...]
        o_ref[...] = (xv * _rstd(xv) * w_ref[...]).astype(BF16)

    return pl.pallas_call(
        body, name=name, grid=(s // tm,),
        in_specs=[pl.BlockSpec((tm, D), lambda i: (i, 0)), pl.BlockSpec((1, D), lambda i: (0, 0))],
        out_specs=pl.BlockSpec((tm, D), lambda i: (i, 0)),
        out_shape=jax.ShapeDtypeStruct((s, D), BF16), compiler_params=_cp("parallel"),
    )(x, w)


def _norm_bwd_math(xv, wv, dy):
    r = _rstd(xv)
    xh = xv * r
    dxh = dy * wv
    dx = r * (dxh - xh * jnp.mean(dxh * xh, axis=-1, keepdims=True))
    dw = jnp.sum(dy * xh, axis=0, keepdims=True)
    return dx, dw


def _norm_bwd(x, w, dy, add, *, name):
    s = x.shape[0]
    tm = _tile(s, 256, 8)

    def body(*refs):
        if add is None:
            x_ref, w_ref, dy_ref, dx_ref, dw_ref = refs
        else:
            x_ref, w_ref, dy_ref, add_ref, dx_ref, dw_ref = refs

        @pl.when(pl.program_id(0) == 0)
        def _():
            dw_ref[...] = jnp.zeros_like(dw_ref)

        dx, dw = _norm_bwd_math(x_ref[...], w_ref[...], dy_ref[...])
        if add is not None:
            dx = dx + add_ref[...]
        dx_ref[...] = dx
        dw_ref[...] += dw

    row = pl.BlockSpec((tm, D), lambda i: (i, 0))
    vec = pl.BlockSpec((1, D), lambda i: (0, 0))
    in_specs = [row, vec, row] + ([row] if add is not None else [])
    args = [x, w, dy] + ([add] if add is not None else [])
    return pl.pallas_call(
        body, name=name, grid=(s // tm,), in_specs=in_specs, out_specs=[row, vec],
        out_shape=[jax.ShapeDtypeStruct((s, D), F32), jax.ShapeDtypeStruct((1, D), F32)],
        compiler_params=_cp("arbitrary"),
    )(*args)


def _final(x3, w, target, *, name):
    s = x3.shape[0]
    tm = _tile(s, 256, 8)

    def body(x_ref, w_ref, t_ref, dx_ref, dw_ref, loss_ref):
        @pl.when(pl.program_id(0) == 0)
        def _():
            dw_ref[...] = jnp.zeros_like(dw_ref)
            loss_ref[...] = jnp.zeros_like(loss_ref)

        xv = x_ref[...]
        wv = w_ref[...]
        y = xv * _rstd(xv) * wv
        err = y - t_ref[...]
        loss_ref[...] += 0.5 * jnp.sum(jnp.mean(err * err, axis=-1, keepdims=True))
        dx, dw = _norm_bwd_math(xv, wv, err * (1.0 / D))
        dx_ref[...] = dx
        dw_ref[...] += dw

    row = pl.BlockSpec((tm, D), lambda i: (i, 0))
    vec = pl.BlockSpec((1, D), lambda i: (0, 0))
    return pl.pallas_call(
        body, name=name, grid=(s // tm,), in_specs=[row, vec, row], out_specs=[row, vec, vec],
        out_shape=[jax.ShapeDtypeStruct((s, D), F32), jax.ShapeDtypeStruct((1, D), F32),
                   jax.ShapeDtypeStruct((1, D), F32)],
        compiler_params=_cp("arbitrary"),
    )(x3, w, target)


def _head_norm_math(o, w):
    lane = _iota((128, 128), 0) // 64
    bd = (lane == _iota((128, 128), 1) // 64).astype(F32)
    outs = []
    for op in _split(o, 128):
        ms = _dot("nn", op * op, bd, 2, 1) * (1.0 / 64)
        outs.append(op * lax.rsqrt(ms + EPS))
    return jnp.concatenate(outs, axis=1) * w


def _head_norm_fwd(o, w, *, name):
    s = o.shape[0]
    tm = _tile(s, 256, 8)

    def body(o_ref, w_ref, y_ref):
        y_ref[...] = _head_norm_math(o_ref[...], w_ref[...]).astype(BF16)

    row = pl.BlockSpec((tm, D), lambda i: (i, 0))
    vec = pl.BlockSpec((1, D), lambda i: (0, 0))
    return pl.pallas_call(
        body, name=name, grid=(s // tm,), in_specs=[row, vec], out_specs=row,
        out_shape=jax.ShapeDtypeStruct((s, D), BF16), compiler_params=_cp("parallel"),
    )(o, w)


def _head_norm_bwd(o, w, dymix, *, name):
    s = o.shape[0]
    tm = _tile(s, 256, 8)

    def body(o_ref, w_ref, dy_ref, do_ref, dw_ref):
        @pl.when(pl.program_id(0) == 0)
        def _():
            dw_ref[...] = jnp.zeros_like(dw_ref)

        _, vjp = jax.vjp(_head_norm_math, o_ref[...], w_ref[...])
        do, dw = vjp(dy_ref[...])
        do_ref[...] = do
        dw_ref[...] += dw

    row = pl.BlockSpec((tm, D), lambda i: (i, 0))
    vec = pl.BlockSpec((1, D), lambda i: (0, 0))
    return pl.pallas_call(
        body, name=name, grid=(s // tm,),
        in_specs=[row, vec, pl.BlockSpec((tm, D), lambda i: (i, 1))], out_specs=[row, vec],
        out_shape=[jax.ShapeDtypeStruct((s, D), F32), jax.ShapeDtypeStruct((1, D), F32)],
        compiler_params=_cp("arbitrary"),
    )(o, w, dymix)


SB_BQ = 256
SB_BK = 256


def _sb_consts():
    r = _iota((SB_BK, SB_BK), 0)
    c = _iota((SB_BK, SB_BK), 1)
    u_excl = (r > c).astype(BF16)
    u_incl = (r >= c).astype(BF16)
    return u_excl, u_incl


SB_LANES = 256
SB_NCH = SB_LANES // 64


def _nt(a, b):
    return lax.dot_general(a, b, (_DIMS["nt"], ((), ())), preferred_element_type=F32)


def _tn(a, b):
    return lax.dot_general(a, b, (_DIMS["tn"], ((), ())), preferred_element_type=F32)


def _nn(a, b):
    return jnp.dot(a, b, preferred_element_type=F32)


def _sb_heads(ref):
    out = []
    for hp in range(SB_LANES // 128):
        v = ref[:, 128 * hp:128 * (hp + 1)]
        first = _iota(v.shape, 1) < 64
        out += [jnp.where(first, v, 0).astype(BF16), jnp.where(first, 0, v).astype(BF16)]
    return out


def _sb_logits(qh, kbs, mask):
    zs = [_nt(qh[c], kbs[c // 2]) for c in range(SB_NCH)]
    es = [jnp.exp(-jnp.abs(z)) for z in zs]
    ls = [-(jnp.maximum(z, 0.0) + jnp.log(1.0 + e)) for z, e in zip(zs, es)]
    if mask is not None:
        ls = [jnp.where(mask, l, 0.0) for l in ls]
    return zs, es, ls


def _sb_cumsum(xs, u):
    parts = [_terms(x, 2) for x in xs]
    return [_nn(hi, u) + _nn(lo, u) for hi, lo in parts]


def _sb_weights(zs, ls, eins, lrun, mask):
    a = [jnp.exp(z + l + ein + lr) for z, l, ein, lr in zip(zs, ls, eins, lrun)]
    if mask is not None:
        a = [jnp.where(mask, x, 0.0) for x in a]
    return [x.astype(BF16) for x in a]


def _sb_fwd(qkv, *, name):
    s = qkv.shape[0]
    nq = s // SB_BQ
    ng = D // SB_LANES
    assert SB_BQ == SB_BK

    def body(q_ref, k_ref, v_ref, o_ref):
        i = pl.program_id(1)
        u_excl, _ = _sb_consts()
        lane_a = _iota((SB_BQ, 128), 1) < 64
        qh = _sb_heads(q_ref)
        qh = [q * 0.125 for q in qh]
        diag = _iota((SB_BQ, SB_BK), 1) < _iota((SB_BQ, SB_BK), 0)

        def tile(j, carry, mask):
            accs, lrun = carry
            off = pl.multiple_of(j * SB_BK, SB_BK)
            kbs = [k_ref[pl.ds(off, SB_BK), 128 * hp:128 * (hp + 1)] for hp in range(SB_NCH // 2)]
            vbs = [v_ref[pl.ds(off, SB_BK), 128 * hp:128 * (hp + 1)] for hp in range(SB_NCH // 2)]
            zs, _, ls = _sb_logits(qh, kbs, mask)
            eins = _sb_cumsum(ls, u_excl)
            a16 = _sb_weights(zs, ls, eins, lrun, mask)
            outs = [_nn(a16[c], vbs[c // 2]) for c in range(SB_NCH)]
            accs = tuple(acc + jnp.where(lane_a, outs[2 * hp], outs[2 * hp + 1]) for hp, acc in enumerate(accs))
            lrun = tuple(lr + ein[:, 0:1] + l[:, 0:1] for lr, ein, l in zip(lrun, eins, ls))
            return accs, lrun

        carry = ((jnp.zeros((SB_BQ, 128), F32),) * (SB_NCH // 2), (jnp.zeros((SB_BQ, 1), F32),) * SB_NCH)
        carry = tile(i, carry, diag)
        accs, _ = lax.fori_loop(0, i, lambda jj, c: tile(i - 1 - jj, c, None), carry)
        o_ref[...] = jnp.concatenate(accs, axis=1)

    return pl.pallas_call(
        body, name=name, grid=(ng, nq),
        in_specs=[pl.BlockSpec((SB_BQ, SB_LANES), lambda g, i: (i, g)),
                  pl.BlockSpec((s, SB_LANES), lambda g, i: (0, ng + g)),
                  pl.BlockSpec((s, SB_LANES), lambda g, i: (0, 2 * ng + g))],
        out_specs=pl.BlockSpec((SB_BQ, SB_LANES), lambda g, i: (i, g)),
        out_shape=jax.ShapeDtypeStruct((s, D), F32),
        compiler_params=_cp("parallel", "arbitrary"),
    )(qkv, qkv, qkv)


def _sb_bwd(qkv, o, do, *, name):
    s = qkv.shape[0]
    nq = s // SB_BQ
    ng = D // SB_LANES
    nhp = SB_NCH // 2

    def body(q_ref, k_ref, v_ref, o_ref, do_ref, dq_ref, dk_hbm, dv_hbm, dk_acc, dv_acc, sems):
        g_idx = pl.program_id(0)
        i = pl.program_id(1)

        @pl.when(i == 0)
        def _():
            dk_acc[...] = jnp.zeros_like(dk_acc)
            dv_acc[...] = jnp.zeros_like(dv_acc)

        u_excl, u_incl = _sb_consts()
        lane_a = _iota((SB_BQ, 128), 1) < 64
        lane_k = _iota((SB_BK, 128), 1) < 64
        qh = [q * 0.125 for q in _sb_heads(q_ref)]
        qf = [q_ref[:, 128 * hp:128 * (hp + 1)] for hp in range(nhp)]
        doh = _sb_heads(do_ref)
        dof = [do_ref[:, 128 * hp:128 * (hp + 1)].astype(BF16) for hp in range(nhp)]
        delta = []
        for hp in range(nhp):
            prod = dof[hp].astype(F32) * o_ref[:, 128 * hp:128 * (hp + 1)]
            delta += [jnp.sum(jnp.where(lane_a, prod, 0.0), axis=1, keepdims=True),
                      jnp.sum(jnp.where(lane_a, 0.0, prod), axis=1, keepdims=True)]
        diag = _iota((SB_BQ, SB_BK), 1) < _iota((SB_BQ, SB_BK), 0)

        def tile(j, carry, mask):
            dqs, lrun, grun = carry
            off = pl.multiple_of(j * SB_BK, SB_BK)
            kbs = [k_ref[pl.ds(off, SB_BK), 128 * hp:128 * (hp + 1)] for hp in range(nhp)]
            vbs = [v_ref[pl.ds(off, SB_BK), 128 * hp:128 * (hp + 1)] for hp in range(nhp)]
            zs, es, ls = _sb_logits(qh, kbs, mask)
            das = [_nt(doh[c], vbs[c // 2]) for c in range(SB_NCH)]
            eins = _sb_cumsum(ls, u_excl)
            a16 = _sb_weights(zs, ls, eins, lrun, mask)
            gs = [a.astype(F32) * da for a, da in zip(a16, das)]
            rgs = _sb_cumsum(gs, u_incl)
            dz16 = []
            for c in range(SB_NCH):
                cs = delta[c] - grun[c] - rgs[c]
                inv = 1.0 / (1.0 + es[c])
                sig = jnp.where(zs[c] >= 0, inv, es[c] * inv)
                dz = gs[c] * (1.0 - sig) - cs * sig
                if mask is not None:
                    dz = jnp.where(mask, dz, 0.0)
                dz16.append(dz.astype(BF16))
            dq_t = [_nn(dz16[c], kbs[c // 2]) for c in range(SB_NCH)]
            dk_t = [_tn(dz16[c], qf[c // 2]) for c in range(SB_NCH)]
            dv_t = [_tn(a16[c], dof[c // 2]) for c in range(SB_NCH)]
            for hp in range(nhp):
                cols = slice(128 * hp, 128 * (hp + 1))
                dk_acc[pl.ds(off, SB_BK), cols] += 0.125 * jnp.where(lane_k, dk_t[2 * hp], dk_t[2 * hp + 1])
                dv_acc[pl.ds(off, SB_BK), cols] += jnp.where(lane_k, dv_t[2 * hp], dv_t[2 * hp + 1])
            dqs = tuple(dq + jnp.where(lane_a, dq_t[2 * hp], dq_t[2 * hp + 1]) for hp, dq in enumerate(dqs))
            lrun = tuple(lr + ein[:, 0:1] + l[:, 0:1] for lr, ein, l in zip(lrun, eins, ls))
            grun = tuple(gr + rg[:, 0:1] for gr, rg in zip(grun, rgs))
            return dqs, lrun, grun

        zero = (jnp.zeros((SB_BQ, 1), F32),) * SB_NCH
        carry = tile(i, ((jnp.zeros((SB_BQ, 128), F32),) * nhp, zero, zero), diag)
        dqs, _, _ = lax.fori_loop(0, i, lambda jj, c: tile(i - 1 - jj, c, None), carry)
        dq_ref[...] = 0.125 * jnp.concatenate(dqs, axis=1)

        @pl.when(i == nq - 1)
        def _():
            cols = pl.ds(pl.multiple_of(g_idx * SB_LANES, SB_LANES), SB_LANES)
            ck = pltpu.make_async_copy(dk_acc, dk_hbm.at[:, cols], sems.at[0])
            cv = pltpu.make_async_copy(dv_acc, dv_hbm.at[:, cols], sems.at[1])
            ck.start()
            cv.start()
            ck.wait()
            cv.wait()

    qblk = pl.BlockSpec((SB_BQ, SB_LANES), lambda g, i: (i, g))
    hbm = pl.BlockSpec(memory_space=pl.ANY)
    return pl.pallas_call(
        body, name=name, grid=(ng, nq),
        in_specs=[qblk, pl.BlockSpec((s, SB_LANES), lambda g, i: (0, ng + g)),
                  pl.BlockSpec((s, SB_LANES), lambda g, i: (0, 2 * ng + g)), qblk, qblk],
        out_specs=[qblk, hbm, hbm],
        out_shape=[jax.ShapeDtypeStruct((s, D), F32)] * 3,
        scratch_shapes=[pltpu.VMEM((s, SB_LANES), F32), pltpu.VMEM((s, SB_LANES), F32),
                        pltpu.SemaphoreType.DMA((2,))],
        compiler_params=_cp("arbitrary", "arbitrary"),
    )(qkv, qkv, qkv, o, do)


def _ssd_core(z, xpre, dtr, state, dtb, alog, dsk, nw):
    L = SSD_CHUNK
    xa = _silu(xpre)
    pieces = _split(xa, 128)
    xs = jnp.concatenate(pieces[:8], axis=1)
    bm, cm = pieces[8:10], pieces[10:12]
    dt = _softplus(dtr + dtb)
    a = dt * (-jnp.exp(alog))
    tri = (_iota((L, L), 0) >= _iota((L, L), 1)).astype(F32)
    a_cs = _dot("nn", tri, a, 1, 3)
    xc = xs * dt
    tril = _iota((L, L), 0) >= _iota((L, L), 1)
    lane_a = _iota((L, 128), 1) < 64
    acs_p = _split(a_cs, 128)
    xc_p = _split(xc, 128)
    ys, new_states = [], []
    for g in range(2):
        cb = _dot("nt", cm[g], bm[g])
        for pp in range(4):
            pair = 4 * g + pp
            acs = acs_p[pair]
            acs_t = acs.T
            xcp = xc_p[pair]
            st = state[pair]
            heads = []
            for hh in range(2):
                col = _take_fn(1, 64 * hh)(acs)
                row = _take_fn(0, 64 * hh)(acs_t)
                seg = col - row
                lm = jnp.where(tril, jnp.exp(jnp.where(tril, seg, 0.0)), 0.0)
                heads.append(_dot("nn", cb * lm, xcp))
            ydiag = jnp.where(lane_a, heads[0], heads[1])
            last = _take_fn(0, L - 1)(acs)
            snew = _dot("tn", xcp * jnp.exp(last - acs), bm[g])
            new_states.append(st * jnp.exp(_take_fn(1, L - 1)(acs_t)) + snew)
            yoff = _dot("nt", cm[g], st) * jnp.exp(acs)
            ys.append(ydiag + yoff)
    y = jnp.concatenate(ys, axis=1) + xs * dsk
    yg = y * _silu(z)
    outs = []
    for v in _split(yg, 512):
        outs.append(v * lax.rsqrt(jnp.mean(v * v, axis=-1, keepdims=True) + EPS))
    return jnp.concatenate(outs, axis=1) * nw, tuple(new_states)


XBC = 1536


def _ssd_conv(ext_ref, cw, cb):
    acc = cb
    for k in range(4):
        acc = acc + cw[k:k + 1, :] * ext_ref[pl.ds(HALO - 3 + k, SSD_CHUNK), :]
    return acc


def _ssd_fwd(z, xbc, dtr, cw, cb, lanes, *, name):
    s = z.shape[0]
    L = SSD_CHUNK
    nc = s // L

    def body(z_ref, x_ref, h_ref, dtr_ref, cw_ref, cb_ref, ln_ref, y_ref, st_ref, state, ext):
        c = pl.program_id(0)

        @pl.when(c == 0)
        def _():
            state[...] = jnp.zeros_like(state)

        ext[0:HALO, :] = jnp.where(c == 0, 0.0, h_ref[...])
        ext[HALO:, :] = x_ref[...]
        xpre = _ssd_conv(ext, cw_ref[...], cb_ref[...])
        st_ref[0] = state[...]
        st_in = tuple(state[p] for p in range(8))
        yn, st_out = _ssd_core(z_ref[...], xpre, dtr_ref[...], st_in,
                               ln_ref[0:1, :], ln_ref[1:2, :], ln_ref[2:3, :], ln_ref[3:4, :])
        y_ref[...] = yn.astype(BF16)
        for p in range(8):
            state[p] = st_out[p]

    return pl.pallas_call(
        body, name=name, grid=(nc,),
        in_specs=[pl.BlockSpec((L, D), lambda c: (c, 0)),
                  pl.BlockSpec((L, XBC), lambda c: (c, 0)),
                  pl.BlockSpec((HALO, XBC), lambda c: (jnp.maximum(c * (L // HALO) - 1, 0), 0)),
                  pl.BlockSpec((L, D), lambda c: (c, 0)),
                  pl.BlockSpec((4, XBC), lambda c: (0, 0)),
                  pl.BlockSpec((1, XBC), lambda c: (0, 0)),
                  pl.BlockSpec((8, D), lambda c: (0, 0))],
        out_specs=[pl.BlockSpec((L, D), lambda c: (c, 0)),
                   pl.BlockSpec((1, 8, 128, 128), lambda c: (c, 0, 0, 0))],
        out_shape=[jax.ShapeDtypeStruct((s, D), BF16), jax.ShapeDtypeStruct((nc, 8, 128, 128), F32)],
        scratch_shapes=[pltpu.VMEM((8, 128, 128), F32), pltpu.VMEM((L + HALO, XBC), F32)],
        compiler_params=_cp("arbitrary"),
    )(z, xbc, xbc, dtr, cw, cb, lanes)


def _ssd_bwd(z, xbc, dtr, states, dymix, cw, cb, lanes, *, name):
    s = z.shape[0]
    L = SSD_CHUNK
    nc = s // L

    def body(z_ref, x_ref, h_ref, dtr_ref, st_ref, dy_ref, cw_ref, cb_ref, ln_ref,
             dz_ref, dx_ref, ddt_ref, dln_ref, dcv_ref, dstate, ext, dext):
        i = pl.program_id(0)
        c = nc - 1 - i

        @pl.when(i == 0)
        def _():
            dstate[...] = jnp.zeros_like(dstate)
            dext[...] = jnp.zeros_like(dext)
            dln_ref[...] = jnp.zeros_like(dln_ref)
            dcv_ref[...] = jnp.zeros_like(dcv_ref)

        ext[0:HALO, :] = jnp.where(c == 0, 0.0, h_ref[...])
        ext[HALO:, :] = x_ref[...]
        cwv = cw_ref[...]
        xpre = _ssd_conv(ext, cwv, cb_ref[...])
        st_in = tuple(st_ref[0, p] for p in range(8))
        _, vjp = jax.vjp(_ssd_core, z_ref[...], xpre, dtr_ref[...], st_in,
                         ln_ref[0:1, :], ln_ref[1:2, :], ln_ref[2:3, :], ln_ref[3:4, :])
        dz, dxpre, ddtr, dst, d0, d1, d2, d3 = vjp((dy_ref[...], tuple(dstate[p] for p in range(8))))
        for p in range(8):
            dstate[p] = dst[p]
        dz_ref[...] = dz.astype(BF16)
        ddt_ref[...] = ddtr.astype(BF16)
        dln_ref[0:4, :] += jnp.concatenate([d0, d1, d2, d3], axis=0)
        dext[0:L, :] = dxpre
        dx = jnp.zeros((L, XBC), F32)
        rows = []
        for k in range(4):
            dx = dx + cwv[k:k + 1, :] * dext[pl.ds(3 - k, L), :]
            rows.append(jnp.sum(dxpre * ext[pl.ds(HALO - 3 + k, L), :], axis=0, keepdims=True))
        rows.append(jnp.sum(dxpre, axis=0, keepdims=True))
        dx_ref[...] = dx.astype(BF16)
        dcv_ref[0:5, :] += jnp.concatenate(rows, axis=0)
        dext[L:L + HALO, :] = dxpre[0:HALO, :]

    rev = lambda i: (nc - 1 - i, 0)
    return pl.pallas_call(
        body, name=name, grid=(nc,),
        in_specs=[pl.BlockSpec((L, D), rev),
                  pl.BlockSpec((L, XBC), rev),
                  pl.BlockSpec((HALO, XBC), lambda i: (jnp.maximum((nc - 1 - i) * (L // HALO) - 1, 0), 0)),
                  pl.BlockSpec((L, D), rev),
                  pl.BlockSpec((1, 8, 128, 128), lambda i: (nc - 1 - i, 0, 0, 0)),
                  pl.BlockSpec((L, D), rev),
                  pl.BlockSpec((4, XBC), lambda i: (0, 0)),
                  pl.BlockSpec((1, XBC), lambda i: (0, 0)),
                  pl.BlockSpec((8, D), lambda i: (0, 0))],
        out_specs=[pl.BlockSpec((L, D), rev), pl.BlockSpec((L, XBC), rev), pl.BlockSpec((L, D), rev),
                   pl.BlockSpec((8, D), lambda i: (0, 0)), pl.BlockSpec((8, XBC), lambda i: (0, 0))],
        out_shape=[jax.ShapeDtypeStruct((s, D), BF16), jax.ShapeDtypeStruct((s, XBC), BF16),
                   jax.ShapeDtypeStruct((s, D), BF16), jax.ShapeDtypeStruct((8, D), F32),
                   jax.ShapeDtypeStruct((8, XBC), F32)],
        scratch_shapes=[pltpu.VMEM((8, 128, 128), F32), pltpu.VMEM((L + HALO, XBC), F32),
                        pltpu.VMEM((L + HALO, XBC), F32)],
        compiler_params=_cp("arbitrary"),
    )(z, xbc, xbc, dtr, states, dymix, cw, cb, lanes)


def _mem_attn_math(q, k, v):
    outs = []
    for qh, kh, vh in zip(_split(q, 256), _split(k, 256), _split(v, 256)):
        sc = _dot("nt", qh, kh) * (1.0 / 16.0)
        e = jnp.exp(sc - lax.stop_gradient(jnp.max(sc, axis=-1, keepdims=True)))
        p = e / jnp.sum(e, axis=-1, keepdims=True)
        outs.append(_dot("nn", p, vh))
    return jnp.concatenate(outs, axis=1)


def _mem_attn_fwd(q, k, v, *, name):
    s, m = q.shape[0], k.shape[0]
    tm = _tile(s, 256, 8)

    def body(q_ref, k_ref, v_ref, o_ref):
        o_ref[...] = _mem_attn_math(q_ref[...].astype(F32), k_ref[...].astype(F32),
                                    v_ref[...].astype(F32)).astype(BF16)

    row = pl.BlockSpec((tm, D), lambda i: (i, 0))
    kv = pl.BlockSpec((m, D), lambda i: (0, 0))
    return pl.pallas_call(
        body, name=name, grid=(s // tm,), in_specs=[row, kv, kv], out_specs=row,
        out_shape=jax.ShapeDtypeStruct((s, D), BF16), compiler_params=_cp("parallel"),
    )(q, k, v)


def _mem_attn_bwd(q, k, v, do, *, name):
    s, m = q.shape[0], k.shape[0]
    tm = _tile(s, 256, 8)

    def body(q_ref, k_ref, v_ref, do_ref, dq_ref, dk_ref, dv_ref):
        @pl.when(pl.program_id(0) == 0)
        def _():
            dk_ref[...] = jnp.zeros_like(dk_ref)
            dv_ref[...] = jnp.zeros_like(dv_ref)

        _, vjp = jax.vjp(_mem_attn_math, q_ref[...].astype(F32), k_ref[...].astype(F32),
                         v_ref[...].astype(F32))
        dq, dk, dv = vjp(do_ref[...])
        dq_ref[...] = dq.astype(BF16)
        dk_ref[...] += dk
        dv_ref[...] += dv

    row = pl.BlockSpec((tm, D), lambda i: (i, 0))
    kv = pl.BlockSpec((m, D), lambda i: (0, 0))
    return pl.pallas_call(
        body, name=name, grid=(s // tm,), in_specs=[row, kv, kv, row], out_specs=[row, kv, kv],
        out_shape=[jax.ShapeDtypeStruct((s, D), BF16), jax.ShapeDtypeStruct((m, D), F32),
                   jax.ShapeDtypeStruct((m, D), F32)],
        compiler_params=_cp("arbitrary"),
    )(q, k, v, do)


DFF = 2816
FFN_TC = 1408
FFN_TM = 256


def _ffn_conv(ext_ref, cw, cb, tm):
    acc = cb
    for k in range(3):
        acc = acc + cw[k:k + 1, :] * ext_ref[pl.ds(HALO - 2 + k, tm), :]
    return acc


def _ffn_specs(s):
    tm, tc = FFN_TM, FFN_TC
    blk = pl.BlockSpec((tm, tc), lambda i, j: (i, j))
    halo = pl.BlockSpec((HALO, tc), lambda i, j: (jnp.maximum(i * (tm // HALO) - 1, 0), j))
    cw = pl.BlockSpec((3, tc), lambda i, j: (0, j))
    cb = pl.BlockSpec((1, tc), lambda i, j: (0, j))
    return tm, tc, blk, halo, cw, cb


def _glu_fwd(ug, uv, cwg, cwv, cbg, cbv, *, name):
    s = ug.shape[0]
    tm, tc, blk, halo, cw, cb = _ffn_specs(s)

    def body(g_ref, gh_ref, v_ref, vh_ref, cwg_ref, cwv_ref, cbg_ref, cbv_ref, f_ref, eg, ev):
        first = pl.program_id(0) == 0
        eg[0:HALO, :] = jnp.where(first, 0.0, gh_ref[...])
        eg[HALO:, :] = g_ref[...]
        ev[0:HALO, :] = jnp.where(first, 0.0, vh_ref[...])
        ev[HALO:, :] = v_ref[...]
        g = _ffn_conv(eg, cwg_ref[...], cbg_ref[...], tm)
        v = _ffn_conv(ev, cwv_ref[...], cbv_ref[...], tm)
        f_ref[...] = (_silu(g) * v).astype(BF16)

    return pl.pallas_call(
        body, name=name, grid=(s // tm, DFF // tc),
        in_specs=[blk, halo, blk, halo, cw, cw, cb, cb], out_specs=blk,
        out_shape=jax.ShapeDtypeStruct((s, DFF), BF16),
        scratch_shapes=[pltpu.VMEM((tm + HALO, tc), F32)] * 2,
        compiler_params=_cp("parallel", "parallel"),
    )(ug, ug, uv, uv, cwg, cwv, cbg, cbv)


def _glu_bwd(ug, uv, df, cwg, cwv, cbg, cbv, *, name):
    s = ug.shape[0]
    tm, tc, blk, halo, cw, cb = _ffn_specs(s)

    def body(g_ref, gh_ref, v_ref, vh_ref, df_ref, cwg_ref, cwv_ref, cbg_ref, cbv_ref, dg_ref, dv_ref, eg, ev):
        first = pl.program_id(0) == 0
        eg[0:HALO, :] = jnp.where(first, 0.0, gh_ref[...])
        eg[HALO:, :] = g_ref[...]
        ev[0:HALO, :] = jnp.where(first, 0.0, vh_ref[...])
        ev[HALO:, :] = v_ref[...]
        g = _ffn_conv(eg, cwg_ref[...], cbg_ref[...], tm)
        v = _ffn_conv(ev, cwv_ref[...], cbv_ref[...], tm)
        dfv = df_ref[...]
        sg = _sigmoid(g)
        dv_ref[...] = dfv * g * sg
        dg_ref[...] = dfv * v * sg * (1.0 + g * (1.0 - sg))

    return pl.pallas_call(
        body, name=name, grid=(s // tm, DFF // tc),
        in_specs=[blk, halo, blk, halo, blk, cw, cw, cb, cb], out_specs=[blk, blk],
        out_shape=[jax.ShapeDtypeStruct((s, DFF), F32)] * 2,
        scratch_shapes=[pltpu.VMEM((tm + HALO, tc), F32)] * 2,
        compiler_params=_cp("parallel", "parallel"),
    )(ug, ug, uv, uv, df, cwg, cwv, cbg, cbv)


def _ffn_conv_bwd(du, u, cwt, *, name):
    s = du.shape[0]
    tm, tc = FFN_TM, FFN_TC
    nb = s // tm

    def body(du_ref, duh_ref, u_ref, uh_ref, cw_ref, dx_ref, dc_ref, edu, eu):
        i = pl.program_id(1)

        @pl.when(i == 0)
        def _():
            dc_ref[...] = jnp.zeros_like(dc_ref)

        duv = du_ref[...]
        edu[0:tm, :] = duv
        edu[tm:, :] = jnp.where(i == nb - 1, 0.0, duh_ref[...])
        eu[0:HALO, :] = jnp.where(i == 0, 0.0, uh_ref[...])
        eu[HALO:, :] = u_ref[...]
        cwv = cw_ref[...]
        dx = jnp.zeros((tm, tc), F32)
        rows = []
        for k in range(3):
            dx = dx + cwv[k:k + 1, :] * edu[pl.ds(2 - k, tm), :]
            rows.append(jnp.sum(duv * eu[pl.ds(HALO - 2 + k, tm), :], axis=0, keepdims=True))
        rows.append(jnp.sum(duv, axis=0, keepdims=True))
        dx_ref[...] = dx.astype(BF16)
        dc_ref[0:4, :] += jnp.concatenate(rows, axis=0)

    blk = pl.BlockSpec((tm, tc), lambda j, i: (i, j))
    nxt = pl.BlockSpec((HALO, tc), lambda j, i: (jnp.minimum((i + 1) * (tm // HALO), s // HALO - 1), j))
    prv = pl.BlockSpec((HALO, tc), lambda j, i: (jnp.maximum(i * (tm // HALO) - 1, 0), j))
    return pl.pallas_call(
        body, name=name, grid=(DFF // tc, nb),
        in_specs=[blk, nxt, blk, prv, pl.BlockSpec((3, tc), lambda j, i: (0, j))],
        out_specs=[blk, pl.BlockSpec((8, tc), lambda j, i: (0, j))],
        out_shape=[jax.ShapeDtypeStruct((s, DFF), BF16), jax.ShapeDtypeStruct((8, DFF), F32)],
        scratch_shapes=[pltpu.VMEM((tm + HALO, tc), F32)] * 2,
        compiler_params=_cp("parallel", "arbitrary"),
    )(du, du, u, u, cwt)


MESH = pl.DeviceIdType.MESH


def _all_gather(arrs, *, name):
    n = len(arrs)

    def body(*refs):
        x_refs, out_refs = refs[:n], refs[n:2 * n]
        send_sems, recv_sems, local_sems = refs[2 * n:]
        x, y, c = lax.axis_index("x"), lax.axis_index("y"), lax.axis_index("c")
        me, sibling = (x, y, c), (x, y, 1 - c)
        chips = [(1 - x, y), (x, 1 - y), (1 - x, 1 - y)]

        def blk(a, dev):
            return out_refs[a].at[4 * dev[0] + 2 * dev[1] + dev[2]]

        def copy(a, k, block, to, src=None):
            return pltpu.make_async_remote_copy(
                src_ref=blk(a, block) if src is None else src, dst_ref=blk(a, block),
                send_sem=send_sems.at[7 * a + k], recv_sem=recv_sems.at[7 * a + k],
                device_id=to, device_id_type=MESH)

        started = []
        mine = []
        for a in range(n):
            cp = pltpu.make_async_copy(x_refs[a], blk(a, me), local_sems.at[a])
            cp.start()
            mine.append(cp)
            first = [copy(a, 0, me, sibling, src=x_refs[a])]
            first += [copy(a, 1 + j, me, (*chip, c), src=x_refs[a]) for j, chip in enumerate(chips)]
            for cp in first:
                cp.start()
            started += first
        for a in range(n):
            for j, chip in enumerate(chips):
                copy(a, 1 + j, (*chip, c), me).wait_recv()
                fwd = copy(a, 4 + j, (*chip, c), sibling)
                fwd.start()
                started.append(fwd)
        for a in range(n):
            copy(a, 0, sibling, me).wait_recv()
            for j, chip in enumerate(chips):
                copy(a, 4 + j, (*chip, 1 - c), me).wait_recv()
        for cp in started:
            cp.wait_send()
        for cp in mine:
            cp.wait()

    any_spec = pl.BlockSpec(memory_space=pl.ANY)
    return pl.pallas_call(
        body, name=name,
        in_specs=[any_spec] * n, out_specs=[any_spec] * n,
        out_shape=[jax.ShapeDtypeStruct((NDEV,) + a.shape, a.dtype) for a in arrs],
        scratch_shapes=[pltpu.SemaphoreType.DMA((7 * n,)), pltpu.SemaphoreType.DMA((7 * n,)),
                        pltpu.SemaphoreType.DMA((n,))],
    )(*arrs)


def _exchange(send, *, name):
    def body(send_ref, recv_ref, send_sems, recv_sems, local_sem):
        x, y, c = lax.axis_index("x"), lax.axis_index("y"), lax.axis_index("c")
        me = 4 * x + 2 * y + c
        mine = pltpu.make_async_copy(send_ref.at[me], recv_ref.at[me], local_sem)
        mine.start()
        copies = []
        for k in range(1, NDEV):
            px = 1 - x if k & 4 else x
            py = 1 - y if k & 2 else y
            pc = 1 - c if k & 1 else c
            cp = pltpu.make_async_remote_copy(
                src_ref=send_ref.at[4 * px + 2 * py + pc], dst_ref=recv_ref.at[me],
                send_sem=send_sems.at[k - 1], recv_sem=recv_sems.at[k - 1],
                device_id=(px, py, pc), device_id_type=MESH)
            cp.start()
            copies.append(cp)
        for cp in copies:
            cp.wait_recv()
        for cp in copies:
            cp.wait_send()
        mine.wait()

    any_spec = pl.BlockSpec(memory_space=pl.ANY)
    return pl.pallas_call(
        body, name=name, in_specs=[any_spec], out_specs=any_spec,
        out_shape=jax.ShapeDtypeStruct(send.shape, send.dtype),
        scratch_shapes=[pltpu.SemaphoreType.DMA((7,)), pltpu.SemaphoreType.DMA((7,)),
                        pltpu.SemaphoreType.DMA],
    )(send)


def _adamw(parts, w, m, v, *, name):
    r = w.shape[0]
    tm = _tile(r, 256, PACK_ALIGN)
    c1 = 1.0 - ADAM_B1 ** ADAM_STEP
    c2 = 1.0 - ADAM_B2 ** ADAM_STEP

    def body(p_ref, w_ref, m_ref, v_ref, g_ref, d_ref, nm_ref, nv_ref):
        g = p_ref[0].astype(F32)
        for i in range(1, NDEV):
            g = g + p_ref[i].astype(F32)
        nm = ADAM_B1 * m_ref[...] + (1.0 - ADAM_B1) * g
        nv = ADAM_B2 * v_ref[...] + (1.0 - ADAM_B2) * (g * g)
        d_ref[...] = -ADAM_LR * ((nm / c1) / (jnp.sqrt(nv / c2) + ADAM_EPS) + ADAM_WD * w_ref[...])
        g_ref[...] = g
        nm_ref[...] = nm
        nv_ref[...] = nv

    row = pl.BlockSpec((tm, D), lambda i: (i, 0))
    return pl.pallas_call(
        body, name=name, grid=(r // tm,),
        in_specs=[pl.BlockSpec((NDEV, tm, D), lambda i: (0, i, 0)), row, row, row],
        out_specs=[row] * 4, out_shape=[jax.ShapeDtypeStruct((r, D), F32)] * 4,
        compiler_params=_cp("parallel"),
    )(parts, w, m, v)


PACK_ALIGN = 16


def _part_rows(shape):
    n = -(-math.prod(shape) // D)
    return n + (-n) % PACK_ALIGN


def _rows(a):
    flat = a.reshape(-1)
    pad = _part_rows(a.shape) * D - flat.shape[0]
    if pad:
        flat = jnp.concatenate([flat, jnp.zeros((pad,), flat.dtype)])
    return flat.reshape(-1, D)


def _pack(parts, total_rows):
    rows = [_rows(p) for p in parts]
    used = sum(r.shape[0] for r in rows)
    if total_rows > used:
        rows.append(jnp.zeros((total_rows - used, D), rows[0].dtype))
    return jnp.concatenate(rows, axis=0)


def _unpack(buf, shapes):
    out, r0 = [], 0
    for shp in shapes:
        n = math.prod(shp)
        out.append(buf[r0:r0 + _part_rows(shp)].reshape(-1)[:n].reshape(shp))
        r0 += _part_rows(shp)
    return out


BIG = [("w_in", (D, 706)), ("w_out", (256, D)), ("w_mq", (128, D)), ("w_mk", (128, D)), ("w_mv", (128, D)),
       ("w_mo", (128, D)), ("w_up", (D, 704)), ("w_down", (352, D)), ("conv_ssd_w", (4, 192)),
       ("conv_ffn_w", (3, 704))]
BIG_ROW0 = {}
_r = 0
for _n, _shp in BIG:
    BIG_ROW0[_n] = _r
    _r += _part_rows(_shp)
BIG_ROWS = _r + (-_r) % 128
SMALL = [("norm_mix_w", (1, D)), ("conv_ssd_b", (1, 1536)), ("dt_bias", (1, 16)), ("a_log", (1, 16)),
         ("d_skip", (1, 16)), ("ssd_norm_w", (1, D)), ("sb_norm_w", (1, D)), ("norm_mem_w", (1, D)),
         ("norm_memkv_w", (1, D)), ("norm_ffn_w", (1, D)), ("conv_ffn_b", (1, 5632)), ("norm_final_w", (D,))]
LOSS_ROW = sum(_part_rows(_shp) for _, _shp in SMALL)
SMALL_ROWS = LOSS_ROW + PACK_ALIGN
ORDER = ["norm_mix_w", "w_in", "conv_ssd_w", "conv_ssd_b", "dt_bias", "a_log", "d_skip", "ssd_norm_w",
         "sb_norm_w", "w_out", "norm_mem_w", "norm_memkv_w", "w_mq", "w_mk", "w_mv", "w_mo", "norm_ffn_w",
         "w_up", "conv_ffn_w", "conv_ffn_b", "w_down", "norm_final_w"]


def _gather_cols(g, r0, nr, rows, cols):
    t = g[:, r0:r0 + nr].reshape(NDEV, rows, cols)
    return t.transpose(1, 0, 2).reshape(rows, NDEV * cols)


def _scatter_cols(full, cols):
    rows = full.shape[0]
    return full.reshape(rows, NDEV, cols).transpose(1, 0, 2).reshape(NDEV, -1, D)


def _pad_rows(a, nr):
    n = a.shape[1]
    return jnp.concatenate([a, jnp.zeros((NDEV, nr * D - n), a.dtype)], axis=1).reshape(NDEV, nr, D)


def _group_sum(lanes):
    return lanes.reshape(16, 64).sum(axis=1).reshape(1, 16)


def kernel(x, mem, norm_mix_w, w_in, conv_ssd_w, conv_ssd_b, dt_bias, a_log, d_skip, ssd_norm_w, sb_norm_w, w_out, norm_mem_w, norm_memkv_w, w_mq, w_mk, w_mv, w_mo, norm_ffn_w, w_up, conv_ffn_w, conv_ffn_b, w_down, norm_final_w, loss_target, m_norm_mix_w, m_w_in, m_conv_ssd_w, m_conv_ssd_b, m_dt_bias, m_a_log, m_d_skip, m_ssd_norm_w, m_sb_norm_w, m_w_out, m_norm_mem_w, m_norm_memkv_w, m_w_mq, m_w_mk, m_w_mv, m_w_mo, m_norm_ffn_w, m_w_up, m_conv_ffn_w, m_conv_ffn_b, m_w_down, m_norm_final_w, v_norm_mix_w, v_w_in, v_conv_ssd_w, v_conv_ssd_b, v_dt_bias, v_a_log, v_d_skip, v_ssd_norm_w, v_sb_norm_w, v_w_out, v_norm_mem_w, v_norm_memkv_w, v_w_mq, v_w_mk, v_w_mv, v_w_mo, v_norm_ffn_w, v_w_up, v_conv_ffn_w, v_conv_ffn_b, v_w_down, v_norm_final_w):
    P = dict(norm_mix_w=norm_mix_w, w_in=w_in, conv_ssd_w=conv_ssd_w, conv_ssd_b=conv_ssd_b, dt_bias=dt_bias, a_log=a_log, d_skip=d_skip, ssd_norm_w=ssd_norm_w, sb_norm_w=sb_norm_w, w_out=w_out, norm_mem_w=norm_mem_w, norm_memkv_w=norm_memkv_w, w_mq=w_mq, w_mk=w_mk, w_mv=w_mv, w_mo=w_mo, norm_ffn_w=norm_ffn_w, w_up=w_up, conv_ffn_w=conv_ffn_w, conv_ffn_b=conv_ffn_b, w_down=w_down, norm_final_w=norm_final_w)
    M = dict(norm_mix_w=m_norm_mix_w, w_in=m_w_in, conv_ssd_w=m_conv_ssd_w, conv_ssd_b=m_conv_ssd_b, dt_bias=m_dt_bias, a_log=m_a_log, d_skip=m_d_skip, ssd_norm_w=m_ssd_norm_w, sb_norm_w=m_sb_norm_w, w_out=m_w_out, norm_mem_w=m_norm_mem_w, norm_memkv_w=m_norm_memkv_w, w_mq=m_w_mq, w_mk=m_w_mk, w_mv=m_w_mv, w_mo=m_w_mo, norm_ffn_w=m_norm_ffn_w, w_up=m_w_up, conv_ffn_w=m_conv_ffn_w, conv_ffn_b=m_conv_ffn_b, w_down=m_w_down, norm_final_w=m_norm_final_w)
    V = dict(norm_mix_w=v_norm_mix_w, w_in=v_w_in, conv_ssd_w=v_conv_ssd_w, conv_ssd_b=v_conv_ssd_b, dt_bias=v_dt_bias, a_log=v_a_log, d_skip=v_d_skip, ssd_norm_w=v_ssd_norm_w, sb_norm_w=v_sb_norm_w, w_out=v_w_out, norm_mem_w=v_norm_mem_w, norm_memkv_w=v_norm_memkv_w, w_mq=v_w_mq, w_mk=v_w_mk, w_mv=v_w_mv, w_mo=v_w_mo, norm_ffn_w=v_norm_ffn_w, w_up=v_w_up, conv_ffn_w=v_conv_ffn_w, conv_ffn_b=v_conv_ffn_b, w_down=v_w_down, norm_final_w=v_norm_final_w)
    big_shapes = [shp for _, shp in BIG]
    small_shapes = [shp for _, shp in SMALL]

    w_big = _pack([P[n][0] for n, _ in BIG], BIG_ROWS)
    c0 = BIG_ROW0["conv_ssd_w"]
    g16, g32 = _all_gather([w_big.astype(BF16), w_big[c0:c0 + 2 * PACK_ALIGN]], name="gather_weights")
    R0 = BIG_ROW0
    W_in = _gather_cols(g16, R0["w_in"], 706, D, 706)
    W_out = g16[:, R0["w_out"]:R0["w_out"] + 256].reshape(2 * D, D)
    W_mq, W_mk, W_mv, W_mo = [g16[:, R0[n]:R0[n] + 128].reshape(D, D) for n in ("w_mq", "w_mk", "w_mv", "w_mo")]
    W_up = _gather_cols(g16, R0["w_up"], 704, D, 704)
    W_down = g16[:, R0["w_down"]:R0["w_down"] + 352].reshape(DFF, D)
    cw_ssd = g32[:, 0].reshape(NDEV, -1)[:, :768].reshape(NDEV, 4, 192).transpose(1, 0, 2).reshape(4, XBC)
    cw_ffn = (g32[:, PACK_ALIGN:PACK_ALIGN + 3].reshape(NDEV, -1)[:, :2112].reshape(NDEV, 3, 704)
              .transpose(1, 0, 2).reshape(3, 2 * DFF))
    W_z, W_xbc, W_dt, W_qkv = W_in[:, :D], W_in[:, D:D + XBC], W_in[:, D + XBC:D + XBC + 16], W_in[:, D + XBC + 16:]
    W_dtr = jnp.repeat(W_dt, 64, axis=1)
    W_upg, W_upv = W_up[:, :DFF], W_up[:, DFF:]
    cwg, cwv = cw_ffn[:, :DFF], cw_ffn[:, DFF:]
    cbg, cbv = conv_ffn_b[:, :DFF], conv_ffn_b[:, DFF:]
    rep = lambda p: jnp.repeat(p, 64, axis=1)
    lanes = jnp.concatenate([rep(dt_bias), rep(a_log), rep(d_skip), ssd_norm_w, jnp.zeros((4, D), F32)], axis=0)

    xs, tgt, mm = x[0], loss_target[0], mem[0]

    h1 = _norm_fwd(xs, norm_mix_w, name="norm_mix")
    z = _mm(h1, W_z, name="proj_z")
    xbc = _mm(h1, W_xbc, name="proj_xbc")
    dtr = _mm(h1, W_dtr, name="proj_dt")
    qkv = _mm(h1, W_qkv, name="proj_qkv", out_dtype=BF16)
    y_ssd, states = _ssd_fwd(z, xbc, dtr, cw_ssd, conv_ssd_b, lanes, name="ssd_fwd")
    o_sb = _sb_fwd(qkv, name="sb_fwd")
    y_sb = _head_norm_fwd(o_sb, sb_norm_w, name="sb_norm")
    ymix = jnp.concatenate([y_ssd, y_sb], axis=1)
    x1 = _mm(ymix, W_out, add=xs, name="proj_out")
    h2 = _norm_fwd(x1, norm_mem_w, name="norm_mem")
    mn = _norm_fwd(mm, norm_memkv_w, name="norm_memkv")
    qm = _mm(h2, W_mq, name="mem_q", out_dtype=BF16)
    km = _mm(mn, W_mk, name="mem_k", out_dtype=BF16)
    vm = _mm(mn, W_mv, name="mem_v", out_dtype=BF16)
    om = _mem_attn_fwd(qm, km, vm, name="mem_attn")
    x2 = _mm(om, W_mo, add=x1, name="mem_o")
    h3 = _norm_fwd(x2, norm_ffn_w, name="norm_ffn")
    ug = _mm(h3, W_upg, name="ffn_up_g")
    uv = _mm(h3, W_upv, name="ffn_up_v")
    f = _glu_fwd(ug, uv, cwg, cwv, cbg, cbv, name="ffn_glu")
    x3 = _mm(f, W_down, add=x2, name="ffn_down")
    dx3, g_nfinal, loss_part = _final(x3, norm_final_w.reshape(1, D), tgt, name="final_loss")

    G = {}
    G["w_down"] = _mm(f, dx3, trans_a=True, name="g_w_down")
    df = _mm(dx3, W_down.T, name="d_f")
    dug, duv = _glu_bwd(ug, uv, df, cwg, cwv, cbg, cbv, name="ffn_glu_bwd")
    dupg, dcg = _ffn_conv_bwd(dug, ug, cwg, name="ffn_conv_bwd_g")
    dupv, dcv = _ffn_conv_bwd(duv, uv, cwv, name="ffn_conv_bwd_v")
    G["w_up"] = jnp.concatenate([_mm(h3, dupg, trans_a=True, name="g_w_up_g"),
                                 _mm(h3, dupv, trans_a=True, name="g_w_up_v")], axis=1)
    G["conv_ffn_w"] = jnp.concatenate([dcg[0:3], dcv[0:3]], axis=1)
    G["conv_ffn_b"] = jnp.concatenate([dcg[3:4], dcv[3:4]], axis=1)
    dh3 = _mm(dupg, W_upg.T, name="d_h3_g")
    dh3 = _mm(dupv, W_upv.T, add=dh3, name="d_h3_v")
    dx2, G["norm_ffn_w"] = _norm_bwd(x2, norm_ffn_w, dh3, dx3, name="norm_ffn_bwd")
    G["w_mo"] = _mm(om, dx2, trans_a=True, name="g_w_mo")
    dom = _mm(dx2, W_mo.T, name="d_om")
    dqm, dkm, dvm = _mem_attn_bwd(qm, km, vm, dom, name="mem_attn_bwd")
    G["w_mq"] = _mm(h2, dqm, trans_a=True, name="g_w_mq")
    G["w_mk"] = _mm(mn, dkm, trans_a=True, name="g_w_mk")
    G["w_mv"] = _mm(mn, dvm, trans_a=True, name="g_w_mv")
    dh2 = _mm(dqm, W_mq.T, name="d_h2")
    dmn = _mm(dkm, W_mk.T, name="d_mn_k")
    dmn = _mm(dvm, W_mv.T, add=dmn, name="d_mn_v")
    _, G["norm_memkv_w"] = _norm_bwd(mm, norm_memkv_w, dmn, None, name="norm_memkv_bwd")
    dx1, G["norm_mem_w"] = _norm_bwd(x1, norm_mem_w, dh2, dx2, name="norm_mem_bwd")
    G["w_out"] = _mm(ymix, dx1, trans_a=True, name="g_w_out")
    dymix = _mm(dx1, W_out.T, name="d_ymix")
    do_sb, G["sb_norm_w"] = _head_norm_bwd(o_sb, sb_norm_w, dymix, name="sb_norm_bwd")
    dq, dk, dv = _sb_bwd(qkv, o_sb, do_sb, name="sb_bwd")
    dqkv = jnp.concatenate([dq, dk, dv], axis=1)
    dz, dxbc, ddtr, dlanes, dconv = _ssd_bwd(z, xbc, dtr, states, dymix, cw_ssd, conv_ssd_b, lanes, name="ssd_bwd")
    G["dt_bias"], G["a_log"], G["d_skip"] = [_group_sum(dlanes[i:i + 1]) for i in range(3)]
    G["ssd_norm_w"] = dlanes[3:4]
    G["conv_ssd_w"], G["conv_ssd_b"] = dconv[0:4], dconv[4:5]
    g_wdt = _mm(h1, ddtr, trans_a=True, name="g_w_dt").reshape(D, 16, 64).sum(axis=2)
    G["w_in"] = jnp.concatenate([_mm(h1, dz, trans_a=True, name="g_w_z"),
                                 _mm(h1, dxbc, trans_a=True, name="g_w_xbc"), g_wdt,
                                 _mm(h1, dqkv, trans_a=True, name="g_w_qkv")], axis=1)
    dh1 = _mm(dz, W_z.T, name="d_h1_z")
    dh1 = _mm(dxbc, W_xbc.T, add=dh1, name="d_h1_xbc")
    dh1 = _mm(ddtr, W_dtr.T, add=dh1, name="d_h1_dt")
    dh1 = _mm(dqkv, W_qkv.T, add=dh1, name="d_h1_qkv")
    dx, G["norm_mix_w"] = _norm_bwd(xs, norm_mix_w, dh1, dx1, name="norm_mix_bwd")
    G["norm_final_w"] = g_nfinal.reshape(D)

    slabs = [
        _scatter_cols(G["w_in"], 706), G["w_out"].reshape(NDEV, 256, D),
        G["w_mq"].reshape(NDEV, 128, D), G["w_mk"].reshape(NDEV, 128, D), G["w_mv"].reshape(NDEV, 128, D),
        G["w_mo"].reshape(NDEV, 128, D), _scatter_cols(G["w_up"], 704), G["w_down"].reshape(NDEV, 352, D),
        _pad_rows(G["conv_ssd_w"].reshape(4, NDEV, 192).transpose(1, 0, 2).reshape(NDEV, 768), 1),
        _pad_rows(G["conv_ffn_w"].reshape(3, NDEV, 704).transpose(1, 0, 2).reshape(NDEV, 2112), 3)]
    slabs = [jnp.pad(t, ((0, 0), (0, (-t.shape[1]) % PACK_ALIGN), (0, 0))) for t in slabs]
    used = sum(t.shape[1] for t in slabs)
    slabs.append(jnp.zeros((NDEV, BIG_ROWS - used, D), F32))
    send = jnp.concatenate(slabs, axis=1).astype(BF16)
    recv = _exchange(send, name="exchange_grads")
    outs_big = _adamw(recv, w_big, _pack([M[n][0] for n, _ in BIG], BIG_ROWS),
                      _pack([V[n][0] for n, _ in BIG], BIG_ROWS), name="adamw_sharded")
    small_g = _pack([G[n] for n, _ in SMALL] + [loss_part], SMALL_ROWS)
    (parts_small,) = _all_gather([small_g], name="gather_small_grads")
    outs_small = _adamw(parts_small, _pack([P[n] for n, _ in SMALL], SMALL_ROWS),
                        _pack([M[n] for n, _ in SMALL], SMALL_ROWS),
                        _pack([V[n] for n, _ in SMALL], SMALL_ROWS), name="adamw_replicated")

    res = {}
    for kind, ob, osm in zip(("grad", "delta", "new_m", "new_v"), outs_big, outs_small):
        for (n, shp), val in zip(BIG, _unpack(ob, big_shapes)):
            res[kind, n] = val.reshape((1,) + shp)
        for (n, shp), val in zip(SMALL, _unpack(osm, small_shapes)):
            res[kind, n] = val
    loss = outs_small[0][LOSS_ROW, 0]
    out = [loss, dx.reshape(1, -1, D)]
    for kind in ("grad", "delta", "new_m", "new_v"):
        out += [res[kind, n] for n in ORDER]
    return tuple(out)
```

```python
import functools
import math

import jax
import jax.numpy as jnp
from jax import lax
from jax.experimental import pallas as pl
from jax.experimental.pallas import tpu as pltpu

F32 = jnp.float32
BF16 = jnp.bfloat16

D = 1024
NDEV = 8
EPS = 1e-6
SSD_CHUNK = 128
HALO = 8
VMEM_LIMIT = 56 * 2**20

ADAM_LR, ADAM_B1, ADAM_B2, ADAM_EPS, ADAM_WD, ADAM_STEP = 0.001, 0.9, 0.999, 1e-08, 0.01, 10


def _cp(*sem):
    return pltpu.CompilerParams(dimension_semantics=sem, vmem_limit_bytes=VMEM_LIMIT)


def _tile(n, cap, mult):
    if n <= cap:
        return n
    for d in range(cap - cap % mult, 0, -mult):
        if n % d == 0:
            return d
    raise ValueError(f"no tile for {n}")


def _sigmoid(x):
    return 1.0 / (1.0 + jnp.exp(-x))


def _silu(x):
    return x * _sigmoid(x)


def _softplus(x):
    return jnp.maximum(x, 0.0) + jnp.log1p(jnp.exp(-jnp.abs(x)))


def _terms(x, n):
    out = []
    r = x.astype(F32)
    for i in range(n):
        h = r.astype(BF16)
        out.append(h)
        if i + 1 < n:
            r = r - h.astype(F32)
    return out


_DIMS = {"nn": ((1,), (0,)), "nt": ((1,), (1,)), "tn": ((0,), (0,))}


def _dot_raw(form, a, b, ta, tb):
    acc = None
    for ai in _terms(a, ta):
        for bi in _terms(b, tb):
            d = lax.dot_general(ai, bi, (_DIMS[form], ((), ())), preferred_element_type=F32)
            acc = d if acc is None else acc + d
    return acc


@functools.lru_cache(maxsize=None)
def _dot_fn(form, ta, tb):
    @jax.custom_vjp
    def f(a, b):
        return _dot_raw(form, a, b, ta, tb)

    def fwd(a, b):
        return f(a, b), (a, b)

    def bwd(res, ct):
        a, b = res
        if form == "nn":
            return _dot_fn("nt", ta, tb)(ct, b), _dot_fn("tn", ta, tb)(a, ct)
        if form == "nt":
            return _dot_fn("nn", ta, tb)(ct, b), _dot_fn("tn", tb, ta)(ct, a)
        return _dot_fn("nt", tb, ta)(b, ct), _dot_fn("nn", ta, tb)(a, ct)

    f.defvjp(fwd, bwd)
    return f


def _dot(form, a, b, ta=1, tb=1):
    return _dot_fn(form, ta, tb)(a, b)


@functools.lru_cache(maxsize=None)
def _take_fn(axis, idx):
    @jax.custom_vjp
    def f(x):
        return x[:, idx:idx + 1] if axis == 1 else x[idx:idx + 1, :]

    def fwd(x):
        return f(x), x.shape

    def bwd(shape, ct):
        io = lax.broadcasted_iota(jnp.int32, shape, axis)
        return (jnp.where(io == idx, jnp.broadcast_to(ct, shape), 0.0),)

    f.defvjp(fwd, bwd)
    return f


@functools.lru_cache(maxsize=None)
def _split_fn(width, n):
    @jax.custom_vjp
    def f(x):
        return tuple(x[:, i * width:(i + 1) * width] for i in range(n))

    def fwd(x):
        return f(x), None

    def bwd(_, cts):
        return (jnp.concatenate(list(cts), axis=1),)

    f.defvjp(fwd, bwd)
    return f


def _split(x, width):
    return _split_fn(width, x.shape[1] // width)(x)


def _iota(shape, axis):
    return lax.broadcasted_iota(jnp.int32, shape, axis)


def _mm(a, b, *, name, add=None, trans_a=False, out_dtype=F32):
    if trans_a:
        kt, m = a.shape
    else:
        m, kt = a.shape
    kt2, n = b.shape
    assert kt == kt2, (a.shape, b.shape)
    tm = _tile(m, 512, 128 if trans_a else 8)
    tn = _tile(n, 1536, 128)
    tk = _tile(kt, 1024 if trans_a else 1536, 128)
    nk = kt // tk

    def body(*refs):
        if add is None:
            a_ref, b_ref, o_ref, acc = refs
        else:
            a_ref, b_ref, add_ref, o_ref, acc = refs
        k = pl.program_id(2)

        @pl.when(k == 0)
        def _():
            acc[...] = jnp.zeros_like(acc)

        av = a_ref[...].astype(BF16)
        bv = b_ref[...].astype(BF16)
        dims = _DIMS["tn" if trans_a else "nn"]
        acc[...] += lax.dot_general(av, bv, (dims, ((), ())), preferred_element_type=F32)

        @pl.when(k == nk - 1)
        def _():
            r = acc[...]
            if add is not None:
                r = r + add_ref[...]
            o_ref[...] = r.astype(out_dtype)

    a_spec = (pl.BlockSpec((tk, tm), lambda i, j, k: (k, i)) if trans_a
              else pl.BlockSpec((tm, tk), lambda i, j, k: (i, k)))
    in_specs = [a_spec, pl.BlockSpec((tk, tn), lambda i, j, k: (k, j))]
    args = [a, b]
    if add is not None:
        in_specs.append(pl.BlockSpec((tm, tn), lambda i, j, k: (i, j)))
        args.append(add)
    return pl.pallas_call(
        body, name=name, grid=(m // tm, n // tn, nk),
        in_specs=in_specs, out_specs=pl.BlockSpec((tm, tn), lambda i, j, k: (i, j)),
        out_shape=jax.ShapeDtypeStruct((m, n), out_dtype),
        scratch_shapes=[pltpu.VMEM((tm, tn), F32)],
        compiler_params=_cp("parallel", "parallel", "arbitrary"),
    )(*args)


def _rstd(x):
    return lax.rsqrt(jnp.mean(x * x, axis=-1, keepdims=True) + EPS)


def _norm_fwd(x, w, *, name):
    s = x.shape[0]
    tm = _tile(s, 512, 8)

    def body(x_ref, w_ref, o_ref):
        xv = x_ref[...]
        o_ref[...] = (xv * _rstd(xv) * w_ref[...]).astype(BF16)

    return pl.pallas_call(
        body, name=name, grid=(s // tm,),
        in_specs=[pl.BlockSpec((tm, D), lambda i: (i, 0)), pl.BlockSpec((1, D), lambda i: (0, 0))],
        out_specs=pl.BlockSpec((tm, D), lambda i: (i, 0)),
        out_shape=jax.ShapeDtypeStruct((s, D), BF16), compiler_params=_cp("parallel"),
    )(x, w)


def _norm_bwd_math(xv, wv, dy):
    r = _rstd(xv)
    xh = xv * r
    dxh = dy * wv
    dx = r * (dxh - xh * jnp.mean(dxh * xh, axis=-1, keepdims=True))
    dw = jnp.sum(dy * xh, axis=0, keepdims=True)
    return dx, dw


def _norm_bwd(x, w, dy, add, *, name):
    s = x.shape[0]
    tm = _tile(s, 256, 8)

    def body(*refs):
        if add is None:
            x_ref, w_ref, dy_ref, dx_ref, dw_ref = refs
        else:
            x_ref, w_ref, dy_ref, add_ref, dx_ref, dw_ref = refs

        @pl.when(pl.program_id(0) == 0)
        def _():
            dw_ref[...] = jnp.zeros_like(dw_ref)

        dx, dw = _norm_bwd_math(x_ref[...], w_ref[...], dy_ref[...])
        if add is not None:
            dx = dx + add_ref[...]
        dx_ref[...] = dx
        dw_ref[...] += dw

    row = pl.BlockSpec((tm, D), lambda i: (i, 0))
    vec = pl.BlockSpec((1, D), lambda i: (0, 0))
    in_specs = [row, vec, row] + ([row] if add is not None else [])
    args = [x, w, dy] + ([add] if add is not None else [])
    return pl.pallas_call(
        body, name=name, grid=(s // tm,), in_specs=in_specs, out_specs=[row, vec],
        out_shape=[jax.ShapeDtypeStruct((s, D), F32), jax.ShapeDtypeStruct((1, D), F32)],
        compiler_params=_cp("arbitrary"),
    )(*args)


def _final(x3, w, target, *, name):
    s = x3.shape[0]
    tm = _tile(s, 256, 8)

    def body(x_ref, w_ref, t_ref, dx_ref, dw_ref, loss_ref):
        @pl.when(pl.program_id(0) == 0)
        def _():
            dw_ref[...] = jnp.zeros_like(dw_ref)
            loss_ref[...] = jnp.zeros_like(loss_ref)

        xv = x_ref[...]
        wv = w_ref[...]
        y = xv * _rstd(xv) * wv
        err = y - t_ref[...]
        loss_ref[...] += 0.5 * jnp.sum(jnp.mean(err * err, axis=-1, keepdims=True))
        dx, dw = _norm_bwd_math(xv, wv, err * (1.0 / D))
        dx_ref[...] = dx
        dw_ref[...] += dw

    row = pl.BlockSpec((tm, D), lambda i: (i, 0))
    vec = pl.BlockSpec((1, D), lambda i: (0, 0))
    return pl.pallas_call(
        body, name=name, grid=(s // tm,), in_specs=[row, vec, row], out_specs=[row, vec, vec],
        out_shape=[jax.ShapeDtypeStruct((s, D), F32), jax.ShapeDtypeStruct((1, D), F32),
                   jax.ShapeDtypeStruct((1, D), F32)],
        compiler_params=_cp("arbitrary"),
    )(x3, w, target)


def _head_norm_math(o, w):
    lane = _iota((128, 128), 0) // 64
    bd = (lane == _iota((128, 128), 1) // 64).astype(F32)
    outs = []
    for op in _split(o, 128):
        ms = _dot("nn", op * op, bd, 2, 1) * (1.0 / 64)
        outs.append(op * lax.rsqrt(ms + EPS))
    return jnp.concatenate(outs, axis=1) * w


def _head_norm_fwd(o, w, *, name):
    s = o.shape[0]
    tm = _tile(s, 256, 8)

    def body(o_ref, w_ref, y_ref):
        y_ref[...] = _head_norm_math(o_ref[...], w_ref[...]).astype(BF16)

    row = pl.BlockSpec((tm, D), lambda i: (i, 0))
    vec = pl.BlockSpec((1, D), lambda i: (0, 0))
    return pl.pallas_call(
        body, name=name, grid=(s // tm,), in_specs=[row, vec], out_specs=row,
        out_shape=jax.ShapeDtypeStruct((s, D), BF16), compiler_params=_cp("parallel"),
    )(o, w)


def _head_norm_bwd(o, w, dymix, *, name):
    s = o.shape[0]
    tm = _tile(s, 256, 8)

    def body(o_ref, w_ref, dy_ref, do_ref, dw_ref):
        @pl.when(pl.program_id(0) == 0)
        def _():
            dw_ref[...] = jnp.zeros_like(dw_ref)

        _, vjp = jax.vjp(_head_norm_math, o_ref[...], w_ref[...])
        do, dw = vjp(dy_ref[...])
        do_ref[...] = do
        dw_ref[...] += dw

    row = pl.BlockSpec((tm, D), lambda i: (i, 0))
    vec = pl.BlockSpec((1, D), lambda i: (0, 0))
    return pl.pallas_call(
        body, name=name, grid=(s // tm,),
        in_specs=[row, vec, pl.BlockSpec((tm, D), lambda i: (i, 1))], out_specs=[row, vec],
        out_shape=[jax.ShapeDtypeStruct((s, D), F32), jax.ShapeDtypeStruct((1, D), F32)],
        compiler_params=_cp("arbitrary"),
    )(o, w, dymix)


SB_BQ = 256
SB_BK = 256


def _sb_consts():
    r = _iota((SB_BK, SB_BK), 0)
    c = _iota((SB_BK, SB_BK), 1)
    u_excl = (r > c).astype(BF16)
    u_incl = (r >= c).astype(BF16)
    return u_excl, u_incl


SB_LANES = 256
SB_NCH = SB_LANES // 64


def _nt(a, b):
    return lax.dot_general(a, b, (_DIMS["nt"], ((), ())), preferred_element_type=F32)


def _tn(a, b):
    return lax.dot_general(a, b, (_DIMS["tn"], ((), ())), preferred_element_type=F32)


def _nn(a, b):
    return jnp.dot(a, b, preferred_element_type=F32)


def _sb_heads(ref):
    out = []
    for hp in range(SB_LANES // 128):
        v = ref[:, 128 * hp:128 * (hp + 1)]
        first = _iota(v.shape, 1) < 64
        out += [jnp.where(first, v, 0).astype(BF16), jnp.where(first, 0, v).astype(BF16)]
    return out


SB_STRIP = 32


def _neg_abs(x):
    bits = lax.bitcast_convert_type(x, jnp.uint32) | jnp.uint32(0x80000000)
    return lax.bitcast_convert_type(bits, F32)


def _sb_block(ref, j):
    off = pl.multiple_of(j * SB_BK, SB_BK)
    return [ref[pl.ds(off, SB_BK), 128 * hp:128 * (hp + 1)] for hp in range(SB_NCH // 2)]


def _sb_strips():
    return [(r, pl.ds(r, SB_STRIP)) for r in range(0, SB_BQ, SB_STRIP)]


def _sb_diag_mask(r):
    return _iota((SB_STRIP, SB_BK), 1) < _iota((SB_STRIP, SB_BK), 0) + r


def _sb_soft(z, mask):
    e = jnp.exp(_neg_abs(z))
    nl = jnp.maximum(z, 0.0) + jnp.log(1.0 + e)
    if mask is not None:
        nl = jnp.where(mask, nl, 0.0)
    return e, nl


def _sb_split_to(hl_ref, rows, x):
    hi, lo = _terms(x, 2)
    hl_ref[rows, 0:SB_BK] = hi
    hl_ref[rows, SB_BK:2 * SB_BK] = lo


def _sb_stage_soft(z_ref, nl_ref, diag):
    for r, rows in _sb_strips():
        _, nl = _sb_soft(z_ref[rows, :], _sb_diag_mask(r) if diag else None)
        nl_ref[rows, 0:SB_BK] = nl.astype(BF16)


def _sb_stage_weights(z_ref, c_ref, a_ref, nlrun, diag):
    for r, rows in _sb_strips():
        a = jnp.exp(z_ref[rows, :] - c_ref[rows, :] - nlrun[r:r + SB_STRIP, :])
        if diag:
            a = jnp.where(_sb_diag_mask(r), a, 0.0)
        a_ref[rows, :] = a.astype(BF16)


def _sb_fwd(qkv, *, name):
    s = qkv.shape[0]
    nq = s // SB_BQ
    ng = D // SB_LANES
    assert SB_BQ == SB_BK

    def body(q_ref, k_ref, v_ref, o_ref, zbuf, nlbuf, cbuf, abuf):
        i = pl.program_id(1)
        _, u_incl = _sb_consts()
        lane_a = _iota((SB_BQ, 128), 1) < 64
        qh = [q * 0.125 for q in _sb_heads(q_ref)]

        def logits_to(slot, j):
            kbs = _sb_block(k_ref, j)
            for c in range(SB_NCH):
                zbuf[slot, c] = _nt(qh[c], kbs[c // 2])

        def weights(slot, nlrun, diag):
            for c in range(SB_NCH):
                _sb_stage_soft(zbuf.at[slot, c], nlbuf.at[c], diag)
                cbuf[c] = _nn(nlbuf[c], u_incl)
            for c in range(SB_NCH):
                _sb_stage_weights(zbuf.at[slot, c], cbuf.at[c], abuf.at[slot, c], nlrun[c], diag)
            return tuple(nlrun[c] + cbuf[c, :, 0:1] for c in range(SB_NCH))

        def accumulate(accs, slot, j):
            vbs = _sb_block(v_ref, j)
            outs = [_nn(abuf[slot, c], vbs[c // 2]) for c in range(SB_NCH)]
            return tuple(acc + jnp.where(lane_a, outs[2 * hp], outs[2 * hp + 1]) for hp, acc in enumerate(accs))

        def step(slot, j, accs, nlrun):
            accs = accumulate(accs, 1 - slot, j + 1)
            logits_to(1 - slot, jnp.maximum(j - 1, 0))
            return accs, weights(slot, nlrun, False)

        def step2(m, carry):
            j = i - 1 - 2 * m
            return step(0, j - 1, *step(1, j, *carry))

        logits_to(0, i)
        nlrun = weights(0, (jnp.zeros((SB_BQ, 1), F32),) * SB_NCH, True)
        logits_to(1, jnp.maximum(i - 1, 0))
        accs = (jnp.zeros((SB_BQ, 128), F32),) * (SB_NCH // 2)
        accs, nlrun = lax.fori_loop(0, i // 2, step2, (accs, nlrun))
        accs = lax.cond(i % 2 == 1,
                        lambda: accumulate(step(1, 0, accs, nlrun)[0], 1, 0),
                        lambda: accumulate(accs, 0, 0))
        o_ref[...] = jnp.concatenate(accs, axis=1)

    return pl.pallas_call(
        body, name=name, grid=(ng, nq),
        in_specs=[pl.BlockSpec((SB_BQ, SB_LANES), lambda g, i: (i, g)),
                  pl.BlockSpec((s, SB_LANES), lambda g, i: (0, ng + g)),
                  pl.BlockSpec((s, SB_LANES), lambda g, i: (0, 2 * ng + g))],
        out_specs=pl.BlockSpec((SB_BQ, SB_LANES), lambda g, i: (i, g)),
        out_shape=jax.ShapeDtypeStruct((s, D), F32),
        scratch_shapes=[pltpu.VMEM((2, SB_NCH, SB_BQ, SB_BK), F32), pltpu.VMEM((SB_NCH, SB_BQ, SB_BK), BF16),
                        pltpu.VMEM((SB_NCH, SB_BQ, SB_BK), F32), pltpu.VMEM((2, SB_NCH, SB_BQ, SB_BK), BF16)],
        compiler_params=_cp("parallel", "arbitrary"),
    )(qkv, qkv, qkv)


def _sb_bwd(qkv, o, do, *, name):
    s = qkv.shape[0]
    nq = s // SB_BQ
    ng = D // SB_LANES
    nhp = SB_NCH // 2

    def body(q_ref, k_ref, v_ref, o_ref, do_ref, dq_ref, dk_hbm, dv_hbm, dk_acc, dv_acc, sems,
             zbuf, gbuf, hl, cbuf, abuf, dzbuf):
        g_idx = pl.program_id(0)
        i = pl.program_id(1)

        @pl.when(i == 0)
        def _():
            dk_acc[...] = jnp.zeros_like(dk_acc)
            dv_acc[...] = jnp.zeros_like(dv_acc)

        _, u_incl = _sb_consts()
        u2 = jnp.concatenate([u_incl, u_incl], axis=0)
        lane_a = _iota((SB_BQ, 128), 1) < 64
        lane_k = _iota((SB_BK, 128), 1) < 64
        qh = [q * 0.125 for q in _sb_heads(q_ref)]
        qf = [q_ref[:, 128 * hp:128 * (hp + 1)] for hp in range(nhp)]
        doh = _sb_heads(do_ref)
        dof = [do_ref[:, 128 * hp:128 * (hp + 1)].astype(BF16) for hp in range(nhp)]
        delta = []
        for hp in range(nhp):
            prod = dof[hp].astype(F32) * o_ref[:, 128 * hp:128 * (hp + 1)]
            delta += [jnp.sum(jnp.where(lane_a, prod, 0.0), axis=1, keepdims=True),
                      jnp.sum(jnp.where(lane_a, 0.0, prod), axis=1, keepdims=True)]

        def pre(slot, j):
            kbs = _sb_block(k_ref, j)
            vbs = _sb_block(v_ref, j)
            for c in range(SB_NCH):
                zbuf[slot, c] = _nt(qh[c], kbs[c // 2])
                gbuf[slot, c] = _nt(doh[c], vbs[c // 2])

        def stage_g(c, slot):
            for _, rows in _sb_strips():
                g = abuf[slot, c, rows, :].astype(F32) * gbuf[slot, c, rows, :]
                gbuf[slot, c, rows, :] = g
                _sb_split_to(hl.at[c], rows, g)

        def stage_dz(c, slot, grun, diag):
            for r, rows in _sb_strips():
                z = zbuf[slot, c, rows, :]
                g = gbuf[slot, c, rows, :]
                cs = (delta[c] - grun)[r:r + SB_STRIP, :] - cbuf[c, rows, :]
                sig = 1.0 / (1.0 + jnp.exp(-z))
                dz = g - (g + cs) * sig
                if diag:
                    dz = jnp.where(_sb_diag_mask(r), dz, 0.0)
                dzbuf[slot, c, rows, :] = dz.astype(BF16)

        def chain(slot, nlrun, grun, diag):
            for c in range(SB_NCH):
                _sb_stage_soft(zbuf.at[slot, c], hl.at[c], diag)
                cbuf[c] = _nn(hl[c, :, 0:SB_BK], u_incl)
            nl_tot = []
            for c in range(SB_NCH):
                _sb_stage_weights(zbuf.at[slot, c], cbuf.at[c], abuf.at[slot, c], nlrun[c], diag)
                nl_tot.append(cbuf[c, :, 0:1])
                stage_g(c, slot)
                cbuf[c] = _nn(hl[c], u2)
            g_tot = []
            for c in range(SB_NCH):
                stage_dz(c, slot, grun[c], diag)
                g_tot.append(cbuf[c, :, 0:1])
            return (tuple(a + b for a, b in zip(nlrun, nl_tot)), tuple(a + b for a, b in zip(grun, g_tot)))

        def post(slot, j, dqs):
            off = pl.multiple_of(j * SB_BK, SB_BK)
            kbs = _sb_block(k_ref, j)
            dq_t = [_nn(dzbuf[slot, c], kbs[c // 2]) for c in range(SB_NCH)]
            dk_t = [_tn(dzbuf[slot, c], qf[c // 2]) for c in range(SB_NCH)]
            dv_t = [_tn(abuf[slot, c], dof[c // 2]) for c in range(SB_NCH)]
            for hp in range(nhp):
                cols = slice(128 * hp, 128 * (hp + 1))
                dk_acc[pl.ds(off, SB_BK), cols] += 0.125 * jnp.where(lane_k, dk_t[2 * hp], dk_t[2 * hp + 1])
                dv_acc[pl.ds(off, SB_BK), cols] += jnp.where(lane_k, dv_t[2 * hp], dv_t[2 * hp + 1])
            return tuple(dq + jnp.where(lane_a, dq_t[2 * hp], dq_t[2 * hp + 1]) for hp, dq in enumerate(dqs))

        def step(slot, j, dqs, nlrun, grun):
            dqs = post(1 - slot, j + 1, dqs)
            pre(1 - slot, jnp.maximum(j - 1, 0))
            return (dqs,) + chain(slot, nlrun, grun, False)

        def step2(m, carry):
            j = i - 1 - 2 * m
            return step(0, j - 1, *step(1, j, *carry))

        zero = (jnp.zeros((SB_BQ, 1), F32),) * SB_NCH
        pre(0, i)
        nlrun, grun = chain(0, zero, zero, True)
        pre(1, jnp.maximum(i - 1, 0))
        carry = lax.fori_loop(0, i // 2, step2, ((jnp.zeros((SB_BQ, 128), F32),) * nhp, nlrun, grun))
        dqs = lax.cond(i % 2 == 1,
                       lambda: post(1, 0, step(1, 0, *carry)[0]),
                       lambda: post(0, 0, carry[0]))
        dq_ref[...] = 0.125 * jnp.concatenate(dqs, axis=1)

        @pl.when(i == nq - 1)
        def _():
            cols = pl.ds(pl.multiple_of(g_idx * SB_LANES, SB_LANES), SB_LANES)
            ck = pltpu.make_async_copy(dk_acc, dk_hbm.at[:, cols], sems.at[0])
            cv = pltpu.make_async_copy(dv_acc, dv_hbm.at[:, cols], sems.at[1])
            ck.start()
            cv.start()
            ck.wait()
            cv.wait()

    qblk = pl.BlockSpec((SB_BQ, SB_LANES), lambda g, i: (i, g))
    hbm = pl.BlockSpec(memory_space=pl.ANY)
    return pl.pallas_call(
        body, name=name, grid=(ng, nq),
        in_specs=[qblk, pl.BlockSpec((s, SB_LANES), lambda g, i: (0, ng + g)),
                  pl.BlockSpec((s, SB_LANES), lambda g, i: (0, 2 * ng + g)), qblk, qblk],
        out_specs=[qblk, hbm, hbm],
        out_shape=[jax.ShapeDtypeStruct((s, D), F32)] * 3,
        scratch_shapes=[pltpu.VMEM((s, SB_LANES), F32), pltpu.VMEM((s, SB_LANES), F32),
                        pltpu.SemaphoreType.DMA((2,)),
                        pltpu.VMEM((2, SB_NCH, SB_BQ, SB_BK), F32), pltpu.VMEM((2, SB_NCH, SB_BQ, SB_BK), F32),
                        pltpu.VMEM((SB_NCH, SB_BQ, 2 * SB_BK), BF16), pltpu.VMEM((SB_NCH, SB_BQ, SB_BK), F32),
                        pltpu.VMEM((2, SB_NCH, SB_BQ, SB_BK), BF16), pltpu.VMEM((2, SB_NCH, SB_BQ, SB_BK), BF16)],
        compiler_params=_cp("arbitrary", "arbitrary"),
    )(qkv, qkv, qkv, o, do)


def _ssd_core(z, xpre, dtr, state, dtb, alog, dsk, nw):
    L = SSD_CHUNK
    xa = _silu(xpre)
    pieces = _split(xa, 128)
    xs = jnp.concatenate(pieces[:8], axis=1)
    bm, cm = pieces[8:10], pieces[10:12]
    dt = _softplus(dtr + dtb)
    a = dt * (-jnp.exp(alog))
    tri = (_iota((L, L), 0) >= _iota((L, L), 1)).astype(F32)
    a_cs = _dot("nn", tri, a, 1, 3)
    xc = xs * dt
    tril = _iota((L, L), 0) >= _iota((L, L), 1)
    lane_a = _iota((L, 128), 1) < 64
    acs_p = _split(a_cs, 128)
    xc_p = _split(xc, 128)
    ys, new_states = [], []
    for g in range(2):
        cb = _dot("nt", cm[g], bm[g])
        for pp in range(4):
            pair = 4 * g + pp
            acs = acs_p[pair]
            acs_t = acs.T
            xcp = xc_p[pair]
            st = state[pair]
            heads = []
            for hh in range(2):
                col = _take_fn(1, 64 * hh)(acs)
                row = _take_fn(0, 64 * hh)(acs_t)
                seg = col - row
                lm = jnp.where(tril, jnp.exp(jnp.where(tril, seg, 0.0)), 0.0)
                heads.append(_dot("nn", cb * lm, xcp))
            ydiag = jnp.where(lane_a, heads[0], heads[1])
            last = _take_fn(0, L - 1)(acs)
            snew = _dot("tn", xcp * jnp.exp(last - acs), bm[g])
            new_states.append(st * jnp.exp(_take_fn(1, L - 1)(acs_t)) + snew)
            yoff = _dot("nt", cm[g], st) * jnp.exp(acs)
            ys.append(ydiag + yoff)
    y = jnp.concatenate(ys, axis=1) + xs * dsk
    yg = y * _silu(z)
    outs = []
    for v in _split(yg, 512):
        outs.append(v * lax.rsqrt(jnp.mean(v * v, axis=-1, keepdims=True) + EPS))
    return jnp.concatenate(outs, axis=1) * nw, tuple(new_states)


XBC = 1536


def _ssd_conv(ext_ref, cw, cb):
    acc = cb
    for k in range(4):
        acc = acc + cw[k:k + 1, :] * ext_ref[pl.ds(HALO - 3 + k, SSD_CHUNK), :]
    return acc


def _ssd_fwd(z, xbc, dtr, cw, cb, lanes, *, name):
    s = z.shape[0]
    L = SSD_CHUNK
    nc = s // L

    def body(z_ref, x_ref, h_ref, dtr_ref, cw_ref, cb_ref, ln_ref, y_ref, st_ref, state, ext):
        c = pl.program_id(0)

        @pl.when(c == 0)
        def _():
            state[...] = jnp.zeros_like(state)

        ext[0:HALO, :] = jnp.where(c == 0, 0.0, h_ref[...])
        ext[HALO:, :] = x_ref[...]
        xpre = _ssd_conv(ext, cw_ref[...], cb_ref[...])
        st_ref[0] = state[...]
        st_in = tuple(state[p] for p in range(8))
        yn, st_out = _ssd_core(z_ref[...], xpre, dtr_ref[...], st_in,
                               ln_ref[0:1, :], ln_ref[1:2, :], ln_ref[2:3, :], ln_ref[3:4, :])
        y_ref[...] = yn.astype(BF16)
        for p in range(8):
            state[p] = st_out[p]

    return pl.pallas_call(
        body, name=name, grid=(nc,),
        in_specs=[pl.BlockSpec((L, D), lambda c: (c, 0)),
                  pl.BlockSpec((L, XBC), lambda c: (c, 0)),
                  pl.BlockSpec((HALO, XBC), lambda c: (jnp.maximum(c * (L // HALO) - 1, 0), 0)),
                  pl.BlockSpec((L, D), lambda c: (c, 0)),
                  pl.BlockSpec((4, XBC), lambda c: (0, 0)),
                  pl.BlockSpec((1, XBC), lambda c: (0, 0)),
                  pl.BlockSpec((8, D), lambda c: (0, 0))],
        out_specs=[pl.BlockSpec((L, D), lambda c: (c, 0)),
                   pl.BlockSpec((1, 8, 128, 128), lambda c: (c, 0, 0, 0))],
        out_shape=[jax.ShapeDtypeStruct((s, D), BF16), jax.ShapeDtypeStruct((nc, 8, 128, 128), F32)],
        scratch_shapes=[pltpu.VMEM((8, 128, 128), F32), pltpu.VMEM((L + HALO, XBC), F32)],
        compiler_params=_cp("arbitrary"),
    )(z, xbc, xbc, dtr, cw, cb, lanes)


def _ssd_bwd(z, xbc, dtr, states, dymix, cw, cb, lanes, *, name):
    s = z.shape[0]
    L = SSD_CHUNK
    nc = s // L

    def body(z_ref, x_ref, h_ref, dtr_ref, st_ref, dy_ref, cw_ref, cb_ref, ln_ref,
             dz_ref, dx_ref, ddt_ref, dln_ref, dcv_ref, dstate, ext, dext):
        i = pl.program_id(0)
        c = nc - 1 - i

        @pl.when(i == 0)
        def _():
            dstate[...] = jnp.zeros_like(dstate)
            dext[...] = jnp.zeros_like(dext)
            dln_ref[...] = jnp.zeros_like(dln_ref)
            dcv_ref[...] = jnp.zeros_like(dcv_ref)

        ext[0:HALO, :] = jnp.where(c == 0, 0.0, h_ref[...])
        ext[HALO:, :] = x_ref[...]
        cwv = cw_ref[...]
        xpre = _ssd_conv(ext, cwv, cb_ref[...])
        st_in = tuple(st_ref[0, p] for p in range(8))
        _, vjp = jax.vjp(_ssd_core, z_ref[...], xpre, dtr_ref[...], st_in,
                         ln_ref[0:1, :], ln_ref[1:2, :], ln_ref[2:3, :], ln_ref[3:4, :])
        dz, dxpre, ddtr, dst, d0, d1, d2, d3 = vjp((dy_ref[...], tuple(dstate[p] for p in range(8))))
        for p in range(8):
            dstate[p] = dst[p]
        dz_ref[...] = dz.astype(BF16)
        ddt_ref[...] = ddtr.astype(BF16)
        dln_ref[0:4, :] += jnp.concatenate([d0, d1, d2, d3], axis=0)
        dext[0:L, :] = dxpre
        dx = jnp.zeros((L, XBC), F32)
        rows = []
        for k in range(4):
            dx = dx + cwv[k:k + 1, :] * dext[pl.ds(3 - k, L), :]
            rows.append(jnp.sum(dxpre * ext[pl.ds(HALO - 3 + k, L), :], axis=0, keepdims=True))
        rows.append(jnp.sum(dxpre, axis=0, keepdims=True))
        dx_ref[...] = dx.astype(BF16)
        dcv_ref[0:5, :] += jnp.concatenate(rows, axis=0)
        dext[L:L + HALO, :] = dxpre[0:HALO, :]

    rev = lambda i: (nc - 1 - i, 0)
    return pl.pallas_call(
        body, name=name, grid=(nc,),
        in_specs=[pl.BlockSpec((L, D), rev),
                  pl.BlockSpec((L, XBC), rev),
                  pl.BlockSpec((HALO, XBC), lambda i: (jnp.maximum((nc - 1 - i) * (L // HALO) - 1, 0), 0)),
                  pl.BlockSpec((L, D), rev),
                  pl.BlockSpec((1, 8, 128, 128), lambda i: (nc - 1 - i, 0, 0, 0)),
                  pl.BlockSpec((L, D), rev),
                  pl.BlockSpec((4, XBC), lambda i: (0, 0)),
                  pl.BlockSpec((1, XBC), lambda i: (0, 0)),
                  pl.BlockSpec((8, D), lambda i: (0, 0))],
        out_specs=[pl.BlockSpec((L, D), rev), pl.BlockSpec((L, XBC), rev), pl.BlockSpec((L, D), rev),
                   pl.BlockSpec((8, D), lambda i: (0, 0)), pl.BlockSpec((8, XBC), lambda i: (0, 0))],
        out_shape=[jax.ShapeDtypeStruct((s, D), BF16), jax.ShapeDtypeStruct((s, XBC), BF16),
                   jax.ShapeDtypeStruct((s, D), BF16), jax.ShapeDtypeStruct((8, D), F32),
                   jax.ShapeDtypeStruct((8, XBC), F32)],
        scratch_shapes=[pltpu.VMEM((8, 128, 128), F32), pltpu.VMEM((L + HALO, XBC), F32),
                        pltpu.VMEM((L + HALO, XBC), F32)],
        compiler_params=_cp("arbitrary"),
    )(z, xbc, xbc, dtr, states, dymix, cw, cb, lanes)


def _mem_attn_math(q, k, v):
    outs = []
    for qh, kh, vh in zip(_split(q, 256), _split(k, 256), _split(v, 256)):
        sc = _dot("nt", qh, kh) * (1.0 / 16.0)
        e = jnp.exp(sc - lax.stop_gradient(jnp.max(sc, axis=-1, keepdims=True)))
        p = e / jnp.sum(e, axis=-1, keepdims=True)
        outs.append(_dot("nn", p, vh))
    return jnp.concatenate(outs, axis=1)


def _mem_attn_fwd(q, k, v, *, name):
    s, m = q.shape[0], k.shape[0]
    tm = _tile(s, 256, 8)

    def body(q_ref, k_ref, v_ref, o_ref):
        o_ref[...] = _mem_attn_math(q_ref[...].astype(F32), k_ref[...].astype(F32),
                                    v_ref[...].astype(F32)).astype(BF16)

    row = pl.BlockSpec((tm, D), lambda i: (i, 0))
    kv = pl.BlockSpec((m, D), lambda i: (0, 0))
    return pl.pallas_call(
        body, name=name, grid=(s // tm,), in_specs=[row, kv, kv], out_specs=row,
        out_shape=jax.ShapeDtypeStruct((s, D), BF16), compiler_params=_cp("parallel"),
    )(q, k, v)


def _mem_attn_bwd(q, k, v, do, *, name):
    s, m = q.shape[0], k.shape[0]
    tm = _tile(s, 256, 8)

    def body(q_ref, k_ref, v_ref, do_ref, dq_ref, dk_ref, dv_ref):
        @pl.when(pl.program_id(0) == 0)
        def _():
            dk_ref[...] = jnp.zeros_like(dk_ref)
            dv_ref[...] = jnp.zeros_like(dv_ref)

        _, vjp = jax.vjp(_mem_attn_math, q_ref[...].astype(F32), k_ref[...].astype(F32),
                         v_ref[...].astype(F32))
        dq, dk, dv = vjp(do_ref[...])
        dq_ref[...] = dq.astype(BF16)
        dk_ref[...] += dk
        dv_ref[...] += dv

    row = pl.BlockSpec((tm, D), lambda i: (i, 0))
    kv = pl.BlockSpec((m, D), lambda i: (0, 0))
    return pl.pallas_call(
        body, name=name, grid=(s // tm,), in_specs=[row, kv, kv, row], out_specs=[row, kv, kv],
        out_shape=[jax.ShapeDtypeStruct((s, D), BF16), jax.ShapeDtypeStruct((m, D), F32),
                   jax.ShapeDtypeStruct((m, D), F32)],
        compiler_params=_cp("arbitrary"),
    )(q, k, v, do)


DFF = 2816
FFN_TC = 1408
FFN_TM = 256


def _ffn_conv(ext_ref, cw, cb, tm):
    acc = cb
    for k in range(3):
        acc = acc + cw[k:k + 1, :] * ext_ref[pl.ds(HALO - 2 + k, tm), :]
    return acc


def _ffn_specs(s):
    tm, tc = FFN_TM, FFN_TC
    blk = pl.BlockSpec((tm, tc), lambda i, j: (i, j))
    halo = pl.BlockSpec((HALO, tc), lambda i, j: (jnp.maximum(i * (tm // HALO) - 1, 0), j))
    cw = pl.BlockSpec((3, tc), lambda i, j: (0, j))
    cb = pl.BlockSpec((1, tc), lambda i, j: (0, j))
    return tm, tc, blk, halo, cw, cb


def _glu_fwd(ug, uv, cwg, cwv, cbg, cbv, *, name):
    s = ug.shape[0]
    tm, tc, blk, halo, cw, cb = _ffn_specs(s)

    def body(g_ref, gh_ref, v_ref, vh_ref, cwg_ref, cwv_ref, cbg_ref, cbv_ref, f_ref, eg, ev):
        first = pl.program_id(0) == 0
        eg[0:HALO, :] = jnp.where(first, 0.0, gh_ref[...])
        eg[HALO:, :] = g_ref[...]
        ev[0:HALO, :] = jnp.where(first, 0.0, vh_ref[...])
        ev[HALO:, :] = v_ref[...]
        g = _ffn_conv(eg, cwg_ref[...], cbg_ref[...], tm)
        v = _ffn_conv(ev, cwv_ref[...], cbv_ref[...], tm)
        f_ref[...] = (_silu(g) * v).astype(BF16)

    return pl.pallas_call(
        body, name=name, grid=(s // tm, DFF // tc),
        in_specs=[blk, halo, blk, halo, cw, cw, cb, cb], out_specs=blk,
        out_shape=jax.ShapeDtypeStruct((s, DFF), BF16),
        scratch_shapes=[pltpu.VMEM((tm + HALO, tc), F32)] * 2,
        compiler_params=_cp("parallel", "parallel"),
    )(ug, ug, uv, uv, cwg, cwv, cbg, cbv)


def _glu_bwd(ug, uv, df, cwg, cwv, cbg, cbv, *, name):
    s = ug.shape[0]
    tm, tc, blk, halo, cw, cb = _ffn_specs(s)

    def body(g_ref, gh_ref, v_ref, vh_ref, df_ref, cwg_ref, cwv_ref, cbg_ref, cbv_ref, dg_ref, dv_ref, eg, ev):
        first = pl.program_id(0) == 0
        eg[0:HALO, :] = jnp.where(first, 0.0, gh_ref[...])
        eg[HALO:, :] = g_ref[...]
        ev[0:HALO, :] = jnp.where(first, 0.0, vh_ref[...])
        ev[HALO:, :] = v_ref[...]
        g = _ffn_conv(eg, cwg_ref[...], cbg_ref[...], tm)
        v = _ffn_conv(ev, cwv_ref[...], cbv_ref[...], tm)
        dfv = df_ref[...]
        sg = _sigmoid(g)
        dv_ref[...] = dfv * g * sg
        dg_ref[...] = dfv * v * sg * (1.0 + g * (1.0 - sg))

    return pl.pallas_call(
        body, name=name, grid=(s // tm, DFF // tc),
        in_specs=[blk, halo, blk, halo, blk, cw, cw, cb, cb], out_specs=[blk, blk],
        out_shape=[jax.ShapeDtypeStruct((s, DFF), F32)] * 2,
        scratch_shapes=[pltpu.VMEM((tm + HALO, tc), F32)] * 2,
        compiler_params=_cp("parallel", "parallel"),
    )(ug, ug, uv, uv, df, cwg, cwv, cbg, cbv)


def _ffn_conv_bwd(du, u, cwt, *, name):
    s = du.shape[0]
    tm, tc = FFN_TM, FFN_TC
    nb = s // tm

    def body(du_ref, duh_ref, u_ref, uh_ref, cw_ref, dx_ref, dc_ref, edu, eu):
        i = pl.program_id(1)

        @pl.when(i == 0)
        def _():
            dc_ref[...] = jnp.zeros_like(dc_ref)

        duv = du_ref[...]
        edu[0:tm, :] = duv
        edu[tm:, :] = jnp.where(i == nb - 1, 0.0, duh_ref[...])
        eu[0:HALO, :] = jnp.where(i == 0, 0.0, uh_ref[...])
        eu[HALO:, :] = u_ref[...]
        cwv = cw_ref[...]
        dx = jnp.zeros((tm, tc), F32)
        rows = []
        for k in range(3):
            dx = dx + cwv[k:k + 1, :] * edu[pl.ds(2 - k, tm), :]
            rows.append(jnp.sum(duv * eu[pl.ds(HALO - 2 + k, tm), :], axis=0, keepdims=True))
        rows.append(jnp.sum(duv, axis=0, keepdims=True))
        dx_ref[...] = dx.astype(BF16)
        dc_ref[0:4, :] += jnp.concatenate(rows, axis=0)

    blk = pl.BlockSpec((tm, tc), lambda j, i: (i, j))
    nxt = pl.BlockSpec((HALO, tc), lambda j, i: (jnp.minimum((i + 1) * (tm // HALO), s // HALO - 1), j))
    prv = pl.BlockSpec((HALO, tc), lambda j, i: (jnp.maximum(i * (tm // HALO) - 1, 0), j))
    return pl.pallas_call(
        body, name=name, grid=(DFF // tc, nb),
        in_specs=[blk, nxt, blk, prv, pl.BlockSpec((3, tc), lambda j, i: (0, j))],
        out_specs=[blk, pl.BlockSpec((8, tc), lambda j, i: (0, j))],
        out_shape=[jax.ShapeDtypeStruct((s, DFF), BF16), jax.ShapeDtypeStruct((8, DFF), F32)],
        scratch_shapes=[pltpu.VMEM((tm + HALO, tc), F32)] * 2,
        compiler_params=_cp("parallel", "arbitrary"),
    )(du, du, u, u, cwt)


MESH = pl.DeviceIdType.MESH


def _all_gather(arrs, *, name):
    n = len(arrs)

    def body(*refs):
        x_refs, out_refs = refs[:n], refs[n:2 * n]
        send_sems, recv_sems, local_sems = refs[2 * n:]
        x, y, c = lax.axis_index("x"), lax.axis_index("y"), lax.axis_index("c")
        me, sibling = (x, y, c), (x, y, 1 - c)
        chips = [(1 - x, y), (x, 1 - y), (1 - x, 1 - y)]

        def blk(a, dev):
            return out_refs[a].at[4 * dev[0] + 2 * dev[1] + dev[2]]

        def copy(a, k, block, to, src=None):
            return pltpu.make_async_remote_copy(
                src_ref=blk(a, block) if src is None else src, dst_ref=blk(a, block),
                send_sem=send_sems.at[7 * a + k], recv_sem=recv_sems.at[7 * a + k],
                device_id=to, device_id_type=MESH)

        started = []
        mine = []
        for a in range(n):
            cp = pltpu.make_async_copy(x_refs[a], blk(a, me), local_sems.at[a])
            cp.start()
            mine.append(cp)
            first = [copy(a, 0, me, sibling, src=x_refs[a])]
            first += [copy(a, 1 + j, me, (*chip, c), src=x_refs[a]) for j, chip in enumerate(chips)]
            for cp in first:
                cp.start()
            started += first
        for a in range(n):
            for j, chip in enumerate(chips):
                copy(a, 1 + j, (*chip, c), me).wait_recv()
                fwd = copy(a, 4 + j, (*chip, c), sibling)
                fwd.start()
                started.append(fwd)
        for a in range(n):
            copy(a, 0, sibling, me).wait_recv()
            for j, chip in enumerate(chips):
                copy(a, 4 + j, (*chip, 1 - c), me).wait_recv()
        for cp in started:
            cp.wait_send()
        for cp in mine:
            cp.wait()

    any_spec = pl.BlockSpec(memory_space=pl.ANY)
    return pl.pallas_call(
        body, name=name,
        in_specs=[any_spec] * n, out_specs=[any_spec] * n,
        out_shape=[jax.ShapeDtypeStruct((NDEV,) + a.shape, a.dtype) for a in arrs],
        scratch_shapes=[pltpu.SemaphoreType.DMA((7 * n,)), pltpu.SemaphoreType.DMA((7 * n,)),
                        pltpu.SemaphoreType.DMA((n,))],
    )(*arrs)


def _exchange(send, *, name):
    def body(send_ref, recv_ref, send_sems, recv_sems, local_sem):
        x, y, c = lax.axis_index("x"), lax.axis_index("y"), lax.axis_index("c")
        me = 4 * x + 2 * y + c
        mine = pltpu.make_async_copy(send_ref.at[me], recv_ref.at[me], local_sem)
        mine.start()
        copies = []
        for k in range(1, NDEV):
            px = 1 - x if k & 4 else x
            py = 1 - y if k & 2 else y
            pc = 1 - c if k & 1 else c
            cp = pltpu.make_async_remote_copy(
                src_ref=send_ref.at[4 * px + 2 * py + pc], dst_ref=recv_ref.at[me],
                send_sem=send_sems.at[k - 1], recv_sem=recv_sems.at[k - 1],
                device_id=(px, py, pc), device_id_type=MESH)
            cp.start()
            copies.append(cp)
        for cp in copies:
            cp.wait_recv()
        for cp in copies:
            cp.wait_send()
        mine.wait()

    any_spec = pl.BlockSpec(memory_space=pl.ANY)
    return pl.pallas_call(
        body, name=name, in_specs=[any_spec], out_specs=any_spec,
        out_shape=jax.ShapeDtypeStruct(send.shape, send.dtype),
        scratch_shapes=[pltpu.SemaphoreType.DMA((7,)), pltpu.SemaphoreType.DMA((7,)),
                        pltpu.SemaphoreType.DMA],
    )(send)


def _adamw(parts, w, m, v, *, name):
    r = w.shape[0]
    tm = _tile(r, 256, PACK_ALIGN)
    c1 = 1.0 - ADAM_B1 ** ADAM_STEP
    c2 = 1.0 - ADAM_B2 ** ADAM_STEP

    def body(p_ref, w_ref, m_ref, v_ref, g_ref, d_ref, nm_ref, nv_ref):
        g = p_ref[0].astype(F32)
        for i in range(1, NDEV):
            g = g + p_ref[i].astype(F32)
        nm = ADAM_B1 * m_ref[...] + (1.0 - ADAM_B1) * g
        nv = ADAM_B2 * v_ref[...] + (1.0 - ADAM_B2) * (g * g)
        d_ref[...] = -ADAM_LR * ((nm / c1) / (jnp.sqrt(nv / c2) + ADAM_EPS) + ADAM_WD * w_ref[...])
        g_ref[...] = g
        nm_ref[...] = nm
        nv_ref[...] = nv

    row = pl.BlockSpec((tm, D), lambda i: (i, 0))
    return pl.pallas_call(
        body, name=name, grid=(r // tm,),
        in_specs=[pl.BlockSpec((NDEV, tm, D), lambda i: (0, i, 0)), row, row, row],
        out_specs=[row] * 4, out_shape=[jax.ShapeDtypeStruct((r, D), F32)] * 4,
        compiler_params=_cp("parallel"),
    )(parts, w, m, v)


PACK_ALIGN = 16


def _part_rows(shape):
    n = -(-math.prod(shape) // D)
    return n + (-n) % PACK_ALIGN


def _rows(a):
    flat = a.reshape(-1)
    pad = _part_rows(a.shape) * D - flat.shape[0]
    if pad:
        flat = jnp.concatenate([flat, jnp.zeros((pad,), flat.dtype)])
    return flat.reshape(-1, D)


def _pack(parts, total_rows):
    rows = [_rows(p) for p in parts]
    used = sum(r.shape[0] for r in rows)
    if total_rows > used:
        rows.append(jnp.zeros((total_rows - used, D), rows[0].dtype))
    return jnp.concatenate(rows, axis=0)


def _unpack(buf, shapes):
    out, r0 = [], 0
    for shp in shapes:
        n = math.prod(shp)
        out.append(buf[r0:r0 + _part_rows(shp)].reshape(-1)[:n].reshape(shp))
        r0 += _part_rows(shp)
    return out


BIG = [("w_in", (D, 706)), ("w_out", (256, D)), ("w_mq", (128, D)), ("w_mk", (128, D)), ("w_mv", (128, D)),
       ("w_mo", (128, D)), ("w_up", (D, 704)), ("w_down", (352, D)), ("conv_ssd_w", (4, 192)),
       ("conv_ffn_w", (3, 704))]
BIG_ROW0 = {}
_r = 0
for _n, _shp in BIG:
    BIG_ROW0[_n] = _r
    _r += _part_rows(_shp)
BIG_ROWS = _r + (-_r) % 128
SMALL = [("norm_mix_w", (1, D)), ("conv_ssd_b", (1, 1536)), ("dt_bias", (1, 16)), ("a_log", (1, 16)),
         ("d_skip", (1, 16)), ("ssd_norm_w", (1, D)), ("sb_norm_w", (1, D)), ("norm_mem_w", (1, D)),
         ("norm_memkv_w", (1, D)), ("norm_ffn_w", (1, D)), ("conv_ffn_b", (1, 5632)), ("norm_final_w", (D,))]
LOSS_ROW = sum(_part_rows(_shp) for _, _shp in SMALL)
SMALL_ROWS = LOSS_ROW + PACK_ALIGN
ORDER = ["norm_mix_w", "w_in", "conv_ssd_w", "conv_ssd_b", "dt_bias", "a_log", "d_skip", "ssd_norm_w",
         "sb_norm_w", "w_out", "norm_mem_w", "norm_memkv_w", "w_mq", "w_mk", "w_mv", "w_mo", "norm_ffn_w",
         "w_up", "conv_ffn_w", "conv_ffn_b", "w_down", "norm_final_w"]


def _gather_cols(g, r0, nr, rows, cols):
    t = g[:, r0:r0 + nr].reshape(NDEV, rows, cols)
    return t.transpose(1, 0, 2).reshape(rows, NDEV * cols)


def _scatter_cols(full, cols):
    rows = full.shape[0]
    return full.reshape(rows, NDEV, cols).transpose(1, 0, 2).reshape(NDEV, -1, D)


def _pad_rows(a, nr):
    n = a.shape[1]
    return jnp.concatenate([a, jnp.zeros((NDEV, nr * D - n), a.dtype)], axis=1).reshape(NDEV, nr, D)


def _group_sum(lanes):
    return lanes.reshape(16, 64).sum(axis=1).reshape(1, 16)


def kernel(x, mem, norm_mix_w, w_in, conv_ssd_w, conv_ssd_b, dt_bias, a_log, d_skip, ssd_norm_w, sb_norm_w, w_out, norm_mem_w, norm_memkv_w, w_mq, w_mk, w_mv, w_mo, norm_ffn_w, w_up, conv_ffn_w, conv_ffn_b, w_down, norm_final_w, loss_target, m_norm_mix_w, m_w_in, m_conv_ssd_w, m_conv_ssd_b, m_dt_bias, m_a_log, m_d_skip, m_ssd_norm_w, m_sb_norm_w, m_w_out, m_norm_mem_w, m_norm_memkv_w, m_w_mq, m_w_mk, m_w_mv, m_w_mo, m_norm_ffn_w, m_w_up, m_conv_ffn_w, m_conv_ffn_b, m_w_down, m_norm_final_w, v_norm_mix_w, v_w_in, v_conv_ssd_w, v_conv_ssd_b, v_dt_bias, v_a_log, v_d_skip, v_ssd_norm_w, v_sb_norm_w, v_w_out, v_norm_mem_w, v_norm_memkv_w, v_w_mq, v_w_mk, v_w_mv, v_w_mo, v_norm_ffn_w, v_w_up, v_conv_ffn_w, v_conv_ffn_b, v_w_down, v_norm_final_w):
    P = dict(norm_mix_w=norm_mix_w, w_in=w_in, conv_ssd_w=conv_ssd_w, conv_ssd_b=conv_ssd_b, dt_bias=dt_bias, a_log=a_log, d_skip=d_skip, ssd_norm_w=ssd_norm_w, sb_norm_w=sb_norm_w, w_out=w_out, norm_mem_w=norm_mem_w, norm_memkv_w=norm_memkv_w, w_mq=w_mq, w_mk=w_mk, w_mv=w_mv, w_mo=w_mo, norm_ffn_w=norm_ffn_w, w_up=w_up, conv_ffn_w=conv_ffn_w, conv_ffn_b=conv_ffn_b, w_down=w_down, norm_final_w=norm_final_w)
    M = dict(norm_mix_w=m_norm_mix_w, w_in=m_w_in, conv_ssd_w=m_conv_ssd_w, conv_ssd_b=m_conv_ssd_b, dt_bias=m_dt_bias, a_log=m_a_log, d_skip=m_d_skip, ssd_norm_w=m_ssd_norm_w, sb_norm_w=m_sb_norm_w, w_out=m_w_out, norm_mem_w=m_norm_mem_w, norm_memkv_w=m_norm_memkv_w, w_mq=m_w_mq, w_mk=m_w_mk, w_mv=m_w_mv, w_mo=m_w_mo, norm_ffn_w=m_norm_ffn_w, w_up=m_w_up, conv_ffn_w=m_conv_ffn_w, conv_ffn_b=m_conv_ffn_b, w_down=m_w_down, norm_final_w=m_norm_final_w)
    V = dict(norm_mix_w=v_norm_mix_w, w_in=v_w_in, conv_ssd_w=v_conv_ssd_w, conv_ssd_b=v_conv_ssd_b, dt_bias=v_dt_bias, a_log=v_a_log, d_skip=v_d_skip, ssd_norm_w=v_ssd_norm_w, sb_norm_w=v_sb_norm_w, w_out=v_w_out, norm_mem_w=v_norm_mem_w, norm_memkv_w=v_norm_memkv_w, w_mq=v_w_mq, w_mk=v_w_mk, w_mv=v_w_mv, w_mo=v_w_mo, norm_ffn_w=v_norm_ffn_w, w_up=v_w_up, conv_ffn_w=v_conv_ffn_w, conv_ffn_b=v_conv_ffn_b, w_down=v_w_down, norm_final_w=v_norm_final_w)
    big_shapes = [shp for _, shp in BIG]
    small_shapes = [shp for _, shp in SMALL]

    w_big = _pack([P[n][0] for n, _ in BIG], BIG_ROWS)
    c0 = BIG_ROW0["conv_ssd_w"]
    g16, g32 = _all_gather([w_big.astype(BF16), w_big[c0:c0 + 2 * PACK_ALIGN]], name="gather_weights")
    R0 = BIG_ROW0
    W_in = _gather_cols(g16, R0["w_in"], 706, D, 706)
    W_out = g16[:, R0["w_out"]:R0["w_out"] + 256].reshape(2 * D, D)
    W_mq, W_mk, W_mv, W_mo = [g16[:, R0[n]:R0[n] + 128].reshape(D, D) for n in ("w_mq", "w_mk", "w_mv", "w_mo")]
    W_up = _gather_cols(g16, R0["w_up"], 704, D, 704)
    W_down = g16[:, R0["w_down"]:R0["w_down"] + 352].reshape(DFF, D)
    cw_ssd = g32[:, 0].reshape(NDEV, -1)[:, :768].reshape(NDEV, 4, 192).transpose(1, 0, 2).reshape(4, XBC)
    cw_ffn = (g32[:, PACK_ALIGN:PACK_ALIGN + 3].reshape(NDEV, -1)[:, :2112].reshape(NDEV, 3, 704)
              .transpose(1, 0, 2).reshape(3, 2 * DFF))
    W_z, W_xbc, W_dt, W_qkv = W_in[:, :D], W_in[:, D:D + XBC], W_in[:, D + XBC:D + XBC + 16], W_in[:, D + XBC + 16:]
    W_dtr = jnp.repeat(W_dt, 64, axis=1)
    W_upg, W_upv = W_up[:, :DFF], W_up[:, DFF:]
    cwg, cwv = cw_ffn[:, :DFF], cw_ffn[:, DFF:]
    cbg, cbv = conv_ffn_b[:, :DFF], conv_ffn_b[:, DFF:]
    rep = lambda p: jnp.repeat(p, 64, axis=1)
    lanes = jnp.concatenate([rep(dt_bias), rep(a_log), rep(d_skip), ssd_norm_w, jnp.zeros((4, D), F32)], axis=0)

    xs, tgt, mm = x[0], loss_target[0], mem[0]

    h1 = _norm_fwd(xs, norm_mix_w, name="norm_mix")
    z = _mm(h1, W_z, name="proj_z")
    xbc = _mm(h1, W_xbc, name="proj_xbc")
    dtr = _mm(h1, W_dtr, name="proj_dt")
    qkv = _mm(h1, W_qkv, name="proj_qkv", out_dtype=BF16)
    y_ssd, states = _ssd_fwd(z, xbc, dtr, cw_ssd, conv_ssd_b, lanes, name="ssd_fwd")
    o_sb = _sb_fwd(qkv, name="sb_fwd")
    y_sb = _head_norm_fwd(o_sb, sb_norm_w, name="sb_norm")
    ymix = jnp.concatenate([y_ssd, y_sb], axis=1)
    x1 = _mm(ymix, W_out, add=xs, name="proj_out")
    h2 = _norm_fwd(x1, norm_mem_w, name="norm_mem")
    mn = _norm_fwd(mm, norm_memkv_w, name="norm_memkv")
    qm = _mm(h2, W_mq, name="mem_q", out_dtype=BF16)
    km = _mm(mn, W_mk, name="mem_k", out_dtype=BF16)
    vm = _mm(mn, W_mv, name="mem_v", out_dtype=BF16)
    om = _mem_attn_fwd(qm, km, vm, name="mem_attn")
    x2 = _mm(om, W_mo, add=x1, name="mem_o")
    h3 = _norm_fwd(x2, norm_ffn_w, name="norm_ffn")
    ug = _mm(h3, W_upg, name="ffn_up_g")
    uv = _mm(h3, W_upv, name="ffn_up_v")
    f = _glu_fwd(ug, uv, cwg, cwv, cbg, cbv, name="ffn_glu")
    x3 = _mm(f, W_down, add=x2, name="ffn_down")
    dx3, g_nfinal, loss_part = _final(x3, norm_final_w.reshape(1, D), tgt, name="final_loss")

    G = {}
    G["w_down"] = _mm(f, dx3, trans_a=True, name="g_w_down")
    df = _mm(dx3, W_down.T, name="d_f")
    dug, duv = _glu_bwd(ug, uv, df, cwg, cwv, cbg, cbv, name="ffn_glu_bwd")
    dupg, dcg = _ffn_conv_bwd(dug, ug, cwg, name="ffn_conv_bwd_g")
    dupv, dcv = _ffn_conv_bwd(duv, uv, cwv, name="ffn_conv_bwd_v")
    G["w_up"] = jnp.concatenate([_mm(h3, dupg, trans_a=True, name="g_w_up_g"),
                                 _mm(h3, dupv, trans_a=True, name="g_w_up_v")], axis=1)
    G["conv_ffn_w"] = jnp.concatenate([dcg[0:3], dcv[0:3]], axis=1)
    G["conv_ffn_b"] = jnp.concatenate([dcg[3:4], dcv[3:4]], axis=1)
    dh3 = _mm(dupg, W_upg.T, name="d_h3_g")
    dh3 = _mm(dupv, W_upv.T, add=dh3, name="d_h3_v")
    dx2, G["norm_ffn_w"] = _norm_bwd(x2, norm_ffn_w, dh3, dx3, name="norm_ffn_bwd")
    G["w_mo"] = _mm(om, dx2, trans_a=True, name="g_w_mo")
    dom = _mm(dx2, W_mo.T, name="d_om")
    dqm, dkm, dvm = _mem_attn_bwd(qm, km, vm, dom, name="mem_attn_bwd")
    G["w_mq"] = _mm(h2, dqm, trans_a=True, name="g_w_mq")
    G["w_mk"] = _mm(mn, dkm, trans_a=True, name="g_w_mk")
    G["w_mv"] = _mm(mn, dvm, trans_a=True, name="g_w_mv")
    dh2 = _mm(dqm, W_mq.T, name="d_h2")
    dmn = _mm(dkm, W_mk.T, name="d_mn_k")
    dmn = _mm(dvm, W_mv.T, add=dmn, name="d_mn_v")
    _, G["norm_memkv_w"] = _norm_bwd(mm, norm_memkv_w, dmn, None, name="norm_memkv_bwd")
    dx1, G["norm_mem_w"] = _norm_bwd(x1, norm_mem_w, dh2, dx2, name="norm_mem_bwd")
    G["w_out"] = _mm(ymix, dx1, trans_a=True, name="g_w_out")
    dymix = _mm(dx1, W_out.T, name="d_ymix")
    do_sb, G["sb_norm_w"] = _head_norm_bwd(o_sb, sb_norm_w, dymix, name="sb_norm_bwd")
    dq, dk, dv = _sb_bwd(qkv, o_sb, do_sb, name="sb_bwd")
    dqkv = jnp.concatenate([dq, dk, dv], axis=1)
    dz, dxbc, ddtr, dlanes, dconv = _ssd_bwd(z, xbc, dtr, states, dymix, cw_ssd, conv_ssd_b, lanes, name="ssd_bwd")
    G["dt_bias"], G["a_log"], G["d_skip"] = [_group_sum(dlanes[i:i + 1]) for i in range(3)]
    G["ssd_norm_w"] = dlanes[3:4]
    G["conv_ssd_w"], G["conv_ssd_b"] = dconv[0:4], dconv[4:5]
    g_wdt = _mm(h1, ddtr, trans_a=True, name="g_w_dt").reshape(D, 16, 64).sum(axis=2)
    G["w_in"] = jnp.concatenate([_mm(h1, dz, trans_a=True, name="g_w_z"),
                                 _mm(h1, dxbc, trans_a=True, name="g_w_xbc"), g_wdt,
                                 _mm(h1, dqkv, trans_a=True, name="g_w_qkv")], axis=1)
    dh1 = _mm(dz, W_z.T, name="d_h1_z")
    dh1 = _mm(dxbc, W_xbc.T, add=dh1, name="d_h1_xbc")
    dh1 = _mm(ddtr, W_dtr.T, add=dh1, name="d_h1_dt")
    dh1 = _mm(dqkv, W_qkv.T, add=dh1, name="d_h1_qkv")
    dx, G["norm_mix_w"] = _norm_bwd(xs, norm_mix_w, dh1, dx1, name="norm_mix_bwd")
    G["norm_final_w"] = g_nfinal.reshape(D)

    slabs = [
        _scatter_cols(G["w_in"], 706), G["w_out"].reshape(NDEV, 256, D),
        G["w_mq"].reshape(NDEV, 128, D), G["w_mk"].reshape(NDEV, 128, D), G["w_mv"].reshape(NDEV, 128, D),
        G["w_mo"].reshape(NDEV, 128, D), _scatter_cols(G["w_up"], 704), G["w_down"].reshape(NDEV, 352, D),
        _pad_rows(G["conv_ssd_w"].reshape(4, NDEV, 192).transpose(1, 0, 2).reshape(NDEV, 768), 1),
        _pad_rows(G["conv_ffn_w"].reshape(3, NDEV, 704).transpose(1, 0, 2).reshape(NDEV, 2112), 3)]
    slabs = [jnp.pad(t, ((0, 0), (0, (-t.shape[1]) % PACK_ALIGN), (0, 0))) for t in slabs]
    used = sum(t.shape[1] for t in slabs)
    slabs.append(jnp.zeros((NDEV, BIG_ROWS - used, D), F32))
    send = jnp.concatenate(slabs, axis=1).astype(BF16)
    recv = _exchange(send, name="exchange_grads")
    outs_big = _adamw(recv, w_big, _pack([M[n][0] for n, _ in BIG], BIG_ROWS),
                      _pack([V[n][0] for n, _ in BIG], BIG_ROWS), name="adamw_sharded")
    small_g = _pack([G[n] for n, _ in SMALL] + [loss_part], SMALL_ROWS)
    (parts_small,) = _all_gather([small_g], name="gather_small_grads")
    outs_small = _adamw(parts_small, _pack([P[n] for n, _ in SMALL], SMALL_ROWS),
                        _pack([M[n] for n, _ in SMALL], SMALL_ROWS),
                        _pack([V[n] for n, _ in SMALL], SMALL_ROWS), name="adamw_replicated")

    res = {}
    for kind, ob, osm in zip(("grad", "delta", "new_m", "new_v"), outs_big, outs_small):
        for (n, shp), val in zip(BIG, _unpack(ob, big_shapes)):
            res[kind, n] = val.reshape((1,) + shp)
        for (n, shp), val in zip(SMALL, _unpack(osm, small_shapes)):
            res[kind, n] = val
    loss = outs_small[0][LOSS_ROW, 0]
    out = [loss, dx.reshape(1, -1, D)]
    for kind in ("grad", "delta", "new_m", "new_v"):
        out += [res[kind, n] for n in ORDER]
    return tuple(out)
```

```python
import functools
import math

import jax
import jax.numpy as jnp
from jax import lax
from jax.experimental import pallas as pl
from jax.experimental.pallas import tpu as pltpu

F32 = jnp.float32
BF16 = jnp.bfloat16

D = 1024
NDEV = 8
EPS = 1e-6
SSD_CHUNK = 128
HALO = 8
VMEM_LIMIT = 56 * 2**20

ADAM_LR, ADAM_B1, ADAM_B2, ADAM_EPS, ADAM_WD, ADAM_STEP = 0.001, 0.9, 0.999, 1e-08, 0.01, 10


def _cp(*sem):
    return pltpu.CompilerParams(dimension_semantics=sem, vmem_limit_bytes=VMEM_LIMIT)


def _tile(n, cap, mult):
    if n <= cap:
        return n
    for d in range(cap - cap % mult, 0, -mult):
        if n % d == 0:
            return d
    raise ValueError(f"no tile for {n}")


def _sigmoid(x):
    return 1.0 / (1.0 + jnp.exp(-x))


def _silu(x):
    return x * _sigmoid(x)


def _softplus(x):
    return jnp.maximum(x, 0.0) + jnp.log1p(jnp.exp(-jnp.abs(x)))


def _terms(x, n):
    out = []
    r = x.astype(F32)
    for i in range(n):
        h = r.astype(BF16)
        out.append(h)
        if i + 1 < n:
            r = r - h.astype(F32)
    return out


_DIMS = {"nn": ((1,), (0,)), "nt": ((1,), (1,)), "tn": ((0,), (0,))}


def _dot_raw(form, a, b, ta, tb):
    acc = None
    for ai in _terms(a, ta):
        for bi in _terms(b, tb):
            d = lax.dot_general(ai, bi, (_DIMS[form], ((), ())), preferred_element_type=F32)
            acc = d if acc is None else acc + d
    return acc


@functools.lru_cache(maxsize=None)
def _dot_fn(form, ta, tb):
    @jax.custom_vjp
    def f(a, b):
        return _dot_raw(form, a, b, ta, tb)

    def fwd(a, b):
        return f(a, b), (a, b)

    def bwd(res, ct):
        a, b = res
        if form == "nn":
            return _dot_fn("nt", ta, tb)(ct, b), _dot_fn("tn", ta, tb)(a, ct)
        if form == "nt":
            return _dot_fn("nn", ta, tb)(ct, b), _dot_fn("tn", tb, ta)(ct, a)
        return _dot_fn("nt", tb, ta)(b, ct), _dot_fn("nn", ta, tb)(a, ct)

    f.defvjp(fwd, bwd)
    return f


def _dot(form, a, b, ta=1, tb=1):
    return _dot_fn(form, ta, tb)(a, b)


@functools.lru_cache(maxsize=None)
def _take_fn(axis, idx):
    @jax.custom_vjp
    def f(x):
        return x[:, idx:idx + 1] if axis == 1 else x[idx:idx + 1, :]

    def fwd(x):
        return f(x), x.shape

    def bwd(shape, ct):
        io = lax.broadcasted_iota(jnp.int32, shape, axis)
        return (jnp.where(io == idx, jnp.broadcast_to(ct, shape), 0.0),)

    f.defvjp(fwd, bwd)
    return f


@functools.lru_cache(maxsize=None)
def _split_fn(width, n):
    @jax.custom_vjp
    def f(x):
        return tuple(x[:, i * width:(i + 1) * width] for i in range(n))

    def fwd(x):
        return f(x), None

    def bwd(_, cts):
        return (jnp.concatenate(list(cts), axis=1),)

    f.defvjp(fwd, bwd)
    return f


def _split(x, width):
    return _split_fn(width, x.shape[1] // width)(x)


def _iota(shape, axis):
    return lax.broadcasted_iota(jnp.int32, shape, axis)


def _mm(a, b, *, name, add=None, trans_a=False, out_dtype=F32):
    if trans_a:
        kt, m = a.shape
    else:
        m, kt = a.shape
    kt2, n = b.shape
    assert kt == kt2, (a.shape, b.shape)
    tm = _tile(m, 512, 128 if trans_a else 8)
    tn = _tile(n, 1536, 128)
    tk = _tile(kt, 1024 if trans_a else 1536, 128)
    nk = kt // tk

    def body(*refs):
        if add is None:
            a_ref, b_ref, o_ref, acc = refs
        else:
            a_ref, b_ref, add_ref, o_ref, acc = refs
        k = pl.program_id(2)

        @pl.when(k == 0)
        def _():
            acc[...] = jnp.zeros_like(acc)

        av = a_ref[...].astype(BF16)
        bv = b_ref[...].astype(BF16)
        dims = _DIMS["tn" if trans_a else "nn"]
        acc[...] += lax.dot_general(av, bv, (dims, ((), ())), preferred_element_type=F32)

        @pl.when(k == nk - 1)
        def _():
            r = acc[...]
            if add is not None:
                r = r + add_ref[...]
            o_ref[...] = r.astype(out_dtype)

    a_spec = (pl.BlockSpec((tk, tm), lambda i, j, k: (k, i)) if trans_a
              else pl.BlockSpec((tm, tk), lambda i, j, k: (i, k)))
    in_specs = [a_spec, pl.BlockSpec((tk, tn), lambda i, j, k: (k, j))]
    args = [a, b]
    if add is not None:
        in_specs.append(pl.BlockSpec((tm, tn), lambda i, j, k: (i, j)))
        args.append(add)
    return pl.pallas_call(
        body, name=name, grid=(m // tm, n // tn, nk),
        in_specs=in_specs, out_specs=pl.BlockSpec((tm, tn), lambda i, j, k: (i, j)),
        out_shape=jax.ShapeDtypeStruct((m, n), out_dtype),
        scratch_shapes=[pltpu.VMEM((tm, tn), F32)],
        compiler_params=_cp("parallel", "parallel", "arbitrary"),
    )(*args)


def _rstd(x):
    return lax.rsqrt(jnp.mean(x * x, axis=-1, keepdims=True) + EPS)


def _norm_fwd(x, w, *, name):
    s = x.shape[0]
    tm = _tile(s, 512, 8)

    def body(x_ref, w_ref, o_ref):
        xv = x_ref[...]
        o_ref[...] = (xv * _rstd(xv) * w_ref[...]).astype(BF16)

    return pl.pallas_call(
        body, name=name, grid=(s // tm,),
        in_specs=[pl.BlockSpec((tm, D), lambda i: (i, 0)), pl.BlockSpec((1, D), lambda i: (0, 0))],
        out_specs=pl.BlockSpec((tm, D), lambda i: (i, 0)),
        out_shape=jax.ShapeDtypeStruct((s, D), BF16), compiler_params=_cp("parallel"),
    )(x, w)


def _norm_bwd_math(xv, wv, dy):
    r = _rstd(xv)
    xh = xv * r
    dxh = dy * wv
    dx = r * (dxh - xh * jnp.mean(dxh * xh, axis=-1, keepdims=True))
    dw = jnp.sum(dy * xh, axis=0, keepdims=True)
    return dx, dw


def _norm_bwd(x, w, dy, add, *, name):
    s = x.shape[0]
    tm = _tile(s, 256, 8)

    def body(*refs):
        if add is None:
            x_ref, w_ref, dy_ref, dx_ref, dw_ref = refs
        else:
            x_ref, w_ref, dy_ref, add_ref, dx_ref, dw_ref = refs

        @pl.when(pl.program_id(0) == 0)
        def _():
            dw_ref[...] = jnp.zeros_like(dw_ref)

        dx, dw = _norm_bwd_math(x_ref[...], w_ref[...], dy_ref[...])
        if add is not None:
            dx = dx + add_ref[...]
        dx_ref[...] = dx
        dw_ref[...] += dw

    row = pl.BlockSpec((tm, D), lambda i: (i, 0))
    vec = pl.BlockSpec((1, D), lambda i: (0, 0))
    in_specs = [row, vec, row] + ([row] if add is not None else [])
    args = [x, w, dy] + ([add] if add is not None else [])
    return pl.pallas_call(
        body, name=name, grid=(s // tm,), in_specs=in_specs, out_specs=[row, vec],
        out_shape=[jax.ShapeDtypeStruct((s, D), F32), jax.ShapeDtypeStruct((1, D), F32)],
        compiler_params=_cp("arbitrary"),
    )(*args)


def _final(x3, w, target, *, name):
    s = x3.shape[0]
    tm = _tile(s, 256, 8)

    def body(x_ref, w_ref, t_ref, dx_ref, dw_ref, loss_ref):
        @pl.when(pl.program_id(0) == 0)
        def _():
            dw_ref[...] = jnp.zeros_like(dw_ref)
            loss_ref[...] = jnp.zeros_like(loss_ref)

        xv = x_ref[...]
        wv = w_ref[...]
        y = xv * _rstd(xv) * wv
        err = y - t_ref[...]
        loss_ref[...] += 0.5 * jnp.sum(jnp.mean(err * err, axis=-1, keepdims=True))
        dx, dw = _norm_bwd_math(xv, wv, err * (1.0 / D))
        dx_ref[...] = dx
        dw_ref[...] += dw

    row = pl.BlockSpec((tm, D), lambda i: (i, 0))
    vec = pl.BlockSpec((1, D), lambda i: (0, 0))
    return pl.pallas_call(
        body, name=name, grid=(s // tm,), in_specs=[row, vec, row], out_specs=[row, vec, vec],
        out_shape=[jax.ShapeDtypeStruct((s, D), F32), jax.ShapeDtypeStruct((1, D), F32),
                   jax.ShapeDtypeStruct((1, D), F32)],
        compiler_params=_cp("arbitrary"),
    )(x3, w, target)


def _head_norm_math(o, w):
    lane = _iota((128, 128), 0) // 64
    bd = (lane == _iota((128, 128), 1) // 64).astype(F32)
    outs = []
    for op in _split(o, 128):
        ms = _dot("nn", op * op, bd, 2, 1) * (1.0 / 64)
        outs.append(op * lax.rsqrt(ms + EPS))
    return jnp.concatenate(outs, axis=1) * w


def _head_norm_fwd(o, w, *, name):
    s = o.shape[0]
    tm = _tile(s, 256, 8)

    def body(o_ref, w_ref, y_ref):
        y_ref[...] = _head_norm_math(o_ref[...], w_ref[...]).astype(BF16)

    row = pl.BlockSpec((tm, D), lambda i: (i, 0))
    vec = pl.BlockSpec((1, D), lambda i: (0, 0))
    return pl.pallas_call(
        body, name=name, grid=(s // tm,), in_specs=[row, vec], out_specs=row,
        out_shape=jax.ShapeDtypeStruct((s, D), BF16), compiler_params=_cp("parallel"),
    )(o, w)


def _head_norm_bwd(o, w, dymix, *, name):
    s = o.shape[0]
    tm = _tile(s, 256, 8)

    def body(o_ref, w_ref, dy_ref, do_ref, dw_ref):
        @pl.when(pl.program_id(0) == 0)
        def _():
            dw_ref[...] = jnp.zeros_like(dw_ref)

        _, vjp = jax.vjp(_head_norm_math, o_ref[...], w_ref[...])
        do, dw = vjp(dy_ref[...])
        do_ref[...] = do
        dw_ref[...] += dw

    row = pl.BlockSpec((tm, D), lambda i: (i, 0))
    vec = pl.BlockSpec((1, D), lambda i: (0, 0))
    return pl.pallas_call(
        body, name=name, grid=(s // tm,),
        in_specs=[row, vec, pl.BlockSpec((tm, D), lambda i: (i, 1))], out_specs=[row, vec],
        out_shape=[jax.ShapeDtypeStruct((s, D), F32), jax.ShapeDtypeStruct((1, D), F32)],
        compiler_params=_cp("arbitrary"),
    )(o, w, dymix)


SB_BQ = 256
SB_BK = 256


def _sb_consts():
    r = _iota((SB_BK, SB_BK), 0)
    c = _iota((SB_BK, SB_BK), 1)
    u_excl = (r > c).astype(BF16)
    u_incl = (r >= c).astype(BF16)
    return u_excl, u_incl


SB_LANES = 256
SB_NCH = SB_LANES // 64


def _nt(a, b):
    return lax.dot_general(a, b, (_DIMS["nt"], ((), ())), preferred_element_type=F32)


def _tn(a, b):
    return lax.dot_general(a, b, (_DIMS["tn"], ((), ())), preferred_element_type=F32)


def _nn(a, b):
    return jnp.dot(a, b, preferred_element_type=F32)


def _sb_heads(ref):
    out = []
    for hp in range(SB_LANES // 128):
        v = ref[:, 128 * hp:128 * (hp + 1)]
        first = _iota(v.shape, 1) < 64
        out += [jnp.where(first, v, 0).astype(BF16), jnp.where(first, 0, v).astype(BF16)]
    return out


SB_STRIP = 32


def _neg_abs(x):
    bits = lax.bitcast_convert_type(x, jnp.uint32) | jnp.uint32(0x80000000)
    return lax.bitcast_convert_type(bits, F32)


def _sb_block(ref, j):
    off = pl.multiple_of(j * SB_BK, SB_BK)
    return [ref[pl.ds(off, SB_BK), 128 * hp:128 * (hp + 1)] for hp in range(SB_NCH // 2)]


SB_DEAD = 104.0


def _sb_live(nlrun):
    m = nlrun[0]
    for x in nlrun[1:]:
        m = jnp.minimum(m, x)
    return jnp.min(m) < SB_DEAD


def _sb_strips():
    return [(r, pl.ds(r, SB_STRIP)) for r in range(0, SB_BQ, SB_STRIP)]


def _sb_diag_mask(r):
    return _iota((SB_STRIP, SB_BK), 1) < _iota((SB_STRIP, SB_BK), 0) + r


def _sb_soft(z, mask):
    e = jnp.exp(_neg_abs(z))
    nl = jnp.maximum(z, 0.0) + jnp.log(1.0 + e)
    if mask is not None:
        nl = jnp.where(mask, nl, 0.0)
    return e, nl


def _sb_split_to(hl_ref, rows, x):
    hi, lo = _terms(x, 2)
    hl_ref[rows, 0:SB_BK] = hi
    hl_ref[rows, SB_BK:2 * SB_BK] = lo


def _sb_stage_soft(z_ref, nl_ref, diag):
    for r, rows in _sb_strips():
        _, nl = _sb_soft(z_ref[rows, :], _sb_diag_mask(r) if diag else None)
        nl_ref[rows, 0:SB_BK] = nl.astype(BF16)


def _sb_stage_weights(z_ref, c_ref, a_ref, nlrun, diag):
    for r, rows in _sb_strips():
        a = jnp.exp(z_ref[rows, :] - c_ref[rows, :] - nlrun[r:r + SB_STRIP, :])
        if diag:
            a = jnp.where(_sb_diag_mask(r), a, 0.0)
        a_ref[rows, :] = a.astype(BF16)


def _sb_fwd(qkv, *, name):
    s = qkv.shape[0]
    nq = s // SB_BQ
    ng = D // SB_LANES
    assert SB_BQ == SB_BK

    def body(q_ref, k_ref, v_ref, o_ref, zbuf, nlbuf, cbuf, abuf):
        i = pl.program_id(1)
        _, u_incl = _sb_consts()
        lane_a = _iota((SB_BQ, 128), 1) < 64
        qh = [q * 0.125 for q in _sb_heads(q_ref)]

        def tile(j, accs, nlrun, diag):
            kbs = _sb_block(k_ref, j)
            for c in range(SB_NCH):
                zbuf[c] = _nt(qh[c], kbs[c // 2])
            for c in range(SB_NCH):
                _sb_stage_soft(zbuf.at[c], nlbuf.at[c], diag)
                cbuf[c] = _nn(nlbuf[c], u_incl)
            for c in range(SB_NCH):
                _sb_stage_weights(zbuf.at[c], cbuf.at[c], abuf.at[c], nlrun[c], diag)
            nlrun = tuple(nlrun[c] + cbuf[c, :, 0:1] for c in range(SB_NCH))
            vbs = _sb_block(v_ref, j)
            outs = [_nn(abuf[c], vbs[c // 2]) for c in range(SB_NCH)]
            accs = tuple(acc + jnp.where(lane_a, outs[2 * hp], outs[2 * hp + 1]) for hp, acc in enumerate(accs))
            return accs, nlrun

        accs, nlrun = tile(i, (jnp.zeros((SB_BQ, 128), F32),) * (SB_NCH // 2),
                           (jnp.zeros((SB_BQ, 1), F32),) * SB_NCH, True)

        def step(carry):
            j, _, accs, nlrun = carry
            accs, nlrun = tile(j, accs, nlrun, False)
            return j - 1, _sb_live(nlrun), accs, nlrun

        _, _, accs, _ = lax.while_loop(lambda c: (c[0] >= 0) & c[1], step, (i - 1, _sb_live(nlrun), accs, nlrun))
        o_ref[...] = jnp.concatenate(accs, axis=1)

    return pl.pallas_call(
        body, name=name, grid=(ng, nq),
        in_specs=[pl.BlockSpec((SB_BQ, SB_LANES), lambda g, i: (i, g)),
                  pl.BlockSpec((s, SB_LANES), lambda g, i: (0, ng + g)),
                  pl.BlockSpec((s, SB_LANES), lambda g, i: (0, 2 * ng + g))],
        out_specs=pl.BlockSpec((SB_BQ, SB_LANES), lambda g, i: (i, g)),
        out_shape=jax.ShapeDtypeStruct((s, D), F32),
        scratch_shapes=[pltpu.VMEM((SB_NCH, SB_BQ, SB_BK), F32), pltpu.VMEM((SB_NCH, SB_BQ, SB_BK), BF16),
                        pltpu.VMEM((SB_NCH, SB_BQ, SB_BK), F32), pltpu.VMEM((SB_NCH, SB_BQ, SB_BK), BF16)],
        compiler_params=_cp("parallel", "arbitrary"),
    )(qkv, qkv, qkv)


def _sb_bwd(qkv, o, do, *, name):
    s = qkv.shape[0]
    nq = s // SB_BQ
    ng = D // SB_LANES
    nhp = SB_NCH // 2

    def body(q_ref, k_ref, v_ref, o_ref, do_ref, dq_ref, dk_hbm, dv_hbm, dk_acc, dv_acc, sems,
             zbuf, gbuf, hl, cbuf, abuf, dzbuf):
        g_idx = pl.program_id(0)
        i = pl.program_id(1)

        @pl.when(i == 0)
        def _():
            dk_acc[...] = jnp.zeros_like(dk_acc)
            dv_acc[...] = jnp.zeros_like(dv_acc)

        _, u_incl = _sb_consts()
        u2 = jnp.concatenate([u_incl, u_incl], axis=0)
        lane_a = _iota((SB_BQ, 128), 1) < 64
        lane_k = _iota((SB_BK, 128), 1) < 64
        qh = [q * 0.125 for q in _sb_heads(q_ref)]
        qf = [q_ref[:, 128 * hp:128 * (hp + 1)] for hp in range(nhp)]
        doh = _sb_heads(do_ref)
        dof = [do_ref[:, 128 * hp:128 * (hp + 1)].astype(BF16) for hp in range(nhp)]
        delta = []
        for hp in range(nhp):
            prod = dof[hp].astype(F32) * o_ref[:, 128 * hp:128 * (hp + 1)]
            delta += [jnp.sum(jnp.where(lane_a, prod, 0.0), axis=1, keepdims=True),
                      jnp.sum(jnp.where(lane_a, 0.0, prod), axis=1, keepdims=True)]

        def pre(slot, j):
            kbs = _sb_block(k_ref, j)
            vbs = _sb_block(v_ref, j)
            for c in range(SB_NCH):
                zbuf[slot, c] = _nt(qh[c], kbs[c // 2])
                gbuf[slot, c] = _nt(doh[c], vbs[c // 2])

        def stage_g(c, slot):
            for _, rows in _sb_strips():
                g = abuf[slot, c, rows, :].astype(F32) * gbuf[slot, c, rows, :]
                gbuf[slot, c, rows, :] = g
                _sb_split_to(hl.at[c], rows, g)

        def stage_dz(c, slot, grun, diag):
            for r, rows in _sb_strips():
                z = zbuf[slot, c, rows, :]
                g = gbuf[slot, c, rows, :]
                cs = (delta[c] - grun)[r:r + SB_STRIP, :] - cbuf[c, rows, :]
                sig = 1.0 / (1.0 + jnp.exp(-z))
                dz = g - (g + cs) * sig
                if diag:
                    dz = jnp.where(_sb_diag_mask(r), dz, 0.0)
                dzbuf[slot, c, rows, :] = dz.astype(BF16)

        def chain(slot, nlrun, grun, diag):
            for c in range(SB_NCH):
                _sb_stage_soft(zbuf.at[slot, c], hl.at[c], diag)
                cbuf[c] = _nn(hl[c, :, 0:SB_BK], u_incl)
            nl_tot = []
            for c in range(SB_NCH):
                _sb_stage_weights(zbuf.at[slot, c], cbuf.at[c], abuf.at[slot, c], nlrun[c], diag)
                nl_tot.append(cbuf[c, :, 0:1])
                stage_g(c, slot)
                cbuf[c] = _nn(hl[c], u2)
            g_tot = []
            for c in range(SB_NCH):
                stage_dz(c, slot, grun[c], diag)
                g_tot.append(cbuf[c, :, 0:1])
            return (tuple(a + b for a, b in zip(nlrun, nl_tot)), tuple(a + b for a, b in zip(grun, g_tot)))

        def post(slot, j, dqs):
            off = pl.multiple_of(j * SB_BK, SB_BK)
            kbs = _sb_block(k_ref, j)
            dq_t = [_nn(dzbuf[slot, c], kbs[c // 2]) for c in range(SB_NCH)]
            dk_t = [_tn(dzbuf[slot, c], qf[c // 2]) for c in range(SB_NCH)]
            dv_t = [_tn(abuf[slot, c], dof[c // 2]) for c in range(SB_NCH)]
            for hp in range(nhp):
                cols = slice(128 * hp, 128 * (hp + 1))
                dk_acc[pl.ds(off, SB_BK), cols] += 0.125 * jnp.where(lane_k, dk_t[2 * hp], dk_t[2 * hp + 1])
                dv_acc[pl.ds(off, SB_BK), cols] += jnp.where(lane_k, dv_t[2 * hp], dv_t[2 * hp + 1])
            return tuple(dq + jnp.where(lane_a, dq_t[2 * hp], dq_t[2 * hp + 1]) for hp, dq in enumerate(dqs))

        def tile(j, dqs, nlrun, grun, diag):
            pre(0, j)
            nlrun, grun = chain(0, nlrun, grun, diag)
            return post(0, j, dqs), nlrun, grun

        zero = (jnp.zeros((SB_BQ, 1), F32),) * SB_NCH
        dqs, nlrun, grun = tile(i, (jnp.zeros((SB_BQ, 128), F32),) * nhp, zero, zero, True)

        def step(carry):
            j, _, dqs, nlrun, grun = carry
            dqs, nlrun, grun = tile(j, dqs, nlrun, grun, False)
            return j - 1, _sb_live(nlrun), dqs, nlrun, grun

        carry = lax.while_loop(lambda c: (c[0] >= 0) & c[1], step, (i - 1, _sb_live(nlrun), dqs, nlrun, grun))
        dq_ref[...] = 0.125 * jnp.concatenate(carry[2], axis=1)

        @pl.when(i == nq - 1)
        def _():
            cols = pl.ds(pl.multiple_of(g_idx * SB_LANES, SB_LANES), SB_LANES)
            ck = pltpu.make_async_copy(dk_acc, dk_hbm.at[:, cols], sems.at[0])
            cv = pltpu.make_async_copy(dv_acc, dv_hbm.at[:, cols], sems.at[1])
            ck.start()
            cv.start()
            ck.wait()
            cv.wait()

    qblk = pl.BlockSpec((SB_BQ, SB_LANES), lambda g, i: (i, g))
    hbm = pl.BlockSpec(memory_space=pl.ANY)
    return pl.pallas_call(
        body, name=name, grid=(ng, nq),
        in_specs=[qblk, pl.BlockSpec((s, SB_LANES), lambda g, i: (0, ng + g)),
                  pl.BlockSpec((s, SB_LANES), lambda g, i: (0, 2 * ng + g)), qblk, qblk],
        out_specs=[qblk, hbm, hbm],
        out_shape=[jax.ShapeDtypeStruct((s, D), F32)] * 3,
        scratch_shapes=[pltpu.VMEM((s, SB_LANES), F32), pltpu.VMEM((s, SB_LANES), F32),
                        pltpu.SemaphoreType.DMA((2,)),
                        pltpu.VMEM((1, SB_NCH, SB_BQ, SB_BK), F32), pltpu.VMEM((1, SB_NCH, SB_BQ, SB_BK), F32),
                        pltpu.VMEM((SB_NCH, SB_BQ, 2 * SB_BK), BF16), pltpu.VMEM((SB_NCH, SB_BQ, SB_BK), F32),
                        pltpu.VMEM((1, SB_NCH, SB_BQ, SB_BK), BF16), pltpu.VMEM((1, SB_NCH, SB_BQ, SB_BK), BF16)],
        compiler_params=_cp("arbitrary", "arbitrary"),
    )(qkv, qkv, qkv, o, do)


def _ssd_core(z, xpre, dtr, state, dtb, alog, dsk, nw):
    L = SSD_CHUNK
    xa = _silu(xpre)
    pieces = _split(xa, 128)
    xs = jnp.concatenate(pieces[:8], axis=1)
    bm, cm = pieces[8:10], pieces[10:12]
    dt = _softplus(dtr + dtb)
    a = dt * (-jnp.exp(alog))
    tri = (_iota((L, L), 0) >= _iota((L, L), 1)).astype(F32)
    a_cs = _dot("nn", tri, a, 1, 3)
    xc = xs * dt
    tril = _iota((L, L), 0) >= _iota((L, L), 1)
    lane_a = _iota((L, 128), 1) < 64
    acs_p = _split(a_cs, 128)
    xc_p = _split(xc, 128)
    ys, new_states = [], []
    for g in range(2):
        cb = _dot("nt", cm[g], bm[g])
        for pp in range(4):
            pair = 4 * g + pp
            acs = acs_p[pair]
            acs_t = acs.T
            xcp = xc_p[pair]
            st = state[pair]
            heads = []
            for hh in range(2):
                col = _take_fn(1, 64 * hh)(acs)
                row = _take_fn(0, 64 * hh)(acs_t)
                seg = col - row
                lm = jnp.where(tril, jnp.exp(jnp.where(tril, seg, 0.0)), 0.0)
                heads.append(_dot("nn", cb * lm, xcp))
            ydiag = jnp.where(lane_a, heads[0], heads[1])
            last = _take_fn(0, L - 1)(acs)
            snew = _dot("tn", xcp * jnp.exp(last - acs), bm[g])
            new_states.append(st * jnp.exp(_take_fn(1, L - 1)(acs_t)) + snew)
            yoff = _dot("nt", cm[g], st) * jnp.exp(acs)
            ys.append(ydiag + yoff)
    y = jnp.concatenate(ys, axis=1) + xs * dsk
    yg = y * _silu(z)
    outs = []
    for v in _split(yg, 512):
        outs.append(v * lax.rsqrt(jnp.mean(v * v, axis=-1, keepdims=True) + EPS))
    return jnp.concatenate(outs, axis=1) * nw, tuple(new_states)


XBC = 1536


def _ssd_conv(ext_ref, cw, cb):
    acc = cb
    for k in range(4):
        acc = acc + cw[k:k + 1, :] * ext_ref[pl.ds(HALO - 3 + k, SSD_CHUNK), :]
    return acc


def _ssd_fwd(z, xbc, dtr, cw, cb, lanes, *, name):
    s = z.shape[0]
    L = SSD_CHUNK
    nc = s // L

    def body(z_ref, x_ref, h_ref, dtr_ref, cw_ref, cb_ref, ln_ref, y_ref, st_ref, state, ext):
        c = pl.program_id(0)

        @pl.when(c == 0)
        def _():
            state[...] = jnp.zeros_like(state)

        ext[0:HALO, :] = jnp.where(c == 0, 0.0, h_ref[...])
        ext[HALO:, :] = x_ref[...]
        xpre = _ssd_conv(ext, cw_ref[...], cb_ref[...])
        st_ref[0] = state[...]
        st_in = tuple(state[p] for p in range(8))
        yn, st_out = _ssd_core(z_ref[...], xpre, dtr_ref[...], st_in,
                               ln_ref[0:1, :], ln_ref[1:2, :], ln_ref[2:3, :], ln_ref[3:4, :])
        y_ref[...] = yn.astype(BF16)
        for p in range(8):
            state[p] = st_out[p]

    return pl.pallas_call(
        body, name=name, grid=(nc,),
        in_specs=[pl.BlockSpec((L, D), lambda c: (c, 0)),
                  pl.BlockSpec((L, XBC), lambda c: (c, 0)),
                  pl.BlockSpec((HALO, XBC), lambda c: (jnp.maximum(c * (L // HALO) - 1, 0), 0)),
                  pl.BlockSpec((L, D), lambda c: (c, 0)),
                  pl.BlockSpec((4, XBC), lambda c: (0, 0)),
                  pl.BlockSpec((1, XBC), lambda c: (0, 0)),
                  pl.BlockSpec((8, D), lambda c: (0, 0))],
        out_specs=[pl.BlockSpec((L, D), lambda c: (c, 0)),
                   pl.BlockSpec((1, 8, 128, 128), lambda c: (c, 0, 0, 0))],
        out_shape=[jax.ShapeDtypeStruct((s, D), BF16), jax.ShapeDtypeStruct((nc, 8, 128, 128), F32)],
        scratch_shapes=[pltpu.VMEM((8, 128, 128), F32), pltpu.VMEM((L + HALO, XBC), F32)],
        compiler_params=_cp("arbitrary"),
    )(z, xbc, xbc, dtr, cw, cb, lanes)


def _ssd_bwd(z, xbc, dtr, states, dymix, cw, cb, lanes, *, name):
    s = z.shape[0]
    L = SSD_CHUNK
    nc = s // L

    def body(z_ref, x_ref, h_ref, dtr_ref, st_ref, dy_ref, cw_ref, cb_ref, ln_ref,
             dz_ref, dx_ref, ddt_ref, dln_ref, dcv_ref, dstate, ext, dext):
        i = pl.program_id(0)
        c = nc - 1 - i

        @pl.when(i == 0)
        def _():
            dstate[...] = jnp.zeros_like(dstate)
            dext[...] = jnp.zeros_like(dext)
            dln_ref[...] = jnp.zeros_like(dln_ref)
            dcv_ref[...] = jnp.zeros_like(dcv_ref)

        ext[0:HALO, :] = jnp.where(c == 0, 0.0, h_ref[...])
        ext[HALO:, :] = x_ref[...]
        cwv = cw_ref[...]
        xpre = _ssd_conv(ext, cwv, cb_ref[...])
        st_in = tuple(st_ref[0, p] for p in range(8))
        _, vjp = jax.vjp(_ssd_core, z_ref[...], xpre, dtr_ref[...], st_in,
                         ln_ref[0:1, :], ln_ref[1:2, :], ln_ref[2:3, :], ln_ref[3:4, :])
        dz, dxpre, ddtr, dst, d0, d1, d2, d3 = vjp((dy_ref[...], tuple(dstate[p] for p in range(8))))
        for p in range(8):
            dstate[p] = dst[p]
        dz_ref[...] = dz.astype(BF16)
        ddt_ref[...] = ddtr.astype(BF16)
        dln_ref[0:4, :] += jnp.concatenate([d0, d1, d2, d3], axis=0)
        dext[0:L, :] = dxpre
        dx = jnp.zeros((L, XBC), F32)
        rows = []
        for k in range(4):
            dx = dx + cwv[k:k + 1, :] * dext[pl.ds(3 - k, L), :]
            rows.append(jnp.sum(dxpre * ext[pl.ds(HALO - 3 + k, L), :], axis=0, keepdims=True))
        rows.append(jnp.sum(dxpre, axis=0, keepdims=True))
        dx_ref[...] = dx.astype(BF16)
        dcv_ref[0:5, :] += jnp.concatenate(rows, axis=0)
        dext[L:L + HALO, :] = dxpre[0:HALO, :]

    rev = lambda i: (nc - 1 - i, 0)
    return pl.pallas_call(
        body, name=name, grid=(nc,),
        in_specs=[pl.BlockSpec((L, D), rev),
                  pl.BlockSpec((L, XBC), rev),
                  pl.BlockSpec((HALO, XBC), lambda i: (jnp.maximum((nc - 1 - i) * (L // HALO) - 1, 0), 0)),
                  pl.BlockSpec((L, D), rev),
                  pl.BlockSpec((1, 8, 128, 128), lambda i: (nc - 1 - i, 0, 0, 0)),
                  pl.BlockSpec((L, D), rev),
                  pl.BlockSpec((4, XBC), lambda i: (0, 0)),
                  pl.BlockSpec((1, XBC), lambda i: (0, 0)),
                  pl.BlockSpec((8, D), lambda i: (0, 0))],
        out_specs=[pl.BlockSpec((L, D), rev), pl.BlockSpec((L, XBC), rev), pl.BlockSpec((L, D), rev),
                   pl.BlockSpec((8, D), lambda i: (0, 0)), pl.BlockSpec((8, XBC), lambda i: (0, 0))],
        out_shape=[jax.ShapeDtypeStruct((s, D), BF16), jax.ShapeDtypeStruct((s, XBC), BF16),
                   jax.ShapeDtypeStruct((s, D), BF16), jax.ShapeDtypeStruct((8, D), F32),
                   jax.ShapeDtypeStruct((8, XBC), F32)],
        scratch_shapes=[pltpu.VMEM((8, 128, 128), F32), pltpu.VMEM((L + HALO, XBC), F32),
                        pltpu.VMEM((L + HALO, XBC), F32)],
        compiler_params=_cp("arbitrary"),
    )(z, xbc, xbc, dtr, states, dymix, cw, cb, lanes)


def _mem_attn_math(q, k, v):
    outs = []
    for qh, kh, vh in zip(_split(q, 256), _split(k, 256), _split(v, 256)):
        sc = _dot("nt", qh, kh) * (1.0 / 16.0)
        e = jnp.exp(sc - lax.stop_gradient(jnp.max(sc, axis=-1, keepdims=True)))
        p = e / jnp.sum(e, axis=-1, keepdims=True)
        outs.append(_dot("nn", p, vh))
    return jnp.concatenate(outs, axis=1)


def _mem_attn_fwd(q, k, v, *, name):
    s, m = q.shape[0], k.shape[0]
    tm = _tile(s, 256, 8)

    def body(q_ref, k_ref, v_ref, o_ref):
        o_ref[...] = _mem_attn_math(q_ref[...].astype(F32), k_ref[...].astype(F32),
                                    v_ref[...].astype(F32)).astype(BF16)

    row = pl.BlockSpec((tm, D), lambda i: (i, 0))
    kv = pl.BlockSpec((m, D), lambda i: (0, 0))
    return pl.pallas_call(
        body, name=name, grid=(s // tm,), in_specs=[row, kv, kv], out_specs=row,
        out_shape=jax.ShapeDtypeStruct((s, D), BF16), compiler_params=_cp("parallel"),
    )(q, k, v)


def _mem_attn_bwd(q, k, v, do, *, name):
    s, m = q.shape[0], k.shape[0]
    tm = _tile(s, 256, 8)

    def body(q_ref, k_ref, v_ref, do_ref, dq_ref, dk_ref, dv_ref):
        @pl.when(pl.program_id(0) == 0)
        def _():
            dk_ref[...] = jnp.zeros_like(dk_ref)
            dv_ref[...] = jnp.zeros_like(dv_ref)

        _, vjp = jax.vjp(_mem_attn_math, q_ref[...].astype(F32), k_ref[...].astype(F32),
                         v_ref[...].astype(F32))
        dq, dk, dv = vjp(do_ref[...])
        dq_ref[...] = dq.astype(BF16)
        dk_ref[...] += dk
        dv_ref[...] += dv

    row = pl.BlockSpec((tm, D), lambda i: (i, 0))
    kv = pl.BlockSpec((m, D), lambda i: (0, 0))
    return pl.pallas_call(
        body, name=name, grid=(s // tm,), in_specs=[row, kv, kv, row], out_specs=[row, kv, kv],
        out_shape=[jax.ShapeDtypeStruct((s, D), BF16), jax.ShapeDtypeStruct((m, D), F32),
                   jax.ShapeDtypeStruct((m, D), F32)],
        compiler_params=_cp("arbitrary"),
    )(q, k, v, do)


DFF = 2816
FFN_TC = 1408
FFN_TM = 256


def _ffn_conv(ext_ref, cw, cb, tm):
    acc = cb
    for k in range(3):
        acc = acc + cw[k:k + 1, :] * ext_ref[pl.ds(HALO - 2 + k, tm), :]
    return acc


def _ffn_specs(s):
    tm, tc = FFN_TM, FFN_TC
    blk = pl.BlockSpec((tm, tc), lambda i, j: (i, j))
    halo = pl.BlockSpec((HALO, tc), lambda i, j: (jnp.maximum(i * (tm // HALO) - 1, 0), j))
    cw = pl.BlockSpec((3, tc), lambda i, j: (0, j))
    cb = pl.BlockSpec((1, tc), lambda i, j: (0, j))
    return tm, tc, blk, halo, cw, cb


def _glu_fwd(ug, uv, cwg, cwv, cbg, cbv, *, name):
    s = ug.shape[0]
    tm, tc, blk, halo, cw, cb = _ffn_specs(s)

    def body(g_ref, gh_ref, v_ref, vh_ref, cwg_ref, cwv_ref, cbg_ref, cbv_ref, f_ref, eg, ev):
        first = pl.program_id(0) == 0
        eg[0:HALO, :] = jnp.where(first, 0.0, gh_ref[...])
        eg[HALO:, :] = g_ref[...]
        ev[0:HALO, :] = jnp.where(first, 0.0, vh_ref[...])
        ev[HALO:, :] = v_ref[...]
        g = _ffn_conv(eg, cwg_ref[...], cbg_ref[...], tm)
        v = _ffn_conv(ev, cwv_ref[...], cbv_ref[...], tm)
        f_ref[...] = (_silu(g) * v).astype(BF16)

    return pl.pallas_call(
        body, name=name, grid=(s // tm, DFF // tc),
        in_specs=[blk, halo, blk, halo, cw, cw, cb, cb], out_specs=blk,
        out_shape=jax.ShapeDtypeStruct((s, DFF), BF16),
        scratch_shapes=[pltpu.VMEM((tm + HALO, tc), F32)] * 2,
        compiler_params=_cp("parallel", "parallel"),
    )(ug, ug, uv, uv, cwg, cwv, cbg, cbv)


def _glu_bwd(ug, uv, df, cwg, cwv, cbg, cbv, *, name):
    s = ug.shape[0]
    tm, tc, blk, halo, cw, cb = _ffn_specs(s)

    def body(g_ref, gh_ref, v_ref, vh_ref, df_ref, cwg_ref, cwv_ref, cbg_ref, cbv_ref, dg_ref, dv_ref, eg, ev):
        first = pl.program_id(0) == 0
        eg[0:HALO, :] = jnp.where(first, 0.0, gh_ref[...])
        eg[HALO:, :] = g_ref[...]
        ev[0:HALO, :] = jnp.where(first, 0.0, vh_ref[...])
        ev[HALO:, :] = v_ref[...]
        g = _ffn_conv(eg, cwg_ref[...], cbg_ref[...], tm)
        v = _ffn_conv(ev, cwv_ref[...], cbv_ref[...], tm)
        dfv = df_ref[...]
        sg = _sigmoid(g)
        dv_ref[...] = dfv * g * sg
        dg_ref[...] = dfv * v * sg * (1.0 + g * (1.0 - sg))

    return pl.pallas_call(
        body, name=name, grid=(s // tm, DFF // tc),
        in_specs=[blk, halo, blk, halo, blk, cw, cw, cb, cb], out_specs=[blk, blk],
        out_shape=[jax.ShapeDtypeStruct((s, DFF), F32)] * 2,
        scratch_shapes=[pltpu.VMEM((tm + HALO, tc), F32)] * 2,
        compiler_params=_cp("parallel", "parallel"),
    )(ug, ug, uv, uv, df, cwg, cwv, cbg, cbv)


def _ffn_conv_bwd(du, u, cwt, *, name):
    s = du.shape[0]
    tm, tc = FFN_TM, FFN_TC
    nb = s // tm

    def body(du_ref, duh_ref, u_ref, uh_ref, cw_ref, dx_ref, dc_ref, edu, eu):
        i = pl.program_id(1)

        @pl.when(i == 0)
        def _():
            dc_ref[...] = jnp.zeros_like(dc_ref)

        duv = du_ref[...]
        edu[0:tm, :] = duv
        edu[tm:, :] = jnp.where(i == nb - 1, 0.0, duh_ref[...])
        eu[0:HALO, :] = jnp.where(i == 0, 0.0, uh_ref[...])
        eu[HALO:, :] = u_ref[...]
        cwv = cw_ref[...]
        dx = jnp.zeros((tm, tc), F32)
        rows = []
        for k in range(3):
            dx = dx + cwv[k:k + 1, :] * edu[pl.ds(2 - k, tm), :]
            rows.append(jnp.sum(duv * eu[pl.ds(HALO - 2 + k, tm), :], axis=0, keepdims=True))
        rows.append(jnp.sum(duv, axis=0, keepdims=True))
        dx_ref[...] = dx.astype(BF16)
        dc_ref[0:4, :] += jnp.concatenate(rows, axis=0)

    blk = pl.BlockSpec((tm, tc), lambda j, i: (i, j))
    nxt = pl.BlockSpec((HALO, tc), lambda j, i: (jnp.minimum((i + 1) * (tm // HALO), s // HALO - 1), j))
    prv = pl.BlockSpec((HALO, tc), lambda j, i: (jnp.maximum(i * (tm // HALO) - 1, 0), j))
    return pl.pallas_call(
        body, name=name, grid=(DFF // tc, nb),
        in_specs=[blk, nxt, blk, prv, pl.BlockSpec((3, tc), lambda j, i: (0, j))],
        out_specs=[blk, pl.BlockSpec((8, tc), lambda j, i: (0, j))],
        out_shape=[jax.ShapeDtypeStruct((s, DFF), BF16), jax.ShapeDtypeStruct((8, DFF), F32)],
        scratch_shapes=[pltpu.VMEM((tm + HALO, tc), F32)] * 2,
        compiler_params=_cp("parallel", "arbitrary"),
    )(du, du, u, u, cwt)


MESH = pl.DeviceIdType.MESH


def _all_gather(arrs, *, name):
    n = len(arrs)

    def body(*refs):
        x_refs, out_refs = refs[:n], refs[n:2 * n]
        send_sems, recv_sems, local_sems = refs[2 * n:]
        x, y, c = lax.axis_index("x"), lax.axis_index("y"), lax.axis_index("c")
        me, sibling = (x, y, c), (x, y, 1 - c)
        chips = [(1 - x, y), (x, 1 - y), (1 - x, 1 - y)]

        def blk(a, dev):
            return out_refs[a].at[4 * dev[0] + 2 * dev[1] + dev[2]]

        def copy(a, k, block, to, src=None):
            return pltpu.make_async_remote_copy(
                src_ref=blk(a, block) if src is None else src, dst_ref=blk(a, block),
                send_sem=send_sems.at[7 * a + k], recv_sem=recv_sems.at[7 * a + k],
                device_id=to, device_id_type=MESH)

        started = []
        mine = []
        for a in range(n):
            cp = pltpu.make_async_copy(x_refs[a], blk(a, me), local_sems.at[a])
            cp.start()
            mine.append(cp)
            first = [copy(a, 0, me, sibling, src=x_refs[a])]
            first += [copy(a, 1 + j, me, (*chip, c), src=x_refs[a]) for j, chip in enumerate(chips)]
            for cp in first:
                cp.start()
            started += first
        for a in range(n):
            for j, chip in enumerate(chips):
                copy(a, 1 + j, (*chip, c), me).wait_recv()
                fwd = copy(a, 4 + j, (*chip, c), sibling)
                fwd.start()
                started.append(fwd)
        for a in range(n):
            copy(a, 0, sibling, me).wait_recv()
            for j, chip in enumerate(chips):
                copy(a, 4 + j, (*chip, 1 - c), me).wait_recv()
        for cp in started:
            cp.wait_send()
        for cp in mine:
            cp.wait()

    any_spec = pl.BlockSpec(memory_space=pl.ANY)
    return pl.pallas_call(
        body, name=name,
        in_specs=[any_spec] * n, out_specs=[any_spec] * n,
        out_shape=[jax.ShapeDtypeStruct((NDEV,) + a.shape, a.dtype) for a in arrs],
        scratch_shapes=[pltpu.SemaphoreType.DMA((7 * n,)), pltpu.SemaphoreType.DMA((7 * n,)),
                        pltpu.SemaphoreType.DMA((n,))],
    )(*arrs)


def _exchange(send, *, name):
    def body(send_ref, recv_ref, send_sems, recv_sems, local_sem):
        x, y, c = lax.axis_index("x"), lax.axis_index("y"), lax.axis_index("c")
        me = 4 * x + 2 * y + c
        mine = pltpu.make_async_copy(send_ref.at[me], recv_ref.at[me], local_sem)
        mine.start()
        copies = []
        for k in range(1, NDEV):
            px = 1 - x if k & 4 else x
            py = 1 - y if k & 2 else y
            pc = 1 - c if k & 1 else c
            cp = pltpu.make_async_remote_copy(
                src_ref=send_ref.at[4 * px + 2 * py + pc], dst_ref=recv_ref.at[me],
                send_sem=send_sems.at[k - 1], recv_sem=recv_sems.at[k - 1],
                device_id=(px, py, pc), device_id_type=MESH)
            cp.start()
            copies.append(cp)
        for cp in copies:
            cp.wait_recv()
        for cp in copies:
            cp.wait_send()
        mine.wait()

    any_spec = pl.BlockSpec(memory_space=pl.ANY)
    return pl.pallas_call(
        body, name=name, in_specs=[any_spec], out_specs=any_spec,
        out_shape=jax.ShapeDtypeStruct(send.shape, send.dtype),
        scratch_shapes=[pltpu.SemaphoreType.DMA((7,)), pltpu.SemaphoreType.DMA((7,)),
                        pltpu.SemaphoreType.DMA],
    )(send)


def _adamw(parts, w, m, v, *, name):
    r = w.shape[0]
    tm = _tile(r, 256, PACK_ALIGN)
    c1 = 1.0 - ADAM_B1 ** ADAM_STEP
    c2 = 1.0 - ADAM_B2 ** ADAM_STEP

    def body(p_ref, w_ref, m_ref, v_ref, g_ref, d_ref, nm_ref, nv_ref):
        g = p_ref[0].astype(F32)
        for i in range(1, NDEV):
            g = g + p_ref[i].astype(F32)
        nm = ADAM_B1 * m_ref[...] + (1.0 - ADAM_B1) * g
        nv = ADAM_B2 * v_ref[...] + (1.0 - ADAM_B2) * (g * g)
        d_ref[...] = -ADAM_LR * ((nm / c1) / (jnp.sqrt(nv / c2) + ADAM_EPS) + ADAM_WD * w_ref[...])
        g_ref[...] = g
        nm_ref[...] = nm
        nv_ref[...] = nv

    row = pl.BlockSpec((tm, D), lambda i: (i, 0))
    return pl.pallas_call(
        body, name=name, grid=(r // tm,),
        in_specs=[pl.BlockSpec((NDEV, tm, D), lambda i: (0, i, 0)), row, row, row],
        out_specs=[row] * 4, out_shape=[jax.ShapeDtypeStruct((r, D), F32)] * 4,
        compiler_params=_cp("parallel"),
    )(parts, w, m, v)


PACK_ALIGN = 16


def _part_rows(shape):
    n = -(-math.prod(shape) // D)
    return n + (-n) % PACK_ALIGN


def _rows(a):
    flat = a.reshape(-1)
    pad = _part_rows(a.shape) * D - flat.shape[0]
    if pad:
        flat = jnp.concatenate([flat, jnp.zeros((pad,), flat.dtype)])
    return flat.reshape(-1, D)


def _pack(parts, total_rows):
    rows = [_rows(p) for p in parts]
    used = sum(r.shape[0] for r in rows)
    if total_rows > used:
        rows.append(jnp.zeros((total_rows - used, D), rows[0].dtype))
    return jnp.concatenate(rows, axis=0)


def _unpack(buf, shapes):
    out, r0 = [], 0
    for shp in shapes:
        n = math.prod(shp)
        out.append(buf[r0:r0 + _part_rows(shp)].reshape(-1)[:n].reshape(shp))
        r0 += _part_rows(shp)
    return out


BIG = [("w_in", (D, 706)), ("w_out", (256, D)), ("w_mq", (128, D)), ("w_mk", (128, D)), ("w_mv", (128, D)),
       ("w_mo", (128, D)), ("w_up", (D, 704)), ("w_down", (352, D)), ("conv_ssd_w", (4, 192)),
       ("conv_ffn_w", (3, 704))]
BIG_ROW0 = {}
_r = 0
for _n, _shp in BIG:
    BIG_ROW0[_n] = _r
    _r += _part_rows(_shp)
BIG_ROWS = _r + (-_r) % 128
SMALL = [("norm_mix_w", (1, D)), ("conv_ssd_b", (1, 1536)), ("dt_bias", (1, 16)), ("a_log", (1, 16)),
         ("d_skip", (1, 16)), ("ssd_norm_w", (1, D)), ("sb_norm_w", (1, D)), ("norm_mem_w", (1, D)),
         ("norm_memkv_w", (1, D)), ("norm_ffn_w", (1, D)), ("conv_ffn_b", (1, 5632)), ("norm_final_w", (D,))]
LOSS_ROW = sum(_part_rows(_shp) for _, _shp in SMALL)
SMALL_ROWS = LOSS_ROW + PACK_ALIGN
ORDER = ["norm_mix_w", "w_in", "conv_ssd_w", "conv_ssd_b", "dt_bias", "a_log", "d_skip", "ssd_norm_w",
         "sb_norm_w", "w_out", "norm_mem_w", "norm_memkv_w", "w_mq", "w_mk", "w_mv", "w_mo", "norm_ffn_w",
         "w_up", "conv_ffn_w", "conv_ffn_b", "w_down", "norm_final_w"]


def _gather_cols(g, r0, nr, rows, cols):
    t = g[:, r0:r0 + nr].reshape(NDEV, rows, cols)
    return t.transpose(1, 0, 2).reshape(rows, NDEV * cols)


def _scatter_cols(full, cols):
    rows = full.shape[0]
    return full.reshape(rows, NDEV, cols).transpose(1, 0, 2).reshape(NDEV, -1, D)


def _pad_rows(a, nr):
    n = a.shape[1]
    return jnp.concatenate([a, jnp.zeros((NDEV, nr * D - n), a.dtype)], axis=1).reshape(NDEV, nr, D)


def _group_sum(lanes):
    return lanes.reshape(16, 64).sum(axis=1).reshape(1, 16)


def kernel(x, mem, norm_mix_w, w_in, conv_ssd_w, conv_ssd_b, dt_bias, a_log, d_skip, ssd_norm_w, sb_norm_w, w_out, norm_mem_w, norm_memkv_w, w_mq, w_mk, w_mv, w_mo, norm_ffn_w, w_up, conv_ffn_w, conv_ffn_b, w_down, norm_final_w, loss_target, m_norm_mix_w, m_w_in, m_conv_ssd_w, m_conv_ssd_b, m_dt_bias, m_a_log, m_d_skip, m_ssd_norm_w, m_sb_norm_w, m_w_out, m_norm_mem_w, m_norm_memkv_w, m_w_mq, m_w_mk, m_w_mv, m_w_mo, m_norm_ffn_w, m_w_up, m_conv_ffn_w, m_conv_ffn_b, m_w_down, m_norm_final_w, v_norm_mix_w, v_w_in, v_conv_ssd_w, v_conv_ssd_b, v_dt_bias, v_a_log, v_d_skip, v_ssd_norm_w, v_sb_norm_w, v_w_out, v_norm_mem_w, v_norm_memkv_w, v_w_mq, v_w_mk, v_w_mv, v_w_mo, v_norm_ffn_w, v_w_up, v_conv_ffn_w, v_conv_ffn_b, v_w_down, v_norm_final_w):
    P = dict(norm_mix_w=norm_mix_w, w_in=w_in, conv_ssd_w=conv_ssd_w, conv_ssd_b=conv_ssd_b, dt_bias=dt_bias, a_log=a_log, d_skip=d_skip, ssd_norm_w=ssd_norm_w, sb_norm_w=sb_norm_w, w_out=w_out, norm_mem_w=norm_mem_w, norm_memkv_w=norm_memkv_w, w_mq=w_mq, w_mk=w_mk, w_mv=w_mv, w_mo=w_mo, norm_ffn_w=norm_ffn_w, w_up=w_up, conv_ffn_w=conv_ffn_w, conv_ffn_b=conv_ffn_b, w_down=w_down, norm_final_w=norm_final_w)
    M = dict(norm_mix_w=m_norm_mix_w, w_in=m_w_in, conv_ssd_w=m_conv_ssd_w, conv_ssd_b=m_conv_ssd_b, dt_bias=m_dt_bias, a_log=m_a_log, d_skip=m_d_skip, ssd_norm_w=m_ssd_norm_w, sb_norm_w=m_sb_norm_w, w_out=m_w_out, norm_mem_w=m_norm_mem_w, norm_memkv_w=m_norm_memkv_w, w_mq=m_w_mq, w_mk=m_w_mk, w_mv=m_w_mv, w_mo=m_w_mo, norm_ffn_w=m_norm_ffn_w, w_up=m_w_up, conv_ffn_w=m_conv_ffn_w, conv_ffn_b=m_conv_ffn_b, w_down=m_w_down, norm_final_w=m_norm_final_w)
    V = dict(norm_mix_w=v_norm_mix_w, w_in=v_w_in, conv_ssd_w=v_conv_ssd_w, conv_ssd_b=v_conv_ssd_b, dt_bias=v_dt_bias, a_log=v_a_log, d_skip=v_d_skip, ssd_norm_w=v_ssd_norm_w, sb_norm_w=v_sb_norm_w, w_out=v_w_out, norm_mem_w=v_norm_mem_w, norm_memkv_w=v_norm_memkv_w, w_mq=v_w_mq, w_mk=v_w_mk, w_mv=v_w_mv, w_mo=v_w_mo, norm_ffn_w=v_norm_ffn_w, w_up=v_w_up, conv_ffn_w=v_conv_ffn_w, conv_ffn_b=v_conv_ffn_b, w_down=v_w_down, norm_final_w=v_norm_final_w)
    big_shapes = [shp for _, shp in BIG]
    small_shapes = [shp for _, shp in SMALL]

    w_big = _pack([P[n][0] for n, _ in BIG], BIG_ROWS)
    c0 = BIG_ROW0["conv_ssd_w"]
    g16, g32 = _all_gather([w_big.astype(BF16), w_big[c0:c0 + 2 * PACK_ALIGN]], name="gather_weights")
    R0 = BIG_ROW0
    W_in = _gather_cols(g16, R0["w_in"], 706, D, 706)
    W_out = g16[:, R0["w_out"]:R0["w_out"] + 256].reshape(2 * D, D)
    W_mq, W_mk, W_mv, W_mo = [g16[:, R0[n]:R0[n] + 128].reshape(D, D) for n in ("w_mq", "w_mk", "w_mv", "w_mo")]
    W_up = _gather_cols(g16, R0["w_up"], 704, D, 704)
    W_down = g16[:, R0["w_down"]:R0["w_down"] + 352].reshape(DFF, D)
    cw_ssd = g32[:, 0].reshape(NDEV, -1)[:, :768].reshape(NDEV, 4, 192).transpose(1, 0, 2).reshape(4, XBC)
    cw_ffn = (g32[:, PACK_ALIGN:PACK_ALIGN + 3].reshape(NDEV, -1)[:, :2112].reshape(NDEV, 3, 704)
              .transpose(1, 0, 2).reshape(3, 2 * DFF))
    W_z, W_xbc, W_dt, W_qkv = W_in[:, :D], W_in[:, D:D + XBC], W_in[:, D + XBC:D + XBC + 16], W_in[:, D + XBC + 16:]
    W_dtr = jnp.repeat(W_dt, 64, axis=1)
    W_upg, W_upv = W_up[:, :DFF], W_up[:, DFF:]
    cwg, cwv = cw_ffn[:, :DFF], cw_ffn[:, DFF:]
    cbg, cbv = conv_ffn_b[:, :DFF], conv_ffn_b[:, DFF:]
    rep = lambda p: jnp.repeat(p, 64, axis=1)
    lanes = jnp.concatenate([rep(dt_bias), rep(a_log), rep(d_skip), ssd_norm_w, jnp.zeros((4, D), F32)], axis=0)

    xs, tgt, mm = x[0], loss_target[0], mem[0]

    h1 = _norm_fwd(xs, norm_mix_w, name="norm_mix")
    z = _mm(h1, W_z, name="proj_z")
    xbc = _mm(h1, W_xbc, name="proj_xbc")
    dtr = _mm(h1, W_dtr, name="proj_dt")
    qkv = _mm(h1, W_qkv, name="proj_qkv", out_dtype=BF16)
    y_ssd, states = _ssd_fwd(z, xbc, dtr, cw_ssd, conv_ssd_b, lanes, name="ssd_fwd")
    o_sb = _sb_fwd(qkv, name="sb_fwd")
    y_sb = _head_norm_fwd(o_sb, sb_norm_w, name="sb_norm")
    ymix = jnp.concatenate([y_ssd, y_sb], axis=1)
    x1 = _mm(ymix, W_out, add=xs, name="proj_out")
    h2 = _norm_fwd(x1, norm_mem_w, name="norm_mem")
    mn = _norm_fwd(mm, norm_memkv_w, name="norm_memkv")
    qm = _mm(h2, W_mq, name="mem_q", out_dtype=BF16)
    km = _mm(mn, W_mk, name="mem_k", out_dtype=BF16)
    vm = _mm(mn, W_mv, name="mem_v", out_dtype=BF16)
    om = _mem_attn_fwd(qm, km, vm, name="mem_attn")
    x2 = _mm(om, W_mo, add=x1, name="mem_o")
    h3 = _norm_fwd(x2, norm_ffn_w, name="norm_ffn")
    ug = _mm(h3, W_upg, name="ffn_up_g")
    uv = _mm(h3, W_upv, name="ffn_up_v")
    f = _glu_fwd(ug, uv, cwg, cwv, cbg, cbv, name="ffn_glu")
    x3 = _mm(f, W_down, add=x2, name="ffn_down")
    dx3, g_nfinal, loss_part = _final(x3, norm_final_w.reshape(1, D), tgt, name="final_loss")

    G = {}
    G["w_down"] = _mm(f, dx3, trans_a=True, name="g_w_down")
    df = _mm(dx3, W_down.T, name="d_f")
    dug, duv = _glu_bwd(ug, uv, df, cwg, cwv, cbg, cbv, name="ffn_glu_bwd")
    dupg, dcg = _ffn_conv_bwd(dug, ug, cwg, name="ffn_conv_bwd_g")
    dupv, dcv = _ffn_conv_bwd(duv, uv, cwv, name="ffn_conv_bwd_v")
    G["w_up"] = jnp.concatenate([_mm(h3, dupg, trans_a=True, name="g_w_up_g"),
                                 _mm(h3, dupv, trans_a=True, name="g_w_up_v")], axis=1)
    G["conv_ffn_w"] = jnp.concatenate([dcg[0:3], dcv[0:3]], axis=1)
    G["conv_ffn_b"] = jnp.concatenate([dcg[3:4], dcv[3:4]], axis=1)
    dh3 = _mm(dupg, W_upg.T, name="d_h3_g")
    dh3 = _mm(dupv, W_upv.T, add=dh3, name="d_h3_v")
    dx2, G["norm_ffn_w"] = _norm_bwd(x2, norm_ffn_w, dh3, dx3, name="norm_ffn_bwd")
    G["w_mo"] = _mm(om, dx2, trans_a=True, name="g_w_mo")
    dom = _mm(dx2, W_mo.T, name="d_om")
    dqm, dkm, dvm = _mem_attn_bwd(qm, km, vm, dom, name="mem_attn_bwd")
    G["w_mq"] = _mm(h2, dqm, trans_a=True, name="g_w_mq")
    G["w_mk"] = _mm(mn, dkm, trans_a=True, name="g_w_mk")
    G["w_mv"] = _mm(mn, dvm, trans_a=True, name="g_w_mv")
    dh2 = _mm(dqm, W_mq.T, name="d_h2")
    dmn = _mm(dkm, W_mk.T, name="d_mn_k")
    dmn = _mm(dvm, W_mv.T, add=dmn, name="d_mn_v")
    _, G["norm_memkv_w"] = _norm_bwd(mm, norm_memkv_w, dmn, None, name="norm_memkv_bwd")
    dx1, G["norm_mem_w"] = _norm_bwd(x1, norm_mem_w, dh2, dx2, name="norm_mem_bwd")
    G["w_out"] = _mm(ymix, dx1, trans_a=True, name="g_w_out")
    dymix = _mm(dx1, W_out.T, name="d_ymix")
    do_sb, G["sb_norm_w"] = _head_norm_bwd(o_sb, sb_norm_w, dymix, name="sb_norm_bwd")
    dq, dk, dv = _sb_bwd(qkv, o_sb, do_sb, name="sb_bwd")
    dqkv = jnp.concatenate([dq, dk, dv], axis=1)
    dz, dxbc, ddtr, dlanes, dconv = _ssd_bwd(z, xbc, dtr, states, dymix, cw_ssd, conv_ssd_b, lanes, name="ssd_bwd")
    G["dt_bias"], G["a_log"], G["d_skip"] = [_group_sum(dlanes[i:i + 1]) for i in range(3)]
    G["ssd_norm_w"] = dlanes[3:4]
    G["conv_ssd_w"], G["conv_ssd_b"] = dconv[0:4], dconv[4:5]
    g_wdt = _mm(h1, ddtr, trans_a=True, name="g_w_dt").reshape(D, 16, 64).sum(axis=2)
    G["w_in"] = jnp.concatenate([_mm(h1, dz, trans_a=True, name="g_w_z"),
                                 _mm(h1, dxbc, trans_a=True, name="g_w_xbc"), g_wdt,
                                 _mm(h1, dqkv, trans_a=True, name="g_w_qkv")], axis=1)
    dh1 = _mm(dz, W_z.T, name="d_h1_z")
    dh1 = _mm(dxbc, W_xbc.T, add=dh1, name="d_h1_xbc")
    dh1 = _mm(ddtr, W_dtr.T, add=dh1, name="d_h1_dt")
    dh1 = _mm(dqkv, W_qkv.T, add=dh1, name="d_h1_qkv")
    dx, G["norm_mix_w"] = _norm_bwd(xs, norm_mix_w, dh1, dx1, name="norm_mix_bwd")
    G["norm_final_w"] = g_nfinal.reshape(D)

    slabs = [
        _scatter_cols(G["w_in"], 706), G["w_out"].reshape(NDEV, 256, D),
        G["w_mq"].reshape(NDEV, 128, D), G["w_mk"].reshape(NDEV, 128, D), G["w_mv"].reshape(NDEV, 128, D),
        G["w_mo"].reshape(NDEV, 128, D), _scatter_cols(G["w_up"], 704), G["w_down"].reshape(NDEV, 352, D),
        _pad_rows(G["conv_ssd_w"].reshape(4, NDEV, 192).transpose(1, 0, 2).reshape(NDEV, 768), 1),
        _pad_rows(G["conv_ffn_w"].reshape(3, NDEV, 704).transpose(1, 0, 2).reshape(NDEV, 2112), 3)]
    slabs = [jnp.pad(t, ((0, 0), (0, (-t.shape[1]) % PACK_ALIGN), (0, 0))) for t in slabs]
    used = sum(t.shape[1] for t in slabs)
    slabs.append(jnp.zeros((NDEV, BIG_ROWS - used, D), F32))
    send = jnp.concatenate(slabs, axis=1).astype(BF16)
    recv = _exchange(send, name="exchange_grads")
    outs_big = _adamw(recv, w_big, _pack([M[n][0] for n, _ in BIG], BIG_ROWS),
                      _pack([V[n][0] for n, _ in BIG], BIG_ROWS), name="adamw_sharded")
    small_g = _pack([G[n] for n, _ in SMALL] + [loss_part], SMALL_ROWS)
    (parts_small,) = _all_gather([small_g], name="gather_small_grads")
    outs_small = _adamw(parts_small, _pack([P[n] for n, _ in SMALL], SMALL_ROWS),
                        _pack([M[n] for n, _ in SMALL], SMALL_ROWS),
                        _pack([V[n] for n, _ in SMALL], SMALL_ROWS), name="adamw_replicated")

    res = {}
    for kind, ob, osm in zip(("grad", "delta", "new_m", "new_v"), outs_big, outs_small):
        for (n, shp), val in zip(BIG, _unpack(ob, big_shapes)):
            res[kind, n] = val.reshape((1,) + shp)
        for (n, shp), val in zip(SMALL, _unpack(osm, small_shapes)):
            res[kind, n] = val
    loss = outs_small[0][LOSS_ROW, 0]
    out = [loss, dx.reshape(1, -1, D)]
    for kind in ("grad", "delta", "new_m", "new_v"):
        out += [res[kind, n] for n in ORDER]
    return tuple(out)
```

```python
import functools
import math

import jax
import jax.numpy as jnp
from jax import lax
from jax.experimental import pallas as pl
from jax.experimental.pallas import tpu as pltpu

F32 = jnp.float32
BF16 = jnp.bfloat16

D = 1024
NDEV = 8
EPS = 1e-6
SSD_CHUNK = 128
HALO = 8
VMEM_LIMIT = 56 * 2**20

ADAM_LR, ADAM_B1, ADAM_B2, ADAM_EPS, ADAM_WD, ADAM_STEP = 0.001, 0.9, 0.999, 1e-08, 0.01, 10


def _cp(*sem):
    return pltpu.CompilerParams(dimension_semantics=sem, vmem_limit_bytes=VMEM_LIMIT)


def _tile(n, cap, mult):
    if n <= cap:
        return n
    for d in range(cap - cap % mult, 0, -mult):
        if n % d == 0:
            return d
    raise ValueError(f"no tile for {n}")


def _sigmoid(x):
    return 1.0 / (1.0 + jnp.exp(-x))


def _silu(x):
    return x * _sigmoid(x)


def _softplus(x):
    return jnp.maximum(x, 0.0) + jnp.log1p(jnp.exp(-jnp.abs(x)))


def _terms(x, n):
    out = []
    r = x.astype(F32)
    for i in range(n):
        h = r.astype(BF16)
        out.append(h)
        if i + 1 < n:
            r = r - h.astype(F32)
    return out


_DIMS = {"nn": ((1,), (0,)), "nt": ((1,), (1,)), "tn": ((0,), (0,))}


def _dot_raw(form, a, b, ta, tb):
    acc = None
    for ai in _terms(a, ta):
        for bi in _terms(b, tb):
            d = lax.dot_general(ai, bi, (_DIMS[form], ((), ())), preferred_element_type=F32)
            acc = d if acc is None else acc + d
    return acc


@functools.lru_cache(maxsize=None)
def _dot_fn(form, ta, tb):
    @jax.custom_vjp
    def f(a, b):
        return _dot_raw(form, a, b, ta, tb)

    def fwd(a, b):
        return f(a, b), (a, b)

    def bwd(res, ct):
        a, b = res
        if form == "nn":
            return _dot_fn("nt", ta, tb)(ct, b), _dot_fn("tn", ta, tb)(a, ct)
        if form == "nt":
            return _dot_fn("nn", ta, tb)(ct, b), _dot_fn("tn", tb, ta)(ct, a)
        return _dot_fn("nt", tb, ta)(b, ct), _dot_fn("nn", ta, tb)(a, ct)

    f.defvjp(fwd, bwd)
    return f


def _dot(form, a, b, ta=1, tb=1):
    return _dot_fn(form, ta, tb)(a, b)


@functools.lru_cache(maxsize=None)
def _take_fn(axis, idx):
    @jax.custom_vjp
    def f(x):
        return x[:, idx:idx + 1] if axis == 1 else x[idx:idx + 1, :]

    def fwd(x):
        return f(x), x.shape

    def bwd(shape, ct):
        io = lax.broadcasted_iota(jnp.int32, shape, axis)
        return (jnp.where(io == idx, jnp.broadcast_to(ct, shape), 0.0),)

    f.defvjp(fwd, bwd)
    return f


@functools.lru_cache(maxsize=None)
def _split_fn(width, n):
    @jax.custom_vjp
    def f(x):
        return tuple(x[:, i * width:(i + 1) * width] for i in range(n))

    def fwd(x):
        return f(x), None

    def bwd(_, cts):
        return (jnp.concatenate(list(cts), axis=1),)

    f.defvjp(fwd, bwd)
    return f


def _split(x, width):
    return _split_fn(width, x.shape[1] // width)(x)


def _iota(shape, axis):
    return lax.broadcasted_iota(jnp.int32, shape, axis)


def _mm(a, b, *, name, add=None, trans_a=False, out_dtype=F32):
    if trans_a:
        kt, m = a.shape
    else:
        m, kt = a.shape
    kt2, n = b.shape
    assert kt == kt2, (a.shape, b.shape)
    tm = _tile(m, 512, 128 if trans_a else 8)
    tn = _tile(n, 1536, 128)
    tk = _tile(kt, 1024 if trans_a else 1536, 128)
    nk = kt // tk

    def body(*refs):
        if add is None:
            a_ref, b_ref, o_ref, acc = refs
        else:
            a_ref, b_ref, add_ref, o_ref, acc = refs
        k = pl.program_id(2)

        @pl.when(k == 0)
        def _():
            acc[...] = jnp.zeros_like(acc)

        av = a_ref[...].astype(BF16)
        bv = b_ref[...].astype(BF16)
        dims = _DIMS["tn" if trans_a else "nn"]
        acc[...] += lax.dot_general(av, bv, (dims, ((), ())), preferred_element_type=F32)

        @pl.when(k == nk - 1)
        def _():
            r = acc[...]
            if add is not None:
                r = r + add_ref[...]
            o_ref[...] = r.astype(out_dtype)

    a_spec = (pl.BlockSpec((tk, tm), lambda i, j, k: (k, i)) if trans_a
              else pl.BlockSpec((tm, tk), lambda i, j, k: (i, k)))
    in_specs = [a_spec, pl.BlockSpec((tk, tn), lambda i, j, k: (k, j))]
    args = [a, b]
    if add is not None:
        in_specs.append(pl.BlockSpec((tm, tn), lambda i, j, k: (i, j)))
        args.append(add)
    return pl.pallas_call(
        body, name=name, grid=(m // tm, n // tn, nk),
        in_specs=in_specs, out_specs=pl.BlockSpec((tm, tn), lambda i, j, k: (i, j)),
        out_shape=jax.ShapeDtypeStruct((m, n), out_dtype),
        scratch_shapes=[pltpu.VMEM((tm, tn), F32)],
        compiler_params=_cp("parallel", "parallel", "arbitrary"),
    )(*args)


def _rstd(x):
    return lax.rsqrt(jnp.mean(x * x, axis=-1, keepdims=True) + EPS)


def _norm_fwd(x, w, *, name):
    s = x.shape[0]
    tm = _tile(s, 512, 8)

    def body(x_ref, w_ref, o_ref):
        xv = x_ref[...]
        o_ref[...] = (xv * _rstd(xv) * w_ref[...]).astype(BF16)

    return pl.pallas_call(
        body, name=name, grid=(s // tm,),
        in_specs=[pl.BlockSpec((tm, D), lambda i: (i, 0)), pl.BlockSpec((1, D), lambda i: (0, 0))],
        out_specs=pl.BlockSpec((tm, D), lambda i: (i, 0)),
        out_shape=jax.ShapeDtypeStruct((s, D), BF16), compiler_params=_cp("parallel"),
    )(x, w)


def _norm_bwd_math(xv, wv, dy):
    r = _rstd(xv)
    xh = xv * r
    dxh = dy * wv
    dx = r * (dxh - xh * jnp.mean(dxh * xh, axis=-1, keepdims=True))
    dw = jnp.sum(dy * xh, axis=0, keepdims=True)
    return dx, dw


def _norm_bwd(x, w, dy, add, *, name):
    s = x.shape[0]
    tm = _tile(s, 256, 8)

    def body(*refs):
        if add is None:
            x_ref, w_ref, dy_ref, dx_ref, dw_ref = refs
        else:
            x_ref, w_ref, dy_ref, add_ref, dx_ref, dw_ref = refs

        @pl.when(pl.program_id(0) == 0)
        def _():
            dw_ref[...] = jnp.zeros_like(dw_ref)

        dx, dw = _norm_bwd_math(x_ref[...], w_ref[...], dy_ref[...])
        if add is not None:
            dx = dx + add_ref[...]
        dx_ref[...] = dx
        dw_ref[...] += dw

    row = pl.BlockSpec((tm, D), lambda i: (i, 0))
    vec = pl.BlockSpec((1, D), lambda i: (0, 0))
    in_specs = [row, vec, row] + ([row] if add is not None else [])
    args = [x, w, dy] + ([add] if add is not None else [])
    return pl.pallas_call(
        body, name=name, grid=(s // tm,), in_specs=in_specs, out_specs=[row, vec],
        out_shape=[jax.ShapeDtypeStruct((s, D), F32), jax.ShapeDtypeStruct((1, D), F32)],
        compiler_params=_cp("arbitrary"),
    )(*args)


def _final(x3, w, target, *, name):
    s = x3.shape[0]
    tm = _tile(s, 256, 8)

    def body(x_ref, w_ref, t_ref, dx_ref, dw_ref, loss_ref):
        @pl.when(pl.program_id(0) == 0)
        def _():
            dw_ref[...] = jnp.zeros_like(dw_ref)
            loss_ref[...] = jnp.zeros_like(loss_ref)

        xv = x_ref[...]
        wv = w_ref[...]
        y = xv * _rstd(xv) * wv
        err = y - t_ref[...]
        loss_ref[...] += 0.5 * jnp.sum(jnp.mean(err * err, axis=-1, keepdims=True))
        dx, dw = _norm_bwd_math(xv, wv, err * (1.0 / D))
        dx_ref[...] = dx
        dw_ref[...] += dw

    row = pl.BlockSpec((tm, D), lambda i: (i, 0))
    vec = pl.BlockSpec((1, D), lambda i: (0, 0))
    return pl.pallas_call(
        body, name=name, grid=(s // tm,), in_specs=[row, vec, row], out_specs=[row, vec, vec],
        out_shape=[jax.ShapeDtypeStruct((s, D), F32), jax.ShapeDtypeStruct((1, D), F32),
                   jax.ShapeDtypeStruct((1, D), F32)],
        compiler_params=_cp("arbitrary"),
    )(x3, w, target)


def _head_norm_math(o, w):
    lane = _iota((128, 128), 0) // 64
    bd = (lane == _iota((128, 128), 1) // 64).astype(F32)
    outs = []
    for op in _split(o, 128):
        ms = _dot("nn", op * op, bd, 2, 1) * (1.0 / 64)
        outs.append(op * lax.rsqrt(ms + EPS))
    return jnp.concatenate(outs, axis=1) * w


def _head_norm_fwd(o, w, *, name):
    s = o.shape[0]
    tm = _tile(s, 256, 8)

    def body(o_ref, w_ref, y_ref):
        y_ref[...] = _head_norm_math(o_ref[...], w_ref[...]).astype(BF16)

    row = pl.BlockSpec((tm, D), lambda i: (i, 0))
    vec = pl.BlockSpec((1, D), lambda i: (0, 0))
    return pl.pallas_call(
        body, name=name, grid=(s // tm,), in_specs=[row, vec], out_specs=row,
        out_shape=jax.ShapeDtypeStruct((s, D), BF16), compiler_params=_cp("parallel"),
    )(o, w)


def _head_norm_bwd(o, w, dymix, *, name):
    s = o.shape[0]
    tm = _tile(s, 256, 8)

    def body(o_ref, w_ref, dy_ref, do_ref, dw_ref):
        @pl.when(pl.program_id(0) == 0)
        def _():
            dw_ref[...] = jnp.zeros_like(dw_ref)

        _, vjp = jax.vjp(_head_norm_math, o_ref[...], w_ref[...])
        do, dw = vjp(dy_ref[...])
        do_ref[...] = do
        dw_ref[...] += dw

    row = pl.BlockSpec((tm, D), lambda i: (i, 0))
    vec = pl.BlockSpec((1, D), lambda i: (0, 0))
    return pl.pallas_call(
        body, name=name, grid=(s // tm,),
        in_specs=[row, vec, pl.BlockSpec((tm, D), lambda i: (i, 1))], out_specs=[row, vec],
        out_shape=[jax.ShapeDtypeStruct((s, D), F32), jax.ShapeDtypeStruct((1, D), F32)],
        compiler_params=_cp("arbitrary"),
    )(o, w, dymix)


SB_BQ = 256
SB_BK = 256


def _sb_consts():
    r = _iota((SB_BK, SB_BK), 0)
    c = _iota((SB_BK, SB_BK), 1)
    u_excl = (r > c).astype(BF16)
    u_incl = (r >= c).astype(BF16)
    return u_excl, u_incl


SB_LANES = 256
SB_NCH = SB_LANES // 64


def _nt(a, b):
    return lax.dot_general(a, b, (_DIMS["nt"], ((), ())), preferred_element_type=F32)


def _tn(a, b):
    return lax.dot_general(a, b, (_DIMS["tn"], ((), ())), preferred_element_type=F32)


def _nn(a, b):
    return jnp.dot(a, b, preferred_element_type=F32)


def _sb_heads(ref):
    out = []
    for hp in range(SB_LANES // 128):
        v = ref[:, 128 * hp:128 * (hp + 1)]
        first = _iota(v.shape, 1) < 64
        out += [jnp.where(first, v, 0).astype(BF16), jnp.where(first, 0, v).astype(BF16)]
    return out


SB_STRIP = 32


def _neg_abs(x):
    bits = lax.bitcast_convert_type(x, jnp.uint32) | jnp.uint32(0x80000000)
    return lax.bitcast_convert_type(bits, F32)


def _sb_block(ref, j):
    off = pl.multiple_of(j * SB_BK, SB_BK)
    return [ref[pl.ds(off, SB_BK), 128 * hp:128 * (hp + 1)] for hp in range(SB_NCH // 2)]


SB_DEAD = 104.0


def _sb_live(nlrun):
    m = nlrun[0]
    for x in nlrun[1:]:
        m = jnp.minimum(m, x)
    return jnp.min(m) < SB_DEAD


def _sb_strips():
    return [(r, pl.ds(r, SB_STRIP)) for r in range(0, SB_BQ, SB_STRIP)]


def _sb_diag_mask(r):
    return _iota((SB_STRIP, SB_BK), 1) < _iota((SB_STRIP, SB_BK), 0) + r


def _sb_soft(z, mask):
    e = jnp.exp(_neg_abs(z))
    nl = jnp.maximum(z, 0.0) + jnp.log(1.0 + e)
    if mask is not None:
        nl = jnp.where(mask, nl, 0.0)
    return e, nl


def _sb_split_to(hl_ref, rows, x):
    hi, lo = _terms(x, 2)
    hl_ref[rows, 0:SB_BK] = hi
    hl_ref[rows, SB_BK:2 * SB_BK] = lo


def _sb_stage_soft(z_ref, nl_ref, diag):
    for r, rows in _sb_strips():
        _, nl = _sb_soft(z_ref[rows, :], _sb_diag_mask(r) if diag else None)
        nl_ref[rows, 0:SB_BK] = nl.astype(BF16)


def _sb_stage_weights(z_ref, c_ref, a_ref, nlrun, diag):
    for r, rows in _sb_strips():
        a = jnp.exp(z_ref[rows, :] - c_ref[rows, :] - nlrun[r:r + SB_STRIP, :])
        if diag:
            a = jnp.where(_sb_diag_mask(r), a, 0.0)
        a_ref[rows, :] = a.astype(BF16)


def _sb_fwd(qkv, ride, *, name):
    s = qkv.shape[0]
    nq = s // SB_BQ
    ng = D // SB_LANES
    assert SB_BQ == SB_BK

    def body(q_ref, k_ref, v_ref, ride_ref, o_ref, got_ref, zbuf, nlbuf, cbuf, abuf, *sems):
        i = pl.program_id(1)
        step_no = pl.program_id(0) * nq + i

        @pl.when(step_no == 0)
        def _():
            _Direct(ride_ref, got_ref, sems, True).start()

        @pl.when(step_no == ng * nq - 1)
        def _():
            _Direct(ride_ref, got_ref, sems, True).wait()

        _, u_incl = _sb_consts()
        lane_a = _iota((SB_BQ, 128), 1) < 64
        qh = [q * 0.125 for q in _sb_heads(q_ref)]

        def tile(j, accs, nlrun, diag):
            kbs = _sb_block(k_ref, j)
            for c in range(SB_NCH):
                zbuf[c] = _nt(qh[c], kbs[c // 2])
            for c in range(SB_NCH):
                _sb_stage_soft(zbuf.at[c], nlbuf.at[c], diag)
                cbuf[c] = _nn(nlbuf[c], u_incl)
            for c in range(SB_NCH):
                _sb_stage_weights(zbuf.at[c], cbuf.at[c], abuf.at[c], nlrun[c], diag)
            nlrun = tuple(nlrun[c] + cbuf[c, :, 0:1] for c in range(SB_NCH))
            vbs = _sb_block(v_ref, j)
            outs = [_nn(abuf[c], vbs[c // 2]) for c in range(SB_NCH)]
            accs = tuple(acc + jnp.where(lane_a, outs[2 * hp], outs[2 * hp + 1]) for hp, acc in enumerate(accs))
            return accs, nlrun

        accs, nlrun = tile(i, (jnp.zeros((SB_BQ, 128), F32),) * (SB_NCH // 2),
                           (jnp.zeros((SB_BQ, 1), F32),) * SB_NCH, True)

        def step(carry):
            j, _, accs, nlrun = carry
            accs, nlrun = tile(j, accs, nlrun, False)
            return j - 1, _sb_live(nlrun), accs, nlrun

        _, _, accs, _ = lax.while_loop(lambda c: (c[0] >= 0) & c[1], step, (i - 1, _sb_live(nlrun), accs, nlrun))
        o_ref[...] = jnp.concatenate(accs, axis=1)

    return pl.pallas_call(
        body, name=name, grid=(ng, nq),
        in_specs=[pl.BlockSpec((SB_BQ, SB_LANES), lambda g, i: (i, g)),
                  pl.BlockSpec((s, SB_LANES), lambda g, i: (0, ng + g)),
                  pl.BlockSpec((s, SB_LANES), lambda g, i: (0, 2 * ng + g)),
                  pl.BlockSpec(memory_space=pl.ANY)],
        out_specs=[pl.BlockSpec((SB_BQ, SB_LANES), lambda g, i: (i, g)), pl.BlockSpec(memory_space=pl.ANY)],
        out_shape=[jax.ShapeDtypeStruct((s, D), F32), _recv_shape(ride, True)],
        scratch_shapes=[pltpu.VMEM((SB_NCH, SB_BQ, SB_BK), F32), pltpu.VMEM((SB_NCH, SB_BQ, SB_BK), BF16),
                        pltpu.VMEM((SB_NCH, SB_BQ, SB_BK), F32), pltpu.VMEM((SB_NCH, SB_BQ, SB_BK), BF16),
                        *_Direct.SEMS],
        compiler_params=_cp("arbitrary", "arbitrary"),
    )(qkv, qkv, qkv, ride)


def _sb_bwd(qkv, o, do, ride, *, name):
    s = qkv.shape[0]
    nq = s // SB_BQ
    ng = D // SB_LANES
    nhp = SB_NCH // 2

    def body(q_ref, k_ref, v_ref, o_ref, do_ref, ride_ref, dq_ref, dk_hbm, dv_hbm, got_ref, dk_acc, dv_acc, sems,
             zbuf, gbuf, hl, cbuf, abuf, dzbuf, *ride_sems):
        g_idx = pl.program_id(0)
        i = pl.program_id(1)
        step_no = g_idx * nq + i

        @pl.when(step_no == 0)
        def _():
            _Direct(ride_ref, got_ref, ride_sems, False).start()

        @pl.when(step_no == ng * nq - 1)
        def _():
            _Direct(ride_ref, got_ref, ride_sems, False).wait()

        @pl.when(i == 0)
        def _():
            dk_acc[...] = jnp.zeros_like(dk_acc)
            dv_acc[...] = jnp.zeros_like(dv_acc)

        _, u_incl = _sb_consts()
        u2 = jnp.concatenate([u_incl, u_incl], axis=0)
        lane_a = _iota((SB_BQ, 128), 1) < 64
        lane_k = _iota((SB_BK, 128), 1) < 64
        qh = [q * 0.125 for q in _sb_heads(q_ref)]
        qf = [q_ref[:, 128 * hp:128 * (hp + 1)] for hp in range(nhp)]
        doh = _sb_heads(do_ref)
        dof = [do_ref[:, 128 * hp:128 * (hp + 1)].astype(BF16) for hp in range(nhp)]
        delta = []
        for hp in range(nhp):
            prod = dof[hp].astype(F32) * o_ref[:, 128 * hp:128 * (hp + 1)]
            delta += [jnp.sum(jnp.where(lane_a, prod, 0.0), axis=1, keepdims=True),
                      jnp.sum(jnp.where(lane_a, 0.0, prod), axis=1, keepdims=True)]

        def pre(slot, j):
            kbs = _sb_block(k_ref, j)
            vbs = _sb_block(v_ref, j)
            for c in range(SB_NCH):
                zbuf[slot, c] = _nt(qh[c], kbs[c // 2])
                gbuf[slot, c] = _nt(doh[c], vbs[c // 2])

        def stage_g(c, slot):
            for _, rows in _sb_strips():
                g = abuf[slot, c, rows, :].astype(F32) * gbuf[slot, c, rows, :]
                gbuf[slot, c, rows, :] = g
                _sb_split_to(hl.at[c], rows, g)

        def stage_dz(c, slot, grun, diag):
            for r, rows in _sb_strips():
                z = zbuf[slot, c, rows, :]
                g = gbuf[slot, c, rows, :]
                cs = (delta[c] - grun)[r:r + SB_STRIP, :] - cbuf[c, rows, :]
                sig = 1.0 / (1.0 + jnp.exp(-z))
                dz = g - (g + cs) * sig
                if diag:
                    dz = jnp.where(_sb_diag_mask(r), dz, 0.0)
                dzbuf[slot, c, rows, :] = dz.astype(BF16)

        def chain(slot, nlrun, grun, diag):
            for c in range(SB_NCH):
                _sb_stage_soft(zbuf.at[slot, c], hl.at[c], diag)
                cbuf[c] = _nn(hl[c, :, 0:SB_BK], u_incl)
            nl_tot = []
            for c in range(SB_NCH):
                _sb_stage_weights(zbuf.at[slot, c], cbuf.at[c], abuf.at[slot, c], nlrun[c], diag)
                nl_tot.append(cbuf[c, :, 0:1])
                stage_g(c, slot)
                cbuf[c] = _nn(hl[c], u2)
            g_tot = []
            for c in range(SB_NCH):
                stage_dz(c, slot, grun[c], diag)
                g_tot.append(cbuf[c, :, 0:1])
            return (tuple(a + b for a, b in zip(nlrun, nl_tot)), tuple(a + b for a, b in zip(grun, g_tot)))

        def post(slot, j, dqs):
            off = pl.multiple_of(j * SB_BK, SB_BK)
            kbs = _sb_block(k_ref, j)
            dq_t = [_nn(dzbuf[slot, c], kbs[c // 2]) for c in range(SB_NCH)]
            dk_t = [_tn(dzbuf[slot, c], qf[c // 2]) for c in range(SB_NCH)]
            dv_t = [_tn(abuf[slot, c], dof[c // 2]) for c in range(SB_NCH)]
            for hp in range(nhp):
                cols = slice(128 * hp, 128 * (hp + 1))
                dk_acc[pl.ds(off, SB_BK), cols] += 0.125 * jnp.where(lane_k, dk_t[2 * hp], dk_t[2 * hp + 1])
                dv_acc[pl.ds(off, SB_BK), cols] += jnp.where(lane_k, dv_t[2 * hp], dv_t[2 * hp + 1])
            return tuple(dq + jnp.where(lane_a, dq_t[2 * hp], dq_t[2 * hp + 1]) for hp, dq in enumerate(dqs))

        def tile(j, dqs, nlrun, grun, diag):
            pre(0, j)
            nlrun, grun = chain(0, nlrun, grun, diag)
            return post(0, j, dqs), nlrun, grun

        zero = (jnp.zeros((SB_BQ, 1), F32),) * SB_NCH
        dqs, nlrun, grun = tile(i, (jnp.zeros((SB_BQ, 128), F32),) * nhp, zero, zero, True)

        def step(carry):
            j, _, dqs, nlrun, grun = carry
            dqs, nlrun, grun = tile(j, dqs, nlrun, grun, False)
            return j - 1, _sb_live(nlrun), dqs, nlrun, grun

        carry = lax.while_loop(lambda c: (c[0] >= 0) & c[1], step, (i - 1, _sb_live(nlrun), dqs, nlrun, grun))
        dq_ref[...] = 0.125 * jnp.concatenate(carry[2], axis=1)

        @pl.when(i == nq - 1)
        def _():
            cols = pl.ds(pl.multiple_of(g_idx * SB_LANES, SB_LANES), SB_LANES)
            ck = pltpu.make_async_copy(dk_acc, dk_hbm.at[:, cols], sems.at[0])
            cv = pltpu.make_async_copy(dv_acc, dv_hbm.at[:, cols], sems.at[1])
            ck.start()
            cv.start()
            ck.wait()
            cv.wait()

    qblk = pl.BlockSpec((SB_BQ, SB_LANES), lambda g, i: (i, g))
    hbm = pl.BlockSpec(memory_space=pl.ANY)
    return pl.pallas_call(
        body, name=name, grid=(ng, nq),
        in_specs=[qblk, pl.BlockSpec((s, SB_LANES), lambda g, i: (0, ng + g)),
                  pl.BlockSpec((s, SB_LANES), lambda g, i: (0, 2 * ng + g)), qblk, qblk, hbm],
        out_specs=[qblk, hbm, hbm, hbm],
        out_shape=[jax.ShapeDtypeStruct((s, D), F32)] * 3 + [_recv_shape(ride, False)],
        scratch_shapes=[pltpu.VMEM((s, SB_LANES), F32), pltpu.VMEM((s, SB_LANES), F32),
                        pltpu.SemaphoreType.DMA((2,)),
                        pltpu.VMEM((1, SB_NCH, SB_BQ, SB_BK), F32), pltpu.VMEM((1, SB_NCH, SB_BQ, SB_BK), F32),
                        pltpu.VMEM((SB_NCH, SB_BQ, 2 * SB_BK), BF16), pltpu.VMEM((SB_NCH, SB_BQ, SB_BK), F32),
                        pltpu.VMEM((1, SB_NCH, SB_BQ, SB_BK), BF16), pltpu.VMEM((1, SB_NCH, SB_BQ, SB_BK), BF16),
                        *_Direct.SEMS],
        compiler_params=_cp("arbitrary", "arbitrary"),
    )(qkv, qkv, qkv, o, do, ride)


def _ssd_core(z, xpre, dtr, state, dtb, alog, dsk, nw):
    L = SSD_CHUNK
    xa = _silu(xpre)
    pieces = _split(xa, 128)
    xs = jnp.concatenate(pieces[:8], axis=1)
    bm, cm = pieces[8:10], pieces[10:12]
    dt = _softplus(dtr + dtb)
    a = dt * (-jnp.exp(alog))
    tri = (_iota((L, L), 0) >= _iota((L, L), 1)).astype(F32)
    a_cs = _dot("nn", tri, a, 1, 3)
    xc = xs * dt
    tril = _iota((L, L), 0) >= _iota((L, L), 1)
    lane_a = _iota((L, 128), 1) < 64
    acs_p = _split(a_cs, 128)
    xc_p = _split(xc, 128)
    ys, new_states = [], []
    for g in range(2):
        cb = _dot("nt", cm[g], bm[g])
        for pp in range(4):
            pair = 4 * g + pp
            acs = acs_p[pair]
            acs_t = acs.T
            xcp = xc_p[pair]
            st = state[pair]
            heads = []
            for hh in range(2):
                col = _take_fn(1, 64 * hh)(acs)
                row = _take_fn(0, 64 * hh)(acs_t)
                seg = col - row
                lm = jnp.where(tril, jnp.exp(jnp.where(tril, seg, 0.0)), 0.0)
                heads.append(_dot("nn", cb * lm, xcp))
            ydiag = jnp.where(lane_a, heads[0], heads[1])
            last = _take_fn(0, L - 1)(acs)
            snew = _dot("tn", xcp * jnp.exp(last - acs), bm[g])
            new_states.append(st * jnp.exp(_take_fn(1, L - 1)(acs_t)) + snew)
            yoff = _dot("nt", cm[g], st) * jnp.exp(acs)
            ys.append(ydiag + yoff)
    y = jnp.concatenate(ys, axis=1) + xs * dsk
    yg = y * _silu(z)
    outs = []
    for v in _split(yg, 512):
        outs.append(v * lax.rsqrt(jnp.mean(v * v, axis=-1, keepdims=True) + EPS))
    return jnp.concatenate(outs, axis=1) * nw, tuple(new_states)


XBC = 1536


def _ssd_conv(ext_ref, cw, cb):
    acc = cb
    for k in range(4):
        acc = acc + cw[k:k + 1, :] * ext_ref[pl.ds(HALO - 3 + k, SSD_CHUNK), :]
    return acc


def _ssd_fwd(z, xbc, dtr, cw, cb, lanes, ride, *, name):
    s = z.shape[0]
    L = SSD_CHUNK
    nc = s // L

    def body(z_ref, x_ref, h_ref, dtr_ref, cw_ref, cb_ref, ln_ref, ride_ref, y_ref, st_ref, got_ref,
             state, ext, *sems):
        c = pl.program_id(0)

        @pl.when(c == 0)
        def _():
            state[...] = jnp.zeros_like(state)
            _Direct(ride_ref, got_ref, sems, True).start()

        @pl.when(c == nc - 1)
        def _():
            _Direct(ride_ref, got_ref, sems, True).wait()

        ext[0:HALO, :] = jnp.where(c == 0, 0.0, h_ref[...])
        ext[HALO:, :] = x_ref[...]
        xpre = _ssd_conv(ext, cw_ref[...], cb_ref[...])
        st_ref[0] = state[...]
        st_in = tuple(state[p] for p in range(8))
        yn, st_out = _ssd_core(z_ref[...], xpre, dtr_ref[...], st_in,
                               ln_ref[0:1, :], ln_ref[1:2, :], ln_ref[2:3, :], ln_ref[3:4, :])
        y_ref[...] = yn.astype(BF16)
        for p in range(8):
            state[p] = st_out[p]

    return pl.pallas_call(
        body, name=name, grid=(nc,),
        in_specs=[pl.BlockSpec((L, D), lambda c: (c, 0)),
                  pl.BlockSpec((L, XBC), lambda c: (c, 0)),
                  pl.BlockSpec((HALO, XBC), lambda c: (jnp.maximum(c * (L // HALO) - 1, 0), 0)),
                  pl.BlockSpec((L, D), lambda c: (c, 0)),
                  pl.BlockSpec((4, XBC), lambda c: (0, 0)),
                  pl.BlockSpec((1, XBC), lambda c: (0, 0)),
                  pl.BlockSpec((8, D), lambda c: (0, 0)),
                  pl.BlockSpec(memory_space=pl.ANY)],
        out_specs=[pl.BlockSpec((L, D), lambda c: (c, 0)),
                   pl.BlockSpec((1, 8, 128, 128), lambda c: (c, 0, 0, 0)),
                   pl.BlockSpec(memory_space=pl.ANY)],
        out_shape=[jax.ShapeDtypeStruct((s, D), BF16), jax.ShapeDtypeStruct((nc, 8, 128, 128), F32),
                   _recv_shape(ride, True)],
        scratch_shapes=[pltpu.VMEM((8, 128, 128), F32), pltpu.VMEM((L + HALO, XBC), F32), *_Direct.SEMS],
        compiler_params=_cp("arbitrary"),
    )(z, xbc, xbc, dtr, cw, cb, lanes, ride)


def _ssd_bwd(z, xbc, dtr, states, dymix, cw, cb, lanes, *, name):
    s = z.shape[0]
    L = SSD_CHUNK
    nc = s // L

    def body(z_ref, x_ref, h_ref, dtr_ref, st_ref, dy_ref, cw_ref, cb_ref, ln_ref,
             dz_ref, dx_ref, ddt_ref, dln_ref, dcv_ref, dstate, ext, dext):
        i = pl.program_id(0)
        c = nc - 1 - i

        @pl.when(i == 0)
        def _():
            dstate[...] = jnp.zeros_like(dstate)
            dext[...] = jnp.zeros_like(dext)
            dln_ref[...] = jnp.zeros_like(dln_ref)
            dcv_ref[...] = jnp.zeros_like(dcv_ref)

        ext[0:HALO, :] = jnp.where(c == 0, 0.0, h_ref[...])
        ext[HALO:, :] = x_ref[...]
        cwv = cw_ref[...]
        xpre = _ssd_conv(ext, cwv, cb_ref[...])
        st_in = tuple(st_ref[0, p] for p in range(8))
        _, vjp = jax.vjp(_ssd_core, z_ref[...], xpre, dtr_ref[...], st_in,
                         ln_ref[0:1, :], ln_ref[1:2, :], ln_ref[2:3, :], ln_ref[3:4, :])
        dz, dxpre, ddtr, dst, d0, d1, d2, d3 = vjp((dy_ref[...], tuple(dstate[p] for p in range(8))))
        for p in range(8):
            dstate[p] = dst[p]
        dz_ref[...] = dz.astype(BF16)
        ddt_ref[...] = ddtr.astype(BF16)
        dln_ref[0:4, :] += jnp.concatenate([d0, d1, d2, d3], axis=0)
        dext[0:L, :] = dxpre
        dx = jnp.zeros((L, XBC), F32)
        rows = []
        for k in range(4):
            dx = dx + cwv[k:k + 1, :] * dext[pl.ds(3 - k, L), :]
            rows.append(jnp.sum(dxpre * ext[pl.ds(HALO - 3 + k, L), :], axis=0, keepdims=True))
        rows.append(jnp.sum(dxpre, axis=0, keepdims=True))
        dx_ref[...] = dx.astype(BF16)
        dcv_ref[0:5, :] += jnp.concatenate(rows, axis=0)
        dext[L:L + HALO, :] = dxpre[0:HALO, :]

    rev = lambda i: (nc - 1 - i, 0)
    return pl.pallas_call(
        body, name=name, grid=(nc,),
        in_specs=[pl.BlockSpec((L, D), rev),
                  pl.BlockSpec((L, XBC), rev),
                  pl.BlockSpec((HALO, XBC), lambda i: (jnp.maximum((nc - 1 - i) * (L // HALO) - 1, 0), 0)),
                  pl.BlockSpec((L, D), rev),
                  pl.BlockSpec((1, 8, 128, 128), lambda i: (nc - 1 - i, 0, 0, 0)),
                  pl.BlockSpec((L, D), rev),
                  pl.BlockSpec((4, XBC), lambda i: (0, 0)),
                  pl.BlockSpec((1, XBC), lambda i: (0, 0)),
                  pl.BlockSpec((8, D), lambda i: (0, 0))],
        out_specs=[pl.BlockSpec((L, D), rev), pl.BlockSpec((L, XBC), rev), pl.BlockSpec((L, D), rev),
                   pl.BlockSpec((8, D), lambda i: (0, 0)), pl.BlockSpec((8, XBC), lambda i: (0, 0))],
        out_shape=[jax.ShapeDtypeStruct((s, D), BF16), jax.ShapeDtypeStruct((s, XBC), BF16),
                   jax.ShapeDtypeStruct((s, D), BF16), jax.ShapeDtypeStruct((8, D), F32),
                   jax.ShapeDtypeStruct((8, XBC), F32)],
        scratch_shapes=[pltpu.VMEM((8, 128, 128), F32), pltpu.VMEM((L + HALO, XBC), F32),
                        pltpu.VMEM((L + HALO, XBC), F32)],
        compiler_params=_cp("arbitrary"),
    )(z, xbc, xbc, dtr, states, dymix, cw, cb, lanes)


def _mem_attn_math(q, k, v):
    outs = []
    for qh, kh, vh in zip(_split(q, 256), _split(k, 256), _split(v, 256)):
        sc = _dot("nt", qh, kh) * (1.0 / 16.0)
        e = jnp.exp(sc - lax.stop_gradient(jnp.max(sc, axis=-1, keepdims=True)))
        p = e / jnp.sum(e, axis=-1, keepdims=True)
        outs.append(_dot("nn", p, vh))
    return jnp.concatenate(outs, axis=1)


def _mem_attn_fwd(q, k, v, *, name):
    s, m = q.shape[0], k.shape[0]
    tm = _tile(s, 256, 8)

    def body(q_ref, k_ref, v_ref, o_ref):
        o_ref[...] = _mem_attn_math(q_ref[...].astype(F32), k_ref[...].astype(F32),
                                    v_ref[...].astype(F32)).astype(BF16)

    row = pl.BlockSpec((tm, D), lambda i: (i, 0))
    kv = pl.BlockSpec((m, D), lambda i: (0, 0))
    return pl.pallas_call(
        body, name=name, grid=(s // tm,), in_specs=[row, kv, kv], out_specs=row,
        out_shape=jax.ShapeDtypeStruct((s, D), BF16), compiler_params=_cp("parallel"),
    )(q, k, v)


def _mem_attn_bwd(q, k, v, do, *, name):
    s, m = q.shape[0], k.shape[0]
    tm = _tile(s, 256, 8)

    def body(q_ref, k_ref, v_ref, do_ref, dq_ref, dk_ref, dv_ref):
        @pl.when(pl.program_id(0) == 0)
        def _():
            dk_ref[...] = jnp.zeros_like(dk_ref)
            dv_ref[...] = jnp.zeros_like(dv_ref)

        _, vjp = jax.vjp(_mem_attn_math, q_ref[...].astype(F32), k_ref[...].astype(F32),
                         v_ref[...].astype(F32))
        dq, dk, dv = vjp(do_ref[...])
        dq_ref[...] = dq.astype(BF16)
        dk_ref[...] += dk
        dv_ref[...] += dv

    row = pl.BlockSpec((tm, D), lambda i: (i, 0))
    kv = pl.BlockSpec((m, D), lambda i: (0, 0))
    return pl.pallas_call(
        body, name=name, grid=(s // tm,), in_specs=[row, kv, kv, row], out_specs=[row, kv, kv],
        out_shape=[jax.ShapeDtypeStruct((s, D), BF16), jax.ShapeDtypeStruct((m, D), F32),
                   jax.ShapeDtypeStruct((m, D), F32)],
        compiler_params=_cp("arbitrary"),
    )(q, k, v, do)


DFF = 2816
FFN_TC = 1408
FFN_TM = 256


def _ffn_conv(ext_ref, cw, cb, tm):
    acc = cb
    for k in range(3):
        acc = acc + cw[k:k + 1, :] * ext_ref[pl.ds(HALO - 2 + k, tm), :]
    return acc


def _ffn_specs(s):
    tm, tc = FFN_TM, FFN_TC
    blk = pl.BlockSpec((tm, tc), lambda i, j: (i, j))
    halo = pl.BlockSpec((HALO, tc), lambda i, j: (jnp.maximum(i * (tm // HALO) - 1, 0), j))
    cw = pl.BlockSpec((3, tc), lambda i, j: (0, j))
    cb = pl.BlockSpec((1, tc), lambda i, j: (0, j))
    return tm, tc, blk, halo, cw, cb


def _glu_fwd(ug, uv, cwg, cwv, cbg, cbv, *, name):
    s = ug.shape[0]
    tm, tc, blk, halo, cw, cb = _ffn_specs(s)

    def body(g_ref, gh_ref, v_ref, vh_ref, cwg_ref, cwv_ref, cbg_ref, cbv_ref, f_ref, eg, ev):
        first = pl.program_id(0) == 0
        eg[0:HALO, :] = jnp.where(first, 0.0, gh_ref[...])
        eg[HALO:, :] = g_ref[...]
        ev[0:HALO, :] = jnp.where(first, 0.0, vh_ref[...])
        ev[HALO:, :] = v_ref[...]
        g = _ffn_conv(eg, cwg_ref[...], cbg_ref[...], tm)
        v = _ffn_conv(ev, cwv_ref[...], cbv_ref[...], tm)
        f_ref[...] = (_silu(g) * v).astype(BF16)

    return pl.pallas_call(
        body, name=name, grid=(s // tm, DFF // tc),
        in_specs=[blk, halo, blk, halo, cw, cw, cb, cb], out_specs=blk,
        out_shape=jax.ShapeDtypeStruct((s, DFF), BF16),
        scratch_shapes=[pltpu.VMEM((tm + HALO, tc), F32)] * 2,
        compiler_params=_cp("parallel", "parallel"),
    )(ug, ug, uv, uv, cwg, cwv, cbg, cbv)


def _glu_bwd(ug, uv, df, cwg, cwv, cbg, cbv, *, name):
    s = ug.shape[0]
    tm, tc, blk, halo, cw, cb = _ffn_specs(s)

    def body(g_ref, gh_ref, v_ref, vh_ref, df_ref, cwg_ref, cwv_ref, cbg_ref, cbv_ref, dg_ref, dv_ref, eg, ev):
        first = pl.program_id(0) == 0
        eg[0:HALO, :] = jnp.where(first, 0.0, gh_ref[...])
        eg[HALO:, :] = g_ref[...]
        ev[0:HALO, :] = jnp.where(first, 0.0, vh_ref[...])
        ev[HALO:, :] = v_ref[...]
        g = _ffn_conv(eg, cwg_ref[...], cbg_ref[...], tm)
        v = _ffn_conv(ev, cwv_ref[...], cbv_ref[...], tm)
        dfv = df_ref[...]
        sg = _sigmoid(g)
        dv_ref[...] = dfv * g * sg
        dg_ref[...] = dfv * v * sg * (1.0 + g * (1.0 - sg))

    return pl.pallas_call(
        body, name=name, grid=(s // tm, DFF // tc),
        in_specs=[blk, halo, blk, halo, blk, cw, cw, cb, cb], out_specs=[blk, blk],
        out_shape=[jax.ShapeDtypeStruct((s, DFF), F32)] * 2,
        scratch_shapes=[pltpu.VMEM((tm + HALO, tc), F32)] * 2,
        compiler_params=_cp("parallel", "parallel"),
    )(ug, ug, uv, uv, df, cwg, cwv, cbg, cbv)


def _ffn_conv_bwd(du, u, cwt, *, name):
    s = du.shape[0]
    tm, tc = FFN_TM, FFN_TC
    nb = s // tm

    def body(du_ref, duh_ref, u_ref, uh_ref, cw_ref, dx_ref, dc_ref, edu, eu):
        i = pl.program_id(1)

        @pl.when(i == 0)
        def _():
            dc_ref[...] = jnp.zeros_like(dc_ref)

        duv = du_ref[...]
        edu[0:tm, :] = duv
        edu[tm:, :] = jnp.where(i == nb - 1, 0.0, duh_ref[...])
        eu[0:HALO, :] = jnp.where(i == 0, 0.0, uh_ref[...])
        eu[HALO:, :] = u_ref[...]
        cwv = cw_ref[...]
        dx = jnp.zeros((tm, tc), F32)
        rows = []
        for k in range(3):
            dx = dx + cwv[k:k + 1, :] * edu[pl.ds(2 - k, tm), :]
            rows.append(jnp.sum(duv * eu[pl.ds(HALO - 2 + k, tm), :], axis=0, keepdims=True))
        rows.append(jnp.sum(duv, axis=0, keepdims=True))
        dx_ref[...] = dx.astype(BF16)
        dc_ref[0:4, :] += jnp.concatenate(rows, axis=0)

    blk = pl.BlockSpec((tm, tc), lambda j, i: (i, j))
    nxt = pl.BlockSpec((HALO, tc), lambda j, i: (jnp.minimum((i + 1) * (tm // HALO), s // HALO - 1), j))
    prv = pl.BlockSpec((HALO, tc), lambda j, i: (jnp.maximum(i * (tm // HALO) - 1, 0), j))
    return pl.pallas_call(
        body, name=name, grid=(DFF // tc, nb),
        in_specs=[blk, nxt, blk, prv, pl.BlockSpec((3, tc), lambda j, i: (0, j))],
        out_specs=[blk, pl.BlockSpec((8, tc), lambda j, i: (0, j))],
        out_shape=[jax.ShapeDtypeStruct((s, DFF), BF16), jax.ShapeDtypeStruct((8, DFF), F32)],
        scratch_shapes=[pltpu.VMEM((tm + HALO, tc), F32)] * 2,
        compiler_params=_cp("parallel", "arbitrary"),
    )(du, du, u, u, cwt)


MESH = pl.DeviceIdType.MESH


def _all_gather(arrs, *, name):
    n = len(arrs)

    def body(*refs):
        x_refs, out_refs = refs[:n], refs[n:2 * n]
        send_sems, recv_sems, local_sems = refs[2 * n:]
        x, y, c = lax.axis_index("x"), lax.axis_index("y"), lax.axis_index("c")
        me, sibling = (x, y, c), (x, y, 1 - c)
        chips = [(1 - x, y), (x, 1 - y), (1 - x, 1 - y)]

        def blk(a, dev):
            return out_refs[a].at[4 * dev[0] + 2 * dev[1] + dev[2]]

        def copy(a, k, block, to, src=None):
            return pltpu.make_async_remote_copy(
                src_ref=blk(a, block) if src is None else src, dst_ref=blk(a, block),
                send_sem=send_sems.at[7 * a + k], recv_sem=recv_sems.at[7 * a + k],
                device_id=to, device_id_type=MESH)

        started = []
        mine = []
        for a in range(n):
            cp = pltpu.make_async_copy(x_refs[a], blk(a, me), local_sems.at[a])
            cp.start()
            mine.append(cp)
            first = [copy(a, 0, me, sibling, src=x_refs[a])]
            first += [copy(a, 1 + j, me, (*chip, c), src=x_refs[a]) for j, chip in enumerate(chips)]
            for cp in first:
                cp.start()
            started += first
        for a in range(n):
            for j, chip in enumerate(chips):
                copy(a, 1 + j, (*chip, c), me).wait_recv()
                fwd = copy(a, 4 + j, (*chip, c), sibling)
                fwd.start()
                started.append(fwd)
        for a in range(n):
            copy(a, 0, sibling, me).wait_recv()
            for j, chip in enumerate(chips):
                copy(a, 4 + j, (*chip, 1 - c), me).wait_recv()
        for cp in started:
            cp.wait_send()
        for cp in mine:
            cp.wait()

    any_spec = pl.BlockSpec(memory_space=pl.ANY)
    return pl.pallas_call(
        body, name=name,
        in_specs=[any_spec] * n, out_specs=[any_spec] * n,
        out_shape=[jax.ShapeDtypeStruct((NDEV,) + a.shape, a.dtype) for a in arrs],
        scratch_shapes=[pltpu.SemaphoreType.DMA((7 * n,)), pltpu.SemaphoreType.DMA((7 * n,)),
                        pltpu.SemaphoreType.DMA((n,))],
    )(*arrs)


class _Direct:
    SEMS = (pltpu.SemaphoreType.DMA((7,)), pltpu.SemaphoreType.DMA((7,)), pltpu.SemaphoreType.DMA((1,)))

    def __init__(self, src_ref, recv_ref, sems, gather):
        x, y, c = lax.axis_index("x"), lax.axis_index("y"), lax.axis_index("c")
        me = 4 * x + 2 * y + c
        send_sems, recv_sems, local_sem = sems
        src = (lambda pid: src_ref) if gather else (lambda pid: src_ref.at[pid])
        self.mine = pltpu.make_async_copy(src(me), recv_ref.at[me], local_sem.at[0])
        self.copies = []
        for k in range(1, NDEV):
            px = 1 - x if k & 4 else x
            py = 1 - y if k & 2 else y
            pc = 1 - c if k & 1 else c
            self.copies.append(pltpu.make_async_remote_copy(
                src_ref=src(4 * px + 2 * py + pc), dst_ref=recv_ref.at[me],
                send_sem=send_sems.at[k - 1], recv_sem=recv_sems.at[k - 1],
                device_id=(px, py, pc), device_id_type=MESH))

    def start(self):
        self.mine.start()
        for cp in self.copies:
            cp.start()

    def wait(self):
        for cp in self.copies:
            cp.wait_recv()
        for cp in self.copies:
            cp.wait_send()
        self.mine.wait()


def _recv_shape(src, gather):
    return jax.ShapeDtypeStruct(((NDEV,) + src.shape) if gather else src.shape, src.dtype)


def _exchange(arrs, gathers, *, name):
    n = len(arrs)

    def body(*refs):
        sems = refs[2 * n:]
        ex = [_Direct(refs[a], refs[n + a], sems[3 * a:3 * a + 3], gathers[a]) for a in range(n)]
        for e in ex:
            e.start()
        for e in ex:
            e.wait()

    any_spec = pl.BlockSpec(memory_space=pl.ANY)
    return pl.pallas_call(
        body, name=name, in_specs=[any_spec] * n, out_specs=[any_spec] * n,
        out_shape=[_recv_shape(a, g) for a, g in zip(arrs, gathers)],
        scratch_shapes=list(_Direct.SEMS) * n,
    )(*arrs)


def _adamw(parts, w, m, v, *, name):
    r = w.shape[0]
    tm = _tile(r, 256, PACK_ALIGN)
    c1 = 1.0 - ADAM_B1 ** ADAM_STEP
    c2 = 1.0 - ADAM_B2 ** ADAM_STEP

    def body(p_ref, w_ref, m_ref, v_ref, g_ref, d_ref, nm_ref, nv_ref):
        g = p_ref[0].astype(F32)
        for i in range(1, NDEV):
            g = g + p_ref[i].astype(F32)
        nm = ADAM_B1 * m_ref[...] + (1.0 - ADAM_B1) * g
        nv = ADAM_B2 * v_ref[...] + (1.0 - ADAM_B2) * (g * g)
        d_ref[...] = -ADAM_LR * ((nm / c1) / (jnp.sqrt(nv / c2) + ADAM_EPS) + ADAM_WD * w_ref[...])
        g_ref[...] = g
        nm_ref[...] = nm
        nv_ref[...] = nv

    row = pl.BlockSpec((tm, D), lambda i: (i, 0))
    return pl.pallas_call(
        body, name=name, grid=(r // tm,),
        in_specs=[pl.BlockSpec((NDEV, tm, D), lambda i: (0, i, 0)), row, row, row],
        out_specs=[row] * 4, out_shape=[jax.ShapeDtypeStruct((r, D), F32)] * 4,
        compiler_params=_cp("parallel"),
    )(parts, w, m, v)


PACK_ALIGN = 16


def _part_rows(shape):
    n = -(-math.prod(shape) // D)
    return n + (-n) % PACK_ALIGN


def _rows(a):
    flat = a.reshape(-1)
    pad = _part_rows(a.shape) * D - flat.shape[0]
    if pad:
        flat = jnp.concatenate([flat, jnp.zeros((pad,), flat.dtype)])
    return flat.reshape(-1, D)


def _pack(parts, total_rows):
    rows = [_rows(p) for p in parts]
    used = sum(r.shape[0] for r in rows)
    if total_rows > used:
        rows.append(jnp.zeros((total_rows - used, D), rows[0].dtype))
    return jnp.concatenate(rows, axis=0)


def _unpack(buf, shapes):
    out, r0 = [], 0
    for shp in shapes:
        n = math.prod(shp)
        out.append(buf[r0:r0 + _part_rows(shp)].reshape(-1)[:n].reshape(shp))
        r0 += _part_rows(shp)
    return out


SHARD = {"w_in": (D, 706), "w_out": (256, D), "w_mq": (128, D), "w_mk": (128, D), "w_mv": (128, D),
         "w_mo": (128, D), "w_up": (D, 704), "w_down": (352, D), "conv_ssd_w": (4, 192), "conv_ffn_w": (3, 704)}
GATHER_IN, GATHER_MID, GATHER_FFN = ["w_in"], ["w_out", "w_mq", "w_mk", "w_mv", "w_mo"], ["w_up", "w_down"]
CONV_TAPS = ["conv_ssd_w", "conv_ffn_w"]
GRADS_A = ["w_out", "w_mq", "w_mk", "w_mv", "w_mo", "w_up", "w_down", "conv_ffn_w"]
GRADS_B = ["w_in", "conv_ssd_w"]


def _layout(names):
    row0, r = {}, 0
    for n in names:
        row0[n] = r
        r += _part_rows(SHARD[n])
    return row0, r + (-r) % 128


SMALL = [("norm_mix_w", (1, D)), ("conv_ssd_b", (1, 1536)), ("dt_bias", (1, 16)), ("a_log", (1, 16)),
         ("d_skip", (1, 16)), ("ssd_norm_w", (1, D)), ("sb_norm_w", (1, D)), ("norm_mem_w", (1, D)),
         ("norm_memkv_w", (1, D)), ("norm_ffn_w", (1, D)), ("conv_ffn_b", (1, 5632)), ("norm_final_w", (D,))]
LOSS_ROW = sum(_part_rows(_shp) for _, _shp in SMALL)
SMALL_ROWS = LOSS_ROW + PACK_ALIGN
ORDER = ["norm_mix_w", "w_in", "conv_ssd_w", "conv_ssd_b", "dt_bias", "a_log", "d_skip", "ssd_norm_w",
         "sb_norm_w", "w_out", "norm_mem_w", "norm_memkv_w", "w_mq", "w_mk", "w_mv", "w_mo", "norm_ffn_w",
         "w_up", "conv_ffn_w", "conv_ffn_b", "w_down", "norm_final_w"]


def _gather_cols(g, r0, nr, rows, cols):
    t = g[:, r0:r0 + nr].reshape(NDEV, rows, cols)
    return t.transpose(1, 0, 2).reshape(rows, NDEV * cols)


def _scatter_cols(full, cols):
    rows = full.shape[0]
    return full.reshape(rows, NDEV, cols).transpose(1, 0, 2).reshape(NDEV, -1, D)


def _pad_rows(a, nr):
    n = a.shape[1]
    return jnp.concatenate([a, jnp.zeros((NDEV, nr * D - n), a.dtype)], axis=1).reshape(NDEV, nr, D)


def _group_sum(lanes):
    return lanes.reshape(16, 64).sum(axis=1).reshape(1, 16)


def kernel(x, mem, norm_mix_w, w_in, conv_ssd_w, conv_ssd_b, dt_bias, a_log, d_skip, ssd_norm_w, sb_norm_w, w_out, norm_mem_w, norm_memkv_w, w_mq, w_mk, w_mv, w_mo, norm_ffn_w, w_up, conv_ffn_w, conv_ffn_b, w_down, norm_final_w, loss_target, m_norm_mix_w, m_w_in, m_conv_ssd_w, m_conv_ssd_b, m_dt_bias, m_a_log, m_d_skip, m_ssd_norm_w, m_sb_norm_w, m_w_out, m_norm_mem_w, m_norm_memkv_w, m_w_mq, m_w_mk, m_w_mv, m_w_mo, m_norm_ffn_w, m_w_up, m_conv_ffn_w, m_conv_ffn_b, m_w_down, m_norm_final_w, v_norm_mix_w, v_w_in, v_conv_ssd_w, v_conv_ssd_b, v_dt_bias, v_a_log, v_d_skip, v_ssd_norm_w, v_sb_norm_w, v_w_out, v_norm_mem_w, v_norm_memkv_w, v_w_mq, v_w_mk, v_w_mv, v_w_mo, v_norm_ffn_w, v_w_up, v_conv_ffn_w, v_conv_ffn_b, v_w_down, v_norm_final_w):
    P = dict(norm_mix_w=norm_mix_w, w_in=w_in, conv_ssd_w=conv_ssd_w, conv_ssd_b=conv_ssd_b, dt_bias=dt_bias, a_log=a_log, d_skip=d_skip, ssd_norm_w=ssd_norm_w, sb_norm_w=sb_norm_w, w_out=w_out, norm_mem_w=norm_mem_w, norm_memkv_w=norm_memkv_w, w_mq=w_mq, w_mk=w_mk, w_mv=w_mv, w_mo=w_mo, norm_ffn_w=norm_ffn_w, w_up=w_up, conv_ffn_w=conv_ffn_w, conv_ffn_b=conv_ffn_b, w_down=w_down, norm_final_w=norm_final_w)
    M = dict(norm_mix_w=m_norm_mix_w, w_in=m_w_in, conv_ssd_w=m_conv_ssd_w, conv_ssd_b=m_conv_ssd_b, dt_bias=m_dt_bias, a_log=m_a_log, d_skip=m_d_skip, ssd_norm_w=m_ssd_norm_w, sb_norm_w=m_sb_norm_w, w_out=m_w_out, norm_mem_w=m_norm_mem_w, norm_memkv_w=m_norm_memkv_w, w_mq=m_w_mq, w_mk=m_w_mk, w_mv=m_w_mv, w_mo=m_w_mo, norm_ffn_w=m_norm_ffn_w, w_up=m_w_up, conv_ffn_w=m_conv_ffn_w, conv_ffn_b=m_conv_ffn_b, w_down=m_w_down, norm_final_w=m_norm_final_w)
    V = dict(norm_mix_w=v_norm_mix_w, w_in=v_w_in, conv_ssd_w=v_conv_ssd_w, conv_ssd_b=v_conv_ssd_b, dt_bias=v_dt_bias, a_log=v_a_log, d_skip=v_d_skip, ssd_norm_w=v_ssd_norm_w, sb_norm_w=v_sb_norm_w, w_out=v_w_out, norm_mem_w=v_norm_mem_w, norm_memkv_w=v_norm_memkv_w, w_mq=v_w_mq, w_mk=v_w_mk, w_mv=v_w_mv, w_mo=v_w_mo, norm_ffn_w=v_norm_ffn_w, w_up=v_w_up, conv_ffn_w=v_conv_ffn_w, conv_ffn_b=v_conv_ffn_b, w_down=v_w_down, norm_final_w=v_norm_final_w)
    small_shapes = [shp for _, shp in SMALL]

    def packed(src, names, dtype=F32):
        return _pack([src[n][0] for n in names], _layout(names)[1]).astype(dtype)

    g_in, g_taps = _all_gather([packed(P, GATHER_IN, BF16), packed(P, CONV_TAPS)], name="gather_w_in")
    W_in = _gather_cols(g_in, 0, 706, D, 706)
    cw_ssd = g_taps[:, 0].reshape(NDEV, -1)[:, :768].reshape(NDEV, 4, 192).transpose(1, 0, 2).reshape(4, XBC)
    cw_ffn = (g_taps[:, PACK_ALIGN:PACK_ALIGN + 3].reshape(NDEV, -1)[:, :2112].reshape(NDEV, 3, 704)
              .transpose(1, 0, 2).reshape(3, 2 * DFF))
    W_z, W_xbc, W_dt, W_qkv = W_in[:, :D], W_in[:, D:D + XBC], W_in[:, D + XBC:D + XBC + 16], W_in[:, D + XBC + 16:]
    W_dtr = jnp.repeat(W_dt, 64, axis=1)
    cwg, cwv = cw_ffn[:, :DFF], cw_ffn[:, DFF:]
    cbg, cbv = conv_ffn_b[:, :DFF], conv_ffn_b[:, DFF:]
    rep = lambda p: jnp.repeat(p, 64, axis=1)
    lanes = jnp.concatenate([rep(dt_bias), rep(a_log), rep(d_skip), ssd_norm_w, jnp.zeros((4, D), F32)], axis=0)

    xs, tgt, mm = x[0], loss_target[0], mem[0]

    h1 = _norm_fwd(xs, norm_mix_w, name="norm_mix")
    z = _mm(h1, W_z, name="proj_z")
    xbc = _mm(h1, W_xbc, name="proj_xbc")
    dtr = _mm(h1, W_dtr, name="proj_dt")
    qkv = _mm(h1, W_qkv, name="proj_qkv", out_dtype=BF16)
    y_ssd, states, g_ffn = _ssd_fwd(z, xbc, dtr, cw_ssd, conv_ssd_b, lanes, packed(P, GATHER_FFN, BF16),
                                    name="ssd_fwd")
    o_sb, g_mid = _sb_fwd(qkv, packed(P, GATHER_MID, BF16), name="sb_fwd")
    r_mid, r_ffn = _layout(GATHER_MID)[0], _layout(GATHER_FFN)[0]
    W_out = g_mid[:, r_mid["w_out"]:r_mid["w_out"] + 256].reshape(2 * D, D)
    W_mq, W_mk, W_mv, W_mo = [g_mid[:, r_mid[n]:r_mid[n] + 128].reshape(D, D)
                              for n in ("w_mq", "w_mk", "w_mv", "w_mo")]
    W_up = _gather_cols(g_ffn, r_ffn["w_up"], 704, D, 704)
    W_down = g_ffn[:, r_ffn["w_down"]:r_ffn["w_down"] + 352].reshape(DFF, D)
    W_upg, W_upv = W_up[:, :DFF], W_up[:, DFF:]
    y_sb = _head_norm_fwd(o_sb, sb_norm_w, name="sb_norm")
    ymix = jnp.concatenate([y_ssd, y_sb], axis=1)
    x1 = _mm(ymix, W_out, add=xs, name="proj_out")
    h2 = _norm_fwd(x1, norm_mem_w, name="norm_mem")
    mn = _norm_fwd(mm, norm_memkv_w, name="norm_memkv")
    qm = _mm(h2, W_mq, name="mem_q", out_dtype=BF16)
    km = _mm(mn, W_mk, name="mem_k", out_dtype=BF16)
    vm = _mm(mn, W_mv, name="mem_v", out_dtype=BF16)
    om = _mem_attn_fwd(qm, km, vm, name="mem_attn")
    x2 = _mm(om, W_mo, add=x1, name="mem_o")
    h3 = _norm_fwd(x2, norm_ffn_w, name="norm_ffn")
    ug = _mm(h3, W_upg, name="ffn_up_g")
    uv = _mm(h3, W_upv, name="ffn_up_v")
    f = _glu_fwd(ug, uv, cwg, cwv, cbg, cbv, name="ffn_glu")
    x3 = _mm(f, W_down, add=x2, name="ffn_down")
    dx3, g_nfinal, loss_part = _final(x3, norm_final_w.reshape(1, D), tgt, name="final_loss")

    G = {}
    G["w_down"] = _mm(f, dx3, trans_a=True, name="g_w_down")
    df = _mm(dx3, W_down.T, name="d_f")
    dug, duv = _glu_bwd(ug, uv, df, cwg, cwv, cbg, cbv, name="ffn_glu_bwd")
    dupg, dcg = _ffn_conv_bwd(dug, ug, cwg, name="ffn_conv_bwd_g")
    dupv, dcv = _ffn_conv_bwd(duv, uv, cwv, name="ffn_conv_bwd_v")
    G["w_up"] = jnp.concatenate([_mm(h3, dupg, trans_a=True, name="g_w_up_g"),
                                 _mm(h3, dupv, trans_a=True, name="g_w_up_v")], axis=1)
    G["conv_ffn_w"] = jnp.concatenate([dcg[0:3], dcv[0:3]], axis=1)
    G["conv_ffn_b"] = jnp.concatenate([dcg[3:4], dcv[3:4]], axis=1)
    dh3 = _mm(dupg, W_upg.T, name="d_h3_g")
    dh3 = _mm(dupv, W_upv.T, add=dh3, name="d_h3_v")
    dx2, G["norm_ffn_w"] = _norm_bwd(x2, norm_ffn_w, dh3, dx3, name="norm_ffn_bwd")
    G["w_mo"] = _mm(om, dx2, trans_a=True, name="g_w_mo")
    dom = _mm(dx2, W_mo.T, name="d_om")
    dqm, dkm, dvm = _mem_attn_bwd(qm, km, vm, dom, name="mem_attn_bwd")
    G["w_mq"] = _mm(h2, dqm, trans_a=True, name="g_w_mq")
    G["w_mk"] = _mm(mn, dkm, trans_a=True, name="g_w_mk")
    G["w_mv"] = _mm(mn, dvm, trans_a=True, name="g_w_mv")
    dh2 = _mm(dqm, W_mq.T, name="d_h2")
    dmn = _mm(dkm, W_mk.T, name="d_mn_k")
    dmn = _mm(dvm, W_mv.T, add=dmn, name="d_mn_v")
    _, G["norm_memkv_w"] = _norm_bwd(mm, norm_memkv_w, dmn, None, name="norm_memkv_bwd")
    dx1, G["norm_mem_w"] = _norm_bwd(x1, norm_mem_w, dh2, dx2, name="norm_mem_bwd")
    G["w_out"] = _mm(ymix, dx1, trans_a=True, name="g_w_out")
    dymix = _mm(dx1, W_out.T, name="d_ymix")
    do_sb, G["sb_norm_w"] = _head_norm_bwd(o_sb, sb_norm_w, dymix, name="sb_norm_bwd")

    def slabs(names):
        parts = []
        for n in names:
            shp = SHARD[n]
            if shp[-1] == D:
                t = G[n].reshape((NDEV,) + shp)
            elif shp[0] == D:
                t = _scatter_cols(G[n], shp[1])
            else:
                t = _pad_rows(G[n].reshape(shp[0], NDEV, shp[1]).transpose(1, 0, 2).reshape(NDEV, -1),
                              _part_rows(shp))
            parts.append(jnp.pad(t, ((0, 0), (0, _part_rows(shp) - t.shape[1]), (0, 0))))
        used = sum(t.shape[1] for t in parts)
        parts.append(jnp.zeros((NDEV, _layout(names)[1] - used, D), F32))
        return jnp.concatenate(parts, axis=1).astype(BF16)

    dq, dk, dv, recv_a = _sb_bwd(qkv, o_sb, do_sb, slabs(GRADS_A), name="sb_bwd")
    dqkv = jnp.concatenate([dq, dk, dv], axis=1)
    dz, dxbc, ddtr, dlanes, dconv = _ssd_bwd(z, xbc, dtr, states, dymix, cw_ssd, conv_ssd_b, lanes, name="ssd_bwd")
    G["dt_bias"], G["a_log"], G["d_skip"] = [_group_sum(dlanes[i:i + 1]) for i in range(3)]
    G["ssd_norm_w"] = dlanes[3:4]
    G["conv_ssd_w"], G["conv_ssd_b"] = dconv[0:4], dconv[4:5]
    g_wdt = _mm(h1, ddtr, trans_a=True, name="g_w_dt").reshape(D, 16, 64).sum(axis=2)
    G["w_in"] = jnp.concatenate([_mm(h1, dz, trans_a=True, name="g_w_z"),
                                 _mm(h1, dxbc, trans_a=True, name="g_w_xbc"), g_wdt,
                                 _mm(h1, dqkv, trans_a=True, name="g_w_qkv")], axis=1)
    dh1 = _mm(dz, W_z.T, name="d_h1_z")
    dh1 = _mm(dxbc, W_xbc.T, add=dh1, name="d_h1_xbc")
    dh1 = _mm(ddtr, W_dtr.T, add=dh1, name="d_h1_dt")
    dh1 = _mm(dqkv, W_qkv.T, add=dh1, name="d_h1_qkv")
    dx, G["norm_mix_w"] = _norm_bwd(xs, norm_mix_w, dh1, dx1, name="norm_mix_bwd")
    G["norm_final_w"] = g_nfinal.reshape(D)

    small_g = _pack([G[n] for n, _ in SMALL] + [loss_part], SMALL_ROWS)
    recv_b, parts_small = _exchange([slabs(GRADS_B), small_g], [False, True], name="exchange_grads")
    outs_a = _adamw(recv_a, packed(P, GRADS_A), packed(M, GRADS_A), packed(V, GRADS_A), name="adamw_a")
    outs_b = _adamw(recv_b, packed(P, GRADS_B), packed(M, GRADS_B), packed(V, GRADS_B), name="adamw_b")
    outs_small = _adamw(parts_small, _pack([P[n] for n, _ in SMALL], SMALL_ROWS),
                        _pack([M[n] for n, _ in SMALL], SMALL_ROWS),
                        _pack([V[n] for n, _ in SMALL], SMALL_ROWS), name="adamw_replicated")

    res = {}
    for i, kind in enumerate(("grad", "delta", "new_m", "new_v")):
        for names, outs in ((GRADS_A, outs_a), (GRADS_B, outs_b)):
            for n, val in zip(names, _unpack(outs[i], [SHARD[n] for n in names])):
                res[kind, n] = val.reshape((1,) + SHARD[n])
        for (n, shp), val in zip(SMALL, _unpack(outs_small[i], small_shapes)):
            res[kind, n] = val
    loss = outs_small[0][LOSS_ROW, 0]
    out = [loss, dx.reshape(1, -1, D)]
    for kind in ("grad", "delta", "new_m", "new_v"):
        out += [res[kind, n] for n in ORDER]
    return tuple(out)
```

```python
import functools
import math

import jax
import jax.numpy as jnp
from jax import lax
from jax.experimental import pallas as pl
from jax.experimental.pallas import tpu as pltpu

F32 = jnp.float32
BF16 = jnp.bfloat16

D = 1024
NDEV = 8
EPS = 1e-6
SSD_CHUNK = 128
HALO = 8
VMEM_LIMIT = 56 * 2**20

ADAM_LR, ADAM_B1, ADAM_B2, ADAM_EPS, ADAM_WD, ADAM_STEP = 0.001, 0.9, 0.999, 1e-08, 0.01, 10


def _cp(*sem):
    return pltpu.CompilerParams(dimension_semantics=sem, vmem_limit_bytes=VMEM_LIMIT)


def _tile(n, cap, mult):
    if n <= cap:
        return n
    for d in range(cap - cap % mult, 0, -mult):
        if n % d == 0:
            return d
    raise ValueError(f"no tile for {n}")


def _sigmoid(x):
    return 1.0 / (1.0 + jnp.exp(-x))


def _silu(x):
    return x * _sigmoid(x)


def _softplus(x):
    return jnp.maximum(x, 0.0) + jnp.log1p(jnp.exp(-jnp.abs(x)))


def _terms(x, n):
    out = []
    r = x.astype(F32)
    for i in range(n):
        h = r.astype(BF16)
        out.append(h)
        if i + 1 < n:
            r = r - h.astype(F32)
    return out


_DIMS = {"nn": ((1,), (0,)), "nt": ((1,), (1,)), "tn": ((0,), (0,))}


def _dot_raw(form, a, b, ta, tb):
    acc = None
    for ai in _terms(a, ta):
        for bi in _terms(b, tb):
            d = lax.dot_general(ai, bi, (_DIMS[form], ((), ())), preferred_element_type=F32)
            acc = d if acc is None else acc + d
    return acc


@functools.lru_cache(maxsize=None)
def _dot_fn(form, ta, tb):
    @jax.custom_vjp
    def f(a, b):
        return _dot_raw(form, a, b, ta, tb)

    def fwd(a, b):
        return f(a, b), (a, b)

    def bwd(res, ct):
        a, b = res
        if form == "nn":
            return _dot_fn("nt", ta, tb)(ct, b), _dot_fn("tn", ta, tb)(a, ct)
        if form == "nt":
            return _dot_fn("nn", ta, tb)(ct, b), _dot_fn("tn", tb, ta)(ct, a)
        return _dot_fn("nt", tb, ta)(b, ct), _dot_fn("nn", ta, tb)(a, ct)

    f.defvjp(fwd, bwd)
    return f


def _dot(form, a, b, ta=1, tb=1):
    return _dot_fn(form, ta, tb)(a, b)


@functools.lru_cache(maxsize=None)
def _take_fn(axis, idx):
    @jax.custom_vjp
    def f(x):
        return x[:, idx:idx + 1] if axis == 1 else x[idx:idx + 1, :]

    def fwd(x):
        return f(x), x.shape

    def bwd(shape, ct):
        io = lax.broadcasted_iota(jnp.int32, shape, axis)
        return (jnp.where(io == idx, jnp.broadcast_to(ct, shape), 0.0),)

    f.defvjp(fwd, bwd)
    return f


@functools.lru_cache(maxsize=None)
def _split_fn(width, n):
    @jax.custom_vjp
    def f(x):
        return tuple(x[:, i * width:(i + 1) * width] for i in range(n))

    def fwd(x):
        return f(x), None

    def bwd(_, cts):
        return (jnp.concatenate(list(cts), axis=1),)

    f.defvjp(fwd, bwd)
    return f


def _split(x, width):
    return _split_fn(width, x.shape[1] // width)(x)


def _iota(shape, axis):
    return lax.broadcasted_iota(jnp.int32, shape, axis)


MM_VMEM_BUDGET = 36 * 2**20


def _mm_tiles(m, n, kt, trans_a, a_bytes, b_bytes, out_bytes, add_bytes):
    tn = _tile(n, 1536, 128)
    for tm_cap in (1408, 1024, 512, 256, 128):
        tm = _tile(m, tm_cap, 128 if trans_a else 8)
        for tk_cap in (kt, 4096, 2048, 1024, 512):
            tk = _tile(kt, tk_cap, 128)
            blocks = tm * tk * a_bytes + tk * tn * b_bytes + tm * tn * (out_bytes + add_bytes)
            if 2 * blocks + (tm * tn * 4 if tk < kt else 0) <= MM_VMEM_BUDGET:
                return tm, tn, tk
    raise ValueError(f"no matmul tiling for {(m, n, kt)}")


def _mm(a, b, *, name, add=None, trans_a=False, out_dtype=F32):
    if trans_a:
        kt, m = a.shape
    else:
        m, kt = a.shape
    kt2, n = b.shape
    assert kt == kt2, (a.shape, b.shape)
    tm, tn, tk = _mm_tiles(m, n, kt, trans_a, a.dtype.itemsize, b.dtype.itemsize,
                           jnp.dtype(out_dtype).itemsize, 0 if add is None else add.dtype.itemsize)
    nk = kt // tk

    def body(*refs):
        if add is None:
            a_ref, b_ref, o_ref = refs[:3]
        else:
            a_ref, b_ref, add_ref, o_ref = refs[:4]
        k = pl.program_id(2)
        av = a_ref[...].astype(BF16)
        bv = b_ref[...].astype(BF16)
        dims = _DIMS["tn" if trans_a else "nn"]
        d = lax.dot_general(av, bv, (dims, ((), ())), preferred_element_type=F32)

        def finish(r):
            if add is not None:
                r = r + add_ref[...]
            o_ref[...] = r.astype(out_dtype)

        if nk == 1:
            finish(d)
        else:
            acc = refs[-1]

            @pl.when(k == 0)
            def _():
                acc[...] = d

            @pl.when((k > 0) & (k < nk - 1))
            def _():
                acc[...] += d

            @pl.when(k == nk - 1)
            def _():
                finish(acc[...] + d)

    a_spec = (pl.BlockSpec((tk, tm), lambda i, j, k: (k, i)) if trans_a
              else pl.BlockSpec((tm, tk), lambda i, j, k: (i, k)))
    in_specs = [a_spec, pl.BlockSpec((tk, tn), lambda i, j, k: (k, j))]
    args = [a, b]
    if add is not None:
        in_specs.append(pl.BlockSpec((tm, tn), lambda i, j, k: (i, j)))
        args.append(add)
    return pl.pallas_call(
        body, name=name, grid=(m // tm, n // tn, nk),
        in_specs=in_specs, out_specs=pl.BlockSpec((tm, tn), lambda i, j, k: (i, j)),
        out_shape=jax.ShapeDtypeStruct((m, n), out_dtype),
        scratch_shapes=[pltpu.VMEM((tm, tn), F32)] if nk > 1 else [],
        compiler_params=_cp("parallel", "parallel", "arbitrary"),
    )(*args)


def _rstd(x):
    return lax.rsqrt(jnp.mean(x * x, axis=-1, keepdims=True) + EPS)


def _norm_fwd(x, w, *, name):
    s = x.shape[0]
    tm = _tile(s, 512, 8)

    def body(x_ref, w_ref, o_ref):
        xv = x_ref[...]
        o_ref[...] = (xv * _rstd(xv) * w_ref[...]).astype(BF16)

    return pl.pallas_call(
        body, name=name, grid=(s // tm,),
        in_specs=[pl.BlockSpec((tm, D), lambda i: (i, 0)), pl.BlockSpec((1, D), lambda i: (0, 0))],
        out_specs=pl.BlockSpec((tm, D), lambda i: (i, 0)),
        out_shape=jax.ShapeDtypeStruct((s, D), BF16), compiler_params=_cp("parallel"),
    )(x, w)


def _norm_bwd_math(xv, wv, dy):
    r = _rstd(xv)
    xh = xv * r
    dxh = dy * wv
    dx = r * (dxh - xh * jnp.mean(dxh * xh, axis=-1, keepdims=True))
    dw = jnp.sum(dy * xh, axis=0, keepdims=True)
    return dx, dw


def _norm_bwd(x, w, dy, add, *, name):
    s = x.shape[0]
    tm = _tile(s, 256, 8)

    def body(*refs):
        if add is None:
            x_ref, w_ref, dy_ref, dx_ref, dw_ref = refs
        else:
            x_ref, w_ref, dy_ref, add_ref, dx_ref, dw_ref = refs

        @pl.when(pl.program_id(0) == 0)
        def _():
            dw_ref[...] = jnp.zeros_like(dw_ref)

        dx, dw = _norm_bwd_math(x_ref[...], w_ref[...], dy_ref[...])
        if add is not None:
            dx = dx + add_ref[...]
        dx_ref[...] = dx
        dw_ref[...] += dw

    row = pl.BlockSpec((tm, D), lambda i: (i, 0))
    vec = pl.BlockSpec((1, D), lambda i: (0, 0))
    in_specs = [row, vec, row] + ([row] if add is not None else [])
    args = [x, w, dy] + ([add] if add is not None else [])
    return pl.pallas_call(
        body, name=name, grid=(s // tm,), in_specs=in_specs, out_specs=[row, vec],
        out_shape=[jax.ShapeDtypeStruct((s, D), F32), jax.ShapeDtypeStruct((1, D), F32)],
        compiler_params=_cp("arbitrary"),
    )(*args)


def _final(x3, w, target, *, name):
    s = x3.shape[0]
    tm = _tile(s, 256, 8)

    def body(x_ref, w_ref, t_ref, dx_ref, dw_ref, loss_ref):
        @pl.when(pl.program_id(0) == 0)
        def _():
            dw_ref[...] = jnp.zeros_like(dw_ref)
            loss_ref[...] = jnp.zeros_like(loss_ref)

        xv = x_ref[...]
        wv = w_ref[...]
        y = xv * _rstd(xv) * wv
        err = y - t_ref[...]
        loss_ref[...] += 0.5 * jnp.sum(jnp.mean(err * err, axis=-1, keepdims=True))
        dx, dw = _norm_bwd_math(xv, wv, err * (1.0 / D))
        dx_ref[...] = dx
        dw_ref[...] += dw

    row = pl.BlockSpec((tm, D), lambda i: (i, 0))
    vec = pl.BlockSpec((1, D), lambda i: (0, 0))
    return pl.pallas_call(
        body, name=name, grid=(s // tm,), in_specs=[row, vec, row], out_specs=[row, vec, vec],
        out_shape=[jax.ShapeDtypeStruct((s, D), F32), jax.ShapeDtypeStruct((1, D), F32),
                   jax.ShapeDtypeStruct((1, D), F32)],
        compiler_params=_cp("arbitrary"),
    )(x3, w, target)


def _head_norm_math(o, w):
    lane = _iota((128, 128), 0) // 64
    bd = (lane == _iota((128, 128), 1) // 64).astype(F32)
    outs = []
    for op in _split(o, 128):
        ms = _dot("nn", op * op, bd, 2, 1) * (1.0 / 64)
        outs.append(op * lax.rsqrt(ms + EPS))
    return jnp.concatenate(outs, axis=1) * w


def _head_norm_fwd(o, w, *, name):
    s = o.shape[0]
    tm = _tile(s, 256, 8)

    def body(o_ref, w_ref, y_ref):
        y_ref[...] = _head_norm_math(o_ref[...], w_ref[...]).astype(BF16)

    row = pl.BlockSpec((tm, D), lambda i: (i, 0))
    vec = pl.BlockSpec((1, D), lambda i: (0, 0))
    return pl.pallas_call(
        body, name=name, grid=(s // tm,), in_specs=[row, vec], out_specs=row,
        out_shape=jax.ShapeDtypeStruct((s, D), BF16), compiler_params=_cp("parallel"),
    )(o, w)


def _head_norm_bwd(o, w, dymix, *, name):
    s = o.shape[0]
    tm = _tile(s, 256, 8)

    def body(o_ref, w_ref, dy_ref, do_ref, dw_ref):
        @pl.when(pl.program_id(0) == 0)
        def _():
            dw_ref[...] = jnp.zeros_like(dw_ref)

        _, vjp = jax.vjp(_head_norm_math, o_ref[...], w_ref[...])
        do, dw = vjp(dy_ref[...])
        do_ref[...] = do
        dw_ref[...] += dw

    row = pl.BlockSpec((tm, D), lambda i: (i, 0))
    vec = pl.BlockSpec((1, D), lambda i: (0, 0))
    return pl.pallas_call(
        body, name=name, grid=(s // tm,),
        in_specs=[row, vec, pl.BlockSpec((tm, D), lambda i: (i, 1))], out_specs=[row, vec],
        out_shape=[jax.ShapeDtypeStruct((s, D), F32), jax.ShapeDtypeStruct((1, D), F32)],
        compiler_params=_cp("arbitrary"),
    )(o, w, dymix)


SB_BQ = 256
SB_BK = 256


def _sb_consts():
    r = _iota((SB_BK, SB_BK), 0)
    c = _iota((SB_BK, SB_BK), 1)
    u_excl = (r > c).astype(BF16)
    u_incl = (r >= c).astype(BF16)
    return u_excl, u_incl


SB_LANES = 256
SB_NCH = SB_LANES // 64


def _nt(a, b):
    return lax.dot_general(a, b, (_DIMS["nt"], ((), ())), preferred_element_type=F32)


def _tn(a, b):
    return lax.dot_general(a, b, (_DIMS["tn"], ((), ())), preferred_element_type=F32)


def _nn(a, b):
    return jnp.dot(a, b, preferred_element_type=F32)


def _sb_heads(ref):
    out = []
    for hp in range(SB_LANES // 128):
        v = ref[:, 128 * hp:128 * (hp + 1)]
        first = _iota(v.shape, 1) < 64
        out += [jnp.where(first, v, 0).astype(BF16), jnp.where(first, 0, v).astype(BF16)]
    return out


SB_STRIP = 32


def _neg_abs(x):
    bits = lax.bitcast_convert_type(x, jnp.uint32) | jnp.uint32(0x80000000)
    return lax.bitcast_convert_type(bits, F32)


def _sb_block(ref, j):
    off = pl.multiple_of(j * SB_BK, SB_BK)
    return [ref[pl.ds(off, SB_BK), 128 * hp:128 * (hp + 1)] for hp in range(SB_NCH // 2)]


SB_DEAD = 104.0


def _sb_live(nlrun):
    m = nlrun[0]
    for x in nlrun[1:]:
        m = jnp.minimum(m, x)
    return jnp.min(m) < SB_DEAD


def _sb_strips():
    return [(r, pl.ds(r, SB_STRIP)) for r in range(0, SB_BQ, SB_STRIP)]


def _sb_diag_mask(r):
    return _iota((SB_STRIP, SB_BK), 1) < _iota((SB_STRIP, SB_BK), 0) + r


def _sb_soft(z, mask):
    e = jnp.exp(_neg_abs(z))
    nl = jnp.maximum(z, 0.0) + jnp.log(1.0 + e)
    if mask is not None:
        nl = jnp.where(mask, nl, 0.0)
    return e, nl


def _sb_split_to(hl_ref, rows, x):
    hi, lo = _terms(x, 2)
    hl_ref[rows, 0:SB_BK] = hi
    hl_ref[rows, SB_BK:2 * SB_BK] = lo


def _sb_stage_soft(z_ref, nl_ref, diag):
    for r, rows in _sb_strips():
        _, nl = _sb_soft(z_ref[rows, :], _sb_diag_mask(r) if diag else None)
        nl_ref[rows, 0:SB_BK] = nl.astype(BF16)


def _sb_stage_weights(z_ref, c_ref, a_ref, nlrun, diag):
    for r, rows in _sb_strips():
        a = jnp.exp(z_ref[rows, :] - c_ref[rows, :] - nlrun[r:r + SB_STRIP, :])
        if diag:
            a = jnp.where(_sb_diag_mask(r), a, 0.0)
        a_ref[rows, :] = a.astype(BF16)


def _sb_fwd(qkv, rides, *, name):
    s = qkv.shape[0]
    nq = s // SB_BQ
    ng = D // SB_LANES
    assert SB_BQ == SB_BK
    rd = _Rides(rides, [True] * len(rides))

    def body(*refs):
        (q_ref, k_ref, v_ref), (o_ref,), (zbuf, nlbuf, cbuf, abuf), handles = rd.split(refs, 3, 1, 4)
        i = pl.program_id(1)
        step_no = pl.program_id(0) * nq + i
        rd.run(handles, step_no == 0, step_no == ng * nq - 1)

        _, u_incl = _sb_consts()
        lane_a = _iota((SB_BQ, 128), 1) < 64
        qh = [q * 0.125 for q in _sb_heads(q_ref)]

        def tile(j, accs, nlrun, diag):
            kbs = _sb_block(k_ref, j)
            for c in range(SB_NCH):
                zbuf[c] = _nt(qh[c], kbs[c // 2])
            for c in range(SB_NCH):
                _sb_stage_soft(zbuf.at[c], nlbuf.at[c], diag)
                cbuf[c] = _nn(nlbuf[c], u_incl)
            for c in range(SB_NCH):
                _sb_stage_weights(zbuf.at[c], cbuf.at[c], abuf.at[c], nlrun[c], diag)
            nlrun = tuple(nlrun[c] + cbuf[c, :, 0:1] for c in range(SB_NCH))
            vbs = _sb_block(v_ref, j)
            outs = [_nn(abuf[c], vbs[c // 2]) for c in range(SB_NCH)]
            accs = tuple(acc + jnp.where(lane_a, outs[2 * hp], outs[2 * hp + 1]) for hp, acc in enumerate(accs))
            return accs, nlrun

        accs, nlrun = tile(i, (jnp.zeros((SB_BQ, 128), F32),) * (SB_NCH // 2),
                           (jnp.zeros((SB_BQ, 1), F32),) * SB_NCH, True)

        def step(carry):
            j, _, accs, nlrun = carry
            accs, nlrun = tile(j, accs, nlrun, False)
            return j - 1, _sb_live(nlrun), accs, nlrun

        _, _, accs, _ = lax.while_loop(lambda c: (c[0] >= 0) & c[1], step, (i - 1, _sb_live(nlrun), accs, nlrun))
        o_ref[...] = jnp.concatenate(accs, axis=1)

    return pl.pallas_call(
        body, name=name, grid=(ng, nq),
        in_specs=[pl.BlockSpec((SB_BQ, SB_LANES), lambda g, i: (i, g)),
                  pl.BlockSpec((s, SB_LANES), lambda g, i: (0, ng + g)),
                  pl.BlockSpec((s, SB_LANES), lambda g, i: (0, 2 * ng + g)), *rd.in_specs],
        out_specs=[pl.BlockSpec((SB_BQ, SB_LANES), lambda g, i: (i, g)), *rd.out_specs],
        out_shape=[jax.ShapeDtypeStruct((s, D), F32), *rd.out_shape],
        scratch_shapes=[pltpu.VMEM((SB_NCH, SB_BQ, SB_BK), F32), pltpu.VMEM((SB_NCH, SB_BQ, SB_BK), BF16),
                        pltpu.VMEM((SB_NCH, SB_BQ, SB_BK), F32), pltpu.VMEM((SB_NCH, SB_BQ, SB_BK), BF16),
                        *rd.scratch],
        compiler_params=_cp("arbitrary", "arbitrary"),
    )(qkv, qkv, qkv, *rides)


def _sb_bwd(qkv, o, do, rides, *, name):
    s = qkv.shape[0]
    nq = s // SB_BQ
    ng = D // SB_LANES
    nhp = SB_NCH // 2
    rd = _Rides(rides, [False] * len(rides))

    def body(*refs):
        ins, outs, scratch, handles = rd.split(refs, 5, 3, 9)
        q_ref, k_ref, v_ref, o_ref, do_ref = ins
        dq_ref, dk_hbm, dv_hbm = outs
        dk_acc, dv_acc, sems, zbuf, gbuf, hl, cbuf, abuf, dzbuf = scratch
        g_idx = pl.program_id(0)
        i = pl.program_id(1)
        step_no = g_idx * nq + i
        rd.run(handles, step_no == 0, step_no == ng * nq - 1)

        @pl.when(i == 0)
        def _():
            dk_acc[...] = jnp.zeros_like(dk_acc)
            dv_acc[...] = jnp.zeros_like(dv_acc)

        _, u_incl = _sb_consts()
        u2 = jnp.concatenate([u_incl, u_incl], axis=0)
        lane_a = _iota((SB_BQ, 128), 1) < 64
        lane_k = _iota((SB_BK, 128), 1) < 64
        qh = [q * 0.125 for q in _sb_heads(q_ref)]
        qf = [q_ref[:, 128 * hp:128 * (hp + 1)] for hp in range(nhp)]
        doh = _sb_heads(do_ref)
        dof = [do_ref[:, 128 * hp:128 * (hp + 1)].astype(BF16) for hp in range(nhp)]
        delta = []
        for hp in range(nhp):
            prod = dof[hp].astype(F32) * o_ref[:, 128 * hp:128 * (hp + 1)]
            delta += [jnp.sum(jnp.where(lane_a, prod, 0.0), axis=1, keepdims=True),
                      jnp.sum(jnp.where(lane_a, 0.0, prod), axis=1, keepdims=True)]

        def pre(slot, j):
            kbs = _sb_block(k_ref, j)
            vbs = _sb_block(v_ref, j)
            for c in range(SB_NCH):
                zbuf[slot, c] = _nt(qh[c], kbs[c // 2])
                gbuf[slot, c] = _nt(doh[c], vbs[c // 2])

        def stage_g(c, slot):
            for _, rows in _sb_strips():
                g = abuf[slot, c, rows, :].astype(F32) * gbuf[slot, c, rows, :]
                gbuf[slot, c, rows, :] = g
                _sb_split_to(hl.at[c], rows, g)

        def stage_dz(c, slot, grun, diag):
            for r, rows in _sb_strips():
                z = zbuf[slot, c, rows, :]
                g = gbuf[slot, c, rows, :]
                cs = (delta[c] - grun)[r:r + SB_STRIP, :] - cbuf[c, rows, :]
                sig = 1.0 / (1.0 + jnp.exp(-z))
                dz = g - (g + cs) * sig
                if diag:
                    dz = jnp.where(_sb_diag_mask(r), dz, 0.0)
                dzbuf[slot, c, rows, :] = dz.astype(BF16)

        def chain(slot, nlrun, grun, diag):
            for c in range(SB_NCH):
                _sb_stage_soft(zbuf.at[slot, c], hl.at[c], diag)
                cbuf[c] = _nn(hl[c, :, 0:SB_BK], u_incl)
            nl_tot = []
            for c in range(SB_NCH):
                _sb_stage_weights(zbuf.at[slot, c], cbuf.at[c], abuf.at[slot, c], nlrun[c], diag)
                nl_tot.append(cbuf[c, :, 0:1])
                stage_g(c, slot)
                cbuf[c] = _nn(hl[c], u2)
            g_tot = []
            for c in range(SB_NCH):
                stage_dz(c, slot, grun[c], diag)
                g_tot.append(cbuf[c, :, 0:1])
            return (tuple(a + b for a, b in zip(nlrun, nl_tot)), tuple(a + b for a, b in zip(grun, g_tot)))

        def post(slot, j, dqs):
            off = pl.multiple_of(j * SB_BK, SB_BK)
            kbs = _sb_block(k_ref, j)
            dq_t = [_nn(dzbuf[slot, c], kbs[c // 2]) for c in range(SB_NCH)]
            dk_t = [_tn(dzbuf[slot, c], qf[c // 2]) for c in range(SB_NCH)]
            dv_t = [_tn(abuf[slot, c], dof[c // 2]) for c in range(SB_NCH)]
            for hp in range(nhp):
                cols = slice(128 * hp, 128 * (hp + 1))
                dk_acc[pl.ds(off, SB_BK), cols] += 0.125 * jnp.where(lane_k, dk_t[2 * hp], dk_t[2 * hp + 1])
                dv_acc[pl.ds(off, SB_BK), cols] += jnp.where(lane_k, dv_t[2 * hp], dv_t[2 * hp + 1])
            return tuple(dq + jnp.where(lane_a, dq_t[2 * hp], dq_t[2 * hp + 1]) for hp, dq in enumerate(dqs))

        def tile(j, dqs, nlrun, grun, diag):
            pre(0, j)
            nlrun, grun = chain(0, nlrun, grun, diag)
            return post(0, j, dqs), nlrun, grun

        zero = (jnp.zeros((SB_BQ, 1), F32),) * SB_NCH
        dqs, nlrun, grun = tile(i, (jnp.zeros((SB_BQ, 128), F32),) * nhp, zero, zero, True)

        def step(carry):
            j, _, dqs, nlrun, grun = carry
            dqs, nlrun, grun = tile(j, dqs, nlrun, grun, False)
            return j - 1, _sb_live(nlrun), dqs, nlrun, grun

        carry = lax.while_loop(lambda c: (c[0] >= 0) & c[1], step, (i - 1, _sb_live(nlrun), dqs, nlrun, grun))
        dq_ref[...] = 0.125 * jnp.concatenate(carry[2], axis=1)

        @pl.when(i == nq - 1)
        def _():
            cols = pl.ds(pl.multiple_of(g_idx * SB_LANES, SB_LANES), SB_LANES)
            ck = pltpu.make_async_copy(dk_acc, dk_hbm.at[:, cols], sems.at[0])
            cv = pltpu.make_async_copy(dv_acc, dv_hbm.at[:, cols], sems.at[1])
            ck.start()
            cv.start()
            ck.wait()
            cv.wait()

    qblk = pl.BlockSpec((SB_BQ, SB_LANES), lambda g, i: (i, g))
    hbm = pl.BlockSpec(memory_space=pl.ANY)
    return pl.pallas_call(
        body, name=name, grid=(ng, nq),
        in_specs=[qblk, pl.BlockSpec((s, SB_LANES), lambda g, i: (0, ng + g)),
                  pl.BlockSpec((s, SB_LANES), lambda g, i: (0, 2 * ng + g)), qblk, qblk, *rd.in_specs],
        out_specs=[qblk, hbm, hbm, *rd.out_specs],
        out_shape=[jax.ShapeDtypeStruct((s, D), F32)] * 3 + rd.out_shape,
        scratch_shapes=[pltpu.VMEM((s, SB_LANES), F32), pltpu.VMEM((s, SB_LANES), F32),
                        pltpu.SemaphoreType.DMA((2,)),
                        pltpu.VMEM((1, SB_NCH, SB_BQ, SB_BK), F32), pltpu.VMEM((1, SB_NCH, SB_BQ, SB_BK), F32),
                        pltpu.VMEM((SB_NCH, SB_BQ, 2 * SB_BK), BF16), pltpu.VMEM((SB_NCH, SB_BQ, SB_BK), F32),
                        pltpu.VMEM((1, SB_NCH, SB_BQ, SB_BK), BF16), pltpu.VMEM((1, SB_NCH, SB_BQ, SB_BK), BF16),
                        *rd.scratch],
        compiler_params=_cp("arbitrary", "arbitrary"),
    )(qkv, qkv, qkv, o, do, *rides)


def _ssd_core(z, xpre, dtr, state, dtb, alog, dsk, nw):
    L = SSD_CHUNK
    xa = _silu(xpre)
    pieces = _split(xa, 128)
    xs = jnp.concatenate(pieces[:8], axis=1)
    bm, cm = pieces[8:10], pieces[10:12]
    dt = _softplus(dtr + dtb)
    a = dt * (-jnp.exp(alog))
    tri = (_iota((L, L), 0) >= _iota((L, L), 1)).astype(F32)
    a_cs = _dot("nn", tri, a, 1, 3)
    xc = xs * dt
    tril = _iota((L, L), 0) >= _iota((L, L), 1)
    lane_a = _iota((L, 128), 1) < 64
    acs_p = _split(a_cs, 128)
    xc_p = _split(xc, 128)
    ys, new_states = [], []
    for g in range(2):
        cb = _dot("nt", cm[g], bm[g])
        for pp in range(4):
            pair = 4 * g + pp
            acs = acs_p[pair]
            acs_t = acs.T
            xcp = xc_p[pair]
            st = state[pair]
            heads = []
            for hh in range(2):
                col = _take_fn(1, 64 * hh)(acs)
                row = _take_fn(0, 64 * hh)(acs_t)
                seg = col - row
                lm = jnp.where(tril, jnp.exp(jnp.where(tril, seg, 0.0)), 0.0)
                heads.append(_dot("nn", cb * lm, xcp))
            ydiag = jnp.where(lane_a, heads[0], heads[1])
            last = _take_fn(0, L - 1)(acs)
            snew = _dot("tn", xcp * jnp.exp(last - acs), bm[g])
            new_states.append(st * jnp.exp(_take_fn(1, L - 1)(acs_t)) + snew)
            yoff = _dot("nt", cm[g], st) * jnp.exp(acs)
            ys.append(ydiag + yoff)
    y = jnp.concatenate(ys, axis=1) + xs * dsk
    yg = y * _silu(z)
    outs = []
    for v in _split(yg, 512):
        outs.append(v * lax.rsqrt(jnp.mean(v * v, axis=-1, keepdims=True) + EPS))
    return jnp.concatenate(outs, axis=1) * nw, tuple(new_states)


XBC = 1536


def _ssd_conv(ext_ref, cw, cb):
    acc = cb
    for k in range(4):
        acc = acc + cw[k:k + 1, :] * ext_ref[pl.ds(HALO - 3 + k, SSD_CHUNK), :]
    return acc


def _ssd_fwd(z, xbc, dtr, cw, cb, lanes, rides, *, name):
    s = z.shape[0]
    L = SSD_CHUNK
    nc = s // L
    rd = _Rides(rides, [True] * len(rides))

    def body(*refs):
        ins, (y_ref, st_ref), (state, ext), handles = rd.split(refs, 7, 2, 2)
        z_ref, x_ref, h_ref, dtr_ref, cw_ref, cb_ref, ln_ref = ins
        c = pl.program_id(0)
        rd.run(handles, c == 0, c == nc - 1)

        @pl.when(c == 0)
        def _():
            state[...] = jnp.zeros_like(state)

        ext[0:HALO, :] = jnp.where(c == 0, 0.0, h_ref[...])
        ext[HALO:, :] = x_ref[...]
        xpre = _ssd_conv(ext, cw_ref[...], cb_ref[...])
        st_ref[0] = state[...]
        st_in = tuple(state[p] for p in range(8))
        yn, st_out = _ssd_core(z_ref[...], xpre, dtr_ref[...], st_in,
                               ln_ref[0:1, :], ln_ref[1:2, :], ln_ref[2:3, :], ln_ref[3:4, :])
        y_ref[...] = yn.astype(BF16)
        for p in range(8):
            state[p] = st_out[p]

    return pl.pallas_call(
        body, name=name, grid=(nc,),
        in_specs=[pl.BlockSpec((L, D), lambda c: (c, 0)),
                  pl.BlockSpec((L, XBC), lambda c: (c, 0)),
                  pl.BlockSpec((HALO, XBC), lambda c: (jnp.maximum(c * (L // HALO) - 1, 0), 0)),
                  pl.BlockSpec((L, D), lambda c: (c, 0)),
                  pl.BlockSpec((4, XBC), lambda c: (0, 0)),
                  pl.BlockSpec((1, XBC), lambda c: (0, 0)),
                  pl.BlockSpec((8, D), lambda c: (0, 0)), *rd.in_specs],
        out_specs=[pl.BlockSpec((L, D), lambda c: (c, 0)),
                   pl.BlockSpec((1, 8, 128, 128), lambda c: (c, 0, 0, 0)), *rd.out_specs],
        out_shape=[jax.ShapeDtypeStruct((s, D), BF16), jax.ShapeDtypeStruct((nc, 8, 128, 128), F32),
                   *rd.out_shape],
        scratch_shapes=[pltpu.VMEM((8, 128, 128), F32), pltpu.VMEM((L + HALO, XBC), F32), *rd.scratch],
        compiler_params=_cp("arbitrary"),
    )(z, xbc, xbc, dtr, cw, cb, lanes, *rides)


def _ssd_bwd(z, xbc, dtr, states, dymix, cw, cb, lanes, *, name):
    s = z.shape[0]
    L = SSD_CHUNK
    nc = s // L

    def body(z_ref, x_ref, h_ref, dtr_ref, st_ref, dy_ref, cw_ref, cb_ref, ln_ref,
             dz_ref, dx_ref, ddt_ref, dln_ref, dcv_ref, dstate, ext, dext):
        i = pl.program_id(0)
        c = nc - 1 - i

        @pl.when(i == 0)
        def _():
            dstate[...] = jnp.zeros_like(dstate)
            dext[...] = jnp.zeros_like(dext)
            dln_ref[...] = jnp.zeros_like(dln_ref)
            dcv_ref[...] = jnp.zeros_like(dcv_ref)

        ext[0:HALO, :] = jnp.where(c == 0, 0.0, h_ref[...])
        ext[HALO:, :] = x_ref[...]
        cwv = cw_ref[...]
        xpre = _ssd_conv(ext, cwv, cb_ref[...])
        st_in = tuple(st_ref[0, p] for p in range(8))
        _, vjp = jax.vjp(_ssd_core, z_ref[...], xpre, dtr_ref[...], st_in,
                         ln_ref[0:1, :], ln_ref[1:2, :], ln_ref[2:3, :], ln_ref[3:4, :])
        dz, dxpre, ddtr, dst, d0, d1, d2, d3 = vjp((dy_ref[...], tuple(dstate[p] for p in range(8))))
        for p in range(8):
            dstate[p] = dst[p]
        dz_ref[...] = dz.astype(BF16)
        ddt_ref[...] = ddtr.astype(BF16)
        dln_ref[0:4, :] += jnp.concatenate([d0, d1, d2, d3], axis=0)
        dext[0:L, :] = dxpre
        dx = jnp.zeros((L, XBC), F32)
        rows = []
        for k in range(4):
            dx = dx + cwv[k:k + 1, :] * dext[pl.ds(3 - k, L), :]
            rows.append(jnp.sum(dxpre * ext[pl.ds(HALO - 3 + k, L), :], axis=0, keepdims=True))
        rows.append(jnp.sum(dxpre, axis=0, keepdims=True))
        dx_ref[...] = dx.astype(BF16)
        dcv_ref[0:5, :] += jnp.concatenate(rows, axis=0)
        dext[L:L + HALO, :] = dxpre[0:HALO, :]

    rev = lambda i: (nc - 1 - i, 0)
    return pl.pallas_call(
        body, name=name, grid=(nc,),
        in_specs=[pl.BlockSpec((L, D), rev),
                  pl.BlockSpec((L, XBC), rev),
                  pl.BlockSpec((HALO, XBC), lambda i: (jnp.maximum((nc - 1 - i) * (L // HALO) - 1, 0), 0)),
                  pl.BlockSpec((L, D), rev),
                  pl.BlockSpec((1, 8, 128, 128), lambda i: (nc - 1 - i, 0, 0, 0)),
                  pl.BlockSpec((L, D), rev),
                  pl.BlockSpec((4, XBC), lambda i: (0, 0)),
                  pl.BlockSpec((1, XBC), lambda i: (0, 0)),
                  pl.BlockSpec((8, D), lambda i: (0, 0))],
        out_specs=[pl.BlockSpec((L, D), rev), pl.BlockSpec((L, XBC), rev), pl.BlockSpec((L, D), rev),
                   pl.BlockSpec((8, D), lambda i: (0, 0)), pl.BlockSpec((8, XBC), lambda i: (0, 0))],
        out_shape=[jax.ShapeDtypeStruct((s, D), BF16), jax.ShapeDtypeStruct((s, XBC), BF16),
                   jax.ShapeDtypeStruct((s, D), BF16), jax.ShapeDtypeStruct((8, D), F32),
                   jax.ShapeDtypeStruct((8, XBC), F32)],
        scratch_shapes=[pltpu.VMEM((8, 128, 128), F32), pltpu.VMEM((L + HALO, XBC), F32),
                        pltpu.VMEM((L + HALO, XBC), F32)],
        compiler_params=_cp("arbitrary"),
    )(z, xbc, xbc, dtr, states, dymix, cw, cb, lanes)


def _mem_attn_math(q, k, v):
    outs = []
    for qh, kh, vh in zip(_split(q, 256), _split(k, 256), _split(v, 256)):
        sc = _dot("nt", qh, kh) * (1.0 / 16.0)
        e = jnp.exp(sc - lax.stop_gradient(jnp.max(sc, axis=-1, keepdims=True)))
        p = e / jnp.sum(e, axis=-1, keepdims=True)
        outs.append(_dot("nn", p, vh))
    return jnp.concatenate(outs, axis=1)


def _mem_attn_fwd(q, k, v, *, name):
    s, m = q.shape[0], k.shape[0]
    tm = _tile(s, 256, 8)

    def body(q_ref, k_ref, v_ref, o_ref):
        o_ref[...] = _mem_attn_math(q_ref[...].astype(F32), k_ref[...].astype(F32),
                                    v_ref[...].astype(F32)).astype(BF16)

    row = pl.BlockSpec((tm, D), lambda i: (i, 0))
    kv = pl.BlockSpec((m, D), lambda i: (0, 0))
    return pl.pallas_call(
        body, name=name, grid=(s // tm,), in_specs=[row, kv, kv], out_specs=row,
        out_shape=jax.ShapeDtypeStruct((s, D), BF16), compiler_params=_cp("parallel"),
    )(q, k, v)


def _mem_attn_bwd(q, k, v, do, *, name):
    s, m = q.shape[0], k.shape[0]
    tm = _tile(s, 256, 8)

    def body(q_ref, k_ref, v_ref, do_ref, dq_ref, dk_ref, dv_ref):
        @pl.when(pl.program_id(0) == 0)
        def _():
            dk_ref[...] = jnp.zeros_like(dk_ref)
            dv_ref[...] = jnp.zeros_like(dv_ref)

        _, vjp = jax.vjp(_mem_attn_math, q_ref[...].astype(F32), k_ref[...].astype(F32),
                         v_ref[...].astype(F32))
        dq, dk, dv = vjp(do_ref[...])
        dq_ref[...] = dq.astype(BF16)
        dk_ref[...] += dk
        dv_ref[...] += dv

    row = pl.BlockSpec((tm, D), lambda i: (i, 0))
    kv = pl.BlockSpec((m, D), lambda i: (0, 0))
    return pl.pallas_call(
        body, name=name, grid=(s // tm,), in_specs=[row, kv, kv, row], out_specs=[row, kv, kv],
        out_shape=[jax.ShapeDtypeStruct((s, D), BF16), jax.ShapeDtypeStruct((m, D), F32),
                   jax.ShapeDtypeStruct((m, D), F32)],
        compiler_params=_cp("arbitrary"),
    )(q, k, v, do)


DFF = 2816
FFN_TC = 1408
FFN_TM = 256


def _ffn_conv(ext_ref, cw, cb, tm):
    acc = cb
    for k in range(3):
        acc = acc + cw[k:k + 1, :] * ext_ref[pl.ds(HALO - 2 + k, tm), :]
    return acc


def _ffn_specs(s):
    tm, tc = FFN_TM, FFN_TC
    blk = pl.BlockSpec((tm, tc), lambda i, j: (i, j))
    halo = pl.BlockSpec((HALO, tc), lambda i, j: (jnp.maximum(i * (tm // HALO) - 1, 0), j))
    cw = pl.BlockSpec((3, tc), lambda i, j: (0, j))
    cb = pl.BlockSpec((1, tc), lambda i, j: (0, j))
    return tm, tc, blk, halo, cw, cb


def _glu_fwd(ug, uv, cwg, cwv, cbg, cbv, *, name):
    s = ug.shape[0]
    tm, tc, blk, halo, cw, cb = _ffn_specs(s)

    def body(g_ref, gh_ref, v_ref, vh_ref, cwg_ref, cwv_ref, cbg_ref, cbv_ref, f_ref, eg, ev):
        first = pl.program_id(0) == 0
        eg[0:HALO, :] = jnp.where(first, 0.0, gh_ref[...])
        eg[HALO:, :] = g_ref[...]
        ev[0:HALO, :] = jnp.where(first, 0.0, vh_ref[...])
        ev[HALO:, :] = v_ref[...]
        g = _ffn_conv(eg, cwg_ref[...], cbg_ref[...], tm)
        v = _ffn_conv(ev, cwv_ref[...], cbv_ref[...], tm)
        f_ref[...] = (_silu(g) * v).astype(BF16)

    return pl.pallas_call(
        body, name=name, grid=(s // tm, DFF // tc),
        in_specs=[blk, halo, blk, halo, cw, cw, cb, cb], out_specs=blk,
        out_shape=jax.ShapeDtypeStruct((s, DFF), BF16),
        scratch_shapes=[pltpu.VMEM((tm + HALO, tc), F32)] * 2,
        compiler_params=_cp("parallel", "parallel"),
    )(ug, ug, uv, uv, cwg, cwv, cbg, cbv)


def _glu_bwd(ug, uv, df, cwg, cwv, cbg, cbv, *, name):
    s = ug.shape[0]
    tm, tc, blk, halo, cw, cb = _ffn_specs(s)

    def body(g_ref, gh_ref, v_ref, vh_ref, df_ref, cwg_ref, cwv_ref, cbg_ref, cbv_ref, dg_ref, dv_ref, eg, ev):
        first = pl.program_id(0) == 0
        eg[0:HALO, :] = jnp.where(first, 0.0, gh_ref[...])
        eg[HALO:, :] = g_ref[...]
        ev[0:HALO, :] = jnp.where(first, 0.0, vh_ref[...])
        ev[HALO:, :] = v_ref[...]
        g = _ffn_conv(eg, cwg_ref[...], cbg_ref[...], tm)
        v = _ffn_conv(ev, cwv_ref[...], cbv_ref[...], tm)
        dfv = df_ref[...]
        sg = _sigmoid(g)
        dv_ref[...] = dfv * g * sg
        dg_ref[...] = dfv * v * sg * (1.0 + g * (1.0 - sg))

    return pl.pallas_call(
        body, name=name, grid=(s // tm, DFF // tc),
        in_specs=[blk, halo, blk, halo, blk, cw, cw, cb, cb], out_specs=[blk, blk],
        out_shape=[jax.ShapeDtypeStruct((s, DFF), F32)] * 2,
        scratch_shapes=[pltpu.VMEM((tm + HALO, tc), F32)] * 2,
        compiler_params=_cp("parallel", "parallel"),
    )(ug, ug, uv, uv, df, cwg, cwv, cbg, cbv)


def _ffn_conv_bwd(du, u, cwt, *, name):
    s = du.shape[0]
    tm, tc = FFN_TM, FFN_TC
    nb = s // tm

    def body(du_ref, duh_ref, u_ref, uh_ref, cw_ref, dx_ref, dc_ref, edu, eu):
        i = pl.program_id(1)

        @pl.when(i == 0)
        def _():
            dc_ref[...] = jnp.zeros_like(dc_ref)

        duv = du_ref[...]
        edu[0:tm, :] = duv
        edu[tm:, :] = jnp.where(i == nb - 1, 0.0, duh_ref[...])
        eu[0:HALO, :] = jnp.where(i == 0, 0.0, uh_ref[...])
        eu[HALO:, :] = u_ref[...]
        cwv = cw_ref[...]
        dx = jnp.zeros((tm, tc), F32)
        rows = []
        for k in range(3):
            dx = dx + cwv[k:k + 1, :] * edu[pl.ds(2 - k, tm), :]
            rows.append(jnp.sum(duv * eu[pl.ds(HALO - 2 + k, tm), :], axis=0, keepdims=True))
        rows.append(jnp.sum(duv, axis=0, keepdims=True))
        dx_ref[...] = dx.astype(BF16)
        dc_ref[0:4, :] += jnp.concatenate(rows, axis=0)

    blk = pl.BlockSpec((tm, tc), lambda j, i: (i, j))
    nxt = pl.BlockSpec((HALO, tc), lambda j, i: (jnp.minimum((i + 1) * (tm // HALO), s // HALO - 1), j))
    prv = pl.BlockSpec((HALO, tc), lambda j, i: (jnp.maximum(i * (tm // HALO) - 1, 0), j))
    return pl.pallas_call(
        body, name=name, grid=(DFF // tc, nb),
        in_specs=[blk, nxt, blk, prv, pl.BlockSpec((3, tc), lambda j, i: (0, j))],
        out_specs=[blk, pl.BlockSpec((8, tc), lambda j, i: (0, j))],
        out_shape=[jax.ShapeDtypeStruct((s, DFF), BF16), jax.ShapeDtypeStruct((8, DFF), F32)],
        scratch_shapes=[pltpu.VMEM((tm + HALO, tc), F32)] * 2,
        compiler_params=_cp("parallel", "arbitrary"),
    )(du, du, u, u, cwt)


MESH = pl.DeviceIdType.MESH


def _all_gather(arrs, *, name):
    n = len(arrs)

    def body(*refs):
        x_refs, out_refs = refs[:n], refs[n:2 * n]
        send_sems, recv_sems, local_sems = refs[2 * n:]
        x, y, c = lax.axis_index("x"), lax.axis_index("y"), lax.axis_index("c")
        me, sibling = (x, y, c), (x, y, 1 - c)
        chips = [(1 - x, y), (x, 1 - y), (1 - x, 1 - y)]

        def blk(a, dev):
            return out_refs[a].at[4 * dev[0] + 2 * dev[1] + dev[2]]

        def copy(a, k, block, to, src=None):
            return pltpu.make_async_remote_copy(
                src_ref=blk(a, block) if src is None else src, dst_ref=blk(a, block),
                send_sem=send_sems.at[7 * a + k], recv_sem=recv_sems.at[7 * a + k],
                device_id=to, device_id_type=MESH)

        started = []
        mine = []
        for a in range(n):
            cp = pltpu.make_async_copy(x_refs[a], blk(a, me), local_sems.at[a])
            cp.start()
            mine.append(cp)
            first = [copy(a, 0, me, sibling, src=x_refs[a])]
            first += [copy(a, 1 + j, me, (*chip, c), src=x_refs[a]) for j, chip in enumerate(chips)]
            for cp in first:
                cp.start()
            started += first
        for a in range(n):
            for j, chip in enumerate(chips):
                copy(a, 1 + j, (*chip, c), me).wait_recv()
                fwd = copy(a, 4 + j, (*chip, c), sibling)
                fwd.start()
                started.append(fwd)
        for a in range(n):
            copy(a, 0, sibling, me).wait_recv()
            for j, chip in enumerate(chips):
                copy(a, 4 + j, (*chip, 1 - c), me).wait_recv()
        for cp in started:
            cp.wait_send()
        for cp in mine:
            cp.wait()

    any_spec = pl.BlockSpec(memory_space=pl.ANY)
    return pl.pallas_call(
        body, name=name,
        in_specs=[any_spec] * n, out_specs=[any_spec] * n,
        out_shape=[jax.ShapeDtypeStruct((NDEV,) + a.shape, a.dtype) for a in arrs],
        scratch_shapes=[pltpu.SemaphoreType.DMA((7 * n,)), pltpu.SemaphoreType.DMA((7 * n,)),
                        pltpu.SemaphoreType.DMA((n,))],
    )(*arrs)


class _Direct:
    SEMS = (pltpu.SemaphoreType.DMA((7,)), pltpu.SemaphoreType.DMA((7,)), pltpu.SemaphoreType.DMA((1,)))

    def __init__(self, src_ref, recv_ref, sems, gather):
        x, y, c = lax.axis_index("x"), lax.axis_index("y"), lax.axis_index("c")
        me = 4 * x + 2 * y + c
        send_sems, recv_sems, local_sem = sems
        src = (lambda pid: src_ref) if gather else (lambda pid: src_ref.at[pid])
        self.mine = pltpu.make_async_copy(src(me), recv_ref.at[me], local_sem.at[0])
        self.copies = []
        for k in range(1, NDEV):
            px = 1 - x if k & 4 else x
            py = 1 - y if k & 2 else y
            pc = 1 - c if k & 1 else c
            self.copies.append(pltpu.make_async_remote_copy(
                src_ref=src(4 * px + 2 * py + pc), dst_ref=recv_ref.at[me],
                send_sem=send_sems.at[k - 1], recv_sem=recv_sems.at[k - 1],
                device_id=(px, py, pc), device_id_type=MESH))

    def start(self):
        self.mine.start()
        for cp in self.copies:
            cp.start()

    def wait(self):
        for cp in self.copies:
            cp.wait_recv()
        for cp in self.copies:
            cp.wait_send()
        self.mine.wait()


def _recv_shape(src, gather):
    return jax.ShapeDtypeStruct(((NDEV,) + src.shape) if gather else src.shape, src.dtype)


class _Rides:
    def __init__(self, rides, gathers):
        self.n = len(rides)
        self.gathers = list(gathers)
        any_spec = pl.BlockSpec(memory_space=pl.ANY)
        self.in_specs = [any_spec] * self.n
        self.out_specs = [any_spec] * self.n
        self.out_shape = [_recv_shape(a, g) for a, g in zip(rides, gathers)]
        self.scratch = list(_Direct.SEMS) * self.n

    def split(self, refs, n_in, n_out, n_scratch):
        n = self.n
        ins, refs = refs[:n_in], refs[n_in:]
        rides, refs = refs[:n], refs[n:]
        outs, refs = refs[:n_out], refs[n_out:]
        gots, refs = refs[:n], refs[n:]
        scratch, sems = refs[:n_scratch], refs[n_scratch:]
        return ins, outs, scratch, (rides, gots, sems)

    def run(self, handles, first, last):
        rides, gots, sems = handles

        def all_of():
            return [_Direct(rides[a], gots[a], sems[3 * a:3 * a + 3], self.gathers[a]) for a in range(self.n)]

        @pl.when(first)
        def _():
            for e in all_of():
                e.start()

        @pl.when(last)
        def _():
            for e in all_of():
                e.wait()


def _exchange(arrs, gathers, *, name):
    rd = _Rides(arrs, gathers)

    def body(*refs):
        _, _, _, handles = rd.split(refs, 0, 0, 0)
        rd.run(handles, True, True)

    return pl.pallas_call(
        body, name=name, in_specs=rd.in_specs, out_specs=rd.out_specs, out_shape=rd.out_shape,
        scratch_shapes=rd.scratch,
    )(*arrs)


def _adamw(parts, w, m, v, *, name):
    r, cols = w.shape
    tm = _tile(r, 256, PACK_ALIGN)
    c1 = 1.0 - ADAM_B1 ** ADAM_STEP
    c2 = 1.0 - ADAM_B2 ** ADAM_STEP

    def body(p_ref, w_ref, m_ref, v_ref, g_ref, d_ref, nm_ref, nv_ref):
        g = p_ref[0].astype(F32)
        for i in range(1, NDEV):
            g = g + p_ref[i].astype(F32)
        nm = ADAM_B1 * m_ref[...] + (1.0 - ADAM_B1) * g
        nv = ADAM_B2 * v_ref[...] + (1.0 - ADAM_B2) * (g * g)
        d_ref[...] = -ADAM_LR * ((nm / c1) / (jnp.sqrt(nv / c2) + ADAM_EPS) + ADAM_WD * w_ref[...])
        g_ref[...] = g
        nm_ref[...] = nm
        nv_ref[...] = nv

    row = pl.BlockSpec((tm, cols), lambda i: (i, 0))
    return pl.pallas_call(
        body, name=name, grid=(r // tm,),
        in_specs=[pl.BlockSpec((NDEV, tm, cols), lambda i: (0, i, 0)), row, row, row],
        out_specs=[row] * 4, out_shape=[jax.ShapeDtypeStruct((r, cols), F32)] * 4,
        compiler_params=_cp("parallel"),
    )(parts, w, m, v)


PACK_ALIGN = 16


def _part_rows(shape):
    n = -(-math.prod(shape) // D)
    return n + (-n) % PACK_ALIGN


def _rows(a):
    flat = a.reshape(-1)
    pad = _part_rows(a.shape) * D - flat.shape[0]
    if pad:
        flat = jnp.concatenate([flat, jnp.zeros((pad,), flat.dtype)])
    return flat.reshape(-1, D)


def _pack(parts, total_rows):
    rows = [_rows(p) for p in parts]
    used = sum(r.shape[0] for r in rows)
    if total_rows > used:
        rows.append(jnp.zeros((total_rows - used, D), rows[0].dtype))
    return jnp.concatenate(rows, axis=0)


def _unpack(buf, shapes):
    out, r0 = [], 0
    for shp in shapes:
        n = math.prod(shp)
        out.append(buf[r0:r0 + _part_rows(shp)].reshape(-1)[:n].reshape(shp))
        r0 += _part_rows(shp)
    return out


SHARD = {"w_in": (D, 706), "w_out": (256, D), "w_mq": (128, D), "w_mk": (128, D), "w_mv": (128, D),
         "w_mo": (128, D), "w_up": (D, 704), "w_down": (352, D), "conv_ssd_w": (4, 192), "conv_ffn_w": (3, 704)}
GATHER_MID = ["w_out", "w_mq", "w_mk", "w_mv", "w_mo"]
GATHER_FFN = ["w_down"]
CONV_TAPS = ["conv_ssd_w", "conv_ffn_w"]
GRADS_PACKED = ["w_out", "w_mq", "w_mk", "w_mv", "w_mo", "w_down", "conv_ffn_w", "conv_ssd_w"]


def _layout(names):
    row0, r = {}, 0
    for n in names:
        row0[n] = r
        r += _part_rows(SHARD[n])
    return row0, r + (-r) % 128


SMALL = [("norm_mix_w", (1, D)), ("conv_ssd_b", (1, 1536)), ("dt_bias", (1, 16)), ("a_log", (1, 16)),
         ("d_skip", (1, 16)), ("ssd_norm_w", (1, D)), ("sb_norm_w", (1, D)), ("norm_mem_w", (1, D)),
         ("norm_memkv_w", (1, D)), ("norm_ffn_w", (1, D)), ("conv_ffn_b", (1, 5632)), ("norm_final_w", (D,))]
LOSS_ROW = sum(_part_rows(_shp) for _, _shp in SMALL)
SMALL_ROWS = LOSS_ROW + PACK_ALIGN
ORDER = ["norm_mix_w", "w_in", "conv_ssd_w", "conv_ssd_b", "dt_bias", "a_log", "d_skip", "ssd_norm_w",
         "sb_norm_w", "w_out", "norm_mem_w", "norm_memkv_w", "w_mq", "w_mk", "w_mv", "w_mo", "norm_ffn_w",
         "w_up", "conv_ffn_w", "conv_ffn_b", "w_down", "norm_final_w"]


def _pad_rows(a, nr):
    n = a.shape[1]
    return jnp.concatenate([a, jnp.zeros((NDEV, nr * D - n), a.dtype)], axis=1).reshape(NDEV, nr, D)


def _group_sum(lanes):
    return lanes.reshape(16, 64).sum(axis=1).reshape(1, 16)


def kernel(x, mem, norm_mix_w, w_in, conv_ssd_w, conv_ssd_b, dt_bias, a_log, d_skip, ssd_norm_w, sb_norm_w, w_out, norm_mem_w, norm_memkv_w, w_mq, w_mk, w_mv, w_mo, norm_ffn_w, w_up, conv_ffn_w, conv_ffn_b, w_down, norm_final_w, loss_target, m_norm_mix_w, m_w_in, m_conv_ssd_w, m_conv_ssd_b, m_dt_bias, m_a_log, m_d_skip, m_ssd_norm_w, m_sb_norm_w, m_w_out, m_norm_mem_w, m_norm_memkv_w, m_w_mq, m_w_mk, m_w_mv, m_w_mo, m_norm_ffn_w, m_w_up, m_conv_ffn_w, m_conv_ffn_b, m_w_down, m_norm_final_w, v_norm_mix_w, v_w_in, v_conv_ssd_w, v_conv_ssd_b, v_dt_bias, v_a_log, v_d_skip, v_ssd_norm_w, v_sb_norm_w, v_w_out, v_norm_mem_w, v_norm_memkv_w, v_w_mq, v_w_mk, v_w_mv, v_w_mo, v_norm_ffn_w, v_w_up, v_conv_ffn_w, v_conv_ffn_b, v_w_down, v_norm_final_w):
    P = dict(norm_mix_w=norm_mix_w, w_in=w_in, conv_ssd_w=conv_ssd_w, conv_ssd_b=conv_ssd_b, dt_bias=dt_bias, a_log=a_log, d_skip=d_skip, ssd_norm_w=ssd_norm_w, sb_norm_w=sb_norm_w, w_out=w_out, norm_mem_w=norm_mem_w, norm_memkv_w=norm_memkv_w, w_mq=w_mq, w_mk=w_mk, w_mv=w_mv, w_mo=w_mo, norm_ffn_w=norm_ffn_w, w_up=w_up, conv_ffn_w=conv_ffn_w, conv_ffn_b=conv_ffn_b, w_down=w_down, norm_final_w=norm_final_w)
    M = dict(norm_mix_w=m_norm_mix_w, w_in=m_w_in, conv_ssd_w=m_conv_ssd_w, conv_ssd_b=m_conv_ssd_b, dt_bias=m_dt_bias, a_log=m_a_log, d_skip=m_d_skip, ssd_norm_w=m_ssd_norm_w, sb_norm_w=m_sb_norm_w, w_out=m_w_out, norm_mem_w=m_norm_mem_w, norm_memkv_w=m_norm_memkv_w, w_mq=m_w_mq, w_mk=m_w_mk, w_mv=m_w_mv, w_mo=m_w_mo, norm_ffn_w=m_norm_ffn_w, w_up=m_w_up, conv_ffn_w=m_conv_ffn_w, conv_ffn_b=m_conv_ffn_b, w_down=m_w_down, norm_final_w=m_norm_final_w)
    V = dict(norm_mix_w=v_norm_mix_w, w_in=v_w_in, conv_ssd_w=v_conv_ssd_w, conv_ssd_b=v_conv_ssd_b, dt_bias=v_dt_bias, a_log=v_a_log, d_skip=v_d_skip, ssd_norm_w=v_ssd_norm_w, sb_norm_w=v_sb_norm_w, w_out=v_w_out, norm_mem_w=v_norm_mem_w, norm_memkv_w=v_norm_memkv_w, w_mq=v_w_mq, w_mk=v_w_mk, w_mv=v_w_mv, w_mo=v_w_mo, norm_ffn_w=v_norm_ffn_w, w_up=v_w_up, conv_ffn_w=v_conv_ffn_w, conv_ffn_b=v_conv_ffn_b, w_down=v_w_down, norm_final_w=v_norm_final_w)
    small_shapes = [shp for _, shp in SMALL]

    def packed(src, names, dtype=F32):
        return _pack([src[n][0] for n in names], _layout(names)[1]).astype(dtype)

    def columns(g):
        return g.transpose(1, 0, 2).reshape(g.shape[1], NDEV * g.shape[2])

    g_in, g_taps = _all_gather([w_in[0].astype(BF16), packed(P, CONV_TAPS)], name="gather_w_in")
    W_in = columns(g_in)
    cw_ssd = g_taps[:, 0].reshape(NDEV, -1)[:, :768].reshape(NDEV, 4, 192).transpose(1, 0, 2).reshape(4, XBC)
    cw_ffn = (g_taps[:, PACK_ALIGN:PACK_ALIGN + 3].reshape(NDEV, -1)[:, :2112].reshape(NDEV, 3, 704)
              .transpose(1, 0, 2).reshape(3, 2 * DFF))
    W_z, W_xbc, W_dt, W_qkv = W_in[:, :D], W_in[:, D:D + XBC], W_in[:, D + XBC:D + XBC + 16], W_in[:, D + XBC + 16:]
    W_dtr = jnp.repeat(W_dt, 64, axis=1)
    cwg, cwv = cw_ffn[:, :DFF], cw_ffn[:, DFF:]
    cbg, cbv = conv_ffn_b[:, :DFF], conv_ffn_b[:, DFF:]
    rep = lambda p: jnp.repeat(p, 64, axis=1)
    lanes = jnp.concatenate([rep(dt_bias), rep(a_log), rep(d_skip), ssd_norm_w, jnp.zeros((4, D), F32)], axis=0)

    xs, tgt, mm = x[0], loss_target[0], mem[0]

    h1 = _norm_fwd(xs, norm_mix_w, name="norm_mix")
    z = _mm(h1, W_z, name="proj_z")
    xbc = _mm(h1, W_xbc, name="proj_xbc")
    dtr = _mm(h1, W_dtr, name="proj_dt")
    qkv = _mm(h1, W_qkv, name="proj_qkv", out_dtype=BF16)
    y_ssd, states, g_up, g_ffn = _ssd_fwd(z, xbc, dtr, cw_ssd, conv_ssd_b, lanes,
                                          [w_up[0].astype(BF16), packed(P, GATHER_FFN, BF16)], name="ssd_fwd")
    o_sb, g_mid = _sb_fwd(qkv, [packed(P, GATHER_MID, BF16)], name="sb_fwd")
    r_mid = _layout(GATHER_MID)[0]
    W_out = g_mid[:, r_mid["w_out"]:r_mid["w_out"] + 256].reshape(2 * D, D)
    W_mq, W_mk, W_mv, W_mo = [g_mid[:, r_mid[n]:r_mid[n] + 128].reshape(D, D)
                              for n in ("w_mq", "w_mk", "w_mv", "w_mo")]
    W_up = columns(g_up)
    W_down = g_ffn[:, 0:352].reshape(DFF, D)
    W_upg, W_upv = W_up[:, :DFF], W_up[:, DFF:]
    y_sb = _head_norm_fwd(o_sb, sb_norm_w, name="sb_norm")
    ymix = jnp.concatenate([y_ssd, y_sb], axis=1)
    x1 = _mm(ymix, W_out, add=xs, name="proj_out")
    h2 = _norm_fwd(x1, norm_mem_w, name="norm_mem")
    mn = _norm_fwd(mm, norm_memkv_w, name="norm_memkv")
    qm = _mm(h2, W_mq, name="mem_q", out_dtype=BF16)
    km = _mm(mn, W_mk, name="mem_k", out_dtype=BF16)
    vm = _mm(mn, W_mv, name="mem_v", out_dtype=BF16)
    om = _mem_attn_fwd(qm, km, vm, name="mem_attn")
    x2 = _mm(om, W_mo, add=x1, name="mem_o")
    h3 = _norm_fwd(x2, norm_ffn_w, name="norm_ffn")
    ug = _mm(h3, W_upg, name="ffn_up_g")
    uv = _mm(h3, W_upv, name="ffn_up_v")
    f = _glu_fwd(ug, uv, cwg, cwv, cbg, cbv, name="ffn_glu")
    x3 = _mm(f, W_down, add=x2, name="ffn_down")
    dx3, g_nfinal, loss_part = _final(x3, norm_final_w.reshape(1, D), tgt, name="final_loss")

    G = {}
    G["w_down"] = _mm(f, dx3, trans_a=True, name="g_w_down")
    df = _mm(dx3, W_down.T, name="d_f")
    dug, duv = _glu_bwd(ug, uv, df, cwg, cwv, cbg, cbv, name="ffn_glu_bwd")
    dupg, dcg = _ffn_conv_bwd(dug, ug, cwg, name="ffn_conv_bwd_g")
    dupv, dcv = _ffn_conv_bwd(duv, uv, cwv, name="ffn_conv_bwd_v")
    G["w_up"] = jnp.concatenate([_mm(h3, dupg, trans_a=True, name="g_w_up_g"),
                                 _mm(h3, dupv, trans_a=True, name="g_w_up_v")], axis=1)
    G["conv_ffn_w"] = jnp.concatenate([dcg[0:3], dcv[0:3]], axis=1)
    G["conv_ffn_b"] = jnp.concatenate([dcg[3:4], dcv[3:4]], axis=1)
    dh3 = _mm(dupg, W_upg.T, name="d_h3_g")
    dh3 = _mm(dupv, W_upv.T, add=dh3, name="d_h3_v")
    dx2, G["norm_ffn_w"] = _norm_bwd(x2, norm_ffn_w, dh3, dx3, name="norm_ffn_bwd")
    G["w_mo"] = _mm(om, dx2, trans_a=True, name="g_w_mo")
    dom = _mm(dx2, W_mo.T, name="d_om")
    dqm, dkm, dvm = _mem_attn_bwd(qm, km, vm, dom, name="mem_attn_bwd")
    G["w_mq"] = _mm(h2, dqm, trans_a=True, name="g_w_mq")
    G["w_mk"] = _mm(mn, dkm, trans_a=True, name="g_w_mk")
    G["w_mv"] = _mm(mn, dvm, trans_a=True, name="g_w_mv")
    dh2 = _mm(dqm, W_mq.T, name="d_h2")
    dmn = _mm(dkm, W_mk.T, name="d_mn_k")
    dmn = _mm(dvm, W_mv.T, add=dmn, name="d_mn_v")
    _, G["norm_memkv_w"] = _norm_bwd(mm, norm_memkv_w, dmn, None, name="norm_memkv_bwd")
    dx1, G["norm_mem_w"] = _norm_bwd(x1, norm_mem_w, dh2, dx2, name="norm_mem_bwd")
    G["w_out"] = _mm(ymix, dx1, trans_a=True, name="g_w_out")
    dymix = _mm(dx1, W_out.T, name="d_ymix")
    do_sb, G["sb_norm_w"] = _head_norm_bwd(o_sb, sb_norm_w, dymix, name="sb_norm_bwd")

    dz, dxbc, ddtr, dlanes, dconv = _ssd_bwd(z, xbc, dtr, states, dymix, cw_ssd, conv_ssd_b, lanes, name="ssd_bwd")
    G["dt_bias"], G["a_log"], G["d_skip"] = [_group_sum(dlanes[i:i + 1]) for i in range(3)]
    G["ssd_norm_w"] = dlanes[3:4]
    G["conv_ssd_w"], G["conv_ssd_b"] = dconv[0:4], dconv[4:5]

    def col_slabs(g, cols):
        return g.reshape(g.shape[0], NDEV, cols).transpose(1, 0, 2).astype(BF16)

    def packed_slabs(names):
        parts = []
        for n in names:
            shp = SHARD[n]
            if shp[-1] == D:
                t = G[n].reshape((NDEV,) + shp)
            else:
                t = _pad_rows(G[n].reshape(shp[0], NDEV, shp[1]).transpose(1, 0, 2).reshape(NDEV, -1),
                              _part_rows(shp))
            parts.append(jnp.pad(t, ((0, 0), (0, _part_rows(shp) - t.shape[1]), (0, 0))))
        used = sum(t.shape[1] for t in parts)
        parts.append(jnp.zeros((NDEV, _layout(names)[1] - used, D), F32))
        return jnp.concatenate(parts, axis=1).astype(BF16)

    dq, dk, dv, recv_packed, recv_up = _sb_bwd(qkv, o_sb, do_sb, [packed_slabs(GRADS_PACKED), col_slabs(G["w_up"], 704)],
                                               name="sb_bwd")
    dqkv = jnp.concatenate([dq, dk, dv], axis=1)
    g_wdt = _mm(h1, ddtr, trans_a=True, name="g_w_dt").reshape(D, 16, 64).sum(axis=2)
    G["w_in"] = jnp.concatenate([_mm(h1, dz, trans_a=True, name="g_w_z"),
                                 _mm(h1, dxbc, trans_a=True, name="g_w_xbc"), g_wdt,
                                 _mm(h1, dqkv, trans_a=True, name="g_w_qkv")], axis=1)
    dh1 = _mm(dz, W_z.T, name="d_h1_z")
    dh1 = _mm(dxbc, W_xbc.T, add=dh1, name="d_h1_xbc")
    dh1 = _mm(ddtr, W_dtr.T, add=dh1, name="d_h1_dt")
    dh1 = _mm(dqkv, W_qkv.T, add=dh1, name="d_h1_qkv")
    dx, G["norm_mix_w"] = _norm_bwd(xs, norm_mix_w, dh1, dx1, name="norm_mix_bwd")
    G["norm_final_w"] = g_nfinal.reshape(D)

    small_g = _pack([G[n] for n, _ in SMALL] + [loss_part], SMALL_ROWS)
    recv_in, parts_small = _exchange([col_slabs(G["w_in"], 706), small_g], [False, True], name="exchange_grads")
    outs_packed = _adamw(recv_packed, packed(P, GRADS_PACKED), packed(M, GRADS_PACKED), packed(V, GRADS_PACKED),
                         name="adamw_packed")
    outs_up = _adamw(recv_up, w_up[0], m_w_up[0], v_w_up[0], name="adamw_w_up")
    outs_in = _adamw(recv_in, w_in[0], m_w_in[0], v_w_in[0], name="adamw_w_in")
    outs_small = _adamw(parts_small, _pack([P[n] for n, _ in SMALL], SMALL_ROWS),
                        _pack([M[n] for n, _ in SMALL], SMALL_ROWS),
                        _pack([V[n] for n, _ in SMALL], SMALL_ROWS), name="adamw_replicated")

    res = {}
    for i, kind in enumerate(("grad", "delta", "new_m", "new_v")):
        for n, val in zip(GRADS_PACKED, _unpack(outs_packed[i], [SHARD[n] for n in GRADS_PACKED])):
            res[kind, n] = val.reshape((1,) + SHARD[n])
        res[kind, "w_up"] = outs_up[i].reshape((1,) + SHARD["w_up"])
        res[kind, "w_in"] = outs_in[i].reshape((1,) + SHARD["w_in"])
        for (n, shp), val in zip(SMALL, _unpack(outs_small[i], small_shapes)):
            res[kind, n] = val
    loss = outs_small[0][LOSS_ROW, 0]
    out = [loss, dx.reshape(1, -1, D)]
    for kind in ("grad", "delta", "new_m", "new_v"):
        out += [res[kind, n] for n in ORDER]
    return tuple(out)
```

```python
import functools
import math

import jax
import jax.numpy as jnp
from jax import lax
from jax.experimental import pallas as pl
from jax.experimental.pallas import tpu as pltpu

F32 = jnp.float32
BF16 = jnp.bfloat16

D = 1024
NDEV = 8
EPS = 1e-6
SSD_CHUNK = 128
HALO = 8
VMEM_LIMIT = 56 * 2**20

ADAM_LR, ADAM_B1, ADAM_B2, ADAM_EPS, ADAM_WD, ADAM_STEP = 0.001, 0.9, 0.999, 1e-08, 0.01, 10


def _cp(*sem):
    return pltpu.CompilerParams(dimension_semantics=sem, vmem_limit_bytes=VMEM_LIMIT)


def _tile(n, cap, mult):
    if n <= cap:
        return n
    for d in range(cap - cap % mult, 0, -mult):
        if n % d == 0:
            return d
    raise ValueError(f"no tile for {n}")


def _sigmoid(x):
    return 1.0 / (1.0 + jnp.exp(-x))


def _silu(x):
    return x * _sigmoid(x)


def _softplus(x):
    return jnp.maximum(x, 0.0) + jnp.log1p(jnp.exp(-jnp.abs(x)))


def _terms(x, n):
    out = []
    r = x.astype(F32)
    for i in range(n):
        h = r.astype(BF16)
        out.append(h)
        if i + 1 < n:
            r = r - h.astype(F32)
    return out


_DIMS = {"nn": ((1,), (0,)), "nt": ((1,), (1,)), "tn": ((0,), (0,))}


def _dot_raw(form, a, b, ta, tb):
    acc = None
    for ai in _terms(a, ta):
        for bi in _terms(b, tb):
            d = lax.dot_general(ai, bi, (_DIMS[form], ((), ())), preferred_element_type=F32)
            acc = d if acc is None else acc + d
    return acc


@functools.lru_cache(maxsize=None)
def _dot_fn(form, ta, tb):
    @jax.custom_vjp
    def f(a, b):
        return _dot_raw(form, a, b, ta, tb)

    def fwd(a, b):
        return f(a, b), (a, b)

    def bwd(res, ct):
        a, b = res
        if form == "nn":
            return _dot_fn("nt", ta, tb)(ct, b), _dot_fn("tn", ta, tb)(a, ct)
        if form == "nt":
            return _dot_fn("nn", ta, tb)(ct, b), _dot_fn("tn", tb, ta)(ct, a)
        return _dot_fn("nt", tb, ta)(b, ct), _dot_fn("nn", ta, tb)(a, ct)

    f.defvjp(fwd, bwd)
    return f


def _dot(form, a, b, ta=1, tb=1):
    return _dot_fn(form, ta, tb)(a, b)


@functools.lru_cache(maxsize=None)
def _take_fn(axis, idx):
    @jax.custom_vjp
    def f(x):
        return x[:, idx:idx + 1] if axis == 1 else x[idx:idx + 1, :]

    def fwd(x):
        return f(x), x.shape

    def bwd(shape, ct):
        io = lax.broadcasted_iota(jnp.int32, shape, axis)
        return (jnp.where(io == idx, jnp.broadcast_to(ct, shape), 0.0),)

    f.defvjp(fwd, bwd)
    return f


@functools.lru_cache(maxsize=None)
def _split_fn(width, n):
    @jax.custom_vjp
    def f(x):
        return tuple(x[:, i * width:(i + 1) * width] for i in range(n))

    def fwd(x):
        return f(x), None

    def bwd(_, cts):
        return (jnp.concatenate(list(cts), axis=1),)

    f.defvjp(fwd, bwd)
    return f


def _split(x, width):
    return _split_fn(width, x.shape[1] // width)(x)


def _iota(shape, axis):
    return lax.broadcasted_iota(jnp.int32, shape, axis)


MM_VMEM_BUDGET = 36 * 2**20


def _mm_tiles(m, n, kt, trans_a, a_bytes, b_bytes, out_bytes, add_bytes):
    tn = _tile(n, 1536, 128)
    for tm_cap in (1408, 1024, 512, 256, 128):
        tm = _tile(m, tm_cap, 128 if trans_a else 8)
        for tk_cap in (kt, 4096, 2048, 1024, 512):
            tk = _tile(kt, tk_cap, 128)
            blocks = tm * tk * a_bytes + tk * tn * b_bytes + tm * tn * (out_bytes + add_bytes)
            if 2 * blocks + (tm * tn * 4 if tk < kt else 0) <= MM_VMEM_BUDGET:
                return tm, tn, tk
    raise ValueError(f"no matmul tiling for {(m, n, kt)}")


def _mm(a, b, *, name, add=None, trans_a=False, out_dtype=F32):
    if trans_a:
        kt, m = a.shape
    else:
        m, kt = a.shape
    kt2, n = b.shape
    assert kt == kt2, (a.shape, b.shape)
    tm, tn, tk = _mm_tiles(m, n, kt, trans_a, a.dtype.itemsize, b.dtype.itemsize,
                           jnp.dtype(out_dtype).itemsize, 0 if add is None else add.dtype.itemsize)
    nk = kt // tk

    def body(*refs):
        if add is None:
            a_ref, b_ref, o_ref = refs[:3]
        else:
            a_ref, b_ref, add_ref, o_ref = refs[:4]
        k = pl.program_id(2)
        av = a_ref[...].astype(BF16)
        bv = b_ref[...].astype(BF16)
        dims = _DIMS["tn" if trans_a else "nn"]
        d = lax.dot_general(av, bv, (dims, ((), ())), preferred_element_type=F32)

        def finish(r):
            if add is not None:
                r = r + add_ref[...]
            o_ref[...] = r.astype(out_dtype)

        if nk == 1:
            finish(d)
        else:
            acc = refs[-1]

            @pl.when(k == 0)
            def _():
                acc[...] = d

            @pl.when((k > 0) & (k < nk - 1))
            def _():
                acc[...] += d

            @pl.when(k == nk - 1)
            def _():
                finish(acc[...] + d)

    a_spec = (pl.BlockSpec((tk, tm), lambda i, j, k: (k, i)) if trans_a
              else pl.BlockSpec((tm, tk), lambda i, j, k: (i, k)))
    in_specs = [a_spec, pl.BlockSpec((tk, tn), lambda i, j, k: (k, j))]
    args = [a, b]
    if add is not None:
        in_specs.append(pl.BlockSpec((tm, tn), lambda i, j, k: (i, j)))
        args.append(add)
    return pl.pallas_call(
        body, name=name, grid=(m // tm, n // tn, nk),
        in_specs=in_specs, out_specs=pl.BlockSpec((tm, tn), lambda i, j, k: (i, j)),
        out_shape=jax.ShapeDtypeStruct((m, n), out_dtype),
        scratch_shapes=[pltpu.VMEM((tm, tn), F32)] if nk > 1 else [],
        compiler_params=_cp("parallel", "parallel", "arbitrary"),
    )(*args)


def _rstd(x):
    return lax.rsqrt(jnp.mean(x * x, axis=-1, keepdims=True) + EPS)


def _norm_fwd(x, w, *, name):
    s = x.shape[0]
    tm = _tile(s, 512, 8)

    def body(x_ref, w_ref, o_ref):
        xv = x_ref[...]
        o_ref[...] = (xv * _rstd(xv) * w_ref[...]).astype(BF16)

    return pl.pallas_call(
        body, name=name, grid=(s // tm,),
        in_specs=[pl.BlockSpec((tm, D), lambda i: (i, 0)), pl.BlockSpec((1, D), lambda i: (0, 0))],
        out_specs=pl.BlockSpec((tm, D), lambda i: (i, 0)),
        out_shape=jax.ShapeDtypeStruct((s, D), BF16), compiler_params=_cp("parallel"),
    )(x, w)


def _norm_bwd_math(xv, wv, dy):
    r = _rstd(xv)
    xh = xv * r
    dxh = dy * wv
    dx = r * (dxh - xh * jnp.mean(dxh * xh, axis=-1, keepdims=True))
    dw = jnp.sum(dy * xh, axis=0, keepdims=True)
    return dx, dw


def _norm_bwd(x, w, dy, add, *, name):
    s = x.shape[0]
    tm = _tile(s, 256, 8)

    def body(*refs):
        if add is None:
            x_ref, w_ref, dy_ref, dx_ref, dw_ref = refs
        else:
            x_ref, w_ref, dy_ref, add_ref, dx_ref, dw_ref = refs

        @pl.when(pl.program_id(0) == 0)
        def _():
            dw_ref[...] = jnp.zeros_like(dw_ref)

        dx, dw = _norm_bwd_math(x_ref[...], w_ref[...], dy_ref[...])
        if add is not None:
            dx = dx + add_ref[...]
        dx_ref[...] = dx
        dw_ref[...] += dw

    row = pl.BlockSpec((tm, D), lambda i: (i, 0))
    vec = pl.BlockSpec((1, D), lambda i: (0, 0))
    in_specs = [row, vec, row] + ([row] if add is not None else [])
    args = [x, w, dy] + ([add] if add is not None else [])
    return pl.pallas_call(
        body, name=name, grid=(s // tm,), in_specs=in_specs, out_specs=[row, vec],
        out_shape=[jax.ShapeDtypeStruct((s, D), F32), jax.ShapeDtypeStruct((1, D), F32)],
        compiler_params=_cp("arbitrary"),
    )(*args)


def _final(x3, w, target, *, name):
    s = x3.shape[0]
    tm = _tile(s, 256, 8)

    def body(x_ref, w_ref, t_ref, dx_ref, dw_ref, loss_ref):
        @pl.when(pl.program_id(0) == 0)
        def _():
            dw_ref[...] = jnp.zeros_like(dw_ref)
            loss_ref[...] = jnp.zeros_like(loss_ref)

        xv = x_ref[...]
        wv = w_ref[...]
        y = xv * _rstd(xv) * wv
        err = y - t_ref[...]
        loss_ref[...] += 0.5 * jnp.sum(jnp.mean(err * err, axis=-1, keepdims=True))
        dx, dw = _norm_bwd_math(xv, wv, err * (1.0 / D))
        dx_ref[...] = dx
        dw_ref[...] += dw

    row = pl.BlockSpec((tm, D), lambda i: (i, 0))
    vec = pl.BlockSpec((1, D), lambda i: (0, 0))
    return pl.pallas_call(
        body, name=name, grid=(s // tm,), in_specs=[row, vec, row], out_specs=[row, vec, vec],
        out_shape=[jax.ShapeDtypeStruct((s, D), F32), jax.ShapeDtypeStruct((1, D), F32),
                   jax.ShapeDtypeStruct((1, D), F32)],
        compiler_params=_cp("arbitrary"),
    )(x3, w, target)


def _head_norm_math(o, w):
    lane = _iota((128, 128), 0) // 64
    bd = (lane == _iota((128, 128), 1) // 64).astype(F32)
    outs = []
    for op in _split(o, 128):
        ms = _dot("nn", op * op, bd, 2, 1) * (1.0 / 64)
        outs.append(op * lax.rsqrt(ms + EPS))
    return jnp.concatenate(outs, axis=1) * w


def _head_norm_fwd(o, w, *, name):
    s = o.shape[0]
    tm = _tile(s, 256, 8)

    def body(o_ref, w_ref, y_ref):
        y_ref[...] = _head_norm_math(o_ref[...], w_ref[...]).astype(BF16)

    row = pl.BlockSpec((tm, D), lambda i: (i, 0))
    vec = pl.BlockSpec((1, D), lambda i: (0, 0))
    return pl.pallas_call(
        body, name=name, grid=(s // tm,), in_specs=[row, vec], out_specs=row,
        out_shape=jax.ShapeDtypeStruct((s, D), BF16), compiler_params=_cp("parallel"),
    )(o, w)


def _head_norm_bwd(o, w, dymix, *, name):
    s = o.shape[0]
    tm = _tile(s, 256, 8)

    def body(o_ref, w_ref, dy_ref, do_ref, dw_ref):
        @pl.when(pl.program_id(0) == 0)
        def _():
            dw_ref[...] = jnp.zeros_like(dw_ref)

        _, vjp = jax.vjp(_head_norm_math, o_ref[...], w_ref[...])
        do, dw = vjp(dy_ref[...])
        do_ref[...] = do
        dw_ref[...] += dw

    row = pl.BlockSpec((tm, D), lambda i: (i, 0))
    vec = pl.BlockSpec((1, D), lambda i: (0, 0))
    return pl.pallas_call(
        body, name=name, grid=(s // tm,),
        in_specs=[row, vec, pl.BlockSpec((tm, D), lambda i: (i, 1))], out_specs=[row, vec],
        out_shape=[jax.ShapeDtypeStruct((s, D), F32), jax.ShapeDtypeStruct((1, D), F32)],
        compiler_params=_cp("arbitrary"),
    )(o, w, dymix)


SB_BQ = 256
SB_BK = 256


def _sb_consts():
    r = _iota((SB_BK, SB_BK), 0)
    c = _iota((SB_BK, SB_BK), 1)
    u_excl = (r > c).astype(BF16)
    u_incl = (r >= c).astype(BF16)
    return u_excl, u_incl


SB_LANES = 256
SB_NCH = SB_LANES // 64


def _nt(a, b):
    return lax.dot_general(a, b, (_DIMS["nt"], ((), ())), preferred_element_type=F32)


def _tn(a, b):
    return lax.dot_general(a, b, (_DIMS["tn"], ((), ())), preferred_element_type=F32)


def _nn(a, b):
    return jnp.dot(a, b, preferred_element_type=F32)


def _sb_heads(ref):
    out = []
    for hp in range(SB_LANES // 128):
        v = ref[:, 128 * hp:128 * (hp + 1)]
        first = _iota(v.shape, 1) < 64
        out += [jnp.where(first, v, 0).astype(BF16), jnp.where(first, 0, v).astype(BF16)]
    return out


SB_STRIP = 32


def _neg_abs(x):
    bits = lax.bitcast_convert_type(x, jnp.uint32) | jnp.uint32(0x80000000)
    return lax.bitcast_convert_type(bits, F32)


def _sb_block(ref, j):
    off = pl.multiple_of(j * SB_BK, SB_BK)
    return [ref[pl.ds(off, SB_BK), 128 * hp:128 * (hp + 1)] for hp in range(SB_NCH // 2)]


SB_DEAD = 104.0


def _sb_live(nlrun):
    m = nlrun[0]
    for x in nlrun[1:]:
        m = jnp.minimum(m, x)
    return jnp.min(m) < SB_DEAD


def _sb_strips():
    return [(r, pl.ds(r, SB_STRIP)) for r in range(0, SB_BQ, SB_STRIP)]


def _sb_diag_mask(r):
    return _iota((SB_STRIP, SB_BK), 1) < _iota((SB_STRIP, SB_BK), 0) + r


def _sb_soft(z, mask):
    e = jnp.exp(_neg_abs(z))
    nl = jnp.maximum(z, 0.0) + jnp.log(1.0 + e)
    if mask is not None:
        nl = jnp.where(mask, nl, 0.0)
    return e, nl


def _sb_split_to(hl_ref, rows, x):
    hi, lo = _terms(x, 2)
    hl_ref[rows, 0:SB_BK] = hi
    hl_ref[rows, SB_BK:2 * SB_BK] = lo


def _sb_stage_soft(z_ref, nl_ref, diag):
    for r, rows in _sb_strips():
        _, nl = _sb_soft(z_ref[rows, :], _sb_diag_mask(r) if diag else None)
        nl_ref[rows, 0:SB_BK] = nl.astype(BF16)


def _sb_stage_weights(z_ref, c_ref, a_ref, nlrun, diag):
    for r, rows in _sb_strips():
        a = jnp.exp(z_ref[rows, :] - c_ref[rows, :] - nlrun[r:r + SB_STRIP, :])
        if diag:
            a = jnp.where(_sb_diag_mask(r), a, 0.0)
        a_ref[rows, :] = a.astype(BF16)


def _sb_fwd(qkv, rides, *, name):
    s = qkv.shape[0]
    nq = s // SB_BQ
    ng = D // SB_LANES
    assert SB_BQ == SB_BK
    rd = _Rides(rides, [True] * len(rides))

    def body(*refs):
        (q_ref, k_ref, v_ref), (o_ref,), (zbuf, nlbuf, cbuf, abuf), handles = rd.split(refs, 3, 1, 4)
        i = pl.program_id(1)
        step_no = pl.program_id(0) * nq + i
        rd.run(handles, step_no == 0, step_no == ng * nq - 1)

        _, u_incl = _sb_consts()
        lane_a = _iota((SB_BQ, 128), 1) < 64
        qh = [q * 0.125 for q in _sb_heads(q_ref)]

        def tile(j, accs, nlrun, diag):
            kbs = _sb_block(k_ref, j)
            for c in range(SB_NCH):
                zbuf[c] = _nt(qh[c], kbs[c // 2])
            for c in range(SB_NCH):
                _sb_stage_soft(zbuf.at[c], nlbuf.at[c], diag)
                cbuf[c] = _nn(nlbuf[c], u_incl)
            for c in range(SB_NCH):
                _sb_stage_weights(zbuf.at[c], cbuf.at[c], abuf.at[c], nlrun[c], diag)
            nlrun = tuple(nlrun[c] + cbuf[c, :, 0:1] for c in range(SB_NCH))
            vbs = _sb_block(v_ref, j)
            outs = [_nn(abuf[c], vbs[c // 2]) for c in range(SB_NCH)]
            accs = tuple(acc + jnp.where(lane_a, outs[2 * hp], outs[2 * hp + 1]) for hp, acc in enumerate(accs))
            return accs, nlrun

        accs, nlrun = tile(i, (jnp.zeros((SB_BQ, 128), F32),) * (SB_NCH // 2),
                           (jnp.zeros((SB_BQ, 1), F32),) * SB_NCH, True)

        def step(carry):
            j, _, accs, nlrun = carry
            accs, nlrun = tile(j, accs, nlrun, False)
            return j - 1, _sb_live(nlrun), accs, nlrun

        _, _, accs, _ = lax.while_loop(lambda c: (c[0] >= 0) & c[1], step, (i - 1, _sb_live(nlrun), accs, nlrun))
        o_ref[...] = jnp.concatenate(accs, axis=1)

    return pl.pallas_call(
        body, name=name, grid=(ng, nq),
        in_specs=[pl.BlockSpec((SB_BQ, SB_LANES), lambda g, i: (i, g)),
                  pl.BlockSpec((s, SB_LANES), lambda g, i: (0, ng + g)),
                  pl.BlockSpec((s, SB_LANES), lambda g, i: (0, 2 * ng + g)), *rd.in_specs],
        out_specs=[pl.BlockSpec((SB_BQ, SB_LANES), lambda g, i: (i, g)), *rd.out_specs],
        out_shape=[jax.ShapeDtypeStruct((s, D), F32), *rd.out_shape],
        scratch_shapes=[pltpu.VMEM((SB_NCH, SB_BQ, SB_BK), F32), pltpu.VMEM((SB_NCH, SB_BQ, SB_BK), BF16),
                        pltpu.VMEM((SB_NCH, SB_BQ, SB_BK), F32), pltpu.VMEM((SB_NCH, SB_BQ, SB_BK), BF16),
                        *rd.scratch],
        compiler_params=_cp("arbitrary", "arbitrary"),
    )(qkv, qkv, qkv, *rides)


def _sb_bwd(qkv, o, do, rides, *, name):
    s = qkv.shape[0]
    nq = s // SB_BQ
    ng = D // SB_LANES
    nhp = SB_NCH // 2
    rd = _Rides(rides, [False] * len(rides))

    def body(*refs):
        ins, outs, scratch, handles = rd.split(refs, 5, 3, 11)
        q_ref, k_ref, v_ref, o_ref, do_ref = ins
        dq_ref, dk_hbm, dv_hbm = outs
        dk_acc, dv_acc, dk16, dv16, sems, zbuf, gbuf, hl, cbuf, abuf, dzbuf = scratch
        g_idx = pl.program_id(0)
        i = pl.program_id(1)
        step_no = g_idx * nq + i
        rd.run(handles, step_no == 0, step_no == ng * nq - 1)

        @pl.when(i == 0)
        def _():
            dk_acc[...] = jnp.zeros_like(dk_acc)
            dv_acc[...] = jnp.zeros_like(dv_acc)

        _, u_incl = _sb_consts()
        u2 = jnp.concatenate([u_incl, u_incl], axis=0)
        lane_a = _iota((SB_BQ, 128), 1) < 64
        lane_k = _iota((SB_BK, 128), 1) < 64
        qh = [q * 0.125 for q in _sb_heads(q_ref)]
        qf = [q_ref[:, 128 * hp:128 * (hp + 1)] for hp in range(nhp)]
        doh = _sb_heads(do_ref)
        dof = [do_ref[:, 128 * hp:128 * (hp + 1)].astype(BF16) for hp in range(nhp)]
        delta = []
        for hp in range(nhp):
            prod = dof[hp].astype(F32) * o_ref[:, 128 * hp:128 * (hp + 1)]
            delta += [jnp.sum(jnp.where(lane_a, prod, 0.0), axis=1, keepdims=True),
                      jnp.sum(jnp.where(lane_a, 0.0, prod), axis=1, keepdims=True)]

        def pre(slot, j):
            kbs = _sb_block(k_ref, j)
            vbs = _sb_block(v_ref, j)
            for c in range(SB_NCH):
                zbuf[slot, c] = _nt(qh[c], kbs[c // 2])
                gbuf[slot, c] = _nt(doh[c], vbs[c // 2])

        def stage_g(c, slot):
            for _, rows in _sb_strips():
                g = abuf[slot, c, rows, :].astype(F32) * gbuf[slot, c, rows, :]
                gbuf[slot, c, rows, :] = g
                _sb_split_to(hl.at[c], rows, g)

        def stage_dz(c, slot, grun, diag):
            for r, rows in _sb_strips():
                z = zbuf[slot, c, rows, :]
                g = gbuf[slot, c, rows, :]
                cs = (delta[c] - grun)[r:r + SB_STRIP, :] - cbuf[c, rows, :]
                sig = 1.0 / (1.0 + jnp.exp(-z))
                dz = g - (g + cs) * sig
                if diag:
                    dz = jnp.where(_sb_diag_mask(r), dz, 0.0)
                dzbuf[slot, c, rows, :] = dz.astype(BF16)

        def chain(slot, nlrun, grun, diag):
            for c in range(SB_NCH):
                _sb_stage_soft(zbuf.at[slot, c], hl.at[c], diag)
                cbuf[c] = _nn(hl[c, :, 0:SB_BK], u_incl)
            nl_tot = []
            for c in range(SB_NCH):
                _sb_stage_weights(zbuf.at[slot, c], cbuf.at[c], abuf.at[slot, c], nlrun[c], diag)
                nl_tot.append(cbuf[c, :, 0:1])
                stage_g(c, slot)
                cbuf[c] = _nn(hl[c], u2)
            g_tot = []
            for c in range(SB_NCH):
                stage_dz(c, slot, grun[c], diag)
                g_tot.append(cbuf[c, :, 0:1])
            return (tuple(a + b for a, b in zip(nlrun, nl_tot)), tuple(a + b for a, b in zip(grun, g_tot)))

        def post(slot, j, dqs):
            off = pl.multiple_of(j * SB_BK, SB_BK)
            kbs = _sb_block(k_ref, j)
            dq_t = [_nn(dzbuf[slot, c], kbs[c // 2]) for c in range(SB_NCH)]
            dk_t = [_tn(dzbuf[slot, c], qf[c // 2]) for c in range(SB_NCH)]
            dv_t = [_tn(abuf[slot, c], dof[c // 2]) for c in range(SB_NCH)]
            for hp in range(nhp):
                cols = slice(128 * hp, 128 * (hp + 1))
                dk_acc[pl.ds(off, SB_BK), cols] += 0.125 * jnp.where(lane_k, dk_t[2 * hp], dk_t[2 * hp + 1])
                dv_acc[pl.ds(off, SB_BK), cols] += jnp.where(lane_k, dv_t[2 * hp], dv_t[2 * hp + 1])
            return tuple(dq + jnp.where(lane_a, dq_t[2 * hp], dq_t[2 * hp + 1]) for hp, dq in enumerate(dqs))

        def tile(j, dqs, nlrun, grun, diag):
            pre(0, j)
            nlrun, grun = chain(0, nlrun, grun, diag)
            return post(0, j, dqs), nlrun, grun

        zero = (jnp.zeros((SB_BQ, 1), F32),) * SB_NCH
        dqs, nlrun, grun = tile(i, (jnp.zeros((SB_BQ, 128), F32),) * nhp, zero, zero, True)

        def step(carry):
            j, _, dqs, nlrun, grun = carry
            dqs, nlrun, grun = tile(j, dqs, nlrun, grun, False)
            return j - 1, _sb_live(nlrun), dqs, nlrun, grun

        carry = lax.while_loop(lambda c: (c[0] >= 0) & c[1], step, (i - 1, _sb_live(nlrun), dqs, nlrun, grun))
        dq_ref[...] = (0.125 * jnp.concatenate(carry[2], axis=1)).astype(BF16)

        @pl.when(i == nq - 1)
        def _():
            cols = pl.ds(pl.multiple_of(g_idx * SB_LANES, SB_LANES), SB_LANES)

            def narrow(r, carry):
                rows = pl.ds(pl.multiple_of(r * SB_BK, SB_BK), SB_BK)
                dk16[rows, :] = dk_acc[rows, :].astype(BF16)
                dv16[rows, :] = dv_acc[rows, :].astype(BF16)
                return carry

            lax.fori_loop(0, s // SB_BK, narrow, 0)
            ck = pltpu.make_async_copy(dk16, dk_hbm.at[:, cols], sems.at[0])
            cv = pltpu.make_async_copy(dv16, dv_hbm.at[:, cols], sems.at[1])
            ck.start()
            cv.start()
            ck.wait()
            cv.wait()

    qblk = pl.BlockSpec((SB_BQ, SB_LANES), lambda g, i: (i, g))
    hbm = pl.BlockSpec(memory_space=pl.ANY)
    return pl.pallas_call(
        body, name=name, grid=(ng, nq),
        in_specs=[qblk, pl.BlockSpec((s, SB_LANES), lambda g, i: (0, ng + g)),
                  pl.BlockSpec((s, SB_LANES), lambda g, i: (0, 2 * ng + g)), qblk, qblk, *rd.in_specs],
        out_specs=[qblk, hbm, hbm, *rd.out_specs],
        out_shape=[jax.ShapeDtypeStruct((s, D), BF16)] * 3 + rd.out_shape,
        scratch_shapes=[pltpu.VMEM((s, SB_LANES), F32), pltpu.VMEM((s, SB_LANES), F32),
                        pltpu.VMEM((s, SB_LANES), BF16), pltpu.VMEM((s, SB_LANES), BF16),
                        pltpu.SemaphoreType.DMA((2,)),
                        pltpu.VMEM((1, SB_NCH, SB_BQ, SB_BK), F32), pltpu.VMEM((1, SB_NCH, SB_BQ, SB_BK), F32),
                        pltpu.VMEM((SB_NCH, SB_BQ, 2 * SB_BK), BF16), pltpu.VMEM((SB_NCH, SB_BQ, SB_BK), F32),
                        pltpu.VMEM((1, SB_NCH, SB_BQ, SB_BK), BF16), pltpu.VMEM((1, SB_NCH, SB_BQ, SB_BK), BF16),
                        *rd.scratch],
        compiler_params=_cp("arbitrary", "arbitrary"),
    )(qkv, qkv, qkv, o, do, *rides)


def _ssd_core(z, xpre, dtr, state, dtb, alog, dsk, nw):
    L = SSD_CHUNK
    xa = _silu(xpre)
    pieces = _split(xa, 128)
    xs = jnp.concatenate(pieces[:8], axis=1)
    bm, cm = pieces[8:10], pieces[10:12]
    dt = _softplus(dtr + dtb)
    a = dt * (-jnp.exp(alog))
    tri = (_iota((L, L), 0) >= _iota((L, L), 1)).astype(F32)
    a_cs = _dot("nn", tri, a, 1, 3)
    xc = xs * dt
    tril = _iota((L, L), 0) >= _iota((L, L), 1)
    lane_a = _iota((L, 128), 1) < 64
    acs_p = _split(a_cs, 128)
    xc_p = _split(xc, 128)
    ys, new_states = [], []
    for g in range(2):
        cb = _dot("nt", cm[g], bm[g])
        for pp in range(4):
            pair = 4 * g + pp
            acs = acs_p[pair]
            acs_t = acs.T
            xcp = xc_p[pair]
            st = state[pair]
            heads = []
            for hh in range(2):
                col = _take_fn(1, 64 * hh)(acs)
                row = _take_fn(0, 64 * hh)(acs_t)
                seg = col - row
                lm = jnp.where(tril, jnp.exp(jnp.where(tril, seg, 0.0)), 0.0)
                heads.append(_dot("nn", cb * lm, xcp))
            ydiag = jnp.where(lane_a, heads[0], heads[1])
            last = _take_fn(0, L - 1)(acs)
            snew = _dot("tn", xcp * jnp.exp(last - acs), bm[g])
            new_states.append(st * jnp.exp(_take_fn(1, L - 1)(acs_t)) + snew)
            yoff = _dot("nt", cm[g], st) * jnp.exp(acs)
            ys.append(ydiag + yoff)
    y = jnp.concatenate(ys, axis=1) + xs * dsk
    yg = y * _silu(z)
    outs = []
    for v in _split(yg, 512):
        outs.append(v * lax.rsqrt(jnp.mean(v * v, axis=-1, keepdims=True) + EPS))
    return jnp.concatenate(outs, axis=1) * nw, tuple(new_states)


XBC = 1536


def _ssd_conv(ext_ref, cw, cb):
    acc = cb
    for k in range(4):
        acc = acc + cw[k:k + 1, :] * ext_ref[pl.ds(HALO - 3 + k, SSD_CHUNK), :]
    return acc


def _ssd_fwd(z, xbc, dtr, cw, cb, lanes, rides, *, name):
    s = z.shape[0]
    L = SSD_CHUNK
    nc = s // L
    rd = _Rides(rides, [True] * len(rides))

    def body(*refs):
        ins, (y_ref, st_ref), (state, ext), handles = rd.split(refs, 7, 2, 2)
        z_ref, x_ref, h_ref, dtr_ref, cw_ref, cb_ref, ln_ref = ins
        c = pl.program_id(0)
        rd.run(handles, c == 0, c == nc - 1)

        @pl.when(c == 0)
        def _():
            state[...] = jnp.zeros_like(state)

        ext[0:HALO, :] = jnp.where(c == 0, 0.0, h_ref[...])
        ext[HALO:, :] = x_ref[...]
        xpre = _ssd_conv(ext, cw_ref[...], cb_ref[...])
        st_ref[0] = state[...]
        st_in = tuple(state[p] for p in range(8))
        yn, st_out = _ssd_core(z_ref[...], xpre, dtr_ref[...], st_in,
                               ln_ref[0:1, :], ln_ref[1:2, :], ln_ref[2:3, :], ln_ref[3:4, :])
        y_ref[...] = yn.astype(BF16)
        for p in range(8):
            state[p] = st_out[p]

    return pl.pallas_call(
        body, name=name, grid=(nc,),
        in_specs=[pl.BlockSpec((L, D), lambda c: (c, 0)),
                  pl.BlockSpec((L, XBC), lambda c: (c, 0)),
                  pl.BlockSpec((HALO, XBC), lambda c: (jnp.maximum(c * (L // HALO) - 1, 0), 0)),
                  pl.BlockSpec((L, D), lambda c: (c, 0)),
                  pl.BlockSpec((4, XBC), lambda c: (0, 0)),
                  pl.BlockSpec((1, XBC), lambda c: (0, 0)),
                  pl.BlockSpec((8, D), lambda c: (0, 0)), *rd.in_specs],
        out_specs=[pl.BlockSpec((L, D), lambda c: (c, 0)),
                   pl.BlockSpec((1, 8, 128, 128), lambda c: (c, 0, 0, 0)), *rd.out_specs],
        out_shape=[jax.ShapeDtypeStruct((s, D), BF16), jax.ShapeDtypeStruct((nc, 8, 128, 128), F32),
                   *rd.out_shape],
        scratch_shapes=[pltpu.VMEM((8, 128, 128), F32), pltpu.VMEM((L + HALO, XBC), F32), *rd.scratch],
        compiler_params=_cp("arbitrary"),
    )(z, xbc, xbc, dtr, cw, cb, lanes, *rides)


def _ssd_bwd(z, xbc, dtr, states, dymix, cw, cb, lanes, *, name):
    s = z.shape[0]
    L = SSD_CHUNK
    nc = s // L

    def body(z_ref, x_ref, h_ref, dtr_ref, st_ref, dy_ref, cw_ref, cb_ref, ln_ref,
             dz_ref, dx_ref, ddt_ref, dln_ref, dcv_ref, dstate, ext, dext):
        i = pl.program_id(0)
        c = nc - 1 - i

        @pl.when(i == 0)
        def _():
            dstate[...] = jnp.zeros_like(dstate)
            dext[...] = jnp.zeros_like(dext)
            dln_ref[...] = jnp.zeros_like(dln_ref)
            dcv_ref[...] = jnp.zeros_like(dcv_ref)

        ext[0:HALO, :] = jnp.where(c == 0, 0.0, h_ref[...])
        ext[HALO:, :] = x_ref[...]
        cwv = cw_ref[...]
        xpre = _ssd_conv(ext, cwv, cb_ref[...])
        st_in = tuple(st_ref[0, p] for p in range(8))
        _, vjp = jax.vjp(_ssd_core, z_ref[...], xpre, dtr_ref[...], st_in,
                         ln_ref[0:1, :], ln_ref[1:2, :], ln_ref[2:3, :], ln_ref[3:4, :])
        dz, dxpre, ddtr, dst, d0, d1, d2, d3 = vjp((dy_ref[...], tuple(dstate[p] for p in range(8))))
        for p in range(8):
            dstate[p] = dst[p]
        dz_ref[...] = dz.astype(BF16)
        ddt_ref[...] = ddtr.astype(BF16)
        dln_ref[0:4, :] += jnp.concatenate([d0, d1, d2, d3], axis=0)
        dext[0:L, :] = dxpre
        dx = jnp.zeros((L, XBC), F32)
        rows = []
        for k in range(4):
            dx = dx + cwv[k:k + 1, :] * dext[pl.ds(3 - k, L), :]
            rows.append(jnp.sum(dxpre * ext[pl.ds(HALO - 3 + k, L), :], axis=0, keepdims=True))
        rows.append(jnp.sum(dxpre, axis=0, keepdims=True))
        dx_ref[...] = dx.astype(BF16)
        dcv_ref[0:5, :] += jnp.concatenate(rows, axis=0)
        dext[L:L + HALO, :] = dxpre[0:HALO, :]

    rev = lambda i: (nc - 1 - i, 0)
    return pl.pallas_call(
        body, name=name, grid=(nc,),
        in_specs=[pl.BlockSpec((L, D), rev),
                  pl.BlockSpec((L, XBC), rev),
                  pl.BlockSpec((HALO, XBC), lambda i: (jnp.maximum((nc - 1 - i) * (L // HALO) - 1, 0), 0)),
                  pl.BlockSpec((L, D), rev),
                  pl.BlockSpec((1, 8, 128, 128), lambda i: (nc - 1 - i, 0, 0, 0)),
                  pl.BlockSpec((L, D), rev),
                  pl.BlockSpec((4, XBC), lambda i: (0, 0)),
                  pl.BlockSpec((1, XBC), lambda i: (0, 0)),
                  pl.BlockSpec((8, D), lambda i: (0, 0))],
        out_specs=[pl.BlockSpec((L, D), rev), pl.BlockSpec((L, XBC), rev), pl.BlockSpec((L, D), rev),
                   pl.BlockSpec((8, D), lambda i: (0, 0)), pl.BlockSpec((8, XBC), lambda i: (0, 0))],
        out_shape=[jax.ShapeDtypeStruct((s, D), BF16), jax.ShapeDtypeStruct((s, XBC), BF16),
                   jax.ShapeDtypeStruct((s, D), BF16), jax.ShapeDtypeStruct((8, D), F32),
                   jax.ShapeDtypeStruct((8, XBC), F32)],
        scratch_shapes=[pltpu.VMEM((8, 128, 128), F32), pltpu.VMEM((L + HALO, XBC), F32),
                        pltpu.VMEM((L + HALO, XBC), F32)],
        compiler_params=_cp("arbitrary"),
    )(z, xbc, xbc, dtr, states, dymix, cw, cb, lanes)


def _mem_attn_math(q, k, v):
    outs = []
    for qh, kh, vh in zip(_split(q, 256), _split(k, 256), _split(v, 256)):
        sc = _dot("nt", qh, kh) * (1.0 / 16.0)
        e = jnp.exp(sc - lax.stop_gradient(jnp.max(sc, axis=-1, keepdims=True)))
        p = e / jnp.sum(e, axis=-1, keepdims=True)
        outs.append(_dot("nn", p, vh))
    return jnp.concatenate(outs, axis=1)


def _mem_attn_fwd(q, k, v, *, name):
    s, m = q.shape[0], k.shape[0]
    tm = _tile(s, 256, 8)

    def body(q_ref, k_ref, v_ref, o_ref):
        o_ref[...] = _mem_attn_math(q_ref[...].astype(F32), k_ref[...].astype(F32),
                                    v_ref[...].astype(F32)).astype(BF16)

    row = pl.BlockSpec((tm, D), lambda i: (i, 0))
    kv = pl.BlockSpec((m, D), lambda i: (0, 0))
    return pl.pallas_call(
        body, name=name, grid=(s // tm,), in_specs=[row, kv, kv], out_specs=row,
        out_shape=jax.ShapeDtypeStruct((s, D), BF16), compiler_params=_cp("parallel"),
    )(q, k, v)


def _mem_attn_bwd(q, k, v, do, *, name):
    s, m = q.shape[0], k.shape[0]
    tm = _tile(s, 256, 8)

    def body(q_ref, k_ref, v_ref, do_ref, dq_ref, dk_ref, dv_ref):
        @pl.when(pl.program_id(0) == 0)
        def _():
            dk_ref[...] = jnp.zeros_like(dk_ref)
            dv_ref[...] = jnp.zeros_like(dv_ref)

        _, vjp = jax.vjp(_mem_attn_math, q_ref[...].astype(F32), k_ref[...].astype(F32),
                         v_ref[...].astype(F32))
        dq, dk, dv = vjp(do_ref[...])
        dq_ref[...] = dq.astype(BF16)
        dk_ref[...] += dk
        dv_ref[...] += dv

    row = pl.BlockSpec((tm, D), lambda i: (i, 0))
    kv = pl.BlockSpec((m, D), lambda i: (0, 0))
    return pl.pallas_call(
        body, name=name, grid=(s // tm,), in_specs=[row, kv, kv, row], out_specs=[row, kv, kv],
        out_shape=[jax.ShapeDtypeStruct((s, D), BF16), jax.ShapeDtypeStruct((m, D), F32),
                   jax.ShapeDtypeStruct((m, D), F32)],
        compiler_params=_cp("arbitrary"),
    )(q, k, v, do)


DFF = 2816
FFN_TC = 1408
FFN_TM = 256


def _ffn_conv(ext_ref, cw, cb, tm):
    acc = cb
    for k in range(3):
        acc = acc + cw[k:k + 1, :] * ext_ref[pl.ds(HALO - 2 + k, tm), :]
    return acc


def _ffn_specs(s):
    tm, tc = FFN_TM, FFN_TC
    blk = pl.BlockSpec((tm, tc), lambda i, j: (i, j))
    halo = pl.BlockSpec((HALO, tc), lambda i, j: (jnp.maximum(i * (tm // HALO) - 1, 0), j))
    cw = pl.BlockSpec((3, tc), lambda i, j: (0, j))
    cb = pl.BlockSpec((1, tc), lambda i, j: (0, j))
    return tm, tc, blk, halo, cw, cb


def _glu_fwd(ug, uv, cwg, cwv, cbg, cbv, *, name):
    s = ug.shape[0]
    tm, tc, blk, halo, cw, cb = _ffn_specs(s)

    def body(g_ref, gh_ref, v_ref, vh_ref, cwg_ref, cwv_ref, cbg_ref, cbv_ref, f_ref, eg, ev):
        first = pl.program_id(0) == 0
        eg[0:HALO, :] = jnp.where(first, 0.0, gh_ref[...])
        eg[HALO:, :] = g_ref[...]
        ev[0:HALO, :] = jnp.where(first, 0.0, vh_ref[...])
        ev[HALO:, :] = v_ref[...]
        g = _ffn_conv(eg, cwg_ref[...], cbg_ref[...], tm)
        v = _ffn_conv(ev, cwv_ref[...], cbv_ref[...], tm)
        f_ref[...] = (_silu(g) * v).astype(BF16)

    return pl.pallas_call(
        body, name=name, grid=(s // tm, DFF // tc),
        in_specs=[blk, halo, blk, halo, cw, cw, cb, cb], out_specs=blk,
        out_shape=jax.ShapeDtypeStruct((s, DFF), BF16),
        scratch_shapes=[pltpu.VMEM((tm + HALO, tc), F32)] * 2,
        compiler_params=_cp("parallel", "parallel"),
    )(ug, ug, uv, uv, cwg, cwv, cbg, cbv)


def _ffn_bwd(ug, uv, df, cwg, cwv, cbg, cbv, *, name):
    s = ug.shape[0]
    tm, tc = FFN_TM, FFN_TC
    nb = s // tm
    rows_ext = tm + HALO

    def body(g_ref, gp_ref, gn_ref, v_ref, vp_ref, vn_ref, df_ref, dfn_ref, cwg_ref, cwv_ref, cbg_ref, cbv_ref,
             dxg_ref, dxv_ref, dcg_ref, dcv_ref, eg, ev, edg, edv):
        i = pl.program_id(1)
        first, last = i == 0, i == nb - 1

        @pl.when(first)
        def _():
            dcg_ref[...] = jnp.zeros_like(dcg_ref)
            dcv_ref[...] = jnp.zeros_like(dcv_ref)

        for e, prev, main, nxt in ((eg, gp_ref, g_ref, gn_ref), (ev, vp_ref, v_ref, vn_ref)):
            e[0:HALO, :] = jnp.where(first, 0.0, prev[...])
            e[HALO:HALO + tm, :] = main[...]
            e[HALO + tm:, :] = jnp.where(last, 0.0, nxt[...])
        dfe = jnp.concatenate([df_ref[...], jnp.where(last, 0.0, dfn_ref[...])], axis=0)
        cwgv, cwvv = cwg_ref[...], cwv_ref[...]
        g = _ffn_conv(eg, cwgv, cbg_ref[...], rows_ext)
        v = _ffn_conv(ev, cwvv, cbv_ref[...], rows_ext)
        sg = _sigmoid(g)
        edv[...] = dfe * g * sg
        edg[...] = dfe * v * sg * (1.0 + g * (1.0 - sg))
        for edu, eu, cw, dx_ref, dc_ref in ((edg, eg, cwgv, dxg_ref, dcg_ref), (edv, ev, cwvv, dxv_ref, dcv_ref)):
            du = edu[0:tm, :]
            dx = jnp.zeros((tm, tc), F32)
            sums = []
            for k in range(3):
                dx = dx + cw[k:k + 1, :] * edu[pl.ds(2 - k, tm), :]
                sums.append(jnp.sum(du * eu[pl.ds(HALO - 2 + k, tm), :], axis=0, keepdims=True))
            sums.append(jnp.sum(du, axis=0, keepdims=True))
            dx_ref[...] = dx.astype(BF16)
            dc_ref[0:4, :] += jnp.concatenate(sums, axis=0)

    blk = pl.BlockSpec((tm, tc), lambda j, i: (i, j))
    nxt = pl.BlockSpec((HALO, tc), lambda j, i: (jnp.minimum((i + 1) * (tm // HALO), s // HALO - 1), j))
    prv = pl.BlockSpec((HALO, tc), lambda j, i: (jnp.maximum(i * (tm // HALO) - 1, 0), j))
    cw = pl.BlockSpec((3, tc), lambda j, i: (0, j))
    cb = pl.BlockSpec((1, tc), lambda j, i: (0, j))
    acc = pl.BlockSpec((8, tc), lambda j, i: (0, j))
    return pl.pallas_call(
        body, name=name, grid=(DFF // tc, nb),
        in_specs=[blk, prv, nxt, blk, prv, nxt, blk, nxt, cw, cw, cb, cb],
        out_specs=[blk, blk, acc, acc],
        out_shape=[jax.ShapeDtypeStruct((s, DFF), BF16)] * 2 + [jax.ShapeDtypeStruct((8, DFF), F32)] * 2,
        scratch_shapes=[pltpu.VMEM((tm + 2 * HALO, tc), F32)] * 2 + [pltpu.VMEM((rows_ext, tc), F32)] * 2,
        compiler_params=_cp("parallel", "arbitrary"),
    )(ug, ug, ug, uv, uv, uv, df, df, cwg, cwv, cbg, cbv)


MESH = pl.DeviceIdType.MESH


def _all_gather(arrs, *, name):
    n = len(arrs)

    def body(*refs):
        x_refs, out_refs = refs[:n], refs[n:2 * n]
        send_sems, recv_sems, local_sems = refs[2 * n:]
        x, y, c = lax.axis_index("x"), lax.axis_index("y"), lax.axis_index("c")
        me, sibling = (x, y, c), (x, y, 1 - c)
        chips = [(1 - x, y), (x, 1 - y), (1 - x, 1 - y)]

        def blk(a, dev):
            return out_refs[a].at[4 * dev[0] + 2 * dev[1] + dev[2]]

        def copy(a, k, block, to, src=None):
            return pltpu.make_async_remote_copy(
                src_ref=blk(a, block) if src is None else src, dst_ref=blk(a, block),
                send_sem=send_sems.at[7 * a + k], recv_sem=recv_sems.at[7 * a + k],
                device_id=to, device_id_type=MESH)

        started = []
        mine = []
        for a in range(n):
            cp = pltpu.make_async_copy(x_refs[a], blk(a, me), local_sems.at[a])
            cp.start()
            mine.append(cp)
            first = [copy(a, 0, me, sibling, src=x_refs[a])]
            first += [copy(a, 1 + j, me, (*chip, c), src=x_refs[a]) for j, chip in enumerate(chips)]
            for cp in first:
                cp.start()
            started += first
        for a in range(n):
            for j, chip in enumerate(chips):
                copy(a, 1 + j, (*chip, c), me).wait_recv()
                fwd = copy(a, 4 + j, (*chip, c), sibling)
                fwd.start()
                started.append(fwd)
        for a in range(n):
            copy(a, 0, sibling, me).wait_recv()
            for j, chip in enumerate(chips):
                copy(a, 4 + j, (*chip, 1 - c), me).wait_recv()
        for cp in started:
            cp.wait_send()
        for cp in mine:
            cp.wait()

    any_spec = pl.BlockSpec(memory_space=pl.ANY)
    return pl.pallas_call(
        body, name=name,
        in_specs=[any_spec] * n, out_specs=[any_spec] * n,
        out_shape=[jax.ShapeDtypeStruct((NDEV,) + a.shape, a.dtype) for a in arrs],
        scratch_shapes=[pltpu.SemaphoreType.DMA((7 * n,)), pltpu.SemaphoreType.DMA((7 * n,)),
                        pltpu.SemaphoreType.DMA((n,))],
    )(*arrs)


class _Direct:
    SEMS = (pltpu.SemaphoreType.DMA((7,)), pltpu.SemaphoreType.DMA((7,)), pltpu.SemaphoreType.DMA((1,)))

    def __init__(self, src_ref, recv_ref, sems, gather):
        x, y, c = lax.axis_index("x"), lax.axis_index("y"), lax.axis_index("c")
        me = 4 * x + 2 * y + c
        send_sems, recv_sems, local_sem = sems
        src = (lambda pid: src_ref) if gather else (lambda pid: src_ref.at[pid])
        self.mine = pltpu.make_async_copy(src(me), recv_ref.at[me], local_sem.at[0])
        self.copies = []
        for k in range(1, NDEV):
            px = 1 - x if k & 4 else x
            py = 1 - y if k & 2 else y
            pc = 1 - c if k & 1 else c
            self.copies.append(pltpu.make_async_remote_copy(
                src_ref=src(4 * px + 2 * py + pc), dst_ref=recv_ref.at[me],
                send_sem=send_sems.at[k - 1], recv_sem=recv_sems.at[k - 1],
                device_id=(px, py, pc), device_id_type=MESH))

    def start(self):
        self.mine.start()
        for cp in self.copies:
            cp.start()

    def wait(self):
        for cp in self.copies:
            cp.wait_recv()
        for cp in self.copies:
            cp.wait_send()
        self.mine.wait()


def _recv_shape(src, gather):
    return jax.ShapeDtypeStruct(((NDEV,) + src.shape) if gather else src.shape, src.dtype)


class _Rides:
    def __init__(self, rides, gathers):
        self.n = len(rides)
        self.gathers = list(gathers)
        any_spec = pl.BlockSpec(memory_space=pl.ANY)
        self.in_specs = [any_spec] * self.n
        self.out_specs = [any_spec] * self.n
        self.out_shape = [_recv_shape(a, g) for a, g in zip(rides, gathers)]
        self.scratch = list(_Direct.SEMS) * self.n

    def split(self, refs, n_in, n_out, n_scratch):
        n = self.n
        ins, refs = refs[:n_in], refs[n_in:]
        rides, refs = refs[:n], refs[n:]
        outs, refs = refs[:n_out], refs[n_out:]
        gots, refs = refs[:n], refs[n:]
        scratch, sems = refs[:n_scratch], refs[n_scratch:]
        return ins, outs, scratch, (rides, gots, sems)

    def run(self, handles, first, last):
        rides, gots, sems = handles

        def all_of():
            return [_Direct(rides[a], gots[a], sems[3 * a:3 * a + 3], self.gathers[a]) for a in range(self.n)]

        @pl.when(first)
        def _():
            for e in all_of():
                e.start()

        @pl.when(last)
        def _():
            for e in all_of():
                e.wait()


def _exchange(arrs, gathers, *, name):
    rd = _Rides(arrs, gathers)

    def body(*refs):
        _, _, _, handles = rd.split(refs, 0, 0, 0)
        rd.run(handles, True, True)

    return pl.pallas_call(
        body, name=name, in_specs=rd.in_specs, out_specs=rd.out_specs, out_shape=rd.out_shape,
        scratch_shapes=rd.scratch,
    )(*arrs)


def _adamw(parts, w, m, v, *, name):
    r, cols = w.shape
    tm = _tile(r, 256, PACK_ALIGN)
    c1 = 1.0 - ADAM_B1 ** ADAM_STEP
    c2 = 1.0 - ADAM_B2 ** ADAM_STEP

    def body(p_ref, w_ref, m_ref, v_ref, g_ref, d_ref, nm_ref, nv_ref):
        g = p_ref[0].astype(F32)
        for i in range(1, NDEV):
            g = g + p_ref[i].astype(F32)
        nm = ADAM_B1 * m_ref[...] + (1.0 - ADAM_B1) * g
        nv = ADAM_B2 * v_ref[...] + (1.0 - ADAM_B2) * (g * g)
        d_ref[...] = -ADAM_LR * ((nm / c1) / (jnp.sqrt(nv / c2) + ADAM_EPS) + ADAM_WD * w_ref[...])
        g_ref[...] = g
        nm_ref[...] = nm
        nv_ref[...] = nv

    row = pl.BlockSpec((tm, cols), lambda i: (i, 0))
    return pl.pallas_call(
        body, name=name, grid=(r // tm,),
        in_specs=[pl.BlockSpec((NDEV, tm, cols), lambda i: (0, i, 0)), row, row, row],
        out_specs=[row] * 4, out_shape=[jax.ShapeDtypeStruct((r, cols), F32)] * 4,
        compiler_params=_cp("parallel"),
    )(parts, w, m, v)


PACK_ALIGN = 16


def _part_rows(shape):
    n = -(-math.prod(shape) // D)
    return n + (-n) % PACK_ALIGN


def _rows(a):
    flat = a.reshape(-1)
    pad = _part_rows(a.shape) * D - flat.shape[0]
    if pad:
        flat = jnp.concatenate([flat, jnp.zeros((pad,), flat.dtype)])
    return flat.reshape(-1, D)


def _pack(parts, total_rows):
    rows = [_rows(p) for p in parts]
    used = sum(r.shape[0] for r in rows)
    if total_rows > used:
        rows.append(jnp.zeros((total_rows - used, D), rows[0].dtype))
    return jnp.concatenate(rows, axis=0)


def _unpack(buf, shapes):
    out, r0 = [], 0
    for shp in shapes:
        n = math.prod(shp)
        out.append(buf[r0:r0 + _part_rows(shp)].reshape(-1)[:n].reshape(shp))
        r0 += _part_rows(shp)
    return out


SHARD = {"w_in": (D, 706), "w_out": (256, D), "w_mq": (128, D), "w_mk": (128, D), "w_mv": (128, D),
         "w_mo": (128, D), "w_up": (D, 704), "w_down": (352, D), "conv_ssd_w": (4, 192), "conv_ffn_w": (3, 704)}
GATHER_MID = ["w_out", "w_mq", "w_mk", "w_mv", "w_mo"]
GATHER_FFN = ["w_down"]
CONV_TAPS = ["conv_ssd_w", "conv_ffn_w"]
GRADS_PACKED = ["w_out", "w_mq", "w_mk", "w_mv", "w_mo", "w_down", "conv_ffn_w", "conv_ssd_w"]


def _layout(names):
    row0, r = {}, 0
    for n in names:
        row0[n] = r
        r += _part_rows(SHARD[n])
    return row0, r + (-r) % 128


SMALL = [("norm_mix_w", (1, D)), ("conv_ssd_b", (1, 1536)), ("dt_bias", (1, 16)), ("a_log", (1, 16)),
         ("d_skip", (1, 16)), ("ssd_norm_w", (1, D)), ("sb_norm_w", (1, D)), ("norm_mem_w", (1, D)),
         ("norm_memkv_w", (1, D)), ("norm_ffn_w", (1, D)), ("conv_ffn_b", (1, 5632)), ("norm_final_w", (D,))]
LOSS_ROW = sum(_part_rows(_shp) for _, _shp in SMALL)
SMALL_ROWS = LOSS_ROW + PACK_ALIGN
ORDER = ["norm_mix_w", "w_in", "conv_ssd_w", "conv_ssd_b", "dt_bias", "a_log", "d_skip", "ssd_norm_w",
         "sb_norm_w", "w_out", "norm_mem_w", "norm_memkv_w", "w_mq", "w_mk", "w_mv", "w_mo", "norm_ffn_w",
         "w_up", "conv_ffn_w", "conv_ffn_b", "w_down", "norm_final_w"]


def _pad_rows(a, nr):
    n = a.shape[1]
    return jnp.concatenate([a, jnp.zeros((NDEV, nr * D - n), a.dtype)], axis=1).reshape(NDEV, nr, D)


def _group_sum(lanes):
    return lanes.reshape(16, 64).sum(axis=1).reshape(1, 16)


def kernel(x, mem, norm_mix_w, w_in, conv_ssd_w, conv_ssd_b, dt_bias, a_log, d_skip, ssd_norm_w, sb_norm_w, w_out, norm_mem_w, norm_memkv_w, w_mq, w_mk, w_mv, w_mo, norm_ffn_w, w_up, conv_ffn_w, conv_ffn_b, w_down, norm_final_w, loss_target, m_norm_mix_w, m_w_in, m_conv_ssd_w, m_conv_ssd_b, m_dt_bias, m_a_log, m_d_skip, m_ssd_norm_w, m_sb_norm_w, m_w_out, m_norm_mem_w, m_norm_memkv_w, m_w_mq, m_w_mk, m_w_mv, m_w_mo, m_norm_ffn_w, m_w_up, m_conv_ffn_w, m_conv_ffn_b, m_w_down, m_norm_final_w, v_norm_mix_w, v_w_in, v_conv_ssd_w, v_conv_ssd_b, v_dt_bias, v_a_log, v_d_skip, v_ssd_norm_w, v_sb_norm_w, v_w_out, v_norm_mem_w, v_norm_memkv_w, v_w_mq, v_w_mk, v_w_mv, v_w_mo, v_norm_ffn_w, v_w_up, v_conv_ffn_w, v_conv_ffn_b, v_w_down, v_norm_final_w):
    P = dict(norm_mix_w=norm_mix_w, w_in=w_in, conv_ssd_w=conv_ssd_w, conv_ssd_b=conv_ssd_b, dt_bias=dt_bias, a_log=a_log, d_skip=d_skip, ssd_norm_w=ssd_norm_w, sb_norm_w=sb_norm_w, w_out=w_out, norm_mem_w=norm_mem_w, norm_memkv_w=norm_memkv_w, w_mq=w_mq, w_mk=w_mk, w_mv=w_mv, w_mo=w_mo, norm_ffn_w=norm_ffn_w, w_up=w_up, conv_ffn_w=conv_ffn_w, conv_ffn_b=conv_ffn_b, w_down=w_down, norm_final_w=norm_final_w)
    M = dict(norm_mix_w=m_norm_mix_w, w_in=m_w_in, conv_ssd_w=m_conv_ssd_w, conv_ssd_b=m_conv_ssd_b, dt_bias=m_dt_bias, a_log=m_a_log, d_skip=m_d_skip, ssd_norm_w=m_ssd_norm_w, sb_norm_w=m_sb_norm_w, w_out=m_w_out, norm_mem_w=m_norm_mem_w, norm_memkv_w=m_norm_memkv_w, w_mq=m_w_mq, w_mk=m_w_mk, w_mv=m_w_mv, w_mo=m_w_mo, norm_ffn_w=m_norm_ffn_w, w_up=m_w_up, conv_ffn_w=m_conv_ffn_w, conv_ffn_b=m_conv_ffn_b, w_down=m_w_down, norm_final_w=m_norm_final_w)
    V = dict(norm_mix_w=v_norm_mix_w, w_in=v_w_in, conv_ssd_w=v_conv_ssd_w, conv_ssd_b=v_conv_ssd_b, dt_bias=v_dt_bias, a_log=v_a_log, d_skip=v_d_skip, ssd_norm_w=v_ssd_norm_w, sb_norm_w=v_sb_norm_w, w_out=v_w_out, norm_mem_w=v_norm_mem_w, norm_memkv_w=v_norm_memkv_w, w_mq=v_w_mq, w_mk=v_w_mk, w_mv=v_w_mv, w_mo=v_w_mo, norm_ffn_w=v_norm_ffn_w, w_up=v_w_up, conv_ffn_w=v_conv_ffn_w, conv_ffn_b=v_conv_ffn_b, w_down=v_w_down, norm_final_w=v_norm_final_w)
    small_shapes = [shp for _, shp in SMALL]

    def packed(src, names, dtype=F32):
        return _pack([src[n][0] for n in names], _layout(names)[1]).astype(dtype)

    def columns(g):
        return g.transpose(1, 0, 2).reshape(g.shape[1], NDEV * g.shape[2])

    g_in, g_taps = _all_gather([w_in[0].astype(BF16), packed(P, CONV_TAPS)], name="gather_w_in")
    W_in = columns(g_in)
    cw_ssd = g_taps[:, 0].reshape(NDEV, -1)[:, :768].reshape(NDEV, 4, 192).transpose(1, 0, 2).reshape(4, XBC)
    cw_ffn = (g_taps[:, PACK_ALIGN:PACK_ALIGN + 3].reshape(NDEV, -1)[:, :2112].reshape(NDEV, 3, 704)
              .transpose(1, 0, 2).reshape(3, 2 * DFF))
    W_z, W_xbc, W_dt, W_qkv = W_in[:, :D], W_in[:, D:D + XBC], W_in[:, D + XBC:D + XBC + 16], W_in[:, D + XBC + 16:]
    W_dtr = jnp.repeat(W_dt, 64, axis=1)
    cwg, cwv = cw_ffn[:, :DFF], cw_ffn[:, DFF:]
    cbg, cbv = conv_ffn_b[:, :DFF], conv_ffn_b[:, DFF:]
    rep = lambda p: jnp.repeat(p, 64, axis=1)
    lanes = jnp.concatenate([rep(dt_bias), rep(a_log), rep(d_skip), ssd_norm_w, jnp.zeros((4, D), F32)], axis=0)

    xs, tgt, mm = x[0], loss_target[0], mem[0]

    h1 = _norm_fwd(xs, norm_mix_w, name="norm_mix")
    z = _mm(h1, W_z, name="proj_z")
    xbc = _mm(h1, W_xbc, name="proj_xbc")
    dtr = _mm(h1, W_dtr, name="proj_dt")
    qkv = _mm(h1, W_qkv, name="proj_qkv", out_dtype=BF16)
    y_ssd, states, g_up, g_ffn = _ssd_fwd(z, xbc, dtr, cw_ssd, conv_ssd_b, lanes,
                                          [w_up[0].astype(BF16), packed(P, GATHER_FFN, BF16)], name="ssd_fwd")
    o_sb, g_mid = _sb_fwd(qkv, [packed(P, GATHER_MID, BF16)], name="sb_fwd")
    r_mid = _layout(GATHER_MID)[0]
    W_out = g_mid[:, r_mid["w_out"]:r_mid["w_out"] + 256].reshape(2 * D, D)
    W_mq, W_mk, W_mv, W_mo = [g_mid[:, r_mid[n]:r_mid[n] + 128].reshape(D, D)
                              for n in ("w_mq", "w_mk", "w_mv", "w_mo")]
    W_up = columns(g_up)
    W_down = g_ffn[:, 0:352].reshape(DFF, D)
    W_upg, W_upv = W_up[:, :DFF], W_up[:, DFF:]
    y_sb = _head_norm_fwd(o_sb, sb_norm_w, name="sb_norm")
    ymix = jnp.concatenate([y_ssd, y_sb], axis=1)
    x1 = _mm(ymix, W_out, add=xs, name="proj_out")
    h2 = _norm_fwd(x1, norm_mem_w, name="norm_mem")
    mn = _norm_fwd(mm, norm_memkv_w, name="norm_memkv")
    qm = _mm(h2, W_mq, name="mem_q", out_dtype=BF16)
    km = _mm(mn, W_mk, name="mem_k", out_dtype=BF16)
    vm = _mm(mn, W_mv, name="mem_v", out_dtype=BF16)
    om = _mem_attn_fwd(qm, km, vm, name="mem_attn")
    x2 = _mm(om, W_mo, add=x1, name="mem_o")
    h3 = _norm_fwd(x2, norm_ffn_w, name="norm_ffn")
    ug = _mm(h3, W_upg, name="ffn_up_g")
    uv = _mm(h3, W_upv, name="ffn_up_v")
    f = _glu_fwd(ug, uv, cwg, cwv, cbg, cbv, name="ffn_glu")
    x3 = _mm(f, W_down, add=x2, name="ffn_down")
    dx3, g_nfinal, loss_part = _final(x3, norm_final_w.reshape(1, D), tgt, name="final_loss")

    G = {}
    G["w_down"] = _mm(f, dx3, trans_a=True, name="g_w_down")
    df = _mm(dx3, W_down.T, name="d_f")
    dupg, dupv, dcg, dcv = _ffn_bwd(ug, uv, df, cwg, cwv, cbg, cbv, name="ffn_glu_bwd")
    G["w_up"] = jnp.concatenate([_mm(h3, dupg, trans_a=True, name="g_w_up_g"),
                                 _mm(h3, dupv, trans_a=True, name="g_w_up_v")], axis=1)
    G["conv_ffn_w"] = jnp.concatenate([dcg[0:3], dcv[0:3]], axis=1)
    G["conv_ffn_b"] = jnp.concatenate([dcg[3:4], dcv[3:4]], axis=1)
    dh3 = _mm(dupg, W_upg.T, name="d_h3_g")
    dh3 = _mm(dupv, W_upv.T, add=dh3, name="d_h3_v")
    dx2, G["norm_ffn_w"] = _norm_bwd(x2, norm_ffn_w, dh3, dx3, name="norm_ffn_bwd")
    G["w_mo"] = _mm(om, dx2, trans_a=True, name="g_w_mo")
    dom = _mm(dx2, W_mo.T, name="d_om")
    dqm, dkm, dvm = _mem_attn_bwd(qm, km, vm, dom, name="mem_attn_bwd")
    G["w_mq"] = _mm(h2, dqm, trans_a=True, name="g_w_mq")
    G["w_mk"] = _mm(mn, dkm, trans_a=True, name="g_w_mk")
    G["w_mv"] = _mm(mn, dvm, trans_a=True, name="g_w_mv")
    dh2 = _mm(dqm, W_mq.T, name="d_h2")
    dmn = _mm(dkm, W_mk.T, name="d_mn_k")
    dmn = _mm(dvm, W_mv.T, add=dmn, name="d_mn_v")
    _, G["norm_memkv_w"] = _norm_bwd(mm, norm_memkv_w, dmn, None, name="norm_memkv_bwd")
    dx1, G["norm_mem_w"] = _norm_bwd(x1, norm_mem_w, dh2, dx2, name="norm_mem_bwd")
    G["w_out"] = _mm(ymix, dx1, trans_a=True, name="g_w_out")
    dymix = _mm(dx1, W_out.T, name="d_ymix")
    do_sb, G["sb_norm_w"] = _head_norm_bwd(o_sb, sb_norm_w, dymix, name="sb_norm_bwd")

    dz, dxbc, ddtr, dlanes, dconv = _ssd_bwd(z, xbc, dtr, states, dymix, cw_ssd, conv_ssd_b, lanes, name="ssd_bwd")
    G["dt_bias"], G["a_log"], G["d_skip"] = [_group_sum(dlanes[i:i + 1]) for i in range(3)]
    G["ssd_norm_w"] = dlanes[3:4]
    G["conv_ssd_w"], G["conv_ssd_b"] = dconv[0:4], dconv[4:5]

    def col_slabs(g, cols):
        return g.reshape(g.shape[0], NDEV, cols).transpose(1, 0, 2).astype(BF16)

    def packed_slabs(names):
        parts = []
        for n in names:
            shp = SHARD[n]
            if shp[-1] == D:
                t = G[n].reshape((NDEV,) + shp)
            else:
                t = _pad_rows(G[n].reshape(shp[0], NDEV, shp[1]).transpose(1, 0, 2).reshape(NDEV, -1),
                              _part_rows(shp))
            parts.append(jnp.pad(t, ((0, 0), (0, _part_rows(shp) - t.shape[1]), (0, 0))))
        used = sum(t.shape[1] for t in parts)
        parts.append(jnp.zeros((NDEV, _layout(names)[1] - used, D), F32))
        return jnp.concatenate(parts, axis=1).astype(BF16)

    dq, dk, dv, recv_packed, recv_up = _sb_bwd(qkv, o_sb, do_sb, [packed_slabs(GRADS_PACKED), col_slabs(G["w_up"], 704)],
                                               name="sb_bwd")
    dqkv = jnp.concatenate([dq, dk, dv], axis=1)
    g_wdt = _mm(h1, ddtr, trans_a=True, name="g_w_dt").reshape(D, 16, 64).sum(axis=2)
    G["w_in"] = jnp.concatenate([_mm(h1, dz, trans_a=True, name="g_w_z"),
                                 _mm(h1, dxbc, trans_a=True, name="g_w_xbc"), g_wdt,
                                 _mm(h1, dqkv, trans_a=True, name="g_w_qkv")], axis=1)
    dh1 = _mm(dz, W_z.T, name="d_h1_z")
    dh1 = _mm(dxbc, W_xbc.T, add=dh1, name="d_h1_xbc")
    dh1 = _mm(ddtr, W_dtr.T, add=dh1, name="d_h1_dt")
    dh1 = _mm(dqkv, W_qkv.T, add=dh1, name="d_h1_qkv")
    dx, G["norm_mix_w"] = _norm_bwd(xs, norm_mix_w, dh1, dx1, name="norm_mix_bwd")
    G["norm_final_w"] = g_nfinal.reshape(D)

    small_g = _pack([G[n] for n, _ in SMALL] + [loss_part], SMALL_ROWS)
    recv_in, parts_small = _exchange([col_slabs(G["w_in"], 706), small_g], [False, True], name="exchange_grads")
    outs_packed = _adamw(recv_packed, packed(P, GRADS_PACKED), packed(M, GRADS_PACKED), packed(V, GRADS_PACKED),
                         name="adamw_packed")
    outs_up = _adamw(recv_up, w_up[0], m_w_up[0], v_w_up[0], name="adamw_w_up")
    outs_in = _adamw(recv_in, w_in[0], m_w_in[0], v_w_in[0], name="adamw_w_in")
    outs_small = _adamw(parts_small, _pack([P[n] for n, _ in SMALL], SMALL_ROWS),
                        _pack([M[n] for n, _ in SMALL], SMALL_ROWS),
                        _pack([V[n] for n, _ in SMALL], SMALL_ROWS), name="adamw_replicated")

    res = {}
    for i, kind in enumerate(("grad", "delta", "new_m", "new_v")):
        for n, val in zip(GRADS_PACKED, _unpack(outs_packed[i], [SHARD[n] for n in GRADS_PACKED])):
            res[kind, n] = val.reshape((1,) + SHARD[n])
        res[kind, "w_up"] = outs_up[i].reshape((1,) + SHARD["w_up"])
        res[kind, "w_in"] = outs_in[i].reshape((1,) + SHARD["w_in"])
        for (n, shp), val in zip(SMALL, _unpack(outs_small[i], small_shapes)):
            res[kind, n] = val
    loss = outs_small[0][LOSS_ROW, 0]
    out = [loss, dx.reshape(1, -1, D)]
    for kind in ("grad", "delta", "new_m", "new_v"):
        out += [res[kind, n] for n in ORDER]
    return tuple(out)
```

```python
import functools
import math

import jax
import jax.numpy as jnp
from jax import lax
from jax.experimental import pallas as pl
from jax.experimental.pallas import tpu as pltpu

F32 = jnp.float32
BF16 = jnp.bfloat16

D = 1024
NDEV = 8
EPS = 1e-6
SSD_CHUNK = 128
HALO = 8
VMEM_LIMIT = 56 * 2**20

ADAM_LR, ADAM_B1, ADAM_B2, ADAM_EPS, ADAM_WD, ADAM_STEP = 0.001, 0.9, 0.999, 1e-08, 0.01, 10


def _cp(*sem):
    return pltpu.CompilerParams(dimension_semantics=sem, vmem_limit_bytes=VMEM_LIMIT)


def _tile(n, cap, mult):
    if n <= cap:
        return n
    for d in range(cap - cap % mult, 0, -mult):
        if n % d == 0:
            return d
    raise ValueError(f"no tile for {n}")


def _sigmoid(x):
    return 1.0 / (1.0 + jnp.exp(-x))


def _silu(x):
    return x * _sigmoid(x)


def _softplus(x):
    return jnp.maximum(x, 0.0) + jnp.log1p(jnp.exp(-jnp.abs(x)))


def _terms(x, n):
    out = []
    r = x.astype(F32)
    for i in range(n):
        h = r.astype(BF16)
        out.append(h)
        if i + 1 < n:
            r = r - h.astype(F32)
    return out


_DIMS = {"nn": ((1,), (0,)), "nt": ((1,), (1,)), "tn": ((0,), (0,))}


def _dot_raw(form, a, b, ta, tb):
    acc = None
    for ai in _terms(a, ta):
        for bi in _terms(b, tb):
            d = lax.dot_general(ai, bi, (_DIMS[form], ((), ())), preferred_element_type=F32)
            acc = d if acc is None else acc + d
    return acc


@functools.lru_cache(maxsize=None)
def _dot_fn(form, ta, tb):
    @jax.custom_vjp
    def f(a, b):
        return _dot_raw(form, a, b, ta, tb)

    def fwd(a, b):
        return f(a, b), (a, b)

    def bwd(res, ct):
        a, b = res
        if form == "nn":
            return _dot_fn("nt", ta, tb)(ct, b), _dot_fn("tn", ta, tb)(a, ct)
        if form == "nt":
            return _dot_fn("nn", ta, tb)(ct, b), _dot_fn("tn", tb, ta)(ct, a)
        return _dot_fn("nt", tb, ta)(b, ct), _dot_fn("nn", ta, tb)(a, ct)

    f.defvjp(fwd, bwd)
    return f


def _dot(form, a, b, ta=1, tb=1):
    return _dot_fn(form, ta, tb)(a, b)


@functools.lru_cache(maxsize=None)
def _take_fn(axis, idx):
    @jax.custom_vjp
    def f(x):
        return x[:, idx:idx + 1] if axis == 1 else x[idx:idx + 1, :]

    def fwd(x):
        return f(x), x.shape

    def bwd(shape, ct):
        io = lax.broadcasted_iota(jnp.int32, shape, axis)
        return (jnp.where(io == idx, jnp.broadcast_to(ct, shape), 0.0),)

    f.defvjp(fwd, bwd)
    return f


@functools.lru_cache(maxsize=None)
def _split_fn(width, n):
    @jax.custom_vjp
    def f(x):
        return tuple(x[:, i * width:(i + 1) * width] for i in range(n))

    def fwd(x):
        return f(x), None

    def bwd(_, cts):
        return (jnp.concatenate(list(cts), axis=1),)

    f.defvjp(fwd, bwd)
    return f


def _split(x, width):
    return _split_fn(width, x.shape[1] // width)(x)


def _iota(shape, axis):
    return lax.broadcasted_iota(jnp.int32, shape, axis)


MM_VMEM_BUDGET = 36 * 2**20


def _mm_tiles(m, n, kt, trans_a, a_bytes, b_bytes, out_bytes, add_bytes):
    tn = _tile(n, 1536, 128)
    for tm_cap in (1408, 1024, 512, 256, 128):
        tm = _tile(m, tm_cap, 128 if trans_a else 8)
        for tk_cap in (kt, 4096, 2048, 1024, 512):
            tk = _tile(kt, tk_cap, 128)
            blocks = tm * tk * a_bytes + tk * tn * b_bytes + tm * tn * (out_bytes + add_bytes)
            if 2 * blocks + (tm * tn * 4 if tk < kt else 0) <= MM_VMEM_BUDGET:
                return tm, tn, tk
    raise ValueError(f"no matmul tiling for {(m, n, kt)}")


def _mm(a, b, *, name, add=None, trans_a=False, trans_b=False, out_dtype=F32, rides=()):
    assert not (trans_a and trans_b)
    if trans_a:
        kt, m = a.shape
    else:
        m, kt = a.shape
    n, kt2 = b.shape if trans_b else b.shape[::-1]
    assert kt == kt2, (a.shape, b.shape)
    tm, tn, tk = _mm_tiles(m, n, kt, trans_a, a.dtype.itemsize, b.dtype.itemsize,
                           jnp.dtype(out_dtype).itemsize, 0 if add is None else add.dtype.itemsize)
    nk = kt // tk
    grid = (m // tm, n // tn, nk)
    rd = _Rides(rides, [False] * len(rides))
    n_in = 2 if add is None else 3

    def body(*all_refs):
        ins, (o_ref,), scratch, handles = rd.split(all_refs, n_in, 1, 1 if nk > 1 else 0)
        refs = (*ins, o_ref, *scratch)
        if rides:
            ids = [pl.program_id(ax) for ax in range(3)]
            rd.run(handles, (ids[0] == 0) & (ids[1] == 0) & (ids[2] == 0),
                   (ids[0] == grid[0] - 1) & (ids[1] == grid[1] - 1) & (ids[2] == grid[2] - 1))
        if add is None:
            a_ref, b_ref, o_ref = refs[:3]
        else:
            a_ref, b_ref, add_ref, o_ref = refs[:4]
        k = pl.program_id(2)
        av = a_ref[...].astype(BF16)
        bv = b_ref[...].astype(BF16)
        dims = _DIMS["tn" if trans_a else "nt" if trans_b else "nn"]
        d = lax.dot_general(av, bv, (dims, ((), ())), preferred_element_type=F32)

        def finish(r):
            if add is not None:
                r = r + add_ref[...]
            o_ref[...] = r.astype(out_dtype)

        if nk == 1:
            finish(d)
        else:
            acc = refs[-1]

            @pl.when(k == 0)
            def _():
                acc[...] = d

            @pl.when((k > 0) & (k < nk - 1))
            def _():
                acc[...] += d

            @pl.when(k == nk - 1)
            def _():
                finish(acc[...] + d)

    a_spec = (pl.BlockSpec((tk, tm), lambda i, j, k: (k, i)) if trans_a
              else pl.BlockSpec((tm, tk), lambda i, j, k: (i, k)))
    b_spec = (pl.BlockSpec((tn, tk), lambda i, j, k: (j, k)) if trans_b
              else pl.BlockSpec((tk, tn), lambda i, j, k: (k, j)))
    in_specs = [a_spec, b_spec]
    args = [a, b]
    if add is not None:
        in_specs.append(pl.BlockSpec((tm, tn), lambda i, j, k: (i, j)))
        args.append(add)
    out = pl.pallas_call(
        body, name=name, grid=grid,
        in_specs=in_specs + rd.in_specs,
        out_specs=[pl.BlockSpec((tm, tn), lambda i, j, k: (i, j))] + rd.out_specs,
        out_shape=[jax.ShapeDtypeStruct((m, n), out_dtype)] + rd.out_shape,
        scratch_shapes=([pltpu.VMEM((tm, tn), F32)] if nk > 1 else []) + rd.scratch,
        compiler_params=_cp(*(("arbitrary",) * 3 if rides else ("parallel", "parallel", "arbitrary"))),
    )(*args, *rides)
    return out if rides else out[0]


def _rstd(x):
    return lax.rsqrt(jnp.mean(x * x, axis=-1, keepdims=True) + EPS)


def _norm_fwd(x, w, *, name):
    s = x.shape[0]
    tm = _tile(s, 512, 8)

    def body(x_ref, w_ref, o_ref):
        xv = x_ref[...]
        o_ref[...] = (xv * _rstd(xv) * w_ref[...]).astype(BF16)

    return pl.pallas_call(
        body, name=name, grid=(s // tm,),
        in_specs=[pl.BlockSpec((tm, D), lambda i: (i, 0)), pl.BlockSpec((1, D), lambda i: (0, 0))],
        out_specs=pl.BlockSpec((tm, D), lambda i: (i, 0)),
        out_shape=jax.ShapeDtypeStruct((s, D), BF16), compiler_params=_cp("parallel"),
    )(x, w)


def _norm_bwd_math(xv, wv, dy):
    r = _rstd(xv)
    xh = xv * r
    dxh = dy * wv
    dx = r * (dxh - xh * jnp.mean(dxh * xh, axis=-1, keepdims=True))
    dw = jnp.sum(dy * xh, axis=0, keepdims=True)
    return dx, dw


def _norm_bwd(x, w, dy, add, *, name):
    s = x.shape[0]
    tm = _tile(s, 256, 8)

    def body(*refs):
        if add is None:
            x_ref, w_ref, dy_ref, dx_ref, dw_ref = refs
        else:
            x_ref, w_ref, dy_ref, add_ref, dx_ref, dw_ref = refs

        @pl.when(pl.program_id(0) == 0)
        def _():
            dw_ref[...] = jnp.zeros_like(dw_ref)

        dx, dw = _norm_bwd_math(x_ref[...], w_ref[...], dy_ref[...])
        if add is not None:
            dx = dx + add_ref[...]
        dx_ref[...] = dx
        dw_ref[...] += dw

    row = pl.BlockSpec((tm, D), lambda i: (i, 0))
    vec = pl.BlockSpec((1, D), lambda i: (0, 0))
    in_specs = [row, vec, row] + ([row] if add is not None else [])
    args = [x, w, dy] + ([add] if add is not None else [])
    return pl.pallas_call(
        body, name=name, grid=(s // tm,), in_specs=in_specs, out_specs=[row, vec],
        out_shape=[jax.ShapeDtypeStruct((s, D), F32), jax.ShapeDtypeStruct((1, D), F32)],
        compiler_params=_cp("arbitrary"),
    )(*args)


def _final(x3, w, target, *, name):
    s = x3.shape[0]
    tm = _tile(s, 256, 8)

    def body(x_ref, w_ref, t_ref, dx_ref, dw_ref, loss_ref):
        @pl.when(pl.program_id(0) == 0)
        def _():
            dw_ref[...] = jnp.zeros_like(dw_ref)
            loss_ref[...] = jnp.zeros_like(loss_ref)

        xv = x_ref[...]
        wv = w_ref[...]
        y = xv * _rstd(xv) * wv
        err = y - t_ref[...]
        loss_ref[...] += 0.5 * jnp.sum(jnp.mean(err * err, axis=-1, keepdims=True))
        dx, dw = _norm_bwd_math(xv, wv, err * (1.0 / D))
        dx_ref[...] = dx
        dw_ref[...] += dw

    row = pl.BlockSpec((tm, D), lambda i: (i, 0))
    vec = pl.BlockSpec((1, D), lambda i: (0, 0))
    return pl.pallas_call(
        body, name=name, grid=(s // tm,), in_specs=[row, vec, row], out_specs=[row, vec, vec],
        out_shape=[jax.ShapeDtypeStruct((s, D), F32), jax.ShapeDtypeStruct((1, D), F32),
                   jax.ShapeDtypeStruct((1, D), F32)],
        compiler_params=_cp("arbitrary"),
    )(x3, w, target)


def _head_norm_math(o, w):
    lane = _iota((128, 128), 0) // 64
    bd = (lane == _iota((128, 128), 1) // 64).astype(F32)
    outs = []
    for op in _split(o, 128):
        ms = _dot("nn", op * op, bd, 2, 1) * (1.0 / 64)
        outs.append(op * lax.rsqrt(ms + EPS))
    return jnp.concatenate(outs, axis=1) * w


def _head_norm_fwd(o, w, *, name):
    s = o.shape[0]
    tm = _tile(s, 256, 8)

    def body(o_ref, w_ref, y_ref):
        y_ref[...] = _head_norm_math(o_ref[...], w_ref[...]).astype(BF16)

    row = pl.BlockSpec((tm, D), lambda i: (i, 0))
    vec = pl.BlockSpec((1, D), lambda i: (0, 0))
    return pl.pallas_call(
        body, name=name, grid=(s // tm,), in_specs=[row, vec], out_specs=row,
        out_shape=jax.ShapeDtypeStruct((s, D), BF16), compiler_params=_cp("parallel"),
    )(o, w)


def _head_norm_bwd(o, w, dymix, *, name):
    s = o.shape[0]
    tm = _tile(s, 256, 8)

    def body(o_ref, w_ref, dy_ref, do_ref, dw_ref):
        @pl.when(pl.program_id(0) == 0)
        def _():
            dw_ref[...] = jnp.zeros_like(dw_ref)

        _, vjp = jax.vjp(_head_norm_math, o_ref[...], w_ref[...])
        do, dw = vjp(dy_ref[...])
        do_ref[...] = do
        dw_ref[...] += dw

    row = pl.BlockSpec((tm, D), lambda i: (i, 0))
    vec = pl.BlockSpec((1, D), lambda i: (0, 0))
    return pl.pallas_call(
        body, name=name, grid=(s // tm,),
        in_specs=[row, vec, pl.BlockSpec((tm, D), lambda i: (i, 1))], out_specs=[row, vec],
        out_shape=[jax.ShapeDtypeStruct((s, D), F32), jax.ShapeDtypeStruct((1, D), F32)],
        compiler_params=_cp("arbitrary"),
    )(o, w, dymix)


SB_BQ = 256
SB_BK = 256


def _sb_consts():
    r = _iota((SB_BK, SB_BK), 0)
    c = _iota((SB_BK, SB_BK), 1)
    u_excl = (r > c).astype(BF16)
    u_incl = (r >= c).astype(BF16)
    return u_excl, u_incl


SB_LANES = 256
SB_NCH = SB_LANES // 64


def _nt(a, b):
    return lax.dot_general(a, b, (_DIMS["nt"], ((), ())), preferred_element_type=F32)


def _tn(a, b):
    return lax.dot_general(a, b, (_DIMS["tn"], ((), ())), preferred_element_type=F32)


def _nn(a, b):
    return jnp.dot(a, b, preferred_element_type=F32)


def _sb_heads(ref):
    out = []
    for hp in range(SB_LANES // 128):
        v = ref[:, 128 * hp:128 * (hp + 1)]
        first = _iota(v.shape, 1) < 64
        out += [jnp.where(first, v, 0).astype(BF16), jnp.where(first, 0, v).astype(BF16)]
    return out


SB_STRIP = 32


def _neg_abs(x):
    bits = lax.bitcast_convert_type(x, jnp.uint32) | jnp.uint32(0x80000000)
    return lax.bitcast_convert_type(bits, F32)


def _sb_block(ref, j):
    off = pl.multiple_of(j * SB_BK, SB_BK)
    return [ref[pl.ds(off, SB_BK), 128 * hp:128 * (hp + 1)] for hp in range(SB_NCH // 2)]


SB_DEAD = 104.0


def _sb_live(nlrun):
    m = nlrun[0]
    for x in nlrun[1:]:
        m = jnp.minimum(m, x)
    return jnp.min(m) < SB_DEAD


def _sb_strips():
    return [(r, pl.ds(r, SB_STRIP)) for r in range(0, SB_BQ, SB_STRIP)]


def _sb_diag_mask(r):
    return _iota((SB_STRIP, SB_BK), 1) < _iota((SB_STRIP, SB_BK), 0) + r


def _sb_soft(z, mask):
    e = jnp.exp(_neg_abs(z))
    nl = jnp.maximum(z, 0.0) + jnp.log(1.0 + e)
    if mask is not None:
        nl = jnp.where(mask, nl, 0.0)
    return e, nl


def _sb_split_to(hl_ref, rows, x):
    hi, lo = _terms(x, 2)
    hl_ref[rows, 0:SB_BK] = hi
    hl_ref[rows, SB_BK:2 * SB_BK] = lo


def _sb_stage_soft(z_ref, nl_ref, diag):
    for r, rows in _sb_strips():
        _, nl = _sb_soft(z_ref[rows, :], _sb_diag_mask(r) if diag else None)
        nl_ref[rows, 0:SB_BK] = nl.astype(BF16)


def _sb_stage_weights(z_ref, c_ref, a_ref, nlrun, diag):
    for r, rows in _sb_strips():
        a = jnp.exp(z_ref[rows, :] - c_ref[rows, :] - nlrun[r:r + SB_STRIP, :])
        if diag:
            a = jnp.where(_sb_diag_mask(r), a, 0.0)
        a_ref[rows, :] = a.astype(BF16)


def _sb_fwd(qkv, rides, *, name):
    s = qkv.shape[0]
    nq = s // SB_BQ
    ng = D // SB_LANES
    assert SB_BQ == SB_BK
    rd = _Rides(rides, [True] * len(rides))

    def body(*refs):
        (q_ref, k_ref, v_ref), (o_ref,), (zbuf, nlbuf, cbuf, abuf), handles = rd.split(refs, 3, 1, 4)
        i = pl.program_id(1)
        step_no = pl.program_id(0) * nq + i
        rd.run(handles, step_no == 0, step_no == ng * nq - 1)

        _, u_incl = _sb_consts()
        lane_a = _iota((SB_BQ, 128), 1) < 64
        qh = [q * 0.125 for q in _sb_heads(q_ref)]

        def tile(j, accs, nlrun, diag):
            kbs = _sb_block(k_ref, j)
            for c in range(SB_NCH):
                zbuf[c] = _nt(qh[c], kbs[c // 2])
            for c in range(SB_NCH):
                _sb_stage_soft(zbuf.at[c], nlbuf.at[c], diag)
                cbuf[c] = _nn(nlbuf[c], u_incl)
            for c in range(SB_NCH):
                _sb_stage_weights(zbuf.at[c], cbuf.at[c], abuf.at[c], nlrun[c], diag)
            nlrun = tuple(nlrun[c] + cbuf[c, :, 0:1] for c in range(SB_NCH))
            vbs = _sb_block(v_ref, j)
            outs = [_nn(abuf[c], vbs[c // 2]) for c in range(SB_NCH)]
            accs = tuple(acc + jnp.where(lane_a, outs[2 * hp], outs[2 * hp + 1]) for hp, acc in enumerate(accs))
            return accs, nlrun

        accs, nlrun = tile(i, (jnp.zeros((SB_BQ, 128), F32),) * (SB_NCH // 2),
                           (jnp.zeros((SB_BQ, 1), F32),) * SB_NCH, True)

        def step(carry):
            j, _, accs, nlrun = carry
            accs, nlrun = tile(j, accs, nlrun, False)
            return j - 1, _sb_live(nlrun), accs, nlrun

        _, _, accs, _ = lax.while_loop(lambda c: (c[0] >= 0) & c[1], step, (i - 1, _sb_live(nlrun), accs, nlrun))
        o_ref[...] = jnp.concatenate(accs, axis=1)

    return pl.pallas_call(
        body, name=name, grid=(ng, nq),
        in_specs=[pl.BlockSpec((SB_BQ, SB_LANES), lambda g, i: (i, g)),
                  pl.BlockSpec((s, SB_LANES), lambda g, i: (0, ng + g)),
                  pl.BlockSpec((s, SB_LANES), lambda g, i: (0, 2 * ng + g)), *rd.in_specs],
        out_specs=[pl.BlockSpec((SB_BQ, SB_LANES), lambda g, i: (i, g)), *rd.out_specs],
        out_shape=[jax.ShapeDtypeStruct((s, D), F32), *rd.out_shape],
        scratch_shapes=[pltpu.VMEM((SB_NCH, SB_BQ, SB_BK), F32), pltpu.VMEM((SB_NCH, SB_BQ, SB_BK), BF16),
                        pltpu.VMEM((SB_NCH, SB_BQ, SB_BK), F32), pltpu.VMEM((SB_NCH, SB_BQ, SB_BK), BF16),
                        *rd.scratch],
        compiler_params=_cp("arbitrary", "arbitrary"),
    )(qkv, qkv, qkv, *rides)


def _sb_bwd(qkv, o, do, rides, *, name):
    s = qkv.shape[0]
    nq = s // SB_BQ
    ng = D // SB_LANES
    nhp = SB_NCH // 2
    rd = _Rides(rides, [False] * len(rides))

    def body(*refs):
        ins, outs, scratch, handles = rd.split(refs, 5, 3, 11)
        q_ref, k_ref, v_ref, o_ref, do_ref = ins
        dq_ref, dk_hbm, dv_hbm = outs
        dk_acc, dv_acc, dk16, dv16, sems, zbuf, gbuf, hl, cbuf, abuf, dzbuf = scratch
        g_idx = pl.program_id(0)
        i = pl.program_id(1)
        step_no = g_idx * nq + i
        rd.run(handles, step_no == 0, step_no == ng * nq - 1)

        @pl.when(i == 0)
        def _():
            dk_acc[...] = jnp.zeros_like(dk_acc)
            dv_acc[...] = jnp.zeros_like(dv_acc)

        _, u_incl = _sb_consts()
        u2 = jnp.concatenate([u_incl, u_incl], axis=0)
        lane_a = _iota((SB_BQ, 128), 1) < 64
        lane_k = _iota((SB_BK, 128), 1) < 64
        qh = [q * 0.125 for q in _sb_heads(q_ref)]
        qf = [q_ref[:, 128 * hp:128 * (hp + 1)] for hp in range(nhp)]
        doh = _sb_heads(do_ref)
        dof = [do_ref[:, 128 * hp:128 * (hp + 1)].astype(BF16) for hp in range(nhp)]
        delta = []
        for hp in range(nhp):
            prod = dof[hp].astype(F32) * o_ref[:, 128 * hp:128 * (hp + 1)]
            delta += [jnp.sum(jnp.where(lane_a, prod, 0.0), axis=1, keepdims=True),
                      jnp.sum(jnp.where(lane_a, 0.0, prod), axis=1, keepdims=True)]

        def pre(slot, j):
            kbs = _sb_block(k_ref, j)
            vbs = _sb_block(v_ref, j)
            for c in range(SB_NCH):
                zbuf[slot, c] = _nt(qh[c], kbs[c // 2])
                gbuf[slot, c] = _nt(doh[c], vbs[c // 2])

        def stage_g(c, slot):
            for _, rows in _sb_strips():
                g = abuf[slot, c, rows, :].astype(F32) * gbuf[slot, c, rows, :]
                gbuf[slot, c, rows, :] = g
                _sb_split_to(hl.at[c], rows, g)

        def stage_dz(c, slot, grun, diag):
            for r, rows in _sb_strips():
                z = zbuf[slot, c, rows, :]
                g = gbuf[slot, c, rows, :]
                cs = (delta[c] - grun)[r:r + SB_STRIP, :] - cbuf[c, rows, :]
                sig = 1.0 / (1.0 + jnp.exp(-z))
                dz = g - (g + cs) * sig
                if diag:
                    dz = jnp.where(_sb_diag_mask(r), dz, 0.0)
                dzbuf[slot, c, rows, :] = dz.astype(BF16)

        def chain(slot, nlrun, grun, diag):
            for c in range(SB_NCH):
                _sb_stage_soft(zbuf.at[slot, c], hl.at[c], diag)
                cbuf[c] = _nn(hl[c, :, 0:SB_BK], u_incl)
            nl_tot = []
            for c in range(SB_NCH):
                _sb_stage_weights(zbuf.at[slot, c], cbuf.at[c], abuf.at[slot, c], nlrun[c], diag)
                nl_tot.append(cbuf[c, :, 0:1])
                stage_g(c, slot)
                cbuf[c] = _nn(hl[c], u2)
            g_tot = []
            for c in range(SB_NCH):
                stage_dz(c, slot, grun[c], diag)
                g_tot.append(cbuf[c, :, 0:1])
            return (tuple(a + b for a, b in zip(nlrun, nl_tot)), tuple(a + b for a, b in zip(grun, g_tot)))

        def post(slot, j, dqs):
            off = pl.multiple_of(j * SB_BK, SB_BK)
            kbs = _sb_block(k_ref, j)
            dq_t = [_nn(dzbuf[slot, c], kbs[c // 2]) for c in range(SB_NCH)]
            dk_t = [_tn(dzbuf[slot, c], qf[c // 2]) for c in range(SB_NCH)]
            dv_t = [_tn(abuf[slot, c], dof[c // 2]) for c in range(SB_NCH)]
            for hp in range(nhp):
                cols = slice(128 * hp, 128 * (hp + 1))
                dk_acc[pl.ds(off, SB_BK), cols] += 0.125 * jnp.where(lane_k, dk_t[2 * hp], dk_t[2 * hp + 1])
                dv_acc[pl.ds(off, SB_BK), cols] += jnp.where(lane_k, dv_t[2 * hp], dv_t[2 * hp + 1])
            return tuple(dq + jnp.where(lane_a, dq_t[2 * hp], dq_t[2 * hp + 1]) for hp, dq in enumerate(dqs))

        def tile(j, dqs, nlrun, grun, diag):
            pre(0, j)
            nlrun, grun = chain(0, nlrun, grun, diag)
            return post(0, j, dqs), nlrun, grun

        zero = (jnp.zeros((SB_BQ, 1), F32),) * SB_NCH
        dqs, nlrun, grun = tile(i, (jnp.zeros((SB_BQ, 128), F32),) * nhp, zero, zero, True)

        def step(carry):
            j, _, dqs, nlrun, grun = carry
            dqs, nlrun, grun = tile(j, dqs, nlrun, grun, False)
            return j - 1, _sb_live(nlrun), dqs, nlrun, grun

        carry = lax.while_loop(lambda c: (c[0] >= 0) & c[1], step, (i - 1, _sb_live(nlrun), dqs, nlrun, grun))
        dq_ref[...] = (0.125 * jnp.concatenate(carry[2], axis=1)).astype(BF16)

        @pl.when(i == nq - 1)
        def _():
            cols = pl.ds(pl.multiple_of(g_idx * SB_LANES, SB_LANES), SB_LANES)

            def narrow(r, carry):
                rows = pl.ds(pl.multiple_of(r * SB_BK, SB_BK), SB_BK)
                dk16[rows, :] = dk_acc[rows, :].astype(BF16)
                dv16[rows, :] = dv_acc[rows, :].astype(BF16)
                return carry

            lax.fori_loop(0, s // SB_BK, narrow, 0)
            ck = pltpu.make_async_copy(dk16, dk_hbm.at[:, cols], sems.at[0])
            cv = pltpu.make_async_copy(dv16, dv_hbm.at[:, cols], sems.at[1])
            ck.start()
            cv.start()
            ck.wait()
            cv.wait()

    qblk = pl.BlockSpec((SB_BQ, SB_LANES), lambda g, i: (i, g))
    hbm = pl.BlockSpec(memory_space=pl.ANY)
    return pl.pallas_call(
        body, name=name, grid=(ng, nq),
        in_specs=[qblk, pl.BlockSpec((s, SB_LANES), lambda g, i: (0, ng + g)),
                  pl.BlockSpec((s, SB_LANES), lambda g, i: (0, 2 * ng + g)), qblk, qblk, *rd.in_specs],
        out_specs=[qblk, hbm, hbm, *rd.out_specs],
        out_shape=[jax.ShapeDtypeStruct((s, D), BF16)] * 3 + rd.out_shape,
        scratch_shapes=[pltpu.VMEM((s, SB_LANES), F32), pltpu.VMEM((s, SB_LANES), F32),
                        pltpu.VMEM((s, SB_LANES), BF16), pltpu.VMEM((s, SB_LANES), BF16),
                        pltpu.SemaphoreType.DMA((2,)),
                        pltpu.VMEM((1, SB_NCH, SB_BQ, SB_BK), F32), pltpu.VMEM((1, SB_NCH, SB_BQ, SB_BK), F32),
                        pltpu.VMEM((SB_NCH, SB_BQ, 2 * SB_BK), BF16), pltpu.VMEM((SB_NCH, SB_BQ, SB_BK), F32),
                        pltpu.VMEM((1, SB_NCH, SB_BQ, SB_BK), BF16), pltpu.VMEM((1, SB_NCH, SB_BQ, SB_BK), BF16),
                        *rd.scratch],
        compiler_params=_cp("arbitrary", "arbitrary"),
    )(qkv, qkv, qkv, o, do, *rides)


def _ssd_core(z, xpre, dtr, state, dtb, alog, dsk, nw):
    L = SSD_CHUNK
    xa = _silu(xpre)
    pieces = _split(xa, 128)
    xs = jnp.concatenate(pieces[:8], axis=1)
    bm, cm = pieces[8:10], pieces[10:12]
    dt = _softplus(dtr + dtb)
    a = dt * (-jnp.exp(alog))
    tri = (_iota((L, L), 0) >= _iota((L, L), 1)).astype(F32)
    a_cs = _dot("nn", tri, a, 1, 3)
    xc = xs * dt
    tril = _iota((L, L), 0) >= _iota((L, L), 1)
    lane_a = _iota((L, 128), 1) < 64
    acs_p = _split(a_cs, 128)
    xc_p = _split(xc, 128)
    ys, new_states = [], []
    for g in range(2):
        cb = _dot("nt", cm[g], bm[g])
        for pp in range(4):
            pair = 4 * g + pp
            acs = acs_p[pair]
            acs_t = acs.T
            xcp = xc_p[pair]
            st = state[pair]
            heads = []
            for hh in range(2):
                col = _take_fn(1, 64 * hh)(acs)
                row = _take_fn(0, 64 * hh)(acs_t)
                seg = col - row
                lm = jnp.where(tril, jnp.exp(jnp.where(tril, seg, 0.0)), 0.0)
                heads.append(_dot("nn", cb * lm, xcp))
            ydiag = jnp.where(lane_a, heads[0], heads[1])
            last = _take_fn(0, L - 1)(acs)
            snew = _dot("tn", xcp * jnp.exp(last - acs), bm[g])
            new_states.append(st * jnp.exp(_take_fn(1, L - 1)(acs_t)) + snew)
            yoff = _dot("nt", cm[g], st) * jnp.exp(acs)
            ys.append(ydiag + yoff)
    y = jnp.concatenate(ys, axis=1) + xs * dsk
    yg = y * _silu(z)
    outs = []
    for v in _split(yg, 512):
        outs.append(v * lax.rsqrt(jnp.mean(v * v, axis=-1, keepdims=True) + EPS))
    return jnp.concatenate(outs, axis=1) * nw, tuple(new_states)


XBC = 1536


def _ssd_conv(ext_ref, cw, cb):
    acc = cb
    for k in range(4):
        acc = acc + cw[k:k + 1, :] * ext_ref[pl.ds(HALO - 3 + k, SSD_CHUNK), :]
    return acc


def _ssd_fwd(z, xbc, dtr, cw, cb, lanes, rides, *, name):
    s = z.shape[0]
    L = SSD_CHUNK
    nc = s // L
    rd = _Rides(rides, [True] * len(rides))

    def body(*refs):
        ins, (y_ref, st_ref), (state, ext), handles = rd.split(refs, 7, 2, 2)
        z_ref, x_ref, h_ref, dtr_ref, cw_ref, cb_ref, ln_ref = ins
        c = pl.program_id(0)
        rd.run(handles, c == 0, c == nc - 1)

        @pl.when(c == 0)
        def _():
            state[...] = jnp.zeros_like(state)

        ext[0:HALO, :] = jnp.where(c == 0, 0.0, h_ref[...])
        ext[HALO:, :] = x_ref[...]
        xpre = _ssd_conv(ext, cw_ref[...], cb_ref[...])
        st_ref[0] = state[...]
        st_in = tuple(state[p] for p in range(8))
        yn, st_out = _ssd_core(z_ref[...], xpre, dtr_ref[...], st_in,
                               ln_ref[0:1, :], ln_ref[1:2, :], ln_ref[2:3, :], ln_ref[3:4, :])
        y_ref[...] = yn.astype(BF16)
        for p in range(8):
            state[p] = st_out[p]

    return pl.pallas_call(
        body, name=name, grid=(nc,),
        in_specs=[pl.BlockSpec((L, D), lambda c: (c, 0)),
                  pl.BlockSpec((L, XBC), lambda c: (c, 0)),
                  pl.BlockSpec((HALO, XBC), lambda c: (jnp.maximum(c * (L // HALO) - 1, 0), 0)),
                  pl.BlockSpec((L, D), lambda c: (c, 0)),
                  pl.BlockSpec((4, XBC), lambda c: (0, 0)),
                  pl.BlockSpec((1, XBC), lambda c: (0, 0)),
                  pl.BlockSpec((8, D), lambda c: (0, 0)), *rd.in_specs],
        out_specs=[pl.BlockSpec((L, D), lambda c: (c, 0)),
                   pl.BlockSpec((1, 8, 128, 128), lambda c: (c, 0, 0, 0)), *rd.out_specs],
        out_shape=[jax.ShapeDtypeStruct((s, D), BF16), jax.ShapeDtypeStruct((nc, 8, 128, 128), F32),
                   *rd.out_shape],
        scratch_shapes=[pltpu.VMEM((8, 128, 128), F32), pltpu.VMEM((L + HALO, XBC), F32), *rd.scratch],
        compiler_params=_cp("arbitrary"),
    )(z, xbc, xbc, dtr, cw, cb, lanes, *rides)


def _ssd_bwd(z, xbc, dtr, states, dymix, cw, cb, lanes, *, name):
    s = z.shape[0]
    L = SSD_CHUNK
    nc = s // L

    def body(z_ref, x_ref, h_ref, dtr_ref, st_ref, dy_ref, cw_ref, cb_ref, ln_ref,
             dz_ref, dx_ref, ddt_ref, dln_ref, dcv_ref, dstate, ext, dext):
        i = pl.program_id(0)
        c = nc - 1 - i

        @pl.when(i == 0)
        def _():
            dstate[...] = jnp.zeros_like(dstate)
            dext[...] = jnp.zeros_like(dext)
            dln_ref[...] = jnp.zeros_like(dln_ref)
            dcv_ref[...] = jnp.zeros_like(dcv_ref)

        ext[0:HALO, :] = jnp.where(c == 0, 0.0, h_ref[...])
        ext[HALO:, :] = x_ref[...]
        cwv = cw_ref[...]
        xpre = _ssd_conv(ext, cwv, cb_ref[...])
        st_in = tuple(st_ref[0, p] for p in range(8))
        _, vjp = jax.vjp(_ssd_core, z_ref[...], xpre, dtr_ref[...], st_in,
                         ln_ref[0:1, :], ln_ref[1:2, :], ln_ref[2:3, :], ln_ref[3:4, :])
        dz, dxpre, ddtr, dst, d0, d1, d2, d3 = vjp((dy_ref[...], tuple(dstate[p] for p in range(8))))
        for p in range(8):
            dstate[p] = dst[p]
        dz_ref[...] = dz.astype(BF16)
        ddt_ref[...] = ddtr.astype(BF16)
        dln_ref[0:4, :] += jnp.concatenate([d0, d1, d2, d3], axis=0)
        dext[0:L, :] = dxpre
        dx = jnp.zeros((L, XBC), F32)
        rows = []
        for k in range(4):
            dx = dx + cwv[k:k + 1, :] * dext[pl.ds(3 - k, L), :]
            rows.append(jnp.sum(dxpre * ext[pl.ds(HALO - 3 + k, L), :], axis=0, keepdims=True))
        rows.append(jnp.sum(dxpre, axis=0, keepdims=True))
        dx_ref[...] = dx.astype(BF16)
        dcv_ref[0:5, :] += jnp.concatenate(rows, axis=0)
        dext[L:L + HALO, :] = dxpre[0:HALO, :]

    rev = lambda i: (nc - 1 - i, 0)
    return pl.pallas_call(
        body, name=name, grid=(nc,),
        in_specs=[pl.BlockSpec((L, D), rev),
                  pl.BlockSpec((L, XBC), rev),
                  pl.BlockSpec((HALO, XBC), lambda i: (jnp.maximum((nc - 1 - i) * (L // HALO) - 1, 0), 0)),
                  pl.BlockSpec((L, D), rev),
                  pl.BlockSpec((1, 8, 128, 128), lambda i: (nc - 1 - i, 0, 0, 0)),
                  pl.BlockSpec((L, D), rev),
                  pl.BlockSpec((4, XBC), lambda i: (0, 0)),
                  pl.BlockSpec((1, XBC), lambda i: (0, 0)),
                  pl.BlockSpec((8, D), lambda i: (0, 0))],
        out_specs=[pl.BlockSpec((L, D), rev), pl.BlockSpec((L, XBC), rev), pl.BlockSpec((L, D), rev),
                   pl.BlockSpec((8, D), lambda i: (0, 0)), pl.BlockSpec((8, XBC), lambda i: (0, 0))],
        out_shape=[jax.ShapeDtypeStruct((s, D), BF16), jax.ShapeDtypeStruct((s, XBC), BF16),
                   jax.ShapeDtypeStruct((s, D), BF16), jax.ShapeDtypeStruct((8, D), F32),
                   jax.ShapeDtypeStruct((8, XBC), F32)],
        scratch_shapes=[pltpu.VMEM((8, 128, 128), F32), pltpu.VMEM((L + HALO, XBC), F32),
                        pltpu.VMEM((L + HALO, XBC), F32)],
        compiler_params=_cp("arbitrary"),
    )(z, xbc, xbc, dtr, states, dymix, cw, cb, lanes)


def _mem_attn_math(q, k, v):
    outs = []
    for qh, kh, vh in zip(_split(q, 256), _split(k, 256), _split(v, 256)):
        sc = _dot("nt", qh, kh) * (1.0 / 16.0)
        e = jnp.exp(sc - lax.stop_gradient(jnp.max(sc, axis=-1, keepdims=True)))
        p = e / jnp.sum(e, axis=-1, keepdims=True)
        outs.append(_dot("nn", p, vh))
    return jnp.concatenate(outs, axis=1)


def _mem_attn_fwd(q, k, v, *, name):
    s, m = q.shape[0], k.shape[0]
    tm = _tile(s, 256, 8)

    def body(q_ref, k_ref, v_ref, o_ref):
        o_ref[...] = _mem_attn_math(q_ref[...].astype(F32), k_ref[...].astype(F32),
                                    v_ref[...].astype(F32)).astype(BF16)

    row = pl.BlockSpec((tm, D), lambda i: (i, 0))
    kv = pl.BlockSpec((m, D), lambda i: (0, 0))
    return pl.pallas_call(
        body, name=name, grid=(s // tm,), in_specs=[row, kv, kv], out_specs=row,
        out_shape=jax.ShapeDtypeStruct((s, D), BF16), compiler_params=_cp("parallel"),
    )(q, k, v)


def _mem_attn_bwd(q, k, v, do, *, name):
    s, m = q.shape[0], k.shape[0]
    tm = _tile(s, 256, 8)

    def body(q_ref, k_ref, v_ref, do_ref, dq_ref, dk_ref, dv_ref):
        @pl.when(pl.program_id(0) == 0)
        def _():
            dk_ref[...] = jnp.zeros_like(dk_ref)
            dv_ref[...] = jnp.zeros_like(dv_ref)

        _, vjp = jax.vjp(_mem_attn_math, q_ref[...].astype(F32), k_ref[...].astype(F32),
                         v_ref[...].astype(F32))
        dq, dk, dv = vjp(do_ref[...])
        dq_ref[...] = dq.astype(BF16)
        dk_ref[...] += dk
        dv_ref[...] += dv

    row = pl.BlockSpec((tm, D), lambda i: (i, 0))
    kv = pl.BlockSpec((m, D), lambda i: (0, 0))
    return pl.pallas_call(
        body, name=name, grid=(s // tm,), in_specs=[row, kv, kv, row], out_specs=[row, kv, kv],
        out_shape=[jax.ShapeDtypeStruct((s, D), BF16), jax.ShapeDtypeStruct((m, D), F32),
                   jax.ShapeDtypeStruct((m, D), F32)],
        compiler_params=_cp("arbitrary"),
    )(q, k, v, do)


DFF = 2816
FFN_TC = 1408
FFN_TM = 256


def _ffn_conv(ext_ref, cw, cb, tm):
    acc = cb
    for k in range(3):
        acc = acc + cw[k:k + 1, :] * ext_ref[pl.ds(HALO - 2 + k, tm), :]
    return acc


def _ffn_specs(s):
    tm, tc = FFN_TM, FFN_TC
    blk = pl.BlockSpec((tm, tc), lambda i, j: (i, j))
    halo = pl.BlockSpec((HALO, tc), lambda i, j: (jnp.maximum(i * (tm // HALO) - 1, 0), j))
    cw = pl.BlockSpec((3, tc), lambda i, j: (0, j))
    cb = pl.BlockSpec((1, tc), lambda i, j: (0, j))
    return tm, tc, blk, halo, cw, cb


def _glu_fwd(ug, uv, cwg, cwv, cbg, cbv, *, name):
    s = ug.shape[0]
    tm, tc, blk, halo, cw, cb = _ffn_specs(s)

    def body(g_ref, gh_ref, v_ref, vh_ref, cwg_ref, cwv_ref, cbg_ref, cbv_ref, f_ref, eg, ev):
        first = pl.program_id(0) == 0
        eg[0:HALO, :] = jnp.where(first, 0.0, gh_ref[...])
        eg[HALO:, :] = g_ref[...]
        ev[0:HALO, :] = jnp.where(first, 0.0, vh_ref[...])
        ev[HALO:, :] = v_ref[...]
        g = _ffn_conv(eg, cwg_ref[...], cbg_ref[...], tm)
        v = _ffn_conv(ev, cwv_ref[...], cbv_ref[...], tm)
        f_ref[...] = (_silu(g) * v).astype(BF16)

    return pl.pallas_call(
        body, name=name, grid=(s // tm, DFF // tc),
        in_specs=[blk, halo, blk, halo, cw, cw, cb, cb], out_specs=blk,
        out_shape=jax.ShapeDtypeStruct((s, DFF), BF16),
        scratch_shapes=[pltpu.VMEM((tm + HALO, tc), F32)] * 2,
        compiler_params=_cp("parallel", "parallel"),
    )(ug, ug, uv, uv, cwg, cwv, cbg, cbv)


def _ffn_bwd(ug, uv, df, cwg, cwv, cbg, cbv, *, name):
    s = ug.shape[0]
    tm, tc = FFN_TM, FFN_TC
    nb = s // tm
    rows_ext = tm + HALO

    def body(g_ref, gp_ref, gn_ref, v_ref, vp_ref, vn_ref, df_ref, dfn_ref, cwg_ref, cwv_ref, cbg_ref, cbv_ref,
             dxg_ref, dxv_ref, dcg_ref, dcv_ref, eg, ev, edg, edv):
        i = pl.program_id(1)
        first, last = i == 0, i == nb - 1

        @pl.when(first)
        def _():
            dcg_ref[...] = jnp.zeros_like(dcg_ref)
            dcv_ref[...] = jnp.zeros_like(dcv_ref)

        for e, prev, main, nxt in ((eg, gp_ref, g_ref, gn_ref), (ev, vp_ref, v_ref, vn_ref)):
            e[0:HALO, :] = jnp.where(first, 0.0, prev[...])
            e[HALO:HALO + tm, :] = main[...]
            e[HALO + tm:, :] = jnp.where(last, 0.0, nxt[...])
        dfe = jnp.concatenate([df_ref[...], jnp.where(last, 0.0, dfn_ref[...])], axis=0)
        cwgv, cwvv = cwg_ref[...], cwv_ref[...]
        ugs = [eg[pl.ds(HALO - 2 + k, rows_ext), :] for k in range(3)]
        uvs = [ev[pl.ds(HALO - 2 + k, rows_ext), :] for k in range(3)]
        g = cbg_ref[...] + cwgv[0:1, :] * ugs[0] + cwgv[1:2, :] * ugs[1] + cwgv[2:3, :] * ugs[2]
        v = cbv_ref[...] + cwvv[0:1, :] * uvs[0] + cwvv[1:2, :] * uvs[1] + cwvv[2:3, :] * uvs[2]
        sg = _sigmoid(g)
        edv[...] = dfe * g * sg
        edg[...] = dfe * v * sg * (1.0 + g * (1.0 - sg))
        for edu, us, cw, dx_ref, dc_ref in ((edg, ugs, cwgv, dxg_ref, dcg_ref), (edv, uvs, cwvv, dxv_ref, dcv_ref)):
            du = edu[0:tm, :]
            dx = cw[2:3, :] * du
            sums = []
            for k in range(3):
                if k < 2:
                    dx = dx + cw[k:k + 1, :] * edu[pl.ds(2 - k, tm), :]
                sums.append(jnp.sum(du * us[k][0:tm], axis=0, keepdims=True))
            sums.append(jnp.sum(du, axis=0, keepdims=True))
            dx_ref[...] = dx.astype(BF16)
            dc_ref[0:4, :] += jnp.concatenate(sums, axis=0)

    blk = pl.BlockSpec((tm, tc), lambda j, i: (i, j))
    nxt = pl.BlockSpec((HALO, tc), lambda j, i: (jnp.minimum((i + 1) * (tm // HALO), s // HALO - 1), j))
    prv = pl.BlockSpec((HALO, tc), lambda j, i: (jnp.maximum(i * (tm // HALO) - 1, 0), j))
    cw = pl.BlockSpec((3, tc), lambda j, i: (0, j))
    cb = pl.BlockSpec((1, tc), lambda j, i: (0, j))
    acc = pl.BlockSpec((8, tc), lambda j, i: (0, j))
    return pl.pallas_call(
        body, name=name, grid=(DFF // tc, nb),
        in_specs=[blk, prv, nxt, blk, prv, nxt, blk, nxt, cw, cw, cb, cb],
        out_specs=[blk, blk, acc, acc],
        out_shape=[jax.ShapeDtypeStruct((s, DFF), BF16)] * 2 + [jax.ShapeDtypeStruct((8, DFF), F32)] * 2,
        scratch_shapes=[pltpu.VMEM((tm + 2 * HALO, tc), F32)] * 2 + [pltpu.VMEM((rows_ext, tc), F32)] * 2,
        compiler_params=_cp("parallel", "arbitrary"),
    )(ug, ug, ug, uv, uv, uv, df, df, cwg, cwv, cbg, cbv)


MESH = pl.DeviceIdType.MESH


def _all_gather(arrs, *, name):
    n = len(arrs)

    def body(*refs):
        x_refs, out_refs = refs[:n], refs[n:2 * n]
        send_sems, recv_sems, local_sems = refs[2 * n:]
        x, y, c = lax.axis_index("x"), lax.axis_index("y"), lax.axis_index("c")
        me, sibling = (x, y, c), (x, y, 1 - c)
        chips = [(1 - x, y), (x, 1 - y), (1 - x, 1 - y)]

        def blk(a, dev):
            return out_refs[a].at[4 * dev[0] + 2 * dev[1] + dev[2]]

        def copy(a, k, block, to, src=None):
            return pltpu.make_async_remote_copy(
                src_ref=blk(a, block) if src is None else src, dst_ref=blk(a, block),
                send_sem=send_sems.at[7 * a + k], recv_sem=recv_sems.at[7 * a + k],
                device_id=to, device_id_type=MESH)

        started = []
        mine = []
        for a in range(n):
            cp = pltpu.make_async_copy(x_refs[a], blk(a, me), local_sems.at[a])
            cp.start()
            mine.append(cp)
            first = [copy(a, 0, me, sibling, src=x_refs[a])]
            first += [copy(a, 1 + j, me, (*chip, c), src=x_refs[a]) for j, chip in enumerate(chips)]
            for cp in first:
                cp.start()
            started += first
        for a in range(n):
            for j, chip in enumerate(chips):
                copy(a, 1 + j, (*chip, c), me).wait_recv()
                fwd = copy(a, 4 + j, (*chip, c), sibling)
                fwd.start()
                started.append(fwd)
        for a in range(n):
            copy(a, 0, sibling, me).wait_recv()
            for j, chip in enumerate(chips):
                copy(a, 4 + j, (*chip, 1 - c), me).wait_recv()
        for cp in started:
            cp.wait_send()
        for cp in mine:
            cp.wait()

    any_spec = pl.BlockSpec(memory_space=pl.ANY)
    return pl.pallas_call(
        body, name=name,
        in_specs=[any_spec] * n, out_specs=[any_spec] * n,
        out_shape=[jax.ShapeDtypeStruct((NDEV,) + a.shape, a.dtype) for a in arrs],
        scratch_shapes=[pltpu.SemaphoreType.DMA((7 * n,)), pltpu.SemaphoreType.DMA((7 * n,)),
                        pltpu.SemaphoreType.DMA((n,))],
    )(*arrs)


class _Direct:
    SEMS = (pltpu.SemaphoreType.DMA((7,)), pltpu.SemaphoreType.DMA((7,)), pltpu.SemaphoreType.DMA((1,)))

    def __init__(self, src_ref, recv_ref, sems, gather):
        x, y, c = lax.axis_index("x"), lax.axis_index("y"), lax.axis_index("c")
        me = 4 * x + 2 * y + c
        send_sems, recv_sems, local_sem = sems
        src = (lambda pid: src_ref) if gather else (lambda pid: src_ref.at[pid])
        self.mine = pltpu.make_async_copy(src(me), recv_ref.at[me], local_sem.at[0])
        self.copies = []
        for k in range(1, NDEV):
            px = 1 - x if k & 4 else x
            py = 1 - y if k & 2 else y
            pc = 1 - c if k & 1 else c
            self.copies.append(pltpu.make_async_remote_copy(
                src_ref=src(4 * px + 2 * py + pc), dst_ref=recv_ref.at[me],
                send_sem=send_sems.at[k - 1], recv_sem=recv_sems.at[k - 1],
                device_id=(px, py, pc), device_id_type=MESH))

    def start(self):
        self.mine.start()
        for cp in self.copies:
            cp.start()

    def wait(self):
        for cp in self.copies:
            cp.wait_recv()
        for cp in self.copies:
            cp.wait_send()
        self.mine.wait()


def _recv_shape(src, gather):
    return jax.ShapeDtypeStruct(((NDEV,) + src.shape) if gather else src.shape, src.dtype)


class _Rides:
    def __init__(self, rides, gathers):
        self.n = len(rides)
        self.gathers = list(gathers)
        any_spec = pl.BlockSpec(memory_space=pl.ANY)
        self.in_specs = [any_spec] * self.n
        self.out_specs = [any_spec] * self.n
        self.out_shape = [_recv_shape(a, g) for a, g in zip(rides, gathers)]
        self.scratch = list(_Direct.SEMS) * self.n

    def split(self, refs, n_in, n_out, n_scratch):
        n = self.n
        ins, refs = refs[:n_in], refs[n_in:]
        rides, refs = refs[:n], refs[n:]
        outs, refs = refs[:n_out], refs[n_out:]
        gots, refs = refs[:n], refs[n:]
        scratch, sems = refs[:n_scratch], refs[n_scratch:]
        return ins, outs, scratch, (rides, gots, sems)

    def run(self, handles, first, last):
        rides, gots, sems = handles

        def all_of():
            return [_Direct(rides[a], gots[a], sems[3 * a:3 * a + 3], self.gathers[a]) for a in range(self.n)]

        @pl.when(first)
        def _():
            for e in all_of():
                e.start()

        @pl.when(last)
        def _():
            for e in all_of():
                e.wait()


def _exchange(arrs, gathers, *, name):
    rd = _Rides(arrs, gathers)

    def body(*refs):
        _, _, _, handles = rd.split(refs, 0, 0, 0)
        rd.run(handles, True, True)

    return pl.pallas_call(
        body, name=name, in_specs=rd.in_specs, out_specs=rd.out_specs, out_shape=rd.out_shape,
        scratch_shapes=rd.scratch,
    )(*arrs)


def _adamw(parts, w, m, v, *, name):
    r, cols = w.shape
    tm = _tile(r, 256, PACK_ALIGN)
    c1 = 1.0 - ADAM_B1 ** ADAM_STEP
    c2 = 1.0 - ADAM_B2 ** ADAM_STEP

    def body(p_ref, w_ref, m_ref, v_ref, g_ref, d_ref, nm_ref, nv_ref):
        g = p_ref[0].astype(F32)
        for i in range(1, NDEV):
            g = g + p_ref[i].astype(F32)
        nm = ADAM_B1 * m_ref[...] + (1.0 - ADAM_B1) * g
        nv = ADAM_B2 * v_ref[...] + (1.0 - ADAM_B2) * (g * g)
        d_ref[...] = -ADAM_LR * ((nm / c1) / (jnp.sqrt(nv / c2) + ADAM_EPS) + ADAM_WD * w_ref[...])
        g_ref[...] = g
        nm_ref[...] = nm
        nv_ref[...] = nv

    row = pl.BlockSpec((tm, cols), lambda i: (i, 0))
    return pl.pallas_call(
        body, name=name, grid=(r // tm,),
        in_specs=[pl.BlockSpec((NDEV, tm, cols), lambda i: (0, i, 0)), row, row, row],
        out_specs=[row] * 4, out_shape=[jax.ShapeDtypeStruct((r, cols), F32)] * 4,
        compiler_params=_cp("parallel"),
    )(parts, w, m, v)


PACK_ALIGN = 16


def _part_rows(shape):
    n = -(-math.prod(shape) // D)
    return n + (-n) % PACK_ALIGN


def _rows(a):
    flat = a.reshape(-1)
    pad = _part_rows(a.shape) * D - flat.shape[0]
    if pad:
        flat = jnp.concatenate([flat, jnp.zeros((pad,), flat.dtype)])
    return flat.reshape(-1, D)


def _pack(parts, total_rows):
    if all(math.prod(p.shape) % (PACK_ALIGN * D) for p in parts):
        return _pack_small(parts, total_rows)
    rows = [_rows(p) for p in parts]
    used = sum(r.shape[0] for r in rows)
    if total_rows > used:
        rows.append(jnp.zeros((total_rows - used, D), rows[0].dtype))
    return jnp.concatenate(rows, axis=0)


def _pack_small(parts, total_rows):
    flat, used = [], 0
    for p in parts:
        n, nr = math.prod(p.shape), _part_rows(p.shape)
        flat += [p.reshape(-1), jnp.zeros((nr * D - n,), p.dtype)]
        used += nr
    flat.append(jnp.zeros(((total_rows - used) * D,), parts[0].dtype))
    return jnp.concatenate(flat).reshape(total_rows, D)


def _unpack(buf, shapes):
    out, r0 = [], 0
    for shp in shapes:
        n = math.prod(shp)
        out.append(buf[r0:r0 + _part_rows(shp)].reshape(-1)[:n].reshape(shp))
        r0 += _part_rows(shp)
    return out


SHARD = {"w_in": (D, 706), "w_out": (256, D), "w_mq": (128, D), "w_mk": (128, D), "w_mv": (128, D),
         "w_mo": (128, D), "w_up": (D, 704), "w_down": (352, D), "conv_ssd_w": (4, 192), "conv_ffn_w": (3, 704)}
GATHER_MID = ["w_out", "w_mq", "w_mk", "w_mv", "w_mo"]
GATHER_FFN = ["w_down"]
CONV_TAPS = ["conv_ssd_w", "conv_ffn_w"]
GRADS_PACKED = ["w_out", "w_mq", "w_mk", "w_mv", "w_mo", "w_down", "conv_ffn_w", "conv_ssd_w"]


def _layout(names):
    row0, r = {}, 0
    for n in names:
        row0[n] = r
        r += _part_rows(SHARD[n])
    return row0, r + (-r) % 128


SMALL = [("norm_mix_w", (1, D)), ("conv_ssd_b", (1, 1536)), ("dt_bias", (1, 16)), ("a_log", (1, 16)),
         ("d_skip", (1, 16)), ("ssd_norm_w", (1, D)), ("sb_norm_w", (1, D)), ("norm_mem_w", (1, D)),
         ("norm_memkv_w", (1, D)), ("norm_ffn_w", (1, D)), ("conv_ffn_b", (1, 5632)), ("norm_final_w", (D,))]
LOSS_ROW = sum(_part_rows(_shp) for _, _shp in SMALL)
SMALL_ROWS = LOSS_ROW + PACK_ALIGN
ORDER = ["norm_mix_w", "w_in", "conv_ssd_w", "conv_ssd_b", "dt_bias", "a_log", "d_skip", "ssd_norm_w",
         "sb_norm_w", "w_out", "norm_mem_w", "norm_memkv_w", "w_mq", "w_mk", "w_mv", "w_mo", "norm_ffn_w",
         "w_up", "conv_ffn_w", "conv_ffn_b", "w_down", "norm_final_w"]


def _pad_rows(a, nr):
    n = a.shape[1]
    return jnp.concatenate([a, jnp.zeros((NDEV, nr * D - n), a.dtype)], axis=1).reshape(NDEV, nr, D)


def _group_sum(lanes):
    return lanes.reshape(16, 64).sum(axis=1).reshape(1, 16)


def kernel(x, mem, norm_mix_w, w_in, conv_ssd_w, conv_ssd_b, dt_bias, a_log, d_skip, ssd_norm_w, sb_norm_w, w_out, norm_mem_w, norm_memkv_w, w_mq, w_mk, w_mv, w_mo, norm_ffn_w, w_up, conv_ffn_w, conv_ffn_b, w_down, norm_final_w, loss_target, m_norm_mix_w, m_w_in, m_conv_ssd_w, m_conv_ssd_b, m_dt_bias, m_a_log, m_d_skip, m_ssd_norm_w, m_sb_norm_w, m_w_out, m_norm_mem_w, m_norm_memkv_w, m_w_mq, m_w_mk, m_w_mv, m_w_mo, m_norm_ffn_w, m_w_up, m_conv_ffn_w, m_conv_ffn_b, m_w_down, m_norm_final_w, v_norm_mix_w, v_w_in, v_conv_ssd_w, v_conv_ssd_b, v_dt_bias, v_a_log, v_d_skip, v_ssd_norm_w, v_sb_norm_w, v_w_out, v_norm_mem_w, v_norm_memkv_w, v_w_mq, v_w_mk, v_w_mv, v_w_mo, v_norm_ffn_w, v_w_up, v_conv_ffn_w, v_conv_ffn_b, v_w_down, v_norm_final_w):
    P = dict(norm_mix_w=norm_mix_w, w_in=w_in, conv_ssd_w=conv_ssd_w, conv_ssd_b=conv_ssd_b, dt_bias=dt_bias, a_log=a_log, d_skip=d_skip, ssd_norm_w=ssd_norm_w, sb_norm_w=sb_norm_w, w_out=w_out, norm_mem_w=norm_mem_w, norm_memkv_w=norm_memkv_w, w_mq=w_mq, w_mk=w_mk, w_mv=w_mv, w_mo=w_mo, norm_ffn_w=norm_ffn_w, w_up=w_up, conv_ffn_w=conv_ffn_w, conv_ffn_b=conv_ffn_b, w_down=w_down, norm_final_w=norm_final_w)
    M = dict(norm_mix_w=m_norm_mix_w, w_in=m_w_in, conv_ssd_w=m_conv_ssd_w, conv_ssd_b=m_conv_ssd_b, dt_bias=m_dt_bias, a_log=m_a_log, d_skip=m_d_skip, ssd_norm_w=m_ssd_norm_w, sb_norm_w=m_sb_norm_w, w_out=m_w_out, norm_mem_w=m_norm_mem_w, norm_memkv_w=m_norm_memkv_w, w_mq=m_w_mq, w_mk=m_w_mk, w_mv=m_w_mv, w_mo=m_w_mo, norm_ffn_w=m_norm_ffn_w, w_up=m_w_up, conv_ffn_w=m_conv_ffn_w, conv_ffn_b=m_conv_ffn_b, w_down=m_w_down, norm_final_w=m_norm_final_w)
    V = dict(norm_mix_w=v_norm_mix_w, w_in=v_w_in, conv_ssd_w=v_conv_ssd_w, conv_ssd_b=v_conv_ssd_b, dt_bias=v_dt_bias, a_log=v_a_log, d_skip=v_d_skip, ssd_norm_w=v_ssd_norm_w, sb_norm_w=v_sb_norm_w, w_out=v_w_out, norm_mem_w=v_norm_mem_w, norm_memkv_w=v_norm_memkv_w, w_mq=v_w_mq, w_mk=v_w_mk, w_mv=v_w_mv, w_mo=v_w_mo, norm_ffn_w=v_norm_ffn_w, w_up=v_w_up, conv_ffn_w=v_conv_ffn_w, conv_ffn_b=v_conv_ffn_b, w_down=v_w_down, norm_final_w=v_norm_final_w)
    small_shapes = [shp for _, shp in SMALL]

    def packed(src, names, dtype=F32):
        return _pack([src[n][0] for n in names], _layout(names)[1]).astype(dtype)

    def columns(g):
        return g.transpose(1, 0, 2).reshape(g.shape[1], NDEV * g.shape[2])

    g_in, g_taps = _all_gather([w_in[0].astype(BF16), packed(P, CONV_TAPS)], name="gather_w_in")
    W_in = columns(g_in)
    cw_ssd = g_taps[:, 0].reshape(NDEV, -1)[:, :768].reshape(NDEV, 4, 192).transpose(1, 0, 2).reshape(4, XBC)
    cw_ffn = (g_taps[:, PACK_ALIGN:PACK_ALIGN + 3].reshape(NDEV, -1)[:, :2112].reshape(NDEV, 3, 704)
              .transpose(1, 0, 2).reshape(3, 2 * DFF))
    W_z, W_xbc, W_dt, W_qkv = W_in[:, :D], W_in[:, D:D + XBC], W_in[:, D + XBC:D + XBC + 16], W_in[:, D + XBC + 16:]
    W_dtr = jnp.repeat(W_dt, 64, axis=1)
    cwg, cwv = cw_ffn[:, :DFF], cw_ffn[:, DFF:]
    cbg, cbv = conv_ffn_b[:, :DFF], conv_ffn_b[:, DFF:]
    rep = lambda p: jnp.repeat(p, 64, axis=1)
    lanes = jnp.concatenate([rep(dt_bias), rep(a_log), rep(d_skip), ssd_norm_w, jnp.zeros((4, D), F32)], axis=0)

    xs, tgt, mm = x[0], loss_target[0], mem[0]

    h1 = _norm_fwd(xs, norm_mix_w, name="norm_mix")
    z = _mm(h1, W_z, name="proj_z")
    xbc = _mm(h1, W_xbc, name="proj_xbc")
    dtr = _mm(h1, W_dtr, name="proj_dt")
    qkv = _mm(h1, W_qkv, name="proj_qkv", out_dtype=BF16)
    y_ssd, states, g_up, g_ffn = _ssd_fwd(z, xbc, dtr, cw_ssd, conv_ssd_b, lanes,
                                          [w_up[0].astype(BF16), packed(P, GATHER_FFN, BF16)], name="ssd_fwd")
    o_sb, g_mid = _sb_fwd(qkv, [packed(P, GATHER_MID, BF16)], name="sb_fwd")
    r_mid = _layout(GATHER_MID)[0]
    W_out = g_mid[:, r_mid["w_out"]:r_mid["w_out"] + 256].reshape(2 * D, D)
    W_mq, W_mk, W_mv, W_mo = [g_mid[:, r_mid[n]:r_mid[n] + 128].reshape(D, D)
                              for n in ("w_mq", "w_mk", "w_mv", "w_mo")]
    W_up = columns(g_up)
    W_down = g_ffn[:, 0:352].reshape(DFF, D)
    W_upg, W_upv = W_up[:, :DFF], W_up[:, DFF:]
    y_sb = _head_norm_fwd(o_sb, sb_norm_w, name="sb_norm")
    ymix = jnp.concatenate([y_ssd, y_sb], axis=1)
    x1 = _mm(ymix, W_out, add=xs, name="proj_out")
    h2 = _norm_fwd(x1, norm_mem_w, name="norm_mem")
    mn = _norm_fwd(mm, norm_memkv_w, name="norm_memkv")
    qm = _mm(h2, W_mq, name="mem_q", out_dtype=BF16)
    km = _mm(mn, W_mk, name="mem_k", out_dtype=BF16)
    vm = _mm(mn, W_mv, name="mem_v", out_dtype=BF16)
    om = _mem_attn_fwd(qm, km, vm, name="mem_attn")
    x2 = _mm(om, W_mo, add=x1, name="mem_o")
    h3 = _norm_fwd(x2, norm_ffn_w, name="norm_ffn")
    ug = _mm(h3, W_upg, name="ffn_up_g")
    uv = _mm(h3, W_upv, name="ffn_up_v")
    f = _glu_fwd(ug, uv, cwg, cwv, cbg, cbv, name="ffn_glu")
    x3 = _mm(f, W_down, add=x2, name="ffn_down")
    dx3, g_nfinal, loss_part = _final(x3, norm_final_w.reshape(1, D), tgt, name="final_loss")

    G = {}
    G["w_down"] = _mm(f, dx3, trans_a=True, name="g_w_down")
    df = _mm(dx3, W_down, trans_b=True, name="d_f")
    dupg, dupv, dcg, dcv = _ffn_bwd(ug, uv, df, cwg, cwv, cbg, cbv, name="ffn_glu_bwd")
    G["w_up"] = jnp.concatenate([_mm(h3, dupg, trans_a=True, name="g_w_up_g"),
                                 _mm(h3, dupv, trans_a=True, name="g_w_up_v")], axis=1)
    G["conv_ffn_w"] = jnp.concatenate([dcg[0:3], dcv[0:3]], axis=1)
    G["conv_ffn_b"] = jnp.concatenate([dcg[3:4], dcv[3:4]], axis=1)
    dh3 = _mm(dupg, W_upg, trans_b=True, name="d_h3_g")
    dh3 = _mm(dupv, W_upv, trans_b=True, add=dh3, name="d_h3_v")
    dx2, G["norm_ffn_w"] = _norm_bwd(x2, norm_ffn_w, dh3, dx3, name="norm_ffn_bwd")
    G["w_mo"] = _mm(om, dx2, trans_a=True, name="g_w_mo")
    dom = _mm(dx2, W_mo, trans_b=True, name="d_om")
    dqm, dkm, dvm = _mem_attn_bwd(qm, km, vm, dom, name="mem_attn_bwd")
    G["w_mq"] = _mm(h2, dqm, trans_a=True, name="g_w_mq")
    G["w_mk"] = _mm(mn, dkm, trans_a=True, name="g_w_mk")
    G["w_mv"] = _mm(mn, dvm, trans_a=True, name="g_w_mv")
    dh2 = _mm(dqm, W_mq, trans_b=True, name="d_h2")
    dmn = _mm(dkm, W_mk, trans_b=True, name="d_mn_k")
    dmn = _mm(dvm, W_mv, trans_b=True, add=dmn, name="d_mn_v")
    _, G["norm_memkv_w"] = _norm_bwd(mm, norm_memkv_w, dmn, None, name="norm_memkv_bwd")
    dx1, G["norm_mem_w"] = _norm_bwd(x1, norm_mem_w, dh2, dx2, name="norm_mem_bwd")
    G["w_out"] = _mm(ymix, dx1, trans_a=True, name="g_w_out")
    dymix = _mm(dx1, W_out, trans_b=True, name="d_ymix")
    do_sb, G["sb_norm_w"] = _head_norm_bwd(o_sb, sb_norm_w, dymix, name="sb_norm_bwd")

    dz, dxbc, ddtr, dlanes, dconv = _ssd_bwd(z, xbc, dtr, states, dymix, cw_ssd, conv_ssd_b, lanes, name="ssd_bwd")
    G["dt_bias"], G["a_log"], G["d_skip"] = [_group_sum(dlanes[i:i + 1]) for i in range(3)]
    G["ssd_norm_w"] = dlanes[3:4]
    G["conv_ssd_w"], G["conv_ssd_b"] = dconv[0:4], dconv[4:5]

    def col_slabs(g, cols):
        return g.reshape(g.shape[0], NDEV, cols).transpose(1, 0, 2).astype(BF16)

    def packed_slabs(names):
        parts = []
        for n in names:
            shp = SHARD[n]
            if shp[-1] == D:
                t = G[n].reshape((NDEV,) + shp)
            else:
                t = _pad_rows(G[n].reshape(shp[0], NDEV, shp[1]).transpose(1, 0, 2).reshape(NDEV, -1),
                              _part_rows(shp))
            parts.append(jnp.pad(t, ((0, 0), (0, _part_rows(shp) - t.shape[1]), (0, 0))))
        used = sum(t.shape[1] for t in parts)
        parts.append(jnp.zeros((NDEV, _layout(names)[1] - used, D), F32))
        return jnp.concatenate(parts, axis=1).astype(BF16)

    dq, dk, dv, recv_packed, recv_up = _sb_bwd(qkv, o_sb, do_sb, [packed_slabs(GRADS_PACKED), col_slabs(G["w_up"], 704)],
                                               name="sb_bwd")
    dqkv = jnp.concatenate([dq, dk, dv], axis=1)
    g_wdt = _mm(h1, ddtr, trans_a=True, name="g_w_dt").reshape(D, 16, 64).sum(axis=2)
    G["w_in"] = jnp.concatenate([_mm(h1, dz, trans_a=True, name="g_w_z"),
                                 _mm(h1, dxbc, trans_a=True, name="g_w_xbc"), g_wdt,
                                 _mm(h1, dqkv, trans_a=True, name="g_w_qkv")], axis=1)
    dh1 = _mm(dz, W_z, trans_b=True, name="d_h1_z")
    dh1 = _mm(dxbc, W_xbc, trans_b=True, add=dh1, name="d_h1_xbc")
    dh1 = _mm(ddtr, W_dtr, trans_b=True, add=dh1, name="d_h1_dt")
    dh1, recv_in = _mm(dqkv, W_qkv, trans_b=True, add=dh1, rides=[col_slabs(G["w_in"], 706)], name="d_h1_qkv")
    dx, G["norm_mix_w"] = _norm_bwd(xs, norm_mix_w, dh1, dx1, name="norm_mix_bwd")
    G["norm_final_w"] = g_nfinal.reshape(D)

    small_g = _pack([G[n] for n, _ in SMALL] + [loss_part], SMALL_ROWS)
    (parts_small,) = _exchange([small_g], [True], name="exchange_grads")
    outs_packed = _adamw(recv_packed, packed(P, GRADS_PACKED), packed(M, GRADS_PACKED), packed(V, GRADS_PACKED),
                         name="adamw_packed")
    outs_up = _adamw(recv_up, w_up[0], m_w_up[0], v_w_up[0], name="adamw_w_up")
    outs_in = _adamw(recv_in, w_in[0], m_w_in[0], v_w_in[0], name="adamw_w_in")
    outs_small = _adamw(parts_small, _pack([P[n] for n, _ in SMALL], SMALL_ROWS),
                        _pack([M[n] for n, _ in SMALL], SMALL_ROWS),
                        _pack([V[n] for n, _ in SMALL], SMALL_ROWS), name="adamw_replicated")

    res = {}
    for i, kind in enumerate(("grad", "delta", "new_m", "new_v")):
        for n, val in zip(GRADS_PACKED, _unpack(outs_packed[i], [SHARD[n] for n in GRADS_PACKED])):
            res[kind, n] = val.reshape((1,) + SHARD[n])
        res[kind, "w_up"] = outs_up[i].reshape((1,) + SHARD["w_up"])
        res[kind, "w_in"] = outs_in[i].reshape((1,) + SHARD["w_in"])
        for (n, shp), val in zip(SMALL, _unpack(outs_small[i], small_shapes)):
            res[kind, n] = val
    loss = outs_small[0][LOSS_ROW, 0]
    out = [loss, dx.reshape(1, -1, D)]
    for kind in ("grad", "delta", "new_m", "new_v"):
        out += [res[kind, n] for n in ORDER]
    return tuple(out)
```

```python
import functools
import math

import jax
import jax.numpy as jnp
from jax import lax
from jax.experimental import pallas as pl
from jax.experimental.pallas import tpu as pltpu

F32 = jnp.float32
BF16 = jnp.bfloat16

D = 1024
NDEV = 8
EPS = 1e-6
SSD_CHUNK = 128
HALO = 8
VMEM_LIMIT = 56 * 2**20

ADAM_LR, ADAM_B1, ADAM_B2, ADAM_EPS, ADAM_WD, ADAM_STEP = 0.001, 0.9, 0.999, 1e-08, 0.01, 10


def _cp(*sem):
    return pltpu.CompilerParams(dimension_semantics=sem, vmem_limit_bytes=VMEM_LIMIT)


def _tile(n, cap, mult):
    if n <= cap:
        return n
    for d in range(cap - cap % mult, 0, -mult):
        if n % d == 0:
            return d
    raise ValueError(f"no tile for {n}")


def _sigmoid(x):
    return 1.0 / (1.0 + jnp.exp(-x))


def _silu(x):
    return x * _sigmoid(x)


def _softplus(x):
    return jnp.maximum(x, 0.0) + jnp.log1p(jnp.exp(-jnp.abs(x)))


def _terms(x, n):
    out = []
    r = x.astype(F32)
    for i in range(n):
        h = r.astype(BF16)
        out.append(h)
        if i + 1 < n:
            r = r - h.astype(F32)
    return out


_DIMS = {"nn": ((1,), (0,)), "nt": ((1,), (1,)), "tn": ((0,), (0,))}


def _dot_raw(form, a, b, ta, tb):
    acc = None
    for ai in _terms(a, ta):
        for bi in _terms(b, tb):
            d = lax.dot_general(ai, bi, (_DIMS[form], ((), ())), preferred_element_type=F32)
            acc = d if acc is None else acc + d
    return acc


@functools.lru_cache(maxsize=None)
def _dot_fn(form, ta, tb):
    @jax.custom_vjp
    def f(a, b):
        return _dot_raw(form, a, b, ta, tb)

    def fwd(a, b):
        return f(a, b), (a, b)

    def bwd(res, ct):
        a, b = res
        if form == "nn":
            return _dot_fn("nt", ta, tb)(ct, b), _dot_fn("tn", ta, tb)(a, ct)
        if form == "nt":
            return _dot_fn("nn", ta, tb)(ct, b), _dot_fn("tn", tb, ta)(ct, a)
        return _dot_fn("nt", tb, ta)(b, ct), _dot_fn("nn", ta, tb)(a, ct)

    f.defvjp(fwd, bwd)
    return f


def _dot(form, a, b, ta=1, tb=1):
    return _dot_fn(form, ta, tb)(a, b)


@functools.lru_cache(maxsize=None)
def _take_fn(axis, idx):
    @jax.custom_vjp
    def f(x):
        return x[:, idx:idx + 1] if axis == 1 else x[idx:idx + 1, :]

    def fwd(x):
        return f(x), x.shape

    def bwd(shape, ct):
        io = lax.broadcasted_iota(jnp.int32, shape, axis)
        return (jnp.where(io == idx, jnp.broadcast_to(ct, shape), 0.0),)

    f.defvjp(fwd, bwd)
    return f


@functools.lru_cache(maxsize=None)
def _split_fn(width, n):
    @jax.custom_vjp
    def f(x):
        return tuple(x[:, i * width:(i + 1) * width] for i in range(n))

    def fwd(x):
        return f(x), None

    def bwd(_, cts):
        return (jnp.concatenate(list(cts), axis=1),)

    f.defvjp(fwd, bwd)
    return f


def _split(x, width):
    return _split_fn(width, x.shape[1] // width)(x)


def _iota(shape, axis):
    return lax.broadcasted_iota(jnp.int32, shape, axis)


MM_VMEM_BUDGET = 36 * 2**20


def _mm_tiles(m, n, kt, trans_a, a_bytes, b_bytes, out_bytes, add_bytes):
    tn = _tile(n, 1536, 128)
    for tm_cap in (1408, 1024, 512, 256, 128):
        tm = _tile(m, tm_cap, 128 if trans_a else 8)
        for tk_cap in (kt, 4096, 2048, 1024, 512):
            tk = _tile(kt, tk_cap, 128)
            blocks = tm * tk * a_bytes + tk * tn * b_bytes + tm * tn * (out_bytes + add_bytes)
            if 2 * blocks + (tm * tn * 4 if tk < kt else 0) <= MM_VMEM_BUDGET:
                return tm, tn, tk
    raise ValueError(f"no matmul tiling for {(m, n, kt)}")


def _mm(a, b, *, name, add=None, trans_a=False, trans_b=False, out_dtype=F32, rides=()):
    assert not (trans_a and trans_b)
    if trans_a:
        kt, m = a.shape
    else:
        m, kt = a.shape
    n, kt2 = b.shape if trans_b else b.shape[::-1]
    assert kt == kt2, (a.shape, b.shape)
    tm, tn, tk = _mm_tiles(m, n, kt, trans_a, a.dtype.itemsize, b.dtype.itemsize,
                           jnp.dtype(out_dtype).itemsize, 0 if add is None else add.dtype.itemsize)
    nk = kt // tk
    grid = (m // tm, n // tn, nk)
    rd = _Rides(rides, [False] * len(rides))
    n_in = 2 if add is None else 3

    def body(*all_refs):
        ins, (o_ref,), scratch, handles = rd.split(all_refs, n_in, 1, 1 if nk > 1 else 0)
        refs = (*ins, o_ref, *scratch)
        if rides:
            ids = [pl.program_id(ax) for ax in range(3)]
            rd.run(handles, (ids[0] == 0) & (ids[1] == 0) & (ids[2] == 0),
                   (ids[0] == grid[0] - 1) & (ids[1] == grid[1] - 1) & (ids[2] == grid[2] - 1))
        if add is None:
            a_ref, b_ref, o_ref = refs[:3]
        else:
            a_ref, b_ref, add_ref, o_ref = refs[:4]
        k = pl.program_id(2)
        av = a_ref[...].astype(BF16)
        bv = b_ref[...].astype(BF16)
        dims = _DIMS["tn" if trans_a else "nt" if trans_b else "nn"]
        d = lax.dot_general(av, bv, (dims, ((), ())), preferred_element_type=F32)

        def finish(r):
            if add is not None:
                r = r + add_ref[...]
            o_ref[...] = r.astype(out_dtype)

        if nk == 1:
            finish(d)
        else:
            acc = refs[-1]

            @pl.when(k == 0)
            def _():
                acc[...] = d

            @pl.when((k > 0) & (k < nk - 1))
            def _():
                acc[...] += d

            @pl.when(k == nk - 1)
            def _():
                finish(acc[...] + d)

    a_spec = (pl.BlockSpec((tk, tm), lambda i, j, k: (k, i)) if trans_a
              else pl.BlockSpec((tm, tk), lambda i, j, k: (i, k)))
    b_spec = (pl.BlockSpec((tn, tk), lambda i, j, k: (j, k)) if trans_b
              else pl.BlockSpec((tk, tn), lambda i, j, k: (k, j)))
    in_specs = [a_spec, b_spec]
    args = [a, b]
    if add is not None:
        in_specs.append(pl.BlockSpec((tm, tn), lambda i, j, k: (i, j)))
        args.append(add)
    out = pl.pallas_call(
        body, name=name, grid=grid,
        in_specs=in_specs + rd.in_specs,
        out_specs=[pl.BlockSpec((tm, tn), lambda i, j, k: (i, j))] + rd.out_specs,
        out_shape=[jax.ShapeDtypeStruct((m, n), out_dtype)] + rd.out_shape,
        scratch_shapes=([pltpu.VMEM((tm, tn), F32)] if nk > 1 else []) + rd.scratch,
        compiler_params=_cp(*(("arbitrary",) * 3 if rides else ("parallel", "parallel", "arbitrary"))),
    )(*args, *rides)
    return out if rides else out[0]


def _rstd(x):
    return lax.rsqrt(jnp.mean(x * x, axis=-1, keepdims=True) + EPS)


def _norm_fwd(x, w, *, name):
    s = x.shape[0]
    tm = _tile(s, 512, 8)

    def body(x_ref, w_ref, o_ref):
        xv = x_ref[...]
        o_ref[...] = (xv * _rstd(xv) * w_ref[...]).astype(BF16)

    return pl.pallas_call(
        body, name=name, grid=(s // tm,),
        in_specs=[pl.BlockSpec((tm, D), lambda i: (i, 0)), pl.BlockSpec((1, D), lambda i: (0, 0))],
        out_specs=pl.BlockSpec((tm, D), lambda i: (i, 0)),
        out_shape=jax.ShapeDtypeStruct((s, D), BF16), compiler_params=_cp("parallel"),
    )(x, w)


def _norm_bwd_math(xv, wv, dy):
    r = _rstd(xv)
    xh = xv * r
    dxh = dy * wv
    dx = r * (dxh - xh * jnp.mean(dxh * xh, axis=-1, keepdims=True))
    dw = jnp.sum(dy * xh, axis=0, keepdims=True)
    return dx, dw


def _norm_bwd(x, w, dy, add, *, name):
    s = x.shape[0]
    tm = _tile(s, 256, 8)

    def body(*refs):
        if add is None:
            x_ref, w_ref, dy_ref, dx_ref, dw_ref = refs
        else:
            x_ref, w_ref, dy_ref, add_ref, dx_ref, dw_ref = refs

        @pl.when(pl.program_id(0) == 0)
        def _():
            dw_ref[...] = jnp.zeros_like(dw_ref)

        dx, dw = _norm_bwd_math(x_ref[...], w_ref[...], dy_ref[...])
        if add is not None:
            dx = dx + add_ref[...]
        dx_ref[...] = dx
        dw_ref[...] += dw

    row = pl.BlockSpec((tm, D), lambda i: (i, 0))
    vec = pl.BlockSpec((1, D), lambda i: (0, 0))
    in_specs = [row, vec, row] + ([row] if add is not None else [])
    args = [x, w, dy] + ([add] if add is not None else [])
    return pl.pallas_call(
        body, name=name, grid=(s // tm,), in_specs=in_specs, out_specs=[row, vec],
        out_shape=[jax.ShapeDtypeStruct((s, D), F32), jax.ShapeDtypeStruct((1, D), F32)],
        compiler_params=_cp("arbitrary"),
    )(*args)


def _final(x3, w, target, *, name):
    s = x3.shape[0]
    tm = _tile(s, 256, 8)

    def body(x_ref, w_ref, t_ref, dx_ref, dw_ref, loss_ref):
        @pl.when(pl.program_id(0) == 0)
        def _():
            dw_ref[...] = jnp.zeros_like(dw_ref)
            loss_ref[...] = jnp.zeros_like(loss_ref)

        xv = x_ref[...]
        wv = w_ref[...]
        y = xv * _rstd(xv) * wv
        err = y - t_ref[...]
        loss_ref[...] += 0.5 * jnp.sum(jnp.mean(err * err, axis=-1, keepdims=True))
        dx, dw = _norm_bwd_math(xv, wv, err * (1.0 / D))
        dx_ref[...] = dx
        dw_ref[...] += dw

    row = pl.BlockSpec((tm, D), lambda i: (i, 0))
    vec = pl.BlockSpec((1, D), lambda i: (0, 0))
    return pl.pallas_call(
        body, name=name, grid=(s // tm,), in_specs=[row, vec, row], out_specs=[row, vec, vec],
        out_shape=[jax.ShapeDtypeStruct((s, D), F32), jax.ShapeDtypeStruct((1, D), F32),
                   jax.ShapeDtypeStruct((1, D), F32)],
        compiler_params=_cp("arbitrary"),
    )(x3, w, target)


def _head_norm_math(o, w):
    lane = _iota((128, 128), 0) // 64
    bd = (lane == _iota((128, 128), 1) // 64).astype(F32)
    outs = []
    for op in _split(o, 128):
        ms = _dot("nn", op * op, bd, 2, 1) * (1.0 / 64)
        outs.append(op * lax.rsqrt(ms + EPS))
    return jnp.concatenate(outs, axis=1) * w


def _head_norm_fwd(o, w, *, name):
    s = o.shape[0]
    tm = _tile(s, 256, 8)

    def body(o_ref, w_ref, y_ref):
        y_ref[...] = _head_norm_math(o_ref[...], w_ref[...]).astype(BF16)

    row = pl.BlockSpec((tm, D), lambda i: (i, 0))
    vec = pl.BlockSpec((1, D), lambda i: (0, 0))
    return pl.pallas_call(
        body, name=name, grid=(s // tm,), in_specs=[row, vec], out_specs=row,
        out_shape=jax.ShapeDtypeStruct((s, D), BF16), compiler_params=_cp("parallel"),
    )(o, w)


def _head_norm_bwd(o, w, dymix, *, name):
    s = o.shape[0]
    tm = _tile(s, 256, 8)

    def body(o_ref, w_ref, dy_ref, do_ref, dw_ref):
        @pl.when(pl.program_id(0) == 0)
        def _():
            dw_ref[...] = jnp.zeros_like(dw_ref)

        _, vjp = jax.vjp(_head_norm_math, o_ref[...], w_ref[...])
        do, dw = vjp(dy_ref[...])
        do_ref[...] = do
        dw_ref[...] += dw

    row = pl.BlockSpec((tm, D), lambda i: (i, 0))
    vec = pl.BlockSpec((1, D), lambda i: (0, 0))
    return pl.pallas_call(
        body, name=name, grid=(s // tm,),
        in_specs=[row, vec, pl.BlockSpec((tm, D), lambda i: (i, 1))], out_specs=[row, vec],
        out_shape=[jax.ShapeDtypeStruct((s, D), F32), jax.ShapeDtypeStruct((1, D), F32)],
        compiler_params=_cp("arbitrary"),
    )(o, w, dymix)


SB_BQ = 256
SB_BK = 256


def _sb_consts():
    r = _iota((SB_BK, SB_BK), 0)
    c = _iota((SB_BK, SB_BK), 1)
    u_excl = (r > c).astype(BF16)
    u_incl = (r >= c).astype(BF16)
    return u_excl, u_incl


SB_LANES = 256
SB_NCH = SB_LANES // 64


def _nt(a, b):
    return lax.dot_general(a, b, (_DIMS["nt"], ((), ())), preferred_element_type=F32)


def _tn(a, b):
    return lax.dot_general(a, b, (_DIMS["tn"], ((), ())), preferred_element_type=F32)


def _nn(a, b):
    return jnp.dot(a, b, preferred_element_type=F32)


def _sb_heads(ref):
    out = []
    for hp in range(SB_LANES // 128):
        v = ref[:, 128 * hp:128 * (hp + 1)]
        first = _iota(v.shape, 1) < 64
        out += [jnp.where(first, v, 0).astype(BF16), jnp.where(first, 0, v).astype(BF16)]
    return out


SB_STRIP = 32


def _neg_abs(x):
    bits = lax.bitcast_convert_type(x, jnp.uint32) | jnp.uint32(0x80000000)
    return lax.bitcast_convert_type(bits, F32)


def _sb_block(ref, j):
    off = pl.multiple_of(j * SB_BK, SB_BK)
    return [ref[pl.ds(off, SB_BK), 128 * hp:128 * (hp + 1)] for hp in range(SB_NCH // 2)]


SB_DEAD = 104.0


def _sb_live(nlrun):
    m = nlrun[0]
    for x in nlrun[1:]:
        m = jnp.minimum(m, x)
    return jnp.min(m) < SB_DEAD


def _sb_strips():
    return [(r, pl.ds(r, SB_STRIP)) for r in range(0, SB_BQ, SB_STRIP)]


def _sb_diag_mask(r):
    return _iota((SB_STRIP, SB_BK), 1) < _iota((SB_STRIP, SB_BK), 0) + r


def _sb_soft(z, mask):
    e = jnp.exp(_neg_abs(z))
    nl = jnp.maximum(z, 0.0) + jnp.log(1.0 + e)
    if mask is not None:
        nl = jnp.where(mask, nl, 0.0)
    return e, nl


def _sb_split_to(hl_ref, rows, x):
    hi, lo = _terms(x, 2)
    hl_ref[rows, 0:SB_BK] = hi
    hl_ref[rows, SB_BK:2 * SB_BK] = lo


def _sb_stage_soft(z_ref, nl_ref, diag):
    for r, rows in _sb_strips():
        _, nl = _sb_soft(z_ref[rows, :], _sb_diag_mask(r) if diag else None)
        nl_ref[rows, 0:SB_BK] = nl.astype(BF16)


def _sb_stage_weights(z_ref, c_ref, a_ref, nlrun, diag):
    for r, rows in _sb_strips():
        a = jnp.exp(z_ref[rows, :] - c_ref[rows, :] - nlrun[r:r + SB_STRIP, :])
        if diag:
            a = jnp.where(_sb_diag_mask(r), a, 0.0)
        a_ref[rows, :] = a.astype(BF16)


def _sb_fwd(qkv, rides, *, name):
    s = qkv.shape[0]
    nq = s // SB_BQ
    ng = D // SB_LANES
    assert SB_BQ == SB_BK
    rd = _Rides(rides, [True] * len(rides))

    def body(*refs):
        (q_ref, k_ref, v_ref), (o_ref,), (zbuf, nlbuf, cbuf, abuf), handles = rd.split(refs, 3, 1, 4)
        i = pl.program_id(1)
        step_no = pl.program_id(0) * nq + i
        rd.run(handles, step_no == 0, step_no == ng * nq - 1)

        _, u_incl = _sb_consts()
        lane_a = _iota((SB_BQ, 128), 1) < 64
        qh = [q * 0.125 for q in _sb_heads(q_ref)]

        def tile(j, accs, nlrun, diag):
            kbs = _sb_block(k_ref, j)
            for c in range(SB_NCH):
                zbuf[c] = _nt(qh[c], kbs[c // 2])
            for c in range(SB_NCH):
                _sb_stage_soft(zbuf.at[c], nlbuf.at[c], diag)
                cbuf[c] = _nn(nlbuf[c], u_incl)
            for c in range(SB_NCH):
                _sb_stage_weights(zbuf.at[c], cbuf.at[c], abuf.at[c], nlrun[c], diag)
            nlrun = tuple(nlrun[c] + cbuf[c, :, 0:1] for c in range(SB_NCH))
            vbs = _sb_block(v_ref, j)
            outs = [_nn(abuf[c], vbs[c // 2]) for c in range(SB_NCH)]
            accs = tuple(acc + jnp.where(lane_a, outs[2 * hp], outs[2 * hp + 1]) for hp, acc in enumerate(accs))
            return accs, nlrun

        accs, nlrun = tile(i, (jnp.zeros((SB_BQ, 128), F32),) * (SB_NCH // 2),
                           (jnp.zeros((SB_BQ, 1), F32),) * SB_NCH, True)

        def step(carry):
            j, _, accs, nlrun = carry
            accs, nlrun = tile(j, accs, nlrun, False)
            return j - 1, _sb_live(nlrun), accs, nlrun

        _, _, accs, _ = lax.while_loop(lambda c: (c[0] >= 0) & c[1], step, (i - 1, _sb_live(nlrun), accs, nlrun))
        o_ref[...] = jnp.concatenate(accs, axis=1)

    return pl.pallas_call(
        body, name=name, grid=(ng, nq),
        in_specs=[pl.BlockSpec((SB_BQ, SB_LANES), lambda g, i: (i, g)),
                  pl.BlockSpec((s, SB_LANES), lambda g, i: (0, ng + g)),
                  pl.BlockSpec((s, SB_LANES), lambda g, i: (0, 2 * ng + g)), *rd.in_specs],
        out_specs=[pl.BlockSpec((SB_BQ, SB_LANES), lambda g, i: (i, g)), *rd.out_specs],
        out_shape=[jax.ShapeDtypeStruct((s, D), F32), *rd.out_shape],
        scratch_shapes=[pltpu.VMEM((SB_NCH, SB_BQ, SB_BK), F32), pltpu.VMEM((SB_NCH, SB_BQ, SB_BK), BF16),
                        pltpu.VMEM((SB_NCH, SB_BQ, SB_BK), F32), pltpu.VMEM((SB_NCH, SB_BQ, SB_BK), BF16),
                        *rd.scratch],
        compiler_params=_cp("arbitrary", "arbitrary"),
    )(qkv, qkv, qkv, *rides)


def _sb_bwd(qkv, o, do, rides, *, name):
    s = qkv.shape[0]
    nq = s // SB_BQ
    ng = D // SB_LANES
    nhp = SB_NCH // 2
    rd = _Rides(rides, [False] * len(rides))

    def body(*refs):
        ins, outs, scratch, handles = rd.split(refs, 5, 3, 11)
        q_ref, k_ref, v_ref, o_ref, do_ref = ins
        dq_ref, dk_hbm, dv_hbm = outs
        dk_acc, dv_acc, dk16, dv16, sems, zbuf, gbuf, hl, cbuf, abuf, dzbuf = scratch
        g_idx = pl.program_id(0)
        i = pl.program_id(1)
        step_no = g_idx * nq + i
        rd.run(handles, step_no == 0, step_no == ng * nq - 1)

        @pl.when(i == 0)
        def _():
            dk_acc[...] = jnp.zeros_like(dk_acc)
            dv_acc[...] = jnp.zeros_like(dv_acc)

        _, u_incl = _sb_consts()
        u2 = jnp.concatenate([u_incl, u_incl], axis=0)
        lane_a = _iota((SB_BQ, 128), 1) < 64
        lane_k = _iota((SB_BK, 128), 1) < 64
        qh = [q * 0.125 for q in _sb_heads(q_ref)]
        qf = [q_ref[:, 128 * hp:128 * (hp + 1)] for hp in range(nhp)]
        doh = _sb_heads(do_ref)
        dof = [do_ref[:, 128 * hp:128 * (hp + 1)].astype(BF16) for hp in range(nhp)]
        delta = []
        for hp in range(nhp):
            prod = dof[hp].astype(F32) * o_ref[:, 128 * hp:128 * (hp + 1)]
            delta += [jnp.sum(jnp.where(lane_a, prod, 0.0), axis=1, keepdims=True),
                      jnp.sum(jnp.where(lane_a, 0.0, prod), axis=1, keepdims=True)]

        def pre(slot, j):
            kbs = _sb_block(k_ref, j)
            vbs = _sb_block(v_ref, j)
            for c in range(SB_NCH):
                zbuf[slot, c] = _nt(qh[c], kbs[c // 2])
                gbuf[slot, c] = _nt(doh[c], vbs[c // 2])

        def stage_g(c, slot):
            for _, rows in _sb_strips():
                g = abuf[slot, c, rows, :].astype(F32) * gbuf[slot, c, rows, :]
                gbuf[slot, c, rows, :] = g
                _sb_split_to(hl.at[c], rows, g)

        def stage_dz(c, slot, grun, diag):
            for r, rows in _sb_strips():
                z = zbuf[slot, c, rows, :]
                g = gbuf[slot, c, rows, :]
                cs = (delta[c] - grun)[r:r + SB_STRIP, :] - cbuf[c, rows, :]
                sig = 1.0 / (1.0 + jnp.exp(-z))
                dz = g - (g + cs) * sig
                if diag:
                    dz = jnp.where(_sb_diag_mask(r), dz, 0.0)
                dzbuf[slot, c, rows, :] = dz.astype(BF16)

        def chain(slot, nlrun, grun, diag):
            for c in range(SB_NCH):
                _sb_stage_soft(zbuf.at[slot, c], hl.at[c], diag)
                cbuf[c] = _nn(hl[c, :, 0:SB_BK], u_incl)
            nl_tot = []
            for c in range(SB_NCH):
                _sb_stage_weights(zbuf.at[slot, c], cbuf.at[c], abuf.at[slot, c], nlrun[c], diag)
                nl_tot.append(cbuf[c, :, 0:1])
                stage_g(c, slot)
                cbuf[c] = _nn(hl[c], u2)
            g_tot = []
            for c in range(SB_NCH):
                stage_dz(c, slot, grun[c], diag)
                g_tot.append(cbuf[c, :, 0:1])
            return (tuple(a + b for a, b in zip(nlrun, nl_tot)), tuple(a + b for a, b in zip(grun, g_tot)))

        def post(slot, j, dqs):
            off = pl.multiple_of(j * SB_BK, SB_BK)
            kbs = _sb_block(k_ref, j)
            dq_t = [_nn(dzbuf[slot, c], kbs[c // 2]) for c in range(SB_NCH)]
            dk_t = [_tn(dzbuf[slot, c], qf[c // 2]) for c in range(SB_NCH)]
            dv_t = [_tn(abuf[slot, c], dof[c // 2]) for c in range(SB_NCH)]
            for hp in range(nhp):
                cols = slice(128 * hp, 128 * (hp + 1))
                dk_acc[pl.ds(off, SB_BK), cols] += 0.125 * jnp.where(lane_k, dk_t[2 * hp], dk_t[2 * hp + 1])
                dv_acc[pl.ds(off, SB_BK), cols] += jnp.where(lane_k, dv_t[2 * hp], dv_t[2 * hp + 1])
            return tuple(dq + jnp.where(lane_a, dq_t[2 * hp], dq_t[2 * hp + 1]) for hp, dq in enumerate(dqs))

        def tile(j, dqs, nlrun, grun, diag):
            pre(0, j)
            nlrun, grun = chain(0, nlrun, grun, diag)
            return post(0, j, dqs), nlrun, grun

        zero = (jnp.zeros((SB_BQ, 1), F32),) * SB_NCH
        dqs, nlrun, grun = tile(i, (jnp.zeros((SB_BQ, 128), F32),) * nhp, zero, zero, True)

        def step(carry):
            j, _, dqs, nlrun, grun = carry
            dqs, nlrun, grun = tile(j, dqs, nlrun, grun, False)
            return j - 1, _sb_live(nlrun), dqs, nlrun, grun

        carry = lax.while_loop(lambda c: (c[0] >= 0) & c[1], step, (i - 1, _sb_live(nlrun), dqs, nlrun, grun))
        dq_ref[...] = (0.125 * jnp.concatenate(carry[2], axis=1)).astype(BF16)

        def out_copies(g):
            cols = pl.ds(pl.multiple_of(g * SB_LANES, SB_LANES), SB_LANES)
            return (pltpu.make_async_copy(dk16, dk_hbm.at[:, cols], sems.at[0]),
                    pltpu.make_async_copy(dv16, dv_hbm.at[:, cols], sems.at[1]))

        @pl.when((i == nq - 1) & (g_idx > 0))
        def _():
            for cp in out_copies(g_idx - 1):
                cp.wait()

        @pl.when(i == nq - 1)
        def _():
            def narrow(r, carry):
                rows = pl.ds(pl.multiple_of(r * SB_BK, SB_BK), SB_BK)
                dk16[rows, :] = dk_acc[rows, :].astype(BF16)
                dv16[rows, :] = dv_acc[rows, :].astype(BF16)
                return carry

            lax.fori_loop(0, s // SB_BK, narrow, 0)
            for cp in out_copies(g_idx):
                cp.start()

        @pl.when((i == nq - 1) & (g_idx == ng - 1))
        def _():
            for cp in out_copies(g_idx):
                cp.wait()

    qblk = pl.BlockSpec((SB_BQ, SB_LANES), lambda g, i: (i, g))
    hbm = pl.BlockSpec(memory_space=pl.ANY)
    return pl.pallas_call(
        body, name=name, grid=(ng, nq),
        in_specs=[qblk, pl.BlockSpec((s, SB_LANES), lambda g, i: (0, ng + g)),
                  pl.BlockSpec((s, SB_LANES), lambda g, i: (0, 2 * ng + g)), qblk, qblk, *rd.in_specs],
        out_specs=[qblk, hbm, hbm, *rd.out_specs],
        out_shape=[jax.ShapeDtypeStruct((s, D), BF16)] * 3 + rd.out_shape,
        scratch_shapes=[pltpu.VMEM((s, SB_LANES), F32), pltpu.VMEM((s, SB_LANES), F32),
                        pltpu.VMEM((s, SB_LANES), BF16), pltpu.VMEM((s, SB_LANES), BF16),
                        pltpu.SemaphoreType.DMA((2,)),
                        pltpu.VMEM((1, SB_NCH, SB_BQ, SB_BK), F32), pltpu.VMEM((1, SB_NCH, SB_BQ, SB_BK), F32),
                        pltpu.VMEM((SB_NCH, SB_BQ, 2 * SB_BK), BF16), pltpu.VMEM((SB_NCH, SB_BQ, SB_BK), F32),
                        pltpu.VMEM((1, SB_NCH, SB_BQ, SB_BK), BF16), pltpu.VMEM((1, SB_NCH, SB_BQ, SB_BK), BF16),
                        *rd.scratch],
        compiler_params=_cp("arbitrary", "arbitrary"),
    )(qkv, qkv, qkv, o, do, *rides)


def _ssd_core(z, xpre, dtr, state, dtb, alog, dsk, nw):
    L = SSD_CHUNK
    xa = _silu(xpre)
    pieces = _split(xa, 128)
    xs = jnp.concatenate(pieces[:8], axis=1)
    bm, cm = pieces[8:10], pieces[10:12]
    dt = _softplus(dtr + dtb)
    a = dt * (-jnp.exp(alog))
    tri = (_iota((L, L), 0) >= _iota((L, L), 1)).astype(F32)
    a_cs = _dot("nn", tri, a, 1, 3)
    xc = xs * dt
    tril = _iota((L, L), 0) >= _iota((L, L), 1)
    lane_a = _iota((L, 128), 1) < 64
    acs_p = _split(a_cs, 128)
    xc_p = _split(xc, 128)
    ys, new_states = [], []
    for g in range(2):
        cb = _dot("nt", cm[g], bm[g])
        for pp in range(4):
            pair = 4 * g + pp
            acs = acs_p[pair]
            acs_t = acs.T
            xcp = xc_p[pair]
            st = state[pair]
            heads = []
            for hh in range(2):
                col = _take_fn(1, 64 * hh)(acs)
                row = _take_fn(0, 64 * hh)(acs_t)
                seg = col - row
                lm = jnp.where(tril, jnp.exp(jnp.where(tril, seg, 0.0)), 0.0)
                heads.append(_dot("nn", cb * lm, xcp))
            ydiag = jnp.where(lane_a, heads[0], heads[1])
            last = _take_fn(0, L - 1)(acs)
            snew = _dot("tn", xcp * jnp.exp(last - acs), bm[g])
            new_states.append(st * jnp.exp(_take_fn(1, L - 1)(acs_t)) + snew)
            yoff = _dot("nt", cm[g], st) * jnp.exp(acs)
            ys.append(ydiag + yoff)
    y = jnp.concatenate(ys, axis=1) + xs * dsk
    yg = y * _silu(z)
    outs = []
    for v in _split(yg, 512):
        outs.append(v * lax.rsqrt(jnp.mean(v * v, axis=-1, keepdims=True) + EPS))
    return jnp.concatenate(outs, axis=1) * nw, tuple(new_states)


XBC = 1536


def _ssd_conv(ext_ref, cw, cb):
    acc = cb
    for k in range(4):
        acc = acc + cw[k:k + 1, :] * ext_ref[pl.ds(HALO - 3 + k, SSD_CHUNK), :]
    return acc


def _ssd_fwd(z, xbc, dtr, cw, cb, lanes, rides, *, name):
    s = z.shape[0]
    L = SSD_CHUNK
    nc = s // L
    rd = _Rides(rides, [True] * len(rides))

    def body(*refs):
        ins, (y_ref, st_ref), (state, ext), handles = rd.split(refs, 7, 2, 2)
        z_ref, x_ref, h_ref, dtr_ref, cw_ref, cb_ref, ln_ref = ins
        c = pl.program_id(0)
        rd.run(handles, c == 0, c == nc - 1)

        @pl.when(c == 0)
        def _():
            state[...] = jnp.zeros_like(state)

        ext[0:HALO, :] = jnp.where(c == 0, 0.0, h_ref[...])
        ext[HALO:, :] = x_ref[...]
        xpre = _ssd_conv(ext, cw_ref[...], cb_ref[...])
        st_ref[0] = state[...]
        st_in = tuple(state[p] for p in range(8))
        yn, st_out = _ssd_core(z_ref[...], xpre, dtr_ref[...], st_in,
                               ln_ref[0:1, :], ln_ref[1:2, :], ln_ref[2:3, :], ln_ref[3:4, :])
        y_ref[...] = yn.astype(BF16)
        for p in range(8):
            state[p] = st_out[p]

    return pl.pallas_call(
        body, name=name, grid=(nc,),
        in_specs=[pl.BlockSpec((L, D), lambda c: (c, 0)),
                  pl.BlockSpec((L, XBC), lambda c: (c, 0)),
                  pl.BlockSpec((HALO, XBC), lambda c: (jnp.maximum(c * (L // HALO) - 1, 0), 0)),
                  pl.BlockSpec((L, D), lambda c: (c, 0)),
                  pl.BlockSpec((4, XBC), lambda c: (0, 0)),
                  pl.BlockSpec((1, XBC), lambda c: (0, 0)),
                  pl.BlockSpec((8, D), lambda c: (0, 0)), *rd.in_specs],
        out_specs=[pl.BlockSpec((L, D), lambda c: (c, 0)),
                   pl.BlockSpec((1, 8, 128, 128), lambda c: (c, 0, 0, 0)), *rd.out_specs],
        out_shape=[jax.ShapeDtypeStruct((s, D), BF16), jax.ShapeDtypeStruct((nc, 8, 128, 128), F32),
                   *rd.out_shape],
        scratch_shapes=[pltpu.VMEM((8, 128, 128), F32), pltpu.VMEM((L + HALO, XBC), F32), *rd.scratch],
        compiler_params=_cp("arbitrary"),
    )(z, xbc, xbc, dtr, cw, cb, lanes, *rides)


def _ssd_bwd(z, xbc, dtr, states, dymix, cw, cb, lanes, *, name):
    s = z.shape[0]
    L = SSD_CHUNK
    nc = s // L

    def body(z_ref, x_ref, h_ref, dtr_ref, st_ref, dy_ref, cw_ref, cb_ref, ln_ref,
             dz_ref, dx_ref, ddt_ref, dln_ref, dcv_ref, dstate, ext, dext):
        i = pl.program_id(0)
        c = nc - 1 - i

        @pl.when(i == 0)
        def _():
            dstate[...] = jnp.zeros_like(dstate)
            dext[...] = jnp.zeros_like(dext)
            dln_ref[...] = jnp.zeros_like(dln_ref)
            dcv_ref[...] = jnp.zeros_like(dcv_ref)

        ext[0:HALO, :] = jnp.where(c == 0, 0.0, h_ref[...])
        ext[HALO:, :] = x_ref[...]
        cwv = cw_ref[...]
        xpre = _ssd_conv(ext, cwv, cb_ref[...])
        st_in = tuple(st_ref[0, p] for p in range(8))
        _, vjp = jax.vjp(_ssd_core, z_ref[...], xpre, dtr_ref[...], st_in,
                         ln_ref[0:1, :], ln_ref[1:2, :], ln_ref[2:3, :], ln_ref[3:4, :])
        dz, dxpre, ddtr, dst, d0, d1, d2, d3 = vjp((dy_ref[...], tuple(dstate[p] for p in range(8))))
        for p in range(8):
            dstate[p] = dst[p]
        dz_ref[...] = dz.astype(BF16)
        ddt_ref[...] = ddtr.astype(BF16)
        dln_ref[0:4, :] += jnp.concatenate([d0, d1, d2, d3], axis=0)
        dext[0:L, :] = dxpre
        dx = jnp.zeros((L, XBC), F32)
        rows = []
        for k in range(4):
            dx = dx + cwv[k:k + 1, :] * dext[pl.ds(3 - k, L), :]
            rows.append(jnp.sum(dxpre * ext[pl.ds(HALO - 3 + k, L), :], axis=0, keepdims=True))
        rows.append(jnp.sum(dxpre, axis=0, keepdims=True))
        dx_ref[...] = dx.astype(BF16)
        dcv_ref[0:5, :] += jnp.concatenate(rows, axis=0)
        dext[L:L + HALO, :] = dxpre[0:HALO, :]

    rev = lambda i: (nc - 1 - i, 0)
    return pl.pallas_call(
        body, name=name, grid=(nc,),
        in_specs=[pl.BlockSpec((L, D), rev),
                  pl.BlockSpec((L, XBC), rev),
                  pl.BlockSpec((HALO, XBC), lambda i: (jnp.maximum((nc - 1 - i) * (L // HALO) - 1, 0), 0)),
                  pl.BlockSpec((L, D), rev),
                  pl.BlockSpec((1, 8, 128, 128), lambda i: (nc - 1 - i, 0, 0, 0)),
                  pl.BlockSpec((L, D), rev),
                  pl.BlockSpec((4, XBC), lambda i: (0, 0)),
                  pl.BlockSpec((1, XBC), lambda i: (0, 0)),
                  pl.BlockSpec((8, D), lambda i: (0, 0))],
        out_specs=[pl.BlockSpec((L, D), rev), pl.BlockSpec((L, XBC), rev), pl.BlockSpec((L, D), rev),
                   pl.BlockSpec((8, D), lambda i: (0, 0)), pl.BlockSpec((8, XBC), lambda i: (0, 0))],
        out_shape=[jax.ShapeDtypeStruct((s, D), BF16), jax.ShapeDtypeStruct((s, XBC), BF16),
                   jax.ShapeDtypeStruct((s, D), BF16), jax.ShapeDtypeStruct((8, D), F32),
                   jax.ShapeDtypeStruct((8, XBC), F32)],
        scratch_shapes=[pltpu.VMEM((8, 128, 128), F32), pltpu.VMEM((L + HALO, XBC), F32),
                        pltpu.VMEM((L + HALO, XBC), F32)],
        compiler_params=_cp("arbitrary"),
    )(z, xbc, xbc, dtr, states, dymix, cw, cb, lanes)


def _mem_attn_math(q, k, v):
    outs = []
    for qh, kh, vh in zip(_split(q, 256), _split(k, 256), _split(v, 256)):
        sc = _dot("nt", qh, kh) * (1.0 / 16.0)
        e = jnp.exp(sc - lax.stop_gradient(jnp.max(sc, axis=-1, keepdims=True)))
        p = e / jnp.sum(e, axis=-1, keepdims=True)
        outs.append(_dot("nn", p, vh))
    return jnp.concatenate(outs, axis=1)


def _mem_attn_fwd(q, k, v, *, name):
    s, m = q.shape[0], k.shape[0]
    tm = _tile(s, 256, 8)

    def body(q_ref, k_ref, v_ref, o_ref):
        o_ref[...] = _mem_attn_math(q_ref[...].astype(F32), k_ref[...].astype(F32),
                                    v_ref[...].astype(F32)).astype(BF16)

    row = pl.BlockSpec((tm, D), lambda i: (i, 0))
    kv = pl.BlockSpec((m, D), lambda i: (0, 0))
    return pl.pallas_call(
        body, name=name, grid=(s // tm,), in_specs=[row, kv, kv], out_specs=row,
        out_shape=jax.ShapeDtypeStruct((s, D), BF16), compiler_params=_cp("parallel"),
    )(q, k, v)


def _mem_attn_bwd(q, k, v, do, *, name):
    s, m = q.shape[0], k.shape[0]
    tm = _tile(s, 256, 8)

    def body(q_ref, k_ref, v_ref, do_ref, dq_ref, dk_ref, dv_ref):
        @pl.when(pl.program_id(0) == 0)
        def _():
            dk_ref[...] = jnp.zeros_like(dk_ref)
            dv_ref[...] = jnp.zeros_like(dv_ref)

        _, vjp = jax.vjp(_mem_attn_math, q_ref[...].astype(F32), k_ref[...].astype(F32),
                         v_ref[...].astype(F32))
        dq, dk, dv = vjp(do_ref[...])
        dq_ref[...] = dq.astype(BF16)
        dk_ref[...] += dk
        dv_ref[...] += dv

    row = pl.BlockSpec((tm, D), lambda i: (i, 0))
    kv = pl.BlockSpec((m, D), lambda i: (0, 0))
    return pl.pallas_call(
        body, name=name, grid=(s // tm,), in_specs=[row, kv, kv, row], out_specs=[row, kv, kv],
        out_shape=[jax.ShapeDtypeStruct((s, D), BF16), jax.ShapeDtypeStruct((m, D), F32),
                   jax.ShapeDtypeStruct((m, D), F32)],
        compiler_params=_cp("arbitrary"),
    )(q, k, v, do)


DFF = 2816
FFN_TC = 1408
FFN_TM = 256


def _ffn_conv(ext_ref, cw, cb, tm):
    acc = cb
    for k in range(3):
        acc = acc + cw[k:k + 1, :] * ext_ref[pl.ds(HALO - 2 + k, tm), :]
    return acc


def _ffn_specs(s):
    tm, tc = FFN_TM, FFN_TC
    blk = pl.BlockSpec((tm, tc), lambda i, j: (i, j))
    halo = pl.BlockSpec((HALO, tc), lambda i, j: (jnp.maximum(i * (tm // HALO) - 1, 0), j))
    cw = pl.BlockSpec((3, tc), lambda i, j: (0, j))
    cb = pl.BlockSpec((1, tc), lambda i, j: (0, j))
    return tm, tc, blk, halo, cw, cb


def _glu_fwd(ug, uv, cwg, cwv, cbg, cbv, *, name):
    s = ug.shape[0]
    tm, tc, blk, halo, cw, cb = _ffn_specs(s)

    def body(g_ref, gh_ref, v_ref, vh_ref, cwg_ref, cwv_ref, cbg_ref, cbv_ref, f_ref, eg, ev):
        first = pl.program_id(0) == 0
        eg[0:HALO, :] = jnp.where(first, 0.0, gh_ref[...])
        eg[HALO:, :] = g_ref[...]
        ev[0:HALO, :] = jnp.where(first, 0.0, vh_ref[...])
        ev[HALO:, :] = v_ref[...]
        g = _ffn_conv(eg, cwg_ref[...], cbg_ref[...], tm)
        v = _ffn_conv(ev, cwv_ref[...], cbv_ref[...], tm)
        f_ref[...] = (_silu(g) * v).astype(BF16)

    return pl.pallas_call(
        body, name=name, grid=(s // tm, DFF // tc),
        in_specs=[blk, halo, blk, halo, cw, cw, cb, cb], out_specs=blk,
        out_shape=jax.ShapeDtypeStruct((s, DFF), BF16),
        scratch_shapes=[pltpu.VMEM((tm + HALO, tc), F32)] * 2,
        compiler_params=_cp("parallel", "parallel"),
    )(ug, ug, uv, uv, cwg, cwv, cbg, cbv)


def _ffn_bwd(ug, uv, df, cwg, cwv, cbg, cbv, *, name):
    s = ug.shape[0]
    tm, tc = FFN_TM, FFN_TC
    nb = s // tm
    rows_ext = tm + HALO

    def body(g_ref, gp_ref, gn_ref, v_ref, vp_ref, vn_ref, df_ref, dfn_ref, cwg_ref, cwv_ref, cbg_ref, cbv_ref,
             dxg_ref, dxv_ref, dcg_ref, dcv_ref, eg, ev, edg, edv):
        i = pl.program_id(1)
        first, last = i == 0, i == nb - 1

        @pl.when(first)
        def _():
            dcg_ref[...] = jnp.zeros_like(dcg_ref)
            dcv_ref[...] = jnp.zeros_like(dcv_ref)

        for e, prev, main, nxt in ((eg, gp_ref, g_ref, gn_ref), (ev, vp_ref, v_ref, vn_ref)):
            e[0:HALO, :] = jnp.where(first, 0.0, prev[...])
            e[HALO:HALO + tm, :] = main[...]
            e[HALO + tm:, :] = jnp.where(last, 0.0, nxt[...])
        dfe = jnp.concatenate([df_ref[...], jnp.where(last, 0.0, dfn_ref[...])], axis=0)
        cwgv, cwvv = cwg_ref[...], cwv_ref[...]
        ugs = [eg[pl.ds(HALO - 2 + k, rows_ext), :] for k in range(3)]
        uvs = [ev[pl.ds(HALO - 2 + k, rows_ext), :] for k in range(3)]
        g = cbg_ref[...] + cwgv[0:1, :] * ugs[0] + cwgv[1:2, :] * ugs[1] + cwgv[2:3, :] * ugs[2]
        v = cbv_ref[...] + cwvv[0:1, :] * uvs[0] + cwvv[1:2, :] * uvs[1] + cwvv[2:3, :] * uvs[2]
        sg = _sigmoid(g)
        edv[...] = dfe * g * sg
        edg[...] = dfe * v * sg * (1.0 + g * (1.0 - sg))
        for edu, us, cw, dx_ref, dc_ref in ((edg, ugs, cwgv, dxg_ref, dcg_ref), (edv, uvs, cwvv, dxv_ref, dcv_ref)):
            du = edu[0:tm, :]
            dx = cw[2:3, :] * du
            sums = []
            for k in range(3):
                if k < 2:
                    dx = dx + cw[k:k + 1, :] * edu[pl.ds(2 - k, tm), :]
                sums.append(jnp.sum(du * us[k][0:tm], axis=0, keepdims=True))
            sums.append(jnp.sum(du, axis=0, keepdims=True))
            dx_ref[...] = dx.astype(BF16)
            dc_ref[0:4, :] += jnp.concatenate(sums, axis=0)

    blk = pl.BlockSpec((tm, tc), lambda j, i: (i, j))
    nxt = pl.BlockSpec((HALO, tc), lambda j, i: (jnp.minimum((i + 1) * (tm // HALO), s // HALO - 1), j))
    prv = pl.BlockSpec((HALO, tc), lambda j, i: (jnp.maximum(i * (tm // HALO) - 1, 0), j))
    cw = pl.BlockSpec((3, tc), lambda j, i: (0, j))
    cb = pl.BlockSpec((1, tc), lambda j, i: (0, j))
    acc = pl.BlockSpec((8, tc), lambda j, i: (0, j))
    return pl.pallas_call(
        body, name=name, grid=(DFF // tc, nb),
        in_specs=[blk, prv, nxt, blk, prv, nxt, blk, nxt, cw, cw, cb, cb],
        out_specs=[blk, blk, acc, acc],
        out_shape=[jax.ShapeDtypeStruct((s, DFF), BF16)] * 2 + [jax.ShapeDtypeStruct((8, DFF), F32)] * 2,
        scratch_shapes=[pltpu.VMEM((tm + 2 * HALO, tc), F32)] * 2 + [pltpu.VMEM((rows_ext, tc), F32)] * 2,
        compiler_params=_cp("parallel", "arbitrary"),
    )(ug, ug, ug, uv, uv, uv, df, df, cwg, cwv, cbg, cbv)


MESH = pl.DeviceIdType.MESH


def _all_gather(arrs, *, name):
    n = len(arrs)

    def body(*refs):
        x_refs, out_refs = refs[:n], refs[n:2 * n]
        send_sems, recv_sems, local_sems = refs[2 * n:]
        x, y, c = lax.axis_index("x"), lax.axis_index("y"), lax.axis_index("c")
        me, sibling = (x, y, c), (x, y, 1 - c)
        chips = [(1 - x, y), (x, 1 - y), (1 - x, 1 - y)]

        def blk(a, dev):
            return out_refs[a].at[4 * dev[0] + 2 * dev[1] + dev[2]]

        def copy(a, k, block, to, src=None):
            return pltpu.make_async_remote_copy(
                src_ref=blk(a, block) if src is None else src, dst_ref=blk(a, block),
                send_sem=send_sems.at[7 * a + k], recv_sem=recv_sems.at[7 * a + k],
                device_id=to, device_id_type=MESH)

        started = []
        mine = []
        for a in range(n):
            cp = pltpu.make_async_copy(x_refs[a], blk(a, me), local_sems.at[a])
            cp.start()
            mine.append(cp)
            first = [copy(a, 0, me, sibling, src=x_refs[a])]
            first += [copy(a, 1 + j, me, (*chip, c), src=x_refs[a]) for j, chip in enumerate(chips)]
            for cp in first:
                cp.start()
            started += first
        for a in range(n):
            for j, chip in enumerate(chips):
                copy(a, 1 + j, (*chip, c), me).wait_recv()
                fwd = copy(a, 4 + j, (*chip, c), sibling)
                fwd.start()
                started.append(fwd)
        for a in range(n):
            copy(a, 0, sibling, me).wait_recv()
            for j, chip in enumerate(chips):
                copy(a, 4 + j, (*chip, 1 - c), me).wait_recv()
        for cp in started:
            cp.wait_send()
        for cp in mine:
            cp.wait()

    any_spec = pl.BlockSpec(memory_space=pl.ANY)
    return pl.pallas_call(
        body, name=name,
        in_specs=[any_spec] * n, out_specs=[any_spec] * n,
        out_shape=[jax.ShapeDtypeStruct((NDEV,) + a.shape, a.dtype) for a in arrs],
        scratch_shapes=[pltpu.SemaphoreType.DMA((7 * n,)), pltpu.SemaphoreType.DMA((7 * n,)),
                        pltpu.SemaphoreType.DMA((n,))],
    )(*arrs)


class _Direct:
    SEMS = (pltpu.SemaphoreType.DMA((7,)), pltpu.SemaphoreType.DMA((7,)), pltpu.SemaphoreType.DMA((1,)))

    def __init__(self, src_ref, recv_ref, sems, gather):
        x, y, c = lax.axis_index("x"), lax.axis_index("y"), lax.axis_index("c")
        me = 4 * x + 2 * y + c
        send_sems, recv_sems, local_sem = sems
        src = (lambda pid: src_ref) if gather else (lambda pid: src_ref.at[pid])
        self.mine = pltpu.make_async_copy(src(me), recv_ref.at[me], local_sem.at[0])
        self.copies = []
        for k in range(1, NDEV):
            px = 1 - x if k & 4 else x
            py = 1 - y if k & 2 else y
            pc = 1 - c if k & 1 else c
            self.copies.append(pltpu.make_async_remote_copy(
                src_ref=src(4 * px + 2 * py + pc), dst_ref=recv_ref.at[me],
                send_sem=send_sems.at[k - 1], recv_sem=recv_sems.at[k - 1],
                device_id=(px, py, pc), device_id_type=MESH))

    def start(self):
        self.mine.start()
        for cp in self.copies:
            cp.start()

    def wait(self):
        for cp in self.copies:
            cp.wait_recv()
        for cp in self.copies:
            cp.wait_send()
        self.mine.wait()


def _recv_shape(src, gather):
    return jax.ShapeDtypeStruct(((NDEV,) + src.shape) if gather else src.shape, src.dtype)


class _Rides:
    def __init__(self, rides, gathers):
        self.n = len(rides)
        self.gathers = list(gathers)
        any_spec = pl.BlockSpec(memory_space=pl.ANY)
        self.in_specs = [any_spec] * self.n
        self.out_specs = [any_spec] * self.n
        self.out_shape = [_recv_shape(a, g) for a, g in zip(rides, gathers)]
        self.scratch = list(_Direct.SEMS) * self.n

    def split(self, refs, n_in, n_out, n_scratch):
        n = self.n
        ins, refs = refs[:n_in], refs[n_in:]
        rides, refs = refs[:n], refs[n:]
        outs, refs = refs[:n_out], refs[n_out:]
        gots, refs = refs[:n], refs[n:]
        scratch, sems = refs[:n_scratch], refs[n_scratch:]
        return ins, outs, scratch, (rides, gots, sems)

    def run(self, handles, first, last):
        rides, gots, sems = handles

        def all_of():
            return [_Direct(rides[a], gots[a], sems[3 * a:3 * a + 3], self.gathers[a]) for a in range(self.n)]

        @pl.when(first)
        def _():
            for e in all_of():
                e.start()

        @pl.when(last)
        def _():
            for e in all_of():
                e.wait()


def _exchange(arrs, gathers, *, name):
    rd = _Rides(arrs, gathers)

    def body(*refs):
        _, _, _, handles = rd.split(refs, 0, 0, 0)
        rd.run(handles, True, True)

    return pl.pallas_call(
        body, name=name, in_specs=rd.in_specs, out_specs=rd.out_specs, out_shape=rd.out_shape,
        scratch_shapes=rd.scratch,
    )(*arrs)


def _adamw(parts, w, m, v, *, name):
    r, cols = w.shape
    tm = _tile(r, 256, PACK_ALIGN)
    c1 = 1.0 - ADAM_B1 ** ADAM_STEP
    c2 = 1.0 - ADAM_B2 ** ADAM_STEP

    def body(p_ref, w_ref, m_ref, v_ref, g_ref, d_ref, nm_ref, nv_ref):
        g = p_ref[0].astype(F32)
        for i in range(1, NDEV):
            g = g + p_ref[i].astype(F32)
        nm = ADAM_B1 * m_ref[...] + (1.0 - ADAM_B1) * g
        nv = ADAM_B2 * v_ref[...] + (1.0 - ADAM_B2) * (g * g)
        d_ref[...] = -ADAM_LR * ((nm / c1) / (jnp.sqrt(nv / c2) + ADAM_EPS) + ADAM_WD * w_ref[...])
        g_ref[...] = g
        nm_ref[...] = nm
        nv_ref[...] = nv

    row = pl.BlockSpec((tm, cols), lambda i: (i, 0))
    return pl.pallas_call(
        body, name=name, grid=(r // tm,),
        in_specs=[pl.BlockSpec((NDEV, tm, cols), lambda i: (0, i, 0)), row, row, row],
        out_specs=[row] * 4, out_shape=[jax.ShapeDtypeStruct((r, cols), F32)] * 4,
        compiler_params=_cp("parallel"),
    )(parts, w, m, v)


PACK_ALIGN = 16


def _part_rows(shape):
    n = -(-math.prod(shape) // D)
    return n + (-n) % PACK_ALIGN


def _rows(a):
    flat = a.reshape(-1)
    pad = _part_rows(a.shape) * D - flat.shape[0]
    if pad:
        flat = jnp.concatenate([flat, jnp.zeros((pad,), flat.dtype)])
    return flat.reshape(-1, D)


def _pack(parts, total_rows):
    if all(math.prod(p.shape) % (PACK_ALIGN * D) for p in parts):
        return _pack_small(parts, total_rows)
    rows = [_rows(p) for p in parts]
    used = sum(r.shape[0] for r in rows)
    if total_rows > used:
        rows.append(jnp.zeros((total_rows - used, D), rows[0].dtype))
    return jnp.concatenate(rows, axis=0)


def _pack_small(parts, total_rows):
    flat, used = [], 0
    for p in parts:
        n, nr = math.prod(p.shape), _part_rows(p.shape)
        flat += [p.reshape(-1), jnp.zeros((nr * D - n,), p.dtype)]
        used += nr
    flat.append(jnp.zeros(((total_rows - used) * D,), parts[0].dtype))
    return jnp.concatenate(flat).reshape(total_rows, D)


def _unpack(buf, shapes, part_rows=_part_rows):
    out, r0 = [], 0
    for shp in shapes:
        n = math.prod(shp)
        out.append(buf[r0:r0 + part_rows(shp)].reshape(-1)[:n].reshape(shp))
        r0 += part_rows(shp)
    return out


def _tight_rows(shape):
    return -(-math.prod(shape) // D)


def _pack_tight(parts, total_rows):
    flat, used = [], 0
    for p in parts:
        n, nr = math.prod(p.shape), _tight_rows(p.shape)
        flat += [p.reshape(-1), jnp.zeros((nr * D - n,), p.dtype)]
        used += nr
    flat.append(jnp.zeros(((total_rows - used) * D,), parts[0].dtype))
    return jnp.concatenate(flat).reshape(total_rows, D)


SHARD = {"w_in": (D, 706), "w_out": (256, D), "w_mq": (128, D), "w_mk": (128, D), "w_mv": (128, D),
         "w_mo": (128, D), "w_up": (D, 704), "w_down": (352, D), "conv_ssd_w": (4, 192), "conv_ffn_w": (3, 704)}
GATHER_MID = ["w_out", "w_mq", "w_mk", "w_mv", "w_mo"]
GATHER_FFN = ["w_down"]
CONV_TAPS = ["conv_ssd_w", "conv_ffn_w"]
GRADS_PACKED = ["w_out", "w_mq", "w_mk", "w_mv", "w_mo", "w_down", "conv_ffn_w", "conv_ssd_w"]


def _layout(names):
    row0, r = {}, 0
    for n in names:
        row0[n] = r
        r += _part_rows(SHARD[n])
    return row0, r + (-r) % 128


SMALL = [("norm_mix_w", (1, D)), ("conv_ssd_b", (1, 1536)), ("dt_bias", (1, 16)), ("a_log", (1, 16)),
         ("d_skip", (1, 16)), ("ssd_norm_w", (1, D)), ("sb_norm_w", (1, D)), ("norm_mem_w", (1, D)),
         ("norm_memkv_w", (1, D)), ("norm_ffn_w", (1, D)), ("conv_ffn_b", (1, 5632)), ("norm_final_w", (D,))]
LOSS_ROW = sum(_tight_rows(_shp) for _, _shp in SMALL)
SMALL_ROWS = LOSS_ROW + 1 + (-(LOSS_ROW + 1)) % 8
ORDER = ["norm_mix_w", "w_in", "conv_ssd_w", "conv_ssd_b", "dt_bias", "a_log", "d_skip", "ssd_norm_w",
         "sb_norm_w", "w_out", "norm_mem_w", "norm_memkv_w", "w_mq", "w_mk", "w_mv", "w_mo", "norm_ffn_w",
         "w_up", "conv_ffn_w", "conv_ffn_b", "w_down", "norm_final_w"]


def _pad_rows(a, nr):
    n = a.shape[1]
    return jnp.concatenate([a, jnp.zeros((NDEV, nr * D - n), a.dtype)], axis=1).reshape(NDEV, nr, D)


def _group_sum(lanes):
    return lanes.reshape(16, 64).sum(axis=1).reshape(1, 16)


def kernel(x, mem, norm_mix_w, w_in, conv_ssd_w, conv_ssd_b, dt_bias, a_log, d_skip, ssd_norm_w, sb_norm_w, w_out, norm_mem_w, norm_memkv_w, w_mq, w_mk, w_mv, w_mo, norm_ffn_w, w_up, conv_ffn_w, conv_ffn_b, w_down, norm_final_w, loss_target, m_norm_mix_w, m_w_in, m_conv_ssd_w, m_conv_ssd_b, m_dt_bias, m_a_log, m_d_skip, m_ssd_norm_w, m_sb_norm_w, m_w_out, m_norm_mem_w, m_norm_memkv_w, m_w_mq, m_w_mk, m_w_mv, m_w_mo, m_norm_ffn_w, m_w_up, m_conv_ffn_w, m_conv_ffn_b, m_w_down, m_norm_final_w, v_norm_mix_w, v_w_in, v_conv_ssd_w, v_conv_ssd_b, v_dt_bias, v_a_log, v_d_skip, v_ssd_norm_w, v_sb_norm_w, v_w_out, v_norm_mem_w, v_norm_memkv_w, v_w_mq, v_w_mk, v_w_mv, v_w_mo, v_norm_ffn_w, v_w_up, v_conv_ffn_w, v_conv_ffn_b, v_w_down, v_norm_final_w):
    P = dict(norm_mix_w=norm_mix_w, w_in=w_in, conv_ssd_w=conv_ssd_w, conv_ssd_b=conv_ssd_b, dt_bias=dt_bias, a_log=a_log, d_skip=d_skip, ssd_norm_w=ssd_norm_w, sb_norm_w=sb_norm_w, w_out=w_out, norm_mem_w=norm_mem_w, norm_memkv_w=norm_memkv_w, w_mq=w_mq, w_mk=w_mk, w_mv=w_mv, w_mo=w_mo, norm_ffn_w=norm_ffn_w, w_up=w_up, conv_ffn_w=conv_ffn_w, conv_ffn_b=conv_ffn_b, w_down=w_down, norm_final_w=norm_final_w)
    M = dict(norm_mix_w=m_norm_mix_w, w_in=m_w_in, conv_ssd_w=m_conv_ssd_w, conv_ssd_b=m_conv_ssd_b, dt_bias=m_dt_bias, a_log=m_a_log, d_skip=m_d_skip, ssd_norm_w=m_ssd_norm_w, sb_norm_w=m_sb_norm_w, w_out=m_w_out, norm_mem_w=m_norm_mem_w, norm_memkv_w=m_norm_memkv_w, w_mq=m_w_mq, w_mk=m_w_mk, w_mv=m_w_mv, w_mo=m_w_mo, norm_ffn_w=m_norm_ffn_w, w_up=m_w_up, conv_ffn_w=m_conv_ffn_w, conv_ffn_b=m_conv_ffn_b, w_down=m_w_down, norm_final_w=m_norm_final_w)
    V = dict(norm_mix_w=v_norm_mix_w, w_in=v_w_in, conv_ssd_w=v_conv_ssd_w, conv_ssd_b=v_conv_ssd_b, dt_bias=v_dt_bias, a_log=v_a_log, d_skip=v_d_skip, ssd_norm_w=v_ssd_norm_w, sb_norm_w=v_sb_norm_w, w_out=v_w_out, norm_mem_w=v_norm_mem_w, norm_memkv_w=v_norm_memkv_w, w_mq=v_w_mq, w_mk=v_w_mk, w_mv=v_w_mv, w_mo=v_w_mo, norm_ffn_w=v_norm_ffn_w, w_up=v_w_up, conv_ffn_w=v_conv_ffn_w, conv_ffn_b=v_conv_ffn_b, w_down=v_w_down, norm_final_w=v_norm_final_w)
    small_shapes = [shp for _, shp in SMALL]

    def packed(src, names, dtype=F32):
        return _pack([src[n][0] for n in names], _layout(names)[1]).astype(dtype)

    def columns(g):
        return g.transpose(1, 0, 2).reshape(g.shape[1], NDEV * g.shape[2])

    g_in, g_taps = _all_gather([w_in[0].astype(BF16), packed(P, CONV_TAPS)], name="gather_w_in")
    W_in = columns(g_in)
    cw_ssd = g_taps[:, 0].reshape(NDEV, -1)[:, :768].reshape(NDEV, 4, 192).transpose(1, 0, 2).reshape(4, XBC)
    cw_ffn = (g_taps[:, PACK_ALIGN:PACK_ALIGN + 3].reshape(NDEV, -1)[:, :2112].reshape(NDEV, 3, 704)
              .transpose(1, 0, 2).reshape(3, 2 * DFF))
    W_z, W_xbc, W_dt, W_qkv = W_in[:, :D], W_in[:, D:D + XBC], W_in[:, D + XBC:D + XBC + 16], W_in[:, D + XBC + 16:]
    W_dtr = jnp.repeat(W_dt, 64, axis=1)
    cwg, cwv = cw_ffn[:, :DFF], cw_ffn[:, DFF:]
    cbg, cbv = conv_ffn_b[:, :DFF], conv_ffn_b[:, DFF:]
    rep = lambda p: jnp.repeat(p, 64, axis=1)
    lanes = jnp.concatenate([rep(dt_bias), rep(a_log), rep(d_skip), ssd_norm_w, jnp.zeros((4, D), F32)], axis=0)

    xs, tgt, mm = x[0], loss_target[0], mem[0]

    h1 = _norm_fwd(xs, norm_mix_w, name="norm_mix")
    z = _mm(h1, W_z, name="proj_z")
    xbc = _mm(h1, W_xbc, name="proj_xbc")
    dtr = _mm(h1, W_dtr, name="proj_dt")
    qkv = _mm(h1, W_qkv, name="proj_qkv", out_dtype=BF16)
    y_ssd, states, g_ffn = _ssd_fwd(z, xbc, dtr, cw_ssd, conv_ssd_b, lanes, [packed(P, GATHER_FFN, BF16)],
                                    name="ssd_fwd")
    o_sb, g_mid, g_up = _sb_fwd(qkv, [packed(P, GATHER_MID, BF16), w_up[0].astype(BF16)], name="sb_fwd")
    r_mid = _layout(GATHER_MID)[0]
    W_out = g_mid[:, r_mid["w_out"]:r_mid["w_out"] + 256].reshape(2 * D, D)
    W_mq, W_mk, W_mv, W_mo = [g_mid[:, r_mid[n]:r_mid[n] + 128].reshape(D, D)
                              for n in ("w_mq", "w_mk", "w_mv", "w_mo")]
    W_up = columns(g_up)
    W_down = g_ffn[:, 0:352].reshape(DFF, D)
    W_upg, W_upv = W_up[:, :DFF], W_up[:, DFF:]
    y_sb = _head_norm_fwd(o_sb, sb_norm_w, name="sb_norm")
    ymix = jnp.concatenate([y_ssd, y_sb], axis=1)
    x1 = _mm(ymix, W_out, add=xs, name="proj_out")
    h2 = _norm_fwd(x1, norm_mem_w, name="norm_mem")
    mn = _norm_fwd(mm, norm_memkv_w, name="norm_memkv")
    qm = _mm(h2, W_mq, name="mem_q", out_dtype=BF16)
    km = _mm(mn, W_mk, name="mem_k", out_dtype=BF16)
    vm = _mm(mn, W_mv, name="mem_v", out_dtype=BF16)
    om = _mem_attn_fwd(qm, km, vm, name="mem_attn")
    x2 = _mm(om, W_mo, add=x1, name="mem_o")
    h3 = _norm_fwd(x2, norm_ffn_w, name="norm_ffn")
    ug = _mm(h3, W_upg, name="ffn_up_g")
    uv = _mm(h3, W_upv, name="ffn_up_v")
    f = _glu_fwd(ug, uv, cwg, cwv, cbg, cbv, name="ffn_glu")
    x3 = _mm(f, W_down, add=x2, name="ffn_down")
    dx3, g_nfinal, loss_part = _final(x3, norm_final_w.reshape(1, D), tgt, name="final_loss")

    G = {}
    G["w_down"] = _mm(f, dx3, trans_a=True, name="g_w_down")
    df = _mm(dx3, W_down, trans_b=True, name="d_f")
    dupg, dupv, dcg, dcv = _ffn_bwd(ug, uv, df, cwg, cwv, cbg, cbv, name="ffn_glu_bwd")
    G["w_up"] = jnp.concatenate([_mm(h3, dupg, trans_a=True, name="g_w_up_g"),
                                 _mm(h3, dupv, trans_a=True, name="g_w_up_v")], axis=1)
    G["conv_ffn_w"] = jnp.concatenate([dcg[0:3], dcv[0:3]], axis=1)
    G["conv_ffn_b"] = jnp.concatenate([dcg[3:4], dcv[3:4]], axis=1)
    dh3 = _mm(dupg, W_upg, trans_b=True, name="d_h3_g")
    dh3 = _mm(dupv, W_upv, trans_b=True, add=dh3, name="d_h3_v")
    dx2, G["norm_ffn_w"] = _norm_bwd(x2, norm_ffn_w, dh3, dx3, name="norm_ffn_bwd")
    G["w_mo"] = _mm(om, dx2, trans_a=True, name="g_w_mo")
    dom = _mm(dx2, W_mo, trans_b=True, name="d_om")
    dqm, dkm, dvm = _mem_attn_bwd(qm, km, vm, dom, name="mem_attn_bwd")
    G["w_mq"] = _mm(h2, dqm, trans_a=True, name="g_w_mq")
    G["w_mk"] = _mm(mn, dkm, trans_a=True, name="g_w_mk")
    G["w_mv"] = _mm(mn, dvm, trans_a=True, name="g_w_mv")
    dh2 = _mm(dqm, W_mq, trans_b=True, name="d_h2")
    dmn = _mm(dkm, W_mk, trans_b=True, name="d_mn_k")
    dmn = _mm(dvm, W_mv, trans_b=True, add=dmn, name="d_mn_v")
    _, G["norm_memkv_w"] = _norm_bwd(mm, norm_memkv_w, dmn, None, name="norm_memkv_bwd")
    dx1, G["norm_mem_w"] = _norm_bwd(x1, norm_mem_w, dh2, dx2, name="norm_mem_bwd")
    G["w_out"] = _mm(ymix, dx1, trans_a=True, name="g_w_out")
    dymix = _mm(dx1, W_out, trans_b=True, name="d_ymix")
    do_sb, G["sb_norm_w"] = _head_norm_bwd(o_sb, sb_norm_w, dymix, name="sb_norm_bwd")

    dz, dxbc, ddtr, dlanes, dconv = _ssd_bwd(z, xbc, dtr, states, dymix, cw_ssd, conv_ssd_b, lanes, name="ssd_bwd")
    G["dt_bias"], G["a_log"], G["d_skip"] = [_group_sum(dlanes[i:i + 1]) for i in range(3)]
    G["ssd_norm_w"] = dlanes[3:4]
    G["conv_ssd_w"], G["conv_ssd_b"] = dconv[0:4], dconv[4:5]

    def col_slabs(g, cols):
        return g.reshape(g.shape[0], NDEV, cols).transpose(1, 0, 2).astype(BF16)

    def packed_slabs(names):
        parts = []
        for n in names:
            shp = SHARD[n]
            if shp[-1] == D:
                t = G[n].reshape((NDEV,) + shp)
            else:
                t = _pad_rows(G[n].reshape(shp[0], NDEV, shp[1]).transpose(1, 0, 2).reshape(NDEV, -1),
                              _part_rows(shp))
            parts.append(jnp.pad(t, ((0, 0), (0, _part_rows(shp) - t.shape[1]), (0, 0))))
        used = sum(t.shape[1] for t in parts)
        parts.append(jnp.zeros((NDEV, _layout(names)[1] - used, D), F32))
        return jnp.concatenate(parts, axis=1).astype(BF16)

    dq, dk, dv, recv_packed, recv_up = _sb_bwd(qkv, o_sb, do_sb, [packed_slabs(GRADS_PACKED), col_slabs(G["w_up"], 704)],
                                               name="sb_bwd")
    dqkv = jnp.concatenate([dq, dk, dv], axis=1)
    g_wdt = _mm(h1, ddtr, trans_a=True, name="g_w_dt").reshape(D, 16, 64).sum(axis=2)
    G["w_in"] = jnp.concatenate([_mm(h1, dz, trans_a=True, name="g_w_z"),
                                 _mm(h1, dxbc, trans_a=True, name="g_w_xbc"), g_wdt,
                                 _mm(h1, dqkv, trans_a=True, name="g_w_qkv")], axis=1)
    dh1 = _mm(dz, W_z, trans_b=True, name="d_h1_z")
    dh1 = _mm(dxbc, W_xbc, trans_b=True, add=dh1, name="d_h1_xbc")
    dh1 = _mm(ddtr, W_dtr, trans_b=True, add=dh1, name="d_h1_dt")
    dh1, recv_in = _mm(dqkv, W_qkv, trans_b=True, add=dh1, rides=[col_slabs(G["w_in"], 706)], name="d_h1_qkv")
    dx, G["norm_mix_w"] = _norm_bwd(xs, norm_mix_w, dh1, dx1, name="norm_mix_bwd")
    G["norm_final_w"] = g_nfinal.reshape(D)

    small_g = _pack_tight([G[n] for n, _ in SMALL] + [loss_part], SMALL_ROWS)
    (parts_small,) = _exchange([small_g], [True], name="exchange_grads")
    outs_packed = _adamw(recv_packed, packed(P, GRADS_PACKED), packed(M, GRADS_PACKED), packed(V, GRADS_PACKED),
                         name="adamw_packed")
    outs_up = _adamw(recv_up, w_up[0], m_w_up[0], v_w_up[0], name="adamw_w_up")
    outs_in = _adamw(recv_in, w_in[0], m_w_in[0], v_w_in[0], name="adamw_w_in")
    outs_small = _adamw(parts_small, _pack_tight([P[n] for n, _ in SMALL], SMALL_ROWS),
                        _pack_tight([M[n] for n, _ in SMALL], SMALL_ROWS),
                        _pack_tight([V[n] for n, _ in SMALL], SMALL_ROWS), name="adamw_replicated")

    res = {}
    for i, kind in enumerate(("grad", "delta", "new_m", "new_v")):
        for n, val in zip(GRADS_PACKED, _unpack(outs_packed[i], [SHARD[n] for n in GRADS_PACKED])):
            res[kind, n] = val.reshape((1,) + SHARD[n])
        res[kind, "w_up"] = outs_up[i].reshape((1,) + SHARD["w_up"])
        res[kind, "w_in"] = outs_in[i].reshape((1,) + SHARD["w_in"])
        for (n, shp), val in zip(SMALL, _unpack(outs_small[i], small_shapes, _tight_rows)):
            res[kind, n] = val
    loss = outs_small[0][LOSS_ROW, 0]
    out = [loss, dx.reshape(1, -1, D)]
    for kind in ("grad", "delta", "new_m", "new_v"):
        out += [res[kind, n] for n in ORDER]
    return tuple(out)
```

```python
import functools
import math

import jax
import jax.numpy as jnp
from jax import lax
from jax.experimental import pallas as pl
from jax.experimental.pallas import tpu as pltpu

F32 = jnp.float32
BF16 = jnp.bfloat16

D = 1024
NDEV = 8
EPS = 1e-6
SSD_CHUNK = 128
HALO = 8
VMEM_LIMIT = 56 * 2**20

ADAM_LR, ADAM_B1, ADAM_B2, ADAM_EPS, ADAM_WD, ADAM_STEP = 0.001, 0.9, 0.999, 1e-08, 0.01, 10


def _cp(*sem):
    return pltpu.CompilerParams(dimension_semantics=sem, vmem_limit_bytes=VMEM_LIMIT)


def _tile(n, cap, mult):
    if n <= cap:
        return n
    for d in range(cap - cap % mult, 0, -mult):
        if n % d == 0:
            return d
    raise ValueError(f"no tile for {n}")


def _sigmoid(x):
    return 1.0 / (1.0 + jnp.exp(-x))


def _silu(x):
    return x * _sigmoid(x)


def _softplus(x):
    return jnp.maximum(x, 0.0) + jnp.log1p(jnp.exp(-jnp.abs(x)))


def _terms(x, n):
    out = []
    r = x.astype(F32)
    for i in range(n):
        h = r.astype(BF16)
        out.append(h)
        if i + 1 < n:
            r = r - h.astype(F32)
    return out


_DIMS = {"nn": ((1,), (0,)), "nt": ((1,), (1,)), "tn": ((0,), (0,))}


def _dot_raw(form, a, b, ta, tb):
    acc = None
    for ai in _terms(a, ta):
        for bi in _terms(b, tb):
            d = lax.dot_general(ai, bi, (_DIMS[form], ((), ())), preferred_element_type=F32)
            acc = d if acc is None else acc + d
    return acc


@functools.lru_cache(maxsize=None)
def _dot_fn(form, ta, tb):
    @jax.custom_vjp
    def f(a, b):
        return _dot_raw(form, a, b, ta, tb)

    def fwd(a, b):
        return f(a, b), (a, b)

    def bwd(res, ct):
        a, b = res
        if form == "nn":
            return _dot_fn("nt", ta, tb)(ct, b), _dot_fn("tn", ta, tb)(a, ct)
        if form == "nt":
            return _dot_fn("nn", ta, tb)(ct, b), _dot_fn("tn", tb, ta)(ct, a)
        return _dot_fn("nt", tb, ta)(b, ct), _dot_fn("nn", ta, tb)(a, ct)

    f.defvjp(fwd, bwd)
    return f


def _dot(form, a, b, ta=1, tb=1):
    return _dot_fn(form, ta, tb)(a, b)


@functools.lru_cache(maxsize=None)
def _take_fn(axis, idx):
    @jax.custom_vjp
    def f(x):
        return x[:, idx:idx + 1] if axis == 1 else x[idx:idx + 1, :]

    def fwd(x):
        return f(x), x.shape

    def bwd(shape, ct):
        io = lax.broadcasted_iota(jnp.int32, shape, axis)
        return (jnp.where(io == idx, jnp.broadcast_to(ct, shape), 0.0),)

    f.defvjp(fwd, bwd)
    return f


@functools.lru_cache(maxsize=None)
def _split_fn(width, n):
    @jax.custom_vjp
    def f(x):
        return tuple(x[:, i * width:(i + 1) * width] for i in range(n))

    def fwd(x):
        return f(x), None

    def bwd(_, cts):
        return (jnp.concatenate(list(cts), axis=1),)

    f.defvjp(fwd, bwd)
    return f


def _split(x, width):
    return _split_fn(width, x.shape[1] // width)(x)


def _iota(shape, axis):
    return lax.broadcasted_iota(jnp.int32, shape, axis)


MM_VMEM_BUDGET = 36 * 2**20


def _mm_tiles(m, n, kt, trans_a, a_bytes, b_bytes, out_bytes, add_bytes):
    tn = _tile(n, 1536, 128)
    for tm_cap in (1408, 1024, 512, 256, 128):
        tm = _tile(m, tm_cap, 128 if trans_a else 8)
        for tk_cap in (kt, 4096, 2048, 1024, 512):
            tk = _tile(kt, tk_cap, 128)
            blocks = tm * tk * a_bytes + tk * tn * b_bytes + tm * tn * (out_bytes + add_bytes)
            if 2 * blocks + (tm * tn * 4 if tk < kt else 0) <= MM_VMEM_BUDGET:
                return tm, tn, tk
    raise ValueError(f"no matmul tiling for {(m, n, kt)}")


def _mm(a, b, *, name, add=None, trans_a=False, trans_b=False, out_dtype=F32, rides=()):
    assert not (trans_a and trans_b)
    if trans_a:
        kt, m = a.shape
    else:
        m, kt = a.shape
    n, kt2 = b.shape if trans_b else b.shape[::-1]
    assert kt == kt2, (a.shape, b.shape)
    tm, tn, tk = _mm_tiles(m, n, kt, trans_a, a.dtype.itemsize, b.dtype.itemsize,
                           jnp.dtype(out_dtype).itemsize, 0 if add is None else add.dtype.itemsize)
    nk = kt // tk
    grid = (m // tm, n // tn, nk)
    rd = _Rides(rides, [False] * len(rides))
    n_in = 2 if add is None else 3

    def body(*all_refs):
        ins, (o_ref,), scratch, handles = rd.split(all_refs, n_in, 1, 1 if nk > 1 else 0)
        refs = (*ins, o_ref, *scratch)
        if rides:
            ids = [pl.program_id(ax) for ax in range(3)]
            rd.run(handles, (ids[0] == 0) & (ids[1] == 0) & (ids[2] == 0),
                   (ids[0] == grid[0] - 1) & (ids[1] == grid[1] - 1) & (ids[2] == grid[2] - 1))
        if add is None:
            a_ref, b_ref, o_ref = refs[:3]
        else:
            a_ref, b_ref, add_ref, o_ref = refs[:4]
        k = pl.program_id(2)
        av = a_ref[...].astype(BF16)
        bv = b_ref[...].astype(BF16)
        dims = _DIMS["tn" if trans_a else "nt" if trans_b else "nn"]
        d = lax.dot_general(av, bv, (dims, ((), ())), preferred_element_type=F32)

        def finish(r):
            if add is not None:
                r = r + add_ref[...]
            o_ref[...] = r.astype(out_dtype)

        if nk == 1:
            finish(d)
        else:
            acc = refs[-1]

            @pl.when(k == 0)
            def _():
                acc[...] = d

            @pl.when((k > 0) & (k < nk - 1))
            def _():
                acc[...] += d

            @pl.when(k == nk - 1)
            def _():
                finish(acc[...] + d)

    a_spec = (pl.BlockSpec((tk, tm), lambda i, j, k: (k, i)) if trans_a
              else pl.BlockSpec((tm, tk), lambda i, j, k: (i, k)))
    b_spec = (pl.BlockSpec((tn, tk), lambda i, j, k: (j, k)) if trans_b
              else pl.BlockSpec((tk, tn), lambda i, j, k: (k, j)))
    in_specs = [a_spec, b_spec]
    args = [a, b]
    if add is not None:
        in_specs.append(pl.BlockSpec((tm, tn), lambda i, j, k: (i, j)))
        args.append(add)
    out = pl.pallas_call(
        body, name=name, grid=grid,
        in_specs=in_specs + rd.in_specs,
        out_specs=[pl.BlockSpec((tm, tn), lambda i, j, k: (i, j))] + rd.out_specs,
        out_shape=[jax.ShapeDtypeStruct((m, n), out_dtype)] + rd.out_shape,
        scratch_shapes=([pltpu.VMEM((tm, tn), F32)] if nk > 1 else []) + rd.scratch,
        compiler_params=_cp(*(("arbitrary",) * 3 if rides else ("parallel", "parallel", "arbitrary"))),
    )(*args, *rides)
    return out if rides else out[0]


def _rstd(x):
    return lax.rsqrt(jnp.mean(x * x, axis=-1, keepdims=True) + EPS)


def _norm_fwd(x, w, *, name):
    s = x.shape[0]
    tm = _tile(s, 512, 8)

    def body(x_ref, w_ref, o_ref):
        xv = x_ref[...]
        o_ref[...] = (xv * _rstd(xv) * w_ref[...]).astype(BF16)

    return pl.pallas_call(
        body, name=name, grid=(s // tm,),
        in_specs=[pl.BlockSpec((tm, D), lambda i: (i, 0)), pl.BlockSpec((1, D), lambda i: (0, 0))],
        out_specs=pl.BlockSpec((tm, D), lambda i: (i, 0)),
        out_shape=jax.ShapeDtypeStruct((s, D), BF16), compiler_params=_cp("parallel"),
    )(x, w)


def _norm_bwd_math(xv, wv, dy):
    r = _rstd(xv)
    xh = xv * r
    dxh = dy * wv
    dx = r * (dxh - xh * jnp.mean(dxh * xh, axis=-1, keepdims=True))
    dw = jnp.sum(dy * xh, axis=0, keepdims=True)
    return dx, dw


def _norm_bwd(x, w, dy, add, *, name):
    s = x.shape[0]
    tm = _tile(s, 256, 8)

    def body(*refs):
        if add is None:
            x_ref, w_ref, dy_ref, dx_ref, dw_ref = refs
        else:
            x_ref, w_ref, dy_ref, add_ref, dx_ref, dw_ref = refs

        @pl.when(pl.program_id(0) == 0)
        def _():
            dw_ref[...] = jnp.zeros_like(dw_ref)

        dx, dw = _norm_bwd_math(x_ref[...], w_ref[...], dy_ref[...])
        if add is not None:
            dx = dx + add_ref[...]
        dx_ref[...] = dx
        dw_ref[...] += dw

    row = pl.BlockSpec((tm, D), lambda i: (i, 0))
    vec = pl.BlockSpec((1, D), lambda i: (0, 0))
    in_specs = [row, vec, row] + ([row] if add is not None else [])
    args = [x, w, dy] + ([add] if add is not None else [])
    return pl.pallas_call(
        body, name=name, grid=(s // tm,), in_specs=in_specs, out_specs=[row, vec],
        out_shape=[jax.ShapeDtypeStruct((s, D), F32), jax.ShapeDtypeStruct((1, D), F32)],
        compiler_params=_cp("arbitrary"),
    )(*args)


def _final(x3, w, target, *, name):
    s = x3.shape[0]
    tm = _tile(s, 256, 8)

    def body(x_ref, w_ref, t_ref, dx_ref, dw_ref, loss_ref):
        @pl.when(pl.program_id(0) == 0)
        def _():
            dw_ref[...] = jnp.zeros_like(dw_ref)
            loss_ref[...] = jnp.zeros_like(loss_ref)

        xv = x_ref[...]
        wv = w_ref[...]
        y = xv * _rstd(xv) * wv
        err = y - t_ref[...]
        loss_ref[...] += 0.5 * jnp.sum(jnp.mean(err * err, axis=-1, keepdims=True))
        dx, dw = _norm_bwd_math(xv, wv, err * (1.0 / D))
        dx_ref[...] = dx
        dw_ref[...] += dw

    row = pl.BlockSpec((tm, D), lambda i: (i, 0))
    vec = pl.BlockSpec((1, D), lambda i: (0, 0))
    return pl.pallas_call(
        body, name=name, grid=(s // tm,), in_specs=[row, vec, row], out_specs=[row, vec, vec],
        out_shape=[jax.ShapeDtypeStruct((s, D), F32), jax.ShapeDtypeStruct((1, D), F32),
                   jax.ShapeDtypeStruct((1, D), F32)],
        compiler_params=_cp("arbitrary"),
    )(x3, w, target)


def _head_norm_math(o, w):
    lane = _iota((128, 128), 0) // 64
    bd = (lane == _iota((128, 128), 1) // 64).astype(F32)
    outs = []
    for op in _split(o, 128):
        ms = _dot("nn", op * op, bd, 2, 1) * (1.0 / 64)
        outs.append(op * lax.rsqrt(ms + EPS))
    return jnp.concatenate(outs, axis=1) * w


def _head_norm_fwd(o, w, *, name):
    s = o.shape[0]
    tm = _tile(s, 256, 8)

    def body(o_ref, w_ref, y_ref):
        y_ref[...] = _head_norm_math(o_ref[...], w_ref[...]).astype(BF16)

    row = pl.BlockSpec((tm, D), lambda i: (i, 0))
    vec = pl.BlockSpec((1, D), lambda i: (0, 0))
    return pl.pallas_call(
        body, name=name, grid=(s // tm,), in_specs=[row, vec], out_specs=row,
        out_shape=jax.ShapeDtypeStruct((s, D), BF16), compiler_params=_cp("parallel"),
    )(o, w)


def _head_norm_bwd(o, w, dymix, *, name):
    s = o.shape[0]
    tm = _tile(s, 256, 8)

    def body(o_ref, w_ref, dy_ref, do_ref, dw_ref):
        @pl.when(pl.program_id(0) == 0)
        def _():
            dw_ref[...] = jnp.zeros_like(dw_ref)

        _, vjp = jax.vjp(_head_norm_math, o_ref[...], w_ref[...])
        do, dw = vjp(dy_ref[...])
        do_ref[...] = do
        dw_ref[...] += dw

    row = pl.BlockSpec((tm, D), lambda i: (i, 0))
    vec = pl.BlockSpec((1, D), lambda i: (0, 0))
    return pl.pallas_call(
        body, name=name, grid=(s // tm,),
        in_specs=[row, vec, pl.BlockSpec((tm, D), lambda i: (i, 1))], out_specs=[row, vec],
        out_shape=[jax.ShapeDtypeStruct((s, D), F32), jax.ShapeDtypeStruct((1, D), F32)],
        compiler_params=_cp("arbitrary"),
    )(o, w, dymix)


SB_BQ = 256
SB_BK = 256


def _sb_consts():
    r = _iota((SB_BK, SB_BK), 0)
    c = _iota((SB_BK, SB_BK), 1)
    u_excl = (r > c).astype(BF16)
    u_incl = (r >= c).astype(BF16)
    return u_excl, u_incl


SB_LANES = 256
SB_NCH = SB_LANES // 64


def _nt(a, b):
    return lax.dot_general(a, b, (_DIMS["nt"], ((), ())), preferred_element_type=F32)


def _tn(a, b):
    return lax.dot_general(a, b, (_DIMS["tn"], ((), ())), preferred_element_type=F32)


def _nn(a, b):
    return jnp.dot(a, b, preferred_element_type=F32)


def _sb_heads(ref):
    out = []
    for hp in range(SB_LANES // 128):
        v = ref[:, 128 * hp:128 * (hp + 1)]
        first = _iota(v.shape, 1) < 64
        out += [jnp.where(first, v, 0).astype(BF16), jnp.where(first, 0, v).astype(BF16)]
    return out


SB_STRIP = 32


def _neg_abs(x):
    bits = lax.bitcast_convert_type(x, jnp.uint32) | jnp.uint32(0x80000000)
    return lax.bitcast_convert_type(bits, F32)


def _sb_block(ref, j):
    off = pl.multiple_of(j * SB_BK, SB_BK)
    return [ref[pl.ds(off, SB_BK), 128 * hp:128 * (hp + 1)] for hp in range(SB_NCH // 2)]


SB_DEAD = 104.0


def _sb_live(nlrun):
    m = nlrun[0]
    for x in nlrun[1:]:
        m = jnp.minimum(m, x)
    return jnp.min(m) < SB_DEAD


def _sb_strips():
    return [(r, pl.ds(r, SB_STRIP)) for r in range(0, SB_BQ, SB_STRIP)]


def _sb_diag_mask(r):
    return _iota((SB_STRIP, SB_BK), 1) < _iota((SB_STRIP, SB_BK), 0) + r


def _sb_soft(z, mask):
    e = jnp.exp(_neg_abs(z))
    nl = jnp.maximum(z, 0.0) + jnp.log(1.0 + e)
    if mask is not None:
        nl = jnp.where(mask, nl, 0.0)
    return e, nl


def _sb_split_to(hl_ref, rows, x):
    hi, lo = _terms(x, 2)
    hl_ref[rows, 0:SB_BK] = hi
    hl_ref[rows, SB_BK:2 * SB_BK] = lo


def _sb_stage_soft(z_ref, nl_ref, diag):
    for r, rows in _sb_strips():
        _, nl = _sb_soft(z_ref[rows, :], _sb_diag_mask(r) if diag else None)
        nl_ref[rows, 0:SB_BK] = nl.astype(BF16)


def _sb_stage_weights(z_ref, c_ref, a_ref, nlrun, diag):
    for r, rows in _sb_strips():
        a = jnp.exp(z_ref[rows, :] - c_ref[rows, :] - nlrun[r:r + SB_STRIP, :])
        if diag:
            a = jnp.where(_sb_diag_mask(r), a, 0.0)
        a_ref[rows, :] = a.astype(BF16)


def _sb_fwd(qkv, rides, *, name):
    s = qkv.shape[0]
    nq = s // SB_BQ
    ng = D // SB_LANES
    assert SB_BQ == SB_BK
    rd = _Rides(rides, [True] * len(rides))

    def body(*refs):
        (q_ref, k_ref, v_ref), (o_ref,), (zbuf, nlbuf, cbuf, abuf), handles = rd.split(refs, 3, 1, 4)
        i = pl.program_id(1)
        step_no = pl.program_id(0) * nq + i
        rd.run(handles, step_no == 0, step_no == ng * nq - 1)

        _, u_incl = _sb_consts()
        lane_a = _iota((SB_BQ, 128), 1) < 64
        qh = [q * 0.125 for q in _sb_heads(q_ref)]

        def tile(j, accs, nlrun, diag):
            kbs = _sb_block(k_ref, j)
            for c in range(SB_NCH):
                zbuf[c] = _nt(qh[c], kbs[c // 2])
            for c in range(SB_NCH):
                _sb_stage_soft(zbuf.at[c], nlbuf.at[c], diag)
                cbuf[c] = _nn(nlbuf[c], u_incl)
            for c in range(SB_NCH):
                _sb_stage_weights(zbuf.at[c], cbuf.at[c], abuf.at[c], nlrun[c], diag)
            nlrun = tuple(nlrun[c] + cbuf[c, :, 0:1] for c in range(SB_NCH))
            vbs = _sb_block(v_ref, j)
            outs = [_nn(abuf[c], vbs[c // 2]) for c in range(SB_NCH)]
            accs = tuple(acc + jnp.where(lane_a, outs[2 * hp], outs[2 * hp + 1]) for hp, acc in enumerate(accs))
            return accs, nlrun

        accs, nlrun = tile(i, (jnp.zeros((SB_BQ, 128), F32),) * (SB_NCH // 2),
                           (jnp.zeros((SB_BQ, 1), F32),) * SB_NCH, True)

        def step(carry):
            j, _, accs, nlrun = carry
            accs, nlrun = tile(j, accs, nlrun, False)
            return j - 1, _sb_live(nlrun), accs, nlrun

        _, _, accs, _ = lax.while_loop(lambda c: (c[0] >= 0) & c[1], step, (i - 1, _sb_live(nlrun), accs, nlrun))
        o_ref[...] = jnp.concatenate(accs, axis=1)

    return pl.pallas_call(
        body, name=name, grid=(ng, nq),
        in_specs=[pl.BlockSpec((SB_BQ, SB_LANES), lambda g, i: (i, g)),
                  pl.BlockSpec((s, SB_LANES), lambda g, i: (0, ng + g)),
                  pl.BlockSpec((s, SB_LANES), lambda g, i: (0, 2 * ng + g)), *rd.in_specs],
        out_specs=[pl.BlockSpec((SB_BQ, SB_LANES), lambda g, i: (i, g)), *rd.out_specs],
        out_shape=[jax.ShapeDtypeStruct((s, D), F32), *rd.out_shape],
        scratch_shapes=[pltpu.VMEM((SB_NCH, SB_BQ, SB_BK), F32), pltpu.VMEM((SB_NCH, SB_BQ, SB_BK), BF16),
                        pltpu.VMEM((SB_NCH, SB_BQ, SB_BK), F32), pltpu.VMEM((SB_NCH, SB_BQ, SB_BK), BF16),
                        *rd.scratch],
        compiler_params=_cp("arbitrary", "arbitrary"),
    )(qkv, qkv, qkv, *rides)


def _sb_bwd(qkv, o, do, rides, *, name):
    s = qkv.shape[0]
    nq = s // SB_BQ
    ng = D // SB_LANES
    nhp = SB_NCH // 2
    rd = _Rides(rides, [False] * len(rides))

    def body(*refs):
        ins, outs, scratch, handles = rd.split(refs, 5, 3, 11)
        q_ref, k_ref, v_ref, o_ref, do_ref = ins
        dq_ref, dk_hbm, dv_hbm = outs
        dk_acc, dv_acc, dk16, dv16, sems, zbuf, gbuf, hl, cbuf, abuf, dzbuf = scratch
        g_idx = pl.program_id(0)
        i = pl.program_id(1)
        step_no = g_idx * nq + i
        rd.run(handles, step_no == 0, step_no == ng * nq - 1)

        @pl.when(i == 0)
        def _():
            dk_acc[...] = jnp.zeros_like(dk_acc)
            dv_acc[...] = jnp.zeros_like(dv_acc)

        _, u_incl = _sb_consts()
        u2 = jnp.concatenate([u_incl, u_incl], axis=0)
        lane_a = _iota((SB_BQ, 128), 1) < 64
        lane_k = _iota((SB_BK, 128), 1) < 64
        qh = [q * 0.125 for q in _sb_heads(q_ref)]
        qf = [q_ref[:, 128 * hp:128 * (hp + 1)] for hp in range(nhp)]
        doh = _sb_heads(do_ref)
        dof = [do_ref[:, 128 * hp:128 * (hp + 1)].astype(BF16) for hp in range(nhp)]
        delta = []
        for hp in range(nhp):
            prod = dof[hp].astype(F32) * o_ref[:, 128 * hp:128 * (hp + 1)]
            delta += [jnp.sum(jnp.where(lane_a, prod, 0.0), axis=1, keepdims=True),
                      jnp.sum(jnp.where(lane_a, 0.0, prod), axis=1, keepdims=True)]

        def pre(slot, j):
            kbs = _sb_block(k_ref, j)
            vbs = _sb_block(v_ref, j)
            for c in range(SB_NCH):
                zbuf[slot, c] = _nt(qh[c], kbs[c // 2])
                gbuf[slot, c] = _nt(doh[c], vbs[c // 2])

        def stage_g(c, slot):
            for _, rows in _sb_strips():
                g = abuf[slot, c, rows, :].astype(F32) * gbuf[slot, c, rows, :]
                gbuf[slot, c, rows, :] = g
                _sb_split_to(hl.at[c], rows, g)

        def stage_dz(c, slot, grun, diag):
            for r, rows in _sb_strips():
                z = zbuf[slot, c, rows, :]
                g = gbuf[slot, c, rows, :]
                cs = (delta[c] - grun)[r:r + SB_STRIP, :] - cbuf[c, rows, :]
                sig = 1.0 / (1.0 + jnp.exp(-z))
                dz = g - (g + cs) * sig
                if diag:
                    dz = jnp.where(_sb_diag_mask(r), dz, 0.0)
                dzbuf[slot, c, rows, :] = dz.astype(BF16)

        def chain(slot, nlrun, grun, diag):
            for c in range(SB_NCH):
                _sb_stage_soft(zbuf.at[slot, c], hl.at[c], diag)
                cbuf[c] = _nn(hl[c, :, 0:SB_BK], u_incl)
            nl_tot = []
            for c in range(SB_NCH):
                _sb_stage_weights(zbuf.at[slot, c], cbuf.at[c], abuf.at[slot, c], nlrun[c], diag)
                nl_tot.append(cbuf[c, :, 0:1])
                stage_g(c, slot)
                cbuf[c] = _nn(hl[c], u2)
            g_tot = []
            for c in range(SB_NCH):
                stage_dz(c, slot, grun[c], diag)
                g_tot.append(cbuf[c, :, 0:1])
            return (tuple(a + b for a, b in zip(nlrun, nl_tot)), tuple(a + b for a, b in zip(grun, g_tot)))

        def post(slot, j, dqs):
            off = pl.multiple_of(j * SB_BK, SB_BK)
            kbs = _sb_block(k_ref, j)
            dq_t = [_nn(dzbuf[slot, c], kbs[c // 2]) for c in range(SB_NCH)]
            dk_t = [_tn(dzbuf[slot, c], qf[c // 2]) for c in range(SB_NCH)]
            dv_t = [_tn(abuf[slot, c], dof[c // 2]) for c in range(SB_NCH)]
            for hp in range(nhp):
                cols = slice(128 * hp, 128 * (hp + 1))
                dk_acc[pl.ds(off, SB_BK), cols] += 0.125 * jnp.where(lane_k, dk_t[2 * hp], dk_t[2 * hp + 1])
                dv_acc[pl.ds(off, SB_BK), cols] += jnp.where(lane_k, dv_t[2 * hp], dv_t[2 * hp + 1])
            return tuple(dq + jnp.where(lane_a, dq_t[2 * hp], dq_t[2 * hp + 1]) for hp, dq in enumerate(dqs))

        def tile(j, dqs, nlrun, grun, diag):
            pre(0, j)
            nlrun, grun = chain(0, nlrun, grun, diag)
            return post(0, j, dqs), nlrun, grun

        zero = (jnp.zeros((SB_BQ, 1), F32),) * SB_NCH
        dqs, nlrun, grun = tile(i, (jnp.zeros((SB_BQ, 128), F32),) * nhp, zero, zero, True)

        def step(carry):
            j, _, dqs, nlrun, grun = carry
            dqs, nlrun, grun = tile(j, dqs, nlrun, grun, False)
            return j - 1, _sb_live(nlrun), dqs, nlrun, grun

        carry = lax.while_loop(lambda c: (c[0] >= 0) & c[1], step, (i - 1, _sb_live(nlrun), dqs, nlrun, grun))
        dq_ref[...] = (0.125 * jnp.concatenate(carry[2], axis=1)).astype(BF16)

        def out_copies(g):
            cols = pl.ds(pl.multiple_of(g * SB_LANES, SB_LANES), SB_LANES)
            return (pltpu.make_async_copy(dk16, dk_hbm.at[:, cols], sems.at[0]),
                    pltpu.make_async_copy(dv16, dv_hbm.at[:, cols], sems.at[1]))

        @pl.when((i == nq - 1) & (g_idx > 0))
        def _():
            for cp in out_copies(g_idx - 1):
                cp.wait()

        @pl.when(i == nq - 1)
        def _():
            def narrow(r, carry):
                rows = pl.ds(pl.multiple_of(r * SB_BK, SB_BK), SB_BK)
                dk16[rows, :] = dk_acc[rows, :].astype(BF16)
                dv16[rows, :] = dv_acc[rows, :].astype(BF16)
                return carry

            lax.fori_loop(0, s // SB_BK, narrow, 0)
            for cp in out_copies(g_idx):
                cp.start()

        @pl.when((i == nq - 1) & (g_idx == ng - 1))
        def _():
            for cp in out_copies(g_idx):
                cp.wait()

    qblk = pl.BlockSpec((SB_BQ, SB_LANES), lambda g, i: (i, g))
    hbm = pl.BlockSpec(memory_space=pl.ANY)
    return pl.pallas_call(
        body, name=name, grid=(ng, nq),
        in_specs=[qblk, pl.BlockSpec((s, SB_LANES), lambda g, i: (0, ng + g)),
                  pl.BlockSpec((s, SB_LANES), lambda g, i: (0, 2 * ng + g)), qblk, qblk, *rd.in_specs],
        out_specs=[qblk, hbm, hbm, *rd.out_specs],
        out_shape=[jax.ShapeDtypeStruct((s, D), BF16)] * 3 + rd.out_shape,
        scratch_shapes=[pltpu.VMEM((s, SB_LANES), F32), pltpu.VMEM((s, SB_LANES), F32),
                        pltpu.VMEM((s, SB_LANES), BF16), pltpu.VMEM((s, SB_LANES), BF16),
                        pltpu.SemaphoreType.DMA((2,)),
                        pltpu.VMEM((1, SB_NCH, SB_BQ, SB_BK), F32), pltpu.VMEM((1, SB_NCH, SB_BQ, SB_BK), F32),
                        pltpu.VMEM((SB_NCH, SB_BQ, 2 * SB_BK), BF16), pltpu.VMEM((SB_NCH, SB_BQ, SB_BK), F32),
                        pltpu.VMEM((1, SB_NCH, SB_BQ, SB_BK), BF16), pltpu.VMEM((1, SB_NCH, SB_BQ, SB_BK), BF16),
                        *rd.scratch],
        compiler_params=_cp("arbitrary", "arbitrary"),
    )(qkv, qkv, qkv, o, do, *rides)


def _ssd_core(z, xpre, dtr, state, dtb, alog, dsk, nw):
    L = SSD_CHUNK
    xa = _silu(xpre)
    pieces = _split(xa, 128)
    xs = jnp.concatenate(pieces[:8], axis=1)
    bm, cm = pieces[8:10], pieces[10:12]
    dt = _softplus(dtr + dtb)
    a = dt * (-jnp.exp(alog))
    tri = (_iota((L, L), 0) >= _iota((L, L), 1)).astype(F32)
    a_cs = _dot("nn", tri, a, 1, 3)
    xc = xs * dt
    tril = _iota((L, L), 0) >= _iota((L, L), 1)
    lane_a = _iota((L, 128), 1) < 64
    acs_p = _split(a_cs, 128)
    xc_p = _split(xc, 128)
    ys, new_states = [], []
    for g in range(2):
        cb = _dot("nt", cm[g], bm[g])
        for pp in range(4):
            pair = 4 * g + pp
            acs = acs_p[pair]
            acs_t = acs.T
            xcp = xc_p[pair]
            st = state[pair]
            heads = []
            for hh in range(2):
                col = _take_fn(1, 64 * hh)(acs)
                row = _take_fn(0, 64 * hh)(acs_t)
                seg = col - row
                lm = jnp.where(tril, jnp.exp(jnp.where(tril, seg, 0.0)), 0.0)
                heads.append(_dot("nn", cb * lm, xcp))
            ydiag = jnp.where(lane_a, heads[0], heads[1])
            last = _take_fn(0, L - 1)(acs)
            snew = _dot("tn", xcp * jnp.exp(last - acs), bm[g])
            new_states.append(st * jnp.exp(_take_fn(1, L - 1)(acs_t)) + snew)
            yoff = _dot("nt", cm[g], st) * jnp.exp(acs)
            ys.append(ydiag + yoff)
    y = jnp.concatenate(ys, axis=1) + xs * dsk
    yg = y * _silu(z)
    outs = []
    for v in _split(yg, 512):
        outs.append(v * lax.rsqrt(jnp.mean(v * v, axis=-1, keepdims=True) + EPS))
    return jnp.concatenate(outs, axis=1) * nw, tuple(new_states)


XBC = 1536


def _ssd_conv(ext_ref, cw, cb):
    acc = cb
    for k in range(4):
        acc = acc + cw[k:k + 1, :] * ext_ref[pl.ds(HALO - 3 + k, SSD_CHUNK), :]
    return acc


def _ssd_fwd(z, xbc, dtr, cw, cb, lanes, rides, *, name):
    s = z.shape[0]
    L = SSD_CHUNK
    nc = s // L
    rd = _Rides(rides, [True] * len(rides))

    def body(*refs):
        ins, (y_ref, st_ref), (state, ext), handles = rd.split(refs, 7, 2, 2)
        z_ref, x_ref, h_ref, dtr_ref, cw_ref, cb_ref, ln_ref = ins
        c = pl.program_id(0)
        rd.run(handles, c == 0, c == nc - 1)

        @pl.when(c == 0)
        def _():
            state[...] = jnp.zeros_like(state)

        ext[0:HALO, :] = jnp.where(c == 0, 0.0, h_ref[...])
        ext[HALO:, :] = x_ref[...]
        xpre = _ssd_conv(ext, cw_ref[...], cb_ref[...])
        st_ref[0] = state[...]
        st_in = tuple(state[p] for p in range(8))
        yn, st_out = _ssd_core(z_ref[...], xpre, dtr_ref[...], st_in,
                               ln_ref[0:1, :], ln_ref[1:2, :], ln_ref[2:3, :], ln_ref[3:4, :])
        y_ref[...] = yn.astype(BF16)
        for p in range(8):
            state[p] = st_out[p]

    return pl.pallas_call(
        body, name=name, grid=(nc,),
        in_specs=[pl.BlockSpec((L, D), lambda c: (c, 0)),
                  pl.BlockSpec((L, XBC), lambda c: (c, 0)),
                  pl.BlockSpec((HALO, XBC), lambda c: (jnp.maximum(c * (L // HALO) - 1, 0), 0)),
                  pl.BlockSpec((L, D), lambda c: (c, 0)),
                  pl.BlockSpec((4, XBC), lambda c: (0, 0)),
                  pl.BlockSpec((1, XBC), lambda c: (0, 0)),
                  pl.BlockSpec((8, D), lambda c: (0, 0)), *rd.in_specs],
        out_specs=[pl.BlockSpec((L, D), lambda c: (c, 0)),
                   pl.BlockSpec((1, 8, 128, 128), lambda c: (c, 0, 0, 0)), *rd.out_specs],
        out_shape=[jax.ShapeDtypeStruct((s, D), BF16), jax.ShapeDtypeStruct((nc, 8, 128, 128), F32),
                   *rd.out_shape],
        scratch_shapes=[pltpu.VMEM((8, 128, 128), F32), pltpu.VMEM((L + HALO, XBC), F32), *rd.scratch],
        compiler_params=_cp("arbitrary"),
    )(z, xbc, xbc, dtr, cw, cb, lanes, *rides)


def _ssd_bwd(z, xbc, dtr, states, dymix, cw, cb, lanes, *, name):
    s = z.shape[0]
    L = SSD_CHUNK
    nc = s // L

    def body(z_ref, x_ref, h_ref, dtr_ref, st_ref, dy_ref, cw_ref, cb_ref, ln_ref,
             dz_ref, dx_ref, ddt_ref, dln_ref, dcv_ref, dstate, ext, dext):
        i = pl.program_id(0)
        c = nc - 1 - i

        @pl.when(i == 0)
        def _():
            dstate[...] = jnp.zeros_like(dstate)
            dext[...] = jnp.zeros_like(dext)
            dln_ref[...] = jnp.zeros_like(dln_ref)
            dcv_ref[...] = jnp.zeros_like(dcv_ref)

        ext[0:HALO, :] = jnp.where(c == 0, 0.0, h_ref[...])
        ext[HALO:, :] = x_ref[...]
        cwv = cw_ref[...]
        xpre = _ssd_conv(ext, cwv, cb_ref[...])
        st_in = tuple(st_ref[0, p] for p in range(8))
        _, vjp = jax.vjp(_ssd_core, z_ref[...], xpre, dtr_ref[...], st_in,
                         ln_ref[0:1, :], ln_ref[1:2, :], ln_ref[2:3, :], ln_ref[3:4, :])
        dz, dxpre, ddtr, dst, d0, d1, d2, d3 = vjp((dy_ref[...], tuple(dstate[p] for p in range(8))))
        for p in range(8):
            dstate[p] = dst[p]
        dz_ref[...] = dz.astype(BF16)
        ddt_ref[...] = ddtr.astype(BF16)
        dln_ref[0:4, :] += jnp.concatenate([d0, d1, d2, d3], axis=0)
        dext[0:L, :] = dxpre
        dx = jnp.zeros((L, XBC), F32)
        rows = []
        for k in range(4):
            dx = dx + cwv[k:k + 1, :] * dext[pl.ds(3 - k, L), :]
            rows.append(jnp.sum(dxpre * ext[pl.ds(HALO - 3 + k, L), :], axis=0, keepdims=True))
        rows.append(jnp.sum(dxpre, axis=0, keepdims=True))
        dx_ref[...] = dx.astype(BF16)
        dcv_ref[0:5, :] += jnp.concatenate(rows, axis=0)
        dext[L:L + HALO, :] = dxpre[0:HALO, :]

    rev = lambda i: (nc - 1 - i, 0)
    return pl.pallas_call(
        body, name=name, grid=(nc,),
        in_specs=[pl.BlockSpec((L, D), rev),
                  pl.BlockSpec((L, XBC), rev),
                  pl.BlockSpec((HALO, XBC), lambda i: (jnp.maximum((nc - 1 - i) * (L // HALO) - 1, 0), 0)),
                  pl.BlockSpec((L, D), rev),
                  pl.BlockSpec((1, 8, 128, 128), lambda i: (nc - 1 - i, 0, 0, 0)),
                  pl.BlockSpec((L, D), rev),
                  pl.BlockSpec((4, XBC), lambda i: (0, 0)),
                  pl.BlockSpec((1, XBC), lambda i: (0, 0)),
                  pl.BlockSpec((8, D), lambda i: (0, 0))],
        out_specs=[pl.BlockSpec((L, D), rev), pl.BlockSpec((L, XBC), rev), pl.BlockSpec((L, D), rev),
                   pl.BlockSpec((8, D), lambda i: (0, 0)), pl.BlockSpec((8, XBC), lambda i: (0, 0))],
        out_shape=[jax.ShapeDtypeStruct((s, D), BF16), jax.ShapeDtypeStruct((s, XBC), BF16),
                   jax.ShapeDtypeStruct((s, D), BF16), jax.ShapeDtypeStruct((8, D), F32),
                   jax.ShapeDtypeStruct((8, XBC), F32)],
        scratch_shapes=[pltpu.VMEM((8, 128, 128), F32), pltpu.VMEM((L + HALO, XBC), F32),
                        pltpu.VMEM((L + HALO, XBC), F32)],
        compiler_params=_cp("arbitrary"),
    )(z, xbc, xbc, dtr, states, dymix, cw, cb, lanes)


def _mem_attn_math(q, k, v):
    outs = []
    for qh, kh, vh in zip(_split(q, 256), _split(k, 256), _split(v, 256)):
        sc = _dot("nt", qh, kh) * (1.0 / 16.0)
        e = jnp.exp(sc - lax.stop_gradient(jnp.max(sc, axis=-1, keepdims=True)))
        p = e / jnp.sum(e, axis=-1, keepdims=True)
        outs.append(_dot("nn", p, vh))
    return jnp.concatenate(outs, axis=1)


def _mem_attn_fwd(q, k, v, *, name):
    s, m = q.shape[0], k.shape[0]
    tm = _tile(s, 256, 8)

    def body(q_ref, k_ref, v_ref, o_ref):
        o_ref[...] = _mem_attn_math(q_ref[...].astype(F32), k_ref[...].astype(F32),
                                    v_ref[...].astype(F32)).astype(BF16)

    row = pl.BlockSpec((tm, D), lambda i: (i, 0))
    kv = pl.BlockSpec((m, D), lambda i: (0, 0))
    return pl.pallas_call(
        body, name=name, grid=(s // tm,), in_specs=[row, kv, kv], out_specs=row,
        out_shape=jax.ShapeDtypeStruct((s, D), BF16), compiler_params=_cp("parallel"),
    )(q, k, v)


def _mem_attn_bwd(q, k, v, do, *, name):
    s, m = q.shape[0], k.shape[0]
    tm = _tile(s, 256, 8)

    def body(q_ref, k_ref, v_ref, do_ref, dq_ref, dk_ref, dv_ref):
        @pl.when(pl.program_id(0) == 0)
        def _():
            dk_ref[...] = jnp.zeros_like(dk_ref)
            dv_ref[...] = jnp.zeros_like(dv_ref)

        _, vjp = jax.vjp(_mem_attn_math, q_ref[...].astype(F32), k_ref[...].astype(F32),
                         v_ref[...].astype(F32))
        dq, dk, dv = vjp(do_ref[...])
        dq_ref[...] = dq.astype(BF16)
        dk_ref[...] += dk
        dv_ref[...] += dv

    row = pl.BlockSpec((tm, D), lambda i: (i, 0))
    kv = pl.BlockSpec((m, D), lambda i: (0, 0))
    return pl.pallas_call(
        body, name=name, grid=(s // tm,), in_specs=[row, kv, kv, row], out_specs=[row, kv, kv],
        out_shape=[jax.ShapeDtypeStruct((s, D), BF16), jax.ShapeDtypeStruct((m, D), F32),
                   jax.ShapeDtypeStruct((m, D), F32)],
        compiler_params=_cp("arbitrary"),
    )(q, k, v, do)


DFF = 2816
FFN_TC = 1408
FFN_TM = 256


FFN_CHUNKS = tuple((c, min(512, FFN_TC - c)) for c in range(0, FFN_TC, 512))


def _rows8(ref, r, cols):
    return ref[pl.ds(pl.multiple_of(r, HALO), HALO), cols]


def _shift_down(prev, cur, s):
    return jnp.where(_iota(cur.shape, 0) < s, pltpu.roll(prev, s, 0), pltpu.roll(cur, s, 0))


def _shift_up(cur, nxt, s):
    return jnp.where(_iota(cur.shape, 0) >= HALO - s, pltpu.roll(nxt, HALO - s, 0), pltpu.roll(cur, HALO - s, 0))


def _ffn_conv_strip(ext_ref, r, cols, cw, cb):
    prev, cur = _rows8(ext_ref, r, cols), _rows8(ext_ref, r + HALO, cols)
    return cb + cw[0:1, :] * _shift_down(prev, cur, 2) + cw[1:2, :] * _shift_down(prev, cur, 1) + cw[2:3, :] * cur


def _ffn_specs(s):
    tm, tc = FFN_TM, FFN_TC
    blk = pl.BlockSpec((tm, tc), lambda i, j: (i, j))
    halo = pl.BlockSpec((HALO, tc), lambda i, j: (jnp.maximum(i * (tm // HALO) - 1, 0), j))
    cw = pl.BlockSpec((3, tc), lambda i, j: (0, j))
    cb = pl.BlockSpec((1, tc), lambda i, j: (0, j))
    return tm, tc, blk, halo, cw, cb


def _glu_fwd(ug, uv, cwg, cwv, cbg, cbv, *, name):
    s = ug.shape[0]
    tm, tc, blk, halo, cw, cb = _ffn_specs(s)

    def body(g_ref, gh_ref, v_ref, vh_ref, cwg_ref, cwv_ref, cbg_ref, cbv_ref, f_ref, eg, ev):
        first = pl.program_id(0) == 0
        eg[0:HALO, :] = jnp.where(first, 0.0, gh_ref[...])
        eg[HALO:, :] = g_ref[...]
        ev[0:HALO, :] = jnp.where(first, 0.0, vh_ref[...])
        ev[HALO:, :] = v_ref[...]
        cwgv, cwvv, cbgv, cbvv = cwg_ref[...], cwv_ref[...], cbg_ref[...], cbv_ref[...]

        def step(t, carry):
            for c0, w in FFN_CHUNKS:
                cols = slice(c0, c0 + w)
                outs = []
                for h in range(2):
                    r = t * 16 + HALO * h
                    g = _ffn_conv_strip(eg, r, cols, cwgv[:, cols], cbgv[:, cols])
                    v = _ffn_conv_strip(ev, r, cols, cwvv[:, cols], cbvv[:, cols])
                    outs.append(_silu(g) * v)
                f_ref[pl.ds(pl.multiple_of(t * 16, 16), 16), cols] = jnp.concatenate(outs, axis=0).astype(BF16)
            return carry

        lax.fori_loop(0, tm // 16, step, 0)

    return pl.pallas_call(
        body, name=name, grid=(s // tm, DFF // tc),
        in_specs=[blk, halo, blk, halo, cw, cw, cb, cb], out_specs=blk,
        out_shape=jax.ShapeDtypeStruct((s, DFF), BF16),
        scratch_shapes=[pltpu.VMEM((tm + HALO, tc), F32)] * 2,
        compiler_params=_cp("parallel", "parallel"),
    )(ug, ug, uv, uv, cwg, cwv, cbg, cbv)


def _ffn_bwd(ug, uv, df, cwg, cwv, cbg, cbv, *, name):
    s = ug.shape[0]
    tm, tc = FFN_TM, FFN_TC
    nb = s // tm
    rows_ext = tm + HALO

    def body(g_ref, gp_ref, gn_ref, v_ref, vp_ref, vn_ref, df_ref, dfn_ref, cwg_ref, cwv_ref, cbg_ref, cbv_ref,
             dxg_ref, dxv_ref, dcg_ref, dcv_ref, eg, ev, edf, edg, edv, accg, accv):
        i = pl.program_id(1)
        first, last = i == 0, i == nb - 1

        @pl.when(first)
        def _():
            dcg_ref[...] = jnp.zeros_like(dcg_ref)
            dcv_ref[...] = jnp.zeros_like(dcv_ref)

        for e, prev, main, nxt in ((eg, gp_ref, g_ref, gn_ref), (ev, vp_ref, v_ref, vn_ref)):
            e[0:HALO, :] = jnp.where(first, 0.0, prev[...])
            e[HALO:HALO + tm, :] = main[...]
            e[HALO + tm:, :] = jnp.where(last, 0.0, nxt[...])
        edf[0:tm, :] = df_ref[...]
        edf[tm:, :] = jnp.where(last, 0.0, dfn_ref[...])
        accg[...] = jnp.zeros_like(accg)
        accv[...] = jnp.zeros_like(accv)
        cwgv, cwvv, cbgv, cbvv = cwg_ref[...], cwv_ref[...], cbg_ref[...], cbv_ref[...]

        def cotangents(t, carry):
            r = t * HALO
            for c0, w in FFN_CHUNKS:
                cols = slice(c0, c0 + w)
                g = _ffn_conv_strip(eg, r, cols, cwgv[:, cols], cbgv[:, cols])
                v = _ffn_conv_strip(ev, r, cols, cwvv[:, cols], cbvv[:, cols])
                dfs = _rows8(edf, r, cols)
                sg = _sigmoid(g)
                edv[pl.ds(pl.multiple_of(r, HALO), HALO), cols] = dfs * g * sg
                edg[pl.ds(pl.multiple_of(r, HALO), HALO), cols] = dfs * v * sg * (1.0 + g * (1.0 - sg))
            return carry

        lax.fori_loop(0, rows_ext // HALO, cotangents, 0)

        def conv_backward(t, carry):
            for c0, w in FFN_CHUNKS:
                cols = slice(c0, c0 + w)
                for edu, e, cw, dx_ref, acc in ((edg, eg, cwgv[:, cols], dxg_ref, accg),
                                                (edv, ev, cwvv[:, cols], dxv_ref, accv)):
                    dxs = []
                    for h in range(2):
                        r = t * 16 + HALO * h
                        cur, nxt = _rows8(edu, r, cols), _rows8(edu, r + HALO, cols)
                        up1, up2 = _shift_up(cur, nxt, 1), _shift_up(cur, nxt, 2)
                        x = _rows8(e, r + HALO, cols)
                        dxs.append(cw[2:3, :] * cur + cw[1:2, :] * up1 + cw[0:1, :] * up2)
                        acc[0, :, cols] += up2 * x
                        acc[1, :, cols] += up1 * x
                        acc[2, :, cols] += cur * x
                        acc[3, :, cols] += cur
                    dx_ref[pl.ds(pl.multiple_of(t * 16, 16), 16), cols] = jnp.concatenate(dxs, axis=0).astype(BF16)
            return carry

        lax.fori_loop(0, tm // 16, conv_backward, 0)
        for acc, dc_ref in ((accg, dcg_ref), (accv, dcv_ref)):
            dc_ref[0:4, :] += jnp.concatenate([jnp.sum(acc[k], axis=0, keepdims=True) for k in range(4)], axis=0)

    blk = pl.BlockSpec((tm, tc), lambda j, i: (i, j))
    nxt = pl.BlockSpec((HALO, tc), lambda j, i: (jnp.minimum((i + 1) * (tm // HALO), s // HALO - 1), j))
    prv = pl.BlockSpec((HALO, tc), lambda j, i: (jnp.maximum(i * (tm // HALO) - 1, 0), j))
    cw = pl.BlockSpec((3, tc), lambda j, i: (0, j))
    cb = pl.BlockSpec((1, tc), lambda j, i: (0, j))
    acc = pl.BlockSpec((8, tc), lambda j, i: (0, j))
    return pl.pallas_call(
        body, name=name, grid=(DFF // tc, nb),
        in_specs=[blk, prv, nxt, blk, prv, nxt, blk, nxt, cw, cw, cb, cb],
        out_specs=[blk, blk, acc, acc],
        out_shape=[jax.ShapeDtypeStruct((s, DFF), BF16)] * 2 + [jax.ShapeDtypeStruct((8, DFF), F32)] * 2,
        scratch_shapes=[pltpu.VMEM((tm + 2 * HALO, tc), F32)] * 2 + [pltpu.VMEM((rows_ext, tc), F32)] * 3
                       + [pltpu.VMEM((4, HALO, tc), F32)] * 2,
        compiler_params=_cp("parallel", "arbitrary"),
    )(ug, ug, ug, uv, uv, uv, df, df, cwg, cwv, cbg, cbv)


MESH = pl.DeviceIdType.MESH


def _all_gather(arrs, *, name):
    n = len(arrs)

    def body(*refs):
        x_refs, out_refs = refs[:n], refs[n:2 * n]
        send_sems, recv_sems, local_sems = refs[2 * n:]
        x, y, c = lax.axis_index("x"), lax.axis_index("y"), lax.axis_index("c")
        me, sibling = (x, y, c), (x, y, 1 - c)
        chips = [(1 - x, y), (x, 1 - y), (1 - x, 1 - y)]

        def blk(a, dev):
            return out_refs[a].at[4 * dev[0] + 2 * dev[1] + dev[2]]

        def copy(a, k, block, to, src=None):
            return pltpu.make_async_remote_copy(
                src_ref=blk(a, block) if src is None else src, dst_ref=blk(a, block),
                send_sem=send_sems.at[7 * a + k], recv_sem=recv_sems.at[7 * a + k],
                device_id=to, device_id_type=MESH)

        started = []
        mine = []
        for a in range(n):
            cp = pltpu.make_async_copy(x_refs[a], blk(a, me), local_sems.at[a])
            cp.start()
            mine.append(cp)
            first = [copy(a, 0, me, sibling, src=x_refs[a])]
            first += [copy(a, 1 + j, me, (*chip, c), src=x_refs[a]) for j, chip in enumerate(chips)]
            for cp in first:
                cp.start()
            started += first
        for a in range(n):
            for j, chip in enumerate(chips):
                copy(a, 1 + j, (*chip, c), me).wait_recv()
                fwd = copy(a, 4 + j, (*chip, c), sibling)
                fwd.start()
                started.append(fwd)
        for a in range(n):
            copy(a, 0, sibling, me).wait_recv()
            for j, chip in enumerate(chips):
                copy(a, 4 + j, (*chip, 1 - c), me).wait_recv()
        for cp in started:
            cp.wait_send()
        for cp in mine:
            cp.wait()

    any_spec = pl.BlockSpec(memory_space=pl.ANY)
    return pl.pallas_call(
        body, name=name,
        in_specs=[any_spec] * n, out_specs=[any_spec] * n,
        out_shape=[jax.ShapeDtypeStruct((NDEV,) + a.shape, a.dtype) for a in arrs],
        scratch_shapes=[pltpu.SemaphoreType.DMA((7 * n,)), pltpu.SemaphoreType.DMA((7 * n,)),
                        pltpu.SemaphoreType.DMA((n,))],
    )(*arrs)


class _Direct:
    SEMS = (pltpu.SemaphoreType.DMA((7,)), pltpu.SemaphoreType.DMA((7,)), pltpu.SemaphoreType.DMA((1,)))

    def __init__(self, src_ref, recv_ref, sems, gather):
        x, y, c = lax.axis_index("x"), lax.axis_index("y"), lax.axis_index("c")
        me = 4 * x + 2 * y + c
        send_sems, recv_sems, local_sem = sems
        src = (lambda pid: src_ref) if gather else (lambda pid: src_ref.at[pid])
        self.mine = pltpu.make_async_copy(src(me), recv_ref.at[me], local_sem.at[0])
        self.copies = []
        for k in range(1, NDEV):
            px = 1 - x if k & 4 else x
            py = 1 - y if k & 2 else y
            pc = 1 - c if k & 1 else c
            self.copies.append(pltpu.make_async_remote_copy(
                src_ref=src(4 * px + 2 * py + pc), dst_ref=recv_ref.at[me],
                send_sem=send_sems.at[k - 1], recv_sem=recv_sems.at[k - 1],
                device_id=(px, py, pc), device_id_type=MESH))

    def start(self):
        self.mine.start()
        for cp in self.copies:
            cp.start()

    def wait(self):
        for cp in self.copies:
            cp.wait_recv()
        for cp in self.copies:
            cp.wait_send()
        self.mine.wait()


def _recv_shape(src, gather):
    return jax.ShapeDtypeStruct(((NDEV,) + src.shape) if gather else src.shape, src.dtype)


class _Rides:
    def __init__(self, rides, gathers):
        self.n = len(rides)
        self.gathers = list(gathers)
        any_spec = pl.BlockSpec(memory_space=pl.ANY)
        self.in_specs = [any_spec] * self.n
        self.out_specs = [any_spec] * self.n
        self.out_shape = [_recv_shape(a, g) for a, g in zip(rides, gathers)]
        self.scratch = list(_Direct.SEMS) * self.n

    def split(self, refs, n_in, n_out, n_scratch):
        n = self.n
        ins, refs = refs[:n_in], refs[n_in:]
        rides, refs = refs[:n], refs[n:]
        outs, refs = refs[:n_out], refs[n_out:]
        gots, refs = refs[:n], refs[n:]
        scratch, sems = refs[:n_scratch], refs[n_scratch:]
        return ins, outs, scratch, (rides, gots, sems)

    def run(self, handles, first, last):
        rides, gots, sems = handles

        def all_of():
            return [_Direct(rides[a], gots[a], sems[3 * a:3 * a + 3], self.gathers[a]) for a in range(self.n)]

        @pl.when(first)
        def _():
            for e in all_of():
                e.start()

        @pl.when(last)
        def _():
            for e in all_of():
                e.wait()


def _exchange(arrs, gathers, *, name):
    rd = _Rides(arrs, gathers)

    def body(*refs):
        _, _, _, handles = rd.split(refs, 0, 0, 0)
        rd.run(handles, True, True)

    return pl.pallas_call(
        body, name=name, in_specs=rd.in_specs, out_specs=rd.out_specs, out_shape=rd.out_shape,
        scratch_shapes=rd.scratch,
    )(*arrs)


def _adamw(parts, w, m, v, *, name):
    r, cols = w.shape
    tm = _tile(r, 256, PACK_ALIGN)
    c1 = 1.0 - ADAM_B1 ** ADAM_STEP
    c2 = 1.0 - ADAM_B2 ** ADAM_STEP

    def body(p_ref, w_ref, m_ref, v_ref, g_ref, d_ref, nm_ref, nv_ref):
        g = p_ref[0].astype(F32)
        for i in range(1, NDEV):
            g = g + p_ref[i].astype(F32)
        nm = ADAM_B1 * m_ref[...] + (1.0 - ADAM_B1) * g
        nv = ADAM_B2 * v_ref[...] + (1.0 - ADAM_B2) * (g * g)
        d_ref[...] = -ADAM_LR * ((nm / c1) / (jnp.sqrt(nv / c2) + ADAM_EPS) + ADAM_WD * w_ref[...])
        g_ref[...] = g
        nm_ref[...] = nm
        nv_ref[...] = nv

    row = pl.BlockSpec((tm, cols), lambda i: (i, 0))
    return pl.pallas_call(
        body, name=name, grid=(r // tm,),
        in_specs=[pl.BlockSpec((NDEV, tm, cols), lambda i: (0, i, 0)), row, row, row],
        out_specs=[row] * 4, out_shape=[jax.ShapeDtypeStruct((r, cols), F32)] * 4,
        compiler_params=_cp("parallel"),
    )(parts, w, m, v)


PACK_ALIGN = 16


def _part_rows(shape):
    n = -(-math.prod(shape) // D)
    return n + (-n) % PACK_ALIGN


def _rows(a):
    flat = a.reshape(-1)
    pad = _part_rows(a.shape) * D - flat.shape[0]
    if pad:
        flat = jnp.concatenate([flat, jnp.zeros((pad,), flat.dtype)])
    return flat.reshape(-1, D)


def _pack(parts, total_rows):
    if all(math.prod(p.shape) % (PACK_ALIGN * D) for p in parts):
        return _pack_small(parts, total_rows)
    rows = [_rows(p) for p in parts]
    used = sum(r.shape[0] for r in rows)
    if total_rows > used:
        rows.append(jnp.zeros((total_rows - used, D), rows[0].dtype))
    return jnp.concatenate(rows, axis=0)


def _pack_small(parts, total_rows):
    flat, used = [], 0
    for p in parts:
        n, nr = math.prod(p.shape), _part_rows(p.shape)
        flat += [p.reshape(-1), jnp.zeros((nr * D - n,), p.dtype)]
        used += nr
    flat.append(jnp.zeros(((total_rows - used) * D,), parts[0].dtype))
    return jnp.concatenate(flat).reshape(total_rows, D)


def _unpack(buf, shapes, part_rows=_part_rows):
    out, r0 = [], 0
    for shp in shapes:
        n = math.prod(shp)
        out.append(buf[r0:r0 + part_rows(shp)].reshape(-1)[:n].reshape(shp))
        r0 += part_rows(shp)
    return out


def _tight_rows(shape):
    return -(-math.prod(shape) // D)


def _pack_tight(parts, total_rows):
    flat, used = [], 0
    for p in parts:
        n, nr = math.prod(p.shape), _tight_rows(p.shape)
        flat += [p.reshape(-1), jnp.zeros((nr * D - n,), p.dtype)]
        used += nr
    flat.append(jnp.zeros(((total_rows - used) * D,), parts[0].dtype))
    return jnp.concatenate(flat).reshape(total_rows, D)


SHARD = {"w_in": (D, 706), "w_out": (256, D), "w_mq": (128, D), "w_mk": (128, D), "w_mv": (128, D),
         "w_mo": (128, D), "w_up": (D, 704), "w_down": (352, D), "conv_ssd_w": (4, 192), "conv_ffn_w": (3, 704)}
GATHER_MID = ["w_out", "w_mq", "w_mk", "w_mv", "w_mo"]
GATHER_FFN = ["w_down"]
CONV_TAPS = ["conv_ssd_w", "conv_ffn_w"]
GRADS_PACKED = ["w_out", "w_mq", "w_mk", "w_mv", "w_mo", "w_down", "conv_ffn_w", "conv_ssd_w"]


def _layout(names):
    row0, r = {}, 0
    for n in names:
        row0[n] = r
        r += _part_rows(SHARD[n])
    return row0, r + (-r) % 128


SMALL = [("norm_mix_w", (1, D)), ("conv_ssd_b", (1, 1536)), ("dt_bias", (1, 16)), ("a_log", (1, 16)),
         ("d_skip", (1, 16)), ("ssd_norm_w", (1, D)), ("sb_norm_w", (1, D)), ("norm_mem_w", (1, D)),
         ("norm_memkv_w", (1, D)), ("norm_ffn_w", (1, D)), ("conv_ffn_b", (1, 5632)), ("norm_final_w", (D,))]
LOSS_ROW = sum(_tight_rows(_shp) for _, _shp in SMALL)
SMALL_ROWS = LOSS_ROW + 1 + (-(LOSS_ROW + 1)) % 8
ORDER = ["norm_mix_w", "w_in", "conv_ssd_w", "conv_ssd_b", "dt_bias", "a_log", "d_skip", "ssd_norm_w",
         "sb_norm_w", "w_out", "norm_mem_w", "norm_memkv_w", "w_mq", "w_mk", "w_mv", "w_mo", "norm_ffn_w",
         "w_up", "conv_ffn_w", "conv_ffn_b", "w_down", "norm_final_w"]


def _pad_rows(a, nr):
    n = a.shape[1]
    return jnp.concatenate([a, jnp.zeros((NDEV, nr * D - n), a.dtype)], axis=1).reshape(NDEV, nr, D)


def _group_sum(lanes):
    return lanes.reshape(16, 64).sum(axis=1).reshape(1, 16)


def kernel(x, mem, norm_mix_w, w_in, conv_ssd_w, conv_ssd_b, dt_bias, a_log, d_skip, ssd_norm_w, sb_norm_w, w_out, norm_mem_w, norm_memkv_w, w_mq, w_mk, w_mv, w_mo, norm_ffn_w, w_up, conv_ffn_w, conv_ffn_b, w_down, norm_final_w, loss_target, m_norm_mix_w, m_w_in, m_conv_ssd_w, m_conv_ssd_b, m_dt_bias, m_a_log, m_d_skip, m_ssd_norm_w, m_sb_norm_w, m_w_out, m_norm_mem_w, m_norm_memkv_w, m_w_mq, m_w_mk, m_w_mv, m_w_mo, m_norm_ffn_w, m_w_up, m_conv_ffn_w, m_conv_ffn_b, m_w_down, m_norm_final_w, v_norm_mix_w, v_w_in, v_conv_ssd_w, v_conv_ssd_b, v_dt_bias, v_a_log, v_d_skip, v_ssd_norm_w, v_sb_norm_w, v_w_out, v_norm_mem_w, v_norm_memkv_w, v_w_mq, v_w_mk, v_w_mv, v_w_mo, v_norm_ffn_w, v_w_up, v_conv_ffn_w, v_conv_ffn_b, v_w_down, v_norm_final_w):
    P = dict(norm_mix_w=norm_mix_w, w_in=w_in, conv_ssd_w=conv_ssd_w, conv_ssd_b=conv_ssd_b, dt_bias=dt_bias, a_log=a_log, d_skip=d_skip, ssd_norm_w=ssd_norm_w, sb_norm_w=sb_norm_w, w_out=w_out, norm_mem_w=norm_mem_w, norm_memkv_w=norm_memkv_w, w_mq=w_mq, w_mk=w_mk, w_mv=w_mv, w_mo=w_mo, norm_ffn_w=norm_ffn_w, w_up=w_up, conv_ffn_w=conv_ffn_w, conv_ffn_b=conv_ffn_b, w_down=w_down, norm_final_w=norm_final_w)
    M = dict(norm_mix_w=m_norm_mix_w, w_in=m_w_in, conv_ssd_w=m_conv_ssd_w, conv_ssd_b=m_conv_ssd_b, dt_bias=m_dt_bias, a_log=m_a_log, d_skip=m_d_skip, ssd_norm_w=m_ssd_norm_w, sb_norm_w=m_sb_norm_w, w_out=m_w_out, norm_mem_w=m_norm_mem_w, norm_memkv_w=m_norm_memkv_w, w_mq=m_w_mq, w_mk=m_w_mk, w_mv=m_w_mv, w_mo=m_w_mo, norm_ffn_w=m_norm_ffn_w, w_up=m_w_up, conv_ffn_w=m_conv_ffn_w, conv_ffn_b=m_conv_ffn_b, w_down=m_w_down, norm_final_w=m_norm_final_w)
    V = dict(norm_mix_w=v_norm_mix_w, w_in=v_w_in, conv_ssd_w=v_conv_ssd_w, conv_ssd_b=v_conv_ssd_b, dt_bias=v_dt_bias, a_log=v_a_log, d_skip=v_d_skip, ssd_norm_w=v_ssd_norm_w, sb_norm_w=v_sb_norm_w, w_out=v_w_out, norm_mem_w=v_norm_mem_w, norm_memkv_w=v_norm_memkv_w, w_mq=v_w_mq, w_mk=v_w_mk, w_mv=v_w_mv, w_mo=v_w_mo, norm_ffn_w=v_norm_ffn_w, w_up=v_w_up, conv_ffn_w=v_conv_ffn_w, conv_ffn_b=v_conv_ffn_b, w_down=v_w_down, norm_final_w=v_norm_final_w)
    small_shapes = [shp for _, shp in SMALL]

    def packed(src, names, dtype=F32):
        return _pack([src[n][0] for n in names], _layout(names)[1]).astype(dtype)

    def columns(g):
        return g.transpose(1, 0, 2).reshape(g.shape[1], NDEV * g.shape[2])

    g_in, g_taps = _all_gather([w_in[0].astype(BF16), packed(P, CONV_TAPS)], name="gather_w_in")
    W_in = columns(g_in)
    cw_ssd = g_taps[:, 0].reshape(NDEV, -1)[:, :768].reshape(NDEV, 4, 192).transpose(1, 0, 2).reshape(4, XBC)
    cw_ffn = (g_taps[:, PACK_ALIGN:PACK_ALIGN + 3].reshape(NDEV, -1)[:, :2112].reshape(NDEV, 3, 704)
              .transpose(1, 0, 2).reshape(3, 2 * DFF))
    W_z, W_xbc, W_dt, W_qkv = W_in[:, :D], W_in[:, D:D + XBC], W_in[:, D + XBC:D + XBC + 16], W_in[:, D + XBC + 16:]
    W_dtr = jnp.repeat(W_dt, 64, axis=1)
    cwg, cwv = cw_ffn[:, :DFF], cw_ffn[:, DFF:]
    cbg, cbv = conv_ffn_b[:, :DFF], conv_ffn_b[:, DFF:]
    rep = lambda p: jnp.repeat(p, 64, axis=1)
    lanes = jnp.concatenate([rep(dt_bias), rep(a_log), rep(d_skip), ssd_norm_w, jnp.zeros((4, D), F32)], axis=0)

    xs, tgt, mm = x[0], loss_target[0], mem[0]

    h1 = _norm_fwd(xs, norm_mix_w, name="norm_mix")
    z = _mm(h1, W_z, name="proj_z")
    xbc = _mm(h1, W_xbc, name="proj_xbc")
    dtr = _mm(h1, W_dtr, name="proj_dt")
    qkv = _mm(h1, W_qkv, name="proj_qkv", out_dtype=BF16)
    y_ssd, states, g_ffn = _ssd_fwd(z, xbc, dtr, cw_ssd, conv_ssd_b, lanes, [packed(P, GATHER_FFN, BF16)],
                                    name="ssd_fwd")
    o_sb, g_mid, g_up = _sb_fwd(qkv, [packed(P, GATHER_MID, BF16), w_up[0].astype(BF16)], name="sb_fwd")
    r_mid = _layout(GATHER_MID)[0]
    W_out = g_mid[:, r_mid["w_out"]:r_mid["w_out"] + 256].reshape(2 * D, D)
    W_mq, W_mk, W_mv, W_mo = [g_mid[:, r_mid[n]:r_mid[n] + 128].reshape(D, D)
                              for n in ("w_mq", "w_mk", "w_mv", "w_mo")]
    W_up = columns(g_up)
    W_down = g_ffn[:, 0:352].reshape(DFF, D)
    W_upg, W_upv = W_up[:, :DFF], W_up[:, DFF:]
    y_sb = _head_norm_fwd(o_sb, sb_norm_w, name="sb_norm")
    ymix = jnp.concatenate([y_ssd, y_sb], axis=1)
    x1 = _mm(ymix, W_out, add=xs, name="proj_out")
    h2 = _norm_fwd(x1, norm_mem_w, name="norm_mem")
    mn = _norm_fwd(mm, norm_memkv_w, name="norm_memkv")
    qm = _mm(h2, W_mq, name="mem_q", out_dtype=BF16)
    km = _mm(mn, W_mk, name="mem_k", out_dtype=BF16)
    vm = _mm(mn, W_mv, name="mem_v", out_dtype=BF16)
    om = _mem_attn_fwd(qm, km, vm, name="mem_attn")
    x2 = _mm(om, W_mo, add=x1, name="mem_o")
    h3 = _norm_fwd(x2, norm_ffn_w, name="norm_ffn")
    ug = _mm(h3, W_upg, name="ffn_up_g")
    uv = _mm(h3, W_upv, name="ffn_up_v")
    f = _glu_fwd(ug, uv, cwg, cwv, cbg, cbv, name="ffn_glu")
    x3 = _mm(f, W_down, add=x2, name="ffn_down")
    dx3, g_nfinal, loss_part = _final(x3, norm_final_w.reshape(1, D), tgt, name="final_loss")

    G = {}
    G["w_down"] = _mm(f, dx3, trans_a=True, name="g_w_down")
    df = _mm(dx3, W_down, trans_b=True, name="d_f")
    dupg, dupv, dcg, dcv = _ffn_bwd(ug, uv, df, cwg, cwv, cbg, cbv, name="ffn_glu_bwd")
    G["w_up"] = jnp.concatenate([_mm(h3, dupg, trans_a=True, name="g_w_up_g"),
                                 _mm(h3, dupv, trans_a=True, name="g_w_up_v")], axis=1)
    G["conv_ffn_w"] = jnp.concatenate([dcg[0:3], dcv[0:3]], axis=1)
    G["conv_ffn_b"] = jnp.concatenate([dcg[3:4], dcv[3:4]], axis=1)
    dh3 = _mm(dupg, W_upg, trans_b=True, name="d_h3_g")
    dh3 = _mm(dupv, W_upv, trans_b=True, add=dh3, name="d_h3_v")
    dx2, G["norm_ffn_w"] = _norm_bwd(x2, norm_ffn_w, dh3, dx3, name="norm_ffn_bwd")
    G["w_mo"] = _mm(om, dx2, trans_a=True, name="g_w_mo")
    dom = _mm(dx2, W_mo, trans_b=True, name="d_om")
    dqm, dkm, dvm = _mem_attn_bwd(qm, km, vm, dom, name="mem_attn_bwd")
    G["w_mq"] = _mm(h2, dqm, trans_a=True, name="g_w_mq")
    G["w_mk"] = _mm(mn, dkm, trans_a=True, name="g_w_mk")
    G["w_mv"] = _mm(mn, dvm, trans_a=True, name="g_w_mv")
    dh2 = _mm(dqm, W_mq, trans_b=True, name="d_h2")
    dmn = _mm(dkm, W_mk, trans_b=True, name="d_mn_k")
    dmn = _mm(dvm, W_mv, trans_b=True, add=dmn, name="d_mn_v")
    _, G["norm_memkv_w"] = _norm_bwd(mm, norm_memkv_w, dmn, None, name="norm_memkv_bwd")
    dx1, G["norm_mem_w"] = _norm_bwd(x1, norm_mem_w, dh2, dx2, name="norm_mem_bwd")
    G["w_out"] = _mm(ymix, dx1, trans_a=True, name="g_w_out")
    dymix = _mm(dx1, W_out, trans_b=True, name="d_ymix")
    do_sb, G["sb_norm_w"] = _head_norm_bwd(o_sb, sb_norm_w, dymix, name="sb_norm_bwd")

    dz, dxbc, ddtr, dlanes, dconv = _ssd_bwd(z, xbc, dtr, states, dymix, cw_ssd, conv_ssd_b, lanes, name="ssd_bwd")
    G["dt_bias"], G["a_log"], G["d_skip"] = [_group_sum(dlanes[i:i + 1]) for i in range(3)]
    G["ssd_norm_w"] = dlanes[3:4]
    G["conv_ssd_w"], G["conv_ssd_b"] = dconv[0:4], dconv[4:5]

    def col_slabs(g, cols):
        return g.reshape(g.shape[0], NDEV, cols).transpose(1, 0, 2).astype(BF16)

    def packed_slabs(names):
        parts = []
        for n in names:
            shp = SHARD[n]
            if shp[-1] == D:
                t = G[n].reshape((NDEV,) + shp)
            else:
                t = _pad_rows(G[n].reshape(shp[0], NDEV, shp[1]).transpose(1, 0, 2).reshape(NDEV, -1),
                              _part_rows(shp))
            parts.append(jnp.pad(t, ((0, 0), (0, _part_rows(shp) - t.shape[1]), (0, 0))))
        used = sum(t.shape[1] for t in parts)
        parts.append(jnp.zeros((NDEV, _layout(names)[1] - used, D), F32))
        return jnp.concatenate(parts, axis=1).astype(BF16)

    dq, dk, dv, recv_packed, recv_up = _sb_bwd(qkv, o_sb, do_sb, [packed_slabs(GRADS_PACKED), col_slabs(G["w_up"], 704)],
                                               name="sb_bwd")
    dqkv = jnp.concatenate([dq, dk, dv], axis=1)
    g_wdt = _mm(h1, ddtr, trans_a=True, name="g_w_dt").reshape(D, 16, 64).sum(axis=2)
    G["w_in"] = jnp.concatenate([_mm(h1, dz, trans_a=True, name="g_w_z"),
                                 _mm(h1, dxbc, trans_a=True, name="g_w_xbc"), g_wdt,
                                 _mm(h1, dqkv, trans_a=True, name="g_w_qkv")], axis=1)
    dh1 = _mm(dz, W_z, trans_b=True, name="d_h1_z")
    dh1 = _mm(dxbc, W_xbc, trans_b=True, add=dh1, name="d_h1_xbc")
    dh1 = _mm(ddtr, W_dtr, trans_b=True, add=dh1, name="d_h1_dt")
    dh1, recv_in = _mm(dqkv, W_qkv, trans_b=True, add=dh1, rides=[col_slabs(G["w_in"], 706)], name="d_h1_qkv")
    dx, G["norm_mix_w"] = _norm_bwd(xs, norm_mix_w, dh1, dx1, name="norm_mix_bwd")
    G["norm_final_w"] = g_nfinal.reshape(D)

    small_g = _pack_tight([G[n] for n, _ in SMALL] + [loss_part], SMALL_ROWS)
    (parts_small,) = _exchange([small_g], [True], name="exchange_grads")
    outs_packed = _adamw(recv_packed, packed(P, GRADS_PACKED), packed(M, GRADS_PACKED), packed(V, GRADS_PACKED),
                         name="adamw_packed")
    outs_up = _adamw(recv_up, w_up[0], m_w_up[0], v_w_up[0], name="adamw_w_up")
    outs_in = _adamw(recv_in, w_in[0], m_w_in[0], v_w_in[0], name="adamw_w_in")
    outs_small = _adamw(parts_small, _pack_tight([P[n] for n, _ in SMALL], SMALL_ROWS),
                        _pack_tight([M[n] for n, _ in SMALL], SMALL_ROWS),
                        _pack_tight([V[n] for n, _ in SMALL], SMALL_ROWS), name="adamw_replicated")

    res = {}
    for i, kind in enumerate(("grad", "delta", "new_m", "new_v")):
        for n, val in zip(GRADS_PACKED, _unpack(outs_packed[i], [SHARD[n] for n in GRADS_PACKED])):
            res[kind, n] = val.reshape((1,) + SHARD[n])
        res[kind, "w_up"] = outs_up[i].reshape((1,) + SHARD["w_up"])
        res[kind, "w_in"] = outs_in[i].reshape((1,) + SHARD["w_in"])
        for (n, shp), val in zip(SMALL, _unpack(outs_small[i], small_shapes, _tight_rows)):
            res[kind, n] = val
    loss = outs_small[0][LOSS_ROW, 0]
    out = [loss, dx.reshape(1, -1, D)]
    for kind in ("grad", "delta", "new_m", "new_v"):
        out += [res[kind, n] for n in ORDER]
    return tuple(out)
```

```python
import functools
import math

import jax
import jax.numpy as jnp
from jax import lax
from jax.experimental import pallas as pl
from jax.experimental.pallas import tpu as pltpu

F32 = jnp.float32
BF16 = jnp.bfloat16

D = 1024
NDEV = 8
EPS = 1e-6
SSD_CHUNK = 128
HALO = 8
VMEM_LIMIT = 56 * 2**20

ADAM_LR, ADAM_B1, ADAM_B2, ADAM_EPS, ADAM_WD, ADAM_STEP = 0.001, 0.9, 0.999, 1e-08, 0.01, 10


def _cp(*sem):
    return pltpu.CompilerParams(dimension_semantics=sem, vmem_limit_bytes=VMEM_LIMIT)


def _tile(n, cap, mult):
    if n <= cap:
        return n
    for d in range(cap - cap % mult, 0, -mult):
        if n % d == 0:
            return d
    raise ValueError(f"no tile for {n}")


def _sigmoid(x):
    return 1.0 / (1.0 + jnp.exp(-x))


def _silu(x):
    return x * _sigmoid(x)


def _softplus(x):
    return jnp.maximum(x, 0.0) + jnp.log(1.0 + jnp.exp(-jnp.abs(x)))


def _terms(x, n):
    out = []
    r = x.astype(F32)
    for i in range(n):
        h = r.astype(BF16)
        out.append(h)
        if i + 1 < n:
            r = r - h.astype(F32)
    return out


_DIMS = {"nn": ((1,), (0,)), "nt": ((1,), (1,)), "tn": ((0,), (0,))}


def _dot_raw(form, a, b, ta, tb):
    acc = None
    for ai in _terms(a, ta):
        for bi in _terms(b, tb):
            d = lax.dot_general(ai, bi, (_DIMS[form], ((), ())), preferred_element_type=F32)
            acc = d if acc is None else acc + d
    return acc


@functools.lru_cache(maxsize=None)
def _dot_fn(form, ta, tb):
    @jax.custom_vjp
    def f(a, b):
        return _dot_raw(form, a, b, ta, tb)

    def fwd(a, b):
        return f(a, b), (a, b)

    def bwd(res, ct):
        a, b = res
        if form == "nn":
            return _dot_fn("nt", ta, tb)(ct, b), _dot_fn("tn", ta, tb)(a, ct)
        if form == "nt":
            return _dot_fn("nn", ta, tb)(ct, b), _dot_fn("tn", tb, ta)(ct, a)
        return _dot_fn("nt", tb, ta)(b, ct), _dot_fn("nn", ta, tb)(a, ct)

    f.defvjp(fwd, bwd)
    return f


def _dot(form, a, b, ta=1, tb=1):
    return _dot_fn(form, ta, tb)(a, b)


@functools.lru_cache(maxsize=None)
def _take_fn(axis, idx):
    @jax.custom_vjp
    def f(x):
        return x[:, idx:idx + 1] if axis == 1 else x[idx:idx + 1, :]

    def fwd(x):
        return f(x), x.shape

    def bwd(shape, ct):
        io = lax.broadcasted_iota(jnp.int32, shape, axis)
        return (jnp.where(io == idx, jnp.broadcast_to(ct, shape), 0.0),)

    f.defvjp(fwd, bwd)
    return f


@functools.lru_cache(maxsize=None)
def _split_fn(width, n):
    @jax.custom_vjp
    def f(x):
        return tuple(x[:, i * width:(i + 1) * width] for i in range(n))

    def fwd(x):
        return f(x), None

    def bwd(_, cts):
        return (jnp.concatenate(list(cts), axis=1),)

    f.defvjp(fwd, bwd)
    return f


def _split(x, width):
    return _split_fn(width, x.shape[1] // width)(x)


def _iota(shape, axis):
    return lax.broadcasted_iota(jnp.int32, shape, axis)


MM_VMEM_BUDGET = 44 * 2**20


def _mm_tiles(m, n, kt, trans_a, a_bytes, b_bytes, out_bytes, add_bytes):
    tn = _tile(n, 1536, 128)
    for tm_cap in (1408, 1024, 512, 256, 128):
        tm = _tile(m, tm_cap, 128 if trans_a else 8)
        for tk_cap in (kt, 4096, 2048, 1024, 512):
            tk = _tile(kt, tk_cap, 128)
            blocks = tm * tk * a_bytes + tk * tn * b_bytes + tm * tn * (out_bytes + add_bytes)
            if 2 * blocks + (tm * tn * 4 if tk < kt else 0) <= MM_VMEM_BUDGET:
                return tm, tn, tk
    raise ValueError(f"no matmul tiling for {(m, n, kt)}")


def _mm(a, b, *, name, add=None, trans_a=False, trans_b=False, out_dtype=F32, rides=()):
    assert not (trans_a and trans_b)
    if trans_a:
        kt, m = a.shape
    else:
        m, kt = a.shape
    n, kt2 = b.shape if trans_b else b.shape[::-1]
    assert kt == kt2, (a.shape, b.shape)
    tm, tn, tk = _mm_tiles(m, n, kt, trans_a, a.dtype.itemsize, b.dtype.itemsize,
                           jnp.dtype(out_dtype).itemsize, 0 if add is None else add.dtype.itemsize)
    nk = kt // tk
    grid = (m // tm, n // tn, nk)
    rd = _Rides(rides, [False] * len(rides))
    n_in = 2 if add is None else 3

    def body(*all_refs):
        ins, (o_ref,), scratch, handles = rd.split(all_refs, n_in, 1, 1 if nk > 1 else 0)
        refs = (*ins, o_ref, *scratch)
        if rides:
            ids = [pl.program_id(ax) for ax in range(3)]
            rd.run(handles, (ids[0] == 0) & (ids[1] == 0) & (ids[2] == 0),
                   (ids[0] == grid[0] - 1) & (ids[1] == grid[1] - 1) & (ids[2] == grid[2] - 1))
        if add is None:
            a_ref, b_ref, o_ref = refs[:3]
        else:
            a_ref, b_ref, add_ref, o_ref = refs[:4]
        k = pl.program_id(2)
        av = a_ref[...].astype(BF16)
        bv = b_ref[...].astype(BF16)
        dims = _DIMS["tn" if trans_a else "nt" if trans_b else "nn"]
        d = lax.dot_general(av, bv, (dims, ((), ())), preferred_element_type=F32)

        def finish(r):
            if add is not None:
                r = r + add_ref[...]
            o_ref[...] = r.astype(out_dtype)

        if nk == 1:
            finish(d)
        else:
            acc = refs[-1]

            @pl.when(k == 0)
            def _():
                acc[...] = d

            @pl.when((k > 0) & (k < nk - 1))
            def _():
                acc[...] += d

            @pl.when(k == nk - 1)
            def _():
                finish(acc[...] + d)

    a_spec = (pl.BlockSpec((tk, tm), lambda i, j, k: (k, i)) if trans_a
              else pl.BlockSpec((tm, tk), lambda i, j, k: (i, k)))
    b_spec = (pl.BlockSpec((tn, tk), lambda i, j, k: (j, k)) if trans_b
              else pl.BlockSpec((tk, tn), lambda i, j, k: (k, j)))
    in_specs = [a_spec, b_spec]
    args = [a, b]
    if add is not None:
        in_specs.append(pl.BlockSpec((tm, tn), lambda i, j, k: (i, j)))
        args.append(add)
    out = pl.pallas_call(
        body, name=name, grid=grid,
        in_specs=in_specs + rd.in_specs,
        out_specs=[pl.BlockSpec((tm, tn), lambda i, j, k: (i, j))] + rd.out_specs,
        out_shape=[jax.ShapeDtypeStruct((m, n), out_dtype)] + rd.out_shape,
        scratch_shapes=([pltpu.VMEM((tm, tn), F32)] if nk > 1 else []) + rd.scratch,
        compiler_params=_cp(*(("arbitrary",) * 3 if rides else ("parallel", "parallel", "arbitrary"))),
    )(*args, *rides)
    return out if rides else out[0]


def _rstd(x):
    return lax.rsqrt(jnp.mean(x * x, axis=-1, keepdims=True) + EPS)


def _norm_fwd(x, w, *, name):
    s = x.shape[0]
    tm = _tile(s, 512, 8)

    def body(x_ref, w_ref, o_ref):
        xv = x_ref[...]
        o_ref[...] = (xv * _rstd(xv) * w_ref[...]).astype(BF16)

    return pl.pallas_call(
        body, name=name, grid=(s // tm,),
        in_specs=[pl.BlockSpec((tm, D), lambda i: (i, 0)), pl.BlockSpec((1, D), lambda i: (0, 0))],
        out_specs=pl.BlockSpec((tm, D), lambda i: (i, 0)),
        out_shape=jax.ShapeDtypeStruct((s, D), BF16), compiler_params=_cp("parallel"),
    )(x, w)


def _norm_bwd_math(xv, wv, dy):
    r = _rstd(xv)
    xh = xv * r
    dxh = dy * wv
    dx = r * (dxh - xh * jnp.mean(dxh * xh, axis=-1, keepdims=True))
    dw = jnp.sum(dy * xh, axis=0, keepdims=True)
    return dx, dw


def _norm_bwd(x, w, dy, add, *, name):
    s = x.shape[0]
    tm = _tile(s, 256, 8)

    def body(*refs):
        if add is None:
            x_ref, w_ref, dy_ref, dx_ref, dw_ref = refs
        else:
            x_ref, w_ref, dy_ref, add_ref, dx_ref, dw_ref = refs

        @pl.when(pl.program_id(0) == 0)
        def _():
            dw_ref[...] = jnp.zeros_like(dw_ref)

        dx, dw = _norm_bwd_math(x_ref[...], w_ref[...], dy_ref[...])
        if add is not None:
            dx = dx + add_ref[...]
        dx_ref[...] = dx
        dw_ref[...] += dw

    row = pl.BlockSpec((tm, D), lambda i: (i, 0))
    vec = pl.BlockSpec((1, D), lambda i: (0, 0))
    in_specs = [row, vec, row] + ([row] if add is not None else [])
    args = [x, w, dy] + ([add] if add is not None else [])
    return pl.pallas_call(
        body, name=name, grid=(s // tm,), in_specs=in_specs, out_specs=[row, vec],
        out_shape=[jax.ShapeDtypeStruct((s, D), F32), jax.ShapeDtypeStruct((1, D), F32)],
        compiler_params=_cp("arbitrary"),
    )(*args)


def _final(x3, w, target, *, name):
    s = x3.shape[0]
    tm = _tile(s, 256, 8)

    def body(x_ref, w_ref, t_ref, dx_ref, dw_ref, loss_ref):
        @pl.when(pl.program_id(0) == 0)
        def _():
            dw_ref[...] = jnp.zeros_like(dw_ref)
            loss_ref[...] = jnp.zeros_like(loss_ref)

        xv = x_ref[...]
        wv = w_ref[...]
        y = xv * _rstd(xv) * wv
        err = y - t_ref[...]
        loss_ref[...] += 0.5 * jnp.sum(jnp.mean(err * err, axis=-1, keepdims=True))
        dx, dw = _norm_bwd_math(xv, wv, err * (1.0 / D))
        dx_ref[...] = dx
        dw_ref[...] += dw

    row = pl.BlockSpec((tm, D), lambda i: (i, 0))
    vec = pl.BlockSpec((1, D), lambda i: (0, 0))
    return pl.pallas_call(
        body, name=name, grid=(s // tm,), in_specs=[row, vec, row], out_specs=[row, vec, vec],
        out_shape=[jax.ShapeDtypeStruct((s, D), F32), jax.ShapeDtypeStruct((1, D), F32),
                   jax.ShapeDtypeStruct((1, D), F32)],
        compiler_params=_cp("arbitrary"),
    )(x3, w, target)


def _head_norm_math(o, w):
    lane = _iota((128, 128), 0) // 64
    bd = (lane == _iota((128, 128), 1) // 64).astype(F32)
    outs = []
    for op in _split(o, 128):
        ms = _dot("nn", op * op, bd, 2, 1) * (1.0 / 64)
        outs.append(op * lax.rsqrt(ms + EPS))
    return jnp.concatenate(outs, axis=1) * w


def _head_norm_fwd(o, w, *, name):
    s = o.shape[0]
    tm = _tile(s, 256, 8)

    def body(o_ref, w_ref, y_ref):
        y_ref[...] = _head_norm_math(o_ref[...], w_ref[...]).astype(BF16)

    row = pl.BlockSpec((tm, D), lambda i: (i, 0))
    vec = pl.BlockSpec((1, D), lambda i: (0, 0))
    return pl.pallas_call(
        body, name=name, grid=(s // tm,), in_specs=[row, vec], out_specs=row,
        out_shape=jax.ShapeDtypeStruct((s, D), BF16), compiler_params=_cp("parallel"),
    )(o, w)


def _head_norm_bwd(o, w, dymix, *, name):
    s = o.shape[0]
    tm = _tile(s, 256, 8)

    def body(o_ref, w_ref, dy_ref, do_ref, dw_ref):
        @pl.when(pl.program_id(0) == 0)
        def _():
            dw_ref[...] = jnp.zeros_like(dw_ref)

        _, vjp = jax.vjp(_head_norm_math, o_ref[...], w_ref[...])
        do, dw = vjp(dy_ref[...])
        do_ref[...] = do
        dw_ref[...] += dw

    row = pl.BlockSpec((tm, D), lambda i: (i, 0))
    vec = pl.BlockSpec((1, D), lambda i: (0, 0))
    return pl.pallas_call(
        body, name=name, grid=(s // tm,),
        in_specs=[row, vec, pl.BlockSpec((tm, D), lambda i: (i, 1))], out_specs=[row, vec],
        out_shape=[jax.ShapeDtypeStruct((s, D), F32), jax.ShapeDtypeStruct((1, D), F32)],
        compiler_params=_cp("arbitrary"),
    )(o, w, dymix)


SB_BQ = 256
SB_BK = 256


def _sb_consts():
    r = _iota((SB_BK, SB_BK), 0)
    c = _iota((SB_BK, SB_BK), 1)
    u_excl = (r > c).astype(BF16)
    u_incl = (r >= c).astype(BF16)
    return u_excl, u_incl


SB_LANES = 256
SB_NCH = SB_LANES // 64


def _nt(a, b):
    return lax.dot_general(a, b, (_DIMS["nt"], ((), ())), preferred_element_type=F32)


def _tn(a, b):
    return lax.dot_general(a, b, (_DIMS["tn"], ((), ())), preferred_element_type=F32)


def _nn(a, b):
    return jnp.dot(a, b, preferred_element_type=F32)


def _sb_heads(ref):
    out = []
    for hp in range(SB_LANES // 128):
        v = ref[:, 128 * hp:128 * (hp + 1)]
        first = _iota(v.shape, 1) < 64
        out += [jnp.where(first, v, 0).astype(BF16), jnp.where(first, 0, v).astype(BF16)]
    return out


SB_STRIP = 32


def _neg_abs(x):
    bits = lax.bitcast_convert_type(x, jnp.uint32) | jnp.uint32(0x80000000)
    return lax.bitcast_convert_type(bits, F32)


def _sb_block(ref, j):
    off = pl.multiple_of(j * SB_BK, SB_BK)
    return [ref[pl.ds(off, SB_BK), 128 * hp:128 * (hp + 1)] for hp in range(SB_NCH // 2)]


SB_DEAD = 104.0


def _sb_live(nlrun):
    m = nlrun[0]
    for x in nlrun[1:]:
        m = jnp.minimum(m, x)
    return jnp.min(m) < SB_DEAD


def _sb_strips():
    return [(r, pl.ds(r, SB_STRIP)) for r in range(0, SB_BQ, SB_STRIP)]


def _sb_diag_mask(r):
    return _iota((SB_STRIP, SB_BK), 1) < _iota((SB_STRIP, SB_BK), 0) + r


def _sb_soft(z, mask):
    e = jnp.exp(_neg_abs(z))
    nl = jnp.maximum(z, 0.0) + jnp.log(1.0 + e)
    if mask is not None:
        nl = jnp.where(mask, nl, 0.0)
    return e, nl


def _sb_split_to(hl_ref, rows, x):
    hi, lo = _terms(x, 2)
    hl_ref[rows, 0:SB_BK] = hi
    hl_ref[rows, SB_BK:2 * SB_BK] = lo


def _sb_stage_soft(z_ref, nl_ref, diag):
    for r, rows in _sb_strips():
        _, nl = _sb_soft(z_ref[rows, :], _sb_diag_mask(r) if diag else None)
        nl_ref[rows, 0:SB_BK] = nl.astype(BF16)


def _sb_stage_weights(z_ref, c_ref, a_ref, nlrun, diag):
    for r, rows in _sb_strips():
        a = jnp.exp(z_ref[rows, :] - c_ref[rows, :] - nlrun[r:r + SB_STRIP, :])
        if diag:
            a = jnp.where(_sb_diag_mask(r), a, 0.0)
        a_ref[rows, :] = a.astype(BF16)


def _sb_fwd(qkv, rides, *, name):
    s = qkv.shape[0]
    nq = s // SB_BQ
    ng = D // SB_LANES
    assert SB_BQ == SB_BK
    rd = _Rides(rides, [True] * len(rides))

    def body(*refs):
        (q_ref, k_ref, v_ref), (o_ref,), (zbuf, nlbuf, cbuf, abuf), handles = rd.split(refs, 3, 1, 4)
        i = pl.program_id(1)
        step_no = pl.program_id(0) * nq + i
        rd.run(handles, step_no == 0, step_no == ng * nq - 1)

        _, u_incl = _sb_consts()
        lane_a = _iota((SB_BQ, 128), 1) < 64
        qh = [q * 0.125 for q in _sb_heads(q_ref)]

        def tile(j, accs, nlrun, diag):
            kbs = _sb_block(k_ref, j)
            for c in range(SB_NCH):
                zbuf[c] = _nt(qh[c], kbs[c // 2])
            for c in range(SB_NCH):
                _sb_stage_soft(zbuf.at[c], nlbuf.at[c], diag)
                cbuf[c] = _nn(nlbuf[c], u_incl)
            for c in range(SB_NCH):
                _sb_stage_weights(zbuf.at[c], cbuf.at[c], abuf.at[c], nlrun[c], diag)
            nlrun = tuple(nlrun[c] + cbuf[c, :, 0:1] for c in range(SB_NCH))
            vbs = _sb_block(v_ref, j)
            outs = [_nn(abuf[c], vbs[c // 2]) for c in range(SB_NCH)]
            accs = tuple(acc + jnp.where(lane_a, outs[2 * hp], outs[2 * hp + 1]) for hp, acc in enumerate(accs))
            return accs, nlrun

        accs, nlrun = tile(i, (jnp.zeros((SB_BQ, 128), F32),) * (SB_NCH // 2),
                           (jnp.zeros((SB_BQ, 1), F32),) * SB_NCH, True)

        def step(carry):
            j, _, accs, nlrun = carry
            accs, nlrun = tile(j, accs, nlrun, False)
            return j - 1, _sb_live(nlrun), accs, nlrun

        _, _, accs, _ = lax.while_loop(lambda c: (c[0] >= 0) & c[1], step, (i - 1, _sb_live(nlrun), accs, nlrun))
        o_ref[...] = jnp.concatenate(accs, axis=1)

    return pl.pallas_call(
        body, name=name, grid=(ng, nq),
        in_specs=[pl.BlockSpec((SB_BQ, SB_LANES), lambda g, i: (i, g)),
                  pl.BlockSpec((s, SB_LANES), lambda g, i: (0, ng + g)),
                  pl.BlockSpec((s, SB_LANES), lambda g, i: (0, 2 * ng + g)), *rd.in_specs],
        out_specs=[pl.BlockSpec((SB_BQ, SB_LANES), lambda g, i: (i, g)), *rd.out_specs],
        out_shape=[jax.ShapeDtypeStruct((s, D), F32), *rd.out_shape],
        scratch_shapes=[pltpu.VMEM((SB_NCH, SB_BQ, SB_BK), F32), pltpu.VMEM((SB_NCH, SB_BQ, SB_BK), BF16),
                        pltpu.VMEM((SB_NCH, SB_BQ, SB_BK), F32), pltpu.VMEM((SB_NCH, SB_BQ, SB_BK), BF16),
                        *rd.scratch],
        compiler_params=_cp("arbitrary", "arbitrary"),
    )(qkv, qkv, qkv, *rides)


def _sb_bwd(qkv, o, do, rides, *, name):
    s = qkv.shape[0]
    nq = s // SB_BQ
    ng = D // SB_LANES
    nhp = SB_NCH // 2
    rd = _Rides(rides, [False] * len(rides))

    def body(*refs):
        ins, outs, scratch, handles = rd.split(refs, 5, 3, 11)
        q_ref, k_ref, v_ref, o_ref, do_ref = ins
        dq_ref, dk_hbm, dv_hbm = outs
        dk_acc, dv_acc, dk16, dv16, sems, zbuf, gbuf, hl, cbuf, abuf, dzbuf = scratch
        g_idx = pl.program_id(0)
        i = pl.program_id(1)
        step_no = g_idx * nq + i
        rd.run(handles, step_no == 0, step_no == ng * nq - 1)

        @pl.when(i == 0)
        def _():
            dk_acc[...] = jnp.zeros_like(dk_acc)
            dv_acc[...] = jnp.zeros_like(dv_acc)

        _, u_incl = _sb_consts()
        u2 = jnp.concatenate([u_incl, u_incl], axis=0)
        lane_a = _iota((SB_BQ, 128), 1) < 64
        lane_k = _iota((SB_BK, 128), 1) < 64
        qh = [q * 0.125 for q in _sb_heads(q_ref)]
        qf = [q_ref[:, 128 * hp:128 * (hp + 1)] for hp in range(nhp)]
        doh = _sb_heads(do_ref)
        dof = [do_ref[:, 128 * hp:128 * (hp + 1)].astype(BF16) for hp in range(nhp)]
        delta = []
        for hp in range(nhp):
            prod = dof[hp].astype(F32) * o_ref[:, 128 * hp:128 * (hp + 1)]
            delta += [jnp.sum(jnp.where(lane_a, prod, 0.0), axis=1, keepdims=True),
                      jnp.sum(jnp.where(lane_a, 0.0, prod), axis=1, keepdims=True)]

        def pre(slot, j):
            kbs = _sb_block(k_ref, j)
            vbs = _sb_block(v_ref, j)
            for c in range(SB_NCH):
                zbuf[slot, c] = _nt(qh[c], kbs[c // 2])
                gbuf[slot, c] = _nt(doh[c], vbs[c // 2])

        def stage_g(c, slot):
            for _, rows in _sb_strips():
                g = abuf[slot, c, rows, :].astype(F32) * gbuf[slot, c, rows, :]
                gbuf[slot, c, rows, :] = g
                _sb_split_to(hl.at[c], rows, g)

        def stage_dz(c, slot, grun, diag):
            for r, rows in _sb_strips():
                z = zbuf[slot, c, rows, :]
                g = gbuf[slot, c, rows, :]
                cs = (delta[c] - grun)[r:r + SB_STRIP, :] - cbuf[c, rows, :]
                sig = 1.0 / (1.0 + jnp.exp(-z))
                dz = g - (g + cs) * sig
                if diag:
                    dz = jnp.where(_sb_diag_mask(r), dz, 0.0)
                dzbuf[slot, c, rows, :] = dz.astype(BF16)

        def chain(slot, nlrun, grun, diag):
            for c in range(SB_NCH):
                _sb_stage_soft(zbuf.at[slot, c], hl.at[c], diag)
                cbuf[c] = _nn(hl[c, :, 0:SB_BK], u_incl)
            nl_tot = []
            for c in range(SB_NCH):
                _sb_stage_weights(zbuf.at[slot, c], cbuf.at[c], abuf.at[slot, c], nlrun[c], diag)
                nl_tot.append(cbuf[c, :, 0:1])
                stage_g(c, slot)
                cbuf[c] = _nn(hl[c], u2)
            g_tot = []
            for c in range(SB_NCH):
                stage_dz(c, slot, grun[c], diag)
                g_tot.append(cbuf[c, :, 0:1])
            return (tuple(a + b for a, b in zip(nlrun, nl_tot)), tuple(a + b for a, b in zip(grun, g_tot)))

        def post(slot, j, dqs):
            off = pl.multiple_of(j * SB_BK, SB_BK)
            kbs = _sb_block(k_ref, j)
            dq_t = [_nn(dzbuf[slot, c], kbs[c // 2]) for c in range(SB_NCH)]
            dk_t = [_tn(dzbuf[slot, c], qf[c // 2]) for c in range(SB_NCH)]
            dv_t = [_tn(abuf[slot, c], dof[c // 2]) for c in range(SB_NCH)]
            for hp in range(nhp):
                cols = slice(128 * hp, 128 * (hp + 1))
                dk_acc[pl.ds(off, SB_BK), cols] += 0.125 * jnp.where(lane_k, dk_t[2 * hp], dk_t[2 * hp + 1])
                dv_acc[pl.ds(off, SB_BK), cols] += jnp.where(lane_k, dv_t[2 * hp], dv_t[2 * hp + 1])
            return tuple(dq + jnp.where(lane_a, dq_t[2 * hp], dq_t[2 * hp + 1]) for hp, dq in enumerate(dqs))

        def tile(j, dqs, nlrun, grun, diag):
            pre(0, j)
            nlrun, grun = chain(0, nlrun, grun, diag)
            return post(0, j, dqs), nlrun, grun

        zero = (jnp.zeros((SB_BQ, 1), F32),) * SB_NCH
        dqs, nlrun, grun = tile(i, (jnp.zeros((SB_BQ, 128), F32),) * nhp, zero, zero, True)

        def step(carry):
            j, _, dqs, nlrun, grun = carry
            dqs, nlrun, grun = tile(j, dqs, nlrun, grun, False)
            return j - 1, _sb_live(nlrun), dqs, nlrun, grun

        carry = lax.while_loop(lambda c: (c[0] >= 0) & c[1], step, (i - 1, _sb_live(nlrun), dqs, nlrun, grun))
        dq_ref[...] = (0.125 * jnp.concatenate(carry[2], axis=1)).astype(BF16)

        def out_copies(g):
            cols = pl.ds(pl.multiple_of(g * SB_LANES, SB_LANES), SB_LANES)
            return (pltpu.make_async_copy(dk16, dk_hbm.at[:, cols], sems.at[0]),
                    pltpu.make_async_copy(dv16, dv_hbm.at[:, cols], sems.at[1]))

        @pl.when((i == nq - 1) & (g_idx > 0))
        def _():
            for cp in out_copies(g_idx - 1):
                cp.wait()

        @pl.when(i == nq - 1)
        def _():
            def narrow(r, carry):
                rows = pl.ds(pl.multiple_of(r * SB_BK, SB_BK), SB_BK)
                dk16[rows, :] = dk_acc[rows, :].astype(BF16)
                dv16[rows, :] = dv_acc[rows, :].astype(BF16)
                return carry

            lax.fori_loop(0, s // SB_BK, narrow, 0)
            for cp in out_copies(g_idx):
                cp.start()

        @pl.when((i == nq - 1) & (g_idx == ng - 1))
        def _():
            for cp in out_copies(g_idx):
                cp.wait()

    qblk = pl.BlockSpec((SB_BQ, SB_LANES), lambda g, i: (i, g))
    hbm = pl.BlockSpec(memory_space=pl.ANY)
    return pl.pallas_call(
        body, name=name, grid=(ng, nq),
        in_specs=[qblk, pl.BlockSpec((s, SB_LANES), lambda g, i: (0, ng + g)),
                  pl.BlockSpec((s, SB_LANES), lambda g, i: (0, 2 * ng + g)), qblk, qblk, *rd.in_specs],
        out_specs=[qblk, hbm, hbm, *rd.out_specs],
        out_shape=[jax.ShapeDtypeStruct((s, D), BF16)] * 3 + rd.out_shape,
        scratch_shapes=[pltpu.VMEM((s, SB_LANES), F32), pltpu.VMEM((s, SB_LANES), F32),
                        pltpu.VMEM((s, SB_LANES), BF16), pltpu.VMEM((s, SB_LANES), BF16),
                        pltpu.SemaphoreType.DMA((2,)),
                        pltpu.VMEM((1, SB_NCH, SB_BQ, SB_BK), F32), pltpu.VMEM((1, SB_NCH, SB_BQ, SB_BK), F32),
                        pltpu.VMEM((SB_NCH, SB_BQ, 2 * SB_BK), BF16), pltpu.VMEM((SB_NCH, SB_BQ, SB_BK), F32),
                        pltpu.VMEM((1, SB_NCH, SB_BQ, SB_BK), BF16), pltpu.VMEM((1, SB_NCH, SB_BQ, SB_BK), BF16),
                        *rd.scratch],
        compiler_params=_cp("arbitrary", "arbitrary"),
    )(qkv, qkv, qkv, o, do, *rides)


def _ssd_core(z, xpre, dtr, state, dtb, alog, dsk, nw):
    L = SSD_CHUNK
    xa = _silu(xpre)
    pieces = _split(xa, 128)
    xs = jnp.concatenate(pieces[:8], axis=1)
    bm, cm = pieces[8:10], pieces[10:12]
    dt = _softplus(dtr + dtb)
    a = dt * (-jnp.exp(alog))
    tri = (_iota((L, L), 0) >= _iota((L, L), 1)).astype(F32)
    a_cs = _dot("nn", tri, a, 1, 3)
    xc = xs * dt
    tril = _iota((L, L), 0) >= _iota((L, L), 1)
    lane_a = _iota((L, 128), 1) < 64
    acs_p = _split(a_cs, 128)
    xc_p = _split(xc, 128)
    ys, new_states = [], []
    for g in range(2):
        cb = _dot("nt", cm[g], bm[g])
        for pp in range(4):
            pair = 4 * g + pp
            acs = acs_p[pair]
            acs_t = acs.T
            xcp = xc_p[pair]
            st = state[pair]
            heads = []
            for hh in range(2):
                col = _take_fn(1, 64 * hh)(acs)
                row = _take_fn(0, 64 * hh)(acs_t)
                seg = col - row
                lm = jnp.where(tril, jnp.exp(jnp.where(tril, seg, 0.0)), 0.0)
                heads.append(_dot("nn", cb * lm, xcp))
            ydiag = jnp.where(lane_a, heads[0], heads[1])
            last = _take_fn(0, L - 1)(acs)
            snew = _dot("tn", xcp * jnp.exp(last - acs), bm[g])
            new_states.append(st * jnp.exp(_take_fn(1, L - 1)(acs_t)) + snew)
            yoff = _dot("nt", cm[g], st) * jnp.exp(acs)
            ys.append(ydiag + yoff)
    y = jnp.concatenate(ys, axis=1) + xs * dsk
    yg = y * _silu(z)
    outs = []
    for v in _split(yg, 512):
        outs.append(v * lax.rsqrt(jnp.mean(v * v, axis=-1, keepdims=True) + EPS))
    return jnp.concatenate(outs, axis=1) * nw, tuple(new_states)


XBC = 1536


def _ssd_conv(ext_ref, cw, cb):
    acc = cb
    for k in range(4):
        acc = acc + cw[k:k + 1, :] * ext_ref[pl.ds(HALO - 3 + k, SSD_CHUNK), :]
    return acc


def _ssd_fwd(z, xbc, dtr, cw, cb, lanes, rides, *, name):
    s = z.shape[0]
    L = SSD_CHUNK
    nc = s // L
    rd = _Rides(rides, [True] * len(rides))

    def body(*refs):
        ins, (y_ref, st_ref), (state, ext), handles = rd.split(refs, 7, 2, 2)
        z_ref, x_ref, h_ref, dtr_ref, cw_ref, cb_ref, ln_ref = ins
        c = pl.program_id(0)
        rd.run(handles, c == 0, c == nc - 1)

        @pl.when(c == 0)
        def _():
            state[...] = jnp.zeros_like(state)

        ext[0:HALO, :] = jnp.where(c == 0, 0.0, h_ref[...])
        ext[HALO:, :] = x_ref[...]
        xpre = _ssd_conv(ext, cw_ref[...], cb_ref[...])
        st_ref[0] = state[...]
        st_in = tuple(state[p] for p in range(8))
        yn, st_out = _ssd_core(z_ref[...], xpre, dtr_ref[...], st_in,
                               ln_ref[0:1, :], ln_ref[1:2, :], ln_ref[2:3, :], ln_ref[3:4, :])
        y_ref[...] = yn.astype(BF16)
        for p in range(8):
            state[p] = st_out[p]

    return pl.pallas_call(
        body, name=name, grid=(nc,),
        in_specs=[pl.BlockSpec((L, D), lambda c: (c, 0)),
                  pl.BlockSpec((L, XBC), lambda c: (c, 0)),
                  pl.BlockSpec((HALO, XBC), lambda c: (jnp.maximum(c * (L // HALO) - 1, 0), 0)),
                  pl.BlockSpec((L, D), lambda c: (c, 0)),
                  pl.BlockSpec((4, XBC), lambda c: (0, 0)),
                  pl.BlockSpec((1, XBC), lambda c: (0, 0)),
                  pl.BlockSpec((8, D), lambda c: (0, 0)), *rd.in_specs],
        out_specs=[pl.BlockSpec((L, D), lambda c: (c, 0)),
                   pl.BlockSpec((1, 8, 128, 128), lambda c: (c, 0, 0, 0)), *rd.out_specs],
        out_shape=[jax.ShapeDtypeStruct((s, D), BF16), jax.ShapeDtypeStruct((nc, 8, 128, 128), F32),
                   *rd.out_shape],
        scratch_shapes=[pltpu.VMEM((8, 128, 128), F32), pltpu.VMEM((L + HALO, XBC), F32), *rd.scratch],
        compiler_params=_cp("arbitrary"),
    )(z, xbc, xbc, dtr, cw, cb, lanes, *rides)


def _ssd_bwd(z, xbc, dtr, states, dymix, cw, cb, lanes, *, name):
    s = z.shape[0]
    L = SSD_CHUNK
    nc = s // L

    def body(z_ref, x_ref, h_ref, dtr_ref, st_ref, dy_ref, cw_ref, cb_ref, ln_ref,
             dz_ref, dx_ref, ddt_ref, dln_ref, dcv_ref, dstate, ext, dext):
        i = pl.program_id(0)
        c = nc - 1 - i

        @pl.when(i == 0)
        def _():
            dstate[...] = jnp.zeros_like(dstate)
            dext[...] = jnp.zeros_like(dext)
            dln_ref[...] = jnp.zeros_like(dln_ref)
            dcv_ref[...] = jnp.zeros_like(dcv_ref)

        ext[0:HALO, :] = jnp.where(c == 0, 0.0, h_ref[...])
        ext[HALO:, :] = x_ref[...]
        cwv = cw_ref[...]
        xpre = _ssd_conv(ext, cwv, cb_ref[...])
        st_in = tuple(st_ref[0, p] for p in range(8))
        _, vjp = jax.vjp(_ssd_core, z_ref[...], xpre, dtr_ref[...], st_in,
                         ln_ref[0:1, :], ln_ref[1:2, :], ln_ref[2:3, :], ln_ref[3:4, :])
        dz, dxpre, ddtr, dst, d0, d1, d2, d3 = vjp((dy_ref[...], tuple(dstate[p] for p in range(8))))
        for p in range(8):
            dstate[p] = dst[p]
        dz_ref[...] = dz.astype(BF16)
        ddt_ref[...] = ddtr.astype(BF16)
        dln_ref[0:4, :] += jnp.concatenate([d0, d1, d2, d3], axis=0)
        dext[0:L, :] = dxpre
        xcur = x_ref[...]
        dx = jnp.zeros((L, XBC), F32)
        rows = []
        for k in range(4):
            shifted = dext[pl.ds(3 - k, L), :]
            dx = dx + cwv[k:k + 1, :] * shifted
            rows.append(jnp.sum(shifted * xcur, axis=0, keepdims=True))
        rows.append(jnp.sum(dxpre, axis=0, keepdims=True))
        dx_ref[...] = dx.astype(BF16)
        dcv_ref[0:5, :] += jnp.concatenate(rows, axis=0)
        dext[L:L + HALO, :] = dxpre[0:HALO, :]

    rev = lambda i: (nc - 1 - i, 0)
    return pl.pallas_call(
        body, name=name, grid=(nc,),
        in_specs=[pl.BlockSpec((L, D), rev),
                  pl.BlockSpec((L, XBC), rev),
                  pl.BlockSpec((HALO, XBC), lambda i: (jnp.maximum((nc - 1 - i) * (L // HALO) - 1, 0), 0)),
                  pl.BlockSpec((L, D), rev),
                  pl.BlockSpec((1, 8, 128, 128), lambda i: (nc - 1 - i, 0, 0, 0)),
                  pl.BlockSpec((L, D), rev),
                  pl.BlockSpec((4, XBC), lambda i: (0, 0)),
                  pl.BlockSpec((1, XBC), lambda i: (0, 0)),
                  pl.BlockSpec((8, D), lambda i: (0, 0))],
        out_specs=[pl.BlockSpec((L, D), rev), pl.BlockSpec((L, XBC), rev), pl.BlockSpec((L, D), rev),
                   pl.BlockSpec((8, D), lambda i: (0, 0)), pl.BlockSpec((8, XBC), lambda i: (0, 0))],
        out_shape=[jax.ShapeDtypeStruct((s, D), BF16), jax.ShapeDtypeStruct((s, XBC), BF16),
                   jax.ShapeDtypeStruct((s, D), BF16), jax.ShapeDtypeStruct((8, D), F32),
                   jax.ShapeDtypeStruct((8, XBC), F32)],
        scratch_shapes=[pltpu.VMEM((8, 128, 128), F32), pltpu.VMEM((L + HALO, XBC), F32),
                        pltpu.VMEM((L + HALO, XBC), F32)],
        compiler_params=_cp("arbitrary"),
    )(z, xbc, xbc, dtr, states, dymix, cw, cb, lanes)


def _mem_attn_math(q, k, v):
    outs = []
    for qh, kh, vh in zip(_split(q, 256), _split(k, 256), _split(v, 256)):
        sc = _dot("nt", qh, kh) * (1.0 / 16.0)
        e = jnp.exp(sc - lax.stop_gradient(jnp.max(sc, axis=-1, keepdims=True)))
        p = e / jnp.sum(e, axis=-1, keepdims=True)
        outs.append(_dot("nn", p, vh))
    return jnp.concatenate(outs, axis=1)


def _mem_attn_fwd(q, k, v, *, name):
    s, m = q.shape[0], k.shape[0]
    tm = _tile(s, 256, 8)

    def body(q_ref, k_ref, v_ref, o_ref):
        o_ref[...] = _mem_attn_math(q_ref[...].astype(F32), k_ref[...].astype(F32),
                                    v_ref[...].astype(F32)).astype(BF16)

    row = pl.BlockSpec((tm, D), lambda i: (i, 0))
    kv = pl.BlockSpec((m, D), lambda i: (0, 0))
    return pl.pallas_call(
        body, name=name, grid=(s // tm,), in_specs=[row, kv, kv], out_specs=row,
        out_shape=jax.ShapeDtypeStruct((s, D), BF16), compiler_params=_cp("parallel"),
    )(q, k, v)


def _mem_attn_bwd(q, k, v, do, *, name):
    s, m = q.shape[0], k.shape[0]
    tm = _tile(s, 256, 8)

    def body(q_ref, k_ref, v_ref, do_ref, dq_ref, dk_ref, dv_ref):
        @pl.when(pl.program_id(0) == 0)
        def _():
            dk_ref[...] = jnp.zeros_like(dk_ref)
            dv_ref[...] = jnp.zeros_like(dv_ref)

        _, vjp = jax.vjp(_mem_attn_math, q_ref[...].astype(F32), k_ref[...].astype(F32),
                         v_ref[...].astype(F32))
        dq, dk, dv = vjp(do_ref[...])
        dq_ref[...] = dq.astype(BF16)
        dk_ref[...] += dk
        dv_ref[...] += dv

    row = pl.BlockSpec((tm, D), lambda i: (i, 0))
    kv = pl.BlockSpec((m, D), lambda i: (0, 0))
    return pl.pallas_call(
        body, name=name, grid=(s // tm,), in_specs=[row, kv, kv, row], out_specs=[row, kv, kv],
        out_shape=[jax.ShapeDtypeStruct((s, D), BF16), jax.ShapeDtypeStruct((m, D), F32),
                   jax.ShapeDtypeStruct((m, D), F32)],
        compiler_params=_cp("arbitrary"),
    )(q, k, v, do)


DFF = 2816
FFN_TC = 1408
FFN_TM = 256


FFN_CHUNKS = tuple((c, min(512, FFN_TC - c)) for c in range(0, FFN_TC, 512))


def _rows8(ref, r, cols):
    return ref[pl.ds(pl.multiple_of(r, HALO), HALO), cols]


def _shift_down(prev, cur, s):
    return jnp.where(_iota(cur.shape, 0) < s, pltpu.roll(prev, s, 0), pltpu.roll(cur, s, 0))


def _shift_up(cur, nxt, s):
    return jnp.where(_iota(cur.shape, 0) >= HALO - s, pltpu.roll(nxt, HALO - s, 0), pltpu.roll(cur, HALO - s, 0))


def _ffn_conv_strip(ext_ref, r, cols, cw, cb):
    prev, cur = _rows8(ext_ref, r, cols), _rows8(ext_ref, r + HALO, cols)
    return cb + cw[0:1, :] * _shift_down(prev, cur, 2) + cw[1:2, :] * _shift_down(prev, cur, 1) + cw[2:3, :] * cur


def _ffn_specs(s):
    tm, tc = FFN_TM, FFN_TC
    blk = pl.BlockSpec((tm, tc), lambda i, j: (i, j))
    halo = pl.BlockSpec((HALO, tc), lambda i, j: (jnp.maximum(i * (tm // HALO) - 1, 0), j))
    cw = pl.BlockSpec((3, tc), lambda i, j: (0, j))
    cb = pl.BlockSpec((1, tc), lambda i, j: (0, j))
    return tm, tc, blk, halo, cw, cb


def _glu_fwd(ug, uv, cwg, cwv, cbg, cbv, *, name):
    s = ug.shape[0]
    tm, tc, blk, halo, cw, cb = _ffn_specs(s)

    def body(g_ref, gh_ref, v_ref, vh_ref, cwg_ref, cwv_ref, cbg_ref, cbv_ref, f_ref, eg, ev):
        first = pl.program_id(0) == 0
        eg[0:HALO, :] = jnp.where(first, 0.0, gh_ref[...])
        eg[HALO:, :] = g_ref[...]
        ev[0:HALO, :] = jnp.where(first, 0.0, vh_ref[...])
        ev[HALO:, :] = v_ref[...]
        cwgv, cwvv, cbgv, cbvv = cwg_ref[...], cwv_ref[...], cbg_ref[...], cbv_ref[...]

        def step(t, carry):
            for c0, w in FFN_CHUNKS:
                cols = slice(c0, c0 + w)
                outs = []
                for h in range(2):
                    r = t * 16 + HALO * h
                    g = _ffn_conv_strip(eg, r, cols, cwgv[:, cols], cbgv[:, cols])
                    v = _ffn_conv_strip(ev, r, cols, cwvv[:, cols], cbvv[:, cols])
                    outs.append(_silu(g) * v)
                f_ref[pl.ds(pl.multiple_of(t * 16, 16), 16), cols] = jnp.concatenate(outs, axis=0).astype(BF16)
            return carry

        lax.fori_loop(0, tm // 16, step, 0)

    return pl.pallas_call(
        body, name=name, grid=(s // tm, DFF // tc),
        in_specs=[blk, halo, blk, halo, cw, cw, cb, cb], out_specs=blk,
        out_shape=jax.ShapeDtypeStruct((s, DFF), BF16),
        scratch_shapes=[pltpu.VMEM((tm + HALO, tc), F32)] * 2,
        compiler_params=_cp("parallel", "parallel"),
    )(ug, ug, uv, uv, cwg, cwv, cbg, cbv)


def _ffn_bwd(ug, uv, df, cwg, cwv, cbg, cbv, *, name):
    s = ug.shape[0]
    tm, tc = FFN_TM, FFN_TC
    nb = s // tm
    rows_ext = tm + HALO

    def body(g_ref, gp_ref, gn_ref, v_ref, vp_ref, vn_ref, df_ref, dfn_ref, cwg_ref, cwv_ref, cbg_ref, cbv_ref,
             dxg_ref, dxv_ref, dcg_ref, dcv_ref, eg, ev, edf, edg, edv, accg, accv):
        i = pl.program_id(1)
        first, last = i == 0, i == nb - 1

        @pl.when(first)
        def _():
            dcg_ref[...] = jnp.zeros_like(dcg_ref)
            dcv_ref[...] = jnp.zeros_like(dcv_ref)

        for e, prev, main, nxt in ((eg, gp_ref, g_ref, gn_ref), (ev, vp_ref, v_ref, vn_ref)):
            e[0:HALO, :] = jnp.where(first, 0.0, prev[...])
            e[HALO:HALO + tm, :] = main[...]
            e[HALO + tm:, :] = jnp.where(last, 0.0, nxt[...])
        edf[0:tm, :] = df_ref[...]
        edf[tm:, :] = jnp.where(last, 0.0, dfn_ref[...])
        accg[...] = jnp.zeros_like(accg)
        accv[...] = jnp.zeros_like(accv)
        cwgv, cwvv, cbgv, cbvv = cwg_ref[...], cwv_ref[...], cbg_ref[...], cbv_ref[...]

        def cotangents(t, carry):
            r = t * HALO
            for c0, w in FFN_CHUNKS:
                cols = slice(c0, c0 + w)
                g = _ffn_conv_strip(eg, r, cols, cwgv[:, cols], cbgv[:, cols])
                v = _ffn_conv_strip(ev, r, cols, cwvv[:, cols], cbvv[:, cols])
                dfs = _rows8(edf, r, cols)
                sg = _sigmoid(g)
                edv[pl.ds(pl.multiple_of(r, HALO), HALO), cols] = dfs * g * sg
                edg[pl.ds(pl.multiple_of(r, HALO), HALO), cols] = dfs * v * sg * (1.0 + g * (1.0 - sg))
            return carry

        lax.fori_loop(0, rows_ext // HALO, cotangents, 0)

        def conv_backward(t, carry):
            for c0, w in FFN_CHUNKS:
                cols = slice(c0, c0 + w)
                for edu, e, cw, dx_ref, acc in ((edg, eg, cwgv[:, cols], dxg_ref, accg),
                                                (edv, ev, cwvv[:, cols], dxv_ref, accv)):
                    dxs = []
                    for h in range(2):
                        r = t * 16 + HALO * h
                        cur, nxt = _rows8(edu, r, cols), _rows8(edu, r + HALO, cols)
                        up1, up2 = _shift_up(cur, nxt, 1), _shift_up(cur, nxt, 2)
                        x = _rows8(e, r + HALO, cols)
                        dxs.append(cw[2:3, :] * cur + cw[1:2, :] * up1 + cw[0:1, :] * up2)
                        acc[0, :, cols] += up2 * x
                        acc[1, :, cols] += up1 * x
                        acc[2, :, cols] += cur * x
                        acc[3, :, cols] += cur
                    dx_ref[pl.ds(pl.multiple_of(t * 16, 16), 16), cols] = jnp.concatenate(dxs, axis=0).astype(BF16)
            return carry

        lax.fori_loop(0, tm // 16, conv_backward, 0)
        for acc, dc_ref in ((accg, dcg_ref), (accv, dcv_ref)):
            dc_ref[0:4, :] += jnp.concatenate([jnp.sum(acc[k], axis=0, keepdims=True) for k in range(4)], axis=0)

    blk = pl.BlockSpec((tm, tc), lambda j, i: (i, j))
    nxt = pl.BlockSpec((HALO, tc), lambda j, i: (jnp.minimum((i + 1) * (tm // HALO), s // HALO - 1), j))
    prv = pl.BlockSpec((HALO, tc), lambda j, i: (jnp.maximum(i * (tm // HALO) - 1, 0), j))
    cw = pl.BlockSpec((3, tc), lambda j, i: (0, j))
    cb = pl.BlockSpec((1, tc), lambda j, i: (0, j))
    acc = pl.BlockSpec((8, tc), lambda j, i: (0, j))
    return pl.pallas_call(
        body, name=name, grid=(DFF // tc, nb),
        in_specs=[blk, prv, nxt, blk, prv, nxt, blk, nxt, cw, cw, cb, cb],
        out_specs=[blk, blk, acc, acc],
        out_shape=[jax.ShapeDtypeStruct((s, DFF), BF16)] * 2 + [jax.ShapeDtypeStruct((8, DFF), F32)] * 2,
        scratch_shapes=[pltpu.VMEM((tm + 2 * HALO, tc), F32)] * 2 + [pltpu.VMEM((rows_ext, tc), F32)] * 3
                       + [pltpu.VMEM((4, HALO, tc), F32)] * 2,
        compiler_params=_cp("parallel", "arbitrary"),
    )(ug, ug, ug, uv, uv, uv, df, df, cwg, cwv, cbg, cbv)


MESH = pl.DeviceIdType.MESH


def _all_gather(arrs, *, name):
    n = len(arrs)

    def body(*refs):
        x_refs, out_refs = refs[:n], refs[n:2 * n]
        send_sems, recv_sems, local_sems = refs[2 * n:]
        x, y, c = lax.axis_index("x"), lax.axis_index("y"), lax.axis_index("c")
        me, sibling = (x, y, c), (x, y, 1 - c)
        chips = [(1 - x, y), (x, 1 - y), (1 - x, 1 - y)]

        def blk(a, dev):
            return out_refs[a].at[4 * dev[0] + 2 * dev[1] + dev[2]]

        def copy(a, k, block, to, src=None):
            return pltpu.make_async_remote_copy(
                src_ref=blk(a, block) if src is None else src, dst_ref=blk(a, block),
                send_sem=send_sems.at[7 * a + k], recv_sem=recv_sems.at[7 * a + k],
                device_id=to, device_id_type=MESH)

        started = []
        mine = []
        for a in range(n):
            cp = pltpu.make_async_copy(x_refs[a], blk(a, me), local_sems.at[a])
            cp.start()
            mine.append(cp)
            first = [copy(a, 0, me, sibling, src=x_refs[a])]
            first += [copy(a, 1 + j, me, (*chip, c), src=x_refs[a]) for j, chip in enumerate(chips)]
            for cp in first:
                cp.start()
            started += first
        for a in range(n):
            for j, chip in enumerate(chips):
                copy(a, 1 + j, (*chip, c), me).wait_recv()
                fwd = copy(a, 4 + j, (*chip, c), sibling)
                fwd.start()
                started.append(fwd)
        for a in range(n):
            copy(a, 0, sibling, me).wait_recv()
            for j, chip in enumerate(chips):
                copy(a, 4 + j, (*chip, 1 - c), me).wait_recv()
        for cp in started:
            cp.wait_send()
        for cp in mine:
            cp.wait()

    any_spec = pl.BlockSpec(memory_space=pl.ANY)
    return pl.pallas_call(
        body, name=name,
        in_specs=[any_spec] * n, out_specs=[any_spec] * n,
        out_shape=[jax.ShapeDtypeStruct((NDEV,) + a.shape, a.dtype) for a in arrs],
        scratch_shapes=[pltpu.SemaphoreType.DMA((7 * n,)), pltpu.SemaphoreType.DMA((7 * n,)),
                        pltpu.SemaphoreType.DMA((n,))],
    )(*arrs)


class _Direct:
    SEMS = (pltpu.SemaphoreType.DMA((7,)), pltpu.SemaphoreType.DMA((7,)), pltpu.SemaphoreType.DMA((1,)))

    def __init__(self, src_ref, recv_ref, sems, gather):
        x, y, c = lax.axis_index("x"), lax.axis_index("y"), lax.axis_index("c")
        me = 4 * x + 2 * y + c
        send_sems, recv_sems, local_sem = sems
        src = (lambda pid: src_ref) if gather else (lambda pid: src_ref.at[pid])
        self.mine = pltpu.make_async_copy(src(me), recv_ref.at[me], local_sem.at[0])
        self.copies = []
        for k in range(1, NDEV):
            px = 1 - x if k & 4 else x
            py = 1 - y if k & 2 else y
            pc = 1 - c if k & 1 else c
            self.copies.append(pltpu.make_async_remote_copy(
                src_ref=src(4 * px + 2 * py + pc), dst_ref=recv_ref.at[me],
                send_sem=send_sems.at[k - 1], recv_sem=recv_sems.at[k - 1],
                device_id=(px, py, pc), device_id_type=MESH))

    def start(self):
        self.mine.start()
        for cp in self.copies:
            cp.start()

    def wait(self):
        for cp in self.copies:
            cp.wait_recv()
        for cp in self.copies:
            cp.wait_send()
        self.mine.wait()


def _recv_shape(src, gather):
    return jax.ShapeDtypeStruct(((NDEV,) + src.shape) if gather else src.shape, src.dtype)


class _Rides:
    def __init__(self, rides, gathers):
        self.n = len(rides)
        self.gathers = list(gathers)
        any_spec = pl.BlockSpec(memory_space=pl.ANY)
        self.in_specs = [any_spec] * self.n
        self.out_specs = [any_spec] * self.n
        self.out_shape = [_recv_shape(a, g) for a, g in zip(rides, gathers)]
        self.scratch = list(_Direct.SEMS) * self.n

    def split(self, refs, n_in, n_out, n_scratch):
        n = self.n
        ins, refs = refs[:n_in], refs[n_in:]
        rides, refs = refs[:n], refs[n:]
        outs, refs = refs[:n_out], refs[n_out:]
        gots, refs = refs[:n], refs[n:]
        scratch, sems = refs[:n_scratch], refs[n_scratch:]
        return ins, outs, scratch, (rides, gots, sems)

    def run(self, handles, first, last):
        rides, gots, sems = handles

        def all_of():
            return [_Direct(rides[a], gots[a], sems[3 * a:3 * a + 3], self.gathers[a]) for a in range(self.n)]

        @pl.when(first)
        def _():
            for e in all_of():
                e.start()

        @pl.when(last)
        def _():
            for e in all_of():
                e.wait()


def _exchange(arrs, gathers, *, name):
    rd = _Rides(arrs, gathers)

    def body(*refs):
        _, _, _, handles = rd.split(refs, 0, 0, 0)
        rd.run(handles, True, True)

    return pl.pallas_call(
        body, name=name, in_specs=rd.in_specs, out_specs=rd.out_specs, out_shape=rd.out_shape,
        scratch_shapes=rd.scratch,
    )(*arrs)


def _adamw(parts, w, m, v, *, name):
    r, cols = w.shape
    tm = _tile(r, 256, PACK_ALIGN)
    c1 = 1.0 - ADAM_B1 ** ADAM_STEP
    c2 = 1.0 - ADAM_B2 ** ADAM_STEP

    def body(p_ref, w_ref, m_ref, v_ref, g_ref, d_ref, nm_ref, nv_ref):
        g = p_ref[0].astype(F32)
        for i in range(1, NDEV):
            g = g + p_ref[i].astype(F32)
        nm = ADAM_B1 * m_ref[...] + (1.0 - ADAM_B1) * g
        nv = ADAM_B2 * v_ref[...] + (1.0 - ADAM_B2) * (g * g)
        d_ref[...] = -ADAM_LR * ((nm / c1) / (jnp.sqrt(nv / c2) + ADAM_EPS) + ADAM_WD * w_ref[...])
        g_ref[...] = g
        nm_ref[...] = nm
        nv_ref[...] = nv

    row = pl.BlockSpec((tm, cols), lambda i: (i, 0))
    return pl.pallas_call(
        body, name=name, grid=(r // tm,),
        in_specs=[pl.BlockSpec((NDEV, tm, cols), lambda i: (0, i, 0)), row, row, row],
        out_specs=[row] * 4, out_shape=[jax.ShapeDtypeStruct((r, cols), F32)] * 4,
        compiler_params=_cp("parallel"),
    )(parts, w, m, v)


PACK_ALIGN = 16


def _part_rows(shape):
    n = -(-math.prod(shape) // D)
    return n + (-n) % PACK_ALIGN


def _rows(a):
    flat = a.reshape(-1)
    pad = _part_rows(a.shape) * D - flat.shape[0]
    if pad:
        flat = jnp.concatenate([flat, jnp.zeros((pad,), flat.dtype)])
    return flat.reshape(-1, D)


def _pack(parts, total_rows):
    if all(math.prod(p.shape) % (PACK_ALIGN * D) for p in parts):
        return _pack_small(parts, total_rows)
    rows = [_rows(p) for p in parts]
    used = sum(r.shape[0] for r in rows)
    if total_rows > used:
        rows.append(jnp.zeros((total_rows - used, D), rows[0].dtype))
    return jnp.concatenate(rows, axis=0)


def _pack_small(parts, total_rows):
    flat, used = [], 0
    for p in parts:
        n, nr = math.prod(p.shape), _part_rows(p.shape)
        flat += [p.reshape(-1), jnp.zeros((nr * D - n,), p.dtype)]
        used += nr
    flat.append(jnp.zeros(((total_rows - used) * D,), parts[0].dtype))
    return jnp.concatenate(flat).reshape(total_rows, D)


def _unpack(buf, shapes, part_rows=_part_rows):
    out, r0 = [], 0
    for shp in shapes:
        n = math.prod(shp)
        out.append(buf[r0:r0 + part_rows(shp)].reshape(-1)[:n].reshape(shp))
        r0 += part_rows(shp)
    return out


def _tight_rows(shape):
    return -(-math.prod(shape) // D)


def _pack_tight(parts, total_rows):
    flat, used = [], 0
    for p in parts:
        n, nr = math.prod(p.shape), _tight_rows(p.shape)
        flat += [p.reshape(-1), jnp.zeros((nr * D - n,), p.dtype)]
        used += nr
    flat.append(jnp.zeros(((total_rows - used) * D,), parts[0].dtype))
    return jnp.concatenate(flat).reshape(total_rows, D)


SHARD = {"w_in": (D, 706), "w_out": (256, D), "w_mq": (128, D), "w_mk": (128, D), "w_mv": (128, D),
         "w_mo": (128, D), "w_up": (D, 704), "w_down": (352, D), "conv_ssd_w": (4, 192), "conv_ffn_w": (3, 704)}
GATHER_MID = ["w_out", "w_mq", "w_mk", "w_mv", "w_mo"]
GATHER_FFN = ["w_down"]
CONV_TAPS = ["conv_ssd_w", "conv_ffn_w"]
GRADS_PACKED = ["w_out", "w_mq", "w_mk", "w_mv", "w_mo", "w_down", "conv_ffn_w", "conv_ssd_w"]


def _layout(names):
    row0, r = {}, 0
    for n in names:
        row0[n] = r
        r += _part_rows(SHARD[n])
    return row0, r + (-r) % 128


SMALL = [("norm_mix_w", (1, D)), ("conv_ssd_b", (1, 1536)), ("dt_bias", (1, 16)), ("a_log", (1, 16)),
         ("d_skip", (1, 16)), ("ssd_norm_w", (1, D)), ("sb_norm_w", (1, D)), ("norm_mem_w", (1, D)),
         ("norm_memkv_w", (1, D)), ("norm_ffn_w", (1, D)), ("conv_ffn_b", (1, 5632)), ("norm_final_w", (D,))]
LOSS_ROW = sum(_tight_rows(_shp) for _, _shp in SMALL)
SMALL_ROWS = LOSS_ROW + 1 + (-(LOSS_ROW + 1)) % 8
ORDER = ["norm_mix_w", "w_in", "conv_ssd_w", "conv_ssd_b", "dt_bias", "a_log", "d_skip", "ssd_norm_w",
         "sb_norm_w", "w_out", "norm_mem_w", "norm_memkv_w", "w_mq", "w_mk", "w_mv", "w_mo", "norm_ffn_w",
         "w_up", "conv_ffn_w", "conv_ffn_b", "w_down", "norm_final_w"]


def _pad_rows(a, nr):
    n = a.shape[1]
    return jnp.concatenate([a, jnp.zeros((NDEV, nr * D - n), a.dtype)], axis=1).reshape(NDEV, nr, D)


def _group_sum(lanes):
    return lanes.reshape(16, 64).sum(axis=1).reshape(1, 16)


def kernel(x, mem, norm_mix_w, w_in, conv_ssd_w, conv_ssd_b, dt_bias, a_log, d_skip, ssd_norm_w, sb_norm_w, w_out, norm_mem_w, norm_memkv_w, w_mq, w_mk, w_mv, w_mo, norm_ffn_w, w_up, conv_ffn_w, conv_ffn_b, w_down, norm_final_w, loss_target, m_norm_mix_w, m_w_in, m_conv_ssd_w, m_conv_ssd_b, m_dt_bias, m_a_log, m_d_skip, m_ssd_norm_w, m_sb_norm_w, m_w_out, m_norm_mem_w, m_norm_memkv_w, m_w_mq, m_w_mk, m_w_mv, m_w_mo, m_norm_ffn_w, m_w_up, m_conv_ffn_w, m_conv_ffn_b, m_w_down, m_norm_final_w, v_norm_mix_w, v_w_in, v_conv_ssd_w, v_conv_ssd_b, v_dt_bias, v_a_log, v_d_skip, v_ssd_norm_w, v_sb_norm_w, v_w_out, v_norm_mem_w, v_norm_memkv_w, v_w_mq, v_w_mk, v_w_mv, v_w_mo, v_norm_ffn_w, v_w_up, v_conv_ffn_w, v_conv_ffn_b, v_w_down, v_norm_final_w):
    P = dict(norm_mix_w=norm_mix_w, w_in=w_in, conv_ssd_w=conv_ssd_w, conv_ssd_b=conv_ssd_b, dt_bias=dt_bias, a_log=a_log, d_skip=d_skip, ssd_norm_w=ssd_norm_w, sb_norm_w=sb_norm_w, w_out=w_out, norm_mem_w=norm_mem_w, norm_memkv_w=norm_memkv_w, w_mq=w_mq, w_mk=w_mk, w_mv=w_mv, w_mo=w_mo, norm_ffn_w=norm_ffn_w, w_up=w_up, conv_ffn_w=conv_ffn_w, conv_ffn_b=conv_ffn_b, w_down=w_down, norm_final_w=norm_final_w)
    M = dict(norm_mix_w=m_norm_mix_w, w_in=m_w_in, conv_ssd_w=m_conv_ssd_w, conv_ssd_b=m_conv_ssd_b, dt_bias=m_dt_bias, a_log=m_a_log, d_skip=m_d_skip, ssd_norm_w=m_ssd_norm_w, sb_norm_w=m_sb_norm_w, w_out=m_w_out, norm_mem_w=m_norm_mem_w, norm_memkv_w=m_norm_memkv_w, w_mq=m_w_mq, w_mk=m_w_mk, w_mv=m_w_mv, w_mo=m_w_mo, norm_ffn_w=m_norm_ffn_w, w_up=m_w_up, conv_ffn_w=m_conv_ffn_w, conv_ffn_b=m_conv_ffn_b, w_down=m_w_down, norm_final_w=m_norm_final_w)
    V = dict(norm_mix_w=v_norm_mix_w, w_in=v_w_in, conv_ssd_w=v_conv_ssd_w, conv_ssd_b=v_conv_ssd_b, dt_bias=v_dt_bias, a_log=v_a_log, d_skip=v_d_skip, ssd_norm_w=v_ssd_norm_w, sb_norm_w=v_sb_norm_w, w_out=v_w_out, norm_mem_w=v_norm_mem_w, norm_memkv_w=v_norm_memkv_w, w_mq=v_w_mq, w_mk=v_w_mk, w_mv=v_w_mv, w_mo=v_w_mo, norm_ffn_w=v_norm_ffn_w, w_up=v_w_up, conv_ffn_w=v_conv_ffn_w, conv_ffn_b=v_conv_ffn_b, w_down=v_w_down, norm_final_w=v_norm_final_w)
    small_shapes = [shp for _, shp in SMALL]

    def packed(src, names, dtype=F32):
        return _pack([src[n][0] for n in names], _layout(names)[1]).astype(dtype)

    def columns(g):
        return g.transpose(1, 0, 2).reshape(g.shape[1], NDEV * g.shape[2])

    g_in, g_taps = _all_gather([w_in[0].astype(BF16), packed(P, CONV_TAPS)], name="gather_w_in")
    W_in = columns(g_in)
    cw_ssd = g_taps[:, 0].reshape(NDEV, -1)[:, :768].reshape(NDEV, 4, 192).transpose(1, 0, 2).reshape(4, XBC)
    cw_ffn = (g_taps[:, PACK_ALIGN:PACK_ALIGN + 3].reshape(NDEV, -1)[:, :2112].reshape(NDEV, 3, 704)
              .transpose(1, 0, 2).reshape(3, 2 * DFF))
    W_z, W_xbc, W_dt, W_qkv = W_in[:, :D], W_in[:, D:D + XBC], W_in[:, D + XBC:D + XBC + 16], W_in[:, D + XBC + 16:]
    W_dtr = jnp.repeat(W_dt, 64, axis=1)
    cwg, cwv = cw_ffn[:, :DFF], cw_ffn[:, DFF:]
    cbg, cbv = conv_ffn_b[:, :DFF], conv_ffn_b[:, DFF:]
    rep = lambda p: jnp.repeat(p, 64, axis=1)
    lanes = jnp.concatenate([rep(dt_bias), rep(a_log), rep(d_skip), ssd_norm_w, jnp.zeros((4, D), F32)], axis=0)

    xs, tgt, mm = x[0], loss_target[0], mem[0]

    h1 = _norm_fwd(xs, norm_mix_w, name="norm_mix")
    z = _mm(h1, W_z, name="proj_z")
    xbc = _mm(h1, W_xbc, name="proj_xbc")
    dtr = _mm(h1, W_dtr, name="proj_dt")
    qkv = _mm(h1, W_qkv, name="proj_qkv", out_dtype=BF16)
    y_ssd, states, g_ffn = _ssd_fwd(z, xbc, dtr, cw_ssd, conv_ssd_b, lanes, [packed(P, GATHER_FFN, BF16)],
                                    name="ssd_fwd")
    o_sb, g_mid, g_up = _sb_fwd(qkv, [packed(P, GATHER_MID, BF16), w_up[0].astype(BF16)], name="sb_fwd")
    r_mid = _layout(GATHER_MID)[0]
    W_out = g_mid[:, r_mid["w_out"]:r_mid["w_out"] + 256].reshape(2 * D, D)
    W_mq, W_mk, W_mv, W_mo = [g_mid[:, r_mid[n]:r_mid[n] + 128].reshape(D, D)
                              for n in ("w_mq", "w_mk", "w_mv", "w_mo")]
    W_up = columns(g_up)
    W_down = g_ffn[:, 0:352].reshape(DFF, D)
    W_upg, W_upv = W_up[:, :DFF], W_up[:, DFF:]
    y_sb = _head_norm_fwd(o_sb, sb_norm_w, name="sb_norm")
    ymix = jnp.concatenate([y_ssd, y_sb], axis=1)
    x1 = _mm(ymix, W_out, add=xs, name="proj_out")
    h2 = _norm_fwd(x1, norm_mem_w, name="norm_mem")
    mn = _norm_fwd(mm, norm_memkv_w, name="norm_memkv")
    qm = _mm(h2, W_mq, name="mem_q", out_dtype=BF16)
    km = _mm(mn, W_mk, name="mem_k", out_dtype=BF16)
    vm = _mm(mn, W_mv, name="mem_v", out_dtype=BF16)
    om = _mem_attn_fwd(qm, km, vm, name="mem_attn")
    x2 = _mm(om, W_mo, add=x1, name="mem_o")
    h3 = _norm_fwd(x2, norm_ffn_w, name="norm_ffn")
    ug = _mm(h3, W_upg, name="ffn_up_g")
    uv = _mm(h3, W_upv, name="ffn_up_v")
    f = _glu_fwd(ug, uv, cwg, cwv, cbg, cbv, name="ffn_glu")
    x3 = _mm(f, W_down, add=x2, name="ffn_down")
    dx3, g_nfinal, loss_part = _final(x3, norm_final_w.reshape(1, D), tgt, name="final_loss")

    G = {}
    G["w_down"] = _mm(f, dx3, trans_a=True, name="g_w_down")
    df = _mm(dx3, W_down, trans_b=True, name="d_f")
    dupg, dupv, dcg, dcv = _ffn_bwd(ug, uv, df, cwg, cwv, cbg, cbv, name="ffn_glu_bwd")
    G["w_up"] = jnp.concatenate([_mm(h3, dupg, trans_a=True, name="g_w_up_g"),
                                 _mm(h3, dupv, trans_a=True, name="g_w_up_v")], axis=1)
    G["conv_ffn_w"] = jnp.concatenate([dcg[0:3], dcv[0:3]], axis=1)
    G["conv_ffn_b"] = jnp.concatenate([dcg[3:4], dcv[3:4]], axis=1)
    dh3 = _mm(dupg, W_upg, trans_b=True, name="d_h3_g")
    dh3 = _mm(dupv, W_upv, trans_b=True, add=dh3, name="d_h3_v")
    dx2, G["norm_ffn_w"] = _norm_bwd(x2, norm_ffn_w, dh3, dx3, name="norm_ffn_bwd")
    G["w_mo"] = _mm(om, dx2, trans_a=True, name="g_w_mo")
    dom = _mm(dx2, W_mo, trans_b=True, name="d_om")
    dqm, dkm, dvm = _mem_attn_bwd(qm, km, vm, dom, name="mem_attn_bwd")
    G["w_mq"] = _mm(h2, dqm, trans_a=True, name="g_w_mq")
    G["w_mk"] = _mm(mn, dkm, trans_a=True, name="g_w_mk")
    G["w_mv"] = _mm(mn, dvm, trans_a=True, name="g_w_mv")
    dh2 = _mm(dqm, W_mq, trans_b=True, name="d_h2")
    dmn = _mm(dkm, W_mk, trans_b=True, name="d_mn_k")
    dmn = _mm(dvm, W_mv, trans_b=True, add=dmn, name="d_mn_v")
    _, G["norm_memkv_w"] = _norm_bwd(mm, norm_memkv_w, dmn, None, name="norm_memkv_bwd")
    dx1, G["norm_mem_w"] = _norm_bwd(x1, norm_mem_w, dh2, dx2, name="norm_mem_bwd")
    G["w_out"] = _mm(ymix, dx1, trans_a=True, name="g_w_out")
    dymix = _mm(dx1, W_out, trans_b=True, name="d_ymix")
    do_sb, G["sb_norm_w"] = _head_norm_bwd(o_sb, sb_norm_w, dymix, name="sb_norm_bwd")

    dz, dxbc, ddtr, dlanes, dconv = _ssd_bwd(z, xbc, dtr, states, dymix, cw_ssd, conv_ssd_b, lanes, name="ssd_bwd")
    G["dt_bias"], G["a_log"], G["d_skip"] = [_group_sum(dlanes[i:i + 1]) for i in range(3)]
    G["ssd_norm_w"] = dlanes[3:4]
    G["conv_ssd_w"], G["conv_ssd_b"] = dconv[0:4], dconv[4:5]

    def col_slabs(g, cols):
        return g.reshape(g.shape[0], NDEV, cols).transpose(1, 0, 2).astype(BF16)

    def packed_slabs(names):
        parts = []
        for n in names:
            shp = SHARD[n]
            if shp[-1] == D:
                t = G[n].reshape((NDEV,) + shp)
            else:
                t = _pad_rows(G[n].reshape(shp[0], NDEV, shp[1]).transpose(1, 0, 2).reshape(NDEV, -1),
                              _part_rows(shp))
            parts.append(jnp.pad(t, ((0, 0), (0, _part_rows(shp) - t.shape[1]), (0, 0))))
        used = sum(t.shape[1] for t in parts)
        parts.append(jnp.zeros((NDEV, _layout(names)[1] - used, D), F32))
        return jnp.concatenate(parts, axis=1).astype(BF16)

    dq, dk, dv, recv_packed, recv_up = _sb_bwd(qkv, o_sb, do_sb, [packed_slabs(GRADS_PACKED), col_slabs(G["w_up"], 704)],
                                               name="sb_bwd")
    dqkv = jnp.concatenate([dq, dk, dv], axis=1)
    g_wdt = _mm(h1, ddtr, trans_a=True, name="g_w_dt").reshape(D, 16, 64).sum(axis=2)
    G["w_in"] = jnp.concatenate([_mm(h1, dz, trans_a=True, name="g_w_z"),
                                 _mm(h1, dxbc, trans_a=True, name="g_w_xbc"), g_wdt,
                                 _mm(h1, dqkv, trans_a=True, name="g_w_qkv")], axis=1)
    dh1 = _mm(dz, W_z, trans_b=True, name="d_h1_z")
    dh1 = _mm(dxbc, W_xbc, trans_b=True, add=dh1, name="d_h1_xbc")
    dh1 = _mm(ddtr, W_dtr, trans_b=True, add=dh1, name="d_h1_dt")
    dh1, recv_in = _mm(dqkv, W_qkv, trans_b=True, add=dh1, rides=[col_slabs(G["w_in"], 706)], name="d_h1_qkv")
    dx, G["norm_mix_w"] = _norm_bwd(xs, norm_mix_w, dh1, dx1, name="norm_mix_bwd")
    G["norm_final_w"] = g_nfinal.reshape(D)

    small_g = _pack_tight([G[n] for n, _ in SMALL] + [loss_part], SMALL_ROWS)
    (parts_small,) = _exchange([small_g], [True], name="exchange_grads")
    outs_packed = _adamw(recv_packed, packed(P, GRADS_PACKED), packed(M, GRADS_PACKED), packed(V, GRADS_PACKED),
                         name="adamw_packed")
    outs_up = _adamw(recv_up, w_up[0], m_w_up[0], v_w_up[0], name="adamw_w_up")
    outs_in = _adamw(recv_in, w_in[0], m_w_in[0], v_w_in[0], name="adamw_w_in")
    outs_small = _adamw(parts_small, _pack_tight([P[n] for n, _ in SMALL], SMALL_ROWS),
                        _pack_tight([M[n] for n, _ in SMALL], SMALL_ROWS),
                        _pack_tight([V[n] for n, _ in SMALL], SMALL_ROWS), name="adamw_replicated")

    res = {}
    for i, kind in enumerate(("grad", "delta", "new_m", "new_v")):
        for n, val in zip(GRADS_PACKED, _unpack(outs_packed[i], [SHARD[n] for n in GRADS_PACKED])):
            res[kind, n] = val.reshape((1,) + SHARD[n])
        res[kind, "w_up"] = outs_up[i].reshape((1,) + SHARD["w_up"])
        res[kind, "w_in"] = outs_in[i].reshape((1,) + SHARD["w_in"])
        for (n, shp), val in zip(SMALL, _unpack(outs_small[i], small_shapes, _tight_rows)):
            res[kind, n] = val
    loss = outs_small[0][LOSS_ROW, 0]
    out = [loss, dx.reshape(1, -1, D)]
    for kind in ("grad", "delta", "new_m", "new_v"):
        out += [res[kind, n] for n in ORDER]
    return tuple(out)
```

```python
import functools
import math

import jax
import jax.numpy as jnp
from jax import lax
from jax.experimental import pallas as pl
from jax.experimental.pallas import tpu as pltpu

F32 = jnp.float32
BF16 = jnp.bfloat16

D = 1024
NDEV = 8
EPS = 1e-6
SSD_CHUNK = 128
HALO = 8
VMEM_LIMIT = 56 * 2**20

ADAM_LR, ADAM_B1, ADAM_B2, ADAM_EPS, ADAM_WD, ADAM_STEP = 0.001, 0.9, 0.999, 1e-08, 0.01, 10


def _cp(*sem):
    return pltpu.CompilerParams(dimension_semantics=sem, vmem_limit_bytes=VMEM_LIMIT)


def _tile(n, cap, mult):
    if n <= cap:
        return n
    for d in range(cap - cap % mult, 0, -mult):
        if n % d == 0:
            return d
    raise ValueError(f"no tile for {n}")


def _sigmoid(x):
    return 1.0 / (1.0 + jnp.exp(-x))


def _silu(x):
    return x * _sigmoid(x)


def _softplus(x):
    return jnp.maximum(x, 0.0) + jnp.log(1.0 + jnp.exp(-jnp.abs(x)))


def _terms(x, n):
    out = []
    r = x.astype(F32)
    for i in range(n):
        h = r.astype(BF16)
        out.append(h)
        if i + 1 < n:
            r = r - h.astype(F32)
    return out


_DIMS = {"nn": ((1,), (0,)), "nt": ((1,), (1,)), "tn": ((0,), (0,))}


def _dot_raw(form, a, b, ta, tb):
    acc = None
    for ai in _terms(a, ta):
        for bi in _terms(b, tb):
            d = lax.dot_general(ai, bi, (_DIMS[form], ((), ())), preferred_element_type=F32)
            acc = d if acc is None else acc + d
    return acc


@functools.lru_cache(maxsize=None)
def _dot_fn(form, ta, tb):
    @jax.custom_vjp
    def f(a, b):
        return _dot_raw(form, a, b, ta, tb)

    def fwd(a, b):
        return f(a, b), (a, b)

    def bwd(res, ct):
        a, b = res
        if form == "nn":
            return _dot_fn("nt", ta, tb)(ct, b), _dot_fn("tn", ta, tb)(a, ct)
        if form == "nt":
            return _dot_fn("nn", ta, tb)(ct, b), _dot_fn("tn", tb, ta)(ct, a)
        return _dot_fn("nt", tb, ta)(b, ct), _dot_fn("nn", ta, tb)(a, ct)

    f.defvjp(fwd, bwd)
    return f


def _dot(form, a, b, ta=1, tb=1):
    return _dot_fn(form, ta, tb)(a, b)


@functools.lru_cache(maxsize=None)
def _take_fn(axis, idx):
    @jax.custom_vjp
    def f(x):
        return x[:, idx:idx + 1] if axis == 1 else x[idx:idx + 1, :]

    def fwd(x):
        return f(x), x.shape

    def bwd(shape, ct):
        io = lax.broadcasted_iota(jnp.int32, shape, axis)
        return (jnp.where(io == idx, jnp.broadcast_to(ct, shape), 0.0),)

    f.defvjp(fwd, bwd)
    return f


@functools.lru_cache(maxsize=None)
def _split_fn(width, n):
    @jax.custom_vjp
    def f(x):
        return tuple(x[:, i * width:(i + 1) * width] for i in range(n))

    def fwd(x):
        return f(x), None

    def bwd(_, cts):
        return (jnp.concatenate(list(cts), axis=1),)

    f.defvjp(fwd, bwd)
    return f


def _split(x, width):
    return _split_fn(width, x.shape[1] // width)(x)


def _iota(shape, axis):
    return lax.broadcasted_iota(jnp.int32, shape, axis)


MM_VMEM_BUDGET = 44 * 2**20


def _mm_tiles(m, n, kt, trans_a, a_bytes, b_bytes, out_bytes, add_bytes):
    tn = _tile(n, 1536, 128)
    for tm_cap in (1408, 1024, 512, 256, 128):
        tm = _tile(m, tm_cap, 128 if trans_a else 8)
        for tk_cap in (kt, 4096, 2048, 1024, 512):
            tk = _tile(kt, tk_cap, 128)
            blocks = tm * tk * a_bytes + tk * tn * b_bytes + tm * tn * (out_bytes + add_bytes)
            if 2 * blocks + (tm * tn * 4 if tk < kt else 0) <= MM_VMEM_BUDGET:
                return tm, tn, tk
    raise ValueError(f"no matmul tiling for {(m, n, kt)}")


def _mm(a, b, *, name, add=None, trans_a=False, trans_b=False, out_dtype=F32, rides=()):
    assert not (trans_a and trans_b)
    if trans_a:
        kt, m = a.shape
    else:
        m, kt = a.shape
    n, kt2 = b.shape if trans_b else b.shape[::-1]
    assert kt == kt2, (a.shape, b.shape)
    tm, tn, tk = _mm_tiles(m, n, kt, trans_a, a.dtype.itemsize, b.dtype.itemsize,
                           jnp.dtype(out_dtype).itemsize, 0 if add is None else add.dtype.itemsize)
    nk = kt // tk
    grid = (m // tm, n // tn, nk)
    rd = _Rides(rides, [False] * len(rides))
    n_in = 2 if add is None else 3

    def body(*all_refs):
        ins, (o_ref,), scratch, handles = rd.split(all_refs, n_in, 1, 1 if nk > 1 else 0)
        refs = (*ins, o_ref, *scratch)
        if rides:
            ids = [pl.program_id(ax) for ax in range(3)]
            rd.run(handles, (ids[0] == 0) & (ids[1] == 0) & (ids[2] == 0),
                   (ids[0] == grid[0] - 1) & (ids[1] == grid[1] - 1) & (ids[2] == grid[2] - 1))
        if add is None:
            a_ref, b_ref, o_ref = refs[:3]
        else:
            a_ref, b_ref, add_ref, o_ref = refs[:4]
        k = pl.program_id(2)
        av = a_ref[...].astype(BF16)
        bv = b_ref[...].astype(BF16)
        dims = _DIMS["tn" if trans_a else "nt" if trans_b else "nn"]
        d = lax.dot_general(av, bv, (dims, ((), ())), preferred_element_type=F32)

        def finish(r):
            if add is not None:
                r = r + add_ref[...]
            o_ref[...] = r.astype(out_dtype)

        if nk == 1:
            finish(d)
        else:
            acc = refs[-1]

            @pl.when(k == 0)
            def _():
                acc[...] = d

            @pl.when((k > 0) & (k < nk - 1))
            def _():
                acc[...] += d

            @pl.when(k == nk - 1)
            def _():
                finish(acc[...] + d)

    a_spec = (pl.BlockSpec((tk, tm), lambda i, j, k: (k, i)) if trans_a
              else pl.BlockSpec((tm, tk), lambda i, j, k: (i, k)))
    b_spec = (pl.BlockSpec((tn, tk), lambda i, j, k: (j, k)) if trans_b
              else pl.BlockSpec((tk, tn), lambda i, j, k: (k, j)))
    in_specs = [a_spec, b_spec]
    args = [a, b]
    if add is not None:
        in_specs.append(pl.BlockSpec((tm, tn), lambda i, j, k: (i, j)))
        args.append(add)
    out = pl.pallas_call(
        body, name=name, grid=grid,
        in_specs=in_specs + rd.in_specs,
        out_specs=[pl.BlockSpec((tm, tn), lambda i, j, k: (i, j))] + rd.out_specs,
        out_shape=[jax.ShapeDtypeStruct((m, n), out_dtype)] + rd.out_shape,
        scratch_shapes=([pltpu.VMEM((tm, tn), F32)] if nk > 1 else []) + rd.scratch,
        compiler_params=_cp(*(("arbitrary",) * 3 if rides else ("parallel", "parallel", "arbitrary"))),
    )(*args, *rides)
    return out if rides else out[0]


def _rstd(x):
    return lax.rsqrt(jnp.mean(x * x, axis=-1, keepdims=True) + EPS)


def _norm_fwd(x, w, *, name):
    s = x.shape[0]
    tm = _tile(s, 512, 8)

    def body(x_ref, w_ref, o_ref):
        xv = x_ref[...]
        o_ref[...] = (xv * _rstd(xv) * w_ref[...]).astype(BF16)

    return pl.pallas_call(
        body, name=name, grid=(s // tm,),
        in_specs=[pl.BlockSpec((tm, D), lambda i: (i, 0)), pl.BlockSpec((1, D), lambda i: (0, 0))],
        out_specs=pl.BlockSpec((tm, D), lambda i: (i, 0)),
        out_shape=jax.ShapeDtypeStruct((s, D), BF16), compiler_params=_cp("parallel"),
    )(x, w)


def _norm_bwd_math(xv, wv, dy):
    r = _rstd(xv)
    xh = xv * r
    dxh = dy * wv
    dx = r * (dxh - xh * jnp.mean(dxh * xh, axis=-1, keepdims=True))
    dw = jnp.sum(dy * xh, axis=0, keepdims=True)
    return dx, dw


def _norm_bwd(x, w, dy, add, *, name):
    s = x.shape[0]
    tm = _tile(s, 256, 8)

    def body(*refs):
        if add is None:
            x_ref, w_ref, dy_ref, dx_ref, dw_ref = refs
        else:
            x_ref, w_ref, dy_ref, add_ref, dx_ref, dw_ref = refs

        @pl.when(pl.program_id(0) == 0)
        def _():
            dw_ref[...] = jnp.zeros_like(dw_ref)

        dx, dw = _norm_bwd_math(x_ref[...], w_ref[...], dy_ref[...])
        if add is not None:
            dx = dx + add_ref[...]
        dx_ref[...] = dx
        dw_ref[...] += dw

    row = pl.BlockSpec((tm, D), lambda i: (i, 0))
    vec = pl.BlockSpec((1, D), lambda i: (0, 0))
    in_specs = [row, vec, row] + ([row] if add is not None else [])
    args = [x, w, dy] + ([add] if add is not None else [])
    return pl.pallas_call(
        body, name=name, grid=(s // tm,), in_specs=in_specs, out_specs=[row, vec],
        out_shape=[jax.ShapeDtypeStruct((s, D), F32), jax.ShapeDtypeStruct((1, D), F32)],
        compiler_params=_cp("arbitrary"),
    )(*args)


def _final(x3, w, target, *, name):
    s = x3.shape[0]
    tm = _tile(s, 256, 8)

    def body(x_ref, w_ref, t_ref, dx_ref, dw_ref, loss_ref):
        @pl.when(pl.program_id(0) == 0)
        def _():
            dw_ref[...] = jnp.zeros_like(dw_ref)
            loss_ref[...] = jnp.zeros_like(loss_ref)

        xv = x_ref[...]
        wv = w_ref[...]
        y = xv * _rstd(xv) * wv
        err = y - t_ref[...]
        loss_ref[...] += 0.5 * jnp.sum(jnp.mean(err * err, axis=-1, keepdims=True))
        dx, dw = _norm_bwd_math(xv, wv, err * (1.0 / D))
        dx_ref[...] = dx
        dw_ref[...] += dw

    row = pl.BlockSpec((tm, D), lambda i: (i, 0))
    vec = pl.BlockSpec((1, D), lambda i: (0, 0))
    return pl.pallas_call(
        body, name=name, grid=(s // tm,), in_specs=[row, vec, row], out_specs=[row, vec, vec],
        out_shape=[jax.ShapeDtypeStruct((s, D), F32), jax.ShapeDtypeStruct((1, D), F32),
                   jax.ShapeDtypeStruct((1, D), F32)],
        compiler_params=_cp("arbitrary"),
    )(x3, w, target)


def _head_norm_math(o, w):
    lane = _iota((128, 128), 0) // 64
    bd = (lane == _iota((128, 128), 1) // 64).astype(F32)
    outs = []
    for op in _split(o, 128):
        ms = _dot("nn", op * op, bd, 2, 1) * (1.0 / 64)
        outs.append(op * lax.rsqrt(ms + EPS))
    return jnp.concatenate(outs, axis=1) * w


def _head_norm_fwd(o, w, *, name):
    s = o.shape[0]
    tm = _tile(s, 256, 8)

    def body(o_ref, w_ref, y_ref):
        y_ref[...] = _head_norm_math(o_ref[...], w_ref[...]).astype(BF16)

    row = pl.BlockSpec((tm, D), lambda i: (i, 0))
    vec = pl.BlockSpec((1, D), lambda i: (0, 0))
    return pl.pallas_call(
        body, name=name, grid=(s // tm,), in_specs=[row, vec], out_specs=row,
        out_shape=jax.ShapeDtypeStruct((s, D), BF16), compiler_params=_cp("parallel"),
    )(o, w)


def _head_norm_bwd(o, w, dymix, *, name):
    s = o.shape[0]
    tm = _tile(s, 256, 8)

    def body(o_ref, w_ref, dy_ref, do_ref, dw_ref):
        @pl.when(pl.program_id(0) == 0)
        def _():
            dw_ref[...] = jnp.zeros_like(dw_ref)

        _, vjp = jax.vjp(_head_norm_math, o_ref[...], w_ref[...])
        do, dw = vjp(dy_ref[...])
        do_ref[...] = do
        dw_ref[...] += dw

    row = pl.BlockSpec((tm, D), lambda i: (i, 0))
    vec = pl.BlockSpec((1, D), lambda i: (0, 0))
    return pl.pallas_call(
        body, name=name, grid=(s // tm,),
        in_specs=[row, vec, pl.BlockSpec((tm, D), lambda i: (i, 1))], out_specs=[row, vec],
        out_shape=[jax.ShapeDtypeStruct((s, D), F32), jax.ShapeDtypeStruct((1, D), F32)],
        compiler_params=_cp("arbitrary"),
    )(o, w, dymix)


SB_BQ = 256
SB_BK = 256


def _sb_consts():
    r = _iota((SB_BK, SB_BK), 0)
    c = _iota((SB_BK, SB_BK), 1)
    u_excl = (r > c).astype(BF16)
    u_incl = (r >= c).astype(BF16)
    return u_excl, u_incl


SB_LANES = 256
SB_NCH = SB_LANES // 64


def _nt(a, b):
    return lax.dot_general(a, b, (_DIMS["nt"], ((), ())), preferred_element_type=F32)


def _tn(a, b):
    return lax.dot_general(a, b, (_DIMS["tn"], ((), ())), preferred_element_type=F32)


def _nn(a, b):
    return jnp.dot(a, b, preferred_element_type=F32)


def _sb_heads(ref):
    out = []
    for hp in range(SB_LANES // 128):
        v = ref[:, 128 * hp:128 * (hp + 1)]
        first = _iota(v.shape, 1) < 64
        out += [jnp.where(first, v, 0).astype(BF16), jnp.where(first, 0, v).astype(BF16)]
    return out


SB_STRIP = 32


def _neg_abs(x):
    bits = lax.bitcast_convert_type(x, jnp.uint32) | jnp.uint32(0x80000000)
    return lax.bitcast_convert_type(bits, F32)


def _sb_block(ref, j):
    off = pl.multiple_of(j * SB_BK, SB_BK)
    return [ref[pl.ds(off, SB_BK), 128 * hp:128 * (hp + 1)] for hp in range(SB_NCH // 2)]


SB_DEAD = 104.0


def _sb_live(nlrun):
    m = nlrun[0]
    for x in nlrun[1:]:
        m = jnp.minimum(m, x)
    return jnp.min(m) < SB_DEAD


def _sb_strips():
    return [(r, pl.ds(r, SB_STRIP)) for r in range(0, SB_BQ, SB_STRIP)]


def _sb_diag_mask(r):
    return _iota((SB_STRIP, SB_BK), 1) < _iota((SB_STRIP, SB_BK), 0) + r


def _sb_soft(z, mask):
    e = jnp.exp(_neg_abs(z))
    nl = jnp.maximum(z, 0.0) + jnp.log(1.0 + e)
    if mask is not None:
        nl = jnp.where(mask, nl, 0.0)
    return e, nl


def _sb_split_to(hl_ref, rows, x):
    hi, lo = _terms(x, 2)
    hl_ref[rows, 0:SB_BK] = hi
    hl_ref[rows, SB_BK:2 * SB_BK] = lo


def _sb_stage_soft(z_ref, nl_ref, diag):
    for r, rows in _sb_strips():
        _, nl = _sb_soft(z_ref[rows, :], _sb_diag_mask(r) if diag else None)
        nl_ref[rows, 0:SB_BK] = nl.astype(BF16)


def _sb_stage_weights(z_ref, c_ref, a_ref, nlrun, diag):
    for r, rows in _sb_strips():
        a = jnp.exp(z_ref[rows, :] - c_ref[rows, :] - nlrun[r:r + SB_STRIP, :])
        if diag:
            a = jnp.where(_sb_diag_mask(r), a, 0.0)
        a_ref[rows, :] = a.astype(BF16)


def _sb_fwd(qkv, rides, *, name):
    s = qkv.shape[0]
    nq = s // SB_BQ
    ng = D // SB_LANES
    assert SB_BQ == SB_BK
    rd = _Rides(rides, [True] * len(rides))

    def body(*refs):
        (q_ref, k_ref, v_ref), (o_ref,), (zbuf, nlbuf, cbuf, abuf), handles = rd.split(refs, 3, 1, 4)
        i = pl.program_id(1)
        step_no = pl.program_id(0) * nq + i
        rd.run(handles, step_no == 0, step_no == ng * nq - 1)

        _, u_incl = _sb_consts()
        lane_a = _iota((SB_BQ, 128), 1) < 64
        qh = [q * 0.125 for q in _sb_heads(q_ref)]

        def tile(j, accs, nlrun, diag):
            kbs = _sb_block(k_ref, j)
            for c in range(SB_NCH):
                zbuf[c] = _nt(qh[c], kbs[c // 2])
            for c in range(SB_NCH):
                _sb_stage_soft(zbuf.at[c], nlbuf.at[c], diag)
                cbuf[c] = _nn(nlbuf[c], u_incl)
            for c in range(SB_NCH):
                _sb_stage_weights(zbuf.at[c], cbuf.at[c], abuf.at[c], nlrun[c], diag)
            nlrun = tuple(nlrun[c] + cbuf[c, :, 0:1] for c in range(SB_NCH))
            vbs = _sb_block(v_ref, j)
            outs = [_nn(abuf[c], vbs[c // 2]) for c in range(SB_NCH)]
            accs = tuple(acc + jnp.where(lane_a, outs[2 * hp], outs[2 * hp + 1]) for hp, acc in enumerate(accs))
            return accs, nlrun

        accs, nlrun = tile(i, (jnp.zeros((SB_BQ, 128), F32),) * (SB_NCH // 2),
                           (jnp.zeros((SB_BQ, 1), F32),) * SB_NCH, True)

        def step(carry):
            j, _, accs, nlrun = carry
            accs, nlrun = tile(j, accs, nlrun, False)
            return j - 1, _sb_live(nlrun), accs, nlrun

        _, _, accs, _ = lax.while_loop(lambda c: (c[0] >= 0) & c[1], step, (i - 1, _sb_live(nlrun), accs, nlrun))
        o_ref[...] = jnp.concatenate(accs, axis=1)

    return pl.pallas_call(
        body, name=name, grid=(ng, nq),
        in_specs=[pl.BlockSpec((SB_BQ, SB_LANES), lambda g, i: (i, g)),
                  pl.BlockSpec((s, SB_LANES), lambda g, i: (0, ng + g)),
                  pl.BlockSpec((s, SB_LANES), lambda g, i: (0, 2 * ng + g)), *rd.in_specs],
        out_specs=[pl.BlockSpec((SB_BQ, SB_LANES), lambda g, i: (i, g)), *rd.out_specs],
        out_shape=[jax.ShapeDtypeStruct((s, D), F32), *rd.out_shape],
        scratch_shapes=[pltpu.VMEM((SB_NCH, SB_BQ, SB_BK), F32), pltpu.VMEM((SB_NCH, SB_BQ, SB_BK), BF16),
                        pltpu.VMEM((SB_NCH, SB_BQ, SB_BK), F32), pltpu.VMEM((SB_NCH, SB_BQ, SB_BK), BF16),
                        *rd.scratch],
        compiler_params=_cp("arbitrary", "arbitrary"),
    )(qkv, qkv, qkv, *rides)


def _sb_bwd(qkv, o, do, rides, *, name):
    s = qkv.shape[0]
    nq = s // SB_BQ
    ng = D // SB_LANES
    nhp = SB_NCH // 2
    rd = _Rides(rides, [False] * len(rides))

    def body(*refs):
        ins, outs, scratch, handles = rd.split(refs, 5, 3, 11)
        q_ref, k_ref, v_ref, o_ref, do_ref = ins
        dq_ref, dk_hbm, dv_hbm = outs
        dk_acc, dv_acc, dk16, dv16, sems, zbuf, gbuf, hl, cbuf, abuf, dzbuf = scratch
        g_idx = pl.program_id(0)
        i = pl.program_id(1)
        step_no = g_idx * nq + i
        rd.run(handles, step_no == 0, step_no == ng * nq - 1)

        @pl.when(i == 0)
        def _():
            dk_acc[...] = jnp.zeros_like(dk_acc)
            dv_acc[...] = jnp.zeros_like(dv_acc)

        _, u_incl = _sb_consts()
        u2 = jnp.concatenate([u_incl, u_incl], axis=0)
        lane_a = _iota((SB_BQ, 128), 1) < 64
        lane_k = _iota((SB_BK, 128), 1) < 64
        qh = [q * 0.125 for q in _sb_heads(q_ref)]
        qf = [q_ref[:, 128 * hp:128 * (hp + 1)] for hp in range(nhp)]
        doh = _sb_heads(do_ref)
        dof = [do_ref[:, 128 * hp:128 * (hp + 1)].astype(BF16) for hp in range(nhp)]
        delta = []
        for hp in range(nhp):
            prod = dof[hp].astype(F32) * o_ref[:, 128 * hp:128 * (hp + 1)]
            delta += [jnp.sum(jnp.where(lane_a, prod, 0.0), axis=1, keepdims=True),
                      jnp.sum(jnp.where(lane_a, 0.0, prod), axis=1, keepdims=True)]

        def pre(slot, j):
            kbs = _sb_block(k_ref, j)
            vbs = _sb_block(v_ref, j)
            for c in range(SB_NCH):
                zbuf[slot, c] = _nt(qh[c], kbs[c // 2])
                gbuf[slot, c] = _nt(doh[c], vbs[c // 2])

        def stage_g(c, slot):
            for _, rows in _sb_strips():
                g = abuf[slot, c, rows, :].astype(F32) * gbuf[slot, c, rows, :]
                gbuf[slot, c, rows, :] = g
                _sb_split_to(hl.at[c], rows, g)

        def stage_dz(c, slot, grun, diag):
            for r, rows in _sb_strips():
                z = zbuf[slot, c, rows, :]
                g = gbuf[slot, c, rows, :]
                cs = (delta[c] - grun)[r:r + SB_STRIP, :] - cbuf[c, rows, :]
                sig = 1.0 / (1.0 + jnp.exp(-z))
                dz = g - (g + cs) * sig
                if diag:
                    dz = jnp.where(_sb_diag_mask(r), dz, 0.0)
                dzbuf[slot, c, rows, :] = dz.astype(BF16)

        def chain(slot, nlrun, grun, diag):
            for c in range(SB_NCH):
                _sb_stage_soft(zbuf.at[slot, c], hl.at[c], diag)
                cbuf[c] = _nn(hl[c, :, 0:SB_BK], u_incl)
            nl_tot = []
            for c in range(SB_NCH):
                _sb_stage_weights(zbuf.at[slot, c], cbuf.at[c], abuf.at[slot, c], nlrun[c], diag)
                nl_tot.append(cbuf[c, :, 0:1])
                stage_g(c, slot)
                cbuf[c] = _nn(hl[c], u2)
            g_tot = []
            for c in range(SB_NCH):
                stage_dz(c, slot, grun[c], diag)
                g_tot.append(cbuf[c, :, 0:1])
            return (tuple(a + b for a, b in zip(nlrun, nl_tot)), tuple(a + b for a, b in zip(grun, g_tot)))

        def post(slot, j, dqs):
            off = pl.multiple_of(j * SB_BK, SB_BK)
            kbs = _sb_block(k_ref, j)
            dq_t = [_nn(dzbuf[slot, c], kbs[c // 2]) for c in range(SB_NCH)]
            dk_t = [_tn(dzbuf[slot, c], qf[c // 2]) for c in range(SB_NCH)]
            dv_t = [_tn(abuf[slot, c], dof[c // 2]) for c in range(SB_NCH)]
            for hp in range(nhp):
                cols = slice(128 * hp, 128 * (hp + 1))
                dk_acc[pl.ds(off, SB_BK), cols] += 0.125 * jnp.where(lane_k, dk_t[2 * hp], dk_t[2 * hp + 1])
                dv_acc[pl.ds(off, SB_BK), cols] += jnp.where(lane_k, dv_t[2 * hp], dv_t[2 * hp + 1])
            return tuple(dq + jnp.where(lane_a, dq_t[2 * hp], dq_t[2 * hp + 1]) for hp, dq in enumerate(dqs))

        def tile(j, dqs, nlrun, grun, diag):
            pre(0, j)
            nlrun, grun = chain(0, nlrun, grun, diag)
            return post(0, j, dqs), nlrun, grun

        zero = (jnp.zeros((SB_BQ, 1), F32),) * SB_NCH
        dqs, nlrun, grun = tile(i, (jnp.zeros((SB_BQ, 128), F32),) * nhp, zero, zero, True)

        def step(carry):
            j, _, dqs, nlrun, grun = carry
            dqs, nlrun, grun = tile(j, dqs, nlrun, grun, False)
            return j - 1, _sb_live(nlrun), dqs, nlrun, grun

        carry = lax.while_loop(lambda c: (c[0] >= 0) & c[1], step, (i - 1, _sb_live(nlrun), dqs, nlrun, grun))
        dq_ref[...] = (0.125 * jnp.concatenate(carry[2], axis=1)).astype(BF16)

        def out_copies(g):
            cols = pl.ds(pl.multiple_of(g * SB_LANES, SB_LANES), SB_LANES)
            return (pltpu.make_async_copy(dk16, dk_hbm.at[:, cols], sems.at[0]),
                    pltpu.make_async_copy(dv16, dv_hbm.at[:, cols], sems.at[1]))

        @pl.when((i == nq - 1) & (g_idx > 0))
        def _():
            for cp in out_copies(g_idx - 1):
                cp.wait()

        @pl.when(i == nq - 1)
        def _():
            def narrow(r, carry):
                rows = pl.ds(pl.multiple_of(r * SB_BK, SB_BK), SB_BK)
                dk16[rows, :] = dk_acc[rows, :].astype(BF16)
                dv16[rows, :] = dv_acc[rows, :].astype(BF16)
                return carry

            lax.fori_loop(0, s // SB_BK, narrow, 0)
            for cp in out_copies(g_idx):
                cp.start()

        @pl.when((i == nq - 1) & (g_idx == ng - 1))
        def _():
            for cp in out_copies(g_idx):
                cp.wait()

    qblk = pl.BlockSpec((SB_BQ, SB_LANES), lambda g, i: (i, g))
    hbm = pl.BlockSpec(memory_space=pl.ANY)
    return pl.pallas_call(
        body, name=name, grid=(ng, nq),
        in_specs=[qblk, pl.BlockSpec((s, SB_LANES), lambda g, i: (0, ng + g)),
                  pl.BlockSpec((s, SB_LANES), lambda g, i: (0, 2 * ng + g)), qblk, qblk, *rd.in_specs],
        out_specs=[qblk, hbm, hbm, *rd.out_specs],
        out_shape=[jax.ShapeDtypeStruct((s, D), BF16)] * 3 + rd.out_shape,
        scratch_shapes=[pltpu.VMEM((s, SB_LANES), F32), pltpu.VMEM((s, SB_LANES), F32),
                        pltpu.VMEM((s, SB_LANES), BF16), pltpu.VMEM((s, SB_LANES), BF16),
                        pltpu.SemaphoreType.DMA((2,)),
                        pltpu.VMEM((1, SB_NCH, SB_BQ, SB_BK), F32), pltpu.VMEM((1, SB_NCH, SB_BQ, SB_BK), F32),
                        pltpu.VMEM((SB_NCH, SB_BQ, 2 * SB_BK), BF16), pltpu.VMEM((SB_NCH, SB_BQ, SB_BK), F32),
                        pltpu.VMEM((1, SB_NCH, SB_BQ, SB_BK), BF16), pltpu.VMEM((1, SB_NCH, SB_BQ, SB_BK), BF16),
                        *rd.scratch],
        compiler_params=_cp("arbitrary", "arbitrary"),
    )(qkv, qkv, qkv, o, do, *rides)


def _ssd_core(z, xpre, dtr, state, dtb, alog, dsk, nw):
    L = SSD_CHUNK
    xa = _silu(xpre)
    pieces = _split(xa, 128)
    xs = jnp.concatenate(pieces[:8], axis=1)
    bm, cm = pieces[8:10], pieces[10:12]
    dt = _softplus(dtr + dtb)
    a = dt * (-jnp.exp(alog))
    tri = (_iota((L, L), 0) >= _iota((L, L), 1)).astype(F32)
    a_cs = _dot("nn", tri, a, 1, 3)
    xc = xs * dt
    tril = _iota((L, L), 0) >= _iota((L, L), 1)
    lane_a = _iota((L, 128), 1) < 64
    acs_p = _split(a_cs, 128)
    xc_p = _split(xc, 128)
    ys, new_states = [], []
    for g in range(2):
        cb = _dot("nt", cm[g], bm[g])
        for pp in range(4):
            pair = 4 * g + pp
            acs = acs_p[pair]
            acs_t = acs.T
            xcp = xc_p[pair]
            st = state[pair]
            heads = []
            for hh in range(2):
                col = _take_fn(1, 64 * hh)(acs)
                row = _take_fn(0, 64 * hh)(acs_t)
                seg = col - row
                lm = jnp.where(tril, jnp.exp(jnp.where(tril, seg, 0.0)), 0.0)
                heads.append(_dot("nn", cb * lm, xcp))
            ydiag = jnp.where(lane_a, heads[0], heads[1])
            last = _take_fn(0, L - 1)(acs)
            snew = _dot("tn", xcp * jnp.exp(last - acs), bm[g])
            new_states.append(st * jnp.exp(_take_fn(1, L - 1)(acs_t)) + snew)
            yoff = _dot("nt", cm[g], st) * jnp.exp(acs)
            ys.append(ydiag + yoff)
    y = jnp.concatenate(ys, axis=1) + xs * dsk
    yg = y * _silu(z)
    outs = []
    for v in _split(yg, 512):
        outs.append(v * lax.rsqrt(jnp.mean(v * v, axis=-1, keepdims=True) + EPS))
    return jnp.concatenate(outs, axis=1) * nw, tuple(new_states)


XBC = 1536


def _ssd_conv(ext_ref, cw, cb):
    acc = cb
    for k in range(4):
        acc = acc + cw[k:k + 1, :] * ext_ref[pl.ds(HALO - 3 + k, SSD_CHUNK), :]
    return acc


def _ssd_fwd(z, xbc, dtr, cw, cb, lanes, rides, *, name):
    s = z.shape[0]
    L = SSD_CHUNK
    nc = s // L
    rd = _Rides(rides, [True] * len(rides))

    def body(*refs):
        ins, (y_ref, st_ref), (state, ext), handles = rd.split(refs, 7, 2, 2)
        z_ref, x_ref, h_ref, dtr_ref, cw_ref, cb_ref, ln_ref = ins
        c = pl.program_id(0)
        rd.run(handles, c == 0, c == nc - 1)

        @pl.when(c == 0)
        def _():
            state[...] = jnp.zeros_like(state)

        ext[0:HALO, :] = jnp.where(c == 0, 0.0, h_ref[...])
        ext[HALO:, :] = x_ref[...]
        xpre = _ssd_conv(ext, cw_ref[...], cb_ref[...])
        st_ref[0] = state[...]
        st_in = tuple(state[p] for p in range(8))
        yn, st_out = _ssd_core(z_ref[...], xpre, dtr_ref[...], st_in,
                               ln_ref[0:1, :], ln_ref[1:2, :], ln_ref[2:3, :], ln_ref[3:4, :])
        y_ref[...] = yn.astype(BF16)
        for p in range(8):
            state[p] = st_out[p]

    return pl.pallas_call(
        body, name=name, grid=(nc,),
        in_specs=[pl.BlockSpec((L, D), lambda c: (c, 0)),
                  pl.BlockSpec((L, XBC), lambda c: (c, 0)),
                  pl.BlockSpec((HALO, XBC), lambda c: (jnp.maximum(c * (L // HALO) - 1, 0), 0)),
                  pl.BlockSpec((L, D), lambda c: (c, 0)),
                  pl.BlockSpec((4, XBC), lambda c: (0, 0)),
                  pl.BlockSpec((1, XBC), lambda c: (0, 0)),
                  pl.BlockSpec((8, D), lambda c: (0, 0)), *rd.in_specs],
        out_specs=[pl.BlockSpec((L, D), lambda c: (c, 0)),
                   pl.BlockSpec((1, 8, 128, 128), lambda c: (c, 0, 0, 0)), *rd.out_specs],
        out_shape=[jax.ShapeDtypeStruct((s, D), BF16), jax.ShapeDtypeStruct((nc, 8, 128, 128), F32),
                   *rd.out_shape],
        scratch_shapes=[pltpu.VMEM((8, 128, 128), F32), pltpu.VMEM((L + HALO, XBC), F32), *rd.scratch],
        compiler_params=_cp("arbitrary"),
    )(z, xbc, xbc, dtr, cw, cb, lanes, *rides)


def _ssd_bwd(z, xbc, dtr, states, dymix, cw, cb, lanes, *, name):
    s = z.shape[0]
    L = SSD_CHUNK
    nc = s // L

    def body(z_ref, x_ref, h_ref, dtr_ref, st_ref, dy_ref, cw_ref, cb_ref, ln_ref,
             dz_ref, dx_ref, ddt_ref, dln_ref, dcv_ref, dstate, ext, dext):
        i = pl.program_id(0)
        c = nc - 1 - i

        @pl.when(i == 0)
        def _():
            dstate[...] = jnp.zeros_like(dstate)
            dext[...] = jnp.zeros_like(dext)
            dln_ref[...] = jnp.zeros_like(dln_ref)
            dcv_ref[...] = jnp.zeros_like(dcv_ref)

        ext[0:HALO, :] = jnp.where(c == 0, 0.0, h_ref[...])
        ext[HALO:, :] = x_ref[...]
        cwv = cw_ref[...]
        xpre = _ssd_conv(ext, cwv, cb_ref[...])
        st_in = tuple(st_ref[0, p] for p in range(8))
        _, vjp = jax.vjp(_ssd_core, z_ref[...], xpre, dtr_ref[...], st_in,
                         ln_ref[0:1, :], ln_ref[1:2, :], ln_ref[2:3, :], ln_ref[3:4, :])
        dz, dxpre, ddtr, dst, d0, d1, d2, d3 = vjp((dy_ref[...], tuple(dstate[p] for p in range(8))))
        for p in range(8):
            dstate[p] = dst[p]
        dz_ref[...] = dz.astype(BF16)
        ddt_ref[...] = ddtr.astype(BF16)
        dln_ref[0:4, :] += jnp.concatenate([d0, d1, d2, d3], axis=0)
        dext[0:L, :] = dxpre
        xcur = x_ref[...]
        dx = jnp.zeros((L, XBC), F32)
        rows = []
        for k in range(4):
            shifted = dext[pl.ds(3 - k, L), :]
            dx = dx + cwv[k:k + 1, :] * shifted
            rows.append(jnp.sum(shifted * xcur, axis=0, keepdims=True))
        rows.append(jnp.sum(dxpre, axis=0, keepdims=True))
        dx_ref[...] = dx.astype(BF16)
        dcv_ref[0:5, :] += jnp.concatenate(rows, axis=0)
        dext[L:L + HALO, :] = dxpre[0:HALO, :]

    rev = lambda i: (nc - 1 - i, 0)
    return pl.pallas_call(
        body, name=name, grid=(nc,),
        in_specs=[pl.BlockSpec((L, D), rev),
                  pl.BlockSpec((L, XBC), rev),
                  pl.BlockSpec((HALO, XBC), lambda i: (jnp.maximum((nc - 1 - i) * (L // HALO) - 1, 0), 0)),
                  pl.BlockSpec((L, D), rev),
                  pl.BlockSpec((1, 8, 128, 128), lambda i: (nc - 1 - i, 0, 0, 0)),
                  pl.BlockSpec((L, D), rev),
                  pl.BlockSpec((4, XBC), lambda i: (0, 0)),
                  pl.BlockSpec((1, XBC), lambda i: (0, 0)),
                  pl.BlockSpec((8, D), lambda i: (0, 0))],
        out_specs=[pl.BlockSpec((L, D), rev), pl.BlockSpec((L, XBC), rev), pl.BlockSpec((L, D), rev),
                   pl.BlockSpec((8, D), lambda i: (0, 0)), pl.BlockSpec((8, XBC), lambda i: (0, 0))],
        out_shape=[jax.ShapeDtypeStruct((s, D), BF16), jax.ShapeDtypeStruct((s, XBC), BF16),
                   jax.ShapeDtypeStruct((s, D), BF16), jax.ShapeDtypeStruct((8, D), F32),
                   jax.ShapeDtypeStruct((8, XBC), F32)],
        scratch_shapes=[pltpu.VMEM((8, 128, 128), F32), pltpu.VMEM((L + HALO, XBC), F32),
                        pltpu.VMEM((L + HALO, XBC), F32)],
        compiler_params=_cp("arbitrary"),
    )(z, xbc, xbc, dtr, states, dymix, cw, cb, lanes)


def _mem_attn_math(q, k, v):
    outs = []
    for qh, kh, vh in zip(_split(q, 256), _split(k, 256), _split(v, 256)):
        sc = _dot("nt", qh, kh) * (1.0 / 16.0)
        e = jnp.exp(sc - lax.stop_gradient(jnp.max(sc, axis=-1, keepdims=True)))
        p = e / jnp.sum(e, axis=-1, keepdims=True)
        outs.append(_dot("nn", p, vh))
    return jnp.concatenate(outs, axis=1)


def _mem_attn_fwd(q, k, v, *, name):
    s, m = q.shape[0], k.shape[0]
    tm = _tile(s, 256, 8)

    def body(q_ref, k_ref, v_ref, o_ref):
        o_ref[...] = _mem_attn_math(q_ref[...].astype(F32), k_ref[...].astype(F32),
                                    v_ref[...].astype(F32)).astype(BF16)

    row = pl.BlockSpec((tm, D), lambda i: (i, 0))
    kv = pl.BlockSpec((m, D), lambda i: (0, 0))
    return pl.pallas_call(
        body, name=name, grid=(s // tm,), in_specs=[row, kv, kv], out_specs=row,
        out_shape=jax.ShapeDtypeStruct((s, D), BF16), compiler_params=_cp("parallel"),
    )(q, k, v)


def _mem_attn_bwd(q, k, v, do, *, name):
    s, m = q.shape[0], k.shape[0]
    tm = _tile(s, 256, 8)

    def body(q_ref, k_ref, v_ref, do_ref, dq_ref, dk_ref, dv_ref):
        @pl.when(pl.program_id(0) == 0)
        def _():
            dk_ref[...] = jnp.zeros_like(dk_ref)
            dv_ref[...] = jnp.zeros_like(dv_ref)

        _, vjp = jax.vjp(_mem_attn_math, q_ref[...].astype(F32), k_ref[...].astype(F32),
                         v_ref[...].astype(F32))
        dq, dk, dv = vjp(do_ref[...])
        dq_ref[...] = dq.astype(BF16)
        dk_ref[...] += dk
        dv_ref[...] += dv

    row = pl.BlockSpec((tm, D), lambda i: (i, 0))
    kv = pl.BlockSpec((m, D), lambda i: (0, 0))
    return pl.pallas_call(
        body, name=name, grid=(s // tm,), in_specs=[row, kv, kv, row], out_specs=[row, kv, kv],
        out_shape=[jax.ShapeDtypeStruct((s, D), BF16), jax.ShapeDtypeStruct((m, D), F32),
                   jax.ShapeDtypeStruct((m, D), F32)],
        compiler_params=_cp("arbitrary"),
    )(q, k, v, do)


DFF = 2816
FFN_TC = 1408
FFN_TM = 256


FFN_CHUNKS = tuple((c, min(512, FFN_TC - c)) for c in range(0, FFN_TC, 512))


def _rows8(ref, r, cols):
    return ref[pl.ds(pl.multiple_of(r, HALO), HALO), cols]


def _shift_down(prev, cur, s):
    return jnp.where(_iota(cur.shape, 0) < s, pltpu.roll(prev, s, 0), pltpu.roll(cur, s, 0))


def _shift_up(cur, nxt, s):
    return jnp.where(_iota(cur.shape, 0) >= HALO - s, pltpu.roll(nxt, HALO - s, 0), pltpu.roll(cur, HALO - s, 0))


def _ffn_conv_strip(ext_ref, r, cols, cw, cb):
    prev, cur = _rows8(ext_ref, r, cols), _rows8(ext_ref, r + HALO, cols)
    return cb + cw[0:1, :] * _shift_down(prev, cur, 2) + cw[1:2, :] * _shift_down(prev, cur, 1) + cw[2:3, :] * cur


def _ffn_specs(s):
    tm, tc = FFN_TM, FFN_TC
    blk = pl.BlockSpec((tm, tc), lambda i, j: (i, j))
    halo = pl.BlockSpec((HALO, tc), lambda i, j: (jnp.maximum(i * (tm // HALO) - 1, 0), j))
    cw = pl.BlockSpec((3, tc), lambda i, j: (0, j))
    cb = pl.BlockSpec((1, tc), lambda i, j: (0, j))
    return tm, tc, blk, halo, cw, cb


def _glu_fwd(ug, uv, cwg, cwv, cbg, cbv, *, name):
    s = ug.shape[0]
    tm, tc, blk, halo, cw, cb = _ffn_specs(s)

    def body(g_ref, gh_ref, v_ref, vh_ref, cwg_ref, cwv_ref, cbg_ref, cbv_ref, f_ref, eg, ev):
        first = pl.program_id(0) == 0
        eg[0:HALO, :] = jnp.where(first, 0.0, gh_ref[...])
        eg[HALO:, :] = g_ref[...]
        ev[0:HALO, :] = jnp.where(first, 0.0, vh_ref[...])
        ev[HALO:, :] = v_ref[...]
        cwgv, cwvv, cbgv, cbvv = cwg_ref[...], cwv_ref[...], cbg_ref[...], cbv_ref[...]

        def step(t, carry):
            for c0, w in FFN_CHUNKS:
                cols = slice(c0, c0 + w)
                outs = []
                for h in range(2):
                    r = t * 16 + HALO * h
                    g = _ffn_conv_strip(eg, r, cols, cwgv[:, cols], cbgv[:, cols])
                    v = _ffn_conv_strip(ev, r, cols, cwvv[:, cols], cbvv[:, cols])
                    outs.append(_silu(g) * v)
                f_ref[pl.ds(pl.multiple_of(t * 16, 16), 16), cols] = jnp.concatenate(outs, axis=0).astype(BF16)
            return carry

        lax.fori_loop(0, tm // 16, step, 0)

    return pl.pallas_call(
        body, name=name, grid=(s // tm, DFF // tc),
        in_specs=[blk, halo, blk, halo, cw, cw, cb, cb], out_specs=blk,
        out_shape=jax.ShapeDtypeStruct((s, DFF), BF16),
        scratch_shapes=[pltpu.VMEM((tm + HALO, tc), F32)] * 2,
        compiler_params=_cp("parallel", "parallel"),
    )(ug, ug, uv, uv, cwg, cwv, cbg, cbv)


def _ffn_bwd(ug, uv, df, cwg, cwv, cbg, cbv, *, name):
    s = ug.shape[0]
    tm, tc = FFN_TM, FFN_TC
    nb = s // tm
    rows_ext = tm + HALO

    def body(g_ref, gp_ref, gn_ref, v_ref, vp_ref, vn_ref, df_ref, dfn_ref, cwg_ref, cwv_ref, cbg_ref, cbv_ref,
             dxg_ref, dxv_ref, dcg_ref, dcv_ref, eg, ev, edf, edg, edv, accg, accv):
        i = pl.program_id(1)
        first, last = i == 0, i == nb - 1

        @pl.when(first)
        def _():
            dcg_ref[...] = jnp.zeros_like(dcg_ref)
            dcv_ref[...] = jnp.zeros_like(dcv_ref)

        for e, prev, main, nxt in ((eg, gp_ref, g_ref, gn_ref), (ev, vp_ref, v_ref, vn_ref)):
            e[0:HALO, :] = jnp.where(first, 0.0, prev[...])
            e[HALO:HALO + tm, :] = main[...]
            e[HALO + tm:, :] = jnp.where(last, 0.0, nxt[...])
        edf[0:tm, :] = df_ref[...]
        edf[tm:, :] = jnp.where(last, 0.0, dfn_ref[...])
        accg[...] = jnp.zeros_like(accg)
        accv[...] = jnp.zeros_like(accv)
        cwgv, cwvv, cbgv, cbvv = cwg_ref[...], cwv_ref[...], cbg_ref[...], cbv_ref[...]

        def cotangents(t, carry):
            r = t * HALO
            for c0, w in FFN_CHUNKS:
                cols = slice(c0, c0 + w)
                g = _ffn_conv_strip(eg, r, cols, cwgv[:, cols], cbgv[:, cols])
                v = _ffn_conv_strip(ev, r, cols, cwvv[:, cols], cbvv[:, cols])
                dfs = _rows8(edf, r, cols)
                sg = _sigmoid(g)
                edv[pl.ds(pl.multiple_of(r, HALO), HALO), cols] = dfs * g * sg
                edg[pl.ds(pl.multiple_of(r, HALO), HALO), cols] = dfs * v * sg * (1.0 + g * (1.0 - sg))
            return carry

        lax.fori_loop(0, rows_ext // HALO, cotangents, 0)

        def conv_backward(t, carry):
            for c0, w in FFN_CHUNKS:
                cols = slice(c0, c0 + w)
                for edu, e, cw, dx_ref, acc in ((edg, eg, cwgv[:, cols], dxg_ref, accg),
                                                (edv, ev, cwvv[:, cols], dxv_ref, accv)):
                    dxs = []
                    for h in range(2):
                        r = t * 16 + HALO * h
                        cur, nxt = _rows8(edu, r, cols), _rows8(edu, r + HALO, cols)
                        up1, up2 = _shift_up(cur, nxt, 1), _shift_up(cur, nxt, 2)
                        x = _rows8(e, r + HALO, cols)
                        dxs.append(cw[2:3, :] * cur + cw[1:2, :] * up1 + cw[0:1, :] * up2)
                        acc[0, :, cols] += up2 * x
                        acc[1, :, cols] += up1 * x
                        acc[2, :, cols] += cur * x
                        acc[3, :, cols] += cur
                    dx_ref[pl.ds(pl.multiple_of(t * 16, 16), 16), cols] = jnp.concatenate(dxs, axis=0).astype(BF16)
            return carry

        lax.fori_loop(0, tm // 16, conv_backward, 0)
        for acc, dc_ref in ((accg, dcg_ref), (accv, dcv_ref)):
            dc_ref[0:4, :] += jnp.concatenate([jnp.sum(acc[k], axis=0, keepdims=True) for k in range(4)], axis=0)

    blk = pl.BlockSpec((tm, tc), lambda j, i: (i, j))
    nxt = pl.BlockSpec((HALO, tc), lambda j, i: (jnp.minimum((i + 1) * (tm // HALO), s // HALO - 1), j))
    prv = pl.BlockSpec((HALO, tc), lambda j, i: (jnp.maximum(i * (tm // HALO) - 1, 0), j))
    cw = pl.BlockSpec((3, tc), lambda j, i: (0, j))
    cb = pl.BlockSpec((1, tc), lambda j, i: (0, j))
    acc = pl.BlockSpec((8, tc), lambda j, i: (0, j))
    return pl.pallas_call(
        body, name=name, grid=(DFF // tc, nb),
        in_specs=[blk, prv, nxt, blk, prv, nxt, blk, nxt, cw, cw, cb, cb],
        out_specs=[blk, blk, acc, acc],
        out_shape=[jax.ShapeDtypeStruct((s, DFF), BF16)] * 2 + [jax.ShapeDtypeStruct((8, DFF), F32)] * 2,
        scratch_shapes=[pltpu.VMEM((tm + 2 * HALO, tc), F32)] * 2 + [pltpu.VMEM((rows_ext, tc), F32)] * 3
                       + [pltpu.VMEM((4, HALO, tc), F32)] * 2,
        compiler_params=_cp("parallel", "arbitrary"),
    )(ug, ug, ug, uv, uv, uv, df, df, cwg, cwv, cbg, cbv)


MESH = pl.DeviceIdType.MESH


def _all_gather(arrs, *, name):
    n = len(arrs)

    def body(*refs):
        x_refs, out_refs = refs[:n], refs[n:2 * n]
        send_sems, recv_sems, local_sems = refs[2 * n:]
        x, y, c = lax.axis_index("x"), lax.axis_index("y"), lax.axis_index("c")
        me, sibling = (x, y, c), (x, y, 1 - c)
        chips = [(1 - x, y), (x, 1 - y), (1 - x, 1 - y)]

        def blk(a, dev):
            return out_refs[a].at[4 * dev[0] + 2 * dev[1] + dev[2]]

        def copy(a, k, block, to, src=None):
            return pltpu.make_async_remote_copy(
                src_ref=blk(a, block) if src is None else src, dst_ref=blk(a, block),
                send_sem=send_sems.at[7 * a + k], recv_sem=recv_sems.at[7 * a + k],
                device_id=to, device_id_type=MESH)

        started = []
        mine = []
        for a in range(n):
            cp = pltpu.make_async_copy(x_refs[a], blk(a, me), local_sems.at[a])
            cp.start()
            mine.append(cp)
            first = [copy(a, 0, me, sibling, src=x_refs[a])]
            first += [copy(a, 1 + j, me, (*chip, c), src=x_refs[a]) for j, chip in enumerate(chips)]
            for cp in first:
                cp.start()
            started += first
        for a in range(n):
            for j, chip in enumerate(chips):
                copy(a, 1 + j, (*chip, c), me).wait_recv()
                fwd = copy(a, 4 + j, (*chip, c), sibling)
                fwd.start()
                started.append(fwd)
        for a in range(n):
            copy(a, 0, sibling, me).wait_recv()
            for j, chip in enumerate(chips):
                copy(a, 4 + j, (*chip, 1 - c), me).wait_recv()
        for cp in started:
            cp.wait_send()
        for cp in mine:
            cp.wait()

    any_spec = pl.BlockSpec(memory_space=pl.ANY)
    return pl.pallas_call(
        body, name=name,
        in_specs=[any_spec] * n, out_specs=[any_spec] * n,
        out_shape=[jax.ShapeDtypeStruct((NDEV,) + a.shape, a.dtype) for a in arrs],
        scratch_shapes=[pltpu.SemaphoreType.DMA((7 * n,)), pltpu.SemaphoreType.DMA((7 * n,)),
                        pltpu.SemaphoreType.DMA((n,))],
    )(*arrs)


class _Direct:
    SEMS = (pltpu.SemaphoreType.DMA((7,)), pltpu.SemaphoreType.DMA((7,)), pltpu.SemaphoreType.DMA((1,)))

    def __init__(self, src_ref, recv_ref, sems, gather):
        x, y, c = lax.axis_index("x"), lax.axis_index("y"), lax.axis_index("c")
        me = 4 * x + 2 * y + c
        send_sems, recv_sems, local_sem = sems
        src = (lambda pid: src_ref) if gather else (lambda pid: src_ref.at[pid])
        self.mine = pltpu.make_async_copy(src(me), recv_ref.at[me], local_sem.at[0])
        self.copies = []
        for k in range(1, NDEV):
            px = 1 - x if k & 4 else x
            py = 1 - y if k & 2 else y
            pc = 1 - c if k & 1 else c
            self.copies.append(pltpu.make_async_remote_copy(
                src_ref=src(4 * px + 2 * py + pc), dst_ref=recv_ref.at[me],
                send_sem=send_sems.at[k - 1], recv_sem=recv_sems.at[k - 1],
                device_id=(px, py, pc), device_id_type=MESH))

    def start(self):
        self.mine.start()
        for cp in self.copies:
            cp.start()

    def wait(self):
        for cp in self.copies:
            cp.wait_recv()
        for cp in self.copies:
            cp.wait_send()
        self.mine.wait()


def _recv_shape(src, gather):
    return jax.ShapeDtypeStruct(((NDEV,) + src.shape) if gather else src.shape, src.dtype)


class _Rides:
    def __init__(self, rides, gathers):
        self.n = len(rides)
        self.gathers = list(gathers)
        any_spec = pl.BlockSpec(memory_space=pl.ANY)
        self.in_specs = [any_spec] * self.n
        self.out_specs = [any_spec] * self.n
        self.out_shape = [_recv_shape(a, g) for a, g in zip(rides, gathers)]
        self.scratch = list(_Direct.SEMS) * self.n

    def split(self, refs, n_in, n_out, n_scratch):
        n = self.n
        ins, refs = refs[:n_in], refs[n_in:]
        rides, refs = refs[:n], refs[n:]
        outs, refs = refs[:n_out], refs[n_out:]
        gots, refs = refs[:n], refs[n:]
        scratch, sems = refs[:n_scratch], refs[n_scratch:]
        return ins, outs, scratch, (rides, gots, sems)

    def run(self, handles, first, last):
        rides, gots, sems = handles

        def all_of():
            return [_Direct(rides[a], gots[a], sems[3 * a:3 * a + 3], self.gathers[a]) for a in range(self.n)]

        @pl.when(first)
        def _():
            for e in all_of():
                e.start()

        @pl.when(last)
        def _():
            for e in all_of():
                e.wait()


def _exchange(arrs, gathers, *, name):
    rd = _Rides(arrs, gathers)

    def body(*refs):
        _, _, _, handles = rd.split(refs, 0, 0, 0)
        rd.run(handles, True, True)

    return pl.pallas_call(
        body, name=name, in_specs=rd.in_specs, out_specs=rd.out_specs, out_shape=rd.out_shape,
        scratch_shapes=rd.scratch,
    )(*arrs)


def _adamw(parts, w, m, v, *, name):
    r, cols = w.shape
    tm = _tile(r, 256, PACK_ALIGN)
    c1 = 1.0 - ADAM_B1 ** ADAM_STEP
    c2 = 1.0 - ADAM_B2 ** ADAM_STEP

    def body(p_ref, w_ref, m_ref, v_ref, g_ref, d_ref, nm_ref, nv_ref):
        g = p_ref[0].astype(F32)
        for i in range(1, NDEV):
            g = g + p_ref[i].astype(F32)
        nm = ADAM_B1 * m_ref[...] + (1.0 - ADAM_B1) * g
        nv = ADAM_B2 * v_ref[...] + (1.0 - ADAM_B2) * (g * g)
        d_ref[...] = -ADAM_LR * ((nm / c1) / (jnp.sqrt(nv / c2) + ADAM_EPS) + ADAM_WD * w_ref[...])
        g_ref[...] = g
        nm_ref[...] = nm
        nv_ref[...] = nv

    row = pl.BlockSpec((tm, cols), lambda i: (i, 0))
    return pl.pallas_call(
        body, name=name, grid=(r // tm,),
        in_specs=[pl.BlockSpec((NDEV, tm, cols), lambda i: (0, i, 0)), row, row, row],
        out_specs=[row] * 4, out_shape=[jax.ShapeDtypeStruct((r, cols), F32)] * 4,
        compiler_params=_cp("parallel"),
    )(parts, w, m, v)


PACK_ALIGN = 16


def _part_rows(shape):
    n = -(-math.prod(shape) // D)
    return n + (-n) % PACK_ALIGN


def _rows(a):
    flat = a.reshape(-1)
    pad = _part_rows(a.shape) * D - flat.shape[0]
    if pad:
        flat = jnp.concatenate([flat, jnp.zeros((pad,), flat.dtype)])
    return flat.reshape(-1, D)


def _pack(parts, total_rows):
    if all(math.prod(p.shape) % (PACK_ALIGN * D) for p in parts):
        return _pack_small(parts, total_rows)
    rows = [_rows(p) for p in parts]
    used = sum(r.shape[0] for r in rows)
    if total_rows > used:
        rows.append(jnp.zeros((total_rows - used, D), rows[0].dtype))
    return jnp.concatenate(rows, axis=0)


def _pack_small(parts, total_rows):
    flat, used = [], 0
    for p in parts:
        n, nr = math.prod(p.shape), _part_rows(p.shape)
        flat += [p.reshape(-1), jnp.zeros((nr * D - n,), p.dtype)]
        used += nr
    flat.append(jnp.zeros(((total_rows - used) * D,), parts[0].dtype))
    return jnp.concatenate(flat).reshape(total_rows, D)


def _unpack(buf, shapes, part_rows=_part_rows):
    out, r0 = [], 0
    for shp in shapes:
        n = math.prod(shp)
        out.append(buf[r0:r0 + part_rows(shp)].reshape(-1)[:n].reshape(shp))
        r0 += part_rows(shp)
    return out


def _tight_rows(shape):
    return -(-math.prod(shape) // D)


def _pack_tight(parts, total_rows):
    flat, used = [], 0
    for p in parts:
        n, nr = math.prod(p.shape), _tight_rows(p.shape)
        flat += [p.reshape(-1), jnp.zeros((nr * D - n,), p.dtype)]
        used += nr
    flat.append(jnp.zeros(((total_rows - used) * D,), parts[0].dtype))
    return jnp.concatenate(flat).reshape(total_rows, D)


SHARD = {"w_in": (D, 706), "w_out": (256, D), "w_mq": (128, D), "w_mk": (128, D), "w_mv": (128, D),
         "w_mo": (128, D), "w_up": (D, 704), "w_down": (352, D), "conv_ssd_w": (4, 192), "conv_ffn_w": (3, 704)}
GATHER_MID = ["w_out", "w_mq", "w_mk", "w_mv", "w_mo"]
GATHER_FFN = ["w_down"]
CONV_TAPS = ["conv_ssd_w", "conv_ffn_w"]
GRADS_PACKED = ["w_out", "w_mq", "w_mk", "w_mv", "w_mo", "w_down", "conv_ffn_w", "conv_ssd_w"]


def _layout(names):
    row0, r = {}, 0
    for n in names:
        row0[n] = r
        r += _part_rows(SHARD[n])
    return row0, r + (-r) % 128


SMALL = [("norm_mix_w", (1, D)), ("conv_ssd_b", (1, 1536)), ("dt_bias", (1, 16)), ("a_log", (1, 16)),
         ("d_skip", (1, 16)), ("ssd_norm_w", (1, D)), ("sb_norm_w", (1, D)), ("norm_mem_w", (1, D)),
         ("norm_memkv_w", (1, D)), ("norm_ffn_w", (1, D)), ("conv_ffn_b", (1, 5632)), ("norm_final_w", (D,))]
LOSS_ROW = sum(_tight_rows(_shp) for _, _shp in SMALL)
SMALL_ROWS = LOSS_ROW + 1 + (-(LOSS_ROW + 1)) % 8
ORDER = ["norm_mix_w", "w_in", "conv_ssd_w", "conv_ssd_b", "dt_bias", "a_log", "d_skip", "ssd_norm_w",
         "sb_norm_w", "w_out", "norm_mem_w", "norm_memkv_w", "w_mq", "w_mk", "w_mv", "w_mo", "norm_ffn_w",
         "w_up", "conv_ffn_w", "conv_ffn_b", "w_down", "norm_final_w"]


def _pad_rows(a, nr):
    n = a.shape[1]
    return jnp.concatenate([a, jnp.zeros((NDEV, nr * D - n), a.dtype)], axis=1).reshape(NDEV, nr, D)


def _group_sum(lanes):
    return lanes.reshape(16, 64).sum(axis=1).reshape(1, 16)


def kernel(x, mem, norm_mix_w, w_in, conv_ssd_w, conv_ssd_b, dt_bias, a_log, d_skip, ssd_norm_w, sb_norm_w, w_out, norm_mem_w, norm_memkv_w, w_mq, w_mk, w_mv, w_mo, norm_ffn_w, w_up, conv_ffn_w, conv_ffn_b, w_down, norm_final_w, loss_target, m_norm_mix_w, m_w_in, m_conv_ssd_w, m_conv_ssd_b, m_dt_bias, m_a_log, m_d_skip, m_ssd_norm_w, m_sb_norm_w, m_w_out, m_norm_mem_w, m_norm_memkv_w, m_w_mq, m_w_mk, m_w_mv, m_w_mo, m_norm_ffn_w, m_w_up, m_conv_ffn_w, m_conv_ffn_b, m_w_down, m_norm_final_w, v_norm_mix_w, v_w_in, v_conv_ssd_w, v_conv_ssd_b, v_dt_bias, v_a_log, v_d_skip, v_ssd_norm_w, v_sb_norm_w, v_w_out, v_norm_mem_w, v_norm_memkv_w, v_w_mq, v_w_mk, v_w_mv, v_w_mo, v_norm_ffn_w, v_w_up, v_conv_ffn_w, v_conv_ffn_b, v_w_down, v_norm_final_w):
    P = dict(norm_mix_w=norm_mix_w, w_in=w_in, conv_ssd_w=conv_ssd_w, conv_ssd_b=conv_ssd_b, dt_bias=dt_bias, a_log=a_log, d_skip=d_skip, ssd_norm_w=ssd_norm_w, sb_norm_w=sb_norm_w, w_out=w_out, norm_mem_w=norm_mem_w, norm_memkv_w=norm_memkv_w, w_mq=w_mq, w_mk=w_mk, w_mv=w_mv, w_mo=w_mo, norm_ffn_w=norm_ffn_w, w_up=w_up, conv_ffn_w=conv_ffn_w, conv_ffn_b=conv_ffn_b, w_down=w_down, norm_final_w=norm_final_w)
    M = dict(norm_mix_w=m_norm_mix_w, w_in=m_w_in, conv_ssd_w=m_conv_ssd_w, conv_ssd_b=m_conv_ssd_b, dt_bias=m_dt_bias, a_log=m_a_log, d_skip=m_d_skip, ssd_norm_w=m_ssd_norm_w, sb_norm_w=m_sb_norm_w, w_out=m_w_out, norm_mem_w=m_norm_mem_w, norm_memkv_w=m_norm_memkv_w, w_mq=m_w_mq, w_mk=m_w_mk, w_mv=m_w_mv, w_mo=m_w_mo, norm_ffn_w=m_norm_ffn_w, w_up=m_w_up, conv_ffn_w=m_conv_ffn_w, conv_ffn_b=m_conv_ffn_b, w_down=m_w_down, norm_final_w=m_norm_final_w)
    V = dict(norm_mix_w=v_norm_mix_w, w_in=v_w_in, conv_ssd_w=v_conv_ssd_w, conv_ssd_b=v_conv_ssd_b, dt_bias=v_dt_bias, a_log=v_a_log, d_skip=v_d_skip, ssd_norm_w=v_ssd_norm_w, sb_norm_w=v_sb_norm_w, w_out=v_w_out, norm_mem_w=v_norm_mem_w, norm_memkv_w=v_norm_memkv_w, w_mq=v_w_mq, w_mk=v_w_mk, w_mv=v_w_mv, w_mo=v_w_mo, norm_ffn_w=v_norm_ffn_w, w_up=v_w_up, conv_ffn_w=v_conv_ffn_w, conv_ffn_b=v_conv_ffn_b, w_down=v_w_down, norm_final_w=v_norm_final_w)
    small_shapes = [shp for _, shp in SMALL]

    def packed(src, names, dtype=F32):
        return _pack([src[n][0] for n in names], _layout(names)[1]).astype(dtype)

    def columns(g):
        return g.transpose(1, 0, 2).reshape(g.shape[1], NDEV * g.shape[2])

    g_in, g_taps = _all_gather([w_in[0].astype(BF16), packed(P, CONV_TAPS)], name="gather_w_in")
    W_in = columns(g_in)
    cw_ssd = g_taps[:, 0].reshape(NDEV, -1)[:, :768].reshape(NDEV, 4, 192).transpose(1, 0, 2).reshape(4, XBC)
    cw_ffn = (g_taps[:, PACK_ALIGN:PACK_ALIGN + 3].reshape(NDEV, -1)[:, :2112].reshape(NDEV, 3, 704)
              .transpose(1, 0, 2).reshape(3, 2 * DFF))
    W_z, W_xbc, W_dt, W_qkv = W_in[:, :D], W_in[:, D:D + XBC], W_in[:, D + XBC:D + XBC + 16], W_in[:, D + XBC + 16:]
    W_dtr = jnp.repeat(W_dt, 64, axis=1)
    cwg, cwv = cw_ffn[:, :DFF], cw_ffn[:, DFF:]
    cbg, cbv = conv_ffn_b[:, :DFF], conv_ffn_b[:, DFF:]
    rep = lambda p: jnp.repeat(p, 64, axis=1)
    lanes = jnp.concatenate([rep(dt_bias), rep(a_log), rep(d_skip), ssd_norm_w, jnp.zeros((4, D), F32)], axis=0)

    xs, tgt, mm = x[0], loss_target[0], mem[0]

    h1 = _norm_fwd(xs, norm_mix_w, name="norm_mix")
    z = _mm(h1, W_z, name="proj_z")
    xbc = _mm(h1, W_xbc, name="proj_xbc")
    dtr = _mm(h1, W_dtr, name="proj_dt")
    qkv = _mm(h1, W_qkv, name="proj_qkv", out_dtype=BF16)
    y_ssd, states, g_ffn = _ssd_fwd(z, xbc, dtr, cw_ssd, conv_ssd_b, lanes, [packed(P, GATHER_FFN, BF16)],
                                    name="ssd_fwd")
    o_sb, g_mid, g_up = _sb_fwd(qkv, [packed(P, GATHER_MID, BF16), w_up[0].astype(BF16)], name="sb_fwd")
    r_mid = _layout(GATHER_MID)[0]
    W_out = g_mid[:, r_mid["w_out"]:r_mid["w_out"] + 256].reshape(2 * D, D)
    W_mq, W_mk, W_mv, W_mo = [g_mid[:, r_mid[n]:r_mid[n] + 128].reshape(D, D)
                              for n in ("w_mq", "w_mk", "w_mv", "w_mo")]
    W_up = columns(g_up)
    W_down = g_ffn[:, 0:352].reshape(DFF, D)
    W_upg, W_upv = W_up[:, :DFF], W_up[:, DFF:]
    y_sb = _head_norm_fwd(o_sb, sb_norm_w, name="sb_norm")
    ymix = jnp.concatenate([y_ssd, y_sb], axis=1)
    x1 = _mm(ymix, W_out, add=xs, name="proj_out")
    h2 = _norm_fwd(x1, norm_mem_w, name="norm_mem")
    mn = _norm_fwd(mm, norm_memkv_w, name="norm_memkv")
    qm = _mm(h2, W_mq, name="mem_q", out_dtype=BF16)
    km = _mm(mn, W_mk, name="mem_k", out_dtype=BF16)
    vm = _mm(mn, W_mv, name="mem_v", out_dtype=BF16)
    om = _mem_attn_fwd(qm, km, vm, name="mem_attn")
    x2 = _mm(om, W_mo, add=x1, name="mem_o")
    h3 = _norm_fwd(x2, norm_ffn_w, name="norm_ffn")
    ug = _mm(h3, W_upg, name="ffn_up_g")
    uv = _mm(h3, W_upv, name="ffn_up_v")
    f = _glu_fwd(ug, uv, cwg, cwv, cbg, cbv, name="ffn_glu")
    x3 = _mm(f, W_down, add=x2, name="ffn_down")
    dx3, g_nfinal, loss_part = _final(x3, norm_final_w.reshape(1, D), tgt, name="final_loss")

    G = {}
    G["w_down"] = _mm(f, dx3, trans_a=True, name="g_w_down")
    df = _mm(dx3, W_down, trans_b=True, name="d_f")
    dupg, dupv, dcg, dcv = _ffn_bwd(ug, uv, df, cwg, cwv, cbg, cbv, name="ffn_glu_bwd")
    G["w_up"] = jnp.concatenate([_mm(h3, dupg, trans_a=True, name="g_w_up_g"),
                                 _mm(h3, dupv, trans_a=True, name="g_w_up_v")], axis=1)
    G["conv_ffn_w"] = jnp.concatenate([dcg[0:3], dcv[0:3]], axis=1)
    G["conv_ffn_b"] = jnp.concatenate([dcg[3:4], dcv[3:4]], axis=1)
    dh3 = _mm(dupg, W_upg, trans_b=True, name="d_h3_g")
    dh3 = _mm(dupv, W_upv, trans_b=True, add=dh3, name="d_h3_v")
    dx2, G["norm_ffn_w"] = _norm_bwd(x2, norm_ffn_w, dh3, dx3, name="norm_ffn_bwd")
    G["w_mo"] = _mm(om, dx2, trans_a=True, name="g_w_mo")
    dom = _mm(dx2, W_mo, trans_b=True, name="d_om")
    dqm, dkm, dvm = _mem_attn_bwd(qm, km, vm, dom, name="mem_attn_bwd")
    G["w_mq"] = _mm(h2, dqm, trans_a=True, name="g_w_mq")
    G["w_mk"] = _mm(mn, dkm, trans_a=True, name="g_w_mk")
    G["w_mv"] = _mm(mn, dvm, trans_a=True, name="g_w_mv")
    dh2 = _mm(dqm, W_mq, trans_b=True, name="d_h2")
    dmn = _mm(dkm, W_mk, trans_b=True, name="d_mn_k")
    dmn = _mm(dvm, W_mv, trans_b=True, add=dmn, name="d_mn_v")
    _, G["norm_memkv_w"] = _norm_bwd(mm, norm_memkv_w, dmn, None, name="norm_memkv_bwd")
    dx1, G["norm_mem_w"] = _norm_bwd(x1, norm_mem_w, dh2, dx2, name="norm_mem_bwd")
    G["w_out"] = _mm(ymix, dx1, trans_a=True, name="g_w_out")
    dymix = _mm(dx1, W_out, trans_b=True, name="d_ymix")
    do_sb, G["sb_norm_w"] = _head_norm_bwd(o_sb, sb_norm_w, dymix, name="sb_norm_bwd")

    dz, dxbc, ddtr, dlanes, dconv = _ssd_bwd(z, xbc, dtr, states, dymix, cw_ssd, conv_ssd_b, lanes, name="ssd_bwd")
    G["dt_bias"], G["a_log"], G["d_skip"] = [_group_sum(dlanes[i:i + 1]) for i in range(3)]
    G["ssd_norm_w"] = dlanes[3:4]
    G["conv_ssd_w"], G["conv_ssd_b"] = dconv[0:4], dconv[4:5]

    def col_slabs(g, cols):
        return g.reshape(g.shape[0], NDEV, cols).transpose(1, 0, 2).astype(BF16)

    def packed_slabs(names):
        parts = []
        for n in names:
            shp = SHARD[n]
            if shp[-1] == D:
                t = G[n].reshape((NDEV,) + shp)
            else:
                t = _pad_rows(G[n].reshape(shp[0], NDEV, shp[1]).transpose(1, 0, 2).reshape(NDEV, -1),
                              _part_rows(shp))
            parts.append(jnp.pad(t, ((0, 0), (0, _part_rows(shp) - t.shape[1]), (0, 0))))
        used = sum(t.shape[1] for t in parts)
        parts.append(jnp.zeros((NDEV, _layout(names)[1] - used, D), F32))
        return jnp.concatenate(parts, axis=1).astype(BF16)

    dq, dk, dv, recv_packed, recv_up = _sb_bwd(qkv, o_sb, do_sb, [packed_slabs(GRADS_PACKED), col_slabs(G["w_up"], 704)],
                                               name="sb_bwd")
    dproj = jnp.concatenate([dz, dxbc, ddtr, dq, dk, dv], axis=1)
    g_proj = _mm(h1, dproj, trans_a=True, name="g_w_in")
    c_dt = D + XBC
    G["w_in"] = jnp.concatenate([g_proj[:, :c_dt], g_proj[:, c_dt:c_dt + D].reshape(D, 16, 64).sum(axis=2),
                                 g_proj[:, c_dt + D:]], axis=1)
    W_proj = jnp.concatenate([W_z, W_xbc, W_dtr, W_qkv], axis=1)
    dh1, recv_in = _mm(dproj, W_proj, trans_b=True, rides=[col_slabs(G["w_in"], 706)], name="d_h1")
    dx, G["norm_mix_w"] = _norm_bwd(xs, norm_mix_w, dh1, dx1, name="norm_mix_bwd")
    G["norm_final_w"] = g_nfinal.reshape(D)

    small_g = _pack_tight([G[n] for n, _ in SMALL] + [loss_part], SMALL_ROWS)
    (parts_small,) = _exchange([small_g], [True], name="exchange_grads")
    outs_packed = _adamw(recv_packed, packed(P, GRADS_PACKED), packed(M, GRADS_PACKED), packed(V, GRADS_PACKED),
                         name="adamw_packed")
    outs_up = _adamw(recv_up, w_up[0], m_w_up[0], v_w_up[0], name="adamw_w_up")
    outs_in = _adamw(recv_in, w_in[0], m_w_in[0], v_w_in[0], name="adamw_w_in")
    outs_small = _adamw(parts_small, _pack_tight([P[n] for n, _ in SMALL], SMALL_ROWS),
                        _pack_tight([M[n] for n, _ in SMALL], SMALL_ROWS),
                        _pack_tight([V[n] for n, _ in SMALL], SMALL_ROWS), name="adamw_replicated")

    res = {}
    for i, kind in enumerate(("grad", "delta", "new_m", "new_v")):
        for n, val in zip(GRADS_PACKED, _unpack(outs_packed[i], [SHARD[n] for n in GRADS_PACKED])):
            res[kind, n] = val.reshape((1,) + SHARD[n])
        res[kind, "w_up"] = outs_up[i].reshape((1,) + SHARD["w_up"])
        res[kind, "w_in"] = outs_in[i].reshape((1,) + SHARD["w_in"])
        for (n, shp), val in zip(SMALL, _unpack(outs_small[i], small_shapes, _tight_rows)):
            res[kind, n] = val
    loss = outs_small[0][LOSS_ROW, 0]
    out = [loss, dx.reshape(1, -1, D)]
    for kind in ("grad", "delta", "new_m", "new_v"):
        out += [res[kind, n] for n in ORDER]
    return tuple(out)
```

```python
import functools
import math

import jax
import jax.numpy as jnp
from jax import lax
from jax.experimental import pallas as pl
from jax.experimental.pallas import tpu as pltpu

F32 = jnp.float32
BF16 = jnp.bfloat16

D = 1024
NDEV = 8
EPS = 1e-6
SSD_CHUNK = 128
HALO = 8
VMEM_LIMIT = 56 * 2**20

ADAM_LR, ADAM_B1, ADAM_B2, ADAM_EPS, ADAM_WD, ADAM_STEP = 0.001, 0.9, 0.999, 1e-08, 0.01, 10


def _cp(*sem):
    return pltpu.CompilerParams(dimension_semantics=sem, vmem_limit_bytes=VMEM_LIMIT)


def _tile(n, cap, mult):
    if n <= cap:
        return n
    for d in range(cap - cap % mult, 0, -mult):
        if n % d == 0:
            return d
    raise ValueError(f"no tile for {n}")


def _sigmoid(x):
    return 1.0 / (1.0 + jnp.exp(-x))


def _silu(x):
    return x * _sigmoid(x)


def _softplus(x):
    return jnp.maximum(x, 0.0) + jnp.log(1.0 + jnp.exp(-jnp.abs(x)))


def _terms(x, n):
    out = []
    r = x.astype(F32)
    for i in range(n):
        h = r.astype(BF16)
        out.append(h)
        if i + 1 < n:
            r = r - h.astype(F32)
    return out


_DIMS = {"nn": ((1,), (0,)), "nt": ((1,), (1,)), "tn": ((0,), (0,))}


def _dot_raw(form, a, b, ta, tb):
    acc = None
    for ai in _terms(a, ta):
        for bi in _terms(b, tb):
            d = lax.dot_general(ai, bi, (_DIMS[form], ((), ())), preferred_element_type=F32)
            acc = d if acc is None else acc + d
    return acc


@functools.lru_cache(maxsize=None)
def _dot_fn(form, ta, tb):
    @jax.custom_vjp
    def f(a, b):
        return _dot_raw(form, a, b, ta, tb)

    def fwd(a, b):
        return f(a, b), (a, b)

    def bwd(res, ct):
        a, b = res
        if form == "nn":
            return _dot_fn("nt", ta, tb)(ct, b), _dot_fn("tn", ta, tb)(a, ct)
        if form == "nt":
            return _dot_fn("nn", ta, tb)(ct, b), _dot_fn("tn", tb, ta)(ct, a)
        return _dot_fn("nt", tb, ta)(b, ct), _dot_fn("nn", ta, tb)(a, ct)

    f.defvjp(fwd, bwd)
    return f


def _dot(form, a, b, ta=1, tb=1):
    return _dot_fn(form, ta, tb)(a, b)


@functools.lru_cache(maxsize=None)
def _take_fn(axis, idx):
    @jax.custom_vjp
    def f(x):
        return x[:, idx:idx + 1] if axis == 1 else x[idx:idx + 1, :]

    def fwd(x):
        return f(x), x.shape

    def bwd(shape, ct):
        io = lax.broadcasted_iota(jnp.int32, shape, axis)
        return (jnp.where(io == idx, jnp.broadcast_to(ct, shape), 0.0),)

    f.defvjp(fwd, bwd)
    return f


@functools.lru_cache(maxsize=None)
def _split_fn(width, n):
    @jax.custom_vjp
    def f(x):
        return tuple(x[:, i * width:(i + 1) * width] for i in range(n))

    def fwd(x):
        return f(x), None

    def bwd(_, cts):
        return (jnp.concatenate(list(cts), axis=1),)

    f.defvjp(fwd, bwd)
    return f


def _split(x, width):
    return _split_fn(width, x.shape[1] // width)(x)


def _iota(shape, axis):
    return lax.broadcasted_iota(jnp.int32, shape, axis)


MM_VMEM_BUDGET = 44 * 2**20


def _mm_tiles(m, n, kt, trans_a, a_bytes, b_bytes, out_bytes, add_bytes):
    tn = _tile(n, 1536, 128)
    for tm_cap in (1408, 1024, 512, 256, 128):
        tm = _tile(m, tm_cap, 128 if trans_a else 8)
        for tk_cap in (kt, 4096, 2048, 1024, 512):
            tk = _tile(kt, tk_cap, 128)
            blocks = tm * tk * a_bytes + tk * tn * b_bytes + tm * tn * (out_bytes + add_bytes)
            if 2 * blocks + (tm * tn * 4 if tk < kt else 0) <= MM_VMEM_BUDGET:
                return tm, tn, tk
    raise ValueError(f"no matmul tiling for {(m, n, kt)}")


def _mm(a, b, *, name, add=None, trans_a=False, trans_b=False, out_dtype=F32, rides=()):
    assert not (trans_a and trans_b)
    if trans_a:
        kt, m = a.shape
    else:
        m, kt = a.shape
    n, kt2 = b.shape if trans_b else b.shape[::-1]
    assert kt == kt2, (a.shape, b.shape)
    tm, tn, tk = _mm_tiles(m, n, kt, trans_a, a.dtype.itemsize, b.dtype.itemsize,
                           jnp.dtype(out_dtype).itemsize, 0 if add is None else add.dtype.itemsize)
    nk = kt // tk
    grid = (m // tm, n // tn, nk)
    rd = _Rides(rides, [False] * len(rides))
    n_in = 2 if add is None else 3

    def body(*all_refs):
        ins, (o_ref,), scratch, handles = rd.split(all_refs, n_in, 1, 1 if nk > 1 else 0)
        refs = (*ins, o_ref, *scratch)
        if rides:
            ids = [pl.program_id(ax) for ax in range(3)]
            rd.run(handles, (ids[0] == 0) & (ids[1] == 0) & (ids[2] == 0),
                   (ids[0] == grid[0] - 1) & (ids[1] == grid[1] - 1) & (ids[2] == grid[2] - 1))
        if add is None:
            a_ref, b_ref, o_ref = refs[:3]
        else:
            a_ref, b_ref, add_ref, o_ref = refs[:4]
        k = pl.program_id(2)
        av = a_ref[...].astype(BF16)
        bv = b_ref[...].astype(BF16)
        dims = _DIMS["tn" if trans_a else "nt" if trans_b else "nn"]
        d = lax.dot_general(av, bv, (dims, ((), ())), preferred_element_type=F32)

        def finish(r):
            if add is not None:
                r = r + add_ref[...]
            o_ref[...] = r.astype(out_dtype)

        if nk == 1:
            finish(d)
        else:
            acc = refs[-1]

            @pl.when(k == 0)
            def _():
                acc[...] = d

            @pl.when((k > 0) & (k < nk - 1))
            def _():
                acc[...] += d

            @pl.when(k == nk - 1)
            def _():
                finish(acc[...] + d)

    a_spec = (pl.BlockSpec((tk, tm), lambda i, j, k: (k, i)) if trans_a
              else pl.BlockSpec((tm, tk), lambda i, j, k: (i, k)))
    b_spec = (pl.BlockSpec((tn, tk), lambda i, j, k: (j, k)) if trans_b
              else pl.BlockSpec((tk, tn), lambda i, j, k: (k, j)))
    in_specs = [a_spec, b_spec]
    args = [a, b]
    if add is not None:
        in_specs.append(pl.BlockSpec((tm, tn), lambda i, j, k: (i, j)))
        args.append(add)
    out = pl.pallas_call(
        body, name=name, grid=grid,
        in_specs=in_specs + rd.in_specs,
        out_specs=[pl.BlockSpec((tm, tn), lambda i, j, k: (i, j))] + rd.out_specs,
        out_shape=[jax.ShapeDtypeStruct((m, n), out_dtype)] + rd.out_shape,
        scratch_shapes=([pltpu.VMEM((tm, tn), F32)] if nk > 1 else []) + rd.scratch,
        compiler_params=_cp(*(("arbitrary",) * 3 if rides else ("parallel", "parallel", "arbitrary"))),
    )(*args, *rides)
    return out if rides else out[0]


def _rstd(x):
    return lax.rsqrt(jnp.mean(x * x, axis=-1, keepdims=True) + EPS)


def _norm_fwd(x, w, *, name):
    s = x.shape[0]
    tm = _tile(s, 512, 8)

    def body(x_ref, w_ref, o_ref):
        xv = x_ref[...]
        o_ref[...] = (xv * _rstd(xv) * w_ref[...]).astype(BF16)

    return pl.pallas_call(
        body, name=name, grid=(s // tm,),
        in_specs=[pl.BlockSpec((tm, D), lambda i: (i, 0)), pl.BlockSpec((1, D), lambda i: (0, 0))],
        out_specs=pl.BlockSpec((tm, D), lambda i: (i, 0)),
        out_shape=jax.ShapeDtypeStruct((s, D), BF16), compiler_params=_cp("parallel"),
    )(x, w)


def _norm_bwd_math(xv, wv, dy):
    r = _rstd(xv)
    xh = xv * r
    dxh = dy * wv
    dx = r * (dxh - xh * jnp.mean(dxh * xh, axis=-1, keepdims=True))
    dw = jnp.sum(dy * xh, axis=0, keepdims=True)
    return dx, dw


def _norm_bwd(x, w, dy, add, *, name):
    s = x.shape[0]
    tm = _tile(s, 256, 8)

    def body(*refs):
        if add is None:
            x_ref, w_ref, dy_ref, dx_ref, dw_ref = refs
        else:
            x_ref, w_ref, dy_ref, add_ref, dx_ref, dw_ref = refs

        @pl.when(pl.program_id(0) == 0)
        def _():
            dw_ref[...] = jnp.zeros_like(dw_ref)

        dx, dw = _norm_bwd_math(x_ref[...], w_ref[...], dy_ref[...])
        if add is not None:
            dx = dx + add_ref[...]
        dx_ref[...] = dx
        dw_ref[...] += dw

    row = pl.BlockSpec((tm, D), lambda i: (i, 0))
    vec = pl.BlockSpec((1, D), lambda i: (0, 0))
    in_specs = [row, vec, row] + ([row] if add is not None else [])
    args = [x, w, dy] + ([add] if add is not None else [])
    return pl.pallas_call(
        body, name=name, grid=(s // tm,), in_specs=in_specs, out_specs=[row, vec],
        out_shape=[jax.ShapeDtypeStruct((s, D), F32), jax.ShapeDtypeStruct((1, D), F32)],
        compiler_params=_cp("arbitrary"),
    )(*args)


def _final(x3, w, target, *, name):
    s = x3.shape[0]
    tm = _tile(s, 256, 8)

    def body(x_ref, w_ref, t_ref, dx_ref, dw_ref, loss_ref):
        @pl.when(pl.program_id(0) == 0)
        def _():
            dw_ref[...] = jnp.zeros_like(dw_ref)
            loss_ref[...] = jnp.zeros_like(loss_ref)

        xv = x_ref[...]
        wv = w_ref[...]
        y = xv * _rstd(xv) * wv
        err = y - t_ref[...]
        loss_ref[...] += 0.5 * jnp.sum(jnp.mean(err * err, axis=-1, keepdims=True))
        dx, dw = _norm_bwd_math(xv, wv, err * (1.0 / D))
        dx_ref[...] = dx
        dw_ref[...] += dw

    row = pl.BlockSpec((tm, D), lambda i: (i, 0))
    vec = pl.BlockSpec((1, D), lambda i: (0, 0))
    return pl.pallas_call(
        body, name=name, grid=(s // tm,), in_specs=[row, vec, row], out_specs=[row, vec, vec],
        out_shape=[jax.ShapeDtypeStruct((s, D), F32), jax.ShapeDtypeStruct((1, D), F32),
                   jax.ShapeDtypeStruct((1, D), F32)],
        compiler_params=_cp("arbitrary"),
    )(x3, w, target)


def _head_norm_math(o, w):
    lane = _iota((128, 128), 0) // 64
    bd = (lane == _iota((128, 128), 1) // 64).astype(F32)
    outs = []
    for op in _split(o, 128):
        ms = _dot("nn", op * op, bd, 2, 1) * (1.0 / 64)
        outs.append(op * lax.rsqrt(ms + EPS))
    return jnp.concatenate(outs, axis=1) * w


def _head_norm_fwd(o, w, *, name):
    s = o.shape[0]
    tm = _tile(s, 256, 8)

    def body(o_ref, w_ref, y_ref):
        y_ref[...] = _head_norm_math(o_ref[...], w_ref[...]).astype(BF16)

    row = pl.BlockSpec((tm, D), lambda i: (i, 0))
    vec = pl.BlockSpec((1, D), lambda i: (0, 0))
    return pl.pallas_call(
        body, name=name, grid=(s // tm,), in_specs=[row, vec], out_specs=row,
        out_shape=jax.ShapeDtypeStruct((s, D), BF16), compiler_params=_cp("parallel"),
    )(o, w)


def _head_norm_bwd(o, w, dymix, *, name):
    s = o.shape[0]
    tm = _tile(s, 256, 8)

    def body(o_ref, w_ref, dy_ref, do_ref, dw_ref):
        @pl.when(pl.program_id(0) == 0)
        def _():
            dw_ref[...] = jnp.zeros_like(dw_ref)

        _, vjp = jax.vjp(_head_norm_math, o_ref[...], w_ref[...])
        do, dw = vjp(dy_ref[...])
        do_ref[...] = do
        dw_ref[...] += dw

    row = pl.BlockSpec((tm, D), lambda i: (i, 0))
    vec = pl.BlockSpec((1, D), lambda i: (0, 0))
    return pl.pallas_call(
        body, name=name, grid=(s // tm,),
        in_specs=[row, vec, pl.BlockSpec((tm, D), lambda i: (i, 1))], out_specs=[row, vec],
        out_shape=[jax.ShapeDtypeStruct((s, D), F32), jax.ShapeDtypeStruct((1, D), F32)],
        compiler_params=_cp("arbitrary"),
    )(o, w, dymix)


SB_BQ = 256
SB_BK = 256


def _sb_consts():
    r = _iota((SB_BK, SB_BK), 0)
    c = _iota((SB_BK, SB_BK), 1)
    u_excl = (r > c).astype(BF16)
    u_incl = (r >= c).astype(BF16)
    return u_excl, u_incl


SB_LANES = 256
SB_NCH = SB_LANES // 64


def _nt(a, b):
    return lax.dot_general(a, b, (_DIMS["nt"], ((), ())), preferred_element_type=F32)


def _tn(a, b):
    return lax.dot_general(a, b, (_DIMS["tn"], ((), ())), preferred_element_type=F32)


def _nn(a, b):
    return jnp.dot(a, b, preferred_element_type=F32)


def _sb_heads(ref):
    out = []
    for hp in range(SB_LANES // 128):
        v = ref[:, 128 * hp:128 * (hp + 1)]
        first = _iota(v.shape, 1) < 64
        out += [jnp.where(first, v, 0).astype(BF16), jnp.where(first, 0, v).astype(BF16)]
    return out


SB_STRIP = 32


def _neg_abs(x):
    bits = lax.bitcast_convert_type(x, jnp.uint32) | jnp.uint32(0x80000000)
    return lax.bitcast_convert_type(bits, F32)


def _sb_block(ref, j):
    off = pl.multiple_of(j * SB_BK, SB_BK)
    return [ref[pl.ds(off, SB_BK), 128 * hp:128 * (hp + 1)] for hp in range(SB_NCH // 2)]


SB_DEAD = 104.0


def _sb_live(nlrun):
    m = nlrun[0]
    for x in nlrun[1:]:
        m = jnp.minimum(m, x)
    return jnp.min(m) < SB_DEAD


SB_TOP = 160


def _sb_bottom_live(nlrun):
    return _sb_live([x[SB_TOP:, :] for x in nlrun])


def _sb_top(x, fn, nrows):
    if nrows == x.shape[0]:
        return fn(x)
    return jnp.concatenate([fn(x[0:nrows]), x[nrows:]], axis=0)


def _sb_strips(nrows=SB_BQ):
    return [(r, pl.ds(r, SB_STRIP)) for r in range(0, nrows, SB_STRIP)]


def _sb_diag_mask(r):
    return _iota((SB_STRIP, SB_BK), 1) < _iota((SB_STRIP, SB_BK), 0) + r


def _sb_soft(z, mask):
    e = jnp.exp(_neg_abs(z))
    nl = jnp.maximum(z, 0.0) + jnp.log(1.0 + e)
    if mask is not None:
        nl = jnp.where(mask, nl, 0.0)
    return e, nl


def _sb_split_to(hl_ref, rows, x):
    hi, lo = _terms(x, 2)
    hl_ref[rows, 0:SB_BK] = hi
    hl_ref[rows, SB_BK:2 * SB_BK] = lo


def _sb_stage_soft(z_ref, nl_ref, diag, nrows=SB_BQ):
    for r, rows in _sb_strips(nrows):
        _, nl = _sb_soft(z_ref[rows, :], _sb_diag_mask(r) if diag else None)
        nl_ref[rows, 0:SB_BK] = nl.astype(BF16)


def _sb_stage_weights(z_ref, c_ref, a_ref, nlrun, diag, nrows=SB_BQ):
    for r, rows in _sb_strips(nrows):
        a = jnp.exp(z_ref[rows, :] - c_ref[rows, :] - nlrun[r:r + SB_STRIP, :])
        if diag:
            a = jnp.where(_sb_diag_mask(r), a, 0.0)
        a_ref[rows, :] = a.astype(BF16)


def _sb_fwd(qkv, rides, *, name):
    s = qkv.shape[0]
    nq = s // SB_BQ
    ng = D // SB_LANES
    assert SB_BQ == SB_BK
    rd = _Rides(rides, [True] * len(rides))

    def body(*refs):
        (q_ref, k_ref, v_ref), (o_ref,), (zbuf, nlbuf, cbuf, abuf), handles = rd.split(refs, 3, 1, 4)
        i = pl.program_id(1)
        step_no = pl.program_id(0) * nq + i
        rd.run(handles, step_no == 0, step_no == ng * nq - 1)

        _, u_incl = _sb_consts()
        lane_a = _iota((SB_BQ, 128), 1) < 64
        qh = [q * 0.125 for q in _sb_heads(q_ref)]

        def tile(j, accs, nlrun, diag, nrows=SB_BQ):
            top = slice(0, nrows)
            kbs = _sb_block(k_ref, j)
            for c in range(SB_NCH):
                zbuf[c, top] = _nt(qh[c][top], kbs[c // 2])
            for c in range(SB_NCH):
                _sb_stage_soft(zbuf.at[c], nlbuf.at[c], diag, nrows)
                cbuf[c, top] = _nn(nlbuf[c, top], u_incl)
            for c in range(SB_NCH):
                _sb_stage_weights(zbuf.at[c], cbuf.at[c], abuf.at[c], nlrun[c], diag, nrows)
            nlrun = tuple(_sb_top(nlrun[c], lambda x, c=c: x + cbuf[c, top, 0:1], nrows) for c in range(SB_NCH))
            vbs = _sb_block(v_ref, j)
            outs = [_nn(abuf[c, top], vbs[c // 2]) for c in range(SB_NCH)]
            accs = tuple(_sb_top(acc, lambda x, hp=hp: x + jnp.where(lane_a[top], outs[2 * hp], outs[2 * hp + 1]), nrows)
                         for hp, acc in enumerate(accs))
            return accs, nlrun

        accs, nlrun = tile(i, (jnp.zeros((SB_BQ, 128), F32),) * (SB_NCH // 2),
                           (jnp.zeros((SB_BQ, 1), F32),) * SB_NCH, True)

        def step(carry):
            j, _, accs, nlrun = carry
            accs, nlrun = lax.cond(_sb_bottom_live(nlrun),
                                   lambda: tile(j, accs, nlrun, False),
                                   lambda: tile(j, accs, nlrun, False, SB_TOP))
            return j - 1, _sb_live(nlrun), accs, nlrun

        _, _, accs, _ = lax.while_loop(lambda c: (c[0] >= 0) & c[1], step, (i - 1, _sb_live(nlrun), accs, nlrun))
        o_ref[...] = jnp.concatenate(accs, axis=1)

    return pl.pallas_call(
        body, name=name, grid=(ng, nq),
        in_specs=[pl.BlockSpec((SB_BQ, SB_LANES), lambda g, i: (i, g)),
                  pl.BlockSpec((s, SB_LANES), lambda g, i: (0, ng + g)),
                  pl.BlockSpec((s, SB_LANES), lambda g, i: (0, 2 * ng + g)), *rd.in_specs],
        out_specs=[pl.BlockSpec((SB_BQ, SB_LANES), lambda g, i: (i, g)), *rd.out_specs],
        out_shape=[jax.ShapeDtypeStruct((s, D), F32), *rd.out_shape],
        scratch_shapes=[pltpu.VMEM((SB_NCH, SB_BQ, SB_BK), F32), pltpu.VMEM((SB_NCH, SB_BQ, SB_BK), BF16),
                        pltpu.VMEM((SB_NCH, SB_BQ, SB_BK), F32), pltpu.VMEM((SB_NCH, SB_BQ, SB_BK), BF16),
                        *rd.scratch],
        compiler_params=_cp("arbitrary", "arbitrary"),
    )(qkv, qkv, qkv, *rides)


def _sb_bwd(qkv, o, do, rides, *, name):
    s = qkv.shape[0]
    nq = s // SB_BQ
    ng = D // SB_LANES
    nhp = SB_NCH // 2
    rd = _Rides(rides, [False] * len(rides))

    def body(*refs):
        ins, outs, scratch, handles = rd.split(refs, 5, 3, 11)
        q_ref, k_ref, v_ref, o_ref, do_ref = ins
        dq_ref, dk_hbm, dv_hbm = outs
        dk_acc, dv_acc, dk16, dv16, sems, zbuf, gbuf, hl, cbuf, abuf, dzbuf = scratch
        g_idx = pl.program_id(0)
        i = pl.program_id(1)
        step_no = g_idx * nq + i
        rd.run(handles, step_no == 0, step_no == ng * nq - 1)

        @pl.when(i == 0)
        def _():
            dk_acc[...] = jnp.zeros_like(dk_acc)
            dv_acc[...] = jnp.zeros_like(dv_acc)

        _, u_incl = _sb_consts()
        u2 = jnp.concatenate([u_incl, u_incl], axis=0)
        lane_a = _iota((SB_BQ, 128), 1) < 64
        lane_k = _iota((SB_BK, 128), 1) < 64
        qh = [q * 0.125 for q in _sb_heads(q_ref)]
        qf = [q_ref[:, 128 * hp:128 * (hp + 1)] for hp in range(nhp)]
        doh = _sb_heads(do_ref)
        dof = [do_ref[:, 128 * hp:128 * (hp + 1)].astype(BF16) for hp in range(nhp)]
        delta = []
        for hp in range(nhp):
            prod = dof[hp].astype(F32) * o_ref[:, 128 * hp:128 * (hp + 1)]
            delta += [jnp.sum(jnp.where(lane_a, prod, 0.0), axis=1, keepdims=True),
                      jnp.sum(jnp.where(lane_a, 0.0, prod), axis=1, keepdims=True)]

        def pre(slot, j, nrows):
            top = slice(0, nrows)
            kbs = _sb_block(k_ref, j)
            vbs = _sb_block(v_ref, j)
            for c in range(SB_NCH):
                zbuf[slot, c, top] = _nt(qh[c][top], kbs[c // 2])
                gbuf[slot, c, top] = _nt(doh[c][top], vbs[c // 2])

        def stage_g(c, slot, nrows):
            for _, rows in _sb_strips(nrows):
                g = abuf[slot, c, rows, :].astype(F32) * gbuf[slot, c, rows, :]
                gbuf[slot, c, rows, :] = g
                _sb_split_to(hl.at[c], rows, g)

        def stage_dz(c, slot, grun, diag, nrows):
            for r, rows in _sb_strips(nrows):
                z = zbuf[slot, c, rows, :]
                g = gbuf[slot, c, rows, :]
                cs = (delta[c] - grun)[r:r + SB_STRIP, :] - cbuf[c, rows, :]
                sig = 1.0 / (1.0 + jnp.exp(-z))
                dz = g - (g + cs) * sig
                if diag:
                    dz = jnp.where(_sb_diag_mask(r), dz, 0.0)
                dzbuf[slot, c, rows, :] = dz.astype(BF16)

        def chain(slot, nlrun, grun, diag, nrows):
            top = slice(0, nrows)
            for c in range(SB_NCH):
                _sb_stage_soft(zbuf.at[slot, c], hl.at[c], diag, nrows)
                cbuf[c, top] = _nn(hl[c, top, 0:SB_BK], u_incl)
            nl_tot = []
            for c in range(SB_NCH):
                _sb_stage_weights(zbuf.at[slot, c], cbuf.at[c], abuf.at[slot, c], nlrun[c], diag, nrows)
                nl_tot.append(cbuf[c, top, 0:1])
                stage_g(c, slot, nrows)
                cbuf[c, top] = _nn(hl[c, top], u2)
            g_tot = []
            for c in range(SB_NCH):
                stage_dz(c, slot, grun[c], diag, nrows)
                g_tot.append(cbuf[c, top, 0:1])
            return (tuple(_sb_top(a, lambda x, b=b: x + b, nrows) for a, b in zip(nlrun, nl_tot)),
                    tuple(_sb_top(a, lambda x, b=b: x + b, nrows) for a, b in zip(grun, g_tot)))

        def post(slot, j, dqs, nrows):
            top = slice(0, nrows)
            off = pl.multiple_of(j * SB_BK, SB_BK)
            kbs = _sb_block(k_ref, j)
            dq_t = [_nn(dzbuf[slot, c, top], kbs[c // 2]) for c in range(SB_NCH)]
            dk_t = [_tn(dzbuf[slot, c, top], qf[c // 2][top]) for c in range(SB_NCH)]
            dv_t = [_tn(abuf[slot, c, top], dof[c // 2][top]) for c in range(SB_NCH)]
            for hp in range(nhp):
                cols = slice(128 * hp, 128 * (hp + 1))
                dk_acc[pl.ds(off, SB_BK), cols] += 0.125 * jnp.where(lane_k, dk_t[2 * hp], dk_t[2 * hp + 1])
                dv_acc[pl.ds(off, SB_BK), cols] += jnp.where(lane_k, dv_t[2 * hp], dv_t[2 * hp + 1])
            return tuple(_sb_top(dq, lambda x, hp=hp: x + jnp.where(lane_a[top], dq_t[2 * hp], dq_t[2 * hp + 1]), nrows)
                         for hp, dq in enumerate(dqs))

        def tile(j, dqs, nlrun, grun, diag, nrows=SB_BQ):
            pre(0, j, nrows)
            nlrun, grun = chain(0, nlrun, grun, diag, nrows)
            return post(0, j, dqs, nrows), nlrun, grun

        zero = (jnp.zeros((SB_BQ, 1), F32),) * SB_NCH
        dqs, nlrun, grun = tile(i, (jnp.zeros((SB_BQ, 128), F32),) * nhp, zero, zero, True)

        def step(carry):
            j, _, dqs, nlrun, grun = carry
            dqs, nlrun, grun = lax.cond(_sb_bottom_live(nlrun),
                                        lambda: tile(j, dqs, nlrun, grun, False),
                                        lambda: tile(j, dqs, nlrun, grun, False, SB_TOP))
            return j - 1, _sb_live(nlrun), dqs, nlrun, grun

        carry = lax.while_loop(lambda c: (c[0] >= 0) & c[1], step, (i - 1, _sb_live(nlrun), dqs, nlrun, grun))
        dq_ref[...] = (0.125 * jnp.concatenate(carry[2], axis=1)).astype(BF16)

        def out_copies(g):
            cols = pl.ds(pl.multiple_of(g * SB_LANES, SB_LANES), SB_LANES)
            return (pltpu.make_async_copy(dk16, dk_hbm.at[:, cols], sems.at[0]),
                    pltpu.make_async_copy(dv16, dv_hbm.at[:, cols], sems.at[1]))

        @pl.when((i == nq - 1) & (g_idx > 0))
        def _():
            for cp in out_copies(g_idx - 1):
                cp.wait()

        @pl.when(i == nq - 1)
        def _():
            def narrow(r, carry):
                rows = pl.ds(pl.multiple_of(r * SB_BK, SB_BK), SB_BK)
                dk16[rows, :] = dk_acc[rows, :].astype(BF16)
                dv16[rows, :] = dv_acc[rows, :].astype(BF16)
                return carry

            lax.fori_loop(0, s // SB_BK, narrow, 0)
            for cp in out_copies(g_idx):
                cp.start()

        @pl.when((i == nq - 1) & (g_idx == ng - 1))
        def _():
            for cp in out_copies(g_idx):
                cp.wait()

    qblk = pl.BlockSpec((SB_BQ, SB_LANES), lambda g, i: (i, g))
    hbm = pl.BlockSpec(memory_space=pl.ANY)
    return pl.pallas_call(
        body, name=name, grid=(ng, nq),
        in_specs=[qblk, pl.BlockSpec((s, SB_LANES), lambda g, i: (0, ng + g)),
                  pl.BlockSpec((s, SB_LANES), lambda g, i: (0, 2 * ng + g)), qblk, qblk, *rd.in_specs],
        out_specs=[qblk, hbm, hbm, *rd.out_specs],
        out_shape=[jax.ShapeDtypeStruct((s, D), BF16)] * 3 + rd.out_shape,
        scratch_shapes=[pltpu.VMEM((s, SB_LANES), F32), pltpu.VMEM((s, SB_LANES), F32),
                        pltpu.VMEM((s, SB_LANES), BF16), pltpu.VMEM((s, SB_LANES), BF16),
                        pltpu.SemaphoreType.DMA((2,)),
                        pltpu.VMEM((1, SB_NCH, SB_BQ, SB_BK), F32), pltpu.VMEM((1, SB_NCH, SB_BQ, SB_BK), F32),
                        pltpu.VMEM((SB_NCH, SB_BQ, 2 * SB_BK), BF16), pltpu.VMEM((SB_NCH, SB_BQ, SB_BK), F32),
                        pltpu.VMEM((1, SB_NCH, SB_BQ, SB_BK), BF16), pltpu.VMEM((1, SB_NCH, SB_BQ, SB_BK), BF16),
                        *rd.scratch],
        compiler_params=_cp("arbitrary", "arbitrary"),
    )(qkv, qkv, qkv, o, do, *rides)


def _ssd_core(z, xpre, dtr, state, dtb, alog, dsk, nw):
    L = SSD_CHUNK
    xa = _silu(xpre)
    pieces = _split(xa, 128)
    xs = jnp.concatenate(pieces[:8], axis=1)
    bm, cm = pieces[8:10], pieces[10:12]
    dt = _softplus(dtr + dtb)
    a = dt * (-jnp.exp(alog))
    tri = (_iota((L, L), 0) >= _iota((L, L), 1)).astype(F32)
    a_cs = _dot("nn", tri, a, 1, 3)
    xc = xs * dt
    tril = _iota((L, L), 0) >= _iota((L, L), 1)
    lane_a = _iota((L, 128), 1) < 64
    acs_p = _split(a_cs, 128)
    xc_p = _split(xc, 128)
    ys, new_states = [], []
    for g in range(2):
        cb = _dot("nt", cm[g], bm[g])
        for pp in range(4):
            pair = 4 * g + pp
            acs = acs_p[pair]
            acs_t = acs.T
            xcp = xc_p[pair]
            st = state[pair]
            heads = []
            for hh in range(2):
                col = _take_fn(1, 64 * hh)(acs)
                row = _take_fn(0, 64 * hh)(acs_t)
                seg = col - row
                lm = jnp.where(tril, jnp.exp(jnp.where(tril, seg, 0.0)), 0.0)
                heads.append(_dot("nn", cb * lm, xcp))
            ydiag = jnp.where(lane_a, heads[0], heads[1])
            last = _take_fn(0, L - 1)(acs)
            snew = _dot("tn", xcp * jnp.exp(last - acs), bm[g])
            new_states.append(st * jnp.exp(_take_fn(1, L - 1)(acs_t)) + snew)
            yoff = _dot("nt", cm[g], st) * jnp.exp(acs)
            ys.append(ydiag + yoff)
    y = jnp.concatenate(ys, axis=1) + xs * dsk
    yg = y * _silu(z)
    outs = []
    for v in _split(yg, 512):
        outs.append(v * lax.rsqrt(jnp.mean(v * v, axis=-1, keepdims=True) + EPS))
    return jnp.concatenate(outs, axis=1) * nw, tuple(new_states)


XBC = 1536


def _ssd_conv(ext_ref, cw, cb):
    acc = cb
    for k in range(4):
        acc = acc + cw[k:k + 1, :] * ext_ref[pl.ds(HALO - 3 + k, SSD_CHUNK), :]
    return acc


def _ssd_fwd(z, xbc, dtr, cw, cb, lanes, rides, *, name):
    s = z.shape[0]
    L = SSD_CHUNK
    nc = s // L
    rd = _Rides(rides, [True] * len(rides))

    def body(*refs):
        ins, (y_ref, st_ref), (state, ext), handles = rd.split(refs, 7, 2, 2)
        z_ref, x_ref, h_ref, dtr_ref, cw_ref, cb_ref, ln_ref = ins
        c = pl.program_id(0)
        rd.run(handles, c == 0, c == nc - 1)

        @pl.when(c == 0)
        def _():
            state[...] = jnp.zeros_like(state)

        ext[0:HALO, :] = jnp.where(c == 0, 0.0, h_ref[...])
        ext[HALO:, :] = x_ref[...]
        xpre = _ssd_conv(ext, cw_ref[...], cb_ref[...])
        st_ref[0] = state[...]
        st_in = tuple(state[p] for p in range(8))
        yn, st_out = _ssd_core(z_ref[...], xpre, dtr_ref[...], st_in,
                               ln_ref[0:1, :], ln_ref[1:2, :], ln_ref[2:3, :], ln_ref[3:4, :])
        y_ref[...] = yn.astype(BF16)
        for p in range(8):
            state[p] = st_out[p]

    return pl.pallas_call(
        body, name=name, grid=(nc,),
        in_specs=[pl.BlockSpec((L, D), lambda c: (c, 0)),
                  pl.BlockSpec((L, XBC), lambda c: (c, 0)),
                  pl.BlockSpec((HALO, XBC), lambda c: (jnp.maximum(c * (L // HALO) - 1, 0), 0)),
                  pl.BlockSpec((L, D), lambda c: (c, 0)),
                  pl.BlockSpec((4, XBC), lambda c: (0, 0)),
                  pl.BlockSpec((1, XBC), lambda c: (0, 0)),
                  pl.BlockSpec((8, D), lambda c: (0, 0)), *rd.in_specs],
        out_specs=[pl.BlockSpec((L, D), lambda c: (c, 0)),
                   pl.BlockSpec((1, 8, 128, 128), lambda c: (c, 0, 0, 0)), *rd.out_specs],
        out_shape=[jax.ShapeDtypeStruct((s, D), BF16), jax.ShapeDtypeStruct((nc, 8, 128, 128), F32),
                   *rd.out_shape],
        scratch_shapes=[pltpu.VMEM((8, 128, 128), F32), pltpu.VMEM((L + HALO, XBC), F32), *rd.scratch],
        compiler_params=_cp("arbitrary"),
    )(z, xbc, xbc, dtr, cw, cb, lanes, *rides)


def _ssd_bwd(z, xbc, dtr, states, dymix, cw, cb, lanes, *, name):
    s = z.shape[0]
    L = SSD_CHUNK
    nc = s // L

    def body(z_ref, x_ref, h_ref, dtr_ref, st_ref, dy_ref, cw_ref, cb_ref, ln_ref,
             dz_ref, dx_ref, ddt_ref, dln_ref, dcv_ref, dstate, ext, dext):
        i = pl.program_id(0)
        c = nc - 1 - i

        @pl.when(i == 0)
        def _():
            dstate[...] = jnp.zeros_like(dstate)
            dext[...] = jnp.zeros_like(dext)
            dln_ref[...] = jnp.zeros_like(dln_ref)
            dcv_ref[...] = jnp.zeros_like(dcv_ref)

        ext[0:HALO, :] = jnp.where(c == 0, 0.0, h_ref[...])
        ext[HALO:, :] = x_ref[...]
        cwv = cw_ref[...]
        xpre = _ssd_conv(ext, cwv, cb_ref[...])
        st_in = tuple(st_ref[0, p] for p in range(8))
        _, vjp = jax.vjp(_ssd_core, z_ref[...], xpre, dtr_ref[...], st_in,
                         ln_ref[0:1, :], ln_ref[1:2, :], ln_ref[2:3, :], ln_ref[3:4, :])
        dz, dxpre, ddtr, dst, d0, d1, d2, d3 = vjp((dy_ref[...], tuple(dstate[p] for p in range(8))))
        for p in range(8):
            dstate[p] = dst[p]
        dz_ref[...] = dz.astype(BF16)
        ddt_ref[...] = ddtr.astype(BF16)
        dln_ref[0:4, :] += jnp.concatenate([d0, d1, d2, d3], axis=0)
        dext[0:L, :] = dxpre
        xcur = x_ref[...]
        dx = jnp.zeros((L, XBC), F32)
        rows = []
        for k in range(4):
            shifted = dext[pl.ds(3 - k, L), :]
            dx = dx + cwv[k:k + 1, :] * shifted
            rows.append(jnp.sum(shifted * xcur, axis=0, keepdims=True))
        rows.append(jnp.sum(dxpre, axis=0, keepdims=True))
        dx_ref[...] = dx.astype(BF16)
        dcv_ref[0:5, :] += jnp.concatenate(rows, axis=0)
        dext[L:L + HALO, :] = dxpre[0:HALO, :]

    rev = lambda i: (nc - 1 - i, 0)
    return pl.pallas_call(
        body, name=name, grid=(nc,),
        in_specs=[pl.BlockSpec((L, D), rev),
                  pl.BlockSpec((L, XBC), rev),
                  pl.BlockSpec((HALO, XBC), lambda i: (jnp.maximum((nc - 1 - i) * (L // HALO) - 1, 0), 0)),
                  pl.BlockSpec((L, D), rev),
                  pl.BlockSpec((1, 8, 128, 128), lambda i: (nc - 1 - i, 0, 0, 0)),
                  pl.BlockSpec((L, D), rev),
                  pl.BlockSpec((4, XBC), lambda i: (0, 0)),
                  pl.BlockSpec((1, XBC), lambda i: (0, 0)),
                  pl.BlockSpec((8, D), lambda i: (0, 0))],
        out_specs=[pl.BlockSpec((L, D), rev), pl.BlockSpec((L, XBC), rev), pl.BlockSpec((L, D), rev),
                   pl.BlockSpec((8, D), lambda i: (0, 0)), pl.BlockSpec((8, XBC), lambda i: (0, 0))],
        out_shape=[jax.ShapeDtypeStruct((s, D), BF16), jax.ShapeDtypeStruct((s, XBC), BF16),
                   jax.ShapeDtypeStruct((s, D), BF16), jax.ShapeDtypeStruct((8, D), F32),
                   jax.ShapeDtypeStruct((8, XBC), F32)],
        scratch_shapes=[pltpu.VMEM((8, 128, 128), F32), pltpu.VMEM((L + HALO, XBC), F32),
                        pltpu.VMEM((L + HALO, XBC), F32)],
        compiler_params=_cp("arbitrary"),
    )(z, xbc, xbc, dtr, states, dymix, cw, cb, lanes)


def _mem_attn_math(q, k, v):
    outs = []
    for qh, kh, vh in zip(_split(q, 256), _split(k, 256), _split(v, 256)):
        sc = _dot("nt", qh, kh) * (1.0 / 16.0)
        e = jnp.exp(sc - lax.stop_gradient(jnp.max(sc, axis=-1, keepdims=True)))
        p = e / jnp.sum(e, axis=-1, keepdims=True)
        outs.append(_dot("nn", p, vh))
    return jnp.concatenate(outs, axis=1)


def _mem_attn_fwd(q, k, v, *, name):
    s, m = q.shape[0], k.shape[0]
    tm = _tile(s, 256, 8)

    def body(q_ref, k_ref, v_ref, o_ref):
        o_ref[...] = _mem_attn_math(q_ref[...].astype(F32), k_ref[...].astype(F32),
                                    v_ref[...].astype(F32)).astype(BF16)

    row = pl.BlockSpec((tm, D), lambda i: (i, 0))
    kv = pl.BlockSpec((m, D), lambda i: (0, 0))
    return pl.pallas_call(
        body, name=name, grid=(s // tm,), in_specs=[row, kv, kv], out_specs=row,
        out_shape=jax.ShapeDtypeStruct((s, D), BF16), compiler_params=_cp("parallel"),
    )(q, k, v)


def _mem_attn_bwd(q, k, v, do, *, name):
    s, m = q.shape[0], k.shape[0]
    tm = _tile(s, 256, 8)

    def body(q_ref, k_ref, v_ref, do_ref, dq_ref, dk_ref, dv_ref):
        @pl.when(pl.program_id(0) == 0)
        def _():
            dk_ref[...] = jnp.zeros_like(dk_ref)
            dv_ref[...] = jnp.zeros_like(dv_ref)

        _, vjp = jax.vjp(_mem_attn_math, q_ref[...].astype(F32), k_ref[...].astype(F32),
                         v_ref[...].astype(F32))
        dq, dk, dv = vjp(do_ref[...])
        dq_ref[...] = dq.astype(BF16)
        dk_ref[...] += dk
        dv_ref[...] += dv

    row = pl.BlockSpec((tm, D), lambda i: (i, 0))
    kv = pl.BlockSpec((m, D), lambda i: (0, 0))
    return pl.pallas_call(
        body, name=name, grid=(s // tm,), in_specs=[row, kv, kv, row], out_specs=[row, kv, kv],
        out_shape=[jax.ShapeDtypeStruct((s, D), BF16), jax.ShapeDtypeStruct((m, D), F32),
                   jax.ShapeDtypeStruct((m, D), F32)],
        compiler_params=_cp("arbitrary"),
    )(q, k, v, do)


DFF = 2816
FFN_TC = 1408
FFN_TM = 256


FFN_CHUNKS = tuple((c, min(512, FFN_TC - c)) for c in range(0, FFN_TC, 512))


def _rows8(ref, r, cols):
    return ref[pl.ds(pl.multiple_of(r, HALO), HALO), cols]


def _shift_down(prev, cur, s):
    return jnp.where(_iota(cur.shape, 0) < s, pltpu.roll(prev, s, 0), pltpu.roll(cur, s, 0))


def _shift_up(cur, nxt, s):
    return jnp.where(_iota(cur.shape, 0) >= HALO - s, pltpu.roll(nxt, HALO - s, 0), pltpu.roll(cur, HALO - s, 0))


def _ffn_conv_strip(ext_ref, r, cols, cw, cb):
    prev, cur = _rows8(ext_ref, r, cols), _rows8(ext_ref, r + HALO, cols)
    return cb + cw[0:1, :] * _shift_down(prev, cur, 2) + cw[1:2, :] * _shift_down(prev, cur, 1) + cw[2:3, :] * cur


def _ffn_specs(s):
    tm, tc = FFN_TM, FFN_TC
    blk = pl.BlockSpec((tm, tc), lambda i, j: (i, j))
    halo = pl.BlockSpec((HALO, tc), lambda i, j: (jnp.maximum(i * (tm // HALO) - 1, 0), j))
    cw = pl.BlockSpec((3, tc), lambda i, j: (0, j))
    cb = pl.BlockSpec((1, tc), lambda i, j: (0, j))
    return tm, tc, blk, halo, cw, cb


def _glu_fwd(ug, uv, cwg, cwv, cbg, cbv, *, name):
    s = ug.shape[0]
    tm, tc, blk, halo, cw, cb = _ffn_specs(s)

    def body(g_ref, gh_ref, v_ref, vh_ref, cwg_ref, cwv_ref, cbg_ref, cbv_ref, f_ref, eg, ev):
        first = pl.program_id(0) == 0
        eg[0:HALO, :] = jnp.where(first, 0.0, gh_ref[...])
        eg[HALO:, :] = g_ref[...]
        ev[0:HALO, :] = jnp.where(first, 0.0, vh_ref[...])
        ev[HALO:, :] = v_ref[...]
        cwgv, cwvv, cbgv, cbvv = cwg_ref[...], cwv_ref[...], cbg_ref[...], cbv_ref[...]

        def step(t, carry):
            for c0, w in FFN_CHUNKS:
                cols = slice(c0, c0 + w)
                outs = []
                for h in range(2):
                    r = t * 16 + HALO * h
                    g = _ffn_conv_strip(eg, r, cols, cwgv[:, cols], cbgv[:, cols])
                    v = _ffn_conv_strip(ev, r, cols, cwvv[:, cols], cbvv[:, cols])
                    outs.append(_silu(g) * v)
                f_ref[pl.ds(pl.multiple_of(t * 16, 16), 16), cols] = jnp.concatenate(outs, axis=0).astype(BF16)
            return carry

        lax.fori_loop(0, tm // 16, step, 0)

    return pl.pallas_call(
        body, name=name, grid=(s // tm, DFF // tc),
        in_specs=[blk, halo, blk, halo, cw, cw, cb, cb], out_specs=blk,
        out_shape=jax.ShapeDtypeStruct((s, DFF), BF16),
        scratch_shapes=[pltpu.VMEM((tm + HALO, tc), F32)] * 2,
        compiler_params=_cp("parallel", "parallel"),
    )(ug, ug, uv, uv, cwg, cwv, cbg, cbv)


def _ffn_bwd(ug, uv, df, cwg, cwv, cbg, cbv, *, name):
    s = ug.shape[0]
    tm, tc = FFN_TM, FFN_TC
    nb = s // tm
    rows_ext = tm + HALO

    def body(g_ref, gp_ref, gn_ref, v_ref, vp_ref, vn_ref, df_ref, dfn_ref, cwg_ref, cwv_ref, cbg_ref, cbv_ref,
             dxg_ref, dxv_ref, dcg_ref, dcv_ref, eg, ev, edf, edg, edv, accg, accv):
        i = pl.program_id(1)
        first, last = i == 0, i == nb - 1

        @pl.when(first)
        def _():
            dcg_ref[...] = jnp.zeros_like(dcg_ref)
            dcv_ref[...] = jnp.zeros_like(dcv_ref)

        for e, prev, main, nxt in ((eg, gp_ref, g_ref, gn_ref), (ev, vp_ref, v_ref, vn_ref)):
            e[0:HALO, :] = jnp.where(first, 0.0, prev[...])
            e[HALO:HALO + tm, :] = main[...]
            e[HALO + tm:, :] = jnp.where(last, 0.0, nxt[...])
        edf[0:tm, :] = df_ref[...]
        edf[tm:, :] = jnp.where(last, 0.0, dfn_ref[...])
        accg[...] = jnp.zeros_like(accg)
        accv[...] = jnp.zeros_like(accv)
        cwgv, cwvv, cbgv, cbvv = cwg_ref[...], cwv_ref[...], cbg_ref[...], cbv_ref[...]

        def cotangents(t, carry):
            r = t * HALO
            for c0, w in FFN_CHUNKS:
                cols = slice(c0, c0 + w)
                g = _ffn_conv_strip(eg, r, cols, cwgv[:, cols], cbgv[:, cols])
                v = _ffn_conv_strip(ev, r, cols, cwvv[:, cols], cbvv[:, cols])
                dfs = _rows8(edf, r, cols)
                sg = _sigmoid(g)
                edv[pl.ds(pl.multiple_of(r, HALO), HALO), cols] = dfs * g * sg
                edg[pl.ds(pl.multiple_of(r, HALO), HALO), cols] = dfs * v * sg * (1.0 + g * (1.0 - sg))
            return carry

        lax.fori_loop(0, rows_ext // HALO, cotangents, 0)

        def conv_backward(t, carry):
            for c0, w in FFN_CHUNKS:
                cols = slice(c0, c0 + w)
                for edu, e, cw, dx_ref, acc in ((edg, eg, cwgv[:, cols], dxg_ref, accg),
                                                (edv, ev, cwvv[:, cols], dxv_ref, accv)):
                    dxs = []
                    for h in range(2):
                        r = t * 16 + HALO * h
                        cur, nxt = _rows8(edu, r, cols), _rows8(edu, r + HALO, cols)
                        up1, up2 = _shift_up(cur, nxt, 1), _shift_up(cur, nxt, 2)
                        x = _rows8(e, r + HALO, cols)
                        dxs.append(cw[2:3, :] * cur + cw[1:2, :] * up1 + cw[0:1, :] * up2)
                        acc[0, :, cols] += up2 * x
                        acc[1, :, cols] += up1 * x
                        acc[2, :, cols] += cur * x
                        acc[3, :, cols] += cur
                    dx_ref[pl.ds(pl.multiple_of(t * 16, 16), 16), cols] = jnp.concatenate(dxs, axis=0).astype(BF16)
            return carry

        lax.fori_loop(0, tm // 16, conv_backward, 0)
        for acc, dc_ref in ((accg, dcg_ref), (accv, dcv_ref)):
            dc_ref[0:4, :] += jnp.concatenate([jnp.sum(acc[k], axis=0, keepdims=True) for k in range(4)], axis=0)

    blk = pl.BlockSpec((tm, tc), lambda j, i: (i, j))
    nxt = pl.BlockSpec((HALO, tc), lambda j, i: (jnp.minimum((i + 1) * (tm // HALO), s // HALO - 1), j))
    prv = pl.BlockSpec((HALO, tc), lambda j, i: (jnp.maximum(i * (tm // HALO) - 1, 0), j))
    cw = pl.BlockSpec((3, tc), lambda j, i: (0, j))
    cb = pl.BlockSpec((1, tc), lambda j, i: (0, j))
    acc = pl.BlockSpec((8, tc), lambda j, i: (0, j))
    return pl.pallas_call(
        body, name=name, grid=(DFF // tc, nb),
        in_specs=[blk, prv, nxt, blk, prv, nxt, blk, nxt, cw, cw, cb, cb],
        out_specs=[blk, blk, acc, acc],
        out_shape=[jax.ShapeDtypeStruct((s, DFF), BF16)] * 2 + [jax.ShapeDtypeStruct((8, DFF), F32)] * 2,
        scratch_shapes=[pltpu.VMEM((tm + 2 * HALO, tc), F32)] * 2 + [pltpu.VMEM((rows_ext, tc), F32)] * 3
                       + [pltpu.VMEM((4, HALO, tc), F32)] * 2,
        compiler_params=_cp("parallel", "arbitrary"),
    )(ug, ug, ug, uv, uv, uv, df, df, cwg, cwv, cbg, cbv)


MESH = pl.DeviceIdType.MESH


def _all_gather(arrs, *, name):
    n = len(arrs)

    def body(*refs):
        x_refs, out_refs = refs[:n], refs[n:2 * n]
        send_sems, recv_sems, local_sems = refs[2 * n:]
        x, y, c = lax.axis_index("x"), lax.axis_index("y"), lax.axis_index("c")
        me, sibling = (x, y, c), (x, y, 1 - c)
        chips = [(1 - x, y), (x, 1 - y), (1 - x, 1 - y)]

        def blk(a, dev):
            return out_refs[a].at[4 * dev[0] + 2 * dev[1] + dev[2]]

        def copy(a, k, block, to, src=None):
            return pltpu.make_async_remote_copy(
                src_ref=blk(a, block) if src is None else src, dst_ref=blk(a, block),
                send_sem=send_sems.at[7 * a + k], recv_sem=recv_sems.at[7 * a + k],
                device_id=to, device_id_type=MESH)

        started = []
        mine = []
        for a in range(n):
            cp = pltpu.make_async_copy(x_refs[a], blk(a, me), local_sems.at[a])
            cp.start()
            mine.append(cp)
            first = [copy(a, 0, me, sibling, src=x_refs[a])]
            first += [copy(a, 1 + j, me, (*chip, c), src=x_refs[a]) for j, chip in enumerate(chips)]
            for cp in first:
                cp.start()
            started += first
        for a in range(n):
            for j, chip in enumerate(chips):
                copy(a, 1 + j, (*chip, c), me).wait_recv()
                fwd = copy(a, 4 + j, (*chip, c), sibling)
                fwd.start()
                started.append(fwd)
        for a in range(n):
            copy(a, 0, sibling, me).wait_recv()
            for j, chip in enumerate(chips):
                copy(a, 4 + j, (*chip, 1 - c), me).wait_recv()
        for cp in started:
            cp.wait_send()
        for cp in mine:
            cp.wait()

    any_spec = pl.BlockSpec(memory_space=pl.ANY)
    return pl.pallas_call(
        body, name=name,
        in_specs=[any_spec] * n, out_specs=[any_spec] * n,
        out_shape=[jax.ShapeDtypeStruct((NDEV,) + a.shape, a.dtype) for a in arrs],
        scratch_shapes=[pltpu.SemaphoreType.DMA((7 * n,)), pltpu.SemaphoreType.DMA((7 * n,)),
                        pltpu.SemaphoreType.DMA((n,))],
    )(*arrs)


class _Direct:
    SEMS = (pltpu.SemaphoreType.DMA((7,)), pltpu.SemaphoreType.DMA((7,)), pltpu.SemaphoreType.DMA((1,)))

    def __init__(self, src_ref, recv_ref, sems, gather):
        x, y, c = lax.axis_index("x"), lax.axis_index("y"), lax.axis_index("c")
        me = 4 * x + 2 * y + c
        send_sems, recv_sems, local_sem = sems
        src = (lambda pid: src_ref) if gather else (lambda pid: src_ref.at[pid])
        self.mine = pltpu.make_async_copy(src(me), recv_ref.at[me], local_sem.at[0])
        self.copies = []
        for k in range(1, NDEV):
            px = 1 - x if k & 4 else x
            py = 1 - y if k & 2 else y
            pc = 1 - c if k & 1 else c
            self.copies.append(pltpu.make_async_remote_copy(
                src_ref=src(4 * px + 2 * py + pc), dst_ref=recv_ref.at[me],
                send_sem=send_sems.at[k - 1], recv_sem=recv_sems.at[k - 1],
                device_id=(px, py, pc), device_id_type=MESH))

    def start(self):
        self.mine.start()
        for cp in self.copies:
            cp.start()

    def wait(self):
        for cp in self.copies:
            cp.wait_recv()
        for cp in self.copies:
            cp.wait_send()
        self.mine.wait()


def _recv_shape(src, gather):
    return jax.ShapeDtypeStruct(((NDEV,) + src.shape) if gather else src.shape, src.dtype)


class _Rides:
    def __init__(self, rides, gathers):
        self.n = len(rides)
        self.gathers = list(gathers)
        any_spec = pl.BlockSpec(memory_space=pl.ANY)
        self.in_specs = [any_spec] * self.n
        self.out_specs = [any_spec] * self.n
        self.out_shape = [_recv_shape(a, g) for a, g in zip(rides, gathers)]
        self.scratch = list(_Direct.SEMS) * self.n

    def split(self, refs, n_in, n_out, n_scratch):
        n = self.n
        ins, refs = refs[:n_in], refs[n_in:]
        rides, refs = refs[:n], refs[n:]
        outs, refs = refs[:n_out], refs[n_out:]
        gots, refs = refs[:n], refs[n:]
        scratch, sems = refs[:n_scratch], refs[n_scratch:]
        return ins, outs, scratch, (rides, gots, sems)

    def run(self, handles, first, last):
        rides, gots, sems = handles

        def all_of():
            return [_Direct(rides[a], gots[a], sems[3 * a:3 * a + 3], self.gathers[a]) for a in range(self.n)]

        @pl.when(first)
        def _():
            for e in all_of():
                e.start()

        @pl.when(last)
        def _():
            for e in all_of():
                e.wait()


def _exchange(arrs, gathers, *, name):
    rd = _Rides(arrs, gathers)

    def body(*refs):
        _, _, _, handles = rd.split(refs, 0, 0, 0)
        rd.run(handles, True, True)

    return pl.pallas_call(
        body, name=name, in_specs=rd.in_specs, out_specs=rd.out_specs, out_shape=rd.out_shape,
        scratch_shapes=rd.scratch,
    )(*arrs)


def _adamw(parts, w, m, v, *, name):
    r, cols = w.shape
    tm = _tile(r, 256, PACK_ALIGN)
    c1 = 1.0 - ADAM_B1 ** ADAM_STEP
    c2 = 1.0 - ADAM_B2 ** ADAM_STEP

    def body(p_ref, w_ref, m_ref, v_ref, g_ref, d_ref, nm_ref, nv_ref):
        g = p_ref[0].astype(F32)
        for i in range(1, NDEV):
            g = g + p_ref[i].astype(F32)
        nm = ADAM_B1 * m_ref[...] + (1.0 - ADAM_B1) * g
        nv = ADAM_B2 * v_ref[...] + (1.0 - ADAM_B2) * (g * g)
        d_ref[...] = -ADAM_LR * ((nm / c1) / (jnp.sqrt(nv / c2) + ADAM_EPS) + ADAM_WD * w_ref[...])
        g_ref[...] = g
        nm_ref[...] = nm
        nv_ref[...] = nv

    row = pl.BlockSpec((tm, cols), lambda i: (i, 0))
    return pl.pallas_call(
        body, name=name, grid=(r // tm,),
        in_specs=[pl.BlockSpec((NDEV, tm, cols), lambda i: (0, i, 0)), row, row, row],
        out_specs=[row] * 4, out_shape=[jax.ShapeDtypeStruct((r, cols), F32)] * 4,
        compiler_params=_cp("parallel"),
    )(parts, w, m, v)


PACK_ALIGN = 16


def _part_rows(shape):
    n = -(-math.prod(shape) // D)
    return n + (-n) % PACK_ALIGN


def _rows(a):
    flat = a.reshape(-1)
    pad = _part_rows(a.shape) * D - flat.shape[0]
    if pad:
        flat = jnp.concatenate([flat, jnp.zeros((pad,), flat.dtype)])
    return flat.reshape(-1, D)


def _pack(parts, total_rows):
    if all(math.prod(p.shape) % (PACK_ALIGN * D) for p in parts):
        return _pack_small(parts, total_rows)
    rows = [_rows(p) for p in parts]
    used = sum(r.shape[0] for r in rows)
    if total_rows > used:
        rows.append(jnp.zeros((total_rows - used, D), rows[0].dtype))
    return jnp.concatenate(rows, axis=0)


def _pack_small(parts, total_rows):
    flat, used = [], 0
    for p in parts:
        n, nr = math.prod(p.shape), _part_rows(p.shape)
        flat += [p.reshape(-1), jnp.zeros((nr * D - n,), p.dtype)]
        used += nr
    flat.append(jnp.zeros(((total_rows - used) * D,), parts[0].dtype))
    return jnp.concatenate(flat).reshape(total_rows, D)


def _unpack(buf, shapes, part_rows=_part_rows):
    out, r0 = [], 0
    for shp in shapes:
        n = math.prod(shp)
        out.append(buf[r0:r0 + part_rows(shp)].reshape(-1)[:n].reshape(shp))
        r0 += part_rows(shp)
    return out


def _tight_rows(shape):
    return -(-math.prod(shape) // D)


def _pack_tight(parts, total_rows):
    flat, used = [], 0
    for p in parts:
        n, nr = math.prod(p.shape), _tight_rows(p.shape)
        flat += [p.reshape(-1), jnp.zeros((nr * D - n,), p.dtype)]
        used += nr
    flat.append(jnp.zeros(((total_rows - used) * D,), parts[0].dtype))
    return jnp.concatenate(flat).reshape(total_rows, D)


SHARD = {"w_in": (D, 706), "w_out": (256, D), "w_mq": (128, D), "w_mk": (128, D), "w_mv": (128, D),
         "w_mo": (128, D), "w_up": (D, 704), "w_down": (352, D), "conv_ssd_w": (4, 192), "conv_ffn_w": (3, 704)}
GATHER_MID = ["w_out", "w_mq", "w_mk", "w_mv", "w_mo"]
GATHER_FFN = ["w_down"]
CONV_TAPS = ["conv_ssd_w", "conv_ffn_w"]
GRADS_PACKED = ["w_out", "w_mq", "w_mk", "w_mv", "w_mo", "w_down", "conv_ffn_w", "conv_ssd_w"]


def _layout(names):
    row0, r = {}, 0
    for n in names:
        row0[n] = r
        r += _part_rows(SHARD[n])
    return row0, r + (-r) % 128


SMALL = [("norm_mix_w", (1, D)), ("conv_ssd_b", (1, 1536)), ("dt_bias", (1, 16)), ("a_log", (1, 16)),
         ("d_skip", (1, 16)), ("ssd_norm_w", (1, D)), ("sb_norm_w", (1, D)), ("norm_mem_w", (1, D)),
         ("norm_memkv_w", (1, D)), ("norm_ffn_w", (1, D)), ("conv_ffn_b", (1, 5632)), ("norm_final_w", (D,))]
LOSS_ROW = sum(_tight_rows(_shp) for _, _shp in SMALL)
SMALL_ROWS = LOSS_ROW + 1 + (-(LOSS_ROW + 1)) % 8
ORDER = ["norm_mix_w", "w_in", "conv_ssd_w", "conv_ssd_b", "dt_bias", "a_log", "d_skip", "ssd_norm_w",
         "sb_norm_w", "w_out", "norm_mem_w", "norm_memkv_w", "w_mq", "w_mk", "w_mv", "w_mo", "norm_ffn_w",
         "w_up", "conv_ffn_w", "conv_ffn_b", "w_down", "norm_final_w"]


def _pad_rows(a, nr):
    n = a.shape[1]
    return jnp.concatenate([a, jnp.zeros((NDEV, nr * D - n), a.dtype)], axis=1).reshape(NDEV, nr, D)


def _group_sum(lanes):
    return lanes.reshape(16, 64).sum(axis=1).reshape(1, 16)


def kernel(x, mem, norm_mix_w, w_in, conv_ssd_w, conv_ssd_b, dt_bias, a_log, d_skip, ssd_norm_w, sb_norm_w, w_out, norm_mem_w, norm_memkv_w, w_mq, w_mk, w_mv, w_mo, norm_ffn_w, w_up, conv_ffn_w, conv_ffn_b, w_down, norm_final_w, loss_target, m_norm_mix_w, m_w_in, m_conv_ssd_w, m_conv_ssd_b, m_dt_bias, m_a_log, m_d_skip, m_ssd_norm_w, m_sb_norm_w, m_w_out, m_norm_mem_w, m_norm_memkv_w, m_w_mq, m_w_mk, m_w_mv, m_w_mo, m_norm_ffn_w, m_w_up, m_conv_ffn_w, m_conv_ffn_b, m_w_down, m_norm_final_w, v_norm_mix_w, v_w_in, v_conv_ssd_w, v_conv_ssd_b, v_dt_bias, v_a_log, v_d_skip, v_ssd_norm_w, v_sb_norm_w, v_w_out, v_norm_mem_w, v_norm_memkv_w, v_w_mq, v_w_mk, v_w_mv, v_w_mo, v_norm_ffn_w, v_w_up, v_conv_ffn_w, v_conv_ffn_b, v_w_down, v_norm_final_w):
    P = dict(norm_mix_w=norm_mix_w, w_in=w_in, conv_ssd_w=conv_ssd_w, conv_ssd_b=conv_ssd_b, dt_bias=dt_bias, a_log=a_log, d_skip=d_skip, ssd_norm_w=ssd_norm_w, sb_norm_w=sb_norm_w, w_out=w_out, norm_mem_w=norm_mem_w, norm_memkv_w=norm_memkv_w, w_mq=w_mq, w_mk=w_mk, w_mv=w_mv, w_mo=w_mo, norm_ffn_w=norm_ffn_w, w_up=w_up, conv_ffn_w=conv_ffn_w, conv_ffn_b=conv_ffn_b, w_down=w_down, norm_final_w=norm_final_w)
    M = dict(norm_mix_w=m_norm_mix_w, w_in=m_w_in, conv_ssd_w=m_conv_ssd_w, conv_ssd_b=m_conv_ssd_b, dt_bias=m_dt_bias, a_log=m_a_log, d_skip=m_d_skip, ssd_norm_w=m_ssd_norm_w, sb_norm_w=m_sb_norm_w, w_out=m_w_out, norm_mem_w=m_norm_mem_w, norm_memkv_w=m_norm_memkv_w, w_mq=m_w_mq, w_mk=m_w_mk, w_mv=m_w_mv, w_mo=m_w_mo, norm_ffn_w=m_norm_ffn_w, w_up=m_w_up, conv_ffn_w=m_conv_ffn_w, conv_ffn_b=m_conv_ffn_b, w_down=m_w_down, norm_final_w=m_norm_final_w)
    V = dict(norm_mix_w=v_norm_mix_w, w_in=v_w_in, conv_ssd_w=v_conv_ssd_w, conv_ssd_b=v_conv_ssd_b, dt_bias=v_dt_bias, a_log=v_a_log, d_skip=v_d_skip, ssd_norm_w=v_ssd_norm_w, sb_norm_w=v_sb_norm_w, w_out=v_w_out, norm_mem_w=v_norm_mem_w, norm_memkv_w=v_norm_memkv_w, w_mq=v_w_mq, w_mk=v_w_mk, w_mv=v_w_mv, w_mo=v_w_mo, norm_ffn_w=v_norm_ffn_w, w_up=v_w_up, conv_ffn_w=v_conv_ffn_w, conv_ffn_b=v_conv_ffn_b, w_down=v_w_down, norm_final_w=v_norm_final_w)
    small_shapes = [shp for _, shp in SMALL]

    def packed(src, names, dtype=F32):
        return _pack([src[n][0] for n in names], _layout(names)[1]).astype(dtype)

    def columns(g):
        return g.transpose(1, 0, 2).reshape(g.shape[1], NDEV * g.shape[2])

    g_in, g_taps = _all_gather([w_in[0].astype(BF16), packed(P, CONV_TAPS)], name="gather_w_in")
    W_in = columns(g_in)
    cw_ssd = g_taps[:, 0].reshape(NDEV, -1)[:, :768].reshape(NDEV, 4, 192).transpose(1, 0, 2).reshape(4, XBC)
    cw_ffn = (g_taps[:, PACK_ALIGN:PACK_ALIGN + 3].reshape(NDEV, -1)[:, :2112].reshape(NDEV, 3, 704)
              .transpose(1, 0, 2).reshape(3, 2 * DFF))
    W_z, W_xbc, W_dt, W_qkv = W_in[:, :D], W_in[:, D:D + XBC], W_in[:, D + XBC:D + XBC + 16], W_in[:, D + XBC + 16:]
    W_dtr = jnp.repeat(W_dt, 64, axis=1)
    cwg, cwv = cw_ffn[:, :DFF], cw_ffn[:, DFF:]
    cbg, cbv = conv_ffn_b[:, :DFF], conv_ffn_b[:, DFF:]
    rep = lambda p: jnp.repeat(p, 64, axis=1)
    lanes = jnp.concatenate([rep(dt_bias), rep(a_log), rep(d_skip), ssd_norm_w, jnp.zeros((4, D), F32)], axis=0)

    xs, tgt, mm = x[0], loss_target[0], mem[0]

    h1 = _norm_fwd(xs, norm_mix_w, name="norm_mix")
    z = _mm(h1, W_z, name="proj_z")
    xbc = _mm(h1, W_xbc, name="proj_xbc")
    dtr = _mm(h1, W_dtr, name="proj_dt")
    qkv = _mm(h1, W_qkv, name="proj_qkv", out_dtype=BF16)
    y_ssd, states, g_ffn = _ssd_fwd(z, xbc, dtr, cw_ssd, conv_ssd_b, lanes, [packed(P, GATHER_FFN, BF16)],
                                    name="ssd_fwd")
    o_sb, g_mid, g_up = _sb_fwd(qkv, [packed(P, GATHER_MID, BF16), w_up[0].astype(BF16)], name="sb_fwd")
    r_mid = _layout(GATHER_MID)[0]
    W_out = g_mid[:, r_mid["w_out"]:r_mid["w_out"] + 256].reshape(2 * D, D)
    W_mq, W_mk, W_mv, W_mo = [g_mid[:, r_mid[n]:r_mid[n] + 128].reshape(D, D)
                              for n in ("w_mq", "w_mk", "w_mv", "w_mo")]
    W_up = columns(g_up)
    W_down = g_ffn[:, 0:352].reshape(DFF, D)
    W_upg, W_upv = W_up[:, :DFF], W_up[:, DFF:]
    y_sb = _head_norm_fwd(o_sb, sb_norm_w, name="sb_norm")
    ymix = jnp.concatenate([y_ssd, y_sb], axis=1)
    x1 = _mm(ymix, W_out, add=xs, name="proj_out")
    h2 = _norm_fwd(x1, norm_mem_w, name="norm_mem")
    mn = _norm_fwd(mm, norm_memkv_w, name="norm_memkv")
    qm = _mm(h2, W_mq, name="mem_q", out_dtype=BF16)
    km = _mm(mn, W_mk, name="mem_k", out_dtype=BF16)
    vm = _mm(mn, W_mv, name="mem_v", out_dtype=BF16)
    om = _mem_attn_fwd(qm, km, vm, name="mem_attn")
    x2 = _mm(om, W_mo, add=x1, name="mem_o")
    h3 = _norm_fwd(x2, norm_ffn_w, name="norm_ffn")
    ug = _mm(h3, W_upg, name="ffn_up_g")
    uv = _mm(h3, W_upv, name="ffn_up_v")
    f = _glu_fwd(ug, uv, cwg, cwv, cbg, cbv, name="ffn_glu")
    x3 = _mm(f, W_down, add=x2, name="ffn_down")
    dx3, g_nfinal, loss_part = _final(x3, norm_final_w.reshape(1, D), tgt, name="final_loss")

    G = {}
    G["w_down"] = _mm(f, dx3, trans_a=True, name="g_w_down")
    df = _mm(dx3, W_down, trans_b=True, name="d_f")
    dupg, dupv, dcg, dcv = _ffn_bwd(ug, uv, df, cwg, cwv, cbg, cbv, name="ffn_glu_bwd")
    G["w_up"] = jnp.concatenate([_mm(h3, dupg, trans_a=True, name="g_w_up_g"),
                                 _mm(h3, dupv, trans_a=True, name="g_w_up_v")], axis=1)
    G["conv_ffn_w"] = jnp.concatenate([dcg[0:3], dcv[0:3]], axis=1)
    G["conv_ffn_b"] = jnp.concatenate([dcg[3:4], dcv[3:4]], axis=1)
    dh3 = _mm(dupg, W_upg, trans_b=True, name="d_h3_g")
    dh3 = _mm(dupv, W_upv, trans_b=True, add=dh3, name="d_h3_v")
    dx2, G["norm_ffn_w"] = _norm_bwd(x2, norm_ffn_w, dh3, dx3, name="norm_ffn_bwd")
    G["w_mo"] = _mm(om, dx2, trans_a=True, name="g_w_mo")
    dom = _mm(dx2, W_mo, trans_b=True, name="d_om")
    dqm, dkm, dvm = _mem_attn_bwd(qm, km, vm, dom, name="mem_attn_bwd")
    G["w_mq"] = _mm(h2, dqm, trans_a=True, name="g_w_mq")
    G["w_mk"] = _mm(mn, dkm, trans_a=True, name="g_w_mk")
    G["w_mv"] = _mm(mn, dvm, trans_a=True, name="g_w_mv")
    dh2 = _mm(dqm, W_mq, trans_b=True, name="d_h2")
    dmn = _mm(dkm, W_mk, trans_b=True, name="d_mn_k")
    dmn = _mm(dvm, W_mv, trans_b=True, add=dmn, name="d_mn_v")
    _, G["norm_memkv_w"] = _norm_bwd(mm, norm_memkv_w, dmn, None, name="norm_memkv_bwd")
    dx1, G["norm_mem_w"] = _norm_bwd(x1, norm_mem_w, dh2, dx2, name="norm_mem_bwd")
    G["w_out"] = _mm(ymix, dx1, trans_a=True, name="g_w_out")
    dymix = _mm(dx1, W_out, trans_b=True, name="d_ymix")
    do_sb, G["sb_norm_w"] = _head_norm_bwd(o_sb, sb_norm_w, dymix, name="sb_norm_bwd")

    dz, dxbc, ddtr, dlanes, dconv = _ssd_bwd(z, xbc, dtr, states, dymix, cw_ssd, conv_ssd_b, lanes, name="ssd_bwd")
    G["dt_bias"], G["a_log"], G["d_skip"] = [_group_sum(dlanes[i:i + 1]) for i in range(3)]
    G["ssd_norm_w"] = dlanes[3:4]
    G["conv_ssd_w"], G["conv_ssd_b"] = dconv[0:4], dconv[4:5]

    def col_slabs(g, cols):
        return g.reshape(g.shape[0], NDEV, cols).transpose(1, 0, 2).astype(BF16)

    def packed_slabs(names):
        parts = []
        for n in names:
            shp = SHARD[n]
            if shp[-1] == D:
                t = G[n].reshape((NDEV,) + shp)
            else:
                t = _pad_rows(G[n].reshape(shp[0], NDEV, shp[1]).transpose(1, 0, 2).reshape(NDEV, -1),
                              _part_rows(shp))
            parts.append(jnp.pad(t, ((0, 0), (0, _part_rows(shp) - t.shape[1]), (0, 0))))
        used = sum(t.shape[1] for t in parts)
        parts.append(jnp.zeros((NDEV, _layout(names)[1] - used, D), F32))
        return jnp.concatenate(parts, axis=1).astype(BF16)

    dq, dk, dv, recv_packed, recv_up = _sb_bwd(qkv, o_sb, do_sb, [packed_slabs(GRADS_PACKED), col_slabs(G["w_up"], 704)],
                                               name="sb_bwd")
    dproj = jnp.concatenate([dz, dxbc, ddtr, dq, dk, dv], axis=1)
    g_proj = _mm(h1, dproj, trans_a=True, name="g_w_in")
    c_dt = D + XBC
    G["w_in"] = jnp.concatenate([g_proj[:, :c_dt], g_proj[:, c_dt:c_dt + D].reshape(D, 16, 64).sum(axis=2),
                                 g_proj[:, c_dt + D:]], axis=1)
    W_proj = jnp.concatenate([W_z, W_xbc, W_dtr, W_qkv], axis=1)
    dh1, recv_in = _mm(dproj, W_proj, trans_b=True, rides=[col_slabs(G["w_in"], 706)], name="d_h1")
    dx, G["norm_mix_w"] = _norm_bwd(xs, norm_mix_w, dh1, dx1, name="norm_mix_bwd")
    G["norm_final_w"] = g_nfinal.reshape(D)

    small_g = _pack_tight([G[n] for n, _ in SMALL] + [loss_part], SMALL_ROWS)
    (parts_small,) = _exchange([small_g], [True], name="exchange_grads")
    outs_packed = _adamw(recv_packed, packed(P, GRADS_PACKED), packed(M, GRADS_PACKED), packed(V, GRADS_PACKED),
                         name="adamw_packed")
    outs_up = _adamw(recv_up, w_up[0], m_w_up[0], v_w_up[0], name="adamw_w_up")
    outs_in = _adamw(recv_in, w_in[0], m_w_in[0], v_w_in[0], name="adamw_w_in")
    outs_small = _adamw(parts_small, _pack_tight([P[n] for n, _ in SMALL], SMALL_ROWS),
                        _pack_tight([M[n] for n, _ in SMALL], SMALL_ROWS),
                        _pack_tight([V[n] for n, _ in SMALL], SMALL_ROWS), name="adamw_replicated")

    res = {}
    for i, kind in enumerate(("grad", "delta", "new_m", "new_v")):
        for n, val in zip(GRADS_PACKED, _unpack(outs_packed[i], [SHARD[n] for n in GRADS_PACKED])):
            res[kind, n] = val.reshape((1,) + SHARD[n])
        res[kind, "w_up"] = outs_up[i].reshape((1,) + SHARD["w_up"])
        res[kind, "w_in"] = outs_in[i].reshape((1,) + SHARD["w_in"])
        for (n, shp), val in zip(SMALL, _unpack(outs_small[i], small_shapes, _tight_rows)):
            res[kind, n] = val
    loss = outs_small[0][LOSS_ROW, 0]
    out = [loss, dx.reshape(1, -1, D)]
    for kind in ("grad", "delta", "new_m", "new_v"):
        out += [res[kind, n] for n in ORDER]
    return tuple(out)
```

```python
import functools
import math

import jax
import jax.numpy as jnp
from jax import lax
from jax.experimental import pallas as pl
from jax.experimental.pallas import tpu as pltpu

F32 = jnp.float32
BF16 = jnp.bfloat16

D = 1024
NDEV = 8
EPS = 1e-6
SSD_CHUNK = 128
HALO = 8
VMEM_LIMIT = 56 * 2**20

ADAM_LR, ADAM_B1, ADAM_B2, ADAM_EPS, ADAM_WD, ADAM_STEP = 0.001, 0.9, 0.999, 1e-08, 0.01, 10


def _cp(*sem):
    return pltpu.CompilerParams(dimension_semantics=sem, vmem_limit_bytes=VMEM_LIMIT)


def _tile(n, cap, mult):
    if n <= cap:
        return n
    for d in range(cap - cap % mult, 0, -mult):
        if n % d == 0:
            return d
    raise ValueError(f"no tile for {n}")


def _sigmoid(x):
    return 1.0 / (1.0 + jnp.exp(-x))


def _silu(x):
    return x * _sigmoid(x)


def _softplus(x):
    return jnp.maximum(x, 0.0) + jnp.log(1.0 + jnp.exp(-jnp.abs(x)))


def _terms(x, n):
    out = []
    r = x.astype(F32)
    for i in range(n):
        h = r.astype(BF16)
        out.append(h)
        if i + 1 < n:
            r = r - h.astype(F32)
    return out


_DIMS = {"nn": ((1,), (0,)), "nt": ((1,), (1,)), "tn": ((0,), (0,))}


def _dot_raw(form, a, b, ta, tb):
    acc = None
    for ai in _terms(a, ta):
        for bi in _terms(b, tb):
            d = lax.dot_general(ai, bi, (_DIMS[form], ((), ())), preferred_element_type=F32)
            acc = d if acc is None else acc + d
    return acc


@functools.lru_cache(maxsize=None)
def _dot_fn(form, ta, tb):
    @jax.custom_vjp
    def f(a, b):
        return _dot_raw(form, a, b, ta, tb)

    def fwd(a, b):
        return f(a, b), (a, b)

    def bwd(res, ct):
        a, b = res
        if form == "nn":
            return _dot_fn("nt", ta, tb)(ct, b), _dot_fn("tn", ta, tb)(a, ct)
        if form == "nt":
            return _dot_fn("nn", ta, tb)(ct, b), _dot_fn("tn", tb, ta)(ct, a)
        return _dot_fn("nt", tb, ta)(b, ct), _dot_fn("nn", ta, tb)(a, ct)

    f.defvjp(fwd, bwd)
    return f


def _dot(form, a, b, ta=1, tb=1):
    return _dot_fn(form, ta, tb)(a, b)


@functools.lru_cache(maxsize=None)
def _take_fn(axis, idx):
    @jax.custom_vjp
    def f(x):
        return x[:, idx:idx + 1] if axis == 1 else x[idx:idx + 1, :]

    def fwd(x):
        return f(x), x.shape

    def bwd(shape, ct):
        io = lax.broadcasted_iota(jnp.int32, shape, axis)
        return (jnp.where(io == idx, jnp.broadcast_to(ct, shape), 0.0),)

    f.defvjp(fwd, bwd)
    return f


@functools.lru_cache(maxsize=None)
def _split_fn(width, n):
    @jax.custom_vjp
    def f(x):
        return tuple(x[:, i * width:(i + 1) * width] for i in range(n))

    def fwd(x):
        return f(x), None

    def bwd(_, cts):
        return (jnp.concatenate(list(cts), axis=1),)

    f.defvjp(fwd, bwd)
    return f


def _split(x, width):
    return _split_fn(width, x.shape[1] // width)(x)


def _iota(shape, axis):
    return lax.broadcasted_iota(jnp.int32, shape, axis)


MM_VMEM_BUDGET = 44 * 2**20


def _mm_tiles(m, n, kt, trans_a, a_bytes, b_bytes, out_bytes, add_bytes):
    tn = _tile(n, 1536, 128)
    for tm_cap in (1408, 1024, 512, 256, 128):
        tm = _tile(m, tm_cap, 128 if trans_a else 8)
        for tk_cap in (kt, 4096, 2048, 1024, 512):
            tk = _tile(kt, tk_cap, 128)
            blocks = tm * tk * a_bytes + tk * tn * b_bytes + tm * tn * (out_bytes + add_bytes)
            if 2 * blocks + (tm * tn * 4 if tk < kt else 0) <= MM_VMEM_BUDGET:
                return tm, tn, tk
    raise ValueError(f"no matmul tiling for {(m, n, kt)}")


def _mm(a, b, *, name, add=None, trans_a=False, trans_b=False, out_dtype=F32, rides=()):
    assert not (trans_a and trans_b)
    if trans_a:
        kt, m = a.shape
    else:
        m, kt = a.shape
    n, kt2 = b.shape if trans_b else b.shape[::-1]
    assert kt == kt2, (a.shape, b.shape)
    tm, tn, tk = _mm_tiles(m, n, kt, trans_a, a.dtype.itemsize, b.dtype.itemsize,
                           jnp.dtype(out_dtype).itemsize, 0 if add is None else add.dtype.itemsize)
    nk = kt // tk
    grid = (m // tm, n // tn, nk)
    rd = _Rides(rides, [False] * len(rides))
    n_in = 2 if add is None else 3

    def body(*all_refs):
        ins, (o_ref,), scratch, handles = rd.split(all_refs, n_in, 1, 1 if nk > 1 else 0)
        refs = (*ins, o_ref, *scratch)
        if rides:
            ids = [pl.program_id(ax) for ax in range(3)]
            rd.run(handles, (ids[0] == 0) & (ids[1] == 0) & (ids[2] == 0),
                   (ids[0] == grid[0] - 1) & (ids[1] == grid[1] - 1) & (ids[2] == grid[2] - 1))
        if add is None:
            a_ref, b_ref, o_ref = refs[:3]
        else:
            a_ref, b_ref, add_ref, o_ref = refs[:4]
        k = pl.program_id(2)
        av = a_ref[...].astype(BF16)
        bv = b_ref[...].astype(BF16)
        dims = _DIMS["tn" if trans_a else "nt" if trans_b else "nn"]
        d = lax.dot_general(av, bv, (dims, ((), ())), preferred_element_type=F32)

        def finish(r):
            if add is not None:
                r = r + add_ref[...]
            o_ref[...] = r.astype(out_dtype)

        if nk == 1:
            finish(d)
        else:
            acc = refs[-1]

            @pl.when(k == 0)
            def _():
                acc[...] = d

            @pl.when((k > 0) & (k < nk - 1))
            def _():
                acc[...] += d

            @pl.when(k == nk - 1)
            def _():
                finish(acc[...] + d)

    a_spec = (pl.BlockSpec((tk, tm), lambda i, j, k: (k, i)) if trans_a
              else pl.BlockSpec((tm, tk), lambda i, j, k: (i, k)))
    b_spec = (pl.BlockSpec((tn, tk), lambda i, j, k: (j, k)) if trans_b
              else pl.BlockSpec((tk, tn), lambda i, j, k: (k, j)))
    in_specs = [a_spec, b_spec]
    args = [a, b]
    if add is not None:
        in_specs.append(pl.BlockSpec((tm, tn), lambda i, j, k: (i, j)))
        args.append(add)
    out = pl.pallas_call(
        body, name=name, grid=grid,
        in_specs=in_specs + rd.in_specs,
        out_specs=[pl.BlockSpec((tm, tn), lambda i, j, k: (i, j))] + rd.out_specs,
        out_shape=[jax.ShapeDtypeStruct((m, n), out_dtype)] + rd.out_shape,
        scratch_shapes=([pltpu.VMEM((tm, tn), F32)] if nk > 1 else []) + rd.scratch,
        compiler_params=_cp(*(("arbitrary",) * 3 if rides else ("parallel", "parallel", "arbitrary"))),
    )(*args, *rides)
    return out if rides else out[0]


NORM_STRIP = 8


def _rstd(x):
    return lax.rsqrt(jnp.mean(x * x, axis=-1, keepdims=True) + EPS)


def _norm_fwd(x, w, *, name):
    s = x.shape[0]
    tm = _tile(s, 512, 8)

    def body(x_ref, w_ref, o_ref):
        xv = x_ref[...]
        o_ref[...] = (xv * _rstd(xv) * w_ref[...]).astype(BF16)

    return pl.pallas_call(
        body, name=name, grid=(s // tm,),
        in_specs=[pl.BlockSpec((tm, D), lambda i: (i, 0)), pl.BlockSpec((1, D), lambda i: (0, 0))],
        out_specs=pl.BlockSpec((tm, D), lambda i: (i, 0)),
        out_shape=jax.ShapeDtypeStruct((s, D), BF16), compiler_params=_cp("parallel"),
    )(x, w)


def _norm_bwd(x, w, dy, add, *, name):
    s = x.shape[0]
    tm = _tile(s, 256, 8)

    def body(*refs):
        if add is None:
            x_ref, w_ref, dy_ref, dx_ref, dw_ref, acc = refs
        else:
            x_ref, w_ref, dy_ref, add_ref, dx_ref, dw_ref, acc = refs

        @pl.when(pl.program_id(0) == 0)
        def _():
            dw_ref[...] = jnp.zeros_like(dw_ref)

        wv = w_ref[...]
        acc[...] = jnp.zeros_like(acc)

        def strip(t, carry):
            rows = pl.ds(pl.multiple_of(t * NORM_STRIP, NORM_STRIP), NORM_STRIP)
            xv = x_ref[rows, :]
            r = _rstd(xv)
            xh = xv * r
            dyv = dy_ref[rows, :]
            dxh = dyv * wv
            dx = r * (dxh - xh * jnp.mean(dxh * xh, axis=-1, keepdims=True))
            if add is not None:
                dx = dx + add_ref[rows, :]
            dx_ref[rows, :] = dx
            acc[...] += dyv * xh
            return carry

        lax.fori_loop(0, tm // NORM_STRIP, strip, 0)
        dw_ref[...] += jnp.sum(acc[...], axis=0, keepdims=True)

    row = pl.BlockSpec((tm, D), lambda i: (i, 0))
    vec = pl.BlockSpec((1, D), lambda i: (0, 0))
    in_specs = [row, vec, row] + ([row] if add is not None else [])
    args = [x, w, dy] + ([add] if add is not None else [])
    return pl.pallas_call(
        body, name=name, grid=(s // tm,), in_specs=in_specs, out_specs=[row, vec],
        out_shape=[jax.ShapeDtypeStruct((s, D), F32), jax.ShapeDtypeStruct((1, D), F32)],
        scratch_shapes=[pltpu.VMEM((NORM_STRIP, D), F32)],
        compiler_params=_cp("arbitrary"),
    )(*args)


def _final(x3, w, target, *, name):
    s = x3.shape[0]
    tm = _tile(s, 256, 8)

    def body(x_ref, w_ref, t_ref, dx_ref, dw_ref, loss_ref, acc, lacc):
        @pl.when(pl.program_id(0) == 0)
        def _():
            dw_ref[...] = jnp.zeros_like(dw_ref)
            loss_ref[...] = jnp.zeros_like(loss_ref)

        wv = w_ref[...]
        acc[...] = jnp.zeros_like(acc)
        lacc[...] = jnp.zeros_like(lacc)

        def strip(t, carry):
            rows = pl.ds(pl.multiple_of(t * NORM_STRIP, NORM_STRIP), NORM_STRIP)
            xv = x_ref[rows, :]
            r = _rstd(xv)
            xh = xv * r
            err = xh * wv - t_ref[rows, :]
            lacc[...] += jnp.mean(err * err, axis=-1, keepdims=True)
            dy = err * (1.0 / D)
            dxh = dy * wv
            dx_ref[rows, :] = r * (dxh - xh * jnp.mean(dxh * xh, axis=-1, keepdims=True))
            acc[...] += dy * xh
            return carry

        lax.fori_loop(0, tm // NORM_STRIP, strip, 0)
        loss_ref[...] += 0.5 * jnp.sum(lacc[...])
        dw_ref[...] += jnp.sum(acc[...], axis=0, keepdims=True)

    row = pl.BlockSpec((tm, D), lambda i: (i, 0))
    vec = pl.BlockSpec((1, D), lambda i: (0, 0))
    return pl.pallas_call(
        body, name=name, grid=(s // tm,), in_specs=[row, vec, row], out_specs=[row, vec, vec],
        out_shape=[jax.ShapeDtypeStruct((s, D), F32), jax.ShapeDtypeStruct((1, D), F32),
                   jax.ShapeDtypeStruct((1, D), F32)],
        scratch_shapes=[pltpu.VMEM((NORM_STRIP, D), F32), pltpu.VMEM((NORM_STRIP, 1), F32)],
        compiler_params=_cp("arbitrary"),
    )(x3, w, target)


def _head_norm_math(o, w):
    lane = _iota((128, 128), 0) // 64
    bd = (lane == _iota((128, 128), 1) // 64).astype(F32)
    outs = []
    for op in _split(o, 128):
        ms = _dot("nn", op * op, bd, 2, 1) * (1.0 / 64)
        outs.append(op * lax.rsqrt(ms + EPS))
    return jnp.concatenate(outs, axis=1) * w


def _head_norm_fwd(o, w, *, name):
    s = o.shape[0]
    tm = _tile(s, 256, 8)

    def body(o_ref, w_ref, y_ref):
        y_ref[...] = _head_norm_math(o_ref[...], w_ref[...]).astype(BF16)

    row = pl.BlockSpec((tm, D), lambda i: (i, 0))
    vec = pl.BlockSpec((1, D), lambda i: (0, 0))
    return pl.pallas_call(
        body, name=name, grid=(s // tm,), in_specs=[row, vec], out_specs=row,
        out_shape=jax.ShapeDtypeStruct((s, D), BF16), compiler_params=_cp("parallel"),
    )(o, w)


def _head_norm_bwd(o, w, dymix, *, name):
    s = o.shape[0]
    tm = _tile(s, 256, 8)

    def body(o_ref, w_ref, dy_ref, do_ref, dw_ref):
        @pl.when(pl.program_id(0) == 0)
        def _():
            dw_ref[...] = jnp.zeros_like(dw_ref)

        _, vjp = jax.vjp(_head_norm_math, o_ref[...], w_ref[...])
        do, dw = vjp(dy_ref[...])
        do_ref[...] = do
        dw_ref[...] += dw

    row = pl.BlockSpec((tm, D), lambda i: (i, 0))
    vec = pl.BlockSpec((1, D), lambda i: (0, 0))
    return pl.pallas_call(
        body, name=name, grid=(s // tm,),
        in_specs=[row, vec, pl.BlockSpec((tm, D), lambda i: (i, 1))], out_specs=[row, vec],
        out_shape=[jax.ShapeDtypeStruct((s, D), F32), jax.ShapeDtypeStruct((1, D), F32)],
        compiler_params=_cp("arbitrary"),
    )(o, w, dymix)


SB_BQ = 256
SB_BK = 256


def _sb_consts():
    r = _iota((SB_BK, SB_BK), 0)
    c = _iota((SB_BK, SB_BK), 1)
    u_excl = (r > c).astype(BF16)
    u_incl = (r >= c).astype(BF16)
    return u_excl, u_incl


SB_LANES = 256
SB_NCH = SB_LANES // 64


def _nt(a, b):
    return lax.dot_general(a, b, (_DIMS["nt"], ((), ())), preferred_element_type=F32)


def _tn(a, b):
    return lax.dot_general(a, b, (_DIMS["tn"], ((), ())), preferred_element_type=F32)


def _nn(a, b):
    return jnp.dot(a, b, preferred_element_type=F32)


def _sb_heads(ref):
    out = []
    for hp in range(SB_LANES // 128):
        v = ref[:, 128 * hp:128 * (hp + 1)]
        first = _iota(v.shape, 1) < 64
        out += [jnp.where(first, v, 0).astype(BF16), jnp.where(first, 0, v).astype(BF16)]
    return out


SB_STRIP = 32


def _neg_abs(x):
    bits = lax.bitcast_convert_type(x, jnp.uint32) | jnp.uint32(0x80000000)
    return lax.bitcast_convert_type(bits, F32)


def _sb_block(ref, j):
    off = pl.multiple_of(j * SB_BK, SB_BK)
    return [ref[pl.ds(off, SB_BK), 128 * hp:128 * (hp + 1)] for hp in range(SB_NCH // 2)]


SB_DEAD = 104.0


def _sb_live(nlrun):
    m = nlrun[0]
    for x in nlrun[1:]:
        m = jnp.minimum(m, x)
    return jnp.min(m) < SB_DEAD


def _sb_strips():
    return [(r, pl.ds(r, SB_STRIP)) for r in range(0, SB_BQ, SB_STRIP)]


def _sb_diag_mask(r):
    return _iota((SB_STRIP, SB_BK), 1) < _iota((SB_STRIP, SB_BK), 0) + r


def _sb_soft(z, mask):
    e = jnp.exp(_neg_abs(z))
    nl = jnp.maximum(z, 0.0) + jnp.log(1.0 + e)
    if mask is not None:
        nl = jnp.where(mask, nl, 0.0)
    return e, nl


def _sb_split_to(hl_ref, rows, x):
    hi, lo = _terms(x, 2)
    hl_ref[rows, 0:SB_BK] = hi
    hl_ref[rows, SB_BK:2 * SB_BK] = lo


def _sb_stage_soft(z_ref, nl_ref, diag):
    for r, rows in _sb_strips():
        _, nl = _sb_soft(z_ref[rows, :], _sb_diag_mask(r) if diag else None)
        nl_ref[rows, 0:SB_BK] = nl.astype(BF16)


def _sb_stage_weights(z_ref, c_ref, a_ref, nlrun, diag):
    for r, rows in _sb_strips():
        a = jnp.exp(z_ref[rows, :] - c_ref[rows, :] - nlrun[r:r + SB_STRIP, :])
        if diag:
            a = jnp.where(_sb_diag_mask(r), a, 0.0)
        a_ref[rows, :] = a.astype(BF16)


def _sb_fwd(qkv, rides, *, name):
    s = qkv.shape[0]
    nq = s // SB_BQ
    ng = D // SB_LANES
    assert SB_BQ == SB_BK
    rd = _Rides(rides, [True] * len(rides))

    def body(*refs):
        (q_ref, k_ref, v_ref), (o_ref,), (zbuf, nlbuf, cbuf, abuf), handles = rd.split(refs, 3, 1, 4)
        i = pl.program_id(1)
        step_no = pl.program_id(0) * nq + i
        rd.run(handles, step_no == 0, step_no == ng * nq - 1)

        _, u_incl = _sb_consts()
        lane_a = _iota((SB_BQ, 128), 1) < 64
        qh = [q * 0.125 for q in _sb_heads(q_ref)]

        def tile(j, accs, nlrun, diag):
            kbs = _sb_block(k_ref, j)
            for c in range(SB_NCH):
                zbuf[c] = _nt(qh[c], kbs[c // 2])
            for c in range(SB_NCH):
                _sb_stage_soft(zbuf.at[c], nlbuf.at[c], diag)
                cbuf[c] = _nn(nlbuf[c], u_incl)
            for c in range(SB_NCH):
                _sb_stage_weights(zbuf.at[c], cbuf.at[c], abuf.at[c], nlrun[c], diag)
            nlrun = tuple(nlrun[c] + cbuf[c, :, 0:1] for c in range(SB_NCH))
            vbs = _sb_block(v_ref, j)
            outs = [_nn(abuf[c], vbs[c // 2]) for c in range(SB_NCH)]
            accs = tuple(acc + jnp.where(lane_a, outs[2 * hp], outs[2 * hp + 1]) for hp, acc in enumerate(accs))
            return accs, nlrun

        accs, nlrun = tile(i, (jnp.zeros((SB_BQ, 128), F32),) * (SB_NCH // 2),
                           (jnp.zeros((SB_BQ, 1), F32),) * SB_NCH, True)

        def step(carry):
            j, _, accs, nlrun = carry
            accs, nlrun = tile(j, accs, nlrun, False)
            return j - 1, _sb_live(nlrun), accs, nlrun

        _, _, accs, _ = lax.while_loop(lambda c: (c[0] >= 0) & c[1], step, (i - 1, _sb_live(nlrun), accs, nlrun))
        o_ref[...] = jnp.concatenate(accs, axis=1)

    return pl.pallas_call(
        body, name=name, grid=(ng, nq),
        in_specs=[pl.BlockSpec((SB_BQ, SB_LANES), lambda g, i: (i, g)),
                  pl.BlockSpec((s, SB_LANES), lambda g, i: (0, ng + g)),
                  pl.BlockSpec((s, SB_LANES), lambda g, i: (0, 2 * ng + g)), *rd.in_specs],
        out_specs=[pl.BlockSpec((SB_BQ, SB_LANES), lambda g, i: (i, g)), *rd.out_specs],
        out_shape=[jax.ShapeDtypeStruct((s, D), F32), *rd.out_shape],
        scratch_shapes=[pltpu.VMEM((SB_NCH, SB_BQ, SB_BK), F32), pltpu.VMEM((SB_NCH, SB_BQ, SB_BK), BF16),
                        pltpu.VMEM((SB_NCH, SB_BQ, SB_BK), F32), pltpu.VMEM((SB_NCH, SB_BQ, SB_BK), BF16),
                        *rd.scratch],
        compiler_params=_cp("arbitrary", "arbitrary"),
    )(qkv, qkv, qkv, *rides)


def _sb_bwd(qkv, o, do, rides, *, name):
    s = qkv.shape[0]
    nq = s // SB_BQ
    ng = D // SB_LANES
    nhp = SB_NCH // 2
    rd = _Rides(rides, [False] * len(rides))

    def body(*refs):
        ins, outs, scratch, handles = rd.split(refs, 5, 3, 11)
        q_ref, k_ref, v_ref, o_ref, do_ref = ins
        dq_ref, dk_hbm, dv_hbm = outs
        dk_acc, dv_acc, dk16, dv16, sems, zbuf, gbuf, hl, cbuf, abuf, dzbuf = scratch
        g_idx = pl.program_id(0)
        i = pl.program_id(1)
        step_no = g_idx * nq + i
        rd.run(handles, step_no == 0, step_no == ng * nq - 1)

        @pl.when(i == 0)
        def _():
            dk_acc[...] = jnp.zeros_like(dk_acc)
            dv_acc[...] = jnp.zeros_like(dv_acc)

        _, u_incl = _sb_consts()
        u2 = jnp.concatenate([u_incl, u_incl], axis=0)
        lane_a = _iota((SB_BQ, 128), 1) < 64
        lane_k = _iota((SB_BK, 128), 1) < 64
        qh = [q * 0.125 for q in _sb_heads(q_ref)]
        qf = [q_ref[:, 128 * hp:128 * (hp + 1)] for hp in range(nhp)]
        doh = _sb_heads(do_ref)
        dof = [do_ref[:, 128 * hp:128 * (hp + 1)].astype(BF16) for hp in range(nhp)]
        delta = []
        for hp in range(nhp):
            prod = dof[hp].astype(F32) * o_ref[:, 128 * hp:128 * (hp + 1)]
            delta += [jnp.sum(jnp.where(lane_a, prod, 0.0), axis=1, keepdims=True),
                      jnp.sum(jnp.where(lane_a, 0.0, prod), axis=1, keepdims=True)]

        def pre(slot, j):
            kbs = _sb_block(k_ref, j)
            vbs = _sb_block(v_ref, j)
            for c in range(SB_NCH):
                zbuf[slot, c] = _nt(qh[c], kbs[c // 2])
                gbuf[slot, c] = _nt(doh[c], vbs[c // 2])

        def stage_g(c, slot):
            for _, rows in _sb_strips():
                g = abuf[slot, c, rows, :].astype(F32) * gbuf[slot, c, rows, :]
                gbuf[slot, c, rows, :] = g
                _sb_split_to(hl.at[c], rows, g)

        def stage_dz(c, slot, grun, diag):
            for r, rows in _sb_strips():
                z = zbuf[slot, c, rows, :]
                g = gbuf[slot, c, rows, :]
                cs = (delta[c] - grun)[r:r + SB_STRIP, :] - cbuf[c, rows, :]
                sig = 1.0 / (1.0 + jnp.exp(-z))
                dz = g - (g + cs) * sig
                if diag:
                    dz = jnp.where(_sb_diag_mask(r), dz, 0.0)
                dzbuf[slot, c, rows, :] = dz.astype(BF16)

        def chain(slot, nlrun, grun, diag):
            for c in range(SB_NCH):
                _sb_stage_soft(zbuf.at[slot, c], hl.at[c], diag)
                cbuf[c] = _nn(hl[c, :, 0:SB_BK], u_incl)
            nl_tot = []
            for c in range(SB_NCH):
                _sb_stage_weights(zbuf.at[slot, c], cbuf.at[c], abuf.at[slot, c], nlrun[c], diag)
                nl_tot.append(cbuf[c, :, 0:1])
                stage_g(c, slot)
                cbuf[c] = _nn(hl[c], u2)
            g_tot = []
            for c in range(SB_NCH):
                stage_dz(c, slot, grun[c], diag)
                g_tot.append(cbuf[c, :, 0:1])
            return (tuple(a + b for a, b in zip(nlrun, nl_tot)), tuple(a + b for a, b in zip(grun, g_tot)))

        def post(slot, j, dqs):
            off = pl.multiple_of(j * SB_BK, SB_BK)
            kbs = _sb_block(k_ref, j)
            dq_t = [_nn(dzbuf[slot, c], kbs[c // 2]) for c in range(SB_NCH)]
            dk_t = [_tn(dzbuf[slot, c], qf[c // 2]) for c in range(SB_NCH)]
            dv_t = [_tn(abuf[slot, c], dof[c // 2]) for c in range(SB_NCH)]
            for hp in range(nhp):
                cols = slice(128 * hp, 128 * (hp + 1))
                dk_acc[pl.ds(off, SB_BK), cols] += 0.125 * jnp.where(lane_k, dk_t[2 * hp], dk_t[2 * hp + 1])
                dv_acc[pl.ds(off, SB_BK), cols] += jnp.where(lane_k, dv_t[2 * hp], dv_t[2 * hp + 1])
            return tuple(dq + jnp.where(lane_a, dq_t[2 * hp], dq_t[2 * hp + 1]) for hp, dq in enumerate(dqs))

        def tile(j, dqs, nlrun, grun, diag):
            pre(0, j)
            nlrun, grun = chain(0, nlrun, grun, diag)
            return post(0, j, dqs), nlrun, grun

        zero = (jnp.zeros((SB_BQ, 1), F32),) * SB_NCH
        dqs, nlrun, grun = tile(i, (jnp.zeros((SB_BQ, 128), F32),) * nhp, zero, zero, True)

        def step(carry):
            j, _, dqs, nlrun, grun = carry
            dqs, nlrun, grun = tile(j, dqs, nlrun, grun, False)
            return j - 1, _sb_live(nlrun), dqs, nlrun, grun

        carry = lax.while_loop(lambda c: (c[0] >= 0) & c[1], step, (i - 1, _sb_live(nlrun), dqs, nlrun, grun))
        dq_ref[...] = (0.125 * jnp.concatenate(carry[2], axis=1)).astype(BF16)

        def out_copies(g):
            cols = pl.ds(pl.multiple_of(g * SB_LANES, SB_LANES), SB_LANES)
            return (pltpu.make_async_copy(dk16, dk_hbm.at[:, cols], sems.at[0]),
                    pltpu.make_async_copy(dv16, dv_hbm.at[:, cols], sems.at[1]))

        @pl.when((i == nq - 1) & (g_idx > 0))
        def _():
            for cp in out_copies(g_idx - 1):
                cp.wait()

        @pl.when(i == nq - 1)
        def _():
            def narrow(r, carry):
                rows = pl.ds(pl.multiple_of(r * SB_BK, SB_BK), SB_BK)
                dk16[rows, :] = dk_acc[rows, :].astype(BF16)
                dv16[rows, :] = dv_acc[rows, :].astype(BF16)
                return carry

            lax.fori_loop(0, s // SB_BK, narrow, 0)
            for cp in out_copies(g_idx):
                cp.start()

        @pl.when((i == nq - 1) & (g_idx == ng - 1))
        def _():
            for cp in out_copies(g_idx):
                cp.wait()

    qblk = pl.BlockSpec((SB_BQ, SB_LANES), lambda g, i: (i, g))
    hbm = pl.BlockSpec(memory_space=pl.ANY)
    return pl.pallas_call(
        body, name=name, grid=(ng, nq),
        in_specs=[qblk, pl.BlockSpec((s, SB_LANES), lambda g, i: (0, ng + g)),
                  pl.BlockSpec((s, SB_LANES), lambda g, i: (0, 2 * ng + g)), qblk, qblk, *rd.in_specs],
        out_specs=[qblk, hbm, hbm, *rd.out_specs],
        out_shape=[jax.ShapeDtypeStruct((s, D), BF16)] * 3 + rd.out_shape,
        scratch_shapes=[pltpu.VMEM((s, SB_LANES), F32), pltpu.VMEM((s, SB_LANES), F32),
                        pltpu.VMEM((s, SB_LANES), BF16), pltpu.VMEM((s, SB_LANES), BF16),
                        pltpu.SemaphoreType.DMA((2,)),
                        pltpu.VMEM((1, SB_NCH, SB_BQ, SB_BK), F32), pltpu.VMEM((1, SB_NCH, SB_BQ, SB_BK), F32),
                        pltpu.VMEM((SB_NCH, SB_BQ, 2 * SB_BK), BF16), pltpu.VMEM((SB_NCH, SB_BQ, SB_BK), F32),
                        pltpu.VMEM((1, SB_NCH, SB_BQ, SB_BK), BF16), pltpu.VMEM((1, SB_NCH, SB_BQ, SB_BK), BF16),
                        *rd.scratch],
        compiler_params=_cp("arbitrary", "arbitrary"),
    )(qkv, qkv, qkv, o, do, *rides)


def _ssd_core(z, xpre, dtr, state, dtb, alog, dsk, nw):
    L = SSD_CHUNK
    xa = _silu(xpre)
    pieces = _split(xa, 128)
    xs = jnp.concatenate(pieces[:8], axis=1)
    bm, cm = pieces[8:10], pieces[10:12]
    dt = _softplus(dtr + dtb)
    a = dt * (-jnp.exp(alog))
    tri = (_iota((L, L), 0) >= _iota((L, L), 1)).astype(F32)
    a_cs = _dot("nn", tri, a, 1, 3)
    xc = xs * dt
    tril = _iota((L, L), 0) >= _iota((L, L), 1)
    lane_a = _iota((L, 128), 1) < 64
    acs_p = _split(a_cs, 128)
    xc_p = _split(xc, 128)
    ys, new_states = [], []
    for g in range(2):
        cb = _dot("nt", cm[g], bm[g])
        for pp in range(4):
            pair = 4 * g + pp
            acs = acs_p[pair]
            acs_t = acs.T
            xcp = xc_p[pair]
            st = state[pair]
            heads = []
            for hh in range(2):
                col = _take_fn(1, 64 * hh)(acs)
                row = _take_fn(0, 64 * hh)(acs_t)
                seg = col - row
                lm = jnp.where(tril, jnp.exp(jnp.where(tril, seg, 0.0)), 0.0)
                heads.append(_dot("nn", cb * lm, xcp))
            ydiag = jnp.where(lane_a, heads[0], heads[1])
            last = _take_fn(0, L - 1)(acs)
            snew = _dot("tn", xcp * jnp.exp(last - acs), bm[g])
            new_states.append(st * jnp.exp(_take_fn(1, L - 1)(acs_t)) + snew)
            yoff = _dot("nt", cm[g], st) * jnp.exp(acs)
            ys.append(ydiag + yoff)
    y = jnp.concatenate(ys, axis=1) + xs * dsk
    yg = y * _silu(z)
    outs = []
    for v in _split(yg, 512):
        outs.append(v * lax.rsqrt(jnp.mean(v * v, axis=-1, keepdims=True) + EPS))
    return jnp.concatenate(outs, axis=1) * nw, tuple(new_states)


XBC = 1536


def _ssd_conv(ext_ref, cw, cb):
    acc = cb
    for k in range(4):
        acc = acc + cw[k:k + 1, :] * ext_ref[pl.ds(HALO - 3 + k, SSD_CHUNK), :]
    return acc


def _ssd_fwd(z, xbc, dtr, cw, cb, lanes, rides, *, name):
    s = z.shape[0]
    L = SSD_CHUNK
    nc = s // L
    rd = _Rides(rides, [True] * len(rides))

    def body(*refs):
        ins, (y_ref, st_ref), (state, ext), handles = rd.split(refs, 7, 2, 2)
        z_ref, x_ref, h_ref, dtr_ref, cw_ref, cb_ref, ln_ref = ins
        c = pl.program_id(0)
        rd.run(handles, c == 0, c == nc - 1)

        @pl.when(c == 0)
        def _():
            state[...] = jnp.zeros_like(state)

        ext[0:HALO, :] = jnp.where(c == 0, 0.0, h_ref[...])
        ext[HALO:, :] = x_ref[...]
        xpre = _ssd_conv(ext, cw_ref[...], cb_ref[...])
        st_ref[0] = state[...]
        st_in = tuple(state[p] for p in range(8))
        yn, st_out = _ssd_core(z_ref[...], xpre, dtr_ref[...], st_in,
                               ln_ref[0:1, :], ln_ref[1:2, :], ln_ref[2:3, :], ln_ref[3:4, :])
        y_ref[...] = yn.astype(BF16)
        for p in range(8):
            state[p] = st_out[p]

    return pl.pallas_call(
        body, name=name, grid=(nc,),
        in_specs=[pl.BlockSpec((L, D), lambda c: (c, 0)),
                  pl.BlockSpec((L, XBC), lambda c: (c, 0)),
                  pl.BlockSpec((HALO, XBC), lambda c: (jnp.maximum(c * (L // HALO) - 1, 0), 0)),
                  pl.BlockSpec((L, D), lambda c: (c, 0)),
                  pl.BlockSpec((4, XBC), lambda c: (0, 0)),
                  pl.BlockSpec((1, XBC), lambda c: (0, 0)),
                  pl.BlockSpec((8, D), lambda c: (0, 0)), *rd.in_specs],
        out_specs=[pl.BlockSpec((L, D), lambda c: (c, 0)),
                   pl.BlockSpec((1, 8, 128, 128), lambda c: (c, 0, 0, 0)), *rd.out_specs],
        out_shape=[jax.ShapeDtypeStruct((s, D), BF16), jax.ShapeDtypeStruct((nc, 8, 128, 128), F32),
                   *rd.out_shape],
        scratch_shapes=[pltpu.VMEM((8, 128, 128), F32), pltpu.VMEM((L + HALO, XBC), F32), *rd.scratch],
        compiler_params=_cp("arbitrary"),
    )(z, xbc, xbc, dtr, cw, cb, lanes, *rides)


def _ssd_bwd(z, xbc, dtr, states, dymix, cw, cb, lanes, *, name):
    s = z.shape[0]
    L = SSD_CHUNK
    nc = s // L

    def body(z_ref, x_ref, h_ref, dtr_ref, st_ref, dy_ref, cw_ref, cb_ref, ln_ref,
             dz_ref, dx_ref, ddt_ref, dln_ref, dcv_ref, dstate, ext, dext):
        i = pl.program_id(0)
        c = nc - 1 - i

        @pl.when(i == 0)
        def _():
            dstate[...] = jnp.zeros_like(dstate)
            dext[...] = jnp.zeros_like(dext)
            dln_ref[...] = jnp.zeros_like(dln_ref)
            dcv_ref[...] = jnp.zeros_like(dcv_ref)

        ext[0:HALO, :] = jnp.where(c == 0, 0.0, h_ref[...])
        ext[HALO:, :] = x_ref[...]
        cwv = cw_ref[...]
        xpre = _ssd_conv(ext, cwv, cb_ref[...])
        st_in = tuple(st_ref[0, p] for p in range(8))
        _, vjp = jax.vjp(_ssd_core, z_ref[...], xpre, dtr_ref[...], st_in,
                         ln_ref[0:1, :], ln_ref[1:2, :], ln_ref[2:3, :], ln_ref[3:4, :])
        dz, dxpre, ddtr, dst, d0, d1, d2, d3 = vjp((dy_ref[...], tuple(dstate[p] for p in range(8))))
        for p in range(8):
            dstate[p] = dst[p]
        dz_ref[...] = dz.astype(BF16)
        ddt_ref[...] = ddtr.astype(BF16)
        dln_ref[0:4, :] += jnp.concatenate([d0, d1, d2, d3], axis=0)
        dext[0:L, :] = dxpre
        xcur = x_ref[...]
        dx = jnp.zeros((L, XBC), F32)
        rows = []
        for k in range(4):
            shifted = dext[pl.ds(3 - k, L), :]
            dx = dx + cwv[k:k + 1, :] * shifted
            rows.append(jnp.sum(shifted * xcur, axis=0, keepdims=True))
        rows.append(jnp.sum(dxpre, axis=0, keepdims=True))
        dx_ref[...] = dx.astype(BF16)
        dcv_ref[0:5, :] += jnp.concatenate(rows, axis=0)
        dext[L:L + HALO, :] = dxpre[0:HALO, :]

    rev = lambda i: (nc - 1 - i, 0)
    return pl.pallas_call(
        body, name=name, grid=(nc,),
        in_specs=[pl.BlockSpec((L, D), rev),
                  pl.BlockSpec((L, XBC), rev),
                  pl.BlockSpec((HALO, XBC), lambda i: (jnp.maximum((nc - 1 - i) * (L // HALO) - 1, 0), 0)),
                  pl.BlockSpec((L, D), rev),
                  pl.BlockSpec((1, 8, 128, 128), lambda i: (nc - 1 - i, 0, 0, 0)),
                  pl.BlockSpec((L, D), rev),
                  pl.BlockSpec((4, XBC), lambda i: (0, 0)),
                  pl.BlockSpec((1, XBC), lambda i: (0, 0)),
                  pl.BlockSpec((8, D), lambda i: (0, 0))],
        out_specs=[pl.BlockSpec((L, D), rev), pl.BlockSpec((L, XBC), rev), pl.BlockSpec((L, D), rev),
                   pl.BlockSpec((8, D), lambda i: (0, 0)), pl.BlockSpec((8, XBC), lambda i: (0, 0))],
        out_shape=[jax.ShapeDtypeStruct((s, D), BF16), jax.ShapeDtypeStruct((s, XBC), BF16),
                   jax.ShapeDtypeStruct((s, D), BF16), jax.ShapeDtypeStruct((8, D), F32),
                   jax.ShapeDtypeStruct((8, XBC), F32)],
        scratch_shapes=[pltpu.VMEM((8, 128, 128), F32), pltpu.VMEM((L + HALO, XBC), F32),
                        pltpu.VMEM((L + HALO, XBC), F32)],
        compiler_params=_cp("arbitrary"),
    )(z, xbc, xbc, dtr, states, dymix, cw, cb, lanes)


def _mem_attn_math(q, k, v):
    outs = []
    for qh, kh, vh in zip(_split(q, 256), _split(k, 256), _split(v, 256)):
        sc = _dot("nt", qh, kh) * (1.0 / 16.0)
        e = jnp.exp(sc - lax.stop_gradient(jnp.max(sc, axis=-1, keepdims=True)))
        p = e / jnp.sum(e, axis=-1, keepdims=True)
        outs.append(_dot("nn", p, vh))
    return jnp.concatenate(outs, axis=1)


def _mem_attn_fwd(q, k, v, *, name):
    s, m = q.shape[0], k.shape[0]
    tm = _tile(s, 256, 8)

    def body(q_ref, k_ref, v_ref, o_ref):
        o_ref[...] = _mem_attn_math(q_ref[...].astype(F32), k_ref[...].astype(F32),
                                    v_ref[...].astype(F32)).astype(BF16)

    row = pl.BlockSpec((tm, D), lambda i: (i, 0))
    kv = pl.BlockSpec((m, D), lambda i: (0, 0))
    return pl.pallas_call(
        body, name=name, grid=(s // tm,), in_specs=[row, kv, kv], out_specs=row,
        out_shape=jax.ShapeDtypeStruct((s, D), BF16), compiler_params=_cp("parallel"),
    )(q, k, v)


def _mem_attn_bwd(q, k, v, do, *, name):
    s, m = q.shape[0], k.shape[0]
    tm = _tile(s, 256, 8)

    def body(q_ref, k_ref, v_ref, do_ref, dq_ref, dk_ref, dv_ref):
        @pl.when(pl.program_id(0) == 0)
        def _():
            dk_ref[...] = jnp.zeros_like(dk_ref)
            dv_ref[...] = jnp.zeros_like(dv_ref)

        _, vjp = jax.vjp(_mem_attn_math, q_ref[...].astype(F32), k_ref[...].astype(F32),
                         v_ref[...].astype(F32))
        dq, dk, dv = vjp(do_ref[...])
        dq_ref[...] = dq.astype(BF16)
        dk_ref[...] += dk
        dv_ref[...] += dv

    row = pl.BlockSpec((tm, D), lambda i: (i, 0))
    kv = pl.BlockSpec((m, D), lambda i: (0, 0))
    return pl.pallas_call(
        body, name=name, grid=(s // tm,), in_specs=[row, kv, kv, row], out_specs=[row, kv, kv],
        out_shape=[jax.ShapeDtypeStruct((s, D), BF16), jax.ShapeDtypeStruct((m, D), F32),
                   jax.ShapeDtypeStruct((m, D), F32)],
        compiler_params=_cp("arbitrary"),
    )(q, k, v, do)


DFF = 2816
FFN_TC = 1408
FFN_TM = 256


FFN_CHUNKS = tuple((c, min(512, FFN_TC - c)) for c in range(0, FFN_TC, 512))


def _rows8(ref, r, cols):
    return ref[pl.ds(pl.multiple_of(r, HALO), HALO), cols]


def _shift_down(prev, cur, s):
    return jnp.where(_iota(cur.shape, 0) < s, pltpu.roll(prev, s, 0), pltpu.roll(cur, s, 0))


def _shift_up(cur, nxt, s):
    return jnp.where(_iota(cur.shape, 0) >= HALO - s, pltpu.roll(nxt, HALO - s, 0), pltpu.roll(cur, HALO - s, 0))


def _ffn_conv_strip(ext_ref, r, cols, cw, cb):
    prev, cur = _rows8(ext_ref, r, cols), _rows8(ext_ref, r + HALO, cols)
    return cb + cw[0:1, :] * _shift_down(prev, cur, 2) + cw[1:2, :] * _shift_down(prev, cur, 1) + cw[2:3, :] * cur


def _ffn_specs(s):
    tm, tc = FFN_TM, FFN_TC
    blk = pl.BlockSpec((tm, tc), lambda i, j: (i, j))
    halo = pl.BlockSpec((HALO, tc), lambda i, j: (jnp.maximum(i * (tm // HALO) - 1, 0), j))
    cw = pl.BlockSpec((3, tc), lambda i, j: (0, j))
    cb = pl.BlockSpec((1, tc), lambda i, j: (0, j))
    return tm, tc, blk, halo, cw, cb


def _glu_fwd(ug, uv, cwg, cwv, cbg, cbv, *, name):
    s = ug.shape[0]
    tm, tc, blk, halo, cw, cb = _ffn_specs(s)

    def body(g_ref, gh_ref, v_ref, vh_ref, cwg_ref, cwv_ref, cbg_ref, cbv_ref, f_ref, eg, ev):
        first = pl.program_id(0) == 0
        eg[0:HALO, :] = jnp.where(first, 0.0, gh_ref[...])
        eg[HALO:, :] = g_ref[...]
        ev[0:HALO, :] = jnp.where(first, 0.0, vh_ref[...])
        ev[HALO:, :] = v_ref[...]
        cwgv, cwvv, cbgv, cbvv = cwg_ref[...], cwv_ref[...], cbg_ref[...], cbv_ref[...]

        def step(t, carry):
            for c0, w in FFN_CHUNKS:
                cols = slice(c0, c0 + w)
                outs = []
                for h in range(2):
                    r = t * 16 + HALO * h
                    g = _ffn_conv_strip(eg, r, cols, cwgv[:, cols], cbgv[:, cols])
                    v = _ffn_conv_strip(ev, r, cols, cwvv[:, cols], cbvv[:, cols])
                    outs.append(_silu(g) * v)
                f_ref[pl.ds(pl.multiple_of(t * 16, 16), 16), cols] = jnp.concatenate(outs, axis=0).astype(BF16)
            return carry

        lax.fori_loop(0, tm // 16, step, 0)

    return pl.pallas_call(
        body, name=name, grid=(s // tm, DFF // tc),
        in_specs=[blk, halo, blk, halo, cw, cw, cb, cb], out_specs=blk,
        out_shape=jax.ShapeDtypeStruct((s, DFF), BF16),
        scratch_shapes=[pltpu.VMEM((tm + HALO, tc), F32)] * 2,
        compiler_params=_cp("parallel", "parallel"),
    )(ug, ug, uv, uv, cwg, cwv, cbg, cbv)


def _ffn_bwd(ug, uv, df, cwg, cwv, cbg, cbv, *, name):
    s = ug.shape[0]
    tm, tc = FFN_TM, FFN_TC
    nb = s // tm
    rows_ext = tm + HALO

    def body(g_ref, gp_ref, gn_ref, v_ref, vp_ref, vn_ref, df_ref, dfn_ref, cwg_ref, cwv_ref, cbg_ref, cbv_ref,
             dxg_ref, dxv_ref, dcg_ref, dcv_ref, eg, ev, edf, edg, edv, accg, accv):
        i = pl.program_id(1)
        first, last = i == 0, i == nb - 1

        @pl.when(first)
        def _():
            dcg_ref[...] = jnp.zeros_like(dcg_ref)
            dcv_ref[...] = jnp.zeros_like(dcv_ref)

        for e, prev, main, nxt in ((eg, gp_ref, g_ref, gn_ref), (ev, vp_ref, v_ref, vn_ref)):
            e[0:HALO, :] = jnp.where(first, 0.0, prev[...])
            e[HALO:HALO + tm, :] = main[...]
            e[HALO + tm:, :] = jnp.where(last, 0.0, nxt[...])
        edf[0:tm, :] = df_ref[...]
        edf[tm:, :] = jnp.where(last, 0.0, dfn_ref[...])
        accg[...] = jnp.zeros_like(accg)
        accv[...] = jnp.zeros_like(accv)
        cwgv, cwvv, cbgv, cbvv = cwg_ref[...], cwv_ref[...], cbg_ref[...], cbv_ref[...]

        def cotangents(t, carry):
            r = t * HALO
            for c0, w in FFN_CHUNKS:
                cols = slice(c0, c0 + w)
                g = _ffn_conv_strip(eg, r, cols, cwgv[:, cols], cbgv[:, cols])
                v = _ffn_conv_strip(ev, r, cols, cwvv[:, cols], cbvv[:, cols])
                dfs = _rows8(edf, r, cols)
                sg = _sigmoid(g)
                edv[pl.ds(pl.multiple_of(r, HALO), HALO), cols] = dfs * g * sg
                edg[pl.ds(pl.multiple_of(r, HALO), HALO), cols] = dfs * v * sg * (1.0 + g * (1.0 - sg))
            return carry

        lax.fori_loop(0, rows_ext // HALO, cotangents, 0)

        def conv_backward(t, carry):
            for c0, w in FFN_CHUNKS:
                cols = slice(c0, c0 + w)
                for edu, e, cw, dx_ref, acc in ((edg, eg, cwgv[:, cols], dxg_ref, accg),
                                                (edv, ev, cwvv[:, cols], dxv_ref, accv)):
                    dxs = []
                    for h in range(2):
                        r = t * 16 + HALO * h
                        cur, nxt = _rows8(edu, r, cols), _rows8(edu, r + HALO, cols)
                        up1, up2 = _shift_up(cur, nxt, 1), _shift_up(cur, nxt, 2)
                        x = _rows8(e, r + HALO, cols)
                        dxs.append(cw[2:3, :] * cur + cw[1:2, :] * up1 + cw[0:1, :] * up2)
                        acc[0, :, cols] += up2 * x
                        acc[1, :, cols] += up1 * x
                        acc[2, :, cols] += cur * x
                        acc[3, :, cols] += cur
                    dx_ref[pl.ds(pl.multiple_of(t * 16, 16), 16), cols] = jnp.concatenate(dxs, axis=0).astype(BF16)
            return carry

        lax.fori_loop(0, tm // 16, conv_backward, 0)
        for acc, dc_ref in ((accg, dcg_ref), (accv, dcv_ref)):
            dc_ref[0:4, :] += jnp.concatenate([jnp.sum(acc[k], axis=0, keepdims=True) for k in range(4)], axis=0)

    blk = pl.BlockSpec((tm, tc), lambda j, i: (i, j))
    nxt = pl.BlockSpec((HALO, tc), lambda j, i: (jnp.minimum((i + 1) * (tm // HALO), s // HALO - 1), j))
    prv = pl.BlockSpec((HALO, tc), lambda j, i: (jnp.maximum(i * (tm // HALO) - 1, 0), j))
    cw = pl.BlockSpec((3, tc), lambda j, i: (0, j))
    cb = pl.BlockSpec((1, tc), lambda j, i: (0, j))
    acc = pl.BlockSpec((8, tc), lambda j, i: (0, j))
    return pl.pallas_call(
        body, name=name, grid=(DFF // tc, nb),
        in_specs=[blk, prv, nxt, blk, prv, nxt, blk, nxt, cw, cw, cb, cb],
        out_specs=[blk, blk, acc, acc],
        out_shape=[jax.ShapeDtypeStruct((s, DFF), BF16)] * 2 + [jax.ShapeDtypeStruct((8, DFF), F32)] * 2,
        scratch_shapes=[pltpu.VMEM((tm + 2 * HALO, tc), F32)] * 2 + [pltpu.VMEM((rows_ext, tc), F32)] * 3
                       + [pltpu.VMEM((4, HALO, tc), F32)] * 2,
        compiler_params=_cp("parallel", "arbitrary"),
    )(ug, ug, ug, uv, uv, uv, df, df, cwg, cwv, cbg, cbv)


MESH = pl.DeviceIdType.MESH


def _all_gather(arrs, *, name):
    n = len(arrs)

    def body(*refs):
        x_refs, out_refs = refs[:n], refs[n:2 * n]
        send_sems, recv_sems, local_sems = refs[2 * n:]
        x, y, c = lax.axis_index("x"), lax.axis_index("y"), lax.axis_index("c")
        me, sibling = (x, y, c), (x, y, 1 - c)
        chips = [(1 - x, y), (x, 1 - y), (1 - x, 1 - y)]

        def blk(a, dev):
            return out_refs[a].at[4 * dev[0] + 2 * dev[1] + dev[2]]

        def copy(a, k, block, to, src=None):
            return pltpu.make_async_remote_copy(
                src_ref=blk(a, block) if src is None else src, dst_ref=blk(a, block),
                send_sem=send_sems.at[7 * a + k], recv_sem=recv_sems.at[7 * a + k],
                device_id=to, device_id_type=MESH)

        started = []
        mine = []
        for a in range(n):
            cp = pltpu.make_async_copy(x_refs[a], blk(a, me), local_sems.at[a])
            cp.start()
            mine.append(cp)
            first = [copy(a, 0, me, sibling, src=x_refs[a])]
            first += [copy(a, 1 + j, me, (*chip, c), src=x_refs[a]) for j, chip in enumerate(chips)]
            for cp in first:
                cp.start()
            started += first
        for a in range(n):
            for j, chip in enumerate(chips):
                copy(a, 1 + j, (*chip, c), me).wait_recv()
                fwd = copy(a, 4 + j, (*chip, c), sibling)
                fwd.start()
                started.append(fwd)
        for a in range(n):
            copy(a, 0, sibling, me).wait_recv()
            for j, chip in enumerate(chips):
                copy(a, 4 + j, (*chip, 1 - c), me).wait_recv()
        for cp in started:
            cp.wait_send()
        for cp in mine:
            cp.wait()

    any_spec = pl.BlockSpec(memory_space=pl.ANY)
    return pl.pallas_call(
        body, name=name,
        in_specs=[any_spec] * n, out_specs=[any_spec] * n,
        out_shape=[jax.ShapeDtypeStruct((NDEV,) + a.shape, a.dtype) for a in arrs],
        scratch_shapes=[pltpu.SemaphoreType.DMA((7 * n,)), pltpu.SemaphoreType.DMA((7 * n,)),
                        pltpu.SemaphoreType.DMA((n,))],
    )(*arrs)


class _Direct:
    SEMS = (pltpu.SemaphoreType.DMA((7,)), pltpu.SemaphoreType.DMA((7,)), pltpu.SemaphoreType.DMA((1,)))

    def __init__(self, src_ref, recv_ref, sems, gather):
        x, y, c = lax.axis_index("x"), lax.axis_index("y"), lax.axis_index("c")
        me = 4 * x + 2 * y + c
        send_sems, recv_sems, local_sem = sems
        src = (lambda pid: src_ref) if gather else (lambda pid: src_ref.at[pid])
        self.mine = pltpu.make_async_copy(src(me), recv_ref.at[me], local_sem.at[0])
        self.copies = []
        for k in range(1, NDEV):
            px = 1 - x if k & 4 else x
            py = 1 - y if k & 2 else y
            pc = 1 - c if k & 1 else c
            self.copies.append(pltpu.make_async_remote_copy(
                src_ref=src(4 * px + 2 * py + pc), dst_ref=recv_ref.at[me],
                send_sem=send_sems.at[k - 1], recv_sem=recv_sems.at[k - 1],
                device_id=(px, py, pc), device_id_type=MESH))

    def start(self):
        self.mine.start()
        for cp in self.copies:
            cp.start()

    def wait(self):
        for cp in self.copies:
            cp.wait_recv()
        for cp in self.copies:
            cp.wait_send()
        self.mine.wait()


def _recv_shape(src, gather):
    return jax.ShapeDtypeStruct(((NDEV,) + src.shape) if gather else src.shape, src.dtype)


class _Rides:
    def __init__(self, rides, gathers):
        self.n = len(rides)
        self.gathers = list(gathers)
        any_spec = pl.BlockSpec(memory_space=pl.ANY)
        self.in_specs = [any_spec] * self.n
        self.out_specs = [any_spec] * self.n
        self.out_shape = [_recv_shape(a, g) for a, g in zip(rides, gathers)]
        self.scratch = list(_Direct.SEMS) * self.n

    def split(self, refs, n_in, n_out, n_scratch):
        n = self.n
        ins, refs = refs[:n_in], refs[n_in:]
        rides, refs = refs[:n], refs[n:]
        outs, refs = refs[:n_out], refs[n_out:]
        gots, refs = refs[:n], refs[n:]
        scratch, sems = refs[:n_scratch], refs[n_scratch:]
        return ins, outs, scratch, (rides, gots, sems)

    def run(self, handles, first, last):
        rides, gots, sems = handles

        def all_of():
            return [_Direct(rides[a], gots[a], sems[3 * a:3 * a + 3], self.gathers[a]) for a in range(self.n)]

        @pl.when(first)
        def _():
            for e in all_of():
                e.start()

        @pl.when(last)
        def _():
            for e in all_of():
                e.wait()


def _exchange(arrs, gathers, *, name):
    rd = _Rides(arrs, gathers)

    def body(*refs):
        _, _, _, handles = rd.split(refs, 0, 0, 0)
        rd.run(handles, True, True)

    return pl.pallas_call(
        body, name=name, in_specs=rd.in_specs, out_specs=rd.out_specs, out_shape=rd.out_shape,
        scratch_shapes=rd.scratch,
    )(*arrs)


def _adamw(parts, w, m, v, *, name):
    r, cols = w.shape
    tm = _tile(r, 256, PACK_ALIGN)
    c1 = 1.0 - ADAM_B1 ** ADAM_STEP
    c2 = 1.0 - ADAM_B2 ** ADAM_STEP

    def body(p_ref, w_ref, m_ref, v_ref, g_ref, d_ref, nm_ref, nv_ref):
        g = p_ref[0].astype(F32)
        for i in range(1, NDEV):
            g = g + p_ref[i].astype(F32)
        nm = ADAM_B1 * m_ref[...] + (1.0 - ADAM_B1) * g
        nv = ADAM_B2 * v_ref[...] + (1.0 - ADAM_B2) * (g * g)
        d_ref[...] = -ADAM_LR * ((nm / c1) / (jnp.sqrt(nv / c2) + ADAM_EPS) + ADAM_WD * w_ref[...])
        g_ref[...] = g
        nm_ref[...] = nm
        nv_ref[...] = nv

    row = pl.BlockSpec((tm, cols), lambda i: (i, 0))
    return pl.pallas_call(
        body, name=name, grid=(r // tm,),
        in_specs=[pl.BlockSpec((NDEV, tm, cols), lambda i: (0, i, 0)), row, row, row],
        out_specs=[row] * 4, out_shape=[jax.ShapeDtypeStruct((r, cols), F32)] * 4,
        compiler_params=_cp("parallel"),
    )(parts, w, m, v)


PACK_ALIGN = 16


def _part_rows(shape):
    n = -(-math.prod(shape) // D)
    return n + (-n) % PACK_ALIGN


def _rows(a):
    flat = a.reshape(-1)
    pad = _part_rows(a.shape) * D - flat.shape[0]
    if pad:
        flat = jnp.concatenate([flat, jnp.zeros((pad,), flat.dtype)])
    return flat.reshape(-1, D)


def _pack(parts, total_rows):
    if all(math.prod(p.shape) % (PACK_ALIGN * D) for p in parts):
        return _pack_small(parts, total_rows)
    rows = [_rows(p) for p in parts]
    used = sum(r.shape[0] for r in rows)
    if total_rows > used:
        rows.append(jnp.zeros((total_rows - used, D), rows[0].dtype))
    return jnp.concatenate(rows, axis=0)


def _pack_small(parts, total_rows):
    flat, used = [], 0
    for p in parts:
        n, nr = math.prod(p.shape), _part_rows(p.shape)
        flat += [p.reshape(-1), jnp.zeros((nr * D - n,), p.dtype)]
        used += nr
    flat.append(jnp.zeros(((total_rows - used) * D,), parts[0].dtype))
    return jnp.concatenate(flat).reshape(total_rows, D)


def _unpack(buf, shapes, part_rows=_part_rows):
    out, r0 = [], 0
    for shp in shapes:
        n = math.prod(shp)
        out.append(buf[r0:r0 + part_rows(shp)].reshape(-1)[:n].reshape(shp))
        r0 += part_rows(shp)
    return out


def _tight_rows(shape):
    return -(-math.prod(shape) // D)


def _pack_tight(parts, total_rows):
    flat, used = [], 0
    for p in parts:
        n, nr = math.prod(p.shape), _tight_rows(p.shape)
        flat += [p.reshape(-1), jnp.zeros((nr * D - n,), p.dtype)]
        used += nr
    flat.append(jnp.zeros(((total_rows - used) * D,), parts[0].dtype))
    return jnp.concatenate(flat).reshape(total_rows, D)


SHARD = {"w_in": (D, 706), "w_out": (256, D), "w_mq": (128, D), "w_mk": (128, D), "w_mv": (128, D),
         "w_mo": (128, D), "w_up": (D, 704), "w_down": (352, D), "conv_ssd_w": (4, 192), "conv_ffn_w": (3, 704)}
GATHER_MID = ["w_out", "w_mq", "w_mk", "w_mv", "w_mo"]
GATHER_FFN = ["w_down"]
CONV_TAPS = ["conv_ssd_w", "conv_ffn_w"]
GRADS_PACKED = ["w_out", "w_mq", "w_mk", "w_mv", "w_mo", "w_down", "conv_ffn_w", "conv_ssd_w"]


def _layout(names):
    row0, r = {}, 0
    for n in names:
        row0[n] = r
        r += _part_rows(SHARD[n])
    return row0, r + (-r) % 128


SMALL = [("norm_mix_w", (1, D)), ("conv_ssd_b", (1, 1536)), ("dt_bias", (1, 16)), ("a_log", (1, 16)),
         ("d_skip", (1, 16)), ("ssd_norm_w", (1, D)), ("sb_norm_w", (1, D)), ("norm_mem_w", (1, D)),
         ("norm_memkv_w", (1, D)), ("norm_ffn_w", (1, D)), ("conv_ffn_b", (1, 5632)), ("norm_final_w", (D,))]
LOSS_ROW = sum(_tight_rows(_shp) for _, _shp in SMALL)
SMALL_ROWS = LOSS_ROW + 1 + (-(LOSS_ROW + 1)) % 8
ORDER = ["norm_mix_w", "w_in", "conv_ssd_w", "conv_ssd_b", "dt_bias", "a_log", "d_skip", "ssd_norm_w",
         "sb_norm_w", "w_out", "norm_mem_w", "norm_memkv_w", "w_mq", "w_mk", "w_mv", "w_mo", "norm_ffn_w",
         "w_up", "conv_ffn_w", "conv_ffn_b", "w_down", "norm_final_w"]


def _pad_rows(a, nr):
    n = a.shape[1]
    return jnp.concatenate([a, jnp.zeros((NDEV, nr * D - n), a.dtype)], axis=1).reshape(NDEV, nr, D)


def _group_sum(lanes):
    return lanes.reshape(16, 64).sum(axis=1).reshape(1, 16)


def kernel(x, mem, norm_mix_w, w_in, conv_ssd_w, conv_ssd_b, dt_bias, a_log, d_skip, ssd_norm_w, sb_norm_w, w_out, norm_mem_w, norm_memkv_w, w_mq, w_mk, w_mv, w_mo, norm_ffn_w, w_up, conv_ffn_w, conv_ffn_b, w_down, norm_final_w, loss_target, m_norm_mix_w, m_w_in, m_conv_ssd_w, m_conv_ssd_b, m_dt_bias, m_a_log, m_d_skip, m_ssd_norm_w, m_sb_norm_w, m_w_out, m_norm_mem_w, m_norm_memkv_w, m_w_mq, m_w_mk, m_w_mv, m_w_mo, m_norm_ffn_w, m_w_up, m_conv_ffn_w, m_conv_ffn_b, m_w_down, m_norm_final_w, v_norm_mix_w, v_w_in, v_conv_ssd_w, v_conv_ssd_b, v_dt_bias, v_a_log, v_d_skip, v_ssd_norm_w, v_sb_norm_w, v_w_out, v_norm_mem_w, v_norm_memkv_w, v_w_mq, v_w_mk, v_w_mv, v_w_mo, v_norm_ffn_w, v_w_up, v_conv_ffn_w, v_conv_ffn_b, v_w_down, v_norm_final_w):
    P = dict(norm_mix_w=norm_mix_w, w_in=w_in, conv_ssd_w=conv_ssd_w, conv_ssd_b=conv_ssd_b, dt_bias=dt_bias, a_log=a_log, d_skip=d_skip, ssd_norm_w=ssd_norm_w, sb_norm_w=sb_norm_w, w_out=w_out, norm_mem_w=norm_mem_w, norm_memkv_w=norm_memkv_w, w_mq=w_mq, w_mk=w_mk, w_mv=w_mv, w_mo=w_mo, norm_ffn_w=norm_ffn_w, w_up=w_up, conv_ffn_w=conv_ffn_w, conv_ffn_b=conv_ffn_b, w_down=w_down, norm_final_w=norm_final_w)
    M = dict(norm_mix_w=m_norm_mix_w, w_in=m_w_in, conv_ssd_w=m_conv_ssd_w, conv_ssd_b=m_conv_ssd_b, dt_bias=m_dt_bias, a_log=m_a_log, d_skip=m_d_skip, ssd_norm_w=m_ssd_norm_w, sb_norm_w=m_sb_norm_w, w_out=m_w_out, norm_mem_w=m_norm_mem_w, norm_memkv_w=m_norm_memkv_w, w_mq=m_w_mq, w_mk=m_w_mk, w_mv=m_w_mv, w_mo=m_w_mo, norm_ffn_w=m_norm_ffn_w, w_up=m_w_up, conv_ffn_w=m_conv_ffn_w, conv_ffn_b=m_conv_ffn_b, w_down=m_w_down, norm_final_w=m_norm_final_w)
    V = dict(norm_mix_w=v_norm_mix_w, w_in=v_w_in, conv_ssd_w=v_conv_ssd_w, conv_ssd_b=v_conv_ssd_b, dt_bias=v_dt_bias, a_log=v_a_log, d_skip=v_d_skip, ssd_norm_w=v_ssd_norm_w, sb_norm_w=v_sb_norm_w, w_out=v_w_out, norm_mem_w=v_norm_mem_w, norm_memkv_w=v_norm_memkv_w, w_mq=v_w_mq, w_mk=v_w_mk, w_mv=v_w_mv, w_mo=v_w_mo, norm_ffn_w=v_norm_ffn_w, w_up=v_w_up, conv_ffn_w=v_conv_ffn_w, conv_ffn_b=v_conv_ffn_b, w_down=v_w_down, norm_final_w=v_norm_final_w)
    small_shapes = [shp for _, shp in SMALL]

    def packed(src, names, dtype=F32):
        return _pack([src[n][0] for n in names], _layout(names)[1]).astype(dtype)

    def columns(g):
        return g.transpose(1, 0, 2).reshape(g.shape[1], NDEV * g.shape[2])

    g_in, g_taps = _all_gather([w_in[0].astype(BF16), packed(P, CONV_TAPS)], name="gather_w_in")
    W_in = columns(g_in)
    cw_ssd = g_taps[:, 0].reshape(NDEV, -1)[:, :768].reshape(NDEV, 4, 192).transpose(1, 0, 2).reshape(4, XBC)
    cw_ffn = (g_taps[:, PACK_ALIGN:PACK_ALIGN + 3].reshape(NDEV, -1)[:, :2112].reshape(NDEV, 3, 704)
              .transpose(1, 0, 2).reshape(3, 2 * DFF))
    W_z, W_xbc, W_dt, W_qkv = W_in[:, :D], W_in[:, D:D + XBC], W_in[:, D + XBC:D + XBC + 16], W_in[:, D + XBC + 16:]
    W_dtr = jnp.repeat(W_dt, 64, axis=1)
    cwg, cwv = cw_ffn[:, :DFF], cw_ffn[:, DFF:]
    cbg, cbv = conv_ffn_b[:, :DFF], conv_ffn_b[:, DFF:]
    rep = lambda p: jnp.repeat(p, 64, axis=1)
    lanes = jnp.concatenate([rep(dt_bias), rep(a_log), rep(d_skip), ssd_norm_w, jnp.zeros((4, D), F32)], axis=0)

    xs, tgt, mm = x[0], loss_target[0], mem[0]

    h1 = _norm_fwd(xs, norm_mix_w, name="norm_mix")
    z = _mm(h1, W_z, name="proj_z")
    xbc = _mm(h1, W_xbc, name="proj_xbc")
    dtr = _mm(h1, W_dtr, name="proj_dt")
    qkv = _mm(h1, W_qkv, name="proj_qkv", out_dtype=BF16)
    w_up16 = w_up[0].astype(BF16)
    y_ssd, states, g_ffn, g_up0 = _ssd_fwd(z, xbc, dtr, cw_ssd, conv_ssd_b, lanes,
                                           [packed(P, GATHER_FFN, BF16), w_up16[:D // 2]], name="ssd_fwd")
    o_sb, g_mid, g_up1 = _sb_fwd(qkv, [packed(P, GATHER_MID, BF16), w_up16[D // 2:]], name="sb_fwd")
    g_up = jnp.concatenate([g_up0, g_up1], axis=1)
    r_mid = _layout(GATHER_MID)[0]
    W_out = g_mid[:, r_mid["w_out"]:r_mid["w_out"] + 256].reshape(2 * D, D)
    W_mq, W_mk, W_mv, W_mo = [g_mid[:, r_mid[n]:r_mid[n] + 128].reshape(D, D)
                              for n in ("w_mq", "w_mk", "w_mv", "w_mo")]
    W_up = columns(g_up)
    W_down = g_ffn[:, 0:352].reshape(DFF, D)
    W_upg, W_upv = W_up[:, :DFF], W_up[:, DFF:]
    y_sb = _head_norm_fwd(o_sb, sb_norm_w, name="sb_norm")
    ymix = jnp.concatenate([y_ssd, y_sb], axis=1)
    x1 = _mm(ymix, W_out, add=xs, name="proj_out")
    h2 = _norm_fwd(x1, norm_mem_w, name="norm_mem")
    mn = _norm_fwd(mm, norm_memkv_w, name="norm_memkv")
    qm = _mm(h2, W_mq, name="mem_q", out_dtype=BF16)
    km = _mm(mn, W_mk, name="mem_k", out_dtype=BF16)
    vm = _mm(mn, W_mv, name="mem_v", out_dtype=BF16)
    om = _mem_attn_fwd(qm, km, vm, name="mem_attn")
    x2 = _mm(om, W_mo, add=x1, name="mem_o")
    h3 = _norm_fwd(x2, norm_ffn_w, name="norm_ffn")
    ug = _mm(h3, W_upg, name="ffn_up_g")
    uv = _mm(h3, W_upv, name="ffn_up_v")
    f = _glu_fwd(ug, uv, cwg, cwv, cbg, cbv, name="ffn_glu")
    x3 = _mm(f, W_down, add=x2, name="ffn_down")
    dx3, g_nfinal, loss_part = _final(x3, norm_final_w.reshape(1, D), tgt, name="final_loss")

    G = {}
    G["w_down"] = _mm(f, dx3, trans_a=True, name="g_w_down")
    df = _mm(dx3, W_down, trans_b=True, name="d_f")
    dupg, dupv, dcg, dcv = _ffn_bwd(ug, uv, df, cwg, cwv, cbg, cbv, name="ffn_glu_bwd")
    G["w_up"] = jnp.concatenate([_mm(h3, dupg, trans_a=True, name="g_w_up_g"),
                                 _mm(h3, dupv, trans_a=True, name="g_w_up_v")], axis=1)
    G["conv_ffn_w"] = jnp.concatenate([dcg[0:3], dcv[0:3]], axis=1)
    G["conv_ffn_b"] = jnp.concatenate([dcg[3:4], dcv[3:4]], axis=1)
    dh3 = _mm(dupg, W_upg, trans_b=True, name="d_h3_g")
    dh3 = _mm(dupv, W_upv, trans_b=True, add=dh3, name="d_h3_v")
    dx2, G["norm_ffn_w"] = _norm_bwd(x2, norm_ffn_w, dh3, dx3, name="norm_ffn_bwd")
    G["w_mo"] = _mm(om, dx2, trans_a=True, name="g_w_mo")
    dom = _mm(dx2, W_mo, trans_b=True, name="d_om")
    dqm, dkm, dvm = _mem_attn_bwd(qm, km, vm, dom, name="mem_attn_bwd")
    G["w_mq"] = _mm(h2, dqm, trans_a=True, name="g_w_mq")
    G["w_mk"] = _mm(mn, dkm, trans_a=True, name="g_w_mk")
    G["w_mv"] = _mm(mn, dvm, trans_a=True, name="g_w_mv")
    dh2 = _mm(dqm, W_mq, trans_b=True, name="d_h2")
    dmn = _mm(dkm, W_mk, trans_b=True, name="d_mn_k")
    dmn = _mm(dvm, W_mv, trans_b=True, add=dmn, name="d_mn_v")
    _, G["norm_memkv_w"] = _norm_bwd(mm, norm_memkv_w, dmn, None, name="norm_memkv_bwd")
    dx1, G["norm_mem_w"] = _norm_bwd(x1, norm_mem_w, dh2, dx2, name="norm_mem_bwd")
    G["w_out"] = _mm(ymix, dx1, trans_a=True, name="g_w_out")
    dymix = _mm(dx1, W_out, trans_b=True, name="d_ymix")
    do_sb, G["sb_norm_w"] = _head_norm_bwd(o_sb, sb_norm_w, dymix, name="sb_norm_bwd")

    dz, dxbc, ddtr, dlanes, dconv = _ssd_bwd(z, xbc, dtr, states, dymix, cw_ssd, conv_ssd_b, lanes, name="ssd_bwd")
    G["dt_bias"], G["a_log"], G["d_skip"] = [_group_sum(dlanes[i:i + 1]) for i in range(3)]
    G["ssd_norm_w"] = dlanes[3:4]
    G["conv_ssd_w"], G["conv_ssd_b"] = dconv[0:4], dconv[4:5]

    def col_slabs(g, cols):
        return g.reshape(g.shape[0], NDEV, cols).transpose(1, 0, 2).astype(BF16)

    def packed_slabs(names):
        parts = []
        for n in names:
            shp = SHARD[n]
            if shp[-1] == D:
                t = G[n].reshape((NDEV,) + shp)
            else:
                t = _pad_rows(G[n].reshape(shp[0], NDEV, shp[1]).transpose(1, 0, 2).reshape(NDEV, -1),
                              _part_rows(shp))
            parts.append(jnp.pad(t, ((0, 0), (0, _part_rows(shp) - t.shape[1]), (0, 0))))
        used = sum(t.shape[1] for t in parts)
        parts.append(jnp.zeros((NDEV, _layout(names)[1] - used, D), F32))
        return jnp.concatenate(parts, axis=1).astype(BF16)

    dq, dk, dv, recv_packed, recv_up = _sb_bwd(qkv, o_sb, do_sb, [packed_slabs(GRADS_PACKED), col_slabs(G["w_up"], 704)],
                                               name="sb_bwd")
    dproj = jnp.concatenate([dz, dxbc, ddtr, dq, dk, dv], axis=1)
    g_proj = _mm(h1, dproj, trans_a=True, name="g_w_in")
    c_dt = D + XBC
    G["w_in"] = jnp.concatenate([g_proj[:, :c_dt], g_proj[:, c_dt:c_dt + D].reshape(D, 16, 64).sum(axis=2),
                                 g_proj[:, c_dt + D:]], axis=1)
    W_proj = jnp.concatenate([W_z, W_xbc, W_dtr, W_qkv], axis=1)
    dh1, recv_in = _mm(dproj, W_proj, trans_b=True, rides=[col_slabs(G["w_in"], 706)], name="d_h1")
    dx, G["norm_mix_w"] = _norm_bwd(xs, norm_mix_w, dh1, dx1, name="norm_mix_bwd")
    G["norm_final_w"] = g_nfinal.reshape(D)

    small_g = _pack_tight([G[n] for n, _ in SMALL] + [loss_part], SMALL_ROWS)
    (parts_small,) = _exchange([small_g], [True], name="exchange_grads")
    outs_packed = _adamw(recv_packed, packed(P, GRADS_PACKED), packed(M, GRADS_PACKED), packed(V, GRADS_PACKED),
                         name="adamw_packed")
    outs_up = _adamw(recv_up, w_up[0], m_w_up[0], v_w_up[0], name="adamw_w_up")
    outs_in = _adamw(recv_in, w_in[0], m_w_in[0], v_w_in[0], name="adamw_w_in")
    outs_small = _adamw(parts_small, _pack_tight([P[n] for n, _ in SMALL], SMALL_ROWS),
                        _pack_tight([M[n] for n, _ in SMALL], SMALL_ROWS),
                        _pack_tight([V[n] for n, _ in SMALL], SMALL_ROWS), name="adamw_replicated")

    res = {}
    for i, kind in enumerate(("grad", "delta", "new_m", "new_v")):
        for n, val in zip(GRADS_PACKED, _unpack(outs_packed[i], [SHARD[n] for n in GRADS_PACKED])):
            res[kind, n] = val.reshape((1,) + SHARD[n])
        res[kind, "w_up"] = outs_up[i].reshape((1,) + SHARD["w_up"])
        res[kind, "w_in"] = outs_in[i].reshape((1,) + SHARD["w_in"])
        for (n, shp), val in zip(SMALL, _unpack(outs_small[i], small_shapes, _tight_rows)):
            res[kind, n] = val
    loss = outs_small[0][LOSS_ROW, 0]
    out = [loss, dx.reshape(1, -1, D)]
    for kind in ("grad", "delta", "new_m", "new_v"):
        out += [res[kind, n] for n in ORDER]
    return tuple(out)
```

```python
import functools
import math

import jax
import jax.numpy as jnp
from jax import lax
from jax.experimental import pallas as pl
from jax.experimental.pallas import tpu as pltpu

F32 = jnp.float32
BF16 = jnp.bfloat16

D = 1024
NDEV = 8
EPS = 1e-6
SSD_CHUNK = 128
HALO = 8
VMEM_LIMIT = 56 * 2**20

ADAM_LR, ADAM_B1, ADAM_B2, ADAM_EPS, ADAM_WD, ADAM_STEP = 0.001, 0.9, 0.999, 1e-08, 0.01, 10


def _cp(*sem):
    return pltpu.CompilerParams(dimension_semantics=sem, vmem_limit_bytes=VMEM_LIMIT)


def _tile(n, cap, mult):
    if n <= cap:
        return n
    for d in range(cap - cap % mult, 0, -mult):
        if n % d == 0:
            return d
    raise ValueError(f"no tile for {n}")


def _sigmoid(x):
    return 1.0 / (1.0 + jnp.exp(-x))


def _silu(x):
    return x * _sigmoid(x)


def _softplus(x):
    return jnp.maximum(x, 0.0) + jnp.log(1.0 + jnp.exp(-jnp.abs(x)))


def _terms(x, n):
    out = []
    r = x.astype(F32)
    for i in range(n):
        h = r.astype(BF16)
        out.append(h)
        if i + 1 < n:
            r = r - h.astype(F32)
    return out


_DIMS = {"nn": ((1,), (0,)), "nt": ((1,), (1,)), "tn": ((0,), (0,))}


def _dot_raw(form, a, b, ta, tb):
    acc = None
    for ai in _terms(a, ta):
        for bi in _terms(b, tb):
            d = lax.dot_general(ai, bi, (_DIMS[form], ((), ())), preferred_element_type=F32)
            acc = d if acc is None else acc + d
    return acc


@functools.lru_cache(maxsize=None)
def _dot_fn(form, ta, tb):
    @jax.custom_vjp
    def f(a, b):
        return _dot_raw(form, a, b, ta, tb)

    def fwd(a, b):
        return f(a, b), (a, b)

    def bwd(res, ct):
        a, b = res
        if form == "nn":
            return _dot_fn("nt", ta, tb)(ct, b), _dot_fn("tn", ta, tb)(a, ct)
        if form == "nt":
            return _dot_fn("nn", ta, tb)(ct, b), _dot_fn("tn", tb, ta)(ct, a)
        return _dot_fn("nt", tb, ta)(b, ct), _dot_fn("nn", ta, tb)(a, ct)

    f.defvjp(fwd, bwd)
    return f


def _dot(form, a, b, ta=1, tb=1):
    return _dot_fn(form, ta, tb)(a, b)


@functools.lru_cache(maxsize=None)
def _take_fn(axis, idx):
    @jax.custom_vjp
    def f(x):
        return x[:, idx:idx + 1] if axis == 1 else x[idx:idx + 1, :]

    def fwd(x):
        return f(x), x.shape

    def bwd(shape, ct):
        io = lax.broadcasted_iota(jnp.int32, shape, axis)
        return (jnp.where(io == idx, jnp.broadcast_to(ct, shape), 0.0),)

    f.defvjp(fwd, bwd)
    return f


@functools.lru_cache(maxsize=None)
def _split_fn(width, n):
    @jax.custom_vjp
    def f(x):
        return tuple(x[:, i * width:(i + 1) * width] for i in range(n))

    def fwd(x):
        return f(x), None

    def bwd(_, cts):
        return (jnp.concatenate(list(cts), axis=1),)

    f.defvjp(fwd, bwd)
    return f


def _split(x, width):
    return _split_fn(width, x.shape[1] // width)(x)


def _iota(shape, axis):
    return lax.broadcasted_iota(jnp.int32, shape, axis)


MM_VMEM_BUDGET = 44 * 2**20


def _mm_tiles(m, n, kt, trans_a, a_bytes, b_bytes, out_bytes, add_bytes):
    tn = _tile(n, 1536, 128)
    for tm_cap in (1408, 1024, 512, 256, 128):
        tm = _tile(m, tm_cap, 128 if trans_a else 8)
        for tk_cap in (kt, 4096, 2048, 1024, 512):
            tk = _tile(kt, tk_cap, 128)
            blocks = tm * tk * a_bytes + tk * tn * b_bytes + tm * tn * (out_bytes + add_bytes)
            if 2 * blocks + (tm * tn * 4 if tk < kt else 0) <= MM_VMEM_BUDGET:
                return tm, tn, tk
    raise ValueError(f"no matmul tiling for {(m, n, kt)}")


def _mm(a, b, *, name, add=None, trans_a=False, trans_b=False, out_dtype=F32, rides=()):
    assert not (trans_a and trans_b)
    if trans_a:
        kt, m = a.shape
    else:
        m, kt = a.shape
    n, kt2 = b.shape if trans_b else b.shape[::-1]
    assert kt == kt2, (a.shape, b.shape)
    tm, tn, tk = _mm_tiles(m, n, kt, trans_a, a.dtype.itemsize, b.dtype.itemsize,
                           jnp.dtype(out_dtype).itemsize, 0 if add is None else add.dtype.itemsize)
    nk = kt // tk
    grid = (m // tm, n // tn, nk)
    rd = _Rides(rides, [False] * len(rides))
    n_in = 2 if add is None else 3

    def body(*all_refs):
        ins, (o_ref,), scratch, handles = rd.split(all_refs, n_in, 1, 1 if nk > 1 else 0)
        refs = (*ins, o_ref, *scratch)
        if rides:
            ids = [pl.program_id(ax) for ax in range(3)]
            rd.run(handles, (ids[0] == 0) & (ids[1] == 0) & (ids[2] == 0),
                   (ids[0] == grid[0] - 1) & (ids[1] == grid[1] - 1) & (ids[2] == grid[2] - 1))
        if add is None:
            a_ref, b_ref, o_ref = refs[:3]
        else:
            a_ref, b_ref, add_ref, o_ref = refs[:4]
        k = pl.program_id(2)
        av = a_ref[...].astype(BF16)
        bv = b_ref[...].astype(BF16)
        dims = _DIMS["tn" if trans_a else "nt" if trans_b else "nn"]
        d = lax.dot_general(av, bv, (dims, ((), ())), preferred_element_type=F32)

        def finish(r):
            if add is not None:
                r = r + add_ref[...]
            o_ref[...] = r.astype(out_dtype)

        if nk == 1:
            finish(d)
        else:
            acc = refs[-1]

            @pl.when(k == 0)
            def _():
                acc[...] = d

            @pl.when((k > 0) & (k < nk - 1))
            def _():
                acc[...] += d

            @pl.when(k == nk - 1)
            def _():
                finish(acc[...] + d)

    a_spec = (pl.BlockSpec((tk, tm), lambda i, j, k: (k, i)) if trans_a
              else pl.BlockSpec((tm, tk), lambda i, j, k: (i, k)))
    b_spec = (pl.BlockSpec((tn, tk), lambda i, j, k: (j, k)) if trans_b
              else pl.BlockSpec((tk, tn), lambda i, j, k: (k, j)))
    in_specs = [a_spec, b_spec]
    args = [a, b]
    if add is not None:
        in_specs.append(pl.BlockSpec((tm, tn), lambda i, j, k: (i, j)))
        args.append(add)
    out = pl.pallas_call(
        body, name=name, grid=grid,
        in_specs=in_specs + rd.in_specs,
        out_specs=[pl.BlockSpec((tm, tn), lambda i, j, k: (i, j))] + rd.out_specs,
        out_shape=[jax.ShapeDtypeStruct((m, n), out_dtype)] + rd.out_shape,
        scratch_shapes=([pltpu.VMEM((tm, tn), F32)] if nk > 1 else []) + rd.scratch,
        compiler_params=_cp(*(("arbitrary",) * 3 if rides else ("parallel", "parallel", "arbitrary"))),
    )(*args, *rides)
    return out if rides else out[0]


def _rstd(x):
    return lax.rsqrt(jnp.mean(x * x, axis=-1, keepdims=True) + EPS)


def _norm_fwd(x, w, *, name):
    s = x.shape[0]
    tm = _tile(s, 512, 8)

    def body(x_ref, w_ref, o_ref):
        xv = x_ref[...]
        o_ref[...] = (xv * _rstd(xv) * w_ref[...]).astype(BF16)

    return pl.pallas_call(
        body, name=name, grid=(s // tm,),
        in_specs=[pl.BlockSpec((tm, D), lambda i: (i, 0)), pl.BlockSpec((1, D), lambda i: (0, 0))],
        out_specs=pl.BlockSpec((tm, D), lambda i: (i, 0)),
        out_shape=jax.ShapeDtypeStruct((s, D), BF16), compiler_params=_cp("parallel"),
    )(x, w)


def _norm_bwd_math(xv, wv, dy):
    r = _rstd(xv)
    xh = xv * r
    dxh = dy * wv
    dx = r * (dxh - xh * jnp.mean(dxh * xh, axis=-1, keepdims=True))
    dw = jnp.sum(dy * xh, axis=0, keepdims=True)
    return dx, dw


def _norm_bwd(x, w, dy, add, *, name):
    s = x.shape[0]
    tm = _tile(s, 512, 8)

    def body(*refs):
        if add is None:
            x_ref, w_ref, dy_ref, dx_ref, dw_ref = refs
        else:
            x_ref, w_ref, dy_ref, add_ref, dx_ref, dw_ref = refs

        @pl.when(pl.program_id(0) == 0)
        def _():
            dw_ref[...] = jnp.zeros_like(dw_ref)

        dx, dw = _norm_bwd_math(x_ref[...], w_ref[...], dy_ref[...])
        if add is not None:
            dx = dx + add_ref[...]
        dx_ref[...] = dx
        dw_ref[...] += dw

    row = pl.BlockSpec((tm, D), lambda i: (i, 0))
    vec = pl.BlockSpec((1, D), lambda i: (0, 0))
    in_specs = [row, vec, row] + ([row] if add is not None else [])
    args = [x, w, dy] + ([add] if add is not None else [])
    return pl.pallas_call(
        body, name=name, grid=(s // tm,), in_specs=in_specs, out_specs=[row, vec],
        out_shape=[jax.ShapeDtypeStruct((s, D), F32), jax.ShapeDtypeStruct((1, D), F32)],
        compiler_params=_cp("arbitrary"),
    )(*args)


def _final(x3, w, target, *, name):
    s = x3.shape[0]
    tm = _tile(s, 512, 8)

    def body(x_ref, w_ref, t_ref, dx_ref, dw_ref, loss_ref):
        @pl.when(pl.program_id(0) == 0)
        def _():
            dw_ref[...] = jnp.zeros_like(dw_ref)
            loss_ref[...] = jnp.zeros_like(loss_ref)

        xv = x_ref[...]
        wv = w_ref[...]
        y = xv * _rstd(xv) * wv
        err = y - t_ref[...]
        loss_ref[...] += 0.5 * jnp.sum(jnp.mean(err * err, axis=-1, keepdims=True))
        dx, dw = _norm_bwd_math(xv, wv, err * (1.0 / D))
        dx_ref[...] = dx
        dw_ref[...] += dw

    row = pl.BlockSpec((tm, D), lambda i: (i, 0))
    vec = pl.BlockSpec((1, D), lambda i: (0, 0))
    return pl.pallas_call(
        body, name=name, grid=(s // tm,), in_specs=[row, vec, row], out_specs=[row, vec, vec],
        out_shape=[jax.ShapeDtypeStruct((s, D), F32), jax.ShapeDtypeStruct((1, D), F32),
                   jax.ShapeDtypeStruct((1, D), F32)],
        compiler_params=_cp("arbitrary"),
    )(x3, w, target)


def _head_norm_math(o, w):
    lane = _iota((128, 128), 0) // 64
    bd = (lane == _iota((128, 128), 1) // 64).astype(F32)
    outs = []
    for op in _split(o, 128):
        ms = _dot("nn", op * op, bd, 2, 1) * (1.0 / 64)
        outs.append(op * lax.rsqrt(ms + EPS))
    return jnp.concatenate(outs, axis=1) * w


def _head_norm_fwd(o, w, *, name):
    s = o.shape[0]
    tm = _tile(s, 256, 8)

    def body(o_ref, w_ref, y_ref):
        y_ref[...] = _head_norm_math(o_ref[...], w_ref[...]).astype(BF16)

    row = pl.BlockSpec((tm, D), lambda i: (i, 0))
    vec = pl.BlockSpec((1, D), lambda i: (0, 0))
    return pl.pallas_call(
        body, name=name, grid=(s // tm,), in_specs=[row, vec], out_specs=row,
        out_shape=jax.ShapeDtypeStruct((s, D), BF16), compiler_params=_cp("parallel"),
    )(o, w)


def _head_norm_bwd(o, w, dymix, *, name):
    s = o.shape[0]
    tm = _tile(s, 256, 8)

    def body(o_ref, w_ref, dy_ref, do_ref, dw_ref):
        @pl.when(pl.program_id(0) == 0)
        def _():
            dw_ref[...] = jnp.zeros_like(dw_ref)

        _, vjp = jax.vjp(_head_norm_math, o_ref[...], w_ref[...])
        do, dw = vjp(dy_ref[...])
        do_ref[...] = do
        dw_ref[...] += dw

    row = pl.BlockSpec((tm, D), lambda i: (i, 0))
    vec = pl.BlockSpec((1, D), lambda i: (0, 0))
    return pl.pallas_call(
        body, name=name, grid=(s // tm,),
        in_specs=[row, vec, pl.BlockSpec((tm, D), lambda i: (i, 1))], out_specs=[row, vec],
        out_shape=[jax.ShapeDtypeStruct((s, D), F32), jax.ShapeDtypeStruct((1, D), F32)],
        compiler_params=_cp("arbitrary"),
    )(o, w, dymix)


SB_BQ = 256
SB_BK = 256


def _sb_consts():
    r = _iota((SB_BK, SB_BK), 0)
    c = _iota((SB_BK, SB_BK), 1)
    u_excl = (r > c).astype(BF16)
    u_incl = (r >= c).astype(BF16)
    return u_excl, u_incl


SB_LANES = 256
SB_NCH = SB_LANES // 64


def _nt(a, b):
    return lax.dot_general(a, b, (_DIMS["nt"], ((), ())), preferred_element_type=F32)


def _tn(a, b):
    return lax.dot_general(a, b, (_DIMS["tn"], ((), ())), preferred_element_type=F32)


def _nn(a, b):
    return jnp.dot(a, b, preferred_element_type=F32)


def _sb_heads(ref):
    out = []
    for hp in range(SB_LANES // 128):
        v = ref[:, 128 * hp:128 * (hp + 1)]
        first = _iota(v.shape, 1) < 64
        out += [jnp.where(first, v, 0).astype(BF16), jnp.where(first, 0, v).astype(BF16)]
    return out


SB_STRIP = 32


def _neg_abs(x):
    bits = lax.bitcast_convert_type(x, jnp.uint32) | jnp.uint32(0x80000000)
    return lax.bitcast_convert_type(bits, F32)


def _sb_block(ref, j):
    off = pl.multiple_of(j * SB_BK, SB_BK)
    return [ref[pl.ds(off, SB_BK), 128 * hp:128 * (hp + 1)] for hp in range(SB_NCH // 2)]


SB_DEAD = 104.0


def _sb_live(nlrun):
    m = nlrun[0]
    for x in nlrun[1:]:
        m = jnp.minimum(m, x)
    return jnp.min(m) < SB_DEAD


def _sb_strips():
    return [(r, pl.ds(r, SB_STRIP)) for r in range(0, SB_BQ, SB_STRIP)]


def _sb_diag_mask(r):
    return _iota((SB_STRIP, SB_BK), 1) < _iota((SB_STRIP, SB_BK), 0) + r


def _sb_soft(z, mask):
    e = jnp.exp(_neg_abs(z))
    nl = jnp.maximum(z, 0.0) + jnp.log(1.0 + e)
    if mask is not None:
        nl = jnp.where(mask, nl, 0.0)
    return e, nl


def _sb_split_to(hl_ref, rows, x):
    hi, lo = _terms(x, 2)
    hl_ref[rows, 0:SB_BK] = hi
    hl_ref[rows, SB_BK:2 * SB_BK] = lo


def _sb_stage_soft(z_ref, nl_ref, diag):
    for r, rows in _sb_strips():
        _, nl = _sb_soft(z_ref[rows, :], _sb_diag_mask(r) if diag else None)
        nl_ref[rows, 0:SB_BK] = nl.astype(BF16)


def _sb_stage_weights(z_ref, c_ref, a_ref, nlrun, diag):
    for r, rows in _sb_strips():
        a = jnp.exp(z_ref[rows, :] - c_ref[rows, :] - nlrun[r:r + SB_STRIP, :])
        if diag:
            a = jnp.where(_sb_diag_mask(r), a, 0.0)
        a_ref[rows, :] = a.astype(BF16)


def _sb_fwd(qkv, rides, *, name):
    s = qkv.shape[0]
    nq = s // SB_BQ
    ng = D // SB_LANES
    assert SB_BQ == SB_BK
    rd = _Rides(rides, [True] * len(rides))

    def body(*refs):
        (q_ref, k_ref, v_ref), (o_ref,), (zbuf, nlbuf, cbuf, abuf), handles = rd.split(refs, 3, 1, 4)
        i = pl.program_id(1)
        step_no = pl.program_id(0) * nq + i
        rd.run(handles, step_no == 0, step_no == ng * nq - 1)

        _, u_incl = _sb_consts()
        lane_a = _iota((SB_BQ, 128), 1) < 64
        qh = [q * 0.125 for q in _sb_heads(q_ref)]

        def tile(j, accs, nlrun, diag):
            kbs = _sb_block(k_ref, j)
            for c in range(SB_NCH):
                zbuf[c] = _nt(qh[c], kbs[c // 2])
            for c in range(SB_NCH):
                _sb_stage_soft(zbuf.at[c], nlbuf.at[c], diag)
                cbuf[c] = _nn(nlbuf[c], u_incl)
            for c in range(SB_NCH):
                _sb_stage_weights(zbuf.at[c], cbuf.at[c], abuf.at[c], nlrun[c], diag)
            nlrun = tuple(nlrun[c] + cbuf[c, :, 0:1] for c in range(SB_NCH))
            vbs = _sb_block(v_ref, j)
            outs = [_nn(abuf[c], vbs[c // 2]) for c in range(SB_NCH)]
            accs = tuple(acc + jnp.where(lane_a, outs[2 * hp], outs[2 * hp + 1]) for hp, acc in enumerate(accs))
            return accs, nlrun

        accs, nlrun = tile(i, (jnp.zeros((SB_BQ, 128), F32),) * (SB_NCH // 2),
                           (jnp.zeros((SB_BQ, 1), F32),) * SB_NCH, True)

        def step(carry):
            j, _, accs, nlrun = carry
            accs, nlrun = tile(j, accs, nlrun, False)
            return j - 1, _sb_live(nlrun), accs, nlrun

        _, _, accs, _ = lax.while_loop(lambda c: (c[0] >= 0) & c[1], step, (i - 1, _sb_live(nlrun), accs, nlrun))
        o_ref[...] = jnp.concatenate(accs, axis=1)

    return pl.pallas_call(
        body, name=name, grid=(ng, nq),
        in_specs=[pl.BlockSpec((SB_BQ, SB_LANES), lambda g, i: (i, g)),
                  pl.BlockSpec((s, SB_LANES), lambda g, i: (0, ng + g)),
                  pl.BlockSpec((s, SB_LANES), lambda g, i: (0, 2 * ng + g)), *rd.in_specs],
        out_specs=[pl.BlockSpec((SB_BQ, SB_LANES), lambda g, i: (i, g)), *rd.out_specs],
        out_shape=[jax.ShapeDtypeStruct((s, D), F32), *rd.out_shape],
        scratch_shapes=[pltpu.VMEM((SB_NCH, SB_BQ, SB_BK), F32), pltpu.VMEM((SB_NCH, SB_BQ, SB_BK), BF16),
                        pltpu.VMEM((SB_NCH, SB_BQ, SB_BK), F32), pltpu.VMEM((SB_NCH, SB_BQ, SB_BK), BF16),
                        *rd.scratch],
        compiler_params=_cp("arbitrary", "arbitrary"),
    )(qkv, qkv, qkv, *rides)


def _sb_bwd(qkv, o, do, rides, *, name):
    s = qkv.shape[0]
    nq = s // SB_BQ
    ng = D // SB_LANES
    nhp = SB_NCH // 2
    rd = _Rides(rides, [False] * len(rides))

    def body(*refs):
        ins, outs, scratch, handles = rd.split(refs, 5, 3, 11)
        q_ref, k_ref, v_ref, o_ref, do_ref = ins
        dq_ref, dk_hbm, dv_hbm = outs
        dk_acc, dv_acc, dk16, dv16, sems, zbuf, gbuf, hl, cbuf, abuf, dzbuf = scratch
        g_idx = pl.program_id(0)
        i = pl.program_id(1)
        step_no = g_idx * nq + i
        rd.run(handles, step_no == 0, step_no == ng * nq - 1)

        @pl.when(i == 0)
        def _():
            dk_acc[...] = jnp.zeros_like(dk_acc)
            dv_acc[...] = jnp.zeros_like(dv_acc)

        _, u_incl = _sb_consts()
        u2 = jnp.concatenate([u_incl, u_incl], axis=0)
        lane_a = _iota((SB_BQ, 128), 1) < 64
        lane_k = _iota((SB_BK, 128), 1) < 64
        qh = [q * 0.125 for q in _sb_heads(q_ref)]
        qf = [q_ref[:, 128 * hp:128 * (hp + 1)] for hp in range(nhp)]
        doh = _sb_heads(do_ref)
        dof = [do_ref[:, 128 * hp:128 * (hp + 1)].astype(BF16) for hp in range(nhp)]
        delta = []
        for hp in range(nhp):
            prod = dof[hp].astype(F32) * o_ref[:, 128 * hp:128 * (hp + 1)]
            delta += [jnp.sum(jnp.where(lane_a, prod, 0.0), axis=1, keepdims=True),
                      jnp.sum(jnp.where(lane_a, 0.0, prod), axis=1, keepdims=True)]

        def pre(slot, j):
            kbs = _sb_block(k_ref, j)
            vbs = _sb_block(v_ref, j)
            for c in range(SB_NCH):
                zbuf[slot, c] = _nt(qh[c], kbs[c // 2])
                gbuf[slot, c] = _nt(doh[c], vbs[c // 2])

        def stage_g(c, slot):
            for _, rows in _sb_strips():
                g = abuf[slot, c, rows, :].astype(F32) * gbuf[slot, c, rows, :]
                gbuf[slot, c, rows, :] = g
                _sb_split_to(hl.at[c], rows, g)

        def stage_dz(c, slot, grun, diag):
            for r, rows in _sb_strips():
                z = zbuf[slot, c, rows, :]
                g = gbuf[slot, c, rows, :]
                cs = (delta[c] - grun)[r:r + SB_STRIP, :] - cbuf[c, rows, :]
                sig = 1.0 / (1.0 + jnp.exp(-z))
                dz = g - (g + cs) * sig
                if diag:
                    dz = jnp.where(_sb_diag_mask(r), dz, 0.0)
                dzbuf[slot, c, rows, :] = dz.astype(BF16)

        def chain(slot, nlrun, grun, diag):
            for c in range(SB_NCH):
                _sb_stage_soft(zbuf.at[slot, c], hl.at[c], diag)
                cbuf[c] = _nn(hl[c, :, 0:SB_BK], u_incl)
            nl_tot = []
            for c in range(SB_NCH):
                _sb_stage_weights(zbuf.at[slot, c], cbuf.at[c], abuf.at[slot, c], nlrun[c], diag)
                nl_tot.append(cbuf[c, :, 0:1])
                stage_g(c, slot)
                cbuf[c] = _nn(hl[c], u2)
            g_tot = []
            for c in range(SB_NCH):
                stage_dz(c, slot, grun[c], diag)
                g_tot.append(cbuf[c, :, 0:1])
            return (tuple(a + b for a, b in zip(nlrun, nl_tot)), tuple(a + b for a, b in zip(grun, g_tot)))

        def post(slot, j, dqs):
            off = pl.multiple_of(j * SB_BK, SB_BK)
            kbs = _sb_block(k_ref, j)
            dq_t = [_nn(dzbuf[slot, c], kbs[c // 2]) for c in range(SB_NCH)]
            dk_t = [_tn(dzbuf[slot, c], qf[c // 2]) for c in range(SB_NCH)]
            dv_t = [_tn(abuf[slot, c], dof[c // 2]) for c in range(SB_NCH)]
            for hp in range(nhp):
                cols = slice(128 * hp, 128 * (hp + 1))
                dk_acc[pl.ds(off, SB_BK), cols] += 0.125 * jnp.where(lane_k, dk_t[2 * hp], dk_t[2 * hp + 1])
                dv_acc[pl.ds(off, SB_BK), cols] += jnp.where(lane_k, dv_t[2 * hp], dv_t[2 * hp + 1])
            return tuple(dq + jnp.where(lane_a, dq_t[2 * hp], dq_t[2 * hp + 1]) for hp, dq in enumerate(dqs))

        def tile(j, dqs, nlrun, grun, diag):
            pre(0, j)
            nlrun, grun = chain(0, nlrun, grun, diag)
            return post(0, j, dqs), nlrun, grun

        zero = (jnp.zeros((SB_BQ, 1), F32),) * SB_NCH
        dqs, nlrun, grun = tile(i, (jnp.zeros((SB_BQ, 128), F32),) * nhp, zero, zero, True)

        def step(carry):
            j, _, dqs, nlrun, grun = carry
            dqs, nlrun, grun = tile(j, dqs, nlrun, grun, False)
            return j - 1, _sb_live(nlrun), dqs, nlrun, grun

        carry = lax.while_loop(lambda c: (c[0] >= 0) & c[1], step, (i - 1, _sb_live(nlrun), dqs, nlrun, grun))
        dq_ref[...] = (0.125 * jnp.concatenate(carry[2], axis=1)).astype(BF16)

        def out_copies(g):
            cols = pl.ds(pl.multiple_of(g * SB_LANES, SB_LANES), SB_LANES)
            return (pltpu.make_async_copy(dk16, dk_hbm.at[:, cols], sems.at[0]),
                    pltpu.make_async_copy(dv16, dv_hbm.at[:, cols], sems.at[1]))

        @pl.when((i == nq - 1) & (g_idx > 0))
        def _():
            for cp in out_copies(g_idx - 1):
                cp.wait()

        @pl.when(i == nq - 1)
        def _():
            def narrow(r, carry):
                rows = pl.ds(pl.multiple_of(r * SB_BK, SB_BK), SB_BK)
                dk16[rows, :] = dk_acc[rows, :].astype(BF16)
                dv16[rows, :] = dv_acc[rows, :].astype(BF16)
                return carry

            lax.fori_loop(0, s // SB_BK, narrow, 0)
            for cp in out_copies(g_idx):
                cp.start()

        @pl.when((i == nq - 1) & (g_idx == ng - 1))
        def _():
            for cp in out_copies(g_idx):
                cp.wait()

    qblk = pl.BlockSpec((SB_BQ, SB_LANES), lambda g, i: (i, g))
    hbm = pl.BlockSpec(memory_space=pl.ANY)
    return pl.pallas_call(
        body, name=name, grid=(ng, nq),
        in_specs=[qblk, pl.BlockSpec((s, SB_LANES), lambda g, i: (0, ng + g)),
                  pl.BlockSpec((s, SB_LANES), lambda g, i: (0, 2 * ng + g)), qblk, qblk, *rd.in_specs],
        out_specs=[qblk, hbm, hbm, *rd.out_specs],
        out_shape=[jax.ShapeDtypeStruct((s, D), BF16)] * 3 + rd.out_shape,
        scratch_shapes=[pltpu.VMEM((s, SB_LANES), F32), pltpu.VMEM((s, SB_LANES), F32),
                        pltpu.VMEM((s, SB_LANES), BF16), pltpu.VMEM((s, SB_LANES), BF16),
                        pltpu.SemaphoreType.DMA((2,)),
                        pltpu.VMEM((1, SB_NCH, SB_BQ, SB_BK), F32), pltpu.VMEM((1, SB_NCH, SB_BQ, SB_BK), F32),
                        pltpu.VMEM((SB_NCH, SB_BQ, 2 * SB_BK), BF16), pltpu.VMEM((SB_NCH, SB_BQ, SB_BK), F32),
                        pltpu.VMEM((1, SB_NCH, SB_BQ, SB_BK), BF16), pltpu.VMEM((1, SB_NCH, SB_BQ, SB_BK), BF16),
                        *rd.scratch],
        compiler_params=_cp("arbitrary", "arbitrary"),
    )(qkv, qkv, qkv, o, do, *rides)


def _ssd_core(z, xpre, dtr, state, dtb, alog, dsk, nw):
    L = SSD_CHUNK
    xa = _silu(xpre)
    pieces = _split(xa, 128)
    xs = jnp.concatenate(pieces[:8], axis=1)
    bm, cm = pieces[8:10], pieces[10:12]
    dt = _softplus(dtr + dtb)
    a = dt * (-jnp.exp(alog))
    tri = (_iota((L, L), 0) >= _iota((L, L), 1)).astype(F32)
    a_cs = _dot("nn", tri, a, 1, 3)
    xc = xs * dt
    tril = _iota((L, L), 0) >= _iota((L, L), 1)
    lane_a = _iota((L, 128), 1) < 64
    acs_p = _split(a_cs, 128)
    xc_p = _split(xc, 128)
    ys, new_states = [], []
    for g in range(2):
        cb = _dot("nt", cm[g], bm[g])
        for pp in range(4):
            pair = 4 * g + pp
            acs = acs_p[pair]
            acs_t = acs.T
            xcp = xc_p[pair]
            st = state[pair]
            heads = []
            for hh in range(2):
                col = _take_fn(1, 64 * hh)(acs)
                row = _take_fn(0, 64 * hh)(acs_t)
                seg = col - row
                lm = jnp.where(tril, jnp.exp(jnp.where(tril, seg, 0.0)), 0.0)
                heads.append(_dot("nn", cb * lm, xcp))
            ydiag = jnp.where(lane_a, heads[0], heads[1])
            last = _take_fn(0, L - 1)(acs)
            snew = _dot("tn", xcp * jnp.exp(last - acs), bm[g])
            new_states.append(st * jnp.exp(_take_fn(1, L - 1)(acs_t)) + snew)
            yoff = _dot("nt", cm[g], st) * jnp.exp(acs)
            ys.append(ydiag + yoff)
    y = jnp.concatenate(ys, axis=1) + xs * dsk
    yg = y * _silu(z)
    outs = []
    for v in _split(yg, 512):
        outs.append(v * lax.rsqrt(jnp.mean(v * v, axis=-1, keepdims=True) + EPS))
    return jnp.concatenate(outs, axis=1) * nw, tuple(new_states)


XBC = 1536


def _ssd_conv(ext_ref, cw, cb):
    acc = cb
    for k in range(4):
        acc = acc + cw[k:k + 1, :] * ext_ref[pl.ds(HALO - 3 + k, SSD_CHUNK), :]
    return acc


def _ssd_fwd(z, xbc, dtr, cw, cb, lanes, rides, *, name):
    s = z.shape[0]
    L = SSD_CHUNK
    nc = s // L
    rd = _Rides(rides, [True] * len(rides))

    def body(*refs):
        ins, (y_ref, st_ref), (state, ext), handles = rd.split(refs, 7, 2, 2)
        z_ref, x_ref, h_ref, dtr_ref, cw_ref, cb_ref, ln_ref = ins
        c = pl.program_id(0)
        rd.run(handles, c == 0, c == nc - 1)

        @pl.when(c == 0)
        def _():
            state[...] = jnp.zeros_like(state)

        ext[0:HALO, :] = jnp.where(c == 0, 0.0, h_ref[...])
        ext[HALO:, :] = x_ref[...]
        xpre = _ssd_conv(ext, cw_ref[...], cb_ref[...])
        st_ref[0] = state[...]
        st_in = tuple(state[p] for p in range(8))
        yn, st_out = _ssd_core(z_ref[...], xpre, dtr_ref[...], st_in,
                               ln_ref[0:1, :], ln_ref[1:2, :], ln_ref[2:3, :], ln_ref[3:4, :])
        y_ref[...] = yn.astype(BF16)
        for p in range(8):
            state[p] = st_out[p]

    return pl.pallas_call(
        body, name=name, grid=(nc,),
        in_specs=[pl.BlockSpec((L, D), lambda c: (c, 0)),
                  pl.BlockSpec((L, XBC), lambda c: (c, 0)),
                  pl.BlockSpec((HALO, XBC), lambda c: (jnp.maximum(c * (L // HALO) - 1, 0), 0)),
                  pl.BlockSpec((L, D), lambda c: (c, 0)),
                  pl.BlockSpec((4, XBC), lambda c: (0, 0)),
                  pl.BlockSpec((1, XBC), lambda c: (0, 0)),
                  pl.BlockSpec((8, D), lambda c: (0, 0)), *rd.in_specs],
        out_specs=[pl.BlockSpec((L, D), lambda c: (c, 0)),
                   pl.BlockSpec((1, 8, 128, 128), lambda c: (c, 0, 0, 0)), *rd.out_specs],
        out_shape=[jax.ShapeDtypeStruct((s, D), BF16), jax.ShapeDtypeStruct((nc, 8, 128, 128), F32),
                   *rd.out_shape],
        scratch_shapes=[pltpu.VMEM((8, 128, 128), F32), pltpu.VMEM((L + HALO, XBC), F32), *rd.scratch],
        compiler_params=_cp("arbitrary"),
    )(z, xbc, xbc, dtr, cw, cb, lanes, *rides)


def _ssd_bwd(z, xbc, dtr, states, dymix, cw, cb, lanes, *, name):
    s = z.shape[0]
    L = SSD_CHUNK
    nc = s // L

    def body(z_ref, x_ref, h_ref, dtr_ref, st_ref, dy_ref, cw_ref, cb_ref, ln_ref,
             dz_ref, dx_ref, ddt_ref, dln_ref, dcv_ref, dstate, ext, dext):
        i = pl.program_id(0)
        c = nc - 1 - i

        @pl.when(i == 0)
        def _():
            dstate[...] = jnp.zeros_like(dstate)
            dext[...] = jnp.zeros_like(dext)
            dln_ref[...] = jnp.zeros_like(dln_ref)
            dcv_ref[...] = jnp.zeros_like(dcv_ref)

        ext[0:HALO, :] = jnp.where(c == 0, 0.0, h_ref[...])
        ext[HALO:, :] = x_ref[...]
        cwv = cw_ref[...]
        xpre = _ssd_conv(ext, cwv, cb_ref[...])
        st_in = tuple(st_ref[0, p] for p in range(8))
        _, vjp = jax.vjp(_ssd_core, z_ref[...], xpre, dtr_ref[...], st_in,
                         ln_ref[0:1, :], ln_ref[1:2, :], ln_ref[2:3, :], ln_ref[3:4, :])
        dz, dxpre, ddtr, dst, d0, d1, d2, d3 = vjp((dy_ref[...], tuple(dstate[p] for p in range(8))))
        for p in range(8):
            dstate[p] = dst[p]
        dz_ref[...] = dz.astype(BF16)
        ddt_ref[...] = ddtr.astype(BF16)
        dln_ref[0:4, :] += jnp.concatenate([d0, d1, d2, d3], axis=0)
        dext[0:L, :] = dxpre
        xcur = x_ref[...]
        dx = jnp.zeros((L, XBC), F32)
        rows = []
        for k in range(4):
            shifted = dext[pl.ds(3 - k, L), :]
            dx = dx + cwv[k:k + 1, :] * shifted
            rows.append(jnp.sum(shifted * xcur, axis=0, keepdims=True))
        rows.append(jnp.sum(dxpre, axis=0, keepdims=True))
        dx_ref[...] = dx.astype(BF16)
        dcv_ref[0:5, :] += jnp.concatenate(rows, axis=0)
        dext[L:L + HALO, :] = dxpre[0:HALO, :]

    rev = lambda i: (nc - 1 - i, 0)
    return pl.pallas_call(
        body, name=name, grid=(nc,),
        in_specs=[pl.BlockSpec((L, D), rev),
                  pl.BlockSpec((L, XBC), rev),
                  pl.BlockSpec((HALO, XBC), lambda i: (jnp.maximum((nc - 1 - i) * (L // HALO) - 1, 0), 0)),
                  pl.BlockSpec((L, D), rev),
                  pl.BlockSpec((1, 8, 128, 128), lambda i: (nc - 1 - i, 0, 0, 0)),
                  pl.BlockSpec((L, D), rev),
                  pl.BlockSpec((4, XBC), lambda i: (0, 0)),
                  pl.BlockSpec((1, XBC), lambda i: (0, 0)),
                  pl.BlockSpec((8, D), lambda i: (0, 0))],
        out_specs=[pl.BlockSpec((L, D), rev), pl.BlockSpec((L, XBC), rev), pl.BlockSpec((L, D), rev),
                   pl.BlockSpec((8, D), lambda i: (0, 0)), pl.BlockSpec((8, XBC), lambda i: (0, 0))],
        out_shape=[jax.ShapeDtypeStruct((s, D), BF16), jax.ShapeDtypeStruct((s, XBC), BF16),
                   jax.ShapeDtypeStruct((s, D), BF16), jax.ShapeDtypeStruct((8, D), F32),
                   jax.ShapeDtypeStruct((8, XBC), F32)],
        scratch_shapes=[pltpu.VMEM((8, 128, 128), F32), pltpu.VMEM((L + HALO, XBC), F32),
                        pltpu.VMEM((L + HALO, XBC), F32)],
        compiler_params=_cp("arbitrary"),
    )(z, xbc, xbc, dtr, states, dymix, cw, cb, lanes)


def _mem_attn_math(q, k, v):
    outs = []
    for qh, kh, vh in zip(_split(q, 256), _split(k, 256), _split(v, 256)):
        sc = _dot("nt", qh, kh) * (1.0 / 16.0)
        e = jnp.exp(sc - lax.stop_gradient(jnp.max(sc, axis=-1, keepdims=True)))
        p = e / jnp.sum(e, axis=-1, keepdims=True)
        outs.append(_dot("nn", p, vh))
    return jnp.concatenate(outs, axis=1)


def _mem_attn_fwd(q, k, v, *, name):
    s, m = q.shape[0], k.shape[0]
    tm = _tile(s, 256, 8)

    def body(q_ref, k_ref, v_ref, o_ref):
        o_ref[...] = _mem_attn_math(q_ref[...].astype(F32), k_ref[...].astype(F32),
                                    v_ref[...].astype(F32)).astype(BF16)

    row = pl.BlockSpec((tm, D), lambda i: (i, 0))
    kv = pl.BlockSpec((m, D), lambda i: (0, 0))
    return pl.pallas_call(
        body, name=name, grid=(s // tm,), in_specs=[row, kv, kv], out_specs=row,
        out_shape=jax.ShapeDtypeStruct((s, D), BF16), compiler_params=_cp("parallel"),
    )(q, k, v)


def _mem_attn_bwd(q, k, v, do, *, name):
    s, m = q.shape[0], k.shape[0]
    tm = _tile(s, 256, 8)

    def body(q_ref, k_ref, v_ref, do_ref, dq_ref, dk_ref, dv_ref):
        @pl.when(pl.program_id(0) == 0)
        def _():
            dk_ref[...] = jnp.zeros_like(dk_ref)
            dv_ref[...] = jnp.zeros_like(dv_ref)

        _, vjp = jax.vjp(_mem_attn_math, q_ref[...].astype(F32), k_ref[...].astype(F32),
                         v_ref[...].astype(F32))
        dq, dk, dv = vjp(do_ref[...])
        dq_ref[...] = dq.astype(BF16)
        dk_ref[...] += dk
        dv_ref[...] += dv

    row = pl.BlockSpec((tm, D), lambda i: (i, 0))
    kv = pl.BlockSpec((m, D), lambda i: (0, 0))
    return pl.pallas_call(
        body, name=name, grid=(s // tm,), in_specs=[row, kv, kv, row], out_specs=[row, kv, kv],
        out_shape=[jax.ShapeDtypeStruct((s, D), BF16), jax.ShapeDtypeStruct((m, D), F32),
                   jax.ShapeDtypeStruct((m, D), F32)],
        compiler_params=_cp("arbitrary"),
    )(q, k, v, do)


DFF = 2816
FFN_TC = 1408
FFN_TM = 256


FFN_CHUNKS = tuple((c, min(512, FFN_TC - c)) for c in range(0, FFN_TC, 512))


def _rows8(ref, r, cols):
    return ref[pl.ds(pl.multiple_of(r, HALO), HALO), cols]


def _shift_down(prev, cur, s):
    return jnp.where(_iota(cur.shape, 0) < s, pltpu.roll(prev, s, 0), pltpu.roll(cur, s, 0))


def _shift_up(cur, nxt, s):
    return jnp.where(_iota(cur.shape, 0) >= HALO - s, pltpu.roll(nxt, HALO - s, 0), pltpu.roll(cur, HALO - s, 0))


def _ffn_conv_strip(ext_ref, r, cols, cw, cb):
    prev, cur = _rows8(ext_ref, r, cols), _rows8(ext_ref, r + HALO, cols)
    return cb + cw[0:1, :] * _shift_down(prev, cur, 2) + cw[1:2, :] * _shift_down(prev, cur, 1) + cw[2:3, :] * cur


def _ffn_specs(s):
    tm, tc = FFN_TM, FFN_TC
    blk = pl.BlockSpec((tm, tc), lambda i, j: (i, j))
    halo = pl.BlockSpec((HALO, tc), lambda i, j: (jnp.maximum(i * (tm // HALO) - 1, 0), j))
    cw = pl.BlockSpec((3, tc), lambda i, j: (0, j))
    cb = pl.BlockSpec((1, tc), lambda i, j: (0, j))
    return tm, tc, blk, halo, cw, cb


def _glu_fwd(ug, uv, cwg, cwv, cbg, cbv, *, name):
    s = ug.shape[0]
    tm, tc, blk, halo, cw, cb = _ffn_specs(s)

    def body(g_ref, gh_ref, v_ref, vh_ref, cwg_ref, cwv_ref, cbg_ref, cbv_ref, f_ref, eg, ev):
        first = pl.program_id(0) == 0
        eg[0:HALO, :] = jnp.where(first, 0.0, gh_ref[...])
        eg[HALO:, :] = g_ref[...]
        ev[0:HALO, :] = jnp.where(first, 0.0, vh_ref[...])
        ev[HALO:, :] = v_ref[...]
        cwgv, cwvv, cbgv, cbvv = cwg_ref[...], cwv_ref[...], cbg_ref[...], cbv_ref[...]

        def step(t, carry):
            for c0, w in FFN_CHUNKS:
                cols = slice(c0, c0 + w)
                outs = []
                for h in range(2):
                    r = t * 16 + HALO * h
                    g = _ffn_conv_strip(eg, r, cols, cwgv[:, cols], cbgv[:, cols])
                    v = _ffn_conv_strip(ev, r, cols, cwvv[:, cols], cbvv[:, cols])
                    outs.append(_silu(g) * v)
                f_ref[pl.ds(pl.multiple_of(t * 16, 16), 16), cols] = jnp.concatenate(outs, axis=0).astype(BF16)
            return carry

        lax.fori_loop(0, tm // 16, step, 0)

    return pl.pallas_call(
        body, name=name, grid=(s // tm, DFF // tc),
        in_specs=[blk, halo, blk, halo, cw, cw, cb, cb], out_specs=blk,
        out_shape=jax.ShapeDtypeStruct((s, DFF), BF16),
        scratch_shapes=[pltpu.VMEM((tm + HALO, tc), F32)] * 2,
        compiler_params=_cp("parallel", "parallel"),
    )(ug, ug, uv, uv, cwg, cwv, cbg, cbv)


def _ffn_bwd(ug, uv, df, cwg, cwv, cbg, cbv, *, name):
    s = ug.shape[0]
    tm, tc = FFN_TM, FFN_TC
    nb = s // tm
    rows_ext = tm + HALO

    def body(g_ref, gp_ref, gn_ref, v_ref, vp_ref, vn_ref, df_ref, dfn_ref, cwg_ref, cwv_ref, cbg_ref, cbv_ref,
             dxg_ref, dxv_ref, dcg_ref, dcv_ref, eg, ev, edf, edg, edv, accg, accv):
        i = pl.program_id(1)
        first, last = i == 0, i == nb - 1

        @pl.when(first)
        def _():
            dcg_ref[...] = jnp.zeros_like(dcg_ref)
            dcv_ref[...] = jnp.zeros_like(dcv_ref)

        for e, prev, main, nxt in ((eg, gp_ref, g_ref, gn_ref), (ev, vp_ref, v_ref, vn_ref)):
            e[0:HALO, :] = jnp.where(first, 0.0, prev[...])
            e[HALO:HALO + tm, :] = main[...]
            e[HALO + tm:, :] = jnp.where(last, 0.0, nxt[...])
        edf[0:tm, :] = df_ref[...]
        edf[tm:, :] = jnp.where(last, 0.0, dfn_ref[...])
        accg[...] = jnp.zeros_like(accg)
        accv[...] = jnp.zeros_like(accv)
        cwgv, cwvv, cbgv, cbvv = cwg_ref[...], cwv_ref[...], cbg_ref[...], cbv_ref[...]

        def cotangents(t, carry):
            r = t * HALO
            for c0, w in FFN_CHUNKS:
                cols = slice(c0, c0 + w)
                g = _ffn_conv_strip(eg, r, cols, cwgv[:, cols], cbgv[:, cols])
                v = _ffn_conv_strip(ev, r, cols, cwvv[:, cols], cbvv[:, cols])
                dfs = _rows8(edf, r, cols)
                sg = _sigmoid(g)
                edv[pl.ds(pl.multiple_of(r, HALO), HALO), cols] = dfs * g * sg
                edg[pl.ds(pl.multiple_of(r, HALO), HALO), cols] = dfs * v * sg * (1.0 + g * (1.0 - sg))
            return carry

        lax.fori_loop(0, rows_ext // HALO, cotangents, 0)

        def conv_backward(t, carry):
            for c0, w in FFN_CHUNKS:
                cols = slice(c0, c0 + w)
                for edu, e, cw, dx_ref, acc in ((edg, eg, cwgv[:, cols], dxg_ref, accg),
                                                (edv, ev, cwvv[:, cols], dxv_ref, accv)):
                    dxs = []
                    for h in range(2):
                        r = t * 16 + HALO * h
                        cur, nxt = _rows8(edu, r, cols), _rows8(edu, r + HALO, cols)
                        up1, up2 = _shift_up(cur, nxt, 1), _shift_up(cur, nxt, 2)
                        x = _rows8(e, r + HALO, cols)
                        dxs.append(cw[2:3, :] * cur + cw[1:2, :] * up1 + cw[0:1, :] * up2)
                        acc[0, :, cols] += up2 * x
                        acc[1, :, cols] += up1 * x
                        acc[2, :, cols] += cur * x
                        acc[3, :, cols] += cur
                    dx_ref[pl.ds(pl.multiple_of(t * 16, 16), 16), cols] = jnp.concatenate(dxs, axis=0).astype(BF16)
            return carry

        lax.fori_loop(0, tm // 16, conv_backward, 0)
        for acc, dc_ref in ((accg, dcg_ref), (accv, dcv_ref)):
            dc_ref[0:4, :] += jnp.concatenate([jnp.sum(acc[k], axis=0, keepdims=True) for k in range(4)], axis=0)

    blk = pl.BlockSpec((tm, tc), lambda j, i: (i, j))
    nxt = pl.BlockSpec((HALO, tc), lambda j, i: (jnp.minimum((i + 1) * (tm // HALO), s // HALO - 1), j))
    prv = pl.BlockSpec((HALO, tc), lambda j, i: (jnp.maximum(i * (tm // HALO) - 1, 0), j))
    cw = pl.BlockSpec((3, tc), lambda j, i: (0, j))
    cb = pl.BlockSpec((1, tc), lambda j, i: (0, j))
    acc = pl.BlockSpec((8, tc), lambda j, i: (0, j))
    return pl.pallas_call(
        body, name=name, grid=(DFF // tc, nb),
        in_specs=[blk, prv, nxt, blk, prv, nxt, blk, nxt, cw, cw, cb, cb],
        out_specs=[blk, blk, acc, acc],
        out_shape=[jax.ShapeDtypeStruct((s, DFF), BF16)] * 2 + [jax.ShapeDtypeStruct((8, DFF), F32)] * 2,
        scratch_shapes=[pltpu.VMEM((tm + 2 * HALO, tc), F32)] * 2 + [pltpu.VMEM((rows_ext, tc), F32)] * 3
                       + [pltpu.VMEM((4, HALO, tc), F32)] * 2,
        compiler_params=_cp("parallel", "arbitrary"),
    )(ug, ug, ug, uv, uv, uv, df, df, cwg, cwv, cbg, cbv)


MESH = pl.DeviceIdType.MESH


def _all_gather(arrs, *, name):
    n = len(arrs)

    def body(*refs):
        x_refs, out_refs = refs[:n], refs[n:2 * n]
        send_sems, recv_sems, local_sems = refs[2 * n:]
        x, y, c = lax.axis_index("x"), lax.axis_index("y"), lax.axis_index("c")
        me, sibling = (x, y, c), (x, y, 1 - c)
        chips = [(1 - x, y), (x, 1 - y), (1 - x, 1 - y)]

        def blk(a, dev):
            return out_refs[a].at[4 * dev[0] + 2 * dev[1] + dev[2]]

        def copy(a, k, block, to, src=None):
            return pltpu.make_async_remote_copy(
                src_ref=blk(a, block) if src is None else src, dst_ref=blk(a, block),
                send_sem=send_sems.at[7 * a + k], recv_sem=recv_sems.at[7 * a + k],
                device_id=to, device_id_type=MESH)

        started = []
        mine = []
        for a in range(n):
            cp = pltpu.make_async_copy(x_refs[a], blk(a, me), local_sems.at[a])
            cp.start()
            mine.append(cp)
            first = [copy(a, 0, me, sibling, src=x_refs[a])]
            first += [copy(a, 1 + j, me, (*chip, c), src=x_refs[a]) for j, chip in enumerate(chips)]
            for cp in first:
                cp.start()
            started += first
        for a in range(n):
            for j, chip in enumerate(chips):
                copy(a, 1 + j, (*chip, c), me).wait_recv()
                fwd = copy(a, 4 + j, (*chip, c), sibling)
                fwd.start()
                started.append(fwd)
        for a in range(n):
            copy(a, 0, sibling, me).wait_recv()
            for j, chip in enumerate(chips):
                copy(a, 4 + j, (*chip, 1 - c), me).wait_recv()
        for cp in started:
            cp.wait_send()
        for cp in mine:
            cp.wait()

    any_spec = pl.BlockSpec(memory_space=pl.ANY)
    return pl.pallas_call(
        body, name=name,
        in_specs=[any_spec] * n, out_specs=[any_spec] * n,
        out_shape=[jax.ShapeDtypeStruct((NDEV,) + a.shape, a.dtype) for a in arrs],
        scratch_shapes=[pltpu.SemaphoreType.DMA((7 * n,)), pltpu.SemaphoreType.DMA((7 * n,)),
                        pltpu.SemaphoreType.DMA((n,))],
    )(*arrs)


class _Direct:
    SEMS = (pltpu.SemaphoreType.DMA((7,)), pltpu.SemaphoreType.DMA((7,)), pltpu.SemaphoreType.DMA((1,)))

    def __init__(self, src_ref, recv_ref, sems, gather):
        x, y, c = lax.axis_index("x"), lax.axis_index("y"), lax.axis_index("c")
        me = 4 * x + 2 * y + c
        send_sems, recv_sems, local_sem = sems
        src = (lambda pid: src_ref) if gather else (lambda pid: src_ref.at[pid])
        self.mine = pltpu.make_async_copy(src(me), recv_ref.at[me], local_sem.at[0])
        self.copies = []
        for k in range(1, NDEV):
            px = 1 - x if k & 4 else x
            py = 1 - y if k & 2 else y
            pc = 1 - c if k & 1 else c
            self.copies.append(pltpu.make_async_remote_copy(
                src_ref=src(4 * px + 2 * py + pc), dst_ref=recv_ref.at[me],
                send_sem=send_sems.at[k - 1], recv_sem=recv_sems.at[k - 1],
                device_id=(px, py, pc), device_id_type=MESH))

    def start(self):
        self.mine.start()
        for cp in self.copies:
            cp.start()

    def wait(self):
        for cp in self.copies:
            cp.wait_recv()
        for cp in self.copies:
            cp.wait_send()
        self.mine.wait()


def _recv_shape(src, gather):
    return jax.ShapeDtypeStruct(((NDEV,) + src.shape) if gather else src.shape, src.dtype)


class _Rides:
    def __init__(self, rides, gathers):
        self.n = len(rides)
        self.gathers = list(gathers)
        any_spec = pl.BlockSpec(memory_space=pl.ANY)
        self.in_specs = [any_spec] * self.n
        self.out_specs = [any_spec] * self.n
        self.out_shape = [_recv_shape(a, g) for a, g in zip(rides, gathers)]
        self.scratch = list(_Direct.SEMS) * self.n

    def split(self, refs, n_in, n_out, n_scratch):
        n = self.n
        ins, refs = refs[:n_in], refs[n_in:]
        rides, refs = refs[:n], refs[n:]
        outs, refs = refs[:n_out], refs[n_out:]
        gots, refs = refs[:n], refs[n:]
        scratch, sems = refs[:n_scratch], refs[n_scratch:]
        return ins, outs, scratch, (rides, gots, sems)

    def run(self, handles, first, last):
        rides, gots, sems = handles

        def all_of():
            return [_Direct(rides[a], gots[a], sems[3 * a:3 * a + 3], self.gathers[a]) for a in range(self.n)]

        @pl.when(first)
        def _():
            for e in all_of():
                e.start()

        @pl.when(last)
        def _():
            for e in all_of():
                e.wait()


def _exchange(arrs, gathers, *, name):
    rd = _Rides(arrs, gathers)

    def body(*refs):
        _, _, _, handles = rd.split(refs, 0, 0, 0)
        rd.run(handles, True, True)

    return pl.pallas_call(
        body, name=name, in_specs=rd.in_specs, out_specs=rd.out_specs, out_shape=rd.out_shape,
        scratch_shapes=rd.scratch,
    )(*arrs)


def _adamw(parts, w, m, v, *, name):
    r, cols = w.shape
    tm = _tile(r, 256, PACK_ALIGN)
    c1 = 1.0 - ADAM_B1 ** ADAM_STEP
    c2 = 1.0 - ADAM_B2 ** ADAM_STEP

    def body(p_ref, w_ref, m_ref, v_ref, g_ref, d_ref, nm_ref, nv_ref):
        g = p_ref[0].astype(F32)
        for i in range(1, NDEV):
            g = g + p_ref[i].astype(F32)
        nm = ADAM_B1 * m_ref[...] + (1.0 - ADAM_B1) * g
        nv = ADAM_B2 * v_ref[...] + (1.0 - ADAM_B2) * (g * g)
        d_ref[...] = -ADAM_LR * ((nm / c1) / (jnp.sqrt(nv / c2) + ADAM_EPS) + ADAM_WD * w_ref[...])
        g_ref[...] = g
        nm_ref[...] = nm
        nv_ref[...] = nv

    row = pl.BlockSpec((tm, cols), lambda i: (i, 0))
    return pl.pallas_call(
        body, name=name, grid=(r // tm,),
        in_specs=[pl.BlockSpec((NDEV, tm, cols), lambda i: (0, i, 0)), row, row, row],
        out_specs=[row] * 4, out_shape=[jax.ShapeDtypeStruct((r, cols), F32)] * 4,
        compiler_params=_cp("parallel"),
    )(parts, w, m, v)


PACK_ALIGN = 16


def _part_rows(shape):
    n = -(-math.prod(shape) // D)
    return n + (-n) % PACK_ALIGN


def _rows(a):
    flat = a.reshape(-1)
    pad = _part_rows(a.shape) * D - flat.shape[0]
    if pad:
        flat = jnp.concatenate([flat, jnp.zeros((pad,), flat.dtype)])
    return flat.reshape(-1, D)


def _pack(parts, total_rows):
    if all(math.prod(p.shape) % (PACK_ALIGN * D) for p in parts):
        return _pack_small(parts, total_rows)
    rows = [_rows(p) for p in parts]
    used = sum(r.shape[0] for r in rows)
    if total_rows > used:
        rows.append(jnp.zeros((total_rows - used, D), rows[0].dtype))
    return jnp.concatenate(rows, axis=0)


def _pack_small(parts, total_rows):
    flat, used = [], 0
    for p in parts:
        n, nr = math.prod(p.shape), _part_rows(p.shape)
        flat += [p.reshape(-1), jnp.zeros((nr * D - n,), p.dtype)]
        used += nr
    flat.append(jnp.zeros(((total_rows - used) * D,), parts[0].dtype))
    return jnp.concatenate(flat).reshape(total_rows, D)


def _unpack(buf, shapes, part_rows=_part_rows):
    out, r0 = [], 0
    for shp in shapes:
        n = math.prod(shp)
        out.append(buf[r0:r0 + part_rows(shp)].reshape(-1)[:n].reshape(shp))
        r0 += part_rows(shp)
    return out


def _tight_rows(shape):
    return -(-math.prod(shape) // D)


def _pack_tight(parts, total_rows):
    flat, used = [], 0
    for p in parts:
        n, nr = math.prod(p.shape), _tight_rows(p.shape)
        flat += [p.reshape(-1), jnp.zeros((nr * D - n,), p.dtype)]
        used += nr
    flat.append(jnp.zeros(((total_rows - used) * D,), parts[0].dtype))
    return jnp.concatenate(flat).reshape(total_rows, D)


SHARD = {"w_in": (D, 706), "w_out": (256, D), "w_mq": (128, D), "w_mk": (128, D), "w_mv": (128, D),
         "w_mo": (128, D), "w_up": (D, 704), "w_down": (352, D), "conv_ssd_w": (4, 192), "conv_ffn_w": (3, 704)}
GATHER_MID = ["w_out", "w_mq", "w_mk", "w_mv", "w_mo"]
GATHER_FFN = ["w_down"]
CONV_TAPS = ["conv_ssd_w", "conv_ffn_w"]
GRADS_PACKED = ["w_out", "w_mq", "w_mk", "w_mv", "w_mo", "w_down", "conv_ffn_w", "conv_ssd_w"]


def _layout(names):
    row0, r = {}, 0
    for n in names:
        row0[n] = r
        r += _part_rows(SHARD[n])
    return row0, r + (-r) % 128


SMALL = [("norm_mix_w", (1, D)), ("conv_ssd_b", (1, 1536)), ("dt_bias", (1, 16)), ("a_log", (1, 16)),
         ("d_skip", (1, 16)), ("ssd_norm_w", (1, D)), ("sb_norm_w", (1, D)), ("norm_mem_w", (1, D)),
         ("norm_memkv_w", (1, D)), ("norm_ffn_w", (1, D)), ("conv_ffn_b", (1, 5632)), ("norm_final_w", (D,))]
LOSS_ROW = sum(_tight_rows(_shp) for _, _shp in SMALL)
SMALL_ROWS = LOSS_ROW + 1 + (-(LOSS_ROW + 1)) % 8
ORDER = ["norm_mix_w", "w_in", "conv_ssd_w", "conv_ssd_b", "dt_bias", "a_log", "d_skip", "ssd_norm_w",
         "sb_norm_w", "w_out", "norm_mem_w", "norm_memkv_w", "w_mq", "w_mk", "w_mv", "w_mo", "norm_ffn_w",
         "w_up", "conv_ffn_w", "conv_ffn_b", "w_down", "norm_final_w"]


def _pad_rows(a, nr):
    n = a.shape[1]
    return jnp.concatenate([a, jnp.zeros((NDEV, nr * D - n), a.dtype)], axis=1).reshape(NDEV, nr, D)


def _group_sum(lanes):
    return lanes.reshape(16, 64).sum(axis=1).reshape(1, 16)


def kernel(x, mem, norm_mix_w, w_in, conv_ssd_w, conv_ssd_b, dt_bias, a_log, d_skip, ssd_norm_w, sb_norm_w, w_out, norm_mem_w, norm_memkv_w, w_mq, w_mk, w_mv, w_mo, norm_ffn_w, w_up, conv_ffn_w, conv_ffn_b, w_down, norm_final_w, loss_target, m_norm_mix_w, m_w_in, m_conv_ssd_w, m_conv_ssd_b, m_dt_bias, m_a_log, m_d_skip, m_ssd_norm_w, m_sb_norm_w, m_w_out, m_norm_mem_w, m_norm_memkv_w, m_w_mq, m_w_mk, m_w_mv, m_w_mo, m_norm_ffn_w, m_w_up, m_conv_ffn_w, m_conv_ffn_b, m_w_down, m_norm_final_w, v_norm_mix_w, v_w_in, v_conv_ssd_w, v_conv_ssd_b, v_dt_bias, v_a_log, v_d_skip, v_ssd_norm_w, v_sb_norm_w, v_w_out, v_norm_mem_w, v_norm_memkv_w, v_w_mq, v_w_mk, v_w_mv, v_w_mo, v_norm_ffn_w, v_w_up, v_conv_ffn_w, v_conv_ffn_b, v_w_down, v_norm_final_w):
    P = dict(norm_mix_w=norm_mix_w, w_in=w_in, conv_ssd_w=conv_ssd_w, conv_ssd_b=conv_ssd_b, dt_bias=dt_bias, a_log=a_log, d_skip=d_skip, ssd_norm_w=ssd_norm_w, sb_norm_w=sb_norm_w, w_out=w_out, norm_mem_w=norm_mem_w, norm_memkv_w=norm_memkv_w, w_mq=w_mq, w_mk=w_mk, w_mv=w_mv, w_mo=w_mo, norm_ffn_w=norm_ffn_w, w_up=w_up, conv_ffn_w=conv_ffn_w, conv_ffn_b=conv_ffn_b, w_down=w_down, norm_final_w=norm_final_w)
    M = dict(norm_mix_w=m_norm_mix_w, w_in=m_w_in, conv_ssd_w=m_conv_ssd_w, conv_ssd_b=m_conv_ssd_b, dt_bias=m_dt_bias, a_log=m_a_log, d_skip=m_d_skip, ssd_norm_w=m_ssd_norm_w, sb_norm_w=m_sb_norm_w, w_out=m_w_out, norm_mem_w=m_norm_mem_w, norm_memkv_w=m_norm_memkv_w, w_mq=m_w_mq, w_mk=m_w_mk, w_mv=m_w_mv, w_mo=m_w_mo, norm_ffn_w=m_norm_ffn_w, w_up=m_w_up, conv_ffn_w=m_conv_ffn_w, conv_ffn_b=m_conv_ffn_b, w_down=m_w_down, norm_final_w=m_norm_final_w)
    V = dict(norm_mix_w=v_norm_mix_w, w_in=v_w_in, conv_ssd_w=v_conv_ssd_w, conv_ssd_b=v_conv_ssd_b, dt_bias=v_dt_bias, a_log=v_a_log, d_skip=v_d_skip, ssd_norm_w=v_ssd_norm_w, sb_norm_w=v_sb_norm_w, w_out=v_w_out, norm_mem_w=v_norm_mem_w, norm_memkv_w=v_norm_memkv_w, w_mq=v_w_mq, w_mk=v_w_mk, w_mv=v_w_mv, w_mo=v_w_mo, norm_ffn_w=v_norm_ffn_w, w_up=v_w_up, conv_ffn_w=v_conv_ffn_w, conv_ffn_b=v_conv_ffn_b, w_down=v_w_down, norm_final_w=v_norm_final_w)
    small_shapes = [shp for _, shp in SMALL]

    def packed(src, names, dtype=F32):
        return _pack([src[n][0] for n in names], _layout(names)[1]).astype(dtype)

    def columns(g):
        return g.transpose(1, 0, 2).reshape(g.shape[1], NDEV * g.shape[2])

    g_in, g_taps = _all_gather([w_in[0].astype(BF16), packed(P, CONV_TAPS)], name="gather_w_in")
    W_in = columns(g_in)
    cw_ssd = g_taps[:, 0].reshape(NDEV, -1)[:, :768].reshape(NDEV, 4, 192).transpose(1, 0, 2).reshape(4, XBC)
    cw_ffn = (g_taps[:, PACK_ALIGN:PACK_ALIGN + 3].reshape(NDEV, -1)[:, :2112].reshape(NDEV, 3, 704)
              .transpose(1, 0, 2).reshape(3, 2 * DFF))
    W_z, W_xbc, W_dt, W_qkv = W_in[:, :D], W_in[:, D:D + XBC], W_in[:, D + XBC:D + XBC + 16], W_in[:, D + XBC + 16:]
    W_dtr = jnp.repeat(W_dt, 64, axis=1)
    cwg, cwv = cw_ffn[:, :DFF], cw_ffn[:, DFF:]
    cbg, cbv = conv_ffn_b[:, :DFF], conv_ffn_b[:, DFF:]
    rep = lambda p: jnp.repeat(p, 64, axis=1)
    lanes = jnp.concatenate([rep(dt_bias), rep(a_log), rep(d_skip), ssd_norm_w, jnp.zeros((4, D), F32)], axis=0)

    xs, tgt, mm = x[0], loss_target[0], mem[0]

    h1 = _norm_fwd(xs, norm_mix_w, name="norm_mix")
    z = _mm(h1, W_z, name="proj_z")
    xbc = _mm(h1, W_xbc, name="proj_xbc")
    dtr = _mm(h1, W_dtr, name="proj_dt")
    qkv = _mm(h1, W_qkv, name="proj_qkv", out_dtype=BF16)
    w_up16 = w_up[0].astype(BF16)
    y_ssd, states, g_ffn, g_up0 = _ssd_fwd(z, xbc, dtr, cw_ssd, conv_ssd_b, lanes,
                                           [packed(P, GATHER_FFN, BF16), w_up16[:D // 2]], name="ssd_fwd")
    o_sb, g_mid, g_up1 = _sb_fwd(qkv, [packed(P, GATHER_MID, BF16), w_up16[D // 2:]], name="sb_fwd")
    g_up = jnp.concatenate([g_up0, g_up1], axis=1)
    r_mid = _layout(GATHER_MID)[0]
    W_out = g_mid[:, r_mid["w_out"]:r_mid["w_out"] + 256].reshape(2 * D, D)
    W_mq, W_mk, W_mv, W_mo = [g_mid[:, r_mid[n]:r_mid[n] + 128].reshape(D, D)
                              for n in ("w_mq", "w_mk", "w_mv", "w_mo")]
    W_up = columns(g_up)
    W_down = g_ffn[:, 0:352].reshape(DFF, D)
    W_upg, W_upv = W_up[:, :DFF], W_up[:, DFF:]
    y_sb = _head_norm_fwd(o_sb, sb_norm_w, name="sb_norm")
    ymix = jnp.concatenate([y_ssd, y_sb], axis=1)
    x1 = _mm(ymix, W_out, add=xs, name="proj_out")
    h2 = _norm_fwd(x1, norm_mem_w, name="norm_mem")
    mn = _norm_fwd(mm, norm_memkv_w, name="norm_memkv")
    qm = _mm(h2, W_mq, name="mem_q", out_dtype=BF16)
    km = _mm(mn, W_mk, name="mem_k", out_dtype=BF16)
    vm = _mm(mn, W_mv, name="mem_v", out_dtype=BF16)
    om = _mem_attn_fwd(qm, km, vm, name="mem_attn")
    x2 = _mm(om, W_mo, add=x1, name="mem_o")
    h3 = _norm_fwd(x2, norm_ffn_w, name="norm_ffn")
    ug = _mm(h3, W_upg, name="ffn_up_g")
    uv = _mm(h3, W_upv, name="ffn_up_v")
    f = _glu_fwd(ug, uv, cwg, cwv, cbg, cbv, name="ffn_glu")
    x3 = _mm(f, W_down, add=x2, name="ffn_down")
    dx3, g_nfinal, loss_part = _final(x3, norm_final_w.reshape(1, D), tgt, name="final_loss")

    G = {}
    G["w_down"] = _mm(f, dx3, trans_a=True, name="g_w_down")
    df = _mm(dx3, W_down, trans_b=True, name="d_f")
    dupg, dupv, dcg, dcv = _ffn_bwd(ug, uv, df, cwg, cwv, cbg, cbv, name="ffn_glu_bwd")
    G["w_up"] = jnp.concatenate([_mm(h3, dupg, trans_a=True, name="g_w_up_g"),
                                 _mm(h3, dupv, trans_a=True, name="g_w_up_v")], axis=1)
    G["conv_ffn_w"] = jnp.concatenate([dcg[0:3], dcv[0:3]], axis=1)
    G["conv_ffn_b"] = jnp.concatenate([dcg[3:4], dcv[3:4]], axis=1)
    dh3 = _mm(dupg, W_upg, trans_b=True, name="d_h3_g")
    dh3 = _mm(dupv, W_upv, trans_b=True, add=dh3, name="d_h3_v")
    dx2, G["norm_ffn_w"] = _norm_bwd(x2, norm_ffn_w, dh3, dx3, name="norm_ffn_bwd")
    G["w_mo"] = _mm(om, dx2, trans_a=True, name="g_w_mo")
    dom = _mm(dx2, W_mo, trans_b=True, name="d_om")
    dqm, dkm, dvm = _mem_attn_bwd(qm, km, vm, dom, name="mem_attn_bwd")
    G["w_mq"] = _mm(h2, dqm, trans_a=True, name="g_w_mq")
    G["w_mk"] = _mm(mn, dkm, trans_a=True, name="g_w_mk")
    G["w_mv"] = _mm(mn, dvm, trans_a=True, name="g_w_mv")
    dh2 = _mm(dqm, W_mq, trans_b=True, name="d_h2")
    dmn = _mm(dkm, W_mk, trans_b=True, name="d_mn_k")
    dmn = _mm(dvm, W_mv, trans_b=True, add=dmn, name="d_mn_v")
    _, G["norm_memkv_w"] = _norm_bwd(mm, norm_memkv_w, dmn, None, name="norm_memkv_bwd")
    dx1, G["norm_mem_w"] = _norm_bwd(x1, norm_mem_w, dh2, dx2, name="norm_mem_bwd")
    G["w_out"] = _mm(ymix, dx1, trans_a=True, name="g_w_out")
    dymix = _mm(dx1, W_out, trans_b=True, name="d_ymix")
    do_sb, G["sb_norm_w"] = _head_norm_bwd(o_sb, sb_norm_w, dymix, name="sb_norm_bwd")

    dz, dxbc, ddtr, dlanes, dconv = _ssd_bwd(z, xbc, dtr, states, dymix, cw_ssd, conv_ssd_b, lanes, name="ssd_bwd")
    G["dt_bias"], G["a_log"], G["d_skip"] = [_group_sum(dlanes[i:i + 1]) for i in range(3)]
    G["ssd_norm_w"] = dlanes[3:4]
    G["conv_ssd_w"], G["conv_ssd_b"] = dconv[0:4], dconv[4:5]

    def col_slabs(g, cols):
        return g.reshape(g.shape[0], NDEV, cols).transpose(1, 0, 2).astype(BF16)

    def packed_slabs(names):
        parts = []
        for n in names:
            shp = SHARD[n]
            if shp[-1] == D:
                t = G[n].reshape((NDEV,) + shp)
            else:
                t = _pad_rows(G[n].reshape(shp[0], NDEV, shp[1]).transpose(1, 0, 2).reshape(NDEV, -1),
                              _part_rows(shp))
            parts.append(jnp.pad(t, ((0, 0), (0, _part_rows(shp) - t.shape[1]), (0, 0))))
        used = sum(t.shape[1] for t in parts)
        parts.append(jnp.zeros((NDEV, _layout(names)[1] - used, D), F32))
        return jnp.concatenate(parts, axis=1).astype(BF16)

    dq, dk, dv, recv_packed, recv_up = _sb_bwd(qkv, o_sb, do_sb, [packed_slabs(GRADS_PACKED), col_slabs(G["w_up"], 704)],
                                               name="sb_bwd")
    dproj = jnp.concatenate([dz, dxbc, ddtr, dq, dk, dv], axis=1)
    g_proj = _mm(h1, dproj, trans_a=True, name="g_w_in")
    c_dt = D + XBC
    G["w_in"] = jnp.concatenate([g_proj[:, :c_dt], g_proj[:, c_dt:c_dt + D].reshape(D, 16, 64).sum(axis=2),
                                 g_proj[:, c_dt + D:]], axis=1)
    W_proj = jnp.concatenate([W_z, W_xbc, W_dtr, W_qkv], axis=1)
    dh1, recv_in = _mm(dproj, W_proj, trans_b=True, rides=[col_slabs(G["w_in"], 706)], name="d_h1")
    dx, G["norm_mix_w"] = _norm_bwd(xs, norm_mix_w, dh1, dx1, name="norm_mix_bwd")
    G["norm_final_w"] = g_nfinal.reshape(D)

    small_g = _pack_tight([G[n] for n, _ in SMALL] + [loss_part], SMALL_ROWS)
    (parts_small,) = _exchange([small_g], [True], name="exchange_grads")
    outs_packed = _adamw(recv_packed, packed(P, GRADS_PACKED), packed(M, GRADS_PACKED), packed(V, GRADS_PACKED),
                         name="adamw_packed")
    outs_up = _adamw(recv_up, w_up[0], m_w_up[0], v_w_up[0], name="adamw_w_up")
    outs_in = _adamw(recv_in, w_in[0], m_w_in[0], v_w_in[0], name="adamw_w_in")
    outs_small = _adamw(parts_small, _pack_tight([P[n] for n, _ in SMALL], SMALL_ROWS),
                        _pack_tight([M[n] for n, _ in SMALL], SMALL_ROWS),
                        _pack_tight([V[n] for n, _ in SMALL], SMALL_ROWS), name="adamw_replicated")

    res = {}
    for i, kind in enumerate(("grad", "delta", "new_m", "new_v")):
        for n, val in zip(GRADS_PACKED, _unpack(outs_packed[i], [SHARD[n] for n in GRADS_PACKED])):
            res[kind, n] = val.reshape((1,) + SHARD[n])
        res[kind, "w_up"] = outs_up[i].reshape((1,) + SHARD["w_up"])
        res[kind, "w_in"] = outs_in[i].reshape((1,) + SHARD["w_in"])
        for (n, shp), val in zip(SMALL, _unpack(outs_small[i], small_shapes, _tight_rows)):
            res[kind, n] = val
    loss = outs_small[0][LOSS_ROW, 0]
    out = [loss, dx.reshape(1, -1, D)]
    for kind in ("grad", "delta", "new_m", "new_v"):
        out += [res[kind, n] for n in ORDER]
    return tuple(out)
```

```python
import functools
import math

import jax
import jax.numpy as jnp
from jax import lax
from jax.experimental import pallas as pl
from jax.experimental.pallas import tpu as pltpu

F32 = jnp.float32
BF16 = jnp.bfloat16

D = 1024
NDEV = 8
EPS = 1e-6
SSD_CHUNK = 128
HALO = 8
VMEM_LIMIT = 56 * 2**20

ADAM_LR, ADAM_B1, ADAM_B2, ADAM_EPS, ADAM_WD, ADAM_STEP = 0.001, 0.9, 0.999, 1e-08, 0.01, 10


def _cp(*sem):
    return pltpu.CompilerParams(dimension_semantics=sem, vmem_limit_bytes=VMEM_LIMIT)


def _tile(n, cap, mult):
    if n <= cap:
        return n
    for d in range(cap - cap % mult, 0, -mult):
        if n % d == 0:
            return d
    raise ValueError(f"no tile for {n}")


def _sigmoid(x):
    return 1.0 / (1.0 + jnp.exp(-x))


def _silu(x):
    return x * _sigmoid(x)


def _softplus(x):
    return jnp.maximum(x, 0.0) + jnp.log(1.0 + jnp.exp(-jnp.abs(x)))


def _terms(x, n):
    out = []
    r = x.astype(F32)
    for i in range(n):
        h = r.astype(BF16)
        out.append(h)
        if i + 1 < n:
            r = r - h.astype(F32)
    return out


_DIMS = {"nn": ((1,), (0,)), "nt": ((1,), (1,)), "tn": ((0,), (0,))}


def _dot_raw(form, a, b, ta, tb):
    acc = None
    for ai in _terms(a, ta):
        for bi in _terms(b, tb):
            d = lax.dot_general(ai, bi, (_DIMS[form], ((), ())), preferred_element_type=F32)
            acc = d if acc is None else acc + d
    return acc


@functools.lru_cache(maxsize=None)
def _dot_fn(form, ta, tb):
    @jax.custom_vjp
    def f(a, b):
        return _dot_raw(form, a, b, ta, tb)

    def fwd(a, b):
        return f(a, b), (a, b)

    def bwd(res, ct):
        a, b = res
        if form == "nn":
            return _dot_fn("nt", ta, tb)(ct, b), _dot_fn("tn", ta, tb)(a, ct)
        if form == "nt":
            return _dot_fn("nn", ta, tb)(ct, b), _dot_fn("tn", tb, ta)(ct, a)
        return _dot_fn("nt", tb, ta)(b, ct), _dot_fn("nn", ta, tb)(a, ct)

    f.defvjp(fwd, bwd)
    return f


def _dot(form, a, b, ta=1, tb=1):
    return _dot_fn(form, ta, tb)(a, b)


@functools.lru_cache(maxsize=None)
def _take_fn(axis, idx):
    @jax.custom_vjp
    def f(x):
        return x[:, idx:idx + 1] if axis == 1 else x[idx:idx + 1, :]

    def fwd(x):
        return f(x), x.shape

    def bwd(shape, ct):
        io = lax.broadcasted_iota(jnp.int32, shape, axis)
        return (jnp.where(io == idx, jnp.broadcast_to(ct, shape), 0.0),)

    f.defvjp(fwd, bwd)
    return f


@functools.lru_cache(maxsize=None)
def _split_fn(width, n):
    @jax.custom_vjp
    def f(x):
        return tuple(x[:, i * width:(i + 1) * width] for i in range(n))

    def fwd(x):
        return f(x), None

    def bwd(_, cts):
        return (jnp.concatenate(list(cts), axis=1),)

    f.defvjp(fwd, bwd)
    return f


def _split(x, width):
    return _split_fn(width, x.shape[1] // width)(x)


def _iota(shape, axis):
    return lax.broadcasted_iota(jnp.int32, shape, axis)


MM_VMEM_BUDGET = 44 * 2**20


def _mm_tiles(m, n, kt, trans_a, a_bytes, b_bytes, out_bytes, add_bytes):
    tn = _tile(n, 1536, 128)
    for tm_cap in (1408, 1024, 512, 256, 128):
        tm = _tile(m, tm_cap, 128 if trans_a else 8)
        for tk_cap in (kt, 4096, 2048, 1024, 512):
            tk = _tile(kt, tk_cap, 128)
            blocks = tm * tk * a_bytes + tk * tn * b_bytes + tm * tn * (out_bytes + add_bytes)
            if 2 * blocks + (tm * tn * 4 if tk < kt else 0) <= MM_VMEM_BUDGET:
                return tm, tn, tk
    raise ValueError(f"no matmul tiling for {(m, n, kt)}")


def _mm(a, b, *, name, add=None, trans_a=False, trans_b=False, out_dtype=F32, rides=()):
    assert not (trans_a and trans_b)
    if trans_a:
        kt, m = a.shape
    else:
        m, kt = a.shape
    n, kt2 = b.shape if trans_b else b.shape[::-1]
    assert kt == kt2, (a.shape, b.shape)
    tm, tn, tk = _mm_tiles(m, n, kt, trans_a, a.dtype.itemsize, b.dtype.itemsize,
                           jnp.dtype(out_dtype).itemsize, 0 if add is None else add.dtype.itemsize)
    nk = kt // tk
    grid = (m // tm, n // tn, nk)
    rd = _Rides(rides, [False] * len(rides))
    n_in = 2 if add is None else 3

    def body(*all_refs):
        ins, (o_ref,), scratch, handles = rd.split(all_refs, n_in, 1, 1 if nk > 1 else 0)
        refs = (*ins, o_ref, *scratch)
        if rides:
            ids = [pl.program_id(ax) for ax in range(3)]
            rd.run(handles, (ids[0] == 0) & (ids[1] == 0) & (ids[2] == 0),
                   (ids[0] == grid[0] - 1) & (ids[1] == grid[1] - 1) & (ids[2] == grid[2] - 1))
        if add is None:
            a_ref, b_ref, o_ref = refs[:3]
        else:
            a_ref, b_ref, add_ref, o_ref = refs[:4]
        k = pl.program_id(2)
        av = a_ref[...].astype(BF16)
        bv = b_ref[...].astype(BF16)
        dims = _DIMS["tn" if trans_a else "nt" if trans_b else "nn"]
        d = lax.dot_general(av, bv, (dims, ((), ())), preferred_element_type=F32)

        def finish(r):
            if add is not None:
                r = r + add_ref[...]
            o_ref[...] = r.astype(out_dtype)

        if nk == 1:
            finish(d)
        else:
            acc = refs[-1]

            @pl.when(k == 0)
            def _():
                acc[...] = d

            @pl.when((k > 0) & (k < nk - 1))
            def _():
                acc[...] += d

            @pl.when(k == nk - 1)
            def _():
                finish(acc[...] + d)

    a_spec = (pl.BlockSpec((tk, tm), lambda i, j, k: (k, i)) if trans_a
              else pl.BlockSpec((tm, tk), lambda i, j, k: (i, k)))
    b_spec = (pl.BlockSpec((tn, tk), lambda i, j, k: (j, k)) if trans_b
              else pl.BlockSpec((tk, tn), lambda i, j, k: (k, j)))
    in_specs = [a_spec, b_spec]
    args = [a, b]
    if add is not None:
        in_specs.append(pl.BlockSpec((tm, tn), lambda i, j, k: (i, j)))
        args.append(add)
    out = pl.pallas_call(
        body, name=name, grid=grid,
        in_specs=in_specs + rd.in_specs,
        out_specs=[pl.BlockSpec((tm, tn), lambda i, j, k: (i, j))] + rd.out_specs,
        out_shape=[jax.ShapeDtypeStruct((m, n), out_dtype)] + rd.out_shape,
        scratch_shapes=([pltpu.VMEM((tm, tn), F32)] if nk > 1 else []) + rd.scratch,
        compiler_params=_cp(*(("arbitrary",) * 3 if rides else ("parallel", "parallel", "arbitrary"))),
    )(*args, *rides)
    return out if rides else out[0]


def _rstd(x):
    return lax.rsqrt(jnp.mean(x * x, axis=-1, keepdims=True) + EPS)


def _norm_fwd(x, w, *, name):
    s = x.shape[0]
    tm = _tile(s, 512, 8)

    def body(x_ref, w_ref, o_ref):
        xv = x_ref[...]
        o_ref[...] = (xv * _rstd(xv) * w_ref[...]).astype(BF16)

    return pl.pallas_call(
        body, name=name, grid=(s // tm,),
        in_specs=[pl.BlockSpec((tm, D), lambda i: (i, 0)), pl.BlockSpec((1, D), lambda i: (0, 0))],
        out_specs=pl.BlockSpec((tm, D), lambda i: (i, 0)),
        out_shape=jax.ShapeDtypeStruct((s, D), BF16), compiler_params=_cp("parallel"),
    )(x, w)


def _norm_bwd_math(xv, wv, dy):
    r = _rstd(xv)
    xh = xv * r
    dxh = dy * wv
    dx = r * (dxh - xh * jnp.mean(dxh * xh, axis=-1, keepdims=True))
    dw = jnp.sum(dy * xh, axis=0, keepdims=True)
    return dx, dw


def _norm_bwd(x, w, dy, add, *, name):
    s = x.shape[0]
    tm = _tile(s, 512, 8)

    def body(*refs):
        if add is None:
            x_ref, w_ref, dy_ref, dx_ref, dw_ref = refs
        else:
            x_ref, w_ref, dy_ref, add_ref, dx_ref, dw_ref = refs

        @pl.when(pl.program_id(0) == 0)
        def _():
            dw_ref[...] = jnp.zeros_like(dw_ref)

        dx, dw = _norm_bwd_math(x_ref[...], w_ref[...], dy_ref[...])
        if add is not None:
            dx = dx + add_ref[...]
        dx_ref[...] = dx
        dw_ref[...] += dw

    row = pl.BlockSpec((tm, D), lambda i: (i, 0))
    vec = pl.BlockSpec((1, D), lambda i: (0, 0))
    in_specs = [row, vec, row] + ([row] if add is not None else [])
    args = [x, w, dy] + ([add] if add is not None else [])
    return pl.pallas_call(
        body, name=name, grid=(s // tm,), in_specs=in_specs, out_specs=[row, vec],
        out_shape=[jax.ShapeDtypeStruct((s, D), F32), jax.ShapeDtypeStruct((1, D), F32)],
        compiler_params=_cp("arbitrary"),
    )(*args)


def _final(x3, w, target, *, name):
    s = x3.shape[0]
    tm = _tile(s, 512, 8)

    def body(x_ref, w_ref, t_ref, dx_ref, dw_ref, loss_ref):
        @pl.when(pl.program_id(0) == 0)
        def _():
            dw_ref[...] = jnp.zeros_like(dw_ref)
            loss_ref[...] = jnp.zeros_like(loss_ref)

        xv = x_ref[...]
        wv = w_ref[...]
        y = xv * _rstd(xv) * wv
        err = y - t_ref[...]
        loss_ref[...] += 0.5 * jnp.sum(jnp.mean(err * err, axis=-1, keepdims=True))
        dx, dw = _norm_bwd_math(xv, wv, err * (1.0 / D))
        dx_ref[...] = dx
        dw_ref[...] += dw

    row = pl.BlockSpec((tm, D), lambda i: (i, 0))
    vec = pl.BlockSpec((1, D), lambda i: (0, 0))
    return pl.pallas_call(
        body, name=name, grid=(s // tm,), in_specs=[row, vec, row], out_specs=[row, vec, vec],
        out_shape=[jax.ShapeDtypeStruct((s, D), F32), jax.ShapeDtypeStruct((1, D), F32),
                   jax.ShapeDtypeStruct((1, D), F32)],
        compiler_params=_cp("arbitrary"),
    )(x3, w, target)


def _head_norm_math(o, w):
    lane = _iota((128, 128), 0) // 64
    bd = (lane == _iota((128, 128), 1) // 64).astype(F32)
    outs = []
    for op in _split(o, 128):
        ms = _dot("nn", op * op, bd, 2, 1) * (1.0 / 64)
        outs.append(op * lax.rsqrt(ms + EPS))
    return jnp.concatenate(outs, axis=1) * w


def _head_norm_fwd(o, w, *, name):
    s = o.shape[0]
    tm = _tile(s, 256, 8)

    def body(o_ref, w_ref, y_ref):
        y_ref[...] = _head_norm_math(o_ref[...], w_ref[...]).astype(BF16)

    row = pl.BlockSpec((tm, D), lambda i: (i, 0))
    vec = pl.BlockSpec((1, D), lambda i: (0, 0))
    return pl.pallas_call(
        body, name=name, grid=(s // tm,), in_specs=[row, vec], out_specs=row,
        out_shape=jax.ShapeDtypeStruct((s, D), BF16), compiler_params=_cp("parallel"),
    )(o, w)


def _head_norm_bwd(o, w, dymix, *, name):
    s = o.shape[0]
    tm = _tile(s, 256, 8)

    def body(o_ref, w_ref, dy_ref, do_ref, dw_ref):
        @pl.when(pl.program_id(0) == 0)
        def _():
            dw_ref[...] = jnp.zeros_like(dw_ref)

        _, vjp = jax.vjp(_head_norm_math, o_ref[...], w_ref[...])
        do, dw = vjp(dy_ref[...])
        do_ref[...] = do
        dw_ref[...] += dw

    row = pl.BlockSpec((tm, D), lambda i: (i, 0))
    vec = pl.BlockSpec((1, D), lambda i: (0, 0))
    return pl.pallas_call(
        body, name=name, grid=(s // tm,),
        in_specs=[row, vec, pl.BlockSpec((tm, D), lambda i: (i, 1))], out_specs=[row, vec],
        out_shape=[jax.ShapeDtypeStruct((s, D), F32), jax.ShapeDtypeStruct((1, D), F32)],
        compiler_params=_cp("arbitrary"),
    )(o, w, dymix)


SB_BQ = 256
SB_BK = 256


def _sb_consts():
    r = _iota((SB_BK, SB_BK), 0)
    c = _iota((SB_BK, SB_BK), 1)
    u_excl = (r > c).astype(BF16)
    u_incl = (r >= c).astype(BF16)
    return u_excl, u_incl


SB_LANES = 256
SB_NCH = SB_LANES // 64


def _nt(a, b):
    return lax.dot_general(a, b, (_DIMS["nt"], ((), ())), preferred_element_type=F32)


def _tn(a, b):
    return lax.dot_general(a, b, (_DIMS["tn"], ((), ())), preferred_element_type=F32)


def _nn(a, b):
    return jnp.dot(a, b, preferred_element_type=F32)


def _sb_heads(ref):
    out = []
    for hp in range(SB_LANES // 128):
        v = ref[:, 128 * hp:128 * (hp + 1)]
        first = _iota(v.shape, 1) < 64
        out += [jnp.where(first, v, 0).astype(BF16), jnp.where(first, 0, v).astype(BF16)]
    return out


SB_STRIP = 32


def _neg_abs(x):
    bits = lax.bitcast_convert_type(x, jnp.uint32) | jnp.uint32(0x80000000)
    return lax.bitcast_convert_type(bits, F32)


def _sb_block(ref, j):
    off = pl.multiple_of(j * SB_BK, SB_BK)
    return [ref[pl.ds(off, SB_BK), 128 * hp:128 * (hp + 1)] for hp in range(SB_NCH // 2)]


SB_DEAD = 104.0


def _sb_live(nlrun):
    m = nlrun[0]
    for x in nlrun[1:]:
        m = jnp.minimum(m, x)
    return jnp.min(m) < SB_DEAD


def _sb_strips():
    return [(r, pl.ds(r, SB_STRIP)) for r in range(0, SB_BQ, SB_STRIP)]


def _sb_diag_mask(r):
    return _iota((SB_STRIP, SB_BK), 1) < _iota((SB_STRIP, SB_BK), 0) + r


def _sb_soft(z, mask):
    e = jnp.exp(_neg_abs(z))
    nl = jnp.maximum(z, 0.0) + jnp.log(1.0 + e)
    if mask is not None:
        nl = jnp.where(mask, nl, 0.0)
    return e, nl


def _sb_split_to(hl_ref, rows, x):
    hi, lo = _terms(x, 2)
    hl_ref[rows, 0:SB_BK] = hi
    hl_ref[rows, SB_BK:2 * SB_BK] = lo


def _sb_stage_soft(z_ref, nl_ref, diag):
    for r, rows in _sb_strips():
        _, nl = _sb_soft(z_ref[rows, :], _sb_diag_mask(r) if diag else None)
        nl_ref[rows, 0:SB_BK] = nl.astype(BF16)


def _sb_stage_weights(z_ref, c_ref, a_ref, nlrun, diag):
    for r, rows in _sb_strips():
        a = jnp.exp(z_ref[rows, :] - c_ref[rows, :] - nlrun[r:r + SB_STRIP, :])
        if diag:
            a = jnp.where(_sb_diag_mask(r), a, 0.0)
        a_ref[rows, :] = a.astype(BF16)


def _sb_fwd(qkv, rides, *, name):
    s = qkv.shape[0]
    nq = s // SB_BQ
    ng = D // SB_LANES
    assert SB_BQ == SB_BK
    rd = _Rides(rides, [True] * len(rides))

    def body(*refs):
        (q_ref, k_ref, v_ref), (o_ref,), (zbuf, nlbuf, cbuf, abuf), handles = rd.split(refs, 3, 1, 4)
        i = pl.program_id(1)
        step_no = pl.program_id(0) * nq + i
        rd.run(handles, step_no == 0, step_no == ng * nq - 1)

        _, u_incl = _sb_consts()
        lane_a = _iota((SB_BQ, 128), 1) < 64
        qh = [q * 0.125 for q in _sb_heads(q_ref)]

        def tile(j, accs, nlrun, diag):
            kbs = _sb_block(k_ref, j)
            for c in range(SB_NCH):
                zbuf[c] = _nt(qh[c], kbs[c // 2])
            for c in range(SB_NCH):
                _sb_stage_soft(zbuf.at[c], nlbuf.at[c], diag)
                cbuf[c] = _nn(nlbuf[c], u_incl)
            for c in range(SB_NCH):
                _sb_stage_weights(zbuf.at[c], cbuf.at[c], abuf.at[c], nlrun[c], diag)
            nlrun = tuple(nlrun[c] + cbuf[c, :, 0:1] for c in range(SB_NCH))
            vbs = _sb_block(v_ref, j)
            outs = [_nn(abuf[c], vbs[c // 2]) for c in range(SB_NCH)]
            accs = tuple(acc + jnp.where(lane_a, outs[2 * hp], outs[2 * hp + 1]) for hp, acc in enumerate(accs))
            return accs, nlrun

        accs, nlrun = tile(i, (jnp.zeros((SB_BQ, 128), F32),) * (SB_NCH // 2),
                           (jnp.zeros((SB_BQ, 1), F32),) * SB_NCH, True)

        def step(carry):
            j, _, accs, nlrun = carry
            accs, nlrun = tile(j, accs, nlrun, False)
            return j - 1, _sb_live(nlrun), accs, nlrun

        _, _, accs, _ = lax.while_loop(lambda c: (c[0] >= 0) & c[1], step, (i - 1, _sb_live(nlrun), accs, nlrun))
        o_ref[...] = jnp.concatenate(accs, axis=1)

    return pl.pallas_call(
        body, name=name, grid=(ng, nq),
        in_specs=[pl.BlockSpec((SB_BQ, SB_LANES), lambda g, i: (i, g)),
                  pl.BlockSpec((s, SB_LANES), lambda g, i: (0, ng + g)),
                  pl.BlockSpec((s, SB_LANES), lambda g, i: (0, 2 * ng + g)), *rd.in_specs],
        out_specs=[pl.BlockSpec((SB_BQ, SB_LANES), lambda g, i: (i, g)), *rd.out_specs],
        out_shape=[jax.ShapeDtypeStruct((s, D), F32), *rd.out_shape],
        scratch_shapes=[pltpu.VMEM((SB_NCH, SB_BQ, SB_BK), F32), pltpu.VMEM((SB_NCH, SB_BQ, SB_BK), BF16),
                        pltpu.VMEM((SB_NCH, SB_BQ, SB_BK), F32), pltpu.VMEM((SB_NCH, SB_BQ, SB_BK), BF16),
                        *rd.scratch],
        compiler_params=_cp("arbitrary", "arbitrary"),
    )(qkv, qkv, qkv, *rides)


def _sb_bwd(qkv, o, do, rides, *, name):
    s = qkv.shape[0]
    nq = s // SB_BQ
    ng = D // SB_LANES
    nhp = SB_NCH // 2
    rd = _Rides(rides, [False] * len(rides))

    def body(*refs):
        ins, outs, scratch, handles = rd.split(refs, 5, 3, 11)
        q_ref, k_ref, v_ref, o_ref, do_ref = ins
        dq_ref, dk_hbm, dv_hbm = outs
        dk_acc, dv_acc, dk16, dv16, sems, zbuf, gbuf, hl, cbuf, abuf, dzbuf = scratch
        g_idx = pl.program_id(0)
        i = pl.program_id(1)
        step_no = g_idx * nq + i
        rd.run(handles, step_no == 0, step_no == ng * nq - 1)

        @pl.when(i == 0)
        def _():
            dk_acc[...] = jnp.zeros_like(dk_acc)
            dv_acc[...] = jnp.zeros_like(dv_acc)

        _, u_incl = _sb_consts()
        u2 = jnp.concatenate([u_incl, u_incl], axis=0)
        lane_a = _iota((SB_BQ, 128), 1) < 64
        lane_k = _iota((SB_BK, 128), 1) < 64
        qh = [q * 0.125 for q in _sb_heads(q_ref)]
        qf = [q_ref[:, 128 * hp:128 * (hp + 1)] for hp in range(nhp)]
        doh = _sb_heads(do_ref)
        dof = [do_ref[:, 128 * hp:128 * (hp + 1)].astype(BF16) for hp in range(nhp)]
        delta = []
        for hp in range(nhp):
            prod = dof[hp].astype(F32) * o_ref[:, 128 * hp:128 * (hp + 1)]
            delta += [jnp.sum(jnp.where(lane_a, prod, 0.0), axis=1, keepdims=True),
                      jnp.sum(jnp.where(lane_a, 0.0, prod), axis=1, keepdims=True)]

        def pre(slot, j):
            kbs = _sb_block(k_ref, j)
            vbs = _sb_block(v_ref, j)
            for c in range(SB_NCH):
                zbuf[slot, c] = _nt(qh[c], kbs[c // 2])
                gbuf[slot, c] = _nt(doh[c], vbs[c // 2])

        def stage_g(c, slot):
            for _, rows in _sb_strips():
                g = abuf[slot, c, rows, :].astype(F32) * gbuf[slot, c, rows, :]
                gbuf[slot, c, rows, :] = g
                _sb_split_to(hl.at[c], rows, g)

        def stage_dz(c, slot, grun, diag):
            for r, rows in _sb_strips():
                z = zbuf[slot, c, rows, :]
                g = gbuf[slot, c, rows, :]
                cs = (delta[c] - grun)[r:r + SB_STRIP, :] - cbuf[c, rows, :]
                sig = 1.0 / (1.0 + jnp.exp(-z))
                dz = g - (g + cs) * sig
                if diag:
                    dz = jnp.where(_sb_diag_mask(r), dz, 0.0)
                dzbuf[slot, c, rows, :] = dz.astype(BF16)

        def chain(slot, nlrun, grun, diag):
            for c in range(SB_NCH):
                _sb_stage_soft(zbuf.at[slot, c], hl.at[c], diag)
                cbuf[c] = _nn(hl[c, :, 0:SB_BK], u_incl)
            nl_tot = []
            for c in range(SB_NCH):
                _sb_stage_weights(zbuf.at[slot, c], cbuf.at[c], abuf.at[slot, c], nlrun[c], diag)
                nl_tot.append(cbuf[c, :, 0:1])
                stage_g(c, slot)
                cbuf[c] = _nn(hl[c], u2)
            g_tot = []
            for c in range(SB_NCH):
                stage_dz(c, slot, grun[c], diag)
                g_tot.append(cbuf[c, :, 0:1])
            return (tuple(a + b for a, b in zip(nlrun, nl_tot)), tuple(a + b for a, b in zip(grun, g_tot)))

        def post(slot, j, dqs):
            off = pl.multiple_of(j * SB_BK, SB_BK)
            kbs = _sb_block(k_ref, j)
            dq_t = [_nn(dzbuf[slot, c], kbs[c // 2]) for c in range(SB_NCH)]
            dk_t = [_tn(dzbuf[slot, c], qf[c // 2]) for c in range(SB_NCH)]
            dv_t = [_tn(abuf[slot, c], dof[c // 2]) for c in range(SB_NCH)]
            for hp in range(nhp):
                cols = slice(128 * hp, 128 * (hp + 1))
                dk_acc[pl.ds(off, SB_BK), cols] += 0.125 * jnp.where(lane_k, dk_t[2 * hp], dk_t[2 * hp + 1])
                dv_acc[pl.ds(off, SB_BK), cols] += jnp.where(lane_k, dv_t[2 * hp], dv_t[2 * hp + 1])
            return tuple(dq + jnp.where(lane_a, dq_t[2 * hp], dq_t[2 * hp + 1]) for hp, dq in enumerate(dqs))

        def tile(j, dqs, nlrun, grun, diag):
            pre(0, j)
            nlrun, grun = chain(0, nlrun, grun, diag)
            return post(0, j, dqs), nlrun, grun

        zero = (jnp.zeros((SB_BQ, 1), F32),) * SB_NCH
        dqs, nlrun, grun = tile(i, (jnp.zeros((SB_BQ, 128), F32),) * nhp, zero, zero, True)

        def step(carry):
            j, _, dqs, nlrun, grun = carry
            dqs, nlrun, grun = tile(j, dqs, nlrun, grun, False)
            return j - 1, _sb_live(nlrun), dqs, nlrun, grun

        carry = lax.while_loop(lambda c: (c[0] >= 0) & c[1], step, (i - 1, _sb_live(nlrun), dqs, nlrun, grun))
        dq_ref[...] = (0.125 * jnp.concatenate(carry[2], axis=1)).astype(BF16)

        def out_copies(g):
            cols = pl.ds(pl.multiple_of(g * SB_LANES, SB_LANES), SB_LANES)
            return (pltpu.make_async_copy(dk16, dk_hbm.at[:, cols], sems.at[0]),
                    pltpu.make_async_copy(dv16, dv_hbm.at[:, cols], sems.at[1]))

        @pl.when((i == nq - 1) & (g_idx > 0))
        def _():
            for cp in out_copies(g_idx - 1):
                cp.wait()

        @pl.when(i == nq - 1)
        def _():
            def narrow(r, carry):
                rows = pl.ds(pl.multiple_of(r * SB_BK, SB_BK), SB_BK)
                dk16[rows, :] = dk_acc[rows, :].astype(BF16)
                dv16[rows, :] = dv_acc[rows, :].astype(BF16)
                return carry

            lax.fori_loop(0, s // SB_BK, narrow, 0)
            for cp in out_copies(g_idx):
                cp.start()

        @pl.when((i == nq - 1) & (g_idx == ng - 1))
        def _():
            for cp in out_copies(g_idx):
                cp.wait()

    qblk = pl.BlockSpec((SB_BQ, SB_LANES), lambda g, i: (i, g))
    hbm = pl.BlockSpec(memory_space=pl.ANY)
    return pl.pallas_call(
        body, name=name, grid=(ng, nq),
        in_specs=[qblk, pl.BlockSpec((s, SB_LANES), lambda g, i: (0, ng + g)),
                  pl.BlockSpec((s, SB_LANES), lambda g, i: (0, 2 * ng + g)), qblk, qblk, *rd.in_specs],
        out_specs=[qblk, hbm, hbm, *rd.out_specs],
        out_shape=[jax.ShapeDtypeStruct((s, D), BF16)] * 3 + rd.out_shape,
        scratch_shapes=[pltpu.VMEM((s, SB_LANES), F32), pltpu.VMEM((s, SB_LANES), F32),
                        pltpu.VMEM((s, SB_LANES), BF16), pltpu.VMEM((s, SB_LANES), BF16),
                        pltpu.SemaphoreType.DMA((2,)),
                        pltpu.VMEM((1, SB_NCH, SB_BQ, SB_BK), F32), pltpu.VMEM((1, SB_NCH, SB_BQ, SB_BK), F32),
                        pltpu.VMEM((SB_NCH, SB_BQ, 2 * SB_BK), BF16), pltpu.VMEM((SB_NCH, SB_BQ, SB_BK), F32),
                        pltpu.VMEM((1, SB_NCH, SB_BQ, SB_BK), BF16), pltpu.VMEM((1, SB_NCH, SB_BQ, SB_BK), BF16),
                        *rd.scratch],
        compiler_params=_cp("arbitrary", "arbitrary"),
    )(qkv, qkv, qkv, o, do, *rides)


def _ssd_core(z, xpre, dtr, state, dtb, alog, dsk, nw):
    L = SSD_CHUNK
    xa = _silu(xpre)
    pieces = _split(xa, 128)
    xs = jnp.concatenate(pieces[:8], axis=1)
    bm, cm = pieces[8:10], pieces[10:12]
    dt = _softplus(dtr + dtb)
    a = dt * (-jnp.exp(alog))
    tri = (_iota((L, L), 0) >= _iota((L, L), 1)).astype(F32)
    a_cs = _dot("nn", tri, a, 1, 3)
    xc = xs * dt
    tril = _iota((L, L), 0) >= _iota((L, L), 1)
    lane_a = _iota((L, 128), 1) < 64
    acs_p = _split(a_cs, 128)
    xc_p = _split(xc, 128)
    ys, new_states = [], []
    for g in range(2):
        cb = _dot("nt", cm[g], bm[g])
        for pp in range(4):
            pair = 4 * g + pp
            acs = acs_p[pair]
            acs_t = acs.T
            xcp = xc_p[pair]
            st = state[pair]
            heads = []
            for hh in range(2):
                col = _take_fn(1, 64 * hh)(acs)
                row = _take_fn(0, 64 * hh)(acs_t)
                seg = col - row
                lm = jnp.where(tril, jnp.exp(jnp.where(tril, seg, 0.0)), 0.0)
                heads.append(_dot("nn", cb * lm, xcp))
            ydiag = jnp.where(lane_a, heads[0], heads[1])
            last = _take_fn(0, L - 1)(acs)
            snew = _dot("tn", xcp * jnp.exp(last - acs), bm[g])
            new_states.append(st * jnp.exp(_take_fn(1, L - 1)(acs_t)) + snew)
            yoff = _dot("nt", cm[g], st) * jnp.exp(acs)
            ys.append(ydiag + yoff)
    y = jnp.concatenate(ys, axis=1) + xs * dsk
    yg = y * _silu(z)
    outs = []
    for v in _split(yg, 512):
        outs.append(v * lax.rsqrt(jnp.mean(v * v, axis=-1, keepdims=True) + EPS))
    return jnp.concatenate(outs, axis=1) * nw, tuple(new_states)


XBC = 1536


def _ssd_conv(ext_ref, cw, cb):
    acc = cb
    for k in range(4):
        acc = acc + cw[k:k + 1, :] * ext_ref[pl.ds(HALO - 3 + k, SSD_CHUNK), :]
    return acc


def _ssd_fwd(z, xbc, dtr, cw, cb, lanes, rides, *, name):
    s = z.shape[0]
    L = SSD_CHUNK
    nc = s // L
    rd = _Rides(rides, [True] * len(rides))

    def body(*refs):
        ins, (y_ref, st_ref), (state, ext), handles = rd.split(refs, 7, 2, 2)
        z_ref, x_ref, h_ref, dtr_ref, cw_ref, cb_ref, ln_ref = ins
        c = pl.program_id(0)
        rd.run(handles, c == 0, c == nc - 1)

        @pl.when(c == 0)
        def _():
            state[...] = jnp.zeros_like(state)

        ext[0:HALO, :] = jnp.where(c == 0, 0.0, h_ref[...])
        ext[HALO:, :] = x_ref[...]
        xpre = _ssd_conv(ext, cw_ref[...], cb_ref[...])
        st_ref[0] = state[...]
        st_in = tuple(state[p] for p in range(8))
        yn, st_out = _ssd_core(z_ref[...], xpre, dtr_ref[...], st_in,
                               ln_ref[0:1, :], ln_ref[1:2, :], ln_ref[2:3, :], ln_ref[3:4, :])
        y_ref[...] = yn.astype(BF16)
        for p in range(8):
            state[p] = st_out[p]

    return pl.pallas_call(
        body, name=name, grid=(nc,),
        in_specs=[pl.BlockSpec((L, D), lambda c: (c, 0)),
                  pl.BlockSpec((L, XBC), lambda c: (c, 0)),
                  pl.BlockSpec((HALO, XBC), lambda c: (jnp.maximum(c * (L // HALO) - 1, 0), 0)),
                  pl.BlockSpec((L, D), lambda c: (c, 0)),
                  pl.BlockSpec((4, XBC), lambda c: (0, 0)),
                  pl.BlockSpec((1, XBC), lambda c: (0, 0)),
                  pl.BlockSpec((8, D), lambda c: (0, 0)), *rd.in_specs],
        out_specs=[pl.BlockSpec((L, D), lambda c: (c, 0)),
                   pl.BlockSpec((1, 8, 128, 128), lambda c: (c, 0, 0, 0)), *rd.out_specs],
        out_shape=[jax.ShapeDtypeStruct((s, D), BF16), jax.ShapeDtypeStruct((nc, 8, 128, 128), F32),
                   *rd.out_shape],
        scratch_shapes=[pltpu.VMEM((8, 128, 128), F32), pltpu.VMEM((L + HALO, XBC), F32), *rd.scratch],
        compiler_params=_cp("arbitrary"),
    )(z, xbc, xbc, dtr, cw, cb, lanes, *rides)


def _ssd_bwd(z, xbc, dtr, states, dymix, cw, cb, lanes, *, name):
    s = z.shape[0]
    L = SSD_CHUNK
    nc = s // L

    def body(z_ref, x_ref, h_ref, dtr_ref, st_ref, dy_ref, cw_ref, cb_ref, ln_ref,
             dz_ref, dx_ref, ddt_ref, dln_ref, dcv_ref, dstate, ext, dext):
        i = pl.program_id(0)
        c = nc - 1 - i

        @pl.when(i == 0)
        def _():
            dstate[...] = jnp.zeros_like(dstate)
            dext[...] = jnp.zeros_like(dext)
            dln_ref[...] = jnp.zeros_like(dln_ref)
            dcv_ref[...] = jnp.zeros_like(dcv_ref)

        ext[0:HALO, :] = jnp.where(c == 0, 0.0, h_ref[...])
        ext[HALO:, :] = x_ref[...]
        cwv = cw_ref[...]
        xpre = _ssd_conv(ext, cwv, cb_ref[...])
        st_in = tuple(st_ref[0, p] for p in range(8))
        _, vjp = jax.vjp(_ssd_core, z_ref[...], xpre, dtr_ref[...], st_in,
                         ln_ref[0:1, :], ln_ref[1:2, :], ln_ref[2:3, :], ln_ref[3:4, :])
        dz, dxpre, ddtr, dst, d0, d1, d2, d3 = vjp((dy_ref[...], tuple(dstate[p] for p in range(8))))
        for p in range(8):
            dstate[p] = dst[p]
        dz_ref[...] = dz.astype(BF16)
        ddt_ref[...] = ddtr.astype(BF16)
        dln_ref[0:4, :] += jnp.concatenate([d0, d1, d2, d3], axis=0)
        dext[0:L, :] = dxpre
        xcur = x_ref[...]
        dx = jnp.zeros((L, XBC), F32)
        rows = []
        for k in range(4):
            shifted = dext[pl.ds(3 - k, L), :]
            dx = dx + cwv[k:k + 1, :] * shifted
            rows.append(jnp.sum(shifted * xcur, axis=0, keepdims=True))
        rows.append(jnp.sum(dxpre, axis=0, keepdims=True))
        dx_ref[...] = dx.astype(BF16)
        dcv_ref[0:5, :] += jnp.concatenate(rows, axis=0)
        dext[L:L + HALO, :] = dxpre[0:HALO, :]

    rev = lambda i: (nc - 1 - i, 0)
    return pl.pallas_call(
        body, name=name, grid=(nc,),
        in_specs=[pl.BlockSpec((L, D), rev),
                  pl.BlockSpec((L, XBC), rev),
                  pl.BlockSpec((HALO, XBC), lambda i: (jnp.maximum((nc - 1 - i) * (L // HALO) - 1, 0), 0)),
                  pl.BlockSpec((L, D), rev),
                  pl.BlockSpec((1, 8, 128, 128), lambda i: (nc - 1 - i, 0, 0, 0)),
                  pl.BlockSpec((L, D), rev),
                  pl.BlockSpec((4, XBC), lambda i: (0, 0)),
                  pl.BlockSpec((1, XBC), lambda i: (0, 0)),
                  pl.BlockSpec((8, D), lambda i: (0, 0))],
        out_specs=[pl.BlockSpec((L, D), rev), pl.BlockSpec((L, XBC), rev), pl.BlockSpec((L, D), rev),
                   pl.BlockSpec((8, D), lambda i: (0, 0)), pl.BlockSpec((8, XBC), lambda i: (0, 0))],
        out_shape=[jax.ShapeDtypeStruct((s, D), BF16), jax.ShapeDtypeStruct((s, XBC), BF16),
                   jax.ShapeDtypeStruct((s, D), BF16), jax.ShapeDtypeStruct((8, D), F32),
                   jax.ShapeDtypeStruct((8, XBC), F32)],
        scratch_shapes=[pltpu.VMEM((8, 128, 128), F32), pltpu.VMEM((L + HALO, XBC), F32),
                        pltpu.VMEM((L + HALO, XBC), F32)],
        compiler_params=_cp("arbitrary"),
    )(z, xbc, xbc, dtr, states, dymix, cw, cb, lanes)


def _mem_attn_math(q, k, v):
    outs = []
    for qh, kh, vh in zip(_split(q, 256), _split(k, 256), _split(v, 256)):
        sc = _dot("nt", qh, kh) * (1.0 / 16.0)
        e = jnp.exp(sc - lax.stop_gradient(jnp.max(sc, axis=-1, keepdims=True)))
        p = e / jnp.sum(e, axis=-1, keepdims=True)
        outs.append(_dot("nn", p, vh))
    return jnp.concatenate(outs, axis=1)


def _mem_attn_fwd(q, k, v, *, name):
    s, m = q.shape[0], k.shape[0]
    tm = _tile(s, 256, 8)

    def body(q_ref, k_ref, v_ref, o_ref):
        o_ref[...] = _mem_attn_math(q_ref[...].astype(F32), k_ref[...].astype(F32),
                                    v_ref[...].astype(F32)).astype(BF16)

    row = pl.BlockSpec((tm, D), lambda i: (i, 0))
    kv = pl.BlockSpec((m, D), lambda i: (0, 0))
    return pl.pallas_call(
        body, name=name, grid=(s // tm,), in_specs=[row, kv, kv], out_specs=row,
        out_shape=jax.ShapeDtypeStruct((s, D), BF16), compiler_params=_cp("parallel"),
    )(q, k, v)


def _mem_attn_bwd(q, k, v, do, *, name):
    s, m = q.shape[0], k.shape[0]
    tm = _tile(s, 256, 8)

    def body(q_ref, k_ref, v_ref, do_ref, dq_ref, dk_ref, dv_ref):
        @pl.when(pl.program_id(0) == 0)
        def _():
            dk_ref[...] = jnp.zeros_like(dk_ref)
            dv_ref[...] = jnp.zeros_like(dv_ref)

        _, vjp = jax.vjp(_mem_attn_math, q_ref[...].astype(F32), k_ref[...].astype(F32),
                         v_ref[...].astype(F32))
        dq, dk, dv = vjp(do_ref[...])
        dq_ref[...] = dq.astype(BF16)
        dk_ref[...] += dk
        dv_ref[...] += dv

    row = pl.BlockSpec((tm, D), lambda i: (i, 0))
    kv = pl.BlockSpec((m, D), lambda i: (0, 0))
    return pl.pallas_call(
        body, name=name, grid=(s // tm,), in_specs=[row, kv, kv, row], out_specs=[row, kv, kv],
        out_shape=[jax.ShapeDtypeStruct((s, D), BF16), jax.ShapeDtypeStruct((m, D), F32),
                   jax.ShapeDtypeStruct((m, D), F32)],
        compiler_params=_cp("arbitrary"),
    )(q, k, v, do)


DFF = 2816
FFN_TC = 1408
FFN_TM = 512


FFN_CHUNKS = tuple((c, min(512, FFN_TC - c)) for c in range(0, FFN_TC, 512))


def _rows8(ref, r, cols):
    return ref[pl.ds(pl.multiple_of(r, HALO), HALO), cols]


def _shift_down(prev, cur, s):
    return jnp.where(_iota(cur.shape, 0) < s, pltpu.roll(prev, s, 0), pltpu.roll(cur, s, 0))


def _shift_up(cur, nxt, s):
    return jnp.where(_iota(cur.shape, 0) >= HALO - s, pltpu.roll(nxt, HALO - s, 0), pltpu.roll(cur, HALO - s, 0))


def _ffn_conv_strip(ext_ref, r, cols, cw, cb):
    prev, cur = _rows8(ext_ref, r, cols), _rows8(ext_ref, r + HALO, cols)
    return cb + cw[0:1, :] * _shift_down(prev, cur, 2) + cw[1:2, :] * _shift_down(prev, cur, 1) + cw[2:3, :] * cur


def _ffn_specs(s):
    tm, tc = FFN_TM, FFN_TC
    blk = pl.BlockSpec((tm, tc), lambda i, j: (i, j))
    halo = pl.BlockSpec((HALO, tc), lambda i, j: (jnp.maximum(i * (tm // HALO) - 1, 0), j))
    cw = pl.BlockSpec((3, tc), lambda i, j: (0, j))
    cb = pl.BlockSpec((1, tc), lambda i, j: (0, j))
    return tm, tc, blk, halo, cw, cb


def _glu_fwd(ug, uv, cwg, cwv, cbg, cbv, *, name):
    s = ug.shape[0]
    tm, tc, blk, halo, cw, cb = _ffn_specs(s)

    def body(g_ref, gh_ref, v_ref, vh_ref, cwg_ref, cwv_ref, cbg_ref, cbv_ref, f_ref, eg, ev):
        first = pl.program_id(0) == 0
        eg[0:HALO, :] = jnp.where(first, 0.0, gh_ref[...])
        eg[HALO:, :] = g_ref[...]
        ev[0:HALO, :] = jnp.where(first, 0.0, vh_ref[...])
        ev[HALO:, :] = v_ref[...]
        cwgv, cwvv, cbgv, cbvv = cwg_ref[...], cwv_ref[...], cbg_ref[...], cbv_ref[...]

        def step(t, carry):
            for c0, w in FFN_CHUNKS:
                cols = slice(c0, c0 + w)
                outs = []
                for h in range(2):
                    r = t * 16 + HALO * h
                    g = _ffn_conv_strip(eg, r, cols, cwgv[:, cols], cbgv[:, cols])
                    v = _ffn_conv_strip(ev, r, cols, cwvv[:, cols], cbvv[:, cols])
                    outs.append(_silu(g) * v)
                f_ref[pl.ds(pl.multiple_of(t * 16, 16), 16), cols] = jnp.concatenate(outs, axis=0).astype(BF16)
            return carry

        lax.fori_loop(0, tm // 16, step, 0)

    return pl.pallas_call(
        body, name=name, grid=(s // tm, DFF // tc),
        in_specs=[blk, halo, blk, halo, cw, cw, cb, cb], out_specs=blk,
        out_shape=jax.ShapeDtypeStruct((s, DFF), BF16),
        scratch_shapes=[pltpu.VMEM((tm + HALO, tc), F32)] * 2,
        compiler_params=_cp("parallel", "parallel"),
    )(ug, ug, uv, uv, cwg, cwv, cbg, cbv)


def _ffn_bwd(ug, uv, df, cwg, cwv, cbg, cbv, *, name):
    s = ug.shape[0]
    tm, tc = FFN_TM, FFN_TC
    nb = s // tm
    rows_ext = tm + HALO

    def body(g_ref, gp_ref, gn_ref, v_ref, vp_ref, vn_ref, df_ref, dfn_ref, cwg_ref, cwv_ref, cbg_ref, cbv_ref,
             dxg_ref, dxv_ref, dcg_ref, dcv_ref, eg, ev, edf, edg, edv, accg, accv):
        i = pl.program_id(1)
        first, last = i == 0, i == nb - 1

        @pl.when(first)
        def _():
            dcg_ref[...] = jnp.zeros_like(dcg_ref)
            dcv_ref[...] = jnp.zeros_like(dcv_ref)

        for e, prev, main, nxt in ((eg, gp_ref, g_ref, gn_ref), (ev, vp_ref, v_ref, vn_ref)):
            e[0:HALO, :] = jnp.where(first, 0.0, prev[...])
            e[HALO:HALO + tm, :] = main[...]
            e[HALO + tm:, :] = jnp.where(last, 0.0, nxt[...])
        edf[0:tm, :] = df_ref[...]
        edf[tm:, :] = jnp.where(last, 0.0, dfn_ref[...])
        accg[...] = jnp.zeros_like(accg)
        accv[...] = jnp.zeros_like(accv)
        cwgv, cwvv, cbgv, cbvv = cwg_ref[...], cwv_ref[...], cbg_ref[...], cbv_ref[...]

        def cotangents(t, carry):
            r = t * HALO
            for c0, w in FFN_CHUNKS:
                cols = slice(c0, c0 + w)
                g = _ffn_conv_strip(eg, r, cols, cwgv[:, cols], cbgv[:, cols])
                v = _ffn_conv_strip(ev, r, cols, cwvv[:, cols], cbvv[:, cols])
                dfs = _rows8(edf, r, cols)
                sg = _sigmoid(g)
                edv[pl.ds(pl.multiple_of(r, HALO), HALO), cols] = dfs * g * sg
                edg[pl.ds(pl.multiple_of(r, HALO), HALO), cols] = dfs * v * sg * (1.0 + g * (1.0 - sg))
            return carry

        lax.fori_loop(0, rows_ext // HALO, cotangents, 0)

        def conv_backward(t, carry):
            for c0, w in FFN_CHUNKS:
                cols = slice(c0, c0 + w)
                for edu, e, cw, dx_ref, acc in ((edg, eg, cwgv[:, cols], dxg_ref, accg),
                                                (edv, ev, cwvv[:, cols], dxv_ref, accv)):
                    dxs = []
                    for h in range(2):
                        r = t * 16 + HALO * h
                        cur, nxt = _rows8(edu, r, cols), _rows8(edu, r + HALO, cols)
                        up1, up2 = _shift_up(cur, nxt, 1), _shift_up(cur, nxt, 2)
                        x = _rows8(e, r + HALO, cols)
                        dxs.append(cw[2:3, :] * cur + cw[1:2, :] * up1 + cw[0:1, :] * up2)
                        acc[0, :, cols] += up2 * x
                        acc[1, :, cols] += up1 * x
                        acc[2, :, cols] += cur * x
                        acc[3, :, cols] += cur
                    dx_ref[pl.ds(pl.multiple_of(t * 16, 16), 16), cols] = jnp.concatenate(dxs, axis=0).astype(BF16)
            return carry

        lax.fori_loop(0, tm // 16, conv_backward, 0)
        for acc, dc_ref in ((accg, dcg_ref), (accv, dcv_ref)):
            dc_ref[0:4, :] += jnp.concatenate([jnp.sum(acc[k], axis=0, keepdims=True) for k in range(4)], axis=0)

    blk = pl.BlockSpec((tm, tc), lambda j, i: (i, j))
    nxt = pl.BlockSpec((HALO, tc), lambda j, i: (jnp.minimum((i + 1) * (tm // HALO), s // HALO - 1), j))
    prv = pl.BlockSpec((HALO, tc), lambda j, i: (jnp.maximum(i * (tm // HALO) - 1, 0), j))
    cw = pl.BlockSpec((3, tc), lambda j, i: (0, j))
    cb = pl.BlockSpec((1, tc), lambda j, i: (0, j))
    acc = pl.BlockSpec((8, tc), lambda j, i: (0, j))
    return pl.pallas_call(
        body, name=name, grid=(DFF // tc, nb),
        in_specs=[blk, prv, nxt, blk, prv, nxt, blk, nxt, cw, cw, cb, cb],
        out_specs=[blk, blk, acc, acc],
        out_shape=[jax.ShapeDtypeStruct((s, DFF), BF16)] * 2 + [jax.ShapeDtypeStruct((8, DFF), F32)] * 2,
        scratch_shapes=[pltpu.VMEM((tm + 2 * HALO, tc), F32)] * 2 + [pltpu.VMEM((rows_ext, tc), F32)] * 3
                       + [pltpu.VMEM((4, HALO, tc), F32)] * 2,
        compiler_params=_cp("parallel", "arbitrary"),
    )(ug, ug, ug, uv, uv, uv, df, df, cwg, cwv, cbg, cbv)


MESH = pl.DeviceIdType.MESH


def _all_gather(arrs, *, name):
    n = len(arrs)

    def body(*refs):
        x_refs, out_refs = refs[:n], refs[n:2 * n]
        send_sems, recv_sems, local_sems = refs[2 * n:]
        x, y, c = lax.axis_index("x"), lax.axis_index("y"), lax.axis_index("c")
        me, sibling = (x, y, c), (x, y, 1 - c)
        chips = [(1 - x, y), (x, 1 - y), (1 - x, 1 - y)]

        def blk(a, dev):
            return out_refs[a].at[4 * dev[0] + 2 * dev[1] + dev[2]]

        def copy(a, k, block, to, src=None):
            return pltpu.make_async_remote_copy(
                src_ref=blk(a, block) if src is None else src, dst_ref=blk(a, block),
                send_sem=send_sems.at[7 * a + k], recv_sem=recv_sems.at[7 * a + k],
                device_id=to, device_id_type=MESH)

        started = []
        mine = []
        for a in range(n):
            cp = pltpu.make_async_copy(x_refs[a], blk(a, me), local_sems.at[a])
            cp.start()
            mine.append(cp)
            first = [copy(a, 0, me, sibling, src=x_refs[a])]
            first += [copy(a, 1 + j, me, (*chip, c), src=x_refs[a]) for j, chip in enumerate(chips)]
            for cp in first:
                cp.start()
            started += first
        for a in range(n):
            for j, chip in enumerate(chips):
                copy(a, 1 + j, (*chip, c), me).wait_recv()
                fwd = copy(a, 4 + j, (*chip, c), sibling)
                fwd.start()
                started.append(fwd)
        for a in range(n):
            copy(a, 0, sibling, me).wait_recv()
            for j, chip in enumerate(chips):
                copy(a, 4 + j, (*chip, 1 - c), me).wait_recv()
        for cp in started:
            cp.wait_send()
        for cp in mine:
            cp.wait()

    any_spec = pl.BlockSpec(memory_space=pl.ANY)
    return pl.pallas_call(
        body, name=name,
        in_specs=[any_spec] * n, out_specs=[any_spec] * n,
        out_shape=[jax.ShapeDtypeStruct((NDEV,) + a.shape, a.dtype) for a in arrs],
        scratch_shapes=[pltpu.SemaphoreType.DMA((7 * n,)), pltpu.SemaphoreType.DMA((7 * n,)),
                        pltpu.SemaphoreType.DMA((n,))],
    )(*arrs)


class _Direct:
    SEMS = (pltpu.SemaphoreType.DMA((7,)), pltpu.SemaphoreType.DMA((7,)), pltpu.SemaphoreType.DMA((1,)))

    def __init__(self, src_ref, recv_ref, sems, gather):
        x, y, c = lax.axis_index("x"), lax.axis_index("y"), lax.axis_index("c")
        me = 4 * x + 2 * y + c
        send_sems, recv_sems, local_sem = sems
        src = (lambda pid: src_ref) if gather else (lambda pid: src_ref.at[pid])
        self.mine = pltpu.make_async_copy(src(me), recv_ref.at[me], local_sem.at[0])
        self.copies = []
        for k in range(1, NDEV):
            px = 1 - x if k & 4 else x
            py = 1 - y if k & 2 else y
            pc = 1 - c if k & 1 else c
            self.copies.append(pltpu.make_async_remote_copy(
                src_ref=src(4 * px + 2 * py + pc), dst_ref=recv_ref.at[me],
                send_sem=send_sems.at[k - 1], recv_sem=recv_sems.at[k - 1],
                device_id=(px, py, pc), device_id_type=MESH))

    def start(self):
        self.mine.start()
        for cp in self.copies:
            cp.start()

    def wait(self):
        for cp in self.copies:
            cp.wait_recv()
        for cp in self.copies:
            cp.wait_send()
        self.mine.wait()


def _recv_shape(src, gather):
    return jax.ShapeDtypeStruct(((NDEV,) + src.shape) if gather else src.shape, src.dtype)


class _Rides:
    def __init__(self, rides, gathers):
        self.n = len(rides)
        self.gathers = list(gathers)
        any_spec = pl.BlockSpec(memory_space=pl.ANY)
        self.in_specs = [any_spec] * self.n
        self.out_specs = [any_spec] * self.n
        self.out_shape = [_recv_shape(a, g) for a, g in zip(rides, gathers)]
        self.scratch = list(_Direct.SEMS) * self.n

    def split(self, refs, n_in, n_out, n_scratch):
        n = self.n
        ins, refs = refs[:n_in], refs[n_in:]
        rides, refs = refs[:n], refs[n:]
        outs, refs = refs[:n_out], refs[n_out:]
        gots, refs = refs[:n], refs[n:]
        scratch, sems = refs[:n_scratch], refs[n_scratch:]
        return ins, outs, scratch, (rides, gots, sems)

    def run(self, handles, first, last):
        rides, gots, sems = handles

        def all_of():
            return [_Direct(rides[a], gots[a], sems[3 * a:3 * a + 3], self.gathers[a]) for a in range(self.n)]

        @pl.when(first)
        def _():
            for e in all_of():
                e.start()

        @pl.when(last)
        def _():
            for e in all_of():
                e.wait()


def _exchange(arrs, gathers, *, name):
    rd = _Rides(arrs, gathers)

    def body(*refs):
        _, _, _, handles = rd.split(refs, 0, 0, 0)
        rd.run(handles, True, True)

    return pl.pallas_call(
        body, name=name, in_specs=rd.in_specs, out_specs=rd.out_specs, out_shape=rd.out_shape,
        scratch_shapes=rd.scratch,
    )(*arrs)


def _adamw(parts, w, m, v, *, name):
    r, cols = w.shape
    tm = _tile(r, 256, PACK_ALIGN)
    c1 = 1.0 - ADAM_B1 ** ADAM_STEP
    c2 = 1.0 - ADAM_B2 ** ADAM_STEP

    def body(p_ref, w_ref, m_ref, v_ref, g_ref, d_ref, nm_ref, nv_ref):
        g = p_ref[0].astype(F32)
        for i in range(1, NDEV):
            g = g + p_ref[i].astype(F32)
        nm = ADAM_B1 * m_ref[...] + (1.0 - ADAM_B1) * g
        nv = ADAM_B2 * v_ref[...] + (1.0 - ADAM_B2) * (g * g)
        d_ref[...] = -ADAM_LR * ((nm / c1) / (jnp.sqrt(nv / c2) + ADAM_EPS) + ADAM_WD * w_ref[...])
        g_ref[...] = g
        nm_ref[...] = nm
        nv_ref[...] = nv

    row = pl.BlockSpec((tm, cols), lambda i: (i, 0))
    return pl.pallas_call(
        body, name=name, grid=(r // tm,),
        in_specs=[pl.BlockSpec((NDEV, tm, cols), lambda i: (0, i, 0)), row, row, row],
        out_specs=[row] * 4, out_shape=[jax.ShapeDtypeStruct((r, cols), F32)] * 4,
        compiler_params=_cp("parallel"),
    )(parts, w, m, v)


PACK_ALIGN = 16


def _part_rows(shape):
    n = -(-math.prod(shape) // D)
    return n + (-n) % PACK_ALIGN


def _rows(a):
    flat = a.reshape(-1)
    pad = _part_rows(a.shape) * D - flat.shape[0]
    if pad:
        flat = jnp.concatenate([flat, jnp.zeros((pad,), flat.dtype)])
    return flat.reshape(-1, D)


def _pack(parts, total_rows):
    if all(math.prod(p.shape) % (PACK_ALIGN * D) for p in parts):
        return _pack_small(parts, total_rows)
    rows = [_rows(p) for p in parts]
    used = sum(r.shape[0] for r in rows)
    if total_rows > used:
        rows.append(jnp.zeros((total_rows - used, D), rows[0].dtype))
    return jnp.concatenate(rows, axis=0)


def _pack_small(parts, total_rows):
    flat, used = [], 0
    for p in parts:
        n, nr = math.prod(p.shape), _part_rows(p.shape)
        flat += [p.reshape(-1), jnp.zeros((nr * D - n,), p.dtype)]
        used += nr
    flat.append(jnp.zeros(((total_rows - used) * D,), parts[0].dtype))
    return jnp.concatenate(flat).reshape(total_rows, D)


def _unpack(buf, shapes, part_rows=_part_rows):
    out, r0 = [], 0
    for shp in shapes:
        n = math.prod(shp)
        out.append(buf[r0:r0 + part_rows(shp)].reshape(-1)[:n].reshape(shp))
        r0 += part_rows(shp)
    return out


def _tight_rows(shape):
    return -(-math.prod(shape) // D)


def _pack_tight(parts, total_rows):
    flat, used = [], 0
    for p in parts:
        n, nr = math.prod(p.shape), _tight_rows(p.shape)
        flat += [p.reshape(-1), jnp.zeros((nr * D - n,), p.dtype)]
        used += nr
    flat.append(jnp.zeros(((total_rows - used) * D,), parts[0].dtype))
    return jnp.concatenate(flat).reshape(total_rows, D)


SHARD = {"w_in": (D, 706), "w_out": (256, D), "w_mq": (128, D), "w_mk": (128, D), "w_mv": (128, D),
         "w_mo": (128, D), "w_up": (D, 704), "w_down": (352, D), "conv_ssd_w": (4, 192), "conv_ffn_w": (3, 704)}
GATHER_MID = ["w_out", "w_mq", "w_mk", "w_mv", "w_mo"]
GATHER_FFN = ["w_down"]
CONV_TAPS = ["conv_ssd_w", "conv_ffn_w"]
GRADS_PACKED = ["w_out", "w_mq", "w_mk", "w_mv", "w_mo", "w_down", "conv_ffn_w", "conv_ssd_w"]


def _layout(names):
    row0, r = {}, 0
    for n in names:
        row0[n] = r
        r += _part_rows(SHARD[n])
    return row0, r + (-r) % 128


SMALL = [("norm_mix_w", (1, D)), ("conv_ssd_b", (1, 1536)), ("dt_bias", (1, 16)), ("a_log", (1, 16)),
         ("d_skip", (1, 16)), ("ssd_norm_w", (1, D)), ("sb_norm_w", (1, D)), ("norm_mem_w", (1, D)),
         ("norm_memkv_w", (1, D)), ("norm_ffn_w", (1, D)), ("conv_ffn_b", (1, 5632)), ("norm_final_w", (D,))]
LOSS_ROW = sum(_tight_rows(_shp) for _, _shp in SMALL)
SMALL_ROWS = LOSS_ROW + 1 + (-(LOSS_ROW + 1)) % 8
ORDER = ["norm_mix_w", "w_in", "conv_ssd_w", "conv_ssd_b", "dt_bias", "a_log", "d_skip", "ssd_norm_w",
         "sb_norm_w", "w_out", "norm_mem_w", "norm_memkv_w", "w_mq", "w_mk", "w_mv", "w_mo", "norm_ffn_w",
         "w_up", "conv_ffn_w", "conv_ffn_b", "w_down", "norm_final_w"]


def _pad_rows(a, nr):
    n = a.shape[1]
    return jnp.concatenate([a, jnp.zeros((NDEV, nr * D - n), a.dtype)], axis=1).reshape(NDEV, nr, D)


def _group_sum(lanes):
    return lanes.reshape(16, 64).sum(axis=1).reshape(1, 16)


def kernel(x, mem, norm_mix_w, w_in, conv_ssd_w, conv_ssd_b, dt_bias, a_log, d_skip, ssd_norm_w, sb_norm_w, w_out, norm_mem_w, norm_memkv_w, w_mq, w_mk, w_mv, w_mo, norm_ffn_w, w_up, conv_ffn_w, conv_ffn_b, w_down, norm_final_w, loss_target, m_norm_mix_w, m_w_in, m_conv_ssd_w, m_conv_ssd_b, m_dt_bias, m_a_log, m_d_skip, m_ssd_norm_w, m_sb_norm_w, m_w_out, m_norm_mem_w, m_norm_memkv_w, m_w_mq, m_w_mk, m_w_mv, m_w_mo, m_norm_ffn_w, m_w_up, m_conv_ffn_w, m_conv_ffn_b, m_w_down, m_norm_final_w, v_norm_mix_w, v_w_in, v_conv_ssd_w, v_conv_ssd_b, v_dt_bias, v_a_log, v_d_skip, v_ssd_norm_w, v_sb_norm_w, v_w_out, v_norm_mem_w, v_norm_memkv_w, v_w_mq, v_w_mk, v_w_mv, v_w_mo, v_norm_ffn_w, v_w_up, v_conv_ffn_w, v_conv_ffn_b, v_w_down, v_norm_final_w):
    P = dict(norm_mix_w=norm_mix_w, w_in=w_in, conv_ssd_w=conv_ssd_w, conv_ssd_b=conv_ssd_b, dt_bias=dt_bias, a_log=a_log, d_skip=d_skip, ssd_norm_w=ssd_norm_w, sb_norm_w=sb_norm_w, w_out=w_out, norm_mem_w=norm_mem_w, norm_memkv_w=norm_memkv_w, w_mq=w_mq, w_mk=w_mk, w_mv=w_mv, w_mo=w_mo, norm_ffn_w=norm_ffn_w, w_up=w_up, conv_ffn_w=conv_ffn_w, conv_ffn_b=conv_ffn_b, w_down=w_down, norm_final_w=norm_final_w)
    M = dict(norm_mix_w=m_norm_mix_w, w_in=m_w_in, conv_ssd_w=m_conv_ssd_w, conv_ssd_b=m_conv_ssd_b, dt_bias=m_dt_bias, a_log=m_a_log, d_skip=m_d_skip, ssd_norm_w=m_ssd_norm_w, sb_norm_w=m_sb_norm_w, w_out=m_w_out, norm_mem_w=m_norm_mem_w, norm_memkv_w=m_norm_memkv_w, w_mq=m_w_mq, w_mk=m_w_mk, w_mv=m_w_mv, w_mo=m_w_mo, norm_ffn_w=m_norm_ffn_w, w_up=m_w_up, conv_ffn_w=m_conv_ffn_w, conv_ffn_b=m_conv_ffn_b, w_down=m_w_down, norm_final_w=m_norm_final_w)
    V = dict(norm_mix_w=v_norm_mix_w, w_in=v_w_in, conv_ssd_w=v_conv_ssd_w, conv_ssd_b=v_conv_ssd_b, dt_bias=v_dt_bias, a_log=v_a_log, d_skip=v_d_skip, ssd_norm_w=v_ssd_norm_w, sb_norm_w=v_sb_norm_w, w_out=v_w_out, norm_mem_w=v_norm_mem_w, norm_memkv_w=v_norm_memkv_w, w_mq=v_w_mq, w_mk=v_w_mk, w_mv=v_w_mv, w_mo=v_w_mo, norm_ffn_w=v_norm_ffn_w, w_up=v_w_up, conv_ffn_w=v_conv_ffn_w, conv_ffn_b=v_conv_ffn_b, w_down=v_w_down, norm_final_w=v_norm_final_w)
    small_shapes = [shp for _, shp in SMALL]

    def packed(src, names, dtype=F32):
        return _pack([src[n][0] for n in names], _layout(names)[1]).astype(dtype)

    def columns(g):
        return g.transpose(1, 0, 2).reshape(g.shape[1], NDEV * g.shape[2])

    g_in, g_taps = _all_gather([w_in[0].astype(BF16), packed(P, CONV_TAPS)], name="gather_w_in")
    W_in = columns(g_in)
    cw_ssd = g_taps[:, 0].reshape(NDEV, -1)[:, :768].reshape(NDEV, 4, 192).transpose(1, 0, 2).reshape(4, XBC)
    cw_ffn = (g_taps[:, PACK_ALIGN:PACK_ALIGN + 3].reshape(NDEV, -1)[:, :2112].reshape(NDEV, 3, 704)
              .transpose(1, 0, 2).reshape(3, 2 * DFF))
    W_z, W_xbc, W_dt, W_qkv = W_in[:, :D], W_in[:, D:D + XBC], W_in[:, D + XBC:D + XBC + 16], W_in[:, D + XBC + 16:]
    W_dtr = jnp.repeat(W_dt, 64, axis=1)
    cwg, cwv = cw_ffn[:, :DFF], cw_ffn[:, DFF:]
    cbg, cbv = conv_ffn_b[:, :DFF], conv_ffn_b[:, DFF:]
    rep = lambda p: jnp.repeat(p, 64, axis=1)
    lanes = jnp.concatenate([rep(dt_bias), rep(a_log), rep(d_skip), ssd_norm_w, jnp.zeros((4, D), F32)], axis=0)

    xs, tgt, mm = x[0], loss_target[0], mem[0]

    h1 = _norm_fwd(xs, norm_mix_w, name="norm_mix")
    z = _mm(h1, W_z, name="proj_z")
    xbc = _mm(h1, W_xbc, name="proj_xbc")
    dtr = _mm(h1, W_dtr, name="proj_dt")
    qkv = _mm(h1, W_qkv, name="proj_qkv", out_dtype=BF16)
    w_up16 = w_up[0].astype(BF16)
    y_ssd, states, g_ffn, g_up0 = _ssd_fwd(z, xbc, dtr, cw_ssd, conv_ssd_b, lanes,
                                           [packed(P, GATHER_FFN, BF16), w_up16[:D // 2]], name="ssd_fwd")
    o_sb, g_mid, g_up1 = _sb_fwd(qkv, [packed(P, GATHER_MID, BF16), w_up16[D // 2:]], name="sb_fwd")
    g_up = jnp.concatenate([g_up0, g_up1], axis=1)
    r_mid = _layout(GATHER_MID)[0]
    W_out = g_mid[:, r_mid["w_out"]:r_mid["w_out"] + 256].reshape(2 * D, D)
    W_mq, W_mk, W_mv, W_mo = [g_mid[:, r_mid[n]:r_mid[n] + 128].reshape(D, D)
                              for n in ("w_mq", "w_mk", "w_mv", "w_mo")]
    W_up = columns(g_up)
    W_down = g_ffn[:, 0:352].reshape(DFF, D)
    W_upg, W_upv = W_up[:, :DFF], W_up[:, DFF:]
    y_sb = _head_norm_fwd(o_sb, sb_norm_w, name="sb_norm")
    ymix = jnp.concatenate([y_ssd, y_sb], axis=1)
    x1 = _mm(ymix, W_out, add=xs, name="proj_out")
    h2 = _norm_fwd(x1, norm_mem_w, name="norm_mem")
    mn = _norm_fwd(mm, norm_memkv_w, name="norm_memkv")
    qm = _mm(h2, W_mq, name="mem_q", out_dtype=BF16)
    km = _mm(mn, W_mk, name="mem_k", out_dtype=BF16)
    vm = _mm(mn, W_mv, name="mem_v", out_dtype=BF16)
    om = _mem_attn_fwd(qm, km, vm, name="mem_attn")
    x2 = _mm(om, W_mo, add=x1, name="mem_o")
    h3 = _norm_fwd(x2, norm_ffn_w, name="norm_ffn")
    ug = _mm(h3, W_upg, name="ffn_up_g")
    uv = _mm(h3, W_upv, name="ffn_up_v")
    f = _glu_fwd(ug, uv, cwg, cwv, cbg, cbv, name="ffn_glu")
    x3 = _mm(f, W_down, add=x2, name="ffn_down")
    dx3, g_nfinal, loss_part = _final(x3, norm_final_w.reshape(1, D), tgt, name="final_loss")

    G = {}
    G["w_down"] = _mm(f, dx3, trans_a=True, name="g_w_down")
    df = _mm(dx3, W_down, trans_b=True, name="d_f")
    dupg, dupv, dcg, dcv = _ffn_bwd(ug, uv, df, cwg, cwv, cbg, cbv, name="ffn_glu_bwd")
    G["w_up"] = jnp.concatenate([_mm(h3, dupg, trans_a=True, name="g_w_up_g"),
                                 _mm(h3, dupv, trans_a=True, name="g_w_up_v")], axis=1)
    G["conv_ffn_w"] = jnp.concatenate([dcg[0:3], dcv[0:3]], axis=1)
    G["conv_ffn_b"] = jnp.concatenate([dcg[3:4], dcv[3:4]], axis=1)
    dh3 = _mm(dupg, W_upg, trans_b=True, name="d_h3_g")
    dh3 = _mm(dupv, W_upv, trans_b=True, add=dh3, name="d_h3_v")
    dx2, G["norm_ffn_w"] = _norm_bwd(x2, norm_ffn_w, dh3, dx3, name="norm_ffn_bwd")
    G["w_mo"] = _mm(om, dx2, trans_a=True, name="g_w_mo")
    dom = _mm(dx2, W_mo, trans_b=True, name="d_om")
    dqm, dkm, dvm = _mem_attn_bwd(qm, km, vm, dom, name="mem_attn_bwd")
    G["w_mq"] = _mm(h2, dqm, trans_a=True, name="g_w_mq")
    G["w_mk"] = _mm(mn, dkm, trans_a=True, name="g_w_mk")
    G["w_mv"] = _mm(mn, dvm, trans_a=True, name="g_w_mv")
    dh2 = _mm(dqm, W_mq, trans_b=True, name="d_h2")
    dmn = _mm(dkm, W_mk, trans_b=True, name="d_mn_k")
    dmn = _mm(dvm, W_mv, trans_b=True, add=dmn, name="d_mn_v")
    _, G["norm_memkv_w"] = _norm_bwd(mm, norm_memkv_w, dmn, None, name="norm_memkv_bwd")
    dx1, G["norm_mem_w"] = _norm_bwd(x1, norm_mem_w, dh2, dx2, name="norm_mem_bwd")
    G["w_out"] = _mm(ymix, dx1, trans_a=True, name="g_w_out")
    dymix = _mm(dx1, W_out, trans_b=True, name="d_ymix")
    do_sb, G["sb_norm_w"] = _head_norm_bwd(o_sb, sb_norm_w, dymix, name="sb_norm_bwd")

    dz, dxbc, ddtr, dlanes, dconv = _ssd_bwd(z, xbc, dtr, states, dymix, cw_ssd, conv_ssd_b, lanes, name="ssd_bwd")
    G["dt_bias"], G["a_log"], G["d_skip"] = [_group_sum(dlanes[i:i + 1]) for i in range(3)]
    G["ssd_norm_w"] = dlanes[3:4]
    G["conv_ssd_w"], G["conv_ssd_b"] = dconv[0:4], dconv[4:5]

    def col_slabs(g, cols):
        return g.reshape(g.shape[0], NDEV, cols).transpose(1, 0, 2).astype(BF16)

    def packed_slabs(names):
        parts = []
        for n in names:
            shp = SHARD[n]
            if shp[-1] == D:
                t = G[n].reshape((NDEV,) + shp)
            else:
                t = _pad_rows(G[n].reshape(shp[0], NDEV, shp[1]).transpose(1, 0, 2).reshape(NDEV, -1),
                              _part_rows(shp))
            parts.append(jnp.pad(t, ((0, 0), (0, _part_rows(shp) - t.shape[1]), (0, 0))))
        used = sum(t.shape[1] for t in parts)
        parts.append(jnp.zeros((NDEV, _layout(names)[1] - used, D), F32))
        return jnp.concatenate(parts, axis=1).astype(BF16)

    dq, dk, dv, recv_packed, recv_up = _sb_bwd(qkv, o_sb, do_sb, [packed_slabs(GRADS_PACKED), col_slabs(G["w_up"], 704)],
                                               name="sb_bwd")
    dproj = jnp.concatenate([dz, dxbc, ddtr, dq, dk, dv], axis=1)
    g_proj = _mm(h1, dproj, trans_a=True, name="g_w_in")
    c_dt = D + XBC
    G["w_in"] = jnp.concatenate([g_proj[:, :c_dt], g_proj[:, c_dt:c_dt + D].reshape(D, 16, 64).sum(axis=2),
                                 g_proj[:, c_dt + D:]], axis=1)
    W_proj = jnp.concatenate([W_z, W_xbc, W_dtr, W_qkv], axis=1)
    dh1, recv_in = _mm(dproj, W_proj, trans_b=True, rides=[col_slabs(G["w_in"], 706)], name="d_h1")
    dx, G["norm_mix_w"] = _norm_bwd(xs, norm_mix_w, dh1, dx1, name="norm_mix_bwd")
    G["norm_final_w"] = g_nfinal.reshape(D)

    small_g = _pack_tight([G[n] for n, _ in SMALL] + [loss_part], SMALL_ROWS)
    (parts_small,) = _exchange([small_g], [True], name="exchange_grads")
    outs_packed = _adamw(recv_packed, packed(P, GRADS_PACKED), packed(M, GRADS_PACKED), packed(V, GRADS_PACKED),
                         name="adamw_packed")
    outs_up = _adamw(recv_up, w_up[0], m_w_up[0], v_w_up[0], name="adamw_w_up")
    outs_in = _adamw(recv_in, w_in[0], m_w_in[0], v_w_in[0], name="adamw_w_in")
    outs_small = _adamw(parts_small, _pack_tight([P[n] for n, _ in SMALL], SMALL_ROWS),
                        _pack_tight([M[n] for n, _ in SMALL], SMALL_ROWS),
                        _pack_tight([V[n] for n, _ in SMALL], SMALL_ROWS), name="adamw_replicated")

    res = {}
    for i, kind in enumerate(("grad", "delta", "new_m", "new_v")):
        for n, val in zip(GRADS_PACKED, _unpack(outs_packed[i], [SHARD[n] for n in GRADS_PACKED])):
            res[kind, n] = val.reshape((1,) + SHARD[n])
        res[kind, "w_up"] = outs_up[i].reshape((1,) + SHARD["w_up"])
        res[kind, "w_in"] = outs_in[i].reshape((1,) + SHARD["w_in"])
        for (n, shp), val in zip(SMALL, _unpack(outs_small[i], small_shapes, _tight_rows)):
            res[kind, n] = val
    loss = outs_small[0][LOSS_ROW, 0]
    out = [loss, dx.reshape(1, -1, D)]
    for kind in ("grad", "delta", "new_m", "new_v"):
        out += [res[kind, n] for n in ORDER]
    return tuple(out)
```

```python
import functools
import math

import jax
import jax.numpy as jnp
from jax import lax
from jax.experimental import pallas as pl
from jax.experimental.pallas import tpu as pltpu

F32 = jnp.float32
BF16 = jnp.bfloat16

D = 1024
NDEV = 8
EPS = 1e-6
SSD_CHUNK = 128
HALO = 8
VMEM_LIMIT = 56 * 2**20

ADAM_LR, ADAM_B1, ADAM_B2, ADAM_EPS, ADAM_WD, ADAM_STEP = 0.001, 0.9, 0.999, 1e-08, 0.01, 10


def _cp(*sem):
    return pltpu.CompilerParams(dimension_semantics=sem, vmem_limit_bytes=VMEM_LIMIT)


def _tile(n, cap, mult):
    if n <= cap:
        return n
    for d in range(cap - cap % mult, 0, -mult):
        if n % d == 0:
            return d
    raise ValueError(f"no tile for {n}")


def _sigmoid(x):
    return 1.0 / (1.0 + jnp.exp(-x))


def _silu(x):
    return x * _sigmoid(x)


def _softplus(x):
    return jnp.maximum(x, 0.0) + jnp.log(1.0 + jnp.exp(-jnp.abs(x)))


def _terms(x, n):
    out = []
    r = x.astype(F32)
    for i in range(n):
        h = r.astype(BF16)
        out.append(h)
        if i + 1 < n:
            r = r - h.astype(F32)
    return out


_DIMS = {"nn": ((1,), (0,)), "nt": ((1,), (1,)), "tn": ((0,), (0,))}


def _dot_raw(form, a, b, ta, tb):
    acc = None
    for ai in _terms(a, ta):
        for bi in _terms(b, tb):
            d = lax.dot_general(ai, bi, (_DIMS[form], ((), ())), preferred_element_type=F32)
            acc = d if acc is None else acc + d
    return acc


@functools.lru_cache(maxsize=None)
def _dot_fn(form, ta, tb):
    @jax.custom_vjp
    def f(a, b):
        return _dot_raw(form, a, b, ta, tb)

    def fwd(a, b):
        return f(a, b), (a, b)

    def bwd(res, ct):
        a, b = res
        if form == "nn":
            return _dot_fn("nt", ta, tb)(ct, b), _dot_fn("tn", ta, tb)(a, ct)
        if form == "nt":
            return _dot_fn("nn", ta, tb)(ct, b), _dot_fn("tn", tb, ta)(ct, a)
        return _dot_fn("nt", tb, ta)(b, ct), _dot_fn("nn", ta, tb)(a, ct)

    f.defvjp(fwd, bwd)
    return f


def _dot(form, a, b, ta=1, tb=1):
    return _dot_fn(form, ta, tb)(a, b)


@functools.lru_cache(maxsize=None)
def _take_fn(axis, idx):
    @jax.custom_vjp
    def f(x):
        return x[:, idx:idx + 1] if axis == 1 else x[idx:idx + 1, :]

    def fwd(x):
        return f(x), x.shape

    def bwd(shape, ct):
        io = lax.broadcasted_iota(jnp.int32, shape, axis)
        return (jnp.where(io == idx, jnp.broadcast_to(ct, shape), 0.0),)

    f.defvjp(fwd, bwd)
    return f


@functools.lru_cache(maxsize=None)
def _split_fn(width, n):
    @jax.custom_vjp
    def f(x):
        return tuple(x[:, i * width:(i + 1) * width] for i in range(n))

    def fwd(x):
        return f(x), None

    def bwd(_, cts):
        return (jnp.concatenate(list(cts), axis=1),)

    f.defvjp(fwd, bwd)
    return f


def _split(x, width):
    return _split_fn(width, x.shape[1] // width)(x)


def _iota(shape, axis):
    return lax.broadcasted_iota(jnp.int32, shape, axis)


MM_VMEM_BUDGET = 44 * 2**20


def _mm_tiles(m, n, kt, trans_a, a_bytes, b_bytes, out_bytes, add_bytes):
    tn = _tile(n, 1536, 128)
    for tm_cap in (1408, 1024, 512, 256, 128):
        tm = _tile(m, tm_cap, 128 if trans_a else 8)
        for tk_cap in (kt, 4096, 2048, 1024, 512):
            tk = _tile(kt, tk_cap, 128)
            blocks = tm * tk * a_bytes + tk * tn * b_bytes + tm * tn * (out_bytes + add_bytes)
            if 2 * blocks + (tm * tn * 4 if tk < kt else 0) <= MM_VMEM_BUDGET:
                return tm, tn, tk
    raise ValueError(f"no matmul tiling for {(m, n, kt)}")


def _mm(a, b, *, name, add=None, trans_a=False, trans_b=False, out_dtype=F32, rides=()):
    assert not (trans_a and trans_b)
    if trans_a:
        kt, m = a.shape
    else:
        m, kt = a.shape
    n, kt2 = b.shape if trans_b else b.shape[::-1]
    assert kt == kt2, (a.shape, b.shape)
    tm, tn, tk = _mm_tiles(m, n, kt, trans_a, a.dtype.itemsize, b.dtype.itemsize,
                           jnp.dtype(out_dtype).itemsize, 0 if add is None else add.dtype.itemsize)
    nk = kt // tk
    grid = (m // tm, n // tn, nk)
    rd = _Rides(rides, [False] * len(rides))
    n_in = 2 if add is None else 3

    def body(*all_refs):
        ins, (o_ref,), scratch, handles = rd.split(all_refs, n_in, 1, 1 if nk > 1 else 0)
        refs = (*ins, o_ref, *scratch)
        if rides:
            ids = [pl.program_id(ax) for ax in range(3)]
            rd.run(handles, (ids[0] == 0) & (ids[1] == 0) & (ids[2] == 0),
                   (ids[0] == grid[0] - 1) & (ids[1] == grid[1] - 1) & (ids[2] == grid[2] - 1))
        if add is None:
            a_ref, b_ref, o_ref = refs[:3]
        else:
            a_ref, b_ref, add_ref, o_ref = refs[:4]
        k = pl.program_id(2)
        av = a_ref[...].astype(BF16)
        bv = b_ref[...].astype(BF16)
        dims = _DIMS["tn" if trans_a else "nt" if trans_b else "nn"]
        d = lax.dot_general(av, bv, (dims, ((), ())), preferred_element_type=F32)

        def finish(r):
            if add is not None:
                r = r + add_ref[...]
            o_ref[...] = r.astype(out_dtype)

        if nk == 1:
            finish(d)
        else:
            acc = refs[-1]

            @pl.when(k == 0)
            def _():
                acc[...] = d

            @pl.when((k > 0) & (k < nk - 1))
            def _():
                acc[...] += d

            @pl.when(k == nk - 1)
            def _():
                finish(acc[...] + d)

    a_spec = (pl.BlockSpec((tk, tm), lambda i, j, k: (k, i)) if trans_a
              else pl.BlockSpec((tm, tk), lambda i, j, k: (i, k)))
    b_spec = (pl.BlockSpec((tn, tk), lambda i, j, k: (j, k)) if trans_b
              else pl.BlockSpec((tk, tn), lambda i, j, k: (k, j)))
    in_specs = [a_spec, b_spec]
    args = [a, b]
    if add is not None:
        in_specs.append(pl.BlockSpec((tm, tn), lambda i, j, k: (i, j)))
        args.append(add)
    out = pl.pallas_call(
        body, name=name, grid=grid,
        in_specs=in_specs + rd.in_specs,
        out_specs=[pl.BlockSpec((tm, tn), lambda i, j, k: (i, j))] + rd.out_specs,
        out_shape=[jax.ShapeDtypeStruct((m, n), out_dtype)] + rd.out_shape,
        scratch_shapes=([pltpu.VMEM((tm, tn), F32)] if nk > 1 else []) + rd.scratch,
        compiler_params=_cp(*(("arbitrary",) * 3 if rides else ("parallel", "parallel", "arbitrary"))),
    )(*args, *rides)
    return out if rides else out[0]


def _rstd(x):
    return lax.rsqrt(jnp.mean(x * x, axis=-1, keepdims=True) + EPS)


def _norm_fwd(x, w, *, name):
    s = x.shape[0]
    tm = _tile(s, 512, 8)

    def body(x_ref, w_ref, o_ref):
        xv = x_ref[...]
        o_ref[...] = (xv * _rstd(xv) * w_ref[...]).astype(BF16)

    return pl.pallas_call(
        body, name=name, grid=(s // tm,),
        in_specs=[pl.BlockSpec((tm, D), lambda i: (i, 0)), pl.BlockSpec((1, D), lambda i: (0, 0))],
        out_specs=pl.BlockSpec((tm, D), lambda i: (i, 0)),
        out_shape=jax.ShapeDtypeStruct((s, D), BF16), compiler_params=_cp("parallel"),
    )(x, w)


def _norm_bwd_math(xv, wv, dy):
    r = _rstd(xv)
    xh = xv * r
    dxh = dy * wv
    dx = r * (dxh - xh * jnp.mean(dxh * xh, axis=-1, keepdims=True))
    dw = jnp.sum(dy * xh, axis=0, keepdims=True)
    return dx, dw


def _norm_bwd(x, w, dy, add, *, name):
    s = x.shape[0]
    tm = _tile(s, 512, 8)

    def body(*refs):
        if add is None:
            x_ref, w_ref, dy_ref, dx_ref, dw_ref = refs
        else:
            x_ref, w_ref, dy_ref, add_ref, dx_ref, dw_ref = refs

        @pl.when(pl.program_id(0) == 0)
        def _():
            dw_ref[...] = jnp.zeros_like(dw_ref)

        dx, dw = _norm_bwd_math(x_ref[...], w_ref[...], dy_ref[...])
        if add is not None:
            dx = dx + add_ref[...]
        dx_ref[...] = dx
        dw_ref[...] += dw

    row = pl.BlockSpec((tm, D), lambda i: (i, 0))
    vec = pl.BlockSpec((1, D), lambda i: (0, 0))
    in_specs = [row, vec, row] + ([row] if add is not None else [])
    args = [x, w, dy] + ([add] if add is not None else [])
    return pl.pallas_call(
        body, name=name, grid=(s // tm,), in_specs=in_specs, out_specs=[row, vec],
        out_shape=[jax.ShapeDtypeStruct((s, D), F32), jax.ShapeDtypeStruct((1, D), F32)],
        compiler_params=_cp("arbitrary"),
    )(*args)


def _final(x3, w, target, *, name):
    s = x3.shape[0]
    tm = _tile(s, 512, 8)

    def body(x_ref, w_ref, t_ref, dx_ref, dw_ref, loss_ref):
        @pl.when(pl.program_id(0) == 0)
        def _():
            dw_ref[...] = jnp.zeros_like(dw_ref)
            loss_ref[...] = jnp.zeros_like(loss_ref)

        xv = x_ref[...]
        wv = w_ref[...]
        y = xv * _rstd(xv) * wv
        err = y - t_ref[...]
        loss_ref[...] += 0.5 * jnp.sum(jnp.mean(err * err, axis=-1, keepdims=True))
        dx, dw = _norm_bwd_math(xv, wv, err * (1.0 / D))
        dx_ref[...] = dx
        dw_ref[...] += dw

    row = pl.BlockSpec((tm, D), lambda i: (i, 0))
    vec = pl.BlockSpec((1, D), lambda i: (0, 0))
    return pl.pallas_call(
        body, name=name, grid=(s // tm,), in_specs=[row, vec, row], out_specs=[row, vec, vec],
        out_shape=[jax.ShapeDtypeStruct((s, D), F32), jax.ShapeDtypeStruct((1, D), F32),
                   jax.ShapeDtypeStruct((1, D), F32)],
        compiler_params=_cp("arbitrary"),
    )(x3, w, target)


def _head_norm_math(o, w):
    lane = _iota((128, 128), 0) // 64
    bd = (lane == _iota((128, 128), 1) // 64).astype(F32)
    outs = []
    for op in _split(o, 128):
        ms = _dot("nn", op * op, bd, 2, 1) * (1.0 / 64)
        outs.append(op * lax.rsqrt(ms + EPS))
    return jnp.concatenate(outs, axis=1) * w


def _head_norm_fwd(o, w, *, name):
    s = o.shape[0]
    tm = _tile(s, 256, 8)

    def body(o_ref, w_ref, y_ref):
        y_ref[...] = _head_norm_math(o_ref[...], w_ref[...]).astype(BF16)

    row = pl.BlockSpec((tm, D), lambda i: (i, 0))
    vec = pl.BlockSpec((1, D), lambda i: (0, 0))
    return pl.pallas_call(
        body, name=name, grid=(s // tm,), in_specs=[row, vec], out_specs=row,
        out_shape=jax.ShapeDtypeStruct((s, D), BF16), compiler_params=_cp("parallel"),
    )(o, w)


def _head_norm_bwd(o, w, dymix, *, name):
    s = o.shape[0]
    tm = _tile(s, 256, 8)

    def body(o_ref, w_ref, dy_ref, do_ref, dw_ref):
        @pl.when(pl.program_id(0) == 0)
        def _():
            dw_ref[...] = jnp.zeros_like(dw_ref)

        _, vjp = jax.vjp(_head_norm_math, o_ref[...], w_ref[...])
        do, dw = vjp(dy_ref[...])
        do_ref[...] = do
        dw_ref[...] += dw

    row = pl.BlockSpec((tm, D), lambda i: (i, 0))
    vec = pl.BlockSpec((1, D), lambda i: (0, 0))
    return pl.pallas_call(
        body, name=name, grid=(s // tm,),
        in_specs=[row, vec, pl.BlockSpec((tm, D), lambda i: (i, 1))], out_specs=[row, vec],
        out_shape=[jax.ShapeDtypeStruct((s, D), F32), jax.ShapeDtypeStruct((1, D), F32)],
        compiler_params=_cp("arbitrary"),
    )(o, w, dymix)


SB_BQ = 256
SB_BK = 256


def _sb_consts():
    r = _iota((SB_BK, SB_BK), 0)
    c = _iota((SB_BK, SB_BK), 1)
    u_excl = (r > c).astype(BF16)
    u_incl = (r >= c).astype(BF16)
    return u_excl, u_incl


SB_LANES = 256
SB_NCH = SB_LANES // 64


def _nt(a, b):
    return lax.dot_general(a, b, (_DIMS["nt"], ((), ())), preferred_element_type=F32)


def _tn(a, b):
    return lax.dot_general(a, b, (_DIMS["tn"], ((), ())), preferred_element_type=F32)


def _nn(a, b):
    return jnp.dot(a, b, preferred_element_type=F32)


def _sb_heads(ref):
    out = []
    for hp in range(SB_LANES // 128):
        v = ref[:, 128 * hp:128 * (hp + 1)]
        first = _iota(v.shape, 1) < 64
        out += [jnp.where(first, v, 0).astype(BF16), jnp.where(first, 0, v).astype(BF16)]
    return out


SB_STRIP = 32


def _neg_abs(x):
    bits = lax.bitcast_convert_type(x, jnp.uint32) | jnp.uint32(0x80000000)
    return lax.bitcast_convert_type(bits, F32)


def _sb_block(ref, j):
    off = pl.multiple_of(j * SB_BK, SB_BK)
    return [ref[pl.ds(off, SB_BK), 128 * hp:128 * (hp + 1)] for hp in range(SB_NCH // 2)]


SB_DEAD = 104.0


def _sb_live(nlrun):
    m = nlrun[0]
    for x in nlrun[1:]:
        m = jnp.minimum(m, x)
    return jnp.min(m) < SB_DEAD


def _sb_strips():
    return [(r, pl.ds(r, SB_STRIP)) for r in range(0, SB_BQ, SB_STRIP)]


def _sb_diag_mask(r):
    return _iota((SB_STRIP, SB_BK), 1) < _iota((SB_STRIP, SB_BK), 0) + r


def _sb_soft(z, mask):
    e = jnp.exp(_neg_abs(z))
    nl = jnp.maximum(z, 0.0) + jnp.log(1.0 + e)
    if mask is not None:
        nl = jnp.where(mask, nl, 0.0)
    return e, nl


def _sb_split_to(hl_ref, rows, x):
    hi, lo = _terms(x, 2)
    hl_ref[rows, 0:SB_BK] = hi
    hl_ref[rows, SB_BK:2 * SB_BK] = lo


def _sb_stage_soft(z_ref, nl_ref, diag):
    for r, rows in _sb_strips():
        _, nl = _sb_soft(z_ref[rows, :], _sb_diag_mask(r) if diag else None)
        nl_ref[rows, 0:SB_BK] = nl.astype(BF16)


def _sb_stage_weights(z_ref, c_ref, a_ref, nlrun, diag):
    for r, rows in _sb_strips():
        a = jnp.exp(z_ref[rows, :] - c_ref[rows, :] - nlrun[r:r + SB_STRIP, :])
        if diag:
            a = jnp.where(_sb_diag_mask(r), a, 0.0)
        a_ref[rows, :] = a.astype(BF16)


def _sb_fwd(qkv, rides, *, name):
    s = qkv.shape[0]
    nq = s // SB_BQ
    ng = D // SB_LANES
    assert SB_BQ == SB_BK
    rd = _Rides(rides, [True] * len(rides))

    def body(*refs):
        (q_ref, k_ref, v_ref), (o_ref,), (zbuf, nlbuf, cbuf, abuf), handles = rd.split(refs, 3, 1, 4)
        i = pl.program_id(1)
        step_no = pl.program_id(0) * nq + i
        rd.run(handles, step_no == 0, step_no == ng * nq - 1)

        _, u_incl = _sb_consts()
        lane_a = _iota((SB_BQ, 128), 1) < 64
        qh = [q * 0.125 for q in _sb_heads(q_ref)]

        def tile(j, accs, nlrun, diag):
            kbs = _sb_block(k_ref, j)
            for c in range(SB_NCH):
                zbuf[c] = _nt(qh[c], kbs[c // 2])
            for c in range(SB_NCH):
                _sb_stage_soft(zbuf.at[c], nlbuf.at[c], diag)
                cbuf[c] = _nn(nlbuf[c], u_incl)
            for c in range(SB_NCH):
                _sb_stage_weights(zbuf.at[c], cbuf.at[c], abuf.at[c], nlrun[c], diag)
            nlrun = tuple(nlrun[c] + cbuf[c, :, 0:1] for c in range(SB_NCH))
            vbs = _sb_block(v_ref, j)
            outs = [_nn(abuf[c], vbs[c // 2]) for c in range(SB_NCH)]
            accs = tuple(acc + jnp.where(lane_a, outs[2 * hp], outs[2 * hp + 1]) for hp, acc in enumerate(accs))
            return accs, nlrun

        accs, nlrun = tile(i, (jnp.zeros((SB_BQ, 128), F32),) * (SB_NCH // 2),
                           (jnp.zeros((SB_BQ, 1), F32),) * SB_NCH, True)

        def step(carry):
            j, _, accs, nlrun = carry
            accs, nlrun = tile(j, accs, nlrun, False)
            return j - 1, _sb_live(nlrun), accs, nlrun

        _, _, accs, _ = lax.while_loop(lambda c: (c[0] >= 0) & c[1], step, (i - 1, _sb_live(nlrun), accs, nlrun))
        o_ref[...] = jnp.concatenate(accs, axis=1)

    return pl.pallas_call(
        body, name=name, grid=(ng, nq),
        in_specs=[pl.BlockSpec((SB_BQ, SB_LANES), lambda g, i: (i, g)),
                  pl.BlockSpec((s, SB_LANES), lambda g, i: (0, ng + g)),
                  pl.BlockSpec((s, SB_LANES), lambda g, i: (0, 2 * ng + g)), *rd.in_specs],
        out_specs=[pl.BlockSpec((SB_BQ, SB_LANES), lambda g, i: (i, g)), *rd.out_specs],
        out_shape=[jax.ShapeDtypeStruct((s, D), F32), *rd.out_shape],
        scratch_shapes=[pltpu.VMEM((SB_NCH, SB_BQ, SB_BK), F32), pltpu.VMEM((SB_NCH, SB_BQ, SB_BK), BF16),
                        pltpu.VMEM((SB_NCH, SB_BQ, SB_BK), F32), pltpu.VMEM((SB_NCH, SB_BQ, SB_BK), BF16),
                        *rd.scratch],
        compiler_params=_cp("arbitrary", "arbitrary"),
    )(qkv, qkv, qkv, *rides)


def _sb_bwd(qkv, o, do, rides, *, name):
    s = qkv.shape[0]
    nq = s // SB_BQ
    ng = D // SB_LANES
    nhp = SB_NCH // 2
    rd = _Rides(rides, [False] * len(rides))

    def body(*refs):
        ins, outs, scratch, handles = rd.split(refs, 5, 3, 11)
        q_ref, k_ref, v_ref, o_ref, do_ref = ins
        dq_ref, dk_hbm, dv_hbm = outs
        dk_acc, dv_acc, dk16, dv16, sems, zbuf, gbuf, hl, cbuf, abuf, dzbuf = scratch
        g_idx = pl.program_id(0)
        i = pl.program_id(1)
        step_no = g_idx * nq + i
        rd.run(handles, step_no == 0, step_no == ng * nq - 1)

        @pl.when(i == 0)
        def _():
            dk_acc[...] = jnp.zeros_like(dk_acc)
            dv_acc[...] = jnp.zeros_like(dv_acc)

        _, u_incl = _sb_consts()
        u2 = jnp.concatenate([u_incl, u_incl], axis=0)
        lane_a = _iota((SB_BQ, 128), 1) < 64
        lane_k = _iota((SB_BK, 128), 1) < 64
        qh = [q * 0.125 for q in _sb_heads(q_ref)]
        qf = [q_ref[:, 128 * hp:128 * (hp + 1)] for hp in range(nhp)]
        doh = _sb_heads(do_ref)
        dof = [do_ref[:, 128 * hp:128 * (hp + 1)].astype(BF16) for hp in range(nhp)]
        delta = []
        for hp in range(nhp):
            prod = dof[hp].astype(F32) * o_ref[:, 128 * hp:128 * (hp + 1)]
            delta += [jnp.sum(jnp.where(lane_a, prod, 0.0), axis=1, keepdims=True),
                      jnp.sum(jnp.where(lane_a, 0.0, prod), axis=1, keepdims=True)]

        def pre(slot, j):
            kbs = _sb_block(k_ref, j)
            vbs = _sb_block(v_ref, j)
            for c in range(SB_NCH):
                zbuf[slot, c] = _nt(qh[c], kbs[c // 2])
                gbuf[slot, c] = _nt(doh[c], vbs[c // 2])

        def stage_g(c, slot):
            for _, rows in _sb_strips():
                g = abuf[slot, c, rows, :].astype(F32) * gbuf[slot, c, rows, :]
                gbuf[slot, c, rows, :] = g
                _sb_split_to(hl.at[c], rows, g)

        def stage_dz(c, slot, grun, diag):
            for r, rows in _sb_strips():
                z = zbuf[slot, c, rows, :]
                g = gbuf[slot, c, rows, :]
                cs = (delta[c] - grun)[r:r + SB_STRIP, :] - cbuf[c, rows, :]
                sig = 1.0 / (1.0 + jnp.exp(-z))
                dz = g - (g + cs) * sig
                if diag:
                    dz = jnp.where(_sb_diag_mask(r), dz, 0.0)
                dzbuf[slot, c, rows, :] = dz.astype(BF16)

        def chain(slot, nlrun, grun, diag):
            for c in range(SB_NCH):
                _sb_stage_soft(zbuf.at[slot, c], hl.at[c], diag)
                cbuf[c] = _nn(hl[c, :, 0:SB_BK], u_incl)
            nl_tot = []
            for c in range(SB_NCH):
                _sb_stage_weights(zbuf.at[slot, c], cbuf.at[c], abuf.at[slot, c], nlrun[c], diag)
                nl_tot.append(cbuf[c, :, 0:1])
                stage_g(c, slot)
                cbuf[c] = _nn(hl[c], u2)
            g_tot = []
            for c in range(SB_NCH):
                stage_dz(c, slot, grun[c], diag)
                g_tot.append(cbuf[c, :, 0:1])
            return (tuple(a + b for a, b in zip(nlrun, nl_tot)), tuple(a + b for a, b in zip(grun, g_tot)))

        def post(slot, j, dqs):
            off = pl.multiple_of(j * SB_BK, SB_BK)
            kbs = _sb_block(k_ref, j)
            dq_t = [_nn(dzbuf[slot, c], kbs[c // 2]) for c in range(SB_NCH)]
            dk_t = [_tn(dzbuf[slot, c], qf[c // 2]) for c in range(SB_NCH)]
            dv_t = [_tn(abuf[slot, c], dof[c // 2]) for c in range(SB_NCH)]
            for hp in range(nhp):
                cols = slice(128 * hp, 128 * (hp + 1))
                dk_acc[pl.ds(off, SB_BK), cols] += 0.125 * jnp.where(lane_k, dk_t[2 * hp], dk_t[2 * hp + 1])
                dv_acc[pl.ds(off, SB_BK), cols] += jnp.where(lane_k, dv_t[2 * hp], dv_t[2 * hp + 1])
            return tuple(dq + jnp.where(lane_a, dq_t[2 * hp], dq_t[2 * hp + 1]) for hp, dq in enumerate(dqs))

        def tile(j, dqs, nlrun, grun, diag):
            pre(0, j)
            nlrun, grun = chain(0, nlrun, grun, diag)
            return post(0, j, dqs), nlrun, grun

        zero = (jnp.zeros((SB_BQ, 1), F32),) * SB_NCH
        dqs, nlrun, grun = tile(i, (jnp.zeros((SB_BQ, 128), F32),) * nhp, zero, zero, True)

        def step(carry):
            j, _, dqs, nlrun, grun = carry
            dqs, nlrun, grun = tile(j, dqs, nlrun, grun, False)
            return j - 1, _sb_live(nlrun), dqs, nlrun, grun

        carry = lax.while_loop(lambda c: (c[0] >= 0) & c[1], step, (i - 1, _sb_live(nlrun), dqs, nlrun, grun))
        dq_ref[...] = (0.125 * jnp.concatenate(carry[2], axis=1)).astype(BF16)

        def out_copies(g):
            cols = pl.ds(pl.multiple_of(g * SB_LANES, SB_LANES), SB_LANES)
            return (pltpu.make_async_copy(dk16, dk_hbm.at[:, cols], sems.at[0]),
                    pltpu.make_async_copy(dv16, dv_hbm.at[:, cols], sems.at[1]))

        @pl.when((i == nq - 1) & (g_idx > 0))
        def _():
            for cp in out_copies(g_idx - 1):
                cp.wait()

        @pl.when(i == nq - 1)
        def _():
            def narrow(r, carry):
                rows = pl.ds(pl.multiple_of(r * SB_BK, SB_BK), SB_BK)
                dk16[rows, :] = dk_acc[rows, :].astype(BF16)
                dv16[rows, :] = dv_acc[rows, :].astype(BF16)
                return carry

            lax.fori_loop(0, s // SB_BK, narrow, 0)
            for cp in out_copies(g_idx):
                cp.start()

        @pl.when((i == nq - 1) & (g_idx == ng - 1))
        def _():
            for cp in out_copies(g_idx):
                cp.wait()

    qblk = pl.BlockSpec((SB_BQ, SB_LANES), lambda g, i: (i, g))
    hbm = pl.BlockSpec(memory_space=pl.ANY)
    return pl.pallas_call(
        body, name=name, grid=(ng, nq),
        in_specs=[qblk, pl.BlockSpec((s, SB_LANES), lambda g, i: (0, ng + g)),
                  pl.BlockSpec((s, SB_LANES), lambda g, i: (0, 2 * ng + g)), qblk, qblk, *rd.in_specs],
        out_specs=[qblk, hbm, hbm, *rd.out_specs],
        out_shape=[jax.ShapeDtypeStruct((s, D), BF16)] * 3 + rd.out_shape,
        scratch_shapes=[pltpu.VMEM((s, SB_LANES), F32), pltpu.VMEM((s, SB_LANES), F32),
                        pltpu.VMEM((s, SB_LANES), BF16), pltpu.VMEM((s, SB_LANES), BF16),
                        pltpu.SemaphoreType.DMA((2,)),
                        pltpu.VMEM((1, SB_NCH, SB_BQ, SB_BK), F32), pltpu.VMEM((1, SB_NCH, SB_BQ, SB_BK), F32),
                        pltpu.VMEM((SB_NCH, SB_BQ, 2 * SB_BK), BF16), pltpu.VMEM((SB_NCH, SB_BQ, SB_BK), F32),
                        pltpu.VMEM((1, SB_NCH, SB_BQ, SB_BK), BF16), pltpu.VMEM((1, SB_NCH, SB_BQ, SB_BK), BF16),
                        *rd.scratch],
        compiler_params=_cp("arbitrary", "arbitrary"),
    )(qkv, qkv, qkv, o, do, *rides)


def _ssd_core(z, xpre, dtr, state, dtb, alog, dsk, nw):
    L = SSD_CHUNK
    xa = _silu(xpre)
    pieces = _split(xa, 128)
    xs = jnp.concatenate(pieces[:8], axis=1)
    bm, cm = pieces[8:10], pieces[10:12]
    dt = _softplus(dtr + dtb)
    a = dt * (-jnp.exp(alog))
    tri = (_iota((L, L), 0) >= _iota((L, L), 1)).astype(F32)
    a_cs = _dot("nn", tri, a, 1, 3)
    xc = xs * dt
    tril = _iota((L, L), 0) >= _iota((L, L), 1)
    lane_a = _iota((L, 128), 1) < 64
    acs_p = _split(a_cs, 128)
    xc_p = _split(xc, 128)
    ys, new_states = [], []
    for g in range(2):
        cb = _dot("nt", cm[g], bm[g])
        for pp in range(4):
            pair = 4 * g + pp
            acs = acs_p[pair]
            acs_t = acs.T
            xcp = xc_p[pair]
            st = state[pair]
            heads = []
            for hh in range(2):
                col = _take_fn(1, 64 * hh)(acs)
                row = _take_fn(0, 64 * hh)(acs_t)
                seg = col - row
                lm = jnp.where(tril, jnp.exp(jnp.where(tril, seg, 0.0)), 0.0)
                heads.append(_dot("nn", cb * lm, xcp))
            ydiag = jnp.where(lane_a, heads[0], heads[1])
            last = _take_fn(0, L - 1)(acs)
            snew = _dot("tn", xcp * jnp.exp(last - acs), bm[g])
            new_states.append(st * jnp.exp(_take_fn(1, L - 1)(acs_t)) + snew)
            yoff = _dot("nt", cm[g], st) * jnp.exp(acs)
            ys.append(ydiag + yoff)
    y = jnp.concatenate(ys, axis=1) + xs * dsk
    yg = y * _silu(z)
    outs = []
    for v in _split(yg, 512):
        outs.append(v * lax.rsqrt(jnp.mean(v * v, axis=-1, keepdims=True) + EPS))
    return jnp.concatenate(outs, axis=1) * nw, tuple(new_states)


XBC = 1536


def _ssd_conv(ext_ref, cw, cb):
    acc = cb
    for k in range(4):
        acc = acc + cw[k:k + 1, :] * ext_ref[pl.ds(HALO - 3 + k, SSD_CHUNK), :]
    return acc


def _ssd_fwd(z, xbc, dtr, cw, cb, lanes, rides, *, name):
    s = z.shape[0]
    L = SSD_CHUNK
    nc = s // L
    rd = _Rides(rides, [True] * len(rides))

    def body(*refs):
        ins, (y_ref, st_ref), (state, ext), handles = rd.split(refs, 7, 2, 2)
        z_ref, x_ref, h_ref, dtr_ref, cw_ref, cb_ref, ln_ref = ins
        c = pl.program_id(0)
        rd.run(handles, c == 0, c == nc - 1)

        @pl.when(c == 0)
        def _():
            state[...] = jnp.zeros_like(state)

        ext[0:HALO, :] = jnp.where(c == 0, 0.0, h_ref[...])
        ext[HALO:, :] = x_ref[...]
        xpre = _ssd_conv(ext, cw_ref[...], cb_ref[...])
        st_ref[0] = state[...]
        st_in = tuple(state[p] for p in range(8))
        yn, st_out = _ssd_core(z_ref[...], xpre, dtr_ref[...], st_in,
                               ln_ref[0:1, :], ln_ref[1:2, :], ln_ref[2:3, :], ln_ref[3:4, :])
        y_ref[...] = yn.astype(BF16)
        for p in range(8):
            state[p] = st_out[p]

    return pl.pallas_call(
        body, name=name, grid=(nc,),
        in_specs=[pl.BlockSpec((L, D), lambda c: (c, 0)),
                  pl.BlockSpec((L, XBC), lambda c: (c, 0)),
                  pl.BlockSpec((HALO, XBC), lambda c: (jnp.maximum(c * (L // HALO) - 1, 0), 0)),
                  pl.BlockSpec((L, D), lambda c: (c, 0)),
                  pl.BlockSpec((4, XBC), lambda c: (0, 0)),
                  pl.BlockSpec((1, XBC), lambda c: (0, 0)),
                  pl.BlockSpec((8, D), lambda c: (0, 0)), *rd.in_specs],
        out_specs=[pl.BlockSpec((L, D), lambda c: (c, 0)),
                   pl.BlockSpec((1, 8, 128, 128), lambda c: (c, 0, 0, 0)), *rd.out_specs],
        out_shape=[jax.ShapeDtypeStruct((s, D), BF16), jax.ShapeDtypeStruct((nc, 8, 128, 128), F32),
                   *rd.out_shape],
        scratch_shapes=[pltpu.VMEM((8, 128, 128), F32), pltpu.VMEM((L + HALO, XBC), F32), *rd.scratch],
        compiler_params=_cp("arbitrary"),
    )(z, xbc, xbc, dtr, cw, cb, lanes, *rides)


def _ssd_bwd(z, xbc, dtr, states, dymix, cw, cb, lanes, *, name):
    s = z.shape[0]
    L = SSD_CHUNK
    nc = s // L

    def body(z_ref, x_ref, h_ref, dtr_ref, st_ref, dy_ref, cw_ref, cb_ref, ln_ref,
             dz_ref, dx_ref, ddt_ref, dln_ref, dcv_ref, dstate, ext, dext):
        i = pl.program_id(0)
        c = nc - 1 - i

        @pl.when(i == 0)
        def _():
            dstate[...] = jnp.zeros_like(dstate)
            dext[...] = jnp.zeros_like(dext)
            dln_ref[...] = jnp.zeros_like(dln_ref)
            dcv_ref[...] = jnp.zeros_like(dcv_ref)

        ext[0:HALO, :] = jnp.where(c == 0, 0.0, h_ref[...])
        ext[HALO:, :] = x_ref[...]
        cwv = cw_ref[...]
        xpre = _ssd_conv(ext, cwv, cb_ref[...])
        st_in = tuple(st_ref[0, p] for p in range(8))
        _, vjp = jax.vjp(_ssd_core, z_ref[...], xpre, dtr_ref[...], st_in,
                         ln_ref[0:1, :], ln_ref[1:2, :], ln_ref[2:3, :], ln_ref[3:4, :])
        dz, dxpre, ddtr, dst, d0, d1, d2, d3 = vjp((dy_ref[...], tuple(dstate[p] for p in range(8))))
        for p in range(8):
            dstate[p] = dst[p]
        dz_ref[...] = dz.astype(BF16)
        ddt_ref[...] = ddtr.astype(BF16)
        dln_ref[0:4, :] += jnp.concatenate([d0, d1, d2, d3], axis=0)
        dext[0:L, :] = dxpre
        xcur = x_ref[...]
        dx = jnp.zeros((L, XBC), F32)
        rows = []
        for k in range(4):
            shifted = dext[pl.ds(3 - k, L), :]
            dx = dx + cwv[k:k + 1, :] * shifted
            rows.append(jnp.sum(shifted * xcur, axis=0, keepdims=True))
        rows.append(jnp.sum(dxpre, axis=0, keepdims=True))
        dx_ref[...] = dx.astype(BF16)
        dcv_ref[0:5, :] += jnp.concatenate(rows, axis=0)
        dext[L:L + HALO, :] = dxpre[0:HALO, :]

    rev = lambda i: (nc - 1 - i, 0)
    return pl.pallas_call(
        body, name=name, grid=(nc,),
        in_specs=[pl.BlockSpec((L, D), rev),
                  pl.BlockSpec((L, XBC), rev),
                  pl.BlockSpec((HALO, XBC), lambda i: (jnp.maximum((nc - 1 - i) * (L // HALO) - 1, 0), 0)),
                  pl.BlockSpec((L, D), rev),
                  pl.BlockSpec((1, 8, 128, 128), lambda i: (nc - 1 - i, 0, 0, 0)),
                  pl.BlockSpec((L, D), rev),
                  pl.BlockSpec((4, XBC), lambda i: (0, 0)),
                  pl.BlockSpec((1, XBC), lambda i: (0, 0)),
                  pl.BlockSpec((8, D), lambda i: (0, 0))],
        out_specs=[pl.BlockSpec((L, D), rev), pl.BlockSpec((L, XBC), rev), pl.BlockSpec((L, D), rev),
                   pl.BlockSpec((8, D), lambda i: (0, 0)), pl.BlockSpec((8, XBC), lambda i: (0, 0))],
        out_shape=[jax.ShapeDtypeStruct((s, D), BF16), jax.ShapeDtypeStruct((s, XBC), BF16),
                   jax.ShapeDtypeStruct((s, D), BF16), jax.ShapeDtypeStruct((8, D), F32),
                   jax.ShapeDtypeStruct((8, XBC), F32)],
        scratch_shapes=[pltpu.VMEM((8, 128, 128), F32), pltpu.VMEM((L + HALO, XBC), F32),
                        pltpu.VMEM((L + HALO, XBC), F32)],
        compiler_params=_cp("arbitrary"),
    )(z, xbc, xbc, dtr, states, dymix, cw, cb, lanes)


def _mem_attn_math(q, k, v):
    outs = []
    for qh, kh, vh in zip(_split(q, 256), _split(k, 256), _split(v, 256)):
        sc = _dot("nt", qh, kh) * (1.0 / 16.0)
        e = jnp.exp(sc - lax.stop_gradient(jnp.max(sc, axis=-1, keepdims=True)))
        p = e / jnp.sum(e, axis=-1, keepdims=True)
        outs.append(_dot("nn", p, vh))
    return jnp.concatenate(outs, axis=1)


def _mem_attn_fwd(q, k, v, *, name):
    s, m = q.shape[0], k.shape[0]
    tm = _tile(s, 512, 8)

    def body(q_ref, k_ref, v_ref, o_ref):
        o_ref[...] = _mem_attn_math(q_ref[...].astype(F32), k_ref[...].astype(F32),
                                    v_ref[...].astype(F32)).astype(BF16)

    row = pl.BlockSpec((tm, D), lambda i: (i, 0))
    kv = pl.BlockSpec((m, D), lambda i: (0, 0))
    return pl.pallas_call(
        body, name=name, grid=(s // tm,), in_specs=[row, kv, kv], out_specs=row,
        out_shape=jax.ShapeDtypeStruct((s, D), BF16), compiler_params=_cp("parallel"),
    )(q, k, v)


def _mem_attn_bwd(q, k, v, do, *, name):
    s, m = q.shape[0], k.shape[0]
    tm = _tile(s, 512, 8)

    def body(q_ref, k_ref, v_ref, do_ref, dq_ref, dk_ref, dv_ref):
        @pl.when(pl.program_id(0) == 0)
        def _():
            dk_ref[...] = jnp.zeros_like(dk_ref)
            dv_ref[...] = jnp.zeros_like(dv_ref)

        _, vjp = jax.vjp(_mem_attn_math, q_ref[...].astype(F32), k_ref[...].astype(F32),
                         v_ref[...].astype(F32))
        dq, dk, dv = vjp(do_ref[...])
        dq_ref[...] = dq.astype(BF16)
        dk_ref[...] += dk
        dv_ref[...] += dv

    row = pl.BlockSpec((tm, D), lambda i: (i, 0))
    kv = pl.BlockSpec((m, D), lambda i: (0, 0))
    return pl.pallas_call(
        body, name=name, grid=(s // tm,), in_specs=[row, kv, kv, row], out_specs=[row, kv, kv],
        out_shape=[jax.ShapeDtypeStruct((s, D), BF16), jax.ShapeDtypeStruct((m, D), F32),
                   jax.ShapeDtypeStruct((m, D), F32)],
        compiler_params=_cp("arbitrary"),
    )(q, k, v, do)


DFF = 2816
FFN_TC = 1408
FFN_TM = 512


FFN_CHUNKS = tuple((c, min(512, FFN_TC - c)) for c in range(0, FFN_TC, 512))


def _rows8(ref, r, cols):
    return ref[pl.ds(pl.multiple_of(r, HALO), HALO), cols]


def _shift_down(prev, cur, s):
    return jnp.where(_iota(cur.shape, 0) < s, pltpu.roll(prev, s, 0), pltpu.roll(cur, s, 0))


def _shift_up(cur, nxt, s):
    return jnp.where(_iota(cur.shape, 0) >= HALO - s, pltpu.roll(nxt, HALO - s, 0), pltpu.roll(cur, HALO - s, 0))


def _ffn_conv_strip(ext_ref, r, cols, cw, cb):
    prev, cur = _rows8(ext_ref, r, cols), _rows8(ext_ref, r + HALO, cols)
    return cb + cw[0:1, :] * _shift_down(prev, cur, 2) + cw[1:2, :] * _shift_down(prev, cur, 1) + cw[2:3, :] * cur


def _ffn_specs(s):
    tm, tc = FFN_TM, FFN_TC
    blk = pl.BlockSpec((tm, tc), lambda i, j: (i, j))
    halo = pl.BlockSpec((HALO, tc), lambda i, j: (jnp.maximum(i * (tm // HALO) - 1, 0), j))
    cw = pl.BlockSpec((3, tc), lambda i, j: (0, j))
    cb = pl.BlockSpec((1, tc), lambda i, j: (0, j))
    return tm, tc, blk, halo, cw, cb


def _glu_fwd(ug, uv, cwg, cwv, cbg, cbv, *, name):
    s = ug.shape[0]
    tm, tc, blk, halo, cw, cb = _ffn_specs(s)

    def body(g_ref, gh_ref, v_ref, vh_ref, cwg_ref, cwv_ref, cbg_ref, cbv_ref, f_ref, eg, ev):
        first = pl.program_id(0) == 0
        eg[0:HALO, :] = jnp.where(first, 0.0, gh_ref[...])
        eg[HALO:, :] = g_ref[...]
        ev[0:HALO, :] = jnp.where(first, 0.0, vh_ref[...])
        ev[HALO:, :] = v_ref[...]
        cwgv, cwvv, cbgv, cbvv = cwg_ref[...], cwv_ref[...], cbg_ref[...], cbv_ref[...]

        def step(t, carry):
            for c0, w in FFN_CHUNKS:
                cols = slice(c0, c0 + w)
                outs = []
                for h in range(2):
                    r = t * 16 + HALO * h
                    g = _ffn_conv_strip(eg, r, cols, cwgv[:, cols], cbgv[:, cols])
                    v = _ffn_conv_strip(ev, r, cols, cwvv[:, cols], cbvv[:, cols])
                    outs.append(_silu(g) * v)
                f_ref[pl.ds(pl.multiple_of(t * 16, 16), 16), cols] = jnp.concatenate(outs, axis=0).astype(BF16)
            return carry

        lax.fori_loop(0, tm // 16, step, 0)

    return pl.pallas_call(
        body, name=name, grid=(s // tm, DFF // tc),
        in_specs=[blk, halo, blk, halo, cw, cw, cb, cb], out_specs=blk,
        out_shape=jax.ShapeDtypeStruct((s, DFF), BF16),
        scratch_shapes=[pltpu.VMEM((tm + HALO, tc), F32)] * 2,
        compiler_params=_cp("parallel", "parallel"),
    )(ug, ug, uv, uv, cwg, cwv, cbg, cbv)


def _ffn_bwd(ug, uv, df, cwg, cwv, cbg, cbv, *, name):
    s = ug.shape[0]
    tm, tc = FFN_TM, FFN_TC
    nb = s // tm
    rows_ext = tm + HALO

    def body(g_ref, gp_ref, gn_ref, v_ref, vp_ref, vn_ref, df_ref, dfn_ref, cwg_ref, cwv_ref, cbg_ref, cbv_ref,
             dxg_ref, dxv_ref, dcg_ref, dcv_ref, eg, ev, edf, edg, edv, accg, accv):
        i = pl.program_id(1)
        first, last = i == 0, i == nb - 1

        @pl.when(first)
        def _():
            dcg_ref[...] = jnp.zeros_like(dcg_ref)
            dcv_ref[...] = jnp.zeros_like(dcv_ref)

        for e, prev, main, nxt in ((eg, gp_ref, g_ref, gn_ref), (ev, vp_ref, v_ref, vn_ref)):
            e[0:HALO, :] = jnp.where(first, 0.0, prev[...])
            e[HALO:HALO + tm, :] = main[...]
            e[HALO + tm:, :] = jnp.where(last, 0.0, nxt[...])
        edf[0:tm, :] = df_ref[...]
        edf[tm:, :] = jnp.where(last, 0.0, dfn_ref[...])
        accg[...] = jnp.zeros_like(accg)
        accv[...] = jnp.zeros_like(accv)
        cwgv, cwvv, cbgv, cbvv = cwg_ref[...], cwv_ref[...], cbg_ref[...], cbv_ref[...]

        def cotangents(t, carry):
            r = t * HALO
            for c0, w in FFN_CHUNKS:
                cols = slice(c0, c0 + w)
                g = _ffn_conv_strip(eg, r, cols, cwgv[:, cols], cbgv[:, cols])
                v = _ffn_conv_strip(ev, r, cols, cwvv[:, cols], cbvv[:, cols])
                dfs = _rows8(edf, r, cols)
                sg = _sigmoid(g)
                edv[pl.ds(pl.multiple_of(r, HALO), HALO), cols] = dfs * g * sg
                edg[pl.ds(pl.multiple_of(r, HALO), HALO), cols] = dfs * v * sg * (1.0 + g * (1.0 - sg))
            return carry

        lax.fori_loop(0, rows_ext // HALO, cotangents, 0)

        def conv_backward(t, carry):
            for c0, w in FFN_CHUNKS:
                cols = slice(c0, c0 + w)
                for edu, e, cw, dx_ref, acc in ((edg, eg, cwgv[:, cols], dxg_ref, accg),
                                                (edv, ev, cwvv[:, cols], dxv_ref, accv)):
                    dxs = []
                    for h in range(2):
                        r = t * 16 + HALO * h
                        cur, nxt = _rows8(edu, r, cols), _rows8(edu, r + HALO, cols)
                        up1, up2 = _shift_up(cur, nxt, 1), _shift_up(cur, nxt, 2)
                        x = _rows8(e, r + HALO, cols)
                        dxs.append(cw[2:3, :] * cur + cw[1:2, :] * up1 + cw[0:1, :] * up2)
                        acc[0, :, cols] += up2 * x
                        acc[1, :, cols] += up1 * x
                        acc[2, :, cols] += cur * x
                        acc[3, :, cols] += cur
                    dx_ref[pl.ds(pl.multiple_of(t * 16, 16), 16), cols] = jnp.concatenate(dxs, axis=0).astype(BF16)
            return carry

        lax.fori_loop(0, tm // 16, conv_backward, 0)
        for acc, dc_ref in ((accg, dcg_ref), (accv, dcv_ref)):
            dc_ref[0:4, :] += jnp.concatenate([jnp.sum(acc[k], axis=0, keepdims=True) for k in range(4)], axis=0)

    blk = pl.BlockSpec((tm, tc), lambda j, i: (i, j))
    nxt = pl.BlockSpec((HALO, tc), lambda j, i: (jnp.minimum((i + 1) * (tm // HALO), s // HALO - 1), j))
    prv = pl.BlockSpec((HALO, tc), lambda j, i: (jnp.maximum(i * (tm // HALO) - 1, 0), j))
    cw = pl.BlockSpec((3, tc), lambda j, i: (0, j))
    cb = pl.BlockSpec((1, tc), lambda j, i: (0, j))
    acc = pl.BlockSpec((8, tc), lambda j, i: (0, j))
    return pl.pallas_call(
        body, name=name, grid=(DFF // tc, nb),
        in_specs=[blk, prv, nxt, blk, prv, nxt, blk, nxt, cw, cw, cb, cb],
        out_specs=[blk, blk, acc, acc],
        out_shape=[jax.ShapeDtypeStruct((s, DFF), BF16)] * 2 + [jax.ShapeDtypeStruct((8, DFF), F32)] * 2,
        scratch_shapes=[pltpu.VMEM((tm + 2 * HALO, tc), F32)] * 2 + [pltpu.VMEM((rows_ext, tc), F32)] * 3
                       + [pltpu.VMEM((4, HALO, tc), F32)] * 2,
        compiler_params=_cp("parallel", "arbitrary"),
    )(ug, ug, ug, uv, uv, uv, df, df, cwg, cwv, cbg, cbv)


MESH = pl.DeviceIdType.MESH


def _all_gather(arrs, *, name):
    n = len(arrs)

    def body(*refs):
        x_refs, out_refs = refs[:n], refs[n:2 * n]
        send_sems, recv_sems, local_sems = refs[2 * n:]
        x, y, c = lax.axis_index("x"), lax.axis_index("y"), lax.axis_index("c")
        me, sibling = (x, y, c), (x, y, 1 - c)
        chips = [(1 - x, y), (x, 1 - y), (1 - x, 1 - y)]

        def blk(a, dev):
            return out_refs[a].at[4 * dev[0] + 2 * dev[1] + dev[2]]

        def copy(a, k, block, to, src=None):
            return pltpu.make_async_remote_copy(
                src_ref=blk(a, block) if src is None else src, dst_ref=blk(a, block),
                send_sem=send_sems.at[7 * a + k], recv_sem=recv_sems.at[7 * a + k],
                device_id=to, device_id_type=MESH)

        started = []
        mine = []
        for a in range(n):
            cp = pltpu.make_async_copy(x_refs[a], blk(a, me), local_sems.at[a])
            cp.start()
            mine.append(cp)
            first = [copy(a, 0, me, sibling, src=x_refs[a])]
            first += [copy(a, 1 + j, me, (*chip, c), src=x_refs[a]) for j, chip in enumerate(chips)]
            for cp in first:
                cp.start()
            started += first
        for a in range(n):
            for j, chip in enumerate(chips):
                copy(a, 1 + j, (*chip, c), me).wait_recv()
                fwd = copy(a, 4 + j, (*chip, c), sibling)
                fwd.start()
                started.append(fwd)
        for a in range(n):
            copy(a, 0, sibling, me).wait_recv()
            for j, chip in enumerate(chips):
                copy(a, 4 + j, (*chip, 1 - c), me).wait_recv()
        for cp in started:
            cp.wait_send()
        for cp in mine:
            cp.wait()

    any_spec = pl.BlockSpec(memory_space=pl.ANY)
    return pl.pallas_call(
        body, name=name,
        in_specs=[any_spec] * n, out_specs=[any_spec] * n,
        out_shape=[jax.ShapeDtypeStruct((NDEV,) + a.shape, a.dtype) for a in arrs],
        scratch_shapes=[pltpu.SemaphoreType.DMA((7 * n,)), pltpu.SemaphoreType.DMA((7 * n,)),
                        pltpu.SemaphoreType.DMA((n,))],
    )(*arrs)


class _Direct:
    SEMS = (pltpu.SemaphoreType.DMA((7,)), pltpu.SemaphoreType.DMA((7,)), pltpu.SemaphoreType.DMA((1,)))

    def __init__(self, src_ref, recv_ref, sems, gather):
        x, y, c = lax.axis_index("x"), lax.axis_index("y"), lax.axis_index("c")
        me = 4 * x + 2 * y + c
        send_sems, recv_sems, local_sem = sems
        src = (lambda pid: src_ref) if gather else (lambda pid: src_ref.at[pid])
        self.mine = pltpu.make_async_copy(src(me), recv_ref.at[me], local_sem.at[0])
        self.copies = []
        for k in range(1, NDEV):
            px = 1 - x if k & 4 else x
            py = 1 - y if k & 2 else y
            pc = 1 - c if k & 1 else c
            self.copies.append(pltpu.make_async_remote_copy(
                src_ref=src(4 * px + 2 * py + pc), dst_ref=recv_ref.at[me],
                send_sem=send_sems.at[k - 1], recv_sem=recv_sems.at[k - 1],
                device_id=(px, py, pc), device_id_type=MESH))

    def start(self):
        self.mine.start()
        for cp in self.copies:
            cp.start()

    def wait(self):
        for cp in self.copies:
            cp.wait_recv()
        for cp in self.copies:
            cp.wait_send()
        self.mine.wait()


def _recv_shape(src, gather):
    return jax.ShapeDtypeStruct(((NDEV,) + src.shape) if gather else src.shape, src.dtype)


class _Rides:
    def __init__(self, rides, gathers):
        self.n = len(rides)
        self.gathers = list(gathers)
        any_spec = pl.BlockSpec(memory_space=pl.ANY)
        self.in_specs = [any_spec] * self.n
        self.out_specs = [any_spec] * self.n
        self.out_shape = [_recv_shape(a, g) for a, g in zip(rides, gathers)]
        self.scratch = list(_Direct.SEMS) * self.n

    def split(self, refs, n_in, n_out, n_scratch):
        n = self.n
        ins, refs = refs[:n_in], refs[n_in:]
        rides, refs = refs[:n], refs[n:]
        outs, refs = refs[:n_out], refs[n_out:]
        gots, refs = refs[:n], refs[n:]
        scratch, sems = refs[:n_scratch], refs[n_scratch:]
        return ins, outs, scratch, (rides, gots, sems)

    def run(self, handles, first, last):
        rides, gots, sems = handles

        def all_of():
            return [_Direct(rides[a], gots[a], sems[3 * a:3 * a + 3], self.gathers[a]) for a in range(self.n)]

        @pl.when(first)
        def _():
            for e in all_of():
                e.start()

        @pl.when(last)
        def _():
            for e in all_of():
                e.wait()


def _exchange(arrs, gathers, *, name):
    rd = _Rides(arrs, gathers)

    def body(*refs):
        _, _, _, handles = rd.split(refs, 0, 0, 0)
        rd.run(handles, True, True)

    return pl.pallas_call(
        body, name=name, in_specs=rd.in_specs, out_specs=rd.out_specs, out_shape=rd.out_shape,
        scratch_shapes=rd.scratch,
    )(*arrs)


def _adamw(parts, w, m, v, *, name):
    r, cols = w.shape
    tm = _tile(r, 256, PACK_ALIGN)
    c1 = 1.0 - ADAM_B1 ** ADAM_STEP
    c2 = 1.0 - ADAM_B2 ** ADAM_STEP

    def body(p_ref, w_ref, m_ref, v_ref, g_ref, d_ref, nm_ref, nv_ref):
        g = p_ref[0].astype(F32)
        for i in range(1, NDEV):
            g = g + p_ref[i].astype(F32)
        nm = ADAM_B1 * m_ref[...] + (1.0 - ADAM_B1) * g
        nv = ADAM_B2 * v_ref[...] + (1.0 - ADAM_B2) * (g * g)
        d_ref[...] = -ADAM_LR * ((nm / c1) / (jnp.sqrt(nv / c2) + ADAM_EPS) + ADAM_WD * w_ref[...])
        g_ref[...] = g
        nm_ref[...] = nm
        nv_ref[...] = nv

    row = pl.BlockSpec((tm, cols), lambda i: (i, 0))
    return pl.pallas_call(
        body, name=name, grid=(r // tm,),
        in_specs=[pl.BlockSpec((NDEV, tm, cols), lambda i: (0, i, 0)), row, row, row],
        out_specs=[row] * 4, out_shape=[jax.ShapeDtypeStruct((r, cols), F32)] * 4,
        compiler_params=_cp("parallel"),
    )(parts, w, m, v)


PACK_ALIGN = 16


def _part_rows(shape):
    n = -(-math.prod(shape) // D)
    return n + (-n) % PACK_ALIGN


def _rows(a):
    flat = a.reshape(-1)
    pad = _part_rows(a.shape) * D - flat.shape[0]
    if pad:
        flat = jnp.concatenate([flat, jnp.zeros((pad,), flat.dtype)])
    return flat.reshape(-1, D)


def _pack(parts, total_rows):
    if all(math.prod(p.shape) % (PACK_ALIGN * D) for p in parts):
        return _pack_small(parts, total_rows)
    rows = [_rows(p) for p in parts]
    used = sum(r.shape[0] for r in rows)
    if total_rows > used:
        rows.append(jnp.zeros((total_rows - used, D), rows[0].dtype))
    return jnp.concatenate(rows, axis=0)


def _pack_small(parts, total_rows):
    flat, used = [], 0
    for p in parts:
        n, nr = math.prod(p.shape), _part_rows(p.shape)
        flat += [p.reshape(-1), jnp.zeros((nr * D - n,), p.dtype)]
        used += nr
    flat.append(jnp.zeros(((total_rows - used) * D,), parts[0].dtype))
    return jnp.concatenate(flat).reshape(total_rows, D)


def _unpack(buf, shapes, part_rows=_part_rows):
    out, r0 = [], 0
    for shp in shapes:
        n = math.prod(shp)
        out.append(buf[r0:r0 + part_rows(shp)].reshape(-1)[:n].reshape(shp))
        r0 += part_rows(shp)
    return out


def _tight_rows(shape):
    return -(-math.prod(shape) // D)


def _pack_tight(parts, total_rows):
    flat, used = [], 0
    for p in parts:
        n, nr = math.prod(p.shape), _tight_rows(p.shape)
        flat += [p.reshape(-1), jnp.zeros((nr * D - n,), p.dtype)]
        used += nr
    flat.append(jnp.zeros(((total_rows - used) * D,), parts[0].dtype))
    return jnp.concatenate(flat).reshape(total_rows, D)


SHARD = {"w_in": (D, 706), "w_out": (256, D), "w_mq": (128, D), "w_mk": (128, D), "w_mv": (128, D),
         "w_mo": (128, D), "w_up": (D, 704), "w_down": (352, D), "conv_ssd_w": (4, 192), "conv_ffn_w": (3, 704)}
GATHER_MID = ["w_out", "w_mq", "w_mk", "w_mv", "w_mo"]
GATHER_FFN = ["w_down"]
CONV_TAPS = ["conv_ssd_w", "conv_ffn_w"]
GRADS_PACKED = ["w_out", "w_mq", "w_mk", "w_mv", "w_mo", "w_down", "conv_ffn_w", "conv_ssd_w"]


def _layout(names):
    row0, r = {}, 0
    for n in names:
        row0[n] = r
        r += _part_rows(SHARD[n])
    return row0, r + (-r) % 128


SMALL = [("norm_mix_w", (1, D)), ("conv_ssd_b", (1, 1536)), ("dt_bias", (1, 16)), ("a_log", (1, 16)),
         ("d_skip", (1, 16)), ("ssd_norm_w", (1, D)), ("sb_norm_w", (1, D)), ("norm_mem_w", (1, D)),
         ("norm_memkv_w", (1, D)), ("norm_ffn_w", (1, D)), ("conv_ffn_b", (1, 5632)), ("norm_final_w", (D,))]
LOSS_ROW = sum(_tight_rows(_shp) for _, _shp in SMALL)
SMALL_ROWS = LOSS_ROW + 1 + (-(LOSS_ROW + 1)) % 8
ORDER = ["norm_mix_w", "w_in", "conv_ssd_w", "conv_ssd_b", "dt_bias", "a_log", "d_skip", "ssd_norm_w",
         "sb_norm_w", "w_out", "norm_mem_w", "norm_memkv_w", "w_mq", "w_mk", "w_mv", "w_mo", "norm_ffn_w",
         "w_up", "conv_ffn_w", "conv_ffn_b", "w_down", "norm_final_w"]


def _pad_rows(a, nr):
    n = a.shape[1]
    return jnp.concatenate([a, jnp.zeros((NDEV, nr * D - n), a.dtype)], axis=1).reshape(NDEV, nr, D)


def _group_sum(lanes):
    return lanes.reshape(16, 64).sum(axis=1).reshape(1, 16)


def kernel(x, mem, norm_mix_w, w_in, conv_ssd_w, conv_ssd_b, dt_bias, a_log, d_skip, ssd_norm_w, sb_norm_w, w_out, norm_mem_w, norm_memkv_w, w_mq, w_mk, w_mv, w_mo, norm_ffn_w, w_up, conv_ffn_w, conv_ffn_b, w_down, norm_final_w, loss_target, m_norm_mix_w, m_w_in, m_conv_ssd_w, m_conv_ssd_b, m_dt_bias, m_a_log, m_d_skip, m_ssd_norm_w, m_sb_norm_w, m_w_out, m_norm_mem_w, m_norm_memkv_w, m_w_mq, m_w_mk, m_w_mv, m_w_mo, m_norm_ffn_w, m_w_up, m_conv_ffn_w, m_conv_ffn_b, m_w_down, m_norm_final_w, v_norm_mix_w, v_w_in, v_conv_ssd_w, v_conv_ssd_b, v_dt_bias, v_a_log, v_d_skip, v_ssd_norm_w, v_sb_norm_w, v_w_out, v_norm_mem_w, v_norm_memkv_w, v_w_mq, v_w_mk, v_w_mv, v_w_mo, v_norm_ffn_w, v_w_up, v_conv_ffn_w, v_conv_ffn_b, v_w_down, v_norm_final_w):
    P = dict(norm_mix_w=norm_mix_w, w_in=w_in, conv_ssd_w=conv_ssd_w, conv_ssd_b=conv_ssd_b, dt_bias=dt_bias, a_log=a_log, d_skip=d_skip, ssd_norm_w=ssd_norm_w, sb_norm_w=sb_norm_w, w_out=w_out, norm_mem_w=norm_mem_w, norm_memkv_w=norm_memkv_w, w_mq=w_mq, w_mk=w_mk, w_mv=w_mv, w_mo=w_mo, norm_ffn_w=norm_ffn_w, w_up=w_up, conv_ffn_w=conv_ffn_w, conv_ffn_b=conv_ffn_b, w_down=w_down, norm_final_w=norm_final_w)
    M = dict(norm_mix_w=m_norm_mix_w, w_in=m_w_in, conv_ssd_w=m_conv_ssd_w, conv_ssd_b=m_conv_ssd_b, dt_bias=m_dt_bias, a_log=m_a_log, d_skip=m_d_skip, ssd_norm_w=m_ssd_norm_w, sb_norm_w=m_sb_norm_w, w_out=m_w_out, norm_mem_w=m_norm_mem_w, norm_memkv_w=m_norm_memkv_w, w_mq=m_w_mq, w_mk=m_w_mk, w_mv=m_w_mv, w_mo=m_w_mo, norm_ffn_w=m_norm_ffn_w, w_up=m_w_up, conv_ffn_w=m_conv_ffn_w, conv_ffn_b=m_conv_ffn_b, w_down=m_w_down, norm_final_w=m_norm_final_w)
    V = dict(norm_mix_w=v_norm_mix_w, w_in=v_w_in, conv_ssd_w=v_conv_ssd_w, conv_ssd_b=v_conv_ssd_b, dt_bias=v_dt_bias, a_log=v_a_log, d_skip=v_d_skip, ssd_norm_w=v_ssd_norm_w, sb_norm_w=v_sb_norm_w, w_out=v_w_out, norm_mem_w=v_norm_mem_w, norm_memkv_w=v_norm_memkv_w, w_mq=v_w_mq, w_mk=v_w_mk, w_mv=v_w_mv, w_mo=v_w_mo, norm_ffn_w=v_norm_ffn_w, w_up=v_w_up, conv_ffn_w=v_conv_ffn_w, conv_ffn_b=v_conv_ffn_b, w_down=v_w_down, norm_final_w=v_norm_final_w)
    small_shapes = [shp for _, shp in SMALL]

    def packed(src, names, dtype=F32):
        return _pack([src[n][0] for n in names], _layout(names)[1]).astype(dtype)

    def columns(g):
        return g.transpose(1, 0, 2).reshape(g.shape[1], NDEV * g.shape[2])

    g_in, g_taps = _all_gather([w_in[0].astype(BF16), packed(P, CONV_TAPS)], name="gather_w_in")
    W_in = columns(g_in)
    cw_ssd = g_taps[:, 0].reshape(NDEV, -1)[:, :768].reshape(NDEV, 4, 192).transpose(1, 0, 2).reshape(4, XBC)
    cw_ffn = (g_taps[:, PACK_ALIGN:PACK_ALIGN + 3].reshape(NDEV, -1)[:, :2112].reshape(NDEV, 3, 704)
              .transpose(1, 0, 2).reshape(3, 2 * DFF))
    W_z, W_xbc, W_dt, W_qkv = W_in[:, :D], W_in[:, D:D + XBC], W_in[:, D + XBC:D + XBC + 16], W_in[:, D + XBC + 16:]
    W_dtr = jnp.repeat(W_dt, 64, axis=1)
    cwg, cwv = cw_ffn[:, :DFF], cw_ffn[:, DFF:]
    cbg, cbv = conv_ffn_b[:, :DFF], conv_ffn_b[:, DFF:]
    rep = lambda p: jnp.repeat(p, 64, axis=1)
    lanes = jnp.concatenate([rep(dt_bias), rep(a_log), rep(d_skip), ssd_norm_w, jnp.zeros((4, D), F32)], axis=0)

    xs, tgt, mm = x[0], loss_target[0], mem[0]

    h1 = _norm_fwd(xs, norm_mix_w, name="norm_mix")
    z = _mm(h1, W_z, name="proj_z")
    xbc = _mm(h1, W_xbc, name="proj_xbc")
    dtr = _mm(h1, W_dtr, name="proj_dt")
    qkv = _mm(h1, W_qkv, name="proj_qkv", out_dtype=BF16)
    w_up16 = w_up[0].astype(BF16)
    y_ssd, states, g_ffn, g_up0 = _ssd_fwd(z, xbc, dtr, cw_ssd, conv_ssd_b, lanes,
                                           [packed(P, GATHER_FFN, BF16), w_up16[:D // 2]], name="ssd_fwd")
    o_sb, g_mid, g_up1 = _sb_fwd(qkv, [packed(P, GATHER_MID, BF16), w_up16[D // 2:]], name="sb_fwd")
    g_up = jnp.concatenate([g_up0, g_up1], axis=1)
    r_mid = _layout(GATHER_MID)[0]
    W_out = g_mid[:, r_mid["w_out"]:r_mid["w_out"] + 256].reshape(2 * D, D)
    W_mq, W_mk, W_mv, W_mo = [g_mid[:, r_mid[n]:r_mid[n] + 128].reshape(D, D)
                              for n in ("w_mq", "w_mk", "w_mv", "w_mo")]
    W_up = columns(g_up)
    W_down = g_ffn[:, 0:352].reshape(DFF, D)
    W_upg, W_upv = W_up[:, :DFF], W_up[:, DFF:]
    y_sb = _head_norm_fwd(o_sb, sb_norm_w, name="sb_norm")
    ymix = jnp.concatenate([y_ssd, y_sb], axis=1)
    x1 = _mm(ymix, W_out, add=xs, name="proj_out")
    h2 = _norm_fwd(x1, norm_mem_w, name="norm_mem")
    mn = _norm_fwd(mm, norm_memkv_w, name="norm_memkv")
    qm = _mm(h2, W_mq, name="mem_q", out_dtype=BF16)
    km = _mm(mn, W_mk, name="mem_k", out_dtype=BF16)
    vm = _mm(mn, W_mv, name="mem_v", out_dtype=BF16)
    om = _mem_attn_fwd(qm, km, vm, name="mem_attn")
    x2 = _mm(om, W_mo, add=x1, name="mem_o")
    h3 = _norm_fwd(x2, norm_ffn_w, name="norm_ffn")
    ug = _mm(h3, W_upg, name="ffn_up_g")
    uv = _mm(h3, W_upv, name="ffn_up_v")
    f = _glu_fwd(ug, uv, cwg, cwv, cbg, cbv, name="ffn_glu")
    x3 = _mm(f, W_down, add=x2, name="ffn_down")
    dx3, g_nfinal, loss_part = _final(x3, norm_final_w.reshape(1, D), tgt, name="final_loss")

    G = {}
    G["w_down"] = _mm(f, dx3, trans_a=True, name="g_w_down")
    df = _mm(dx3, W_down, trans_b=True, name="d_f")
    dupg, dupv, dcg, dcv = _ffn_bwd(ug, uv, df, cwg, cwv, cbg, cbv, name="ffn_glu_bwd")
    G["w_up"] = jnp.concatenate([_mm(h3, dupg, trans_a=True, name="g_w_up_g"),
                                 _mm(h3, dupv, trans_a=True, name="g_w_up_v")], axis=1)
    G["conv_ffn_w"] = jnp.concatenate([dcg[0:3], dcv[0:3]], axis=1)
    G["conv_ffn_b"] = jnp.concatenate([dcg[3:4], dcv[3:4]], axis=1)
    dh3 = _mm(dupg, W_upg, trans_b=True, name="d_h3_g")
    dh3 = _mm(dupv, W_upv, trans_b=True, add=dh3, name="d_h3_v")
    dx2, G["norm_ffn_w"] = _norm_bwd(x2, norm_ffn_w, dh3, dx3, name="norm_ffn_bwd")
    G["w_mo"] = _mm(om, dx2, trans_a=True, name="g_w_mo")
    dom = _mm(dx2, W_mo, trans_b=True, name="d_om")
    dqm, dkm, dvm = _mem_attn_bwd(qm, km, vm, dom, name="mem_attn_bwd")
    G["w_mq"] = _mm(h2, dqm, trans_a=True, name="g_w_mq")
    G["w_mk"] = _mm(mn, dkm, trans_a=True, name="g_w_mk")
    G["w_mv"] = _mm(mn, dvm, trans_a=True, name="g_w_mv")
    dh2 = _mm(dqm, W_mq, trans_b=True, name="d_h2")
    dmn = _mm(dkm, W_mk, trans_b=True, name="d_mn_k")
    dmn = _mm(dvm, W_mv, trans_b=True, add=dmn, name="d_mn_v")
    _, G["norm_memkv_w"] = _norm_bwd(mm, norm_memkv_w, dmn, None, name="norm_memkv_bwd")
    dx1, G["norm_mem_w"] = _norm_bwd(x1, norm_mem_w, dh2, dx2, name="norm_mem_bwd")
    G["w_out"] = _mm(ymix, dx1, trans_a=True, name="g_w_out")
    dymix = _mm(dx1, W_out, trans_b=True, name="d_ymix")
    do_sb, G["sb_norm_w"] = _head_norm_bwd(o_sb, sb_norm_w, dymix, name="sb_norm_bwd")

    dz, dxbc, ddtr, dlanes, dconv = _ssd_bwd(z, xbc, dtr, states, dymix, cw_ssd, conv_ssd_b, lanes, name="ssd_bwd")
    G["dt_bias"], G["a_log"], G["d_skip"] = [_group_sum(dlanes[i:i + 1]) for i in range(3)]
    G["ssd_norm_w"] = dlanes[3:4]
    G["conv_ssd_w"], G["conv_ssd_b"] = dconv[0:4], dconv[4:5]

    def col_slabs(g, cols):
        return g.reshape(g.shape[0], NDEV, cols).transpose(1, 0, 2).astype(BF16)

    def packed_slabs(names):
        parts = []
        for n in names:
            shp = SHARD[n]
            if shp[-1] == D:
                t = G[n].reshape((NDEV,) + shp)
            else:
                t = _pad_rows(G[n].reshape(shp[0], NDEV, shp[1]).transpose(1, 0, 2).reshape(NDEV, -1),
                              _part_rows(shp))
            parts.append(jnp.pad(t, ((0, 0), (0, _part_rows(shp) - t.shape[1]), (0, 0))))
        used = sum(t.shape[1] for t in parts)
        parts.append(jnp.zeros((NDEV, _layout(names)[1] - used, D), F32))
        return jnp.concatenate(parts, axis=1).astype(BF16)

    dq, dk, dv, recv_packed, recv_up = _sb_bwd(qkv, o_sb, do_sb, [packed_slabs(GRADS_PACKED), col_slabs(G["w_up"], 704)],
                                               name="sb_bwd")
    dproj = jnp.concatenate([dz, dxbc, ddtr, dq, dk, dv], axis=1)
    g_proj = _mm(h1, dproj, trans_a=True, name="g_w_in")
    c_dt = D + XBC
    G["w_in"] = jnp.concatenate([g_proj[:, :c_dt], g_proj[:, c_dt:c_dt + D].reshape(D, 16, 64).sum(axis=2),
                                 g_proj[:, c_dt + D:]], axis=1)
    W_proj = jnp.concatenate([W_z, W_xbc, W_dtr, W_qkv], axis=1)
    dh1, recv_in = _mm(dproj, W_proj, trans_b=True, rides=[col_slabs(G["w_in"], 706)], name="d_h1")
    dx, G["norm_mix_w"] = _norm_bwd(xs, norm_mix_w, dh1, dx1, name="norm_mix_bwd")
    G["norm_final_w"] = g_nfinal.reshape(D)

    small_g = _pack_tight([G[n] for n, _ in SMALL] + [loss_part], SMALL_ROWS)
    (parts_small,) = _exchange([small_g], [True], name="exchange_grads")
    outs_packed = _adamw(recv_packed, packed(P, GRADS_PACKED), packed(M, GRADS_PACKED), packed(V, GRADS_PACKED),
                         name="adamw_packed")
    outs_up = _adamw(recv_up, w_up[0], m_w_up[0], v_w_up[0], name="adamw_w_up")
    outs_in = _adamw(recv_in, w_in[0], m_w_in[0], v_w_in[0], name="adamw_w_in")
    outs_small = _adamw(parts_small, _pack_tight([P[n] for n, _ in SMALL], SMALL_ROWS),
                        _pack_tight([M[n] for n, _ in SMALL], SMALL_ROWS),
                        _pack_tight([V[n] for n, _ in SMALL], SMALL_ROWS), name="adamw_replicated")

    res = {}
    for i, kind in enumerate(("grad", "delta", "new_m", "new_v")):
        for n, val in zip(GRADS_PACKED, _unpack(outs_packed[i], [SHARD[n] for n in GRADS_PACKED])):
            res[kind, n] = val.reshape((1,) + SHARD[n])
        res[kind, "w_up"] = outs_up[i].reshape((1,) + SHARD["w_up"])
        res[kind, "w_in"] = outs_in[i].reshape((1,) + SHARD["w_in"])
        for (n, shp), val in zip(SMALL, _unpack(outs_small[i], small_shapes, _tight_rows)):
            res[kind, n] = val
    loss = outs_small[0][LOSS_ROW, 0]
    out = [loss, dx.reshape(1, -1, D)]
    for kind in ("grad", "delta", "new_m", "new_v"):
        out += [res[kind, n] for n in ORDER]
    return tuple(out)
```

```python
import functools
import math

import jax
import jax.numpy as jnp
from jax import lax
from jax.experimental import pallas as pl
from jax.experimental.pallas import tpu as pltpu

F32 = jnp.float32
BF16 = jnp.bfloat16

D = 1024
NDEV = 8
EPS = 1e-6
SSD_CHUNK = 128
HALO = 8
VMEM_LIMIT = 56 * 2**20

ADAM_LR, ADAM_B1, ADAM_B2, ADAM_EPS, ADAM_WD, ADAM_STEP = 0.001, 0.9, 0.999, 1e-08, 0.01, 10


def _cp(*sem):
    return pltpu.CompilerParams(dimension_semantics=sem, vmem_limit_bytes=VMEM_LIMIT)


def _tile(n, cap, mult):
    if n <= cap:
        return n
    for d in range(cap - cap % mult, 0, -mult):
        if n % d == 0:
            return d
    raise ValueError(f"no tile for {n}")


def _sigmoid(x):
    return 1.0 / (1.0 + jnp.exp(-x))


def _silu(x):
    return x * _sigmoid(x)


def _softplus(x):
    return jnp.maximum(x, 0.0) + jnp.log(1.0 + jnp.exp(-jnp.abs(x)))


def _terms(x, n):
    out = []
    r = x.astype(F32)
    for i in range(n):
        h = r.astype(BF16)
        out.append(h)
        if i + 1 < n:
            r = r - h.astype(F32)
    return out


_DIMS = {"nn": ((1,), (0,)), "nt": ((1,), (1,)), "tn": ((0,), (0,))}


def _dot_raw(form, a, b, ta, tb):
    acc = None
    for ai in _terms(a, ta):
        for bi in _terms(b, tb):
            d = lax.dot_general(ai, bi, (_DIMS[form], ((), ())), preferred_element_type=F32)
            acc = d if acc is None else acc + d
    return acc


@functools.lru_cache(maxsize=None)
def _dot_fn(form, ta, tb):
    @jax.custom_vjp
    def f(a, b):
        return _dot_raw(form, a, b, ta, tb)

    def fwd(a, b):
        return f(a, b), (a, b)

    def bwd(res, ct):
        a, b = res
        if form == "nn":
            return _dot_fn("nt", ta, tb)(ct, b), _dot_fn("tn", ta, tb)(a, ct)
        if form == "nt":
            return _dot_fn("nn", ta, tb)(ct, b), _dot_fn("tn", tb, ta)(ct, a)
        return _dot_fn("nt", tb, ta)(b, ct), _dot_fn("nn", ta, tb)(a, ct)

    f.defvjp(fwd, bwd)
    return f


def _dot(form, a, b, ta=1, tb=1):
    return _dot_fn(form, ta, tb)(a, b)


@functools.lru_cache(maxsize=None)
def _take_fn(axis, idx):
    @jax.custom_vjp
    def f(x):
        return x[:, idx:idx + 1] if axis == 1 else x[idx:idx + 1, :]

    def fwd(x):
        return f(x), x.shape

    def bwd(shape, ct):
        io = lax.broadcasted_iota(jnp.int32, shape, axis)
        return (jnp.where(io == idx, jnp.broadcast_to(ct, shape), 0.0),)

    f.defvjp(fwd, bwd)
    return f


@functools.lru_cache(maxsize=None)
def _split_fn(width, n):
    @jax.custom_vjp
    def f(x):
        return tuple(x[:, i * width:(i + 1) * width] for i in range(n))

    def fwd(x):
        return f(x), None

    def bwd(_, cts):
        return (jnp.concatenate(list(cts), axis=1),)

    f.defvjp(fwd, bwd)
    return f


def _split(x, width):
    return _split_fn(width, x.shape[1] // width)(x)


def _iota(shape, axis):
    return lax.broadcasted_iota(jnp.int32, shape, axis)


MM_VMEM_BUDGET = 44 * 2**20


def _mm_tiles(m, n, kt, trans_a, a_bytes, b_bytes, out_bytes, add_bytes):
    tn = _tile(n, 1536, 128)
    for tm_cap in (1408, 1024, 512, 256, 128):
        tm = _tile(m, tm_cap, 128 if trans_a else 8)
        for tk_cap in (kt, 4096, 2048, 1024, 512):
            tk = _tile(kt, tk_cap, 128)
            blocks = tm * tk * a_bytes + tk * tn * b_bytes + tm * tn * (out_bytes + add_bytes)
            if 2 * blocks + (tm * tn * 4 if tk < kt else 0) <= MM_VMEM_BUDGET:
                return tm, tn, tk
    raise ValueError(f"no matmul tiling for {(m, n, kt)}")


def _mm(a, b, *, name, add=None, trans_a=False, trans_b=False, out_dtype=F32, rides=()):
    assert not (trans_a and trans_b)
    if trans_a:
        kt, m = a.shape
    else:
        m, kt = a.shape
    n, kt2 = b.shape if trans_b else b.shape[::-1]
    assert kt == kt2, (a.shape, b.shape)
    tm, tn, tk = _mm_tiles(m, n, kt, trans_a, a.dtype.itemsize, b.dtype.itemsize,
                           jnp.dtype(out_dtype).itemsize, 0 if add is None else add.dtype.itemsize)
    nk = kt // tk
    grid = (m // tm, n // tn, nk)
    rd = _Rides(rides, [False] * len(rides))
    n_in = 2 if add is None else 3

    def body(*all_refs):
        ins, (o_ref,), scratch, handles = rd.split(all_refs, n_in, 1, 1 if nk > 1 else 0)
        refs = (*ins, o_ref, *scratch)
        if rides:
            ids = [pl.program_id(ax) for ax in range(3)]
            rd.run(handles, (ids[0] == 0) & (ids[1] == 0) & (ids[2] == 0),
                   (ids[0] == grid[0] - 1) & (ids[1] == grid[1] - 1) & (ids[2] == grid[2] - 1))
        if add is None:
            a_ref, b_ref, o_ref = refs[:3]
        else:
            a_ref, b_ref, add_ref, o_ref = refs[:4]
        k = pl.program_id(2)
        av = a_ref[...].astype(BF16)
        bv = b_ref[...].astype(BF16)
        dims = _DIMS["tn" if trans_a else "nt" if trans_b else "nn"]
        d = lax.dot_general(av, bv, (dims, ((), ())), preferred_element_type=F32)

        def finish(r):
            if add is not None:
                r = r + add_ref[...]
            o_ref[...] = r.astype(out_dtype)

        if nk == 1:
            finish(d)
        else:
            acc = refs[-1]

            @pl.when(k == 0)
            def _():
                acc[...] = d

            @pl.when((k > 0) & (k < nk - 1))
            def _():
                acc[...] += d

            @pl.when(k == nk - 1)
            def _():
                finish(acc[...] + d)

    a_spec = (pl.BlockSpec((tk, tm), lambda i, j, k: (k, i)) if trans_a
              else pl.BlockSpec((tm, tk), lambda i, j, k: (i, k)))
    b_spec = (pl.BlockSpec((tn, tk), lambda i, j, k: (j, k)) if trans_b
              else pl.BlockSpec((tk, tn), lambda i, j, k: (k, j)))
    in_specs = [a_spec, b_spec]
    args = [a, b]
    if add is not None:
        in_specs.append(pl.BlockSpec((tm, tn), lambda i, j, k: (i, j)))
        args.append(add)
    out = pl.pallas_call(
        body, name=name, grid=grid,
        in_specs=in_specs + rd.in_specs,
        out_specs=[pl.BlockSpec((tm, tn), lambda i, j, k: (i, j))] + rd.out_specs,
        out_shape=[jax.ShapeDtypeStruct((m, n), out_dtype)] + rd.out_shape,
        scratch_shapes=([pltpu.VMEM((tm, tn), F32)] if nk > 1 else []) + rd.scratch,
        compiler_params=_cp(*(("arbitrary",) * 3 if rides else ("parallel", "parallel", "arbitrary"))),
    )(*args, *rides)
    return out if rides else out[0]


def _rstd(x):
    return lax.rsqrt(jnp.mean(x * x, axis=-1, keepdims=True) + EPS)


def _norm_fwd(x, w, *, name):
    s = x.shape[0]
    tm = _tile(s, 512, 8)

    def body(x_ref, w_ref, o_ref):
        xv = x_ref[...]
        o_ref[...] = (xv * _rstd(xv) * w_ref[...]).astype(BF16)

    return pl.pallas_call(
        body, name=name, grid=(s // tm,),
        in_specs=[pl.BlockSpec((tm, D), lambda i: (i, 0)), pl.BlockSpec((1, D), lambda i: (0, 0))],
        out_specs=pl.BlockSpec((tm, D), lambda i: (i, 0)),
        out_shape=jax.ShapeDtypeStruct((s, D), BF16), compiler_params=_cp("parallel"),
    )(x, w)


def _norm_bwd_math(xv, wv, dy):
    r = _rstd(xv)
    xh = xv * r
    dxh = dy * wv
    dx = r * (dxh - xh * jnp.mean(dxh * xh, axis=-1, keepdims=True))
    dw = jnp.sum(dy * xh, axis=0, keepdims=True)
    return dx, dw


def _norm_bwd(x, w, dy, add, *, name):
    s = x.shape[0]
    tm = _tile(s, 512, 8)

    def body(*refs):
        if add is None:
            x_ref, w_ref, dy_ref, dx_ref, dw_ref = refs
        else:
            x_ref, w_ref, dy_ref, add_ref, dx_ref, dw_ref = refs

        @pl.when(pl.program_id(0) == 0)
        def _():
            dw_ref[...] = jnp.zeros_like(dw_ref)

        dx, dw = _norm_bwd_math(x_ref[...], w_ref[...], dy_ref[...])
        if add is not None:
            dx = dx + add_ref[...]
        dx_ref[...] = dx
        dw_ref[...] += dw

    row = pl.BlockSpec((tm, D), lambda i: (i, 0))
    vec = pl.BlockSpec((1, D), lambda i: (0, 0))
    in_specs = [row, vec, row] + ([row] if add is not None else [])
    args = [x, w, dy] + ([add] if add is not None else [])
    return pl.pallas_call(
        body, name=name, grid=(s // tm,), in_specs=in_specs, out_specs=[row, vec],
        out_shape=[jax.ShapeDtypeStruct((s, D), F32), jax.ShapeDtypeStruct((1, D), F32)],
        compiler_params=_cp("arbitrary"),
    )(*args)


def _final(x3, w, target, *, name):
    s = x3.shape[0]
    tm = _tile(s, 512, 8)

    def body(x_ref, w_ref, t_ref, dx_ref, dw_ref, loss_ref):
        @pl.when(pl.program_id(0) == 0)
        def _():
            dw_ref[...] = jnp.zeros_like(dw_ref)
            loss_ref[...] = jnp.zeros_like(loss_ref)

        xv = x_ref[...]
        wv = w_ref[...]
        y = xv * _rstd(xv) * wv
        err = y - t_ref[...]
        loss_ref[...] += 0.5 * jnp.sum(jnp.mean(err * err, axis=-1, keepdims=True))
        dx, dw = _norm_bwd_math(xv, wv, err * (1.0 / D))
        dx_ref[...] = dx
        dw_ref[...] += dw

    row = pl.BlockSpec((tm, D), lambda i: (i, 0))
    vec = pl.BlockSpec((1, D), lambda i: (0, 0))
    return pl.pallas_call(
        body, name=name, grid=(s // tm,), in_specs=[row, vec, row], out_specs=[row, vec, vec],
        out_shape=[jax.ShapeDtypeStruct((s, D), F32), jax.ShapeDtypeStruct((1, D), F32),
                   jax.ShapeDtypeStruct((1, D), F32)],
        compiler_params=_cp("arbitrary"),
    )(x3, w, target)


def _head_norm_math(o, w):
    lane = _iota((128, 128), 0) // 64
    bd = (lane == _iota((128, 128), 1) // 64).astype(F32)
    outs = []
    for op in _split(o, 128):
        ms = _dot("nn", op * op, bd, 2, 1) * (1.0 / 64)
        outs.append(op * lax.rsqrt(ms + EPS))
    return jnp.concatenate(outs, axis=1) * w


def _head_norm_fwd(o, w, *, name):
    s = o.shape[0]
    tm = _tile(s, 512, 8)

    def body(o_ref, w_ref, y_ref):
        y_ref[...] = _head_norm_math(o_ref[...], w_ref[...]).astype(BF16)

    row = pl.BlockSpec((tm, D), lambda i: (i, 0))
    vec = pl.BlockSpec((1, D), lambda i: (0, 0))
    return pl.pallas_call(
        body, name=name, grid=(s // tm,), in_specs=[row, vec], out_specs=row,
        out_shape=jax.ShapeDtypeStruct((s, D), BF16), compiler_params=_cp("parallel"),
    )(o, w)


def _head_norm_bwd(o, w, dymix, *, name):
    s = o.shape[0]
    tm = _tile(s, 512, 8)

    def body(o_ref, w_ref, dy_ref, do_ref, dw_ref):
        @pl.when(pl.program_id(0) == 0)
        def _():
            dw_ref[...] = jnp.zeros_like(dw_ref)

        _, vjp = jax.vjp(_head_norm_math, o_ref[...], w_ref[...])
        do, dw = vjp(dy_ref[...])
        do_ref[...] = do
        dw_ref[...] += dw

    row = pl.BlockSpec((tm, D), lambda i: (i, 0))
    vec = pl.BlockSpec((1, D), lambda i: (0, 0))
    return pl.pallas_call(
        body, name=name, grid=(s // tm,),
        in_specs=[row, vec, pl.BlockSpec((tm, D), lambda i: (i, 1))], out_specs=[row, vec],
        out_shape=[jax.ShapeDtypeStruct((s, D), F32), jax.ShapeDtypeStruct((1, D), F32)],
        compiler_params=_cp("arbitrary"),
    )(o, w, dymix)


SB_BQ = 256
SB_BK = 256


def _sb_consts():
    r = _iota((SB_BK, SB_BK), 0)
    c = _iota((SB_BK, SB_BK), 1)
    u_excl = (r > c).astype(BF16)
    u_incl = (r >= c).astype(BF16)
    return u_excl, u_incl


SB_LANES = 256
SB_NCH = SB_LANES // 64


def _nt(a, b):
    return lax.dot_general(a, b, (_DIMS["nt"], ((), ())), preferred_element_type=F32)


def _tn(a, b):
    return lax.dot_general(a, b, (_DIMS["tn"], ((), ())), preferred_element_type=F32)


def _nn(a, b):
    return jnp.dot(a, b, preferred_element_type=F32)


def _sb_heads(ref):
    out = []
    for hp in range(SB_LANES // 128):
        v = ref[:, 128 * hp:128 * (hp + 1)]
        first = _iota(v.shape, 1) < 64
        out += [jnp.where(first, v, 0).astype(BF16), jnp.where(first, 0, v).astype(BF16)]
    return out


SB_STRIP = 32


def _neg_abs(x):
    bits = lax.bitcast_convert_type(x, jnp.uint32) | jnp.uint32(0x80000000)
    return lax.bitcast_convert_type(bits, F32)


def _sb_block(ref, j):
    off = pl.multiple_of(j * SB_BK, SB_BK)
    return [ref[pl.ds(off, SB_BK), 128 * hp:128 * (hp + 1)] for hp in range(SB_NCH // 2)]


SB_DEAD = 104.0


def _sb_live(nlrun):
    m = nlrun[0]
    for x in nlrun[1:]:
        m = jnp.minimum(m, x)
    return jnp.min(m) < SB_DEAD


def _sb_strips():
    return [(r, pl.ds(r, SB_STRIP)) for r in range(0, SB_BQ, SB_STRIP)]


def _sb_diag_mask(r):
    return _iota((SB_STRIP, SB_BK), 1) < _iota((SB_STRIP, SB_BK), 0) + r


def _sb_soft(z, mask):
    e = jnp.exp(_neg_abs(z))
    nl = jnp.maximum(z, 0.0) + jnp.log(1.0 + e)
    if mask is not None:
        nl = jnp.where(mask, nl, 0.0)
    return e, nl


def _sb_split_to(hl_ref, rows, x):
    hi, lo = _terms(x, 2)
    hl_ref[rows, 0:SB_BK] = hi
    hl_ref[rows, SB_BK:2 * SB_BK] = lo


def _sb_stage_soft(z_ref, nl_ref, diag):
    for r, rows in _sb_strips():
        _, nl = _sb_soft(z_ref[rows, :], _sb_diag_mask(r) if diag else None)
        nl_ref[rows, 0:SB_BK] = nl.astype(BF16)


def _sb_stage_weights(z_ref, c_ref, a_ref, nlrun, diag):
    for r, rows in _sb_strips():
        a = jnp.exp(z_ref[rows, :] - c_ref[rows, :] - nlrun[r:r + SB_STRIP, :])
        if diag:
            a = jnp.where(_sb_diag_mask(r), a, 0.0)
        a_ref[rows, :] = a.astype(BF16)


def _sb_fwd(qkv, rides, *, name):
    s = qkv.shape[0]
    nq = s // SB_BQ
    ng = D // SB_LANES
    assert SB_BQ == SB_BK
    rd = _Rides(rides, [True] * len(rides))

    def body(*refs):
        (q_ref, k_ref, v_ref), (o_ref,), (zbuf, nlbuf, cbuf, abuf), handles = rd.split(refs, 3, 1, 4)
        i = pl.program_id(1)
        step_no = pl.program_id(0) * nq + i
        rd.run(handles, step_no == 0, step_no == ng * nq - 1)

        _, u_incl = _sb_consts()
        lane_a = _iota((SB_BQ, 128), 1) < 64
        qh = [q * 0.125 for q in _sb_heads(q_ref)]

        def tile(j, accs, nlrun, diag):
            kbs = _sb_block(k_ref, j)
            for c in range(SB_NCH):
                zbuf[c] = _nt(qh[c], kbs[c // 2])
            for c in range(SB_NCH):
                _sb_stage_soft(zbuf.at[c], nlbuf.at[c], diag)
                cbuf[c] = _nn(nlbuf[c], u_incl)
            for c in range(SB_NCH):
                _sb_stage_weights(zbuf.at[c], cbuf.at[c], abuf.at[c], nlrun[c], diag)
            nlrun = tuple(nlrun[c] + cbuf[c, :, 0:1] for c in range(SB_NCH))
            vbs = _sb_block(v_ref, j)
            outs = [_nn(abuf[c], vbs[c // 2]) for c in range(SB_NCH)]
            accs = tuple(acc + jnp.where(lane_a, outs[2 * hp], outs[2 * hp + 1]) for hp, acc in enumerate(accs))
            return accs, nlrun

        accs, nlrun = tile(i, (jnp.zeros((SB_BQ, 128), F32),) * (SB_NCH // 2),
                           (jnp.zeros((SB_BQ, 1), F32),) * SB_NCH, True)

        def step(carry):
            j, _, accs, nlrun = carry
            accs, nlrun = tile(j, accs, nlrun, False)
            return j - 1, _sb_live(nlrun), accs, nlrun

        _, _, accs, _ = lax.while_loop(lambda c: (c[0] >= 0) & c[1], step, (i - 1, _sb_live(nlrun), accs, nlrun))
        o_ref[...] = jnp.concatenate(accs, axis=1)

    return pl.pallas_call(
        body, name=name, grid=(ng, nq),
        in_specs=[pl.BlockSpec((SB_BQ, SB_LANES), lambda g, i: (i, g)),
                  pl.BlockSpec((s, SB_LANES), lambda g, i: (0, ng + g)),
                  pl.BlockSpec((s, SB_LANES), lambda g, i: (0, 2 * ng + g)), *rd.in_specs],
        out_specs=[pl.BlockSpec((SB_BQ, SB_LANES), lambda g, i: (i, g)), *rd.out_specs],
        out_shape=[jax.ShapeDtypeStruct((s, D), F32), *rd.out_shape],
        scratch_shapes=[pltpu.VMEM((SB_NCH, SB_BQ, SB_BK), F32), pltpu.VMEM((SB_NCH, SB_BQ, SB_BK), BF16),
                        pltpu.VMEM((SB_NCH, SB_BQ, SB_BK), F32), pltpu.VMEM((SB_NCH, SB_BQ, SB_BK), BF16),
                        *rd.scratch],
        compiler_params=_cp("arbitrary", "arbitrary"),
    )(qkv, qkv, qkv, *rides)


def _sb_bwd(qkv, o, do, rides, *, name):
    s = qkv.shape[0]
    nq = s // SB_BQ
    ng = D // SB_LANES
    nhp = SB_NCH // 2
    rd = _Rides(rides, [False] * len(rides))

    def body(*refs):
        ins, outs, scratch, handles = rd.split(refs, 5, 3, 11)
        q_ref, k_ref, v_ref, o_ref, do_ref = ins
        dq_ref, dk_hbm, dv_hbm = outs
        dk_acc, dv_acc, dk16, dv16, sems, zbuf, gbuf, hl, cbuf, abuf, dzbuf = scratch
        g_idx = pl.program_id(0)
        i = pl.program_id(1)
        step_no = g_idx * nq + i
        rd.run(handles, step_no == 0, step_no == ng * nq - 1)

        @pl.when(i == 0)
        def _():
            dk_acc[...] = jnp.zeros_like(dk_acc)
            dv_acc[...] = jnp.zeros_like(dv_acc)

        _, u_incl = _sb_consts()
        u2 = jnp.concatenate([u_incl, u_incl], axis=0)
        lane_a = _iota((SB_BQ, 128), 1) < 64
        lane_k = _iota((SB_BK, 128), 1) < 64
        qh = [q * 0.125 for q in _sb_heads(q_ref)]
        qf = [q_ref[:, 128 * hp:128 * (hp + 1)] for hp in range(nhp)]
        doh = _sb_heads(do_ref)
        dof = [do_ref[:, 128 * hp:128 * (hp + 1)].astype(BF16) for hp in range(nhp)]
        delta = []
        for hp in range(nhp):
            prod = dof[hp].astype(F32) * o_ref[:, 128 * hp:128 * (hp + 1)]
            delta += [jnp.sum(jnp.where(lane_a, prod, 0.0), axis=1, keepdims=True),
                      jnp.sum(jnp.where(lane_a, 0.0, prod), axis=1, keepdims=True)]

        def pre(slot, j):
            kbs = _sb_block(k_ref, j)
            vbs = _sb_block(v_ref, j)
            for c in range(SB_NCH):
                zbuf[slot, c] = _nt(qh[c], kbs[c // 2])
                gbuf[slot, c] = _nt(doh[c], vbs[c // 2])

        def stage_g(c, slot):
            for _, rows in _sb_strips():
                g = abuf[slot, c, rows, :].astype(F32) * gbuf[slot, c, rows, :]
                gbuf[slot, c, rows, :] = g
                _sb_split_to(hl.at[c], rows, g)

        def stage_dz(c, slot, grun, diag):
            for r, rows in _sb_strips():
                z = zbuf[slot, c, rows, :]
                g = gbuf[slot, c, rows, :]
                cs = (delta[c] - grun)[r:r + SB_STRIP, :] - cbuf[c, rows, :]
                sig = 1.0 / (1.0 + jnp.exp(-z))
                dz = g - (g + cs) * sig
                if diag:
                    dz = jnp.where(_sb_diag_mask(r), dz, 0.0)
                dzbuf[slot, c, rows, :] = dz.astype(BF16)

        def chain(slot, nlrun, grun, diag):
            for c in range(SB_NCH):
                _sb_stage_soft(zbuf.at[slot, c], hl.at[c], diag)
                cbuf[c] = _nn(hl[c, :, 0:SB_BK], u_incl)
            nl_tot = []
            for c in range(SB_NCH):
                _sb_stage_weights(zbuf.at[slot, c], cbuf.at[c], abuf.at[slot, c], nlrun[c], diag)
                nl_tot.append(cbuf[c, :, 0:1])
                stage_g(c, slot)
                cbuf[c] = _nn(hl[c], u2)
            g_tot = []
            for c in range(SB_NCH):
                stage_dz(c, slot, grun[c], diag)
                g_tot.append(cbuf[c, :, 0:1])
            return (tuple(a + b for a, b in zip(nlrun, nl_tot)), tuple(a + b for a, b in zip(grun, g_tot)))

        def post(slot, j, dqs):
            off = pl.multiple_of(j * SB_BK, SB_BK)
            kbs = _sb_block(k_ref, j)
            dq_t = [_nn(dzbuf[slot, c], kbs[c // 2]) for c in range(SB_NCH)]
            dk_t = [_tn(dzbuf[slot, c], qf[c // 2]) for c in range(SB_NCH)]
            dv_t = [_tn(abuf[slot, c], dof[c // 2]) for c in range(SB_NCH)]
            for hp in range(nhp):
                cols = slice(128 * hp, 128 * (hp + 1))
                dk_acc[pl.ds(off, SB_BK), cols] += 0.125 * jnp.where(lane_k, dk_t[2 * hp], dk_t[2 * hp + 1])
                dv_acc[pl.ds(off, SB_BK), cols] += jnp.where(lane_k, dv_t[2 * hp], dv_t[2 * hp + 1])
            return tuple(dq + jnp.where(lane_a, dq_t[2 * hp], dq_t[2 * hp + 1]) for hp, dq in enumerate(dqs))

        def tile(j, dqs, nlrun, grun, diag):
            pre(0, j)
            nlrun, grun = chain(0, nlrun, grun, diag)
            return post(0, j, dqs), nlrun, grun

        zero = (jnp.zeros((SB_BQ, 1), F32),) * SB_NCH
        dqs, nlrun, grun = tile(i, (jnp.zeros((SB_BQ, 128), F32),) * nhp, zero, zero, True)

        def step(carry):
            j, _, dqs, nlrun, grun = carry
            dqs, nlrun, grun = tile(j, dqs, nlrun, grun, False)
            return j - 1, _sb_live(nlrun), dqs, nlrun, grun

        carry = lax.while_loop(lambda c: (c[0] >= 0) & c[1], step, (i - 1, _sb_live(nlrun), dqs, nlrun, grun))
        dq_ref[...] = (0.125 * jnp.concatenate(carry[2], axis=1)).astype(BF16)

        def out_copies(g):
            cols = pl.ds(pl.multiple_of(g * SB_LANES, SB_LANES), SB_LANES)
            return (pltpu.make_async_copy(dk16, dk_hbm.at[:, cols], sems.at[0]),
                    pltpu.make_async_copy(dv16, dv_hbm.at[:, cols], sems.at[1]))

        @pl.when((i == nq - 1) & (g_idx > 0))
        def _():
            for cp in out_copies(g_idx - 1):
                cp.wait()

        @pl.when(i == nq - 1)
        def _():
            def narrow(r, carry):
                rows = pl.ds(pl.multiple_of(r * SB_BK, SB_BK), SB_BK)
                dk16[rows, :] = dk_acc[rows, :].astype(BF16)
                dv16[rows, :] = dv_acc[rows, :].astype(BF16)
                return carry

            lax.fori_loop(0, s // SB_BK, narrow, 0)
            for cp in out_copies(g_idx):
                cp.start()

        @pl.when((i == nq - 1) & (g_idx == ng - 1))
        def _():
            for cp in out_copies(g_idx):
                cp.wait()

    qblk = pl.BlockSpec((SB_BQ, SB_LANES), lambda g, i: (i, g))
    hbm = pl.BlockSpec(memory_space=pl.ANY)
    return pl.pallas_call(
        body, name=name, grid=(ng, nq),
        in_specs=[qblk, pl.BlockSpec((s, SB_LANES), lambda g, i: (0, ng + g)),
                  pl.BlockSpec((s, SB_LANES), lambda g, i: (0, 2 * ng + g)), qblk, qblk, *rd.in_specs],
        out_specs=[qblk, hbm, hbm, *rd.out_specs],
        out_shape=[jax.ShapeDtypeStruct((s, D), BF16)] * 3 + rd.out_shape,
        scratch_shapes=[pltpu.VMEM((s, SB_LANES), F32), pltpu.VMEM((s, SB_LANES), F32),
                        pltpu.VMEM((s, SB_LANES), BF16), pltpu.VMEM((s, SB_LANES), BF16),
                        pltpu.SemaphoreType.DMA((2,)),
                        pltpu.VMEM((1, SB_NCH, SB_BQ, SB_BK), F32), pltpu.VMEM((1, SB_NCH, SB_BQ, SB_BK), F32),
                        pltpu.VMEM((SB_NCH, SB_BQ, 2 * SB_BK), BF16), pltpu.VMEM((SB_NCH, SB_BQ, SB_BK), F32),
                        pltpu.VMEM((1, SB_NCH, SB_BQ, SB_BK), BF16), pltpu.VMEM((1, SB_NCH, SB_BQ, SB_BK), BF16),
                        *rd.scratch],
        compiler_params=_cp("arbitrary", "arbitrary"),
    )(qkv, qkv, qkv, o, do, *rides)


def _ssd_core(z, xpre, dtr, state, dtb, alog, dsk, nw):
    L = SSD_CHUNK
    xa = _silu(xpre)
    pieces = _split(xa, 128)
    xs = jnp.concatenate(pieces[:8], axis=1)
    bm, cm = pieces[8:10], pieces[10:12]
    dt = _softplus(dtr + dtb)
    a = dt * (-jnp.exp(alog))
    tri = (_iota((L, L), 0) >= _iota((L, L), 1)).astype(F32)
    a_cs = _dot("nn", tri, a, 1, 3)
    xc = xs * dt
    tril = _iota((L, L), 0) >= _iota((L, L), 1)
    lane_a = _iota((L, 128), 1) < 64
    acs_p = _split(a_cs, 128)
    xc_p = _split(xc, 128)
    ys, new_states = [], []
    for g in range(2):
        cb = _dot("nt", cm[g], bm[g])
        for pp in range(4):
            pair = 4 * g + pp
            acs = acs_p[pair]
            acs_t = acs.T
            xcp = xc_p[pair]
            st = state[pair]
            heads = []
            for hh in range(2):
                col = _take_fn(1, 64 * hh)(acs)
                row = _take_fn(0, 64 * hh)(acs_t)
                seg = col - row
                lm = jnp.where(tril, jnp.exp(jnp.where(tril, seg, 0.0)), 0.0)
                heads.append(_dot("nn", cb * lm, xcp))
            ydiag = jnp.where(lane_a, heads[0], heads[1])
            last = _take_fn(0, L - 1)(acs)
            snew = _dot("tn", xcp * jnp.exp(last - acs), bm[g])
            new_states.append(st * jnp.exp(_take_fn(1, L - 1)(acs_t)) + snew)
            yoff = _dot("nt", cm[g], st) * jnp.exp(acs)
            ys.append(ydiag + yoff)
    y = jnp.concatenate(ys, axis=1) + xs * dsk
    yg = y * _silu(z)
    outs = []
    for v in _split(yg, 512):
        outs.append(v * lax.rsqrt(jnp.mean(v * v, axis=-1, keepdims=True) + EPS))
    return jnp.concatenate(outs, axis=1) * nw, tuple(new_states)


XBC = 1536


def _ssd_conv(ext_ref, cw, cb):
    acc = cb
    for k in range(4):
        acc = acc + cw[k:k + 1, :] * ext_ref[pl.ds(HALO - 3 + k, SSD_CHUNK), :]
    return acc


def _ssd_fwd(z, xbc, dtr, cw, cb, lanes, rides, *, name):
    s = z.shape[0]
    L = SSD_CHUNK
    nc = s // L
    rd = _Rides(rides, [True] * len(rides))

    def body(*refs):
        ins, (y_ref, st_ref), (state, ext), handles = rd.split(refs, 7, 2, 2)
        z_ref, x_ref, h_ref, dtr_ref, cw_ref, cb_ref, ln_ref = ins
        c = pl.program_id(0)
        rd.run(handles, c == 0, c == nc - 1)

        @pl.when(c == 0)
        def _():
            state[...] = jnp.zeros_like(state)

        ext[0:HALO, :] = jnp.where(c == 0, 0.0, h_ref[...])
        ext[HALO:, :] = x_ref[...]
        xpre = _ssd_conv(ext, cw_ref[...], cb_ref[...])
        st_ref[0] = state[...]
        st_in = tuple(state[p] for p in range(8))
        yn, st_out = _ssd_core(z_ref[...], xpre, dtr_ref[...], st_in,
                               ln_ref[0:1, :], ln_ref[1:2, :], ln_ref[2:3, :], ln_ref[3:4, :])
        y_ref[...] = yn.astype(BF16)
        for p in range(8):
            state[p] = st_out[p]

    return pl.pallas_call(
        body, name=name, grid=(nc,),
        in_specs=[pl.BlockSpec((L, D), lambda c: (c, 0)),
                  pl.BlockSpec((L, XBC), lambda c: (c, 0)),
                  pl.BlockSpec((HALO, XBC), lambda c: (jnp.maximum(c * (L // HALO) - 1, 0), 0)),
                  pl.BlockSpec((L, D), lambda c: (c, 0)),
                  pl.BlockSpec((4, XBC), lambda c: (0, 0)),
                  pl.BlockSpec((1, XBC), lambda c: (0, 0)),
                  pl.BlockSpec((8, D), lambda c: (0, 0)), *rd.in_specs],
        out_specs=[pl.BlockSpec((L, D), lambda c: (c, 0)),
                   pl.BlockSpec((1, 8, 128, 128), lambda c: (c, 0, 0, 0)), *rd.out_specs],
        out_shape=[jax.ShapeDtypeStruct((s, D), BF16), jax.ShapeDtypeStruct((nc, 8, 128, 128), F32),
                   *rd.out_shape],
        scratch_shapes=[pltpu.VMEM((8, 128, 128), F32), pltpu.VMEM((L + HALO, XBC), F32), *rd.scratch],
        compiler_params=_cp("arbitrary"),
    )(z, xbc, xbc, dtr, cw, cb, lanes, *rides)


def _ssd_bwd(z, xbc, dtr, states, dymix, cw, cb, lanes, *, name):
    s = z.shape[0]
    L = SSD_CHUNK
    nc = s // L

    def body(z_ref, x_ref, h_ref, dtr_ref, st_ref, dy_ref, cw_ref, cb_ref, ln_ref,
             dz_ref, dx_ref, ddt_ref, dln_ref, dcv_ref, dstate, ext, dext):
        i = pl.program_id(0)
        c = nc - 1 - i

        @pl.when(i == 0)
        def _():
            dstate[...] = jnp.zeros_like(dstate)
            dext[...] = jnp.zeros_like(dext)
            dln_ref[...] = jnp.zeros_like(dln_ref)
            dcv_ref[...] = jnp.zeros_like(dcv_ref)

        ext[0:HALO, :] = jnp.where(c == 0, 0.0, h_ref[...])
        ext[HALO:, :] = x_ref[...]
        cwv = cw_ref[...]
        xpre = _ssd_conv(ext, cwv, cb_ref[...])
        st_in = tuple(st_ref[0, p] for p in range(8))
        _, vjp = jax.vjp(_ssd_core, z_ref[...], xpre, dtr_ref[...], st_in,
                         ln_ref[0:1, :], ln_ref[1:2, :], ln_ref[2:3, :], ln_ref[3:4, :])
        dz, dxpre, ddtr, dst, d0, d1, d2, d3 = vjp((dy_ref[...], tuple(dstate[p] for p in range(8))))
        for p in range(8):
            dstate[p] = dst[p]
        dz_ref[...] = dz.astype(BF16)
        ddt_ref[...] = ddtr.astype(BF16)
        dln_ref[0:4, :] += jnp.concatenate([d0, d1, d2, d3], axis=0)
        dext[0:L, :] = dxpre
        xcur = x_ref[...]
        dx = jnp.zeros((L, XBC), F32)
        rows = []
        for k in range(4):
            shifted = dext[pl.ds(3 - k, L), :]
            dx = dx + cwv[k:k + 1, :] * shifted
            rows.append(jnp.sum(shifted * xcur, axis=0, keepdims=True))
        rows.append(jnp.sum(dxpre, axis=0, keepdims=True))
        dx_ref[...] = dx.astype(BF16)
        dcv_ref[0:5, :] += jnp.concatenate(rows, axis=0)
        dext[L:L + HALO, :] = dxpre[0:HALO, :]

    rev = lambda i: (nc - 1 - i, 0)
    return pl.pallas_call(
        body, name=name, grid=(nc,),
        in_specs=[pl.BlockSpec((L, D), rev),
                  pl.BlockSpec((L, XBC), rev),
                  pl.BlockSpec((HALO, XBC), lambda i: (jnp.maximum((nc - 1 - i) * (L // HALO) - 1, 0), 0)),
                  pl.BlockSpec((L, D), rev),
                  pl.BlockSpec((1, 8, 128, 128), lambda i: (nc - 1 - i, 0, 0, 0)),
                  pl.BlockSpec((L, D), rev),
                  pl.BlockSpec((4, XBC), lambda i: (0, 0)),
                  pl.BlockSpec((1, XBC), lambda i: (0, 0)),
                  pl.BlockSpec((8, D), lambda i: (0, 0))],
        out_specs=[pl.BlockSpec((L, D), rev), pl.BlockSpec((L, XBC), rev), pl.BlockSpec((L, D), rev),
                   pl.BlockSpec((8, D), lambda i: (0, 0)), pl.BlockSpec((8, XBC), lambda i: (0, 0))],
        out_shape=[jax.ShapeDtypeStruct((s, D), BF16), jax.ShapeDtypeStruct((s, XBC), BF16),
                   jax.ShapeDtypeStruct((s, D), BF16), jax.ShapeDtypeStruct((8, D), F32),
                   jax.ShapeDtypeStruct((8, XBC), F32)],
        scratch_shapes=[pltpu.VMEM((8, 128, 128), F32), pltpu.VMEM((L + HALO, XBC), F32),
                        pltpu.VMEM((L + HALO, XBC), F32)],
        compiler_params=_cp("arbitrary"),
    )(z, xbc, xbc, dtr, states, dymix, cw, cb, lanes)


def _mem_attn_math(q, k, v):
    outs = []
    for qh, kh, vh in zip(_split(q, 256), _split(k, 256), _split(v, 256)):
        sc = _dot("nt", qh, kh) * (1.0 / 16.0)
        e = jnp.exp(sc - lax.stop_gradient(jnp.max(sc, axis=-1, keepdims=True)))
        p = e / jnp.sum(e, axis=-1, keepdims=True)
        outs.append(_dot("nn", p, vh))
    return jnp.concatenate(outs, axis=1)


def _mem_attn_fwd(q, k, v, *, name):
    s, m = q.shape[0], k.shape[0]
    tm = _tile(s, 1024, 8)

    def body(q_ref, k_ref, v_ref, o_ref):
        o_ref[...] = _mem_attn_math(q_ref[...].astype(F32), k_ref[...].astype(F32),
                                    v_ref[...].astype(F32)).astype(BF16)

    row = pl.BlockSpec((tm, D), lambda i: (i, 0))
    kv = pl.BlockSpec((m, D), lambda i: (0, 0))
    return pl.pallas_call(
        body, name=name, grid=(s // tm,), in_specs=[row, kv, kv], out_specs=row,
        out_shape=jax.ShapeDtypeStruct((s, D), BF16), compiler_params=_cp("parallel"),
    )(q, k, v)


def _mem_attn_bwd(q, k, v, do, *, name):
    s, m = q.shape[0], k.shape[0]
    tm = _tile(s, 1024, 8)

    def body(q_ref, k_ref, v_ref, do_ref, dq_ref, dk_ref, dv_ref):
        @pl.when(pl.program_id(0) == 0)
        def _():
            dk_ref[...] = jnp.zeros_like(dk_ref)
            dv_ref[...] = jnp.zeros_like(dv_ref)

        _, vjp = jax.vjp(_mem_attn_math, q_ref[...].astype(F32), k_ref[...].astype(F32),
                         v_ref[...].astype(F32))
        dq, dk, dv = vjp(do_ref[...])
        dq_ref[...] = dq.astype(BF16)
        dk_ref[...] += dk
        dv_ref[...] += dv

    row = pl.BlockSpec((tm, D), lambda i: (i, 0))
    kv = pl.BlockSpec((m, D), lambda i: (0, 0))
    return pl.pallas_call(
        body, name=name, grid=(s // tm,), in_specs=[row, kv, kv, row], out_specs=[row, kv, kv],
        out_shape=[jax.ShapeDtypeStruct((s, D), BF16), jax.ShapeDtypeStruct((m, D), F32),
                   jax.ShapeDtypeStruct((m, D), F32)],
        compiler_params=_cp("arbitrary"),
    )(q, k, v, do)


DFF = 2816
FFN_TC = 1408
FFN_TM = 512


FFN_CHUNKS = tuple((c, min(512, FFN_TC - c)) for c in range(0, FFN_TC, 512))


def _rows8(ref, r, cols):
    return ref[pl.ds(pl.multiple_of(r, HALO), HALO), cols]


def _shift_down(prev, cur, s):
    return jnp.where(_iota(cur.shape, 0) < s, pltpu.roll(prev, s, 0), pltpu.roll(cur, s, 0))


def _shift_up(cur, nxt, s):
    return jnp.where(_iota(cur.shape, 0) >= HALO - s, pltpu.roll(nxt, HALO - s, 0), pltpu.roll(cur, HALO - s, 0))


def _ffn_conv_strip(ext_ref, r, cols, cw, cb):
    prev, cur = _rows8(ext_ref, r, cols), _rows8(ext_ref, r + HALO, cols)
    return cb + cw[0:1, :] * _shift_down(prev, cur, 2) + cw[1:2, :] * _shift_down(prev, cur, 1) + cw[2:3, :] * cur


def _ffn_specs(s):
    tm, tc = FFN_TM, FFN_TC
    blk = pl.BlockSpec((tm, tc), lambda i, j: (i, j))
    halo = pl.BlockSpec((HALO, tc), lambda i, j: (jnp.maximum(i * (tm // HALO) - 1, 0), j))
    cw = pl.BlockSpec((3, tc), lambda i, j: (0, j))
    cb = pl.BlockSpec((1, tc), lambda i, j: (0, j))
    return tm, tc, blk, halo, cw, cb


def _glu_fwd(ug, uv, cwg, cwv, cbg, cbv, *, name):
    s = ug.shape[0]
    tm, tc, blk, halo, cw, cb = _ffn_specs(s)

    def body(g_ref, gh_ref, v_ref, vh_ref, cwg_ref, cwv_ref, cbg_ref, cbv_ref, f_ref, eg, ev):
        first = pl.program_id(0) == 0
        eg[0:HALO, :] = jnp.where(first, 0.0, gh_ref[...])
        eg[HALO:, :] = g_ref[...]
        ev[0:HALO, :] = jnp.where(first, 0.0, vh_ref[...])
        ev[HALO:, :] = v_ref[...]
        cwgv, cwvv, cbgv, cbvv = cwg_ref[...], cwv_ref[...], cbg_ref[...], cbv_ref[...]

        def step(t, carry):
            for c0, w in FFN_CHUNKS:
                cols = slice(c0, c0 + w)
                outs = []
                for h in range(2):
                    r = t * 16 + HALO * h
                    g = _ffn_conv_strip(eg, r, cols, cwgv[:, cols], cbgv[:, cols])
                    v = _ffn_conv_strip(ev, r, cols, cwvv[:, cols], cbvv[:, cols])
                    outs.append(_silu(g) * v)
                f_ref[pl.ds(pl.multiple_of(t * 16, 16), 16), cols] = jnp.concatenate(outs, axis=0).astype(BF16)
            return carry

        lax.fori_loop(0, tm // 16, step, 0)

    return pl.pallas_call(
        body, name=name, grid=(s // tm, DFF // tc),
        in_specs=[blk, halo, blk, halo, cw, cw, cb, cb], out_specs=blk,
        out_shape=jax.ShapeDtypeStruct((s, DFF), BF16),
        scratch_shapes=[pltpu.VMEM((tm + HALO, tc), F32)] * 2,
        compiler_params=_cp("parallel", "parallel"),
    )(ug, ug, uv, uv, cwg, cwv, cbg, cbv)


def _ffn_bwd(ug, uv, df, cwg, cwv, cbg, cbv, *, name):
    s = ug.shape[0]
    tm, tc = FFN_TM, FFN_TC
    nb = s // tm
    rows_ext = tm + HALO

    def body(g_ref, gp_ref, gn_ref, v_ref, vp_ref, vn_ref, df_ref, dfn_ref, cwg_ref, cwv_ref, cbg_ref, cbv_ref,
             dxg_ref, dxv_ref, dcg_ref, dcv_ref, eg, ev, edf, edg, edv, accg, accv):
        i = pl.program_id(1)
        first, last = i == 0, i == nb - 1

        @pl.when(first)
        def _():
            dcg_ref[...] = jnp.zeros_like(dcg_ref)
            dcv_ref[...] = jnp.zeros_like(dcv_ref)

        for e, prev, main, nxt in ((eg, gp_ref, g_ref, gn_ref), (ev, vp_ref, v_ref, vn_ref)):
            e[0:HALO, :] = jnp.where(first, 0.0, prev[...])
            e[HALO:HALO + tm, :] = main[...]
            e[HALO + tm:, :] = jnp.where(last, 0.0, nxt[...])
        edf[0:tm, :] = df_ref[...]
        edf[tm:, :] = jnp.where(last, 0.0, dfn_ref[...])
        accg[...] = jnp.zeros_like(accg)
        accv[...] = jnp.zeros_like(accv)
        cwgv, cwvv, cbgv, cbvv = cwg_ref[...], cwv_ref[...], cbg_ref[...], cbv_ref[...]

        def cotangents(t, carry):
            r = t * HALO
            for c0, w in FFN_CHUNKS:
                cols = slice(c0, c0 + w)
                g = _ffn_conv_strip(eg, r, cols, cwgv[:, cols], cbgv[:, cols])
                v = _ffn_conv_strip(ev, r, cols, cwvv[:, cols], cbvv[:, cols])
                dfs = _rows8(edf, r, cols)
                sg = _sigmoid(g)
                edv[pl.ds(pl.multiple_of(r, HALO), HALO), cols] = dfs * g * sg
                edg[pl.ds(pl.multiple_of(r, HALO), HALO), cols] = dfs * v * sg * (1.0 + g * (1.0 - sg))
            return carry

        lax.fori_loop(0, rows_ext // HALO, cotangents, 0)

        def conv_backward(t, carry):
            for c0, w in FFN_CHUNKS:
                cols = slice(c0, c0 + w)
                for edu, e, cw, dx_ref, acc in ((edg, eg, cwgv[:, cols], dxg_ref, accg),
                                                (edv, ev, cwvv[:, cols], dxv_ref, accv)):
                    dxs = []
                    for h in range(2):
                        r = t * 16 + HALO * h
                        cur, nxt = _rows8(edu, r, cols), _rows8(edu, r + HALO, cols)
                        up1, up2 = _shift_up(cur, nxt, 1), _shift_up(cur, nxt, 2)
                        x = _rows8(e, r + HALO, cols)
                        dxs.append(cw[2:3, :] * cur + cw[1:2, :] * up1 + cw[0:1, :] * up2)
                        acc[0, :, cols] += up2 * x
                        acc[1, :, cols] += up1 * x
                        acc[2, :, cols] += cur * x
                        acc[3, :, cols] += cur
                    dx_ref[pl.ds(pl.multiple_of(t * 16, 16), 16), cols] = jnp.concatenate(dxs, axis=0).astype(BF16)
            return carry

        lax.fori_loop(0, tm // 16, conv_backward, 0)
        for acc, dc_ref in ((accg, dcg_ref), (accv, dcv_ref)):
            dc_ref[0:4, :] += jnp.concatenate([jnp.sum(acc[k], axis=0, keepdims=True) for k in range(4)], axis=0)

    blk = pl.BlockSpec((tm, tc), lambda j, i: (i, j))
    nxt = pl.BlockSpec((HALO, tc), lambda j, i: (jnp.minimum((i + 1) * (tm // HALO), s // HALO - 1), j))
    prv = pl.BlockSpec((HALO, tc), lambda j, i: (jnp.maximum(i * (tm // HALO) - 1, 0), j))
    cw = pl.BlockSpec((3, tc), lambda j, i: (0, j))
    cb = pl.BlockSpec((1, tc), lambda j, i: (0, j))
    acc = pl.BlockSpec((8, tc), lambda j, i: (0, j))
    return pl.pallas_call(
        body, name=name, grid=(DFF // tc, nb),
        in_specs=[blk, prv, nxt, blk, prv, nxt, blk, nxt, cw, cw, cb, cb],
        out_specs=[blk, blk, acc, acc],
        out_shape=[jax.ShapeDtypeStruct((s, DFF), BF16)] * 2 + [jax.ShapeDtypeStruct((8, DFF), F32)] * 2,
        scratch_shapes=[pltpu.VMEM((tm + 2 * HALO, tc), F32)] * 2 + [pltpu.VMEM((rows_ext, tc), F32)] * 3
                       + [pltpu.VMEM((4, HALO, tc), F32)] * 2,
        compiler_params=_cp("parallel", "arbitrary"),
    )(ug, ug, ug, uv, uv, uv, df, df, cwg, cwv, cbg, cbv)


MESH = pl.DeviceIdType.MESH


def _all_gather(arrs, *, name):
    n = len(arrs)

    def body(*refs):
        x_refs, out_refs = refs[:n], refs[n:2 * n]
        send_sems, recv_sems, local_sems = refs[2 * n:]
        x, y, c = lax.axis_index("x"), lax.axis_index("y"), lax.axis_index("c")
        me, sibling = (x, y, c), (x, y, 1 - c)
        chips = [(1 - x, y), (x, 1 - y), (1 - x, 1 - y)]

        def blk(a, dev):
            return out_refs[a].at[4 * dev[0] + 2 * dev[1] + dev[2]]

        def copy(a, k, block, to, src=None):
            return pltpu.make_async_remote_copy(
                src_ref=blk(a, block) if src is None else src, dst_ref=blk(a, block),
                send_sem=send_sems.at[7 * a + k], recv_sem=recv_sems.at[7 * a + k],
                device_id=to, device_id_type=MESH)

        started = []
        mine = []
        for a in range(n):
            cp = pltpu.make_async_copy(x_refs[a], blk(a, me), local_sems.at[a])
            cp.start()
            mine.append(cp)
            first = [copy(a, 0, me, sibling, src=x_refs[a])]
            first += [copy(a, 1 + j, me, (*chip, c), src=x_refs[a]) for j, chip in enumerate(chips)]
            for cp in first:
                cp.start()
            started += first
        for a in range(n):
            for j, chip in enumerate(chips):
                copy(a, 1 + j, (*chip, c), me).wait_recv()
                fwd = copy(a, 4 + j, (*chip, c), sibling)
                fwd.start()
                started.append(fwd)
        for a in range(n):
            copy(a, 0, sibling, me).wait_recv()
            for j, chip in enumerate(chips):
                copy(a, 4 + j, (*chip, 1 - c), me).wait_recv()
        for cp in started:
            cp.wait_send()
        for cp in mine:
            cp.wait()

    any_spec = pl.BlockSpec(memory_space=pl.ANY)
    return pl.pallas_call(
        body, name=name,
        in_specs=[any_spec] * n, out_specs=[any_spec] * n,
        out_shape=[jax.ShapeDtypeStruct((NDEV,) + a.shape, a.dtype) for a in arrs],
        scratch_shapes=[pltpu.SemaphoreType.DMA((7 * n,)), pltpu.SemaphoreType.DMA((7 * n,)),
                        pltpu.SemaphoreType.DMA((n,))],
    )(*arrs)


class _Direct:
    SEMS = (pltpu.SemaphoreType.DMA((7,)), pltpu.SemaphoreType.DMA((7,)), pltpu.SemaphoreType.DMA((1,)))

    def __init__(self, src_ref, recv_ref, sems, gather):
        x, y, c = lax.axis_index("x"), lax.axis_index("y"), lax.axis_index("c")
        me = 4 * x + 2 * y + c
        send_sems, recv_sems, local_sem = sems
        src = (lambda pid: src_ref) if gather else (lambda pid: src_ref.at[pid])
        self.mine = pltpu.make_async_copy(src(me), recv_ref.at[me], local_sem.at[0])
        self.copies = []
        for k in range(1, NDEV):
            px = 1 - x if k & 4 else x
            py = 1 - y if k & 2 else y
            pc = 1 - c if k & 1 else c
            self.copies.append(pltpu.make_async_remote_copy(
                src_ref=src(4 * px + 2 * py + pc), dst_ref=recv_ref.at[me],
                send_sem=send_sems.at[k - 1], recv_sem=recv_sems.at[k - 1],
                device_id=(px, py, pc), device_id_type=MESH))

    def start(self):
        self.mine.start()
        for cp in self.copies:
            cp.start()

    def wait(self):
        for cp in self.copies:
            cp.wait_recv()
        for cp in self.copies:
            cp.wait_send()
        self.mine.wait()


def _recv_shape(src, gather):
    return jax.ShapeDtypeStruct(((NDEV,) + src.shape) if gather else src.shape, src.dtype)


class _Rides:
    def __init__(self, rides, gathers):
        self.n = len(rides)
        self.gathers = list(gathers)
        any_spec = pl.BlockSpec(memory_space=pl.ANY)
        self.in_specs = [any_spec] * self.n
        self.out_specs = [any_spec] * self.n
        self.out_shape = [_recv_shape(a, g) for a, g in zip(rides, gathers)]
        self.scratch = list(_Direct.SEMS) * self.n

    def split(self, refs, n_in, n_out, n_scratch):
        n = self.n
        ins, refs = refs[:n_in], refs[n_in:]
        rides, refs = refs[:n], refs[n:]
        outs, refs = refs[:n_out], refs[n_out:]
        gots, refs = refs[:n], refs[n:]
        scratch, sems = refs[:n_scratch], refs[n_scratch:]
        return ins, outs, scratch, (rides, gots, sems)

    def run(self, handles, first, last):
        rides, gots, sems = handles

        def all_of():
            return [_Direct(rides[a], gots[a], sems[3 * a:3 * a + 3], self.gathers[a]) for a in range(self.n)]

        @pl.when(first)
        def _():
            for e in all_of():
                e.start()

        @pl.when(last)
        def _():
            for e in all_of():
                e.wait()


def _exchange(arrs, gathers, *, name):
    rd = _Rides(arrs, gathers)

    def body(*refs):
        _, _, _, handles = rd.split(refs, 0, 0, 0)
        rd.run(handles, True, True)

    return pl.pallas_call(
        body, name=name, in_specs=rd.in_specs, out_specs=rd.out_specs, out_shape=rd.out_shape,
        scratch_shapes=rd.scratch,
    )(*arrs)


def _adamw(parts, w, m, v, *, name):
    r, cols = w.shape
    tm = _tile(r, 256, PACK_ALIGN)
    c1 = 1.0 - ADAM_B1 ** ADAM_STEP
    c2 = 1.0 - ADAM_B2 ** ADAM_STEP

    def body(p_ref, w_ref, m_ref, v_ref, g_ref, d_ref, nm_ref, nv_ref):
        g = p_ref[0].astype(F32)
        for i in range(1, NDEV):
            g = g + p_ref[i].astype(F32)
        nm = ADAM_B1 * m_ref[...] + (1.0 - ADAM_B1) * g
        nv = ADAM_B2 * v_ref[...] + (1.0 - ADAM_B2) * (g * g)
        d_ref[...] = -ADAM_LR * ((nm / c1) / (jnp.sqrt(nv / c2) + ADAM_EPS) + ADAM_WD * w_ref[...])
        g_ref[...] = g
        nm_ref[...] = nm
        nv_ref[...] = nv

    row = pl.BlockSpec((tm, cols), lambda i: (i, 0))
    return pl.pallas_call(
        body, name=name, grid=(r // tm,),
        in_specs=[pl.BlockSpec((NDEV, tm, cols), lambda i: (0, i, 0)), row, row, row],
        out_specs=[row] * 4, out_shape=[jax.ShapeDtypeStruct((r, cols), F32)] * 4,
        compiler_params=_cp("parallel"),
    )(parts, w, m, v)


PACK_ALIGN = 16


def _part_rows(shape):
    n = -(-math.prod(shape) // D)
    return n + (-n) % PACK_ALIGN


def _rows(a):
    flat = a.reshape(-1)
    pad = _part_rows(a.shape) * D - flat.shape[0]
    if pad:
        flat = jnp.concatenate([flat, jnp.zeros((pad,), flat.dtype)])
    return flat.reshape(-1, D)


def _pack(parts, total_rows):
    if all(math.prod(p.shape) % (PACK_ALIGN * D) for p in parts):
        return _pack_small(parts, total_rows)
    rows = [_rows(p) for p in parts]
    used = sum(r.shape[0] for r in rows)
    if total_rows > used:
        rows.append(jnp.zeros((total_rows - used, D), rows[0].dtype))
    return jnp.concatenate(rows, axis=0)


def _pack_small(parts, total_rows):
    flat, used = [], 0
    for p in parts:
        n, nr = math.prod(p.shape), _part_rows(p.shape)
        flat += [p.reshape(-1), jnp.zeros((nr * D - n,), p.dtype)]
        used += nr
    flat.append(jnp.zeros(((total_rows - used) * D,), parts[0].dtype))
    return jnp.concatenate(flat).reshape(total_rows, D)


def _unpack(buf, shapes, part_rows=_part_rows):
    out, r0 = [], 0
    for shp in shapes:
        n = math.prod(shp)
        out.append(buf[r0:r0 + part_rows(shp)].reshape(-1)[:n].reshape(shp))
        r0 += part_rows(shp)
    return out


def _tight_rows(shape):
    return -(-math.prod(shape) // D)


def _pack_tight(parts, total_rows):
    flat, used = [], 0
    for p in parts:
        n, nr = math.prod(p.shape), _tight_rows(p.shape)
        flat += [p.reshape(-1), jnp.zeros((nr * D - n,), p.dtype)]
        used += nr
    flat.append(jnp.zeros(((total_rows - used) * D,), parts[0].dtype))
    return jnp.concatenate(flat).reshape(total_rows, D)


SHARD = {"w_in": (D, 706), "w_out": (256, D), "w_mq": (128, D), "w_mk": (128, D), "w_mv": (128, D),
         "w_mo": (128, D), "w_up": (D, 704), "w_down": (352, D), "conv_ssd_w": (4, 192), "conv_ffn_w": (3, 704)}
GATHER_MID = ["w_out", "w_mq", "w_mk", "w_mv", "w_mo"]
GATHER_FFN = ["w_down"]
CONV_TAPS = ["conv_ssd_w", "conv_ffn_w"]
GRADS_PACKED = ["w_out", "w_mq", "w_mk", "w_mv", "w_mo", "w_down", "conv_ffn_w", "conv_ssd_w"]


def _layout(names):
    row0, r = {}, 0
    for n in names:
        row0[n] = r
        r += _part_rows(SHARD[n])
    return row0, r + (-r) % 128


SMALL = [("norm_mix_w", (1, D)), ("conv_ssd_b", (1, 1536)), ("dt_bias", (1, 16)), ("a_log", (1, 16)),
         ("d_skip", (1, 16)), ("ssd_norm_w", (1, D)), ("sb_norm_w", (1, D)), ("norm_mem_w", (1, D)),
         ("norm_memkv_w", (1, D)), ("norm_ffn_w", (1, D)), ("conv_ffn_b", (1, 5632)), ("norm_final_w", (D,))]
LOSS_ROW = sum(_tight_rows(_shp) for _, _shp in SMALL)
SMALL_ROWS = LOSS_ROW + 1 + (-(LOSS_ROW + 1)) % 8
ORDER = ["norm_mix_w", "w_in", "conv_ssd_w", "conv_ssd_b", "dt_bias", "a_log", "d_skip", "ssd_norm_w",
         "sb_norm_w", "w_out", "norm_mem_w", "norm_memkv_w", "w_mq", "w_mk", "w_mv", "w_mo", "norm_ffn_w",
         "w_up", "conv_ffn_w", "conv_ffn_b", "w_down", "norm_final_w"]


def _pad_rows(a, nr):
    n = a.shape[1]
    return jnp.concatenate([a, jnp.zeros((NDEV, nr * D - n), a.dtype)], axis=1).reshape(NDEV, nr, D)


def _group_sum(lanes):
    return lanes.reshape(16, 64).sum(axis=1).reshape(1, 16)


def kernel(x, mem, norm_mix_w, w_in, conv_ssd_w, conv_ssd_b, dt_bias, a_log, d_skip, ssd_norm_w, sb_norm_w, w_out, norm_mem_w, norm_memkv_w, w_mq, w_mk, w_mv, w_mo, norm_ffn_w, w_up, conv_ffn_w, conv_ffn_b, w_down, norm_final_w, loss_target, m_norm_mix_w, m_w_in, m_conv_ssd_w, m_conv_ssd_b, m_dt_bias, m_a_log, m_d_skip, m_ssd_norm_w, m_sb_norm_w, m_w_out, m_norm_mem_w, m_norm_memkv_w, m_w_mq, m_w_mk, m_w_mv, m_w_mo, m_norm_ffn_w, m_w_up, m_conv_ffn_w, m_conv_ffn_b, m_w_down, m_norm_final_w, v_norm_mix_w, v_w_in, v_conv_ssd_w, v_conv_ssd_b, v_dt_bias, v_a_log, v_d_skip, v_ssd_norm_w, v_sb_norm_w, v_w_out, v_norm_mem_w, v_norm_memkv_w, v_w_mq, v_w_mk, v_w_mv, v_w_mo, v_norm_ffn_w, v_w_up, v_conv_ffn_w, v_conv_ffn_b, v_w_down, v_norm_final_w):
    P = dict(norm_mix_w=norm_mix_w, w_in=w_in, conv_ssd_w=conv_ssd_w, conv_ssd_b=conv_ssd_b, dt_bias=dt_bias, a_log=a_log, d_skip=d_skip, ssd_norm_w=ssd_norm_w, sb_norm_w=sb_norm_w, w_out=w_out, norm_mem_w=norm_mem_w, norm_memkv_w=norm_memkv_w, w_mq=w_mq, w_mk=w_mk, w_mv=w_mv, w_mo=w_mo, norm_ffn_w=norm_ffn_w, w_up=w_up, conv_ffn_w=conv_ffn_w, conv_ffn_b=conv_ffn_b, w_down=w_down, norm_final_w=norm_final_w)
    M = dict(norm_mix_w=m_norm_mix_w, w_in=m_w_in, conv_ssd_w=m_conv_ssd_w, conv_ssd_b=m_conv_ssd_b, dt_bias=m_dt_bias, a_log=m_a_log, d_skip=m_d_skip, ssd_norm_w=m_ssd_norm_w, sb_norm_w=m_sb_norm_w, w_out=m_w_out, norm_mem_w=m_norm_mem_w, norm_memkv_w=m_norm_memkv_w, w_mq=m_w_mq, w_mk=m_w_mk, w_mv=m_w_mv, w_mo=m_w_mo, norm_ffn_w=m_norm_ffn_w, w_up=m_w_up, conv_ffn_w=m_conv_ffn_w, conv_ffn_b=m_conv_ffn_b, w_down=m_w_down, norm_final_w=m_norm_final_w)
    V = dict(norm_mix_w=v_norm_mix_w, w_in=v_w_in, conv_ssd_w=v_conv_ssd_w, conv_ssd_b=v_conv_ssd_b, dt_bias=v_dt_bias, a_log=v_a_log, d_skip=v_d_skip, ssd_norm_w=v_ssd_norm_w, sb_norm_w=v_sb_norm_w, w_out=v_w_out, norm_mem_w=v_norm_mem_w, norm_memkv_w=v_norm_memkv_w, w_mq=v_w_mq, w_mk=v_w_mk, w_mv=v_w_mv, w_mo=v_w_mo, norm_ffn_w=v_norm_ffn_w, w_up=v_w_up, conv_ffn_w=v_conv_ffn_w, conv_ffn_b=v_conv_ffn_b, w_down=v_w_down, norm_final_w=v_norm_final_w)
    small_shapes = [shp for _, shp in SMALL]

    def packed(src, names, dtype=F32):
        return _pack([src[n][0] for n in names], _layout(names)[1]).astype(dtype)

    def columns(g):
        return g.transpose(1, 0, 2).reshape(g.shape[1], NDEV * g.shape[2])

    g_in, g_taps = _all_gather([w_in[0].astype(BF16), packed(P, CONV_TAPS)], name="gather_w_in")
    W_in = columns(g_in)
    cw_ssd = g_taps[:, 0].reshape(NDEV, -1)[:, :768].reshape(NDEV, 4, 192).transpose(1, 0, 2).reshape(4, XBC)
    cw_ffn = (g_taps[:, PACK_ALIGN:PACK_ALIGN + 3].reshape(NDEV, -1)[:, :2112].reshape(NDEV, 3, 704)
              .transpose(1, 0, 2).reshape(3, 2 * DFF))
    W_z, W_xbc, W_dt, W_qkv = W_in[:, :D], W_in[:, D:D + XBC], W_in[:, D + XBC:D + XBC + 16], W_in[:, D + XBC + 16:]
    W_dtr = jnp.repeat(W_dt, 64, axis=1)
    cwg, cwv = cw_ffn[:, :DFF], cw_ffn[:, DFF:]
    cbg, cbv = conv_ffn_b[:, :DFF], conv_ffn_b[:, DFF:]
    rep = lambda p: jnp.repeat(p, 64, axis=1)
    lanes = jnp.concatenate([rep(dt_bias), rep(a_log), rep(d_skip), ssd_norm_w, jnp.zeros((4, D), F32)], axis=0)

    xs, tgt, mm = x[0], loss_target[0], mem[0]

    h1 = _norm_fwd(xs, norm_mix_w, name="norm_mix")
    z = _mm(h1, W_z, name="proj_z")
    xbc = _mm(h1, W_xbc, name="proj_xbc")
    dtr = _mm(h1, W_dtr, name="proj_dt")
    qkv = _mm(h1, W_qkv, name="proj_qkv", out_dtype=BF16)
    w_up16 = w_up[0].astype(BF16)
    y_ssd, states, g_ffn, g_up0 = _ssd_fwd(z, xbc, dtr, cw_ssd, conv_ssd_b, lanes,
                                           [packed(P, GATHER_FFN, BF16), w_up16[:D // 2]], name="ssd_fwd")
    o_sb, g_mid, g_up1 = _sb_fwd(qkv, [packed(P, GATHER_MID, BF16), w_up16[D // 2:]], name="sb_fwd")
    g_up = jnp.concatenate([g_up0, g_up1], axis=1)
    r_mid = _layout(GATHER_MID)[0]
    W_out = g_mid[:, r_mid["w_out"]:r_mid["w_out"] + 256].reshape(2 * D, D)
    W_mq, W_mk, W_mv, W_mo = [g_mid[:, r_mid[n]:r_mid[n] + 128].reshape(D, D)
                              for n in ("w_mq", "w_mk", "w_mv", "w_mo")]
    W_up = columns(g_up)
    W_down = g_ffn[:, 0:352].reshape(DFF, D)
    W_upg, W_upv = W_up[:, :DFF], W_up[:, DFF:]
    y_sb = _head_norm_fwd(o_sb, sb_norm_w, name="sb_norm")
    ymix = jnp.concatenate([y_ssd, y_sb], axis=1)
    x1 = _mm(ymix, W_out, add=xs, name="proj_out")
    h2 = _norm_fwd(x1, norm_mem_w, name="norm_mem")
    mn = _norm_fwd(mm, norm_memkv_w, name="norm_memkv")
    qm = _mm(h2, W_mq, name="mem_q", out_dtype=BF16)
    km = _mm(mn, W_mk, name="mem_k", out_dtype=BF16)
    vm = _mm(mn, W_mv, name="mem_v", out_dtype=BF16)
    om = _mem_attn_fwd(qm, km, vm, name="mem_attn")
    x2 = _mm(om, W_mo, add=x1, name="mem_o")
    h3 = _norm_fwd(x2, norm_ffn_w, name="norm_ffn")
    ug = _mm(h3, W_upg, name="ffn_up_g")
    uv = _mm(h3, W_upv, name="ffn_up_v")
    f = _glu_fwd(ug, uv, cwg, cwv, cbg, cbv, name="ffn_glu")
    x3 = _mm(f, W_down, add=x2, name="ffn_down")
    dx3, g_nfinal, loss_part = _final(x3, norm_final_w.reshape(1, D), tgt, name="final_loss")

    G = {}
    G["w_down"] = _mm(f, dx3, trans_a=True, name="g_w_down")
    df = _mm(dx3, W_down, trans_b=True, name="d_f")
    dupg, dupv, dcg, dcv = _ffn_bwd(ug, uv, df, cwg, cwv, cbg, cbv, name="ffn_glu_bwd")
    G["w_up"] = jnp.concatenate([_mm(h3, dupg, trans_a=True, name="g_w_up_g"),
                                 _mm(h3, dupv, trans_a=True, name="g_w_up_v")], axis=1)
    G["conv_ffn_w"] = jnp.concatenate([dcg[0:3], dcv[0:3]], axis=1)
    G["conv_ffn_b"] = jnp.concatenate([dcg[3:4], dcv[3:4]], axis=1)
    dh3 = _mm(dupg, W_upg, trans_b=True, name="d_h3_g")
    dh3 = _mm(dupv, W_upv, trans_b=True, add=dh3, name="d_h3_v")
    dx2, G["norm_ffn_w"] = _norm_bwd(x2, norm_ffn_w, dh3, dx3, name="norm_ffn_bwd")
    G["w_mo"] = _mm(om, dx2, trans_a=True, name="g_w_mo")
    dom = _mm(dx2, W_mo, trans_b=True, name="d_om")
    dqm, dkm, dvm = _mem_attn_bwd(qm, km, vm, dom, name="mem_attn_bwd")
    G["w_mq"] = _mm(h2, dqm, trans_a=True, name="g_w_mq")
    G["w_mk"] = _mm(mn, dkm, trans_a=True, name="g_w_mk")
    G["w_mv"] = _mm(mn, dvm, trans_a=True, name="g_w_mv")
    dh2 = _mm(dqm, W_mq, trans_b=True, name="d_h2")
    dmn = _mm(dkm, W_mk, trans_b=True, name="d_mn_k")
    dmn = _mm(dvm, W_mv, trans_b=True, add=dmn, name="d_mn_v")
    _, G["norm_memkv_w"] = _norm_bwd(mm, norm_memkv_w, dmn, None, name="norm_memkv_bwd")
    dx1, G["norm_mem_w"] = _norm_bwd(x1, norm_mem_w, dh2, dx2, name="norm_mem_bwd")
    G["w_out"] = _mm(ymix, dx1, trans_a=True, name="g_w_out")
    dymix = _mm(dx1, W_out, trans_b=True, name="d_ymix")
    do_sb, G["sb_norm_w"] = _head_norm_bwd(o_sb, sb_norm_w, dymix, name="sb_norm_bwd")

    dz, dxbc, ddtr, dlanes, dconv = _ssd_bwd(z, xbc, dtr, states, dymix, cw_ssd, conv_ssd_b, lanes, name="ssd_bwd")
    G["dt_bias"], G["a_log"], G["d_skip"] = [_group_sum(dlanes[i:i + 1]) for i in range(3)]
    G["ssd_norm_w"] = dlanes[3:4]
    G["conv_ssd_w"], G["conv_ssd_b"] = dconv[0:4], dconv[4:5]

    def col_slabs(g, cols):
        return g.reshape(g.shape[0], NDEV, cols).transpose(1, 0, 2).astype(BF16)

    def packed_slabs(names):
        parts = []
        for n in names:
            shp = SHARD[n]
            if shp[-1] == D:
                t = G[n].reshape((NDEV,) + shp)
            else:
                t = _pad_rows(G[n].reshape(shp[0], NDEV, shp[1]).transpose(1, 0, 2).reshape(NDEV, -1),
                              _part_rows(shp))
            parts.append(jnp.pad(t, ((0, 0), (0, _part_rows(shp) - t.shape[1]), (0, 0))))
        used = sum(t.shape[1] for t in parts)
        parts.append(jnp.zeros((NDEV, _layout(names)[1] - used, D), F32))
        return jnp.concatenate(parts, axis=1).astype(BF16)

    dq, dk, dv, recv_packed, recv_up = _sb_bwd(qkv, o_sb, do_sb, [packed_slabs(GRADS_PACKED), col_slabs(G["w_up"], 704)],
                                               name="sb_bwd")
    dproj = jnp.concatenate([dz, dxbc, ddtr, dq, dk, dv], axis=1)
    g_proj = _mm(h1, dproj, trans_a=True, name="g_w_in")
    c_dt = D + XBC
    G["w_in"] = jnp.concatenate([g_proj[:, :c_dt], g_proj[:, c_dt:c_dt + D].reshape(D, 16, 64).sum(axis=2),
                                 g_proj[:, c_dt + D:]], axis=1)
    W_proj = jnp.concatenate([W_z, W_xbc, W_dtr, W_qkv], axis=1)
    dh1, recv_in = _mm(dproj, W_proj, trans_b=True, rides=[col_slabs(G["w_in"], 706)], name="d_h1")
    dx, G["norm_mix_w"] = _norm_bwd(xs, norm_mix_w, dh1, dx1, name="norm_mix_bwd")
    G["norm_final_w"] = g_nfinal.reshape(D)

    small_g = _pack_tight([G[n] for n, _ in SMALL] + [loss_part], SMALL_ROWS)
    (parts_small,) = _exchange([small_g], [True], name="exchange_grads")
    outs_packed = _adamw(recv_packed, packed(P, GRADS_PACKED), packed(M, GRADS_PACKED), packed(V, GRADS_PACKED),
                         name="adamw_packed")
    outs_up = _adamw(recv_up, w_up[0], m_w_up[0], v_w_up[0], name="adamw_w_up")
    outs_in = _adamw(recv_in, w_in[0], m_w_in[0], v_w_in[0], name="adamw_w_in")
    outs_small = _adamw(parts_small, _pack_tight([P[n] for n, _ in SMALL], SMALL_ROWS),
                        _pack_tight([M[n] for n, _ in SMALL], SMALL_ROWS),
                        _pack_tight([V[n] for n, _ in SMALL], SMALL_ROWS), name="adamw_replicated")

    res = {}
    for i, kind in enumerate(("grad", "delta", "new_m", "new_v")):
        for n, val in zip(GRADS_PACKED, _unpack(outs_packed[i], [SHARD[n] for n in GRADS_PACKED])):
            res[kind, n] = val.reshape((1,) + SHARD[n])
        res[kind, "w_up"] = outs_up[i].reshape((1,) + SHARD["w_up"])
        res[kind, "w_in"] = outs_in[i].reshape((1,) + SHARD["w_in"])
        for (n, shp), val in zip(SMALL, _unpack(outs_small[i], small_shapes, _tight_rows)):
            res[kind, n] = val
    loss = outs_small[0][LOSS_ROW, 0]
    out = [loss, dx.reshape(1, -1, D)]
    for kind in ("grad", "delta", "new_m", "new_v"):
        out += [res[kind, n] for n in ORDER]
    return tuple(out)
```

```python
import functools
import math

import jax
import jax.numpy as jnp
from jax import lax
from jax.experimental import pallas as pl
from jax.experimental.pallas import tpu as pltpu

F32 = jnp.float32
BF16 = jnp.bfloat16

D = 1024
NDEV = 8
EPS = 1e-6
SSD_CHUNK = 128
HALO = 8
VMEM_LIMIT = 56 * 2**20

ADAM_LR, ADAM_B1, ADAM_B2, ADAM_EPS, ADAM_WD, ADAM_STEP = 0.001, 0.9, 0.999, 1e-08, 0.01, 10


def _cp(*sem):
    return pltpu.CompilerParams(dimension_semantics=sem, vmem_limit_bytes=VMEM_LIMIT)


def _tile(n, cap, mult):
    if n <= cap:
        return n
    for d in range(cap - cap % mult, 0, -mult):
        if n % d == 0:
            return d
    raise ValueError(f"no tile for {n}")


def _sigmoid(x):
    return 1.0 / (1.0 + jnp.exp(-x))


def _silu(x):
    return x * _sigmoid(x)


def _softplus(x):
    return jnp.maximum(x, 0.0) + jnp.log(1.0 + jnp.exp(-jnp.abs(x)))


def _terms(x, n):
    out = []
    r = x.astype(F32)
    for i in range(n):
        h = r.astype(BF16)
        out.append(h)
        if i + 1 < n:
            r = r - h.astype(F32)
    return out


_DIMS = {"nn": ((1,), (0,)), "nt": ((1,), (1,)), "tn": ((0,), (0,))}


def _dot_raw(form, a, b, ta, tb):
    acc = None
    for ai in _terms(a, ta):
        for bi in _terms(b, tb):
            d = lax.dot_general(ai, bi, (_DIMS[form], ((), ())), preferred_element_type=F32)
            acc = d if acc is None else acc + d
    return acc


@functools.lru_cache(maxsize=None)
def _dot_fn(form, ta, tb):
    @jax.custom_vjp
    def f(a, b):
        return _dot_raw(form, a, b, ta, tb)

    def fwd(a, b):
        return f(a, b), (a, b)

    def bwd(res, ct):
        a, b = res
        if form == "nn":
            return _dot_fn("nt", ta, tb)(ct, b), _dot_fn("tn", ta, tb)(a, ct)
        if form == "nt":
            return _dot_fn("nn", ta, tb)(ct, b), _dot_fn("tn", tb, ta)(ct, a)
        return _dot_fn("nt", tb, ta)(b, ct), _dot_fn("nn", ta, tb)(a, ct)

    f.defvjp(fwd, bwd)
    return f


def _dot(form, a, b, ta=1, tb=1):
    return _dot_fn(form, ta, tb)(a, b)


@functools.lru_cache(maxsize=None)
def _take_fn(axis, idx):
    @jax.custom_vjp
    def f(x):
        return x[:, idx:idx + 1] if axis == 1 else x[idx:idx + 1, :]

    def fwd(x):
        return f(x), x.shape

    def bwd(shape, ct):
        io = lax.broadcasted_iota(jnp.int32, shape, axis)
        return (jnp.where(io == idx, jnp.broadcast_to(ct, shape), 0.0),)

    f.defvjp(fwd, bwd)
    return f


@functools.lru_cache(maxsize=None)
def _split_fn(width, n):
    @jax.custom_vjp
    def f(x):
        return tuple(x[:, i * width:(i + 1) * width] for i in range(n))

    def fwd(x):
        return f(x), None

    def bwd(_, cts):
        return (jnp.concatenate(list(cts), axis=1),)

    f.defvjp(fwd, bwd)
    return f


def _split(x, width):
    return _split_fn(width, x.shape[1] // width)(x)


def _iota(shape, axis):
    return lax.broadcasted_iota(jnp.int32, shape, axis)


MM_VMEM_BUDGET = 44 * 2**20


def _mm_tiles(m, n, kt, trans_a, a_bytes, b_bytes, out_bytes, add_bytes):
    tn = _tile(n, 1536, 128)
    for tm_cap in (1408, 1024, 512, 256, 128):
        tm = _tile(m, tm_cap, 128 if trans_a else 8)
        for tk_cap in (kt, 4096, 2048, 1024, 512):
            tk = _tile(kt, tk_cap, 128)
            blocks = tm * tk * a_bytes + tk * tn * b_bytes + tm * tn * (out_bytes + add_bytes)
            if 2 * blocks + (tm * tn * 4 if tk < kt else 0) <= MM_VMEM_BUDGET:
                return tm, tn, tk
    raise ValueError(f"no matmul tiling for {(m, n, kt)}")


def _mm(a, b, *, name, add=None, trans_a=False, trans_b=False, out_dtype=F32, rides=()):
    assert not (trans_a and trans_b)
    if trans_a:
        kt, m = a.shape
    else:
        m, kt = a.shape
    n, kt2 = b.shape if trans_b else b.shape[::-1]
    assert kt == kt2, (a.shape, b.shape)
    tm, tn, tk = _mm_tiles(m, n, kt, trans_a, a.dtype.itemsize, b.dtype.itemsize,
                           jnp.dtype(out_dtype).itemsize, 0 if add is None else add.dtype.itemsize)
    nk = kt // tk
    grid = (m // tm, n // tn, nk)
    rd = _Rides(rides, [False] * len(rides))
    n_in = 2 if add is None else 3

    def body(*all_refs):
        ins, (o_ref,), scratch, handles = rd.split(all_refs, n_in, 1, 1 if nk > 1 else 0)
        refs = (*ins, o_ref, *scratch)
        if rides:
            ids = [pl.program_id(ax) for ax in range(3)]
            rd.run(handles, (ids[0] == 0) & (ids[1] == 0) & (ids[2] == 0),
                   (ids[0] == grid[0] - 1) & (ids[1] == grid[1] - 1) & (ids[2] == grid[2] - 1))
        if add is None:
            a_ref, b_ref, o_ref = refs[:3]
        else:
            a_ref, b_ref, add_ref, o_ref = refs[:4]
        k = pl.program_id(2)
        av = a_ref[...].astype(BF16)
        bv = b_ref[...].astype(BF16)
        dims = _DIMS["tn" if trans_a else "nt" if trans_b else "nn"]
        d = lax.dot_general(av, bv, (dims, ((), ())), preferred_element_type=F32)

        def finish(r):
            if add is not None:
                r = r + add_ref[...]
            o_ref[...] = r.astype(out_dtype)

        if nk == 1:
            finish(d)
        else:
            acc = refs[-1]

            @pl.when(k == 0)
            def _():
                acc[...] = d

            @pl.when((k > 0) & (k < nk - 1))
            def _():
                acc[...] += d

            @pl.when(k == nk - 1)
            def _():
                finish(acc[...] + d)

    a_spec = (pl.BlockSpec((tk, tm), lambda i, j, k: (k, i)) if trans_a
              else pl.BlockSpec((tm, tk), lambda i, j, k: (i, k)))
    b_spec = (pl.BlockSpec((tn, tk), lambda i, j, k: (j, k)) if trans_b
              else pl.BlockSpec((tk, tn), lambda i, j, k: (k, j)))
    in_specs = [a_spec, b_spec]
    args = [a, b]
    if add is not None:
        in_specs.append(pl.BlockSpec((tm, tn), lambda i, j, k: (i, j)))
        args.append(add)
    out = pl.pallas_call(
        body, name=name, grid=grid,
        in_specs=in_specs + rd.in_specs,
        out_specs=[pl.BlockSpec((tm, tn), lambda i, j, k: (i, j))] + rd.out_specs,
        out_shape=[jax.ShapeDtypeStruct((m, n), out_dtype)] + rd.out_shape,
        scratch_shapes=([pltpu.VMEM((tm, tn), F32)] if nk > 1 else []) + rd.scratch,
        compiler_params=_cp(*(("arbitrary",) * 3 if rides else ("parallel", "parallel", "arbitrary"))),
    )(*args, *rides)
    return out if rides else out[0]


def _rstd(x):
    return lax.rsqrt(jnp.mean(x * x, axis=-1, keepdims=True) + EPS)


def _norm_fwd(x, w, *, name):
    s = x.shape[0]
    tm = _tile(s, 1024, 8)

    def body(x_ref, w_ref, o_ref):
        xv = x_ref[...]
        o_ref[...] = (xv * _rstd(xv) * w_ref[...]).astype(BF16)

    return pl.pallas_call(
        body, name=name, grid=(s // tm,),
        in_specs=[pl.BlockSpec((tm, D), lambda i: (i, 0)), pl.BlockSpec((1, D), lambda i: (0, 0))],
        out_specs=pl.BlockSpec((tm, D), lambda i: (i, 0)),
        out_shape=jax.ShapeDtypeStruct((s, D), BF16), compiler_params=_cp("parallel"),
    )(x, w)


def _norm_bwd_math(xv, wv, dy):
    r = _rstd(xv)
    xh = xv * r
    dxh = dy * wv
    dx = r * (dxh - xh * jnp.mean(dxh * xh, axis=-1, keepdims=True))
    dw = jnp.sum(dy * xh, axis=0, keepdims=True)
    return dx, dw


def _norm_bwd(x, w, dy, add, *, name):
    s = x.shape[0]
    tm = _tile(s, 512, 8)

    def body(*refs):
        if add is None:
            x_ref, w_ref, dy_ref, dx_ref, dw_ref = refs
        else:
            x_ref, w_ref, dy_ref, add_ref, dx_ref, dw_ref = refs

        @pl.when(pl.program_id(0) == 0)
        def _():
            dw_ref[...] = jnp.zeros_like(dw_ref)

        dx, dw = _norm_bwd_math(x_ref[...], w_ref[...], dy_ref[...])
        if add is not None:
            dx = dx + add_ref[...]
        dx_ref[...] = dx
        dw_ref[...] += dw

    row = pl.BlockSpec((tm, D), lambda i: (i, 0))
    vec = pl.BlockSpec((1, D), lambda i: (0, 0))
    in_specs = [row, vec, row] + ([row] if add is not None else [])
    args = [x, w, dy] + ([add] if add is not None else [])
    return pl.pallas_call(
        body, name=name, grid=(s // tm,), in_specs=in_specs, out_specs=[row, vec],
        out_shape=[jax.ShapeDtypeStruct((s, D), F32), jax.ShapeDtypeStruct((1, D), F32)],
        compiler_params=_cp("arbitrary"),
    )(*args)


def _final(x3, w, target, *, name):
    s = x3.shape[0]
    tm = _tile(s, 512, 8)

    def body(x_ref, w_ref, t_ref, dx_ref, dw_ref, loss_ref):
        @pl.when(pl.program_id(0) == 0)
        def _():
            dw_ref[...] = jnp.zeros_like(dw_ref)
            loss_ref[...] = jnp.zeros_like(loss_ref)

        xv = x_ref[...]
        wv = w_ref[...]
        y = xv * _rstd(xv) * wv
        err = y - t_ref[...]
        loss_ref[...] += 0.5 * jnp.sum(jnp.mean(err * err, axis=-1, keepdims=True))
        dx, dw = _norm_bwd_math(xv, wv, err * (1.0 / D))
        dx_ref[...] = dx
        dw_ref[...] += dw

    row = pl.BlockSpec((tm, D), lambda i: (i, 0))
    vec = pl.BlockSpec((1, D), lambda i: (0, 0))
    return pl.pallas_call(
        body, name=name, grid=(s // tm,), in_specs=[row, vec, row], out_specs=[row, vec, vec],
        out_shape=[jax.ShapeDtypeStruct((s, D), F32), jax.ShapeDtypeStruct((1, D), F32),
                   jax.ShapeDtypeStruct((1, D), F32)],
        compiler_params=_cp("arbitrary"),
    )(x3, w, target)


def _head_norm_math(o, w):
    lane = _iota((128, 128), 0) // 64
    bd = (lane == _iota((128, 128), 1) // 64).astype(F32)
    outs = []
    for op in _split(o, 128):
        ms = _dot("nn", op * op, bd, 2, 1) * (1.0 / 64)
        outs.append(op * lax.rsqrt(ms + EPS))
    return jnp.concatenate(outs, axis=1) * w


def _head_norm_fwd(o, w, *, name):
    s = o.shape[0]
    tm = _tile(s, 512, 8)

    def body(o_ref, w_ref, y_ref):
        y_ref[...] = _head_norm_math(o_ref[...], w_ref[...]).astype(BF16)

    row = pl.BlockSpec((tm, D), lambda i: (i, 0))
    vec = pl.BlockSpec((1, D), lambda i: (0, 0))
    return pl.pallas_call(
        body, name=name, grid=(s // tm,), in_specs=[row, vec], out_specs=row,
        out_shape=jax.ShapeDtypeStruct((s, D), BF16), compiler_params=_cp("parallel"),
    )(o, w)


def _head_norm_bwd(o, w, dymix, *, name):
    s = o.shape[0]
    tm = _tile(s, 512, 8)

    def body(o_ref, w_ref, dy_ref, do_ref, dw_ref):
        @pl.when(pl.program_id(0) == 0)
        def _():
            dw_ref[...] = jnp.zeros_like(dw_ref)

        _, vjp = jax.vjp(_head_norm_math, o_ref[...], w_ref[...])
        do, dw = vjp(dy_ref[...])
        do_ref[...] = do
        dw_ref[...] += dw

    row = pl.BlockSpec((tm, D), lambda i: (i, 0))
    vec = pl.BlockSpec((1, D), lambda i: (0, 0))
    return pl.pallas_call(
        body, name=name, grid=(s // tm,),
        in_specs=[row, vec, pl.BlockSpec((tm, D), lambda i: (i, 1))], out_specs=[row, vec],
        out_shape=[jax.ShapeDtypeStruct((s, D), F32), jax.ShapeDtypeStruct((1, D), F32)],
        compiler_params=_cp("arbitrary"),
    )(o, w, dymix)


SB_BQ = 256
SB_BK = 256


def _sb_consts():
    r = _iota((SB_BK, SB_BK), 0)
    c = _iota((SB_BK, SB_BK), 1)
    u_excl = (r > c).astype(BF16)
    u_incl = (r >= c).astype(BF16)
    return u_excl, u_incl


SB_LANES = 256
SB_NCH = SB_LANES // 64


def _nt(a, b):
    return lax.dot_general(a, b, (_DIMS["nt"], ((), ())), preferred_element_type=F32)


def _tn(a, b):
    return lax.dot_general(a, b, (_DIMS["tn"], ((), ())), preferred_element_type=F32)


def _nn(a, b):
    return jnp.dot(a, b, preferred_element_type=F32)


def _sb_heads(ref):
    out = []
    for hp in range(SB_LANES // 128):
        v = ref[:, 128 * hp:128 * (hp + 1)]
        first = _iota(v.shape, 1) < 64
        out += [jnp.where(first, v, 0).astype(BF16), jnp.where(first, 0, v).astype(BF16)]
    return out


SB_STRIP = 32


def _neg_abs(x):
    bits = lax.bitcast_convert_type(x, jnp.uint32) | jnp.uint32(0x80000000)
    return lax.bitcast_convert_type(bits, F32)


def _sb_block(ref, j):
    off = pl.multiple_of(j * SB_BK, SB_BK)
    return [ref[pl.ds(off, SB_BK), 128 * hp:128 * (hp + 1)] for hp in range(SB_NCH // 2)]


SB_DEAD = 104.0


def _sb_live(nlrun):
    m = nlrun[0]
    for x in nlrun[1:]:
        m = jnp.minimum(m, x)
    return jnp.min(m) < SB_DEAD


def _sb_strips():
    return [(r, pl.ds(r, SB_STRIP)) for r in range(0, SB_BQ, SB_STRIP)]


def _sb_diag_mask(r):
    return _iota((SB_STRIP, SB_BK), 1) < _iota((SB_STRIP, SB_BK), 0) + r


def _sb_soft(z, mask):
    e = jnp.exp(_neg_abs(z))
    nl = jnp.maximum(z, 0.0) + jnp.log(1.0 + e)
    if mask is not None:
        nl = jnp.where(mask, nl, 0.0)
    return e, nl


def _sb_split_to(hl_ref, rows, x):
    hi, lo = _terms(x, 2)
    hl_ref[rows, 0:SB_BK] = hi
    hl_ref[rows, SB_BK:2 * SB_BK] = lo


def _sb_stage_soft(z_ref, nl_ref, diag):
    for r, rows in _sb_strips():
        _, nl = _sb_soft(z_ref[rows, :], _sb_diag_mask(r) if diag else None)
        nl_ref[rows, 0:SB_BK] = nl.astype(BF16)


def _sb_stage_weights(z_ref, c_ref, a_ref, nlrun, diag):
    for r, rows in _sb_strips():
        a = jnp.exp(z_ref[rows, :] - c_ref[rows, :] - nlrun[r:r + SB_STRIP, :])
        if diag:
            a = jnp.where(_sb_diag_mask(r), a, 0.0)
        a_ref[rows, :] = a.astype(BF16)


def _sb_fwd(qkv, rides, *, name):
    s = qkv.shape[0]
    nq = s // SB_BQ
    ng = D // SB_LANES
    assert SB_BQ == SB_BK
    rd = _Rides(rides, [True] * len(rides))

    def body(*refs):
        (q_ref, k_ref, v_ref), (o_ref,), (zbuf, nlbuf, cbuf, abuf), handles = rd.split(refs, 3, 1, 4)
        i = pl.program_id(1)
        step_no = pl.program_id(0) * nq + i
        rd.run(handles, step_no == 0, step_no == ng * nq - 1)

        _, u_incl = _sb_consts()
        lane_a = _iota((SB_BQ, 128), 1) < 64
        qh = [q * 0.125 for q in _sb_heads(q_ref)]

        def tile(j, accs, nlrun, diag):
            kbs = _sb_block(k_ref, j)
            for c in range(SB_NCH):
                zbuf[c] = _nt(qh[c], kbs[c // 2])
            for c in range(SB_NCH):
                _sb_stage_soft(zbuf.at[c], nlbuf.at[c], diag)
                cbuf[c] = _nn(nlbuf[c], u_incl)
            for c in range(SB_NCH):
                _sb_stage_weights(zbuf.at[c], cbuf.at[c], abuf.at[c], nlrun[c], diag)
            nlrun = tuple(nlrun[c] + cbuf[c, :, 0:1] for c in range(SB_NCH))
            vbs = _sb_block(v_ref, j)
            outs = [_nn(abuf[c], vbs[c // 2]) for c in range(SB_NCH)]
            accs = tuple(acc + jnp.where(lane_a, outs[2 * hp], outs[2 * hp + 1]) for hp, acc in enumerate(accs))
            return accs, nlrun

        accs, nlrun = tile(i, (jnp.zeros((SB_BQ, 128), F32),) * (SB_NCH // 2),
                           (jnp.zeros((SB_BQ, 1), F32),) * SB_NCH, True)

        def step(carry):
            j, _, accs, nlrun = carry
            accs, nlrun = tile(j, accs, nlrun, False)
            return j - 1, _sb_live(nlrun), accs, nlrun

        _, _, accs, _ = lax.while_loop(lambda c: (c[0] >= 0) & c[1], step, (i - 1, _sb_live(nlrun), accs, nlrun))
        o_ref[...] = jnp.concatenate(accs, axis=1)

    return pl.pallas_call(
        body, name=name, grid=(ng, nq),
        in_specs=[pl.BlockSpec((SB_BQ, SB_LANES), lambda g, i: (i, g)),
                  pl.BlockSpec((s, SB_LANES), lambda g, i: (0, ng + g)),
                  pl.BlockSpec((s, SB_LANES), lambda g, i: (0, 2 * ng + g)), *rd.in_specs],
        out_specs=[pl.BlockSpec((SB_BQ, SB_LANES), lambda g, i: (i, g)), *rd.out_specs],
        out_shape=[jax.ShapeDtypeStruct((s, D), F32), *rd.out_shape],
        scratch_shapes=[pltpu.VMEM((SB_NCH, SB_BQ, SB_BK), F32), pltpu.VMEM((SB_NCH, SB_BQ, SB_BK), BF16),
                        pltpu.VMEM((SB_NCH, SB_BQ, SB_BK), F32), pltpu.VMEM((SB_NCH, SB_BQ, SB_BK), BF16),
                        *rd.scratch],
        compiler_params=_cp("arbitrary", "arbitrary"),
    )(qkv, qkv, qkv, *rides)


def _sb_bwd(qkv, o, do, rides, *, name):
    s = qkv.shape[0]
    nq = s // SB_BQ
    ng = D // SB_LANES
    nhp = SB_NCH // 2
    rd = _Rides(rides, [False] * len(rides))

    def body(*refs):
        ins, outs, scratch, handles = rd.split(refs, 5, 3, 11)
        q_ref, k_ref, v_ref, o_ref, do_ref = ins
        dq_ref, dk_hbm, dv_hbm = outs
        dk_acc, dv_acc, dk16, dv16, sems, zbuf, gbuf, hl, cbuf, abuf, dzbuf = scratch
        g_idx = pl.program_id(0)
        i = pl.program_id(1)
        step_no = g_idx * nq + i
        rd.run(handles, step_no == 0, step_no == ng * nq - 1)

        @pl.when(i == 0)
        def _():
            dk_acc[...] = jnp.zeros_like(dk_acc)
            dv_acc[...] = jnp.zeros_like(dv_acc)

        _, u_incl = _sb_consts()
        u2 = jnp.concatenate([u_incl, u_incl], axis=0)
        lane_a = _iota((SB_BQ, 128), 1) < 64
        lane_k = _iota((SB_BK, 128), 1) < 64
        qh = [q * 0.125 for q in _sb_heads(q_ref)]
        qf = [q_ref[:, 128 * hp:128 * (hp + 1)] for hp in range(nhp)]
        doh = _sb_heads(do_ref)
        dof = [do_ref[:, 128 * hp:128 * (hp + 1)].astype(BF16) for hp in range(nhp)]
        delta = []
        for hp in range(nhp):
            prod = dof[hp].astype(F32) * o_ref[:, 128 * hp:128 * (hp + 1)]
            delta += [jnp.sum(jnp.where(lane_a, prod, 0.0), axis=1, keepdims=True),
                      jnp.sum(jnp.where(lane_a, 0.0, prod), axis=1, keepdims=True)]

        def pre(slot, j):
            kbs = _sb_block(k_ref, j)
            vbs = _sb_block(v_ref, j)
            for c in range(SB_NCH):
                zbuf[slot, c] = _nt(qh[c], kbs[c // 2])
                gbuf[slot, c] = _nt(doh[c], vbs[c // 2])

        def stage_g(c, slot):
            for _, rows in _sb_strips():
                g = abuf[slot, c, rows, :].astype(F32) * gbuf[slot, c, rows, :]
                gbuf[slot, c, rows, :] = g
                _sb_split_to(hl.at[c], rows, g)

        def stage_dz(c, slot, grun, diag):
            for r, rows in _sb_strips():
                z = zbuf[slot, c, rows, :]
                g = gbuf[slot, c, rows, :]
                cs = (delta[c] - grun)[r:r + SB_STRIP, :] - cbuf[c, rows, :]
                sig = 1.0 / (1.0 + jnp.exp(-z))
                dz = g - (g + cs) * sig
                if diag:
                    dz = jnp.where(_sb_diag_mask(r), dz, 0.0)
                dzbuf[slot, c, rows, :] = dz.astype(BF16)

        def chain(slot, nlrun, grun, diag):
            for c in range(SB_NCH):
                _sb_stage_soft(zbuf.at[slot, c], hl.at[c], diag)
                cbuf[c] = _nn(hl[c, :, 0:SB_BK], u_incl)
            nl_tot = []
            for c in range(SB_NCH):
                _sb_stage_weights(zbuf.at[slot, c], cbuf.at[c], abuf.at[slot, c], nlrun[c], diag)
                nl_tot.append(cbuf[c, :, 0:1])
                stage_g(c, slot)
                cbuf[c] = _nn(hl[c], u2)
            g_tot = []
            for c in range(SB_NCH):
                stage_dz(c, slot, grun[c], diag)
                g_tot.append(cbuf[c, :, 0:1])
            return (tuple(a + b for a, b in zip(nlrun, nl_tot)), tuple(a + b for a, b in zip(grun, g_tot)))

        def post(slot, j, dqs):
            off = pl.multiple_of(j * SB_BK, SB_BK)
            kbs = _sb_block(k_ref, j)
            dq_t = [_nn(dzbuf[slot, c], kbs[c // 2]) for c in range(SB_NCH)]
            dk_t = [_tn(dzbuf[slot, c], qf[c // 2]) for c in range(SB_NCH)]
            dv_t = [_tn(abuf[slot, c], dof[c // 2]) for c in range(SB_NCH)]
            for hp in range(nhp):
                cols = slice(128 * hp, 128 * (hp + 1))
                dk_acc[pl.ds(off, SB_BK), cols] += 0.125 * jnp.where(lane_k, dk_t[2 * hp], dk_t[2 * hp + 1])
                dv_acc[pl.ds(off, SB_BK), cols] += jnp.where(lane_k, dv_t[2 * hp], dv_t[2 * hp + 1])
            return tuple(dq + jnp.where(lane_a, dq_t[2 * hp], dq_t[2 * hp + 1]) for hp, dq in enumerate(dqs))

        def tile(j, dqs, nlrun, grun, diag):
            pre(0, j)
            nlrun, grun = chain(0, nlrun, grun, diag)
            return post(0, j, dqs), nlrun, grun

        zero = (jnp.zeros((SB_BQ, 1), F32),) * SB_NCH
        dqs, nlrun, grun = tile(i, (jnp.zeros((SB_BQ, 128), F32),) * nhp, zero, zero, True)

        def step(carry):
            j, _, dqs, nlrun, grun = carry
            dqs, nlrun, grun = tile(j, dqs, nlrun, grun, False)
            return j - 1, _sb_live(nlrun), dqs, nlrun, grun

        carry = lax.while_loop(lambda c: (c[0] >= 0) & c[1], step, (i - 1, _sb_live(nlrun), dqs, nlrun, grun))
        dq_ref[...] = (0.125 * jnp.concatenate(carry[2], axis=1)).astype(BF16)

        def out_copies(g):
            cols = pl.ds(pl.multiple_of(g * SB_LANES, SB_LANES), SB_LANES)
            return (pltpu.make_async_copy(dk16, dk_hbm.at[:, cols], sems.at[0]),
                    pltpu.make_async_copy(dv16, dv_hbm.at[:, cols], sems.at[1]))

        @pl.when((i == nq - 1) & (g_idx > 0))
        def _():
            for cp in out_copies(g_idx - 1):
                cp.wait()

        @pl.when(i == nq - 1)
        def _():
            def narrow(r, carry):
                rows = pl.ds(pl.multiple_of(r * SB_BK, SB_BK), SB_BK)
                dk16[rows, :] = dk_acc[rows, :].astype(BF16)
                dv16[rows, :] = dv_acc[rows, :].astype(BF16)
                return carry

            lax.fori_loop(0, s // SB_BK, narrow, 0)
            for cp in out_copies(g_idx):
                cp.start()

        @pl.when((i == nq - 1) & (g_idx == ng - 1))
        def _():
            for cp in out_copies(g_idx):
                cp.wait()

    qblk = pl.BlockSpec((SB_BQ, SB_LANES), lambda g, i: (i, g))
    hbm = pl.BlockSpec(memory_space=pl.ANY)
    return pl.pallas_call(
        body, name=name, grid=(ng, nq),
        in_specs=[qblk, pl.BlockSpec((s, SB_LANES), lambda g, i: (0, ng + g)),
                  pl.BlockSpec((s, SB_LANES), lambda g, i: (0, 2 * ng + g)), qblk, qblk, *rd.in_specs],
        out_specs=[qblk, hbm, hbm, *rd.out_specs],
        out_shape=[jax.ShapeDtypeStruct((s, D), BF16)] * 3 + rd.out_shape,
        scratch_shapes=[pltpu.VMEM((s, SB_LANES), F32), pltpu.VMEM((s, SB_LANES), F32),
                        pltpu.VMEM((s, SB_LANES), BF16), pltpu.VMEM((s, SB_LANES), BF16),
                        pltpu.SemaphoreType.DMA((2,)),
                        pltpu.VMEM((1, SB_NCH, SB_BQ, SB_BK), F32), pltpu.VMEM((1, SB_NCH, SB_BQ, SB_BK), F32),
                        pltpu.VMEM((SB_NCH, SB_BQ, 2 * SB_BK), BF16), pltpu.VMEM((SB_NCH, SB_BQ, SB_BK), F32),
                        pltpu.VMEM((1, SB_NCH, SB_BQ, SB_BK), BF16), pltpu.VMEM((1, SB_NCH, SB_BQ, SB_BK), BF16),
                        *rd.scratch],
        compiler_params=_cp("arbitrary", "arbitrary"),
    )(qkv, qkv, qkv, o, do, *rides)


def _ssd_core(z, xpre, dtr, state, dtb, alog, dsk, nw):
    L = SSD_CHUNK
    xa = _silu(xpre)
    pieces = _split(xa, 128)
    xs = jnp.concatenate(pieces[:8], axis=1)
    bm, cm = pieces[8:10], pieces[10:12]
    dt = _softplus(dtr + dtb)
    a = dt * (-jnp.exp(alog))
    tri = (_iota((L, L), 0) >= _iota((L, L), 1)).astype(F32)
    a_cs = _dot("nn", tri, a, 1, 3)
    xc = xs * dt
    tril = _iota((L, L), 0) >= _iota((L, L), 1)
    lane_a = _iota((L, 128), 1) < 64
    acs_p = _split(a_cs, 128)
    xc_p = _split(xc, 128)
    ys, new_states = [], []
    for g in range(2):
        cb = _dot("nt", cm[g], bm[g])
        for pp in range(4):
            pair = 4 * g + pp
            acs = acs_p[pair]
            acs_t = acs.T
            xcp = xc_p[pair]
            st = state[pair]
            heads = []
            for hh in range(2):
                col = _take_fn(1, 64 * hh)(acs)
                row = _take_fn(0, 64 * hh)(acs_t)
                seg = col - row
                lm = jnp.where(tril, jnp.exp(jnp.where(tril, seg, 0.0)), 0.0)
                heads.append(_dot("nn", cb * lm, xcp))
            ydiag = jnp.where(lane_a, heads[0], heads[1])
            last = _take_fn(0, L - 1)(acs)
            snew = _dot("tn", xcp * jnp.exp(last - acs), bm[g])
            new_states.append(st * jnp.exp(_take_fn(1, L - 1)(acs_t)) + snew)
            yoff = _dot("nt", cm[g], st) * jnp.exp(acs)
            ys.append(ydiag + yoff)
    y = jnp.concatenate(ys, axis=1) + xs * dsk
    yg = y * _silu(z)
    outs = []
    for v in _split(yg, 512):
        outs.append(v * lax.rsqrt(jnp.mean(v * v, axis=-1, keepdims=True) + EPS))
    return jnp.concatenate(outs, axis=1) * nw, tuple(new_states)


XBC = 1536


def _ssd_conv(ext_ref, cw, cb):
    acc = cb
    for k in range(4):
        acc = acc + cw[k:k + 1, :] * ext_ref[pl.ds(HALO - 3 + k, SSD_CHUNK), :]
    return acc


def _ssd_fwd(z, xbc, dtr, cw, cb, lanes, rides, *, name):
    s = z.shape[0]
    L = SSD_CHUNK
    nc = s // L
    rd = _Rides(rides, [True] * len(rides))

    def body(*refs):
        ins, (y_ref, st_ref), (state, ext), handles = rd.split(refs, 7, 2, 2)
        z_ref, x_ref, h_ref, dtr_ref, cw_ref, cb_ref, ln_ref = ins
        c = pl.program_id(0)
        rd.run(handles, c == 0, c == nc - 1)

        @pl.when(c == 0)
        def _():
            state[...] = jnp.zeros_like(state)

        ext[0:HALO, :] = jnp.where(c == 0, 0.0, h_ref[...])
        ext[HALO:, :] = x_ref[...]
        xpre = _ssd_conv(ext, cw_ref[...], cb_ref[...])
        st_ref[0] = state[...]
        st_in = tuple(state[p] for p in range(8))
        yn, st_out = _ssd_core(z_ref[...], xpre, dtr_ref[...], st_in,
                               ln_ref[0:1, :], ln_ref[1:2, :], ln_ref[2:3, :], ln_ref[3:4, :])
        y_ref[...] = yn.astype(BF16)
        for p in range(8):
            state[p] = st_out[p]

    return pl.pallas_call(
        body, name=name, grid=(nc,),
        in_specs=[pl.BlockSpec((L, D), lambda c: (c, 0)),
                  pl.BlockSpec((L, XBC), lambda c: (c, 0)),
                  pl.BlockSpec((HALO, XBC), lambda c: (jnp.maximum(c * (L // HALO) - 1, 0), 0)),
                  pl.BlockSpec((L, D), lambda c: (c, 0)),
                  pl.BlockSpec((4, XBC), lambda c: (0, 0)),
                  pl.BlockSpec((1, XBC), lambda c: (0, 0)),
                  pl.BlockSpec((8, D), lambda c: (0, 0)), *rd.in_specs],
        out_specs=[pl.BlockSpec((L, D), lambda c: (c, 0)),
                   pl.BlockSpec((1, 8, 128, 128), lambda c: (c, 0, 0, 0)), *rd.out_specs],
        out_shape=[jax.ShapeDtypeStruct((s, D), BF16), jax.ShapeDtypeStruct((nc, 8, 128, 128), F32),
                   *rd.out_shape],
        scratch_shapes=[pltpu.VMEM((8, 128, 128), F32), pltpu.VMEM((L + HALO, XBC), F32), *rd.scratch],
        compiler_params=_cp("arbitrary"),
    )(z, xbc, xbc, dtr, cw, cb, lanes, *rides)


def _ssd_bwd(z, xbc, dtr, states, dymix, cw, cb, lanes, *, name):
    s = z.shape[0]
    L = SSD_CHUNK
    nc = s // L

    def body(z_ref, x_ref, h_ref, dtr_ref, st_ref, dy_ref, cw_ref, cb_ref, ln_ref,
             dz_ref, dx_ref, ddt_ref, dln_ref, dcv_ref, dstate, ext, dext):
        i = pl.program_id(0)
        c = nc - 1 - i

        @pl.when(i == 0)
        def _():
            dstate[...] = jnp.zeros_like(dstate)
            dext[...] = jnp.zeros_like(dext)
            dln_ref[...] = jnp.zeros_like(dln_ref)
            dcv_ref[...] = jnp.zeros_like(dcv_ref)

        ext[0:HALO, :] = jnp.where(c == 0, 0.0, h_ref[...])
        ext[HALO:, :] = x_ref[...]
        cwv = cw_ref[...]
        xpre = _ssd_conv(ext, cwv, cb_ref[...])
        st_in = tuple(st_ref[0, p] for p in range(8))
        _, vjp = jax.vjp(_ssd_core, z_ref[...], xpre, dtr_ref[...], st_in,
                         ln_ref[0:1, :], ln_ref[1:2, :], ln_ref[2:3, :], ln_ref[3:4, :])
        dz, dxpre, ddtr, dst, d0, d1, d2, d3 = vjp((dy_ref[...], tuple(dstate[p] for p in range(8))))
        for p in range(8):
            dstate[p] = dst[p]
        dz_ref[...] = dz.astype(BF16)
        ddt_ref[...] = ddtr.astype(BF16)
        dln_ref[0:4, :] += jnp.concatenate([d0, d1, d2, d3], axis=0)
        dext[0:L, :] = dxpre
        xcur = x_ref[...]
        dx = jnp.zeros((L, XBC), F32)
        rows = []
        for k in range(4):
            shifted = dext[pl.ds(3 - k, L), :]
            dx = dx + cwv[k:k + 1, :] * shifted
            rows.append(jnp.sum(shifted * xcur, axis=0, keepdims=True))
        rows.append(jnp.sum(dxpre, axis=0, keepdims=True))
        dx_ref[...] = dx.astype(BF16)
        dcv_ref[0:5, :] += jnp.concatenate(rows, axis=0)
        dext[L:L + HALO, :] = dxpre[0:HALO, :]

    rev = lambda i: (nc - 1 - i, 0)
    return pl.pallas_call(
        body, name=name, grid=(nc,),
        in_specs=[pl.BlockSpec((L, D), rev),
                  pl.BlockSpec((L, XBC), rev),
                  pl.BlockSpec((HALO, XBC), lambda i: (jnp.maximum((nc - 1 - i) * (L // HALO) - 1, 0), 0)),
                  pl.BlockSpec((L, D), rev),
                  pl.BlockSpec((1, 8, 128, 128), lambda i: (nc - 1 - i, 0, 0, 0)),
                  pl.BlockSpec((L, D), rev),
                  pl.BlockSpec((4, XBC), lambda i: (0, 0)),
                  pl.BlockSpec((1, XBC), lambda i: (0, 0)),
                  pl.BlockSpec((8, D), lambda i: (0, 0))],
        out_specs=[pl.BlockSpec((L, D), rev), pl.BlockSpec((L, XBC), rev), pl.BlockSpec((L, D), rev),
                   pl.BlockSpec((8, D), lambda i: (0, 0)), pl.BlockSpec((8, XBC), lambda i: (0, 0))],
        out_shape=[jax.ShapeDtypeStruct((s, D), BF16), jax.ShapeDtypeStruct((s, XBC), BF16),
                   jax.ShapeDtypeStruct((s, D), BF16), jax.ShapeDtypeStruct((8, D), F32),
                   jax.ShapeDtypeStruct((8, XBC), F32)],
        scratch_shapes=[pltpu.VMEM((8, 128, 128), F32), pltpu.VMEM((L + HALO, XBC), F32),
                        pltpu.VMEM((L + HALO, XBC), F32)],
        compiler_params=_cp("arbitrary"),
    )(z, xbc, xbc, dtr, states, dymix, cw, cb, lanes)


def _mem_attn_math(q, k, v):
    outs = []
    for qh, kh, vh in zip(_split(q, 256), _split(k, 256), _split(v, 256)):
        sc = _dot("nt", qh, kh) * (1.0 / 16.0)
        e = jnp.exp(sc - lax.stop_gradient(jnp.max(sc, axis=-1, keepdims=True)))
        p = e / jnp.sum(e, axis=-1, keepdims=True)
        outs.append(_dot("nn", p, vh))
    return jnp.concatenate(outs, axis=1)


def _mem_attn_fwd(q, k, v, *, name):
    s, m = q.shape[0], k.shape[0]
    tm = _tile(s, 1024, 8)

    def body(q_ref, k_ref, v_ref, o_ref):
        o_ref[...] = _mem_attn_math(q_ref[...].astype(F32), k_ref[...].astype(F32),
                                    v_ref[...].astype(F32)).astype(BF16)

    row = pl.BlockSpec((tm, D), lambda i: (i, 0))
    kv = pl.BlockSpec((m, D), lambda i: (0, 0))
    return pl.pallas_call(
        body, name=name, grid=(s // tm,), in_specs=[row, kv, kv], out_specs=row,
        out_shape=jax.ShapeDtypeStruct((s, D), BF16), compiler_params=_cp("parallel"),
    )(q, k, v)


def _mem_attn_bwd(q, k, v, do, *, name):
    s, m = q.shape[0], k.shape[0]
    tm = _tile(s, 1024, 8)

    def body(q_ref, k_ref, v_ref, do_ref, dq_ref, dk_ref, dv_ref):
        @pl.when(pl.program_id(0) == 0)
        def _():
            dk_ref[...] = jnp.zeros_like(dk_ref)
            dv_ref[...] = jnp.zeros_like(dv_ref)

        _, vjp = jax.vjp(_mem_attn_math, q_ref[...].astype(F32), k_ref[...].astype(F32),
                         v_ref[...].astype(F32))
        dq, dk, dv = vjp(do_ref[...])
        dq_ref[...] = dq.astype(BF16)
        dk_ref[...] += dk
        dv_ref[...] += dv

    row = pl.BlockSpec((tm, D), lambda i: (i, 0))
    kv = pl.BlockSpec((m, D), lambda i: (0, 0))
    return pl.pallas_call(
        body, name=name, grid=(s // tm,), in_specs=[row, kv, kv, row], out_specs=[row, kv, kv],
        out_shape=[jax.ShapeDtypeStruct((s, D), BF16), jax.ShapeDtypeStruct((m, D), F32),
                   jax.ShapeDtypeStruct((m, D), F32)],
        compiler_params=_cp("arbitrary"),
    )(q, k, v, do)


DFF = 2816
FFN_TC = 1408
FFN_TM = 512


FFN_CHUNKS = tuple((c, min(512, FFN_TC - c)) for c in range(0, FFN_TC, 512))


def _rows8(ref, r, cols):
    return ref[pl.ds(pl.multiple_of(r, HALO), HALO), cols]


def _shift_down(prev, cur, s):
    return jnp.where(_iota(cur.shape, 0) < s, pltpu.roll(prev, s, 0), pltpu.roll(cur, s, 0))


def _shift_up(cur, nxt, s):
    return jnp.where(_iota(cur.shape, 0) >= HALO - s, pltpu.roll(nxt, HALO - s, 0), pltpu.roll(cur, HALO - s, 0))


def _ffn_conv_strip(ext_ref, r, cols, cw, cb):
    prev, cur = _rows8(ext_ref, r, cols), _rows8(ext_ref, r + HALO, cols)
    return cb + cw[0:1, :] * _shift_down(prev, cur, 2) + cw[1:2, :] * _shift_down(prev, cur, 1) + cw[2:3, :] * cur


def _ffn_specs(s):
    tm, tc = FFN_TM, FFN_TC
    blk = pl.BlockSpec((tm, tc), lambda i, j: (i, j))
    halo = pl.BlockSpec((HALO, tc), lambda i, j: (jnp.maximum(i * (tm // HALO) - 1, 0), j))
    cw = pl.BlockSpec((3, tc), lambda i, j: (0, j))
    cb = pl.BlockSpec((1, tc), lambda i, j: (0, j))
    return tm, tc, blk, halo, cw, cb


def _glu_fwd(ug, uv, cwg, cwv, cbg, cbv, *, name):
    s = ug.shape[0]
    tm, tc, blk, halo, cw, cb = _ffn_specs(s)

    def body(g_ref, gh_ref, v_ref, vh_ref, cwg_ref, cwv_ref, cbg_ref, cbv_ref, f_ref, eg, ev):
        first = pl.program_id(0) == 0
        eg[0:HALO, :] = jnp.where(first, 0.0, gh_ref[...])
        eg[HALO:, :] = g_ref[...]
        ev[0:HALO, :] = jnp.where(first, 0.0, vh_ref[...])
        ev[HALO:, :] = v_ref[...]
        cwgv, cwvv, cbgv, cbvv = cwg_ref[...], cwv_ref[...], cbg_ref[...], cbv_ref[...]

        def step(t, carry):
            for c0, w in FFN_CHUNKS:
                cols = slice(c0, c0 + w)
                outs = []
                for h in range(2):
                    r = t * 16 + HALO * h
                    g = _ffn_conv_strip(eg, r, cols, cwgv[:, cols], cbgv[:, cols])
                    v = _ffn_conv_strip(ev, r, cols, cwvv[:, cols], cbvv[:, cols])
                    outs.append(_silu(g) * v)
                f_ref[pl.ds(pl.multiple_of(t * 16, 16), 16), cols] = jnp.concatenate(outs, axis=0).astype(BF16)
            return carry

        lax.fori_loop(0, tm // 16, step, 0)

    return pl.pallas_call(
        body, name=name, grid=(s // tm, DFF // tc),
        in_specs=[blk, halo, blk, halo, cw, cw, cb, cb], out_specs=blk,
        out_shape=jax.ShapeDtypeStruct((s, DFF), BF16),
        scratch_shapes=[pltpu.VMEM((tm + HALO, tc), F32)] * 2,
        compiler_params=_cp("parallel", "parallel"),
    )(ug, ug, uv, uv, cwg, cwv, cbg, cbv)


def _ffn_bwd(ug, uv, df, cwg, cwv, cbg, cbv, *, name):
    s = ug.shape[0]
    tm, tc = FFN_TM, FFN_TC
    nb = s // tm
    rows_ext = tm + HALO

    def body(g_ref, gp_ref, gn_ref, v_ref, vp_ref, vn_ref, df_ref, dfn_ref, cwg_ref, cwv_ref, cbg_ref, cbv_ref,
             dxg_ref, dxv_ref, dcg_ref, dcv_ref, eg, ev, edf, edg, edv, accg, accv):
        i = pl.program_id(1)
        first, last = i == 0, i == nb - 1

        @pl.when(first)
        def _():
            dcg_ref[...] = jnp.zeros_like(dcg_ref)
            dcv_ref[...] = jnp.zeros_like(dcv_ref)

        for e, prev, main, nxt in ((eg, gp_ref, g_ref, gn_ref), (ev, vp_ref, v_ref, vn_ref)):
            e[0:HALO, :] = jnp.where(first, 0.0, prev[...])
            e[HALO:HALO + tm, :] = main[...]
            e[HALO + tm:, :] = jnp.where(last, 0.0, nxt[...])
        edf[0:tm, :] = df_ref[...]
        edf[tm:, :] = jnp.where(last, 0.0, dfn_ref[...])
        accg[...] = jnp.zeros_like(accg)
        accv[...] = jnp.zeros_like(accv)
        cwgv, cwvv, cbgv, cbvv = cwg_ref[...], cwv_ref[...], cbg_ref[...], cbv_ref[...]

        def cotangents(t, carry):
            r = t * HALO
            for c0, w in FFN_CHUNKS:
                cols = slice(c0, c0 + w)
                g = _ffn_conv_strip(eg, r, cols, cwgv[:, cols], cbgv[:, cols])
                v = _ffn_conv_strip(ev, r, cols, cwvv[:, cols], cbvv[:, cols])
                dfs = _rows8(edf, r, cols)
                sg = _sigmoid(g)
                edv[pl.ds(pl.multiple_of(r, HALO), HALO), cols] = dfs * g * sg
                edg[pl.ds(pl.multiple_of(r, HALO), HALO), cols] = dfs * v * sg * (1.0 + g * (1.0 - sg))
            return carry

        lax.fori_loop(0, rows_ext // HALO, cotangents, 0)

        def conv_backward(t, carry):
            for c0, w in FFN_CHUNKS:
                cols = slice(c0, c0 + w)
                for edu, e, cw, dx_ref, acc in ((edg, eg, cwgv[:, cols], dxg_ref, accg),
                                                (edv, ev, cwvv[:, cols], dxv_ref, accv)):
                    dxs = []
                    for h in range(2):
                        r = t * 16 + HALO * h
                        cur, nxt = _rows8(edu, r, cols), _rows8(edu, r + HALO, cols)
                        up1, up2 = _shift_up(cur, nxt, 1), _shift_up(cur, nxt, 2)
                        x = _rows8(e, r + HALO, cols)
                        dxs.append(cw[2:3, :] * cur + cw[1:2, :] * up1 + cw[0:1, :] * up2)
                        acc[0, :, cols] += up2 * x
                        acc[1, :, cols] += up1 * x
                        acc[2, :, cols] += cur * x
                        acc[3, :, cols] += cur
                    dx_ref[pl.ds(pl.multiple_of(t * 16, 16), 16), cols] = jnp.concatenate(dxs, axis=0).astype(BF16)
            return carry

        lax.fori_loop(0, tm // 16, conv_backward, 0)
        for acc, dc_ref in ((accg, dcg_ref), (accv, dcv_ref)):
            dc_ref[0:4, :] += jnp.concatenate([jnp.sum(acc[k], axis=0, keepdims=True) for k in range(4)], axis=0)

    blk = pl.BlockSpec((tm, tc), lambda j, i: (i, j))
    nxt = pl.BlockSpec((HALO, tc), lambda j, i: (jnp.minimum((i + 1) * (tm // HALO), s // HALO - 1), j))
    prv = pl.BlockSpec((HALO, tc), lambda j, i: (jnp.maximum(i * (tm // HALO) - 1, 0), j))
    cw = pl.BlockSpec((3, tc), lambda j, i: (0, j))
    cb = pl.BlockSpec((1, tc), lambda j, i: (0, j))
    acc = pl.BlockSpec((8, tc), lambda j, i: (0, j))
    return pl.pallas_call(
        body, name=name, grid=(DFF // tc, nb),
        in_specs=[blk, prv, nxt, blk, prv, nxt, blk, nxt, cw, cw, cb, cb],
        out_specs=[blk, blk, acc, acc],
        out_shape=[jax.ShapeDtypeStruct((s, DFF), BF16)] * 2 + [jax.ShapeDtypeStruct((8, DFF), F32)] * 2,
        scratch_shapes=[pltpu.VMEM((tm + 2 * HALO, tc), F32)] * 2 + [pltpu.VMEM((rows_ext, tc), F32)] * 3
                       + [pltpu.VMEM((4, HALO, tc), F32)] * 2,
        compiler_params=_cp("parallel", "arbitrary"),
    )(ug, ug, ug, uv, uv, uv, df, df, cwg, cwv, cbg, cbv)


MESH = pl.DeviceIdType.MESH


def _all_gather(arrs, *, name):
    n = len(arrs)

    def body(*refs):
        x_refs, out_refs = refs[:n], refs[n:2 * n]
        send_sems, recv_sems, local_sems = refs[2 * n:]
        x, y, c = lax.axis_index("x"), lax.axis_index("y"), lax.axis_index("c")
        me, sibling = (x, y, c), (x, y, 1 - c)
        chips = [(1 - x, y), (x, 1 - y), (1 - x, 1 - y)]

        def blk(a, dev):
            return out_refs[a].at[4 * dev[0] + 2 * dev[1] + dev[2]]

        def copy(a, k, block, to, src=None):
            return pltpu.make_async_remote_copy(
                src_ref=blk(a, block) if src is None else src, dst_ref=blk(a, block),
                send_sem=send_sems.at[7 * a + k], recv_sem=recv_sems.at[7 * a + k],
                device_id=to, device_id_type=MESH)

        started = []
        mine = []
        for a in range(n):
            cp = pltpu.make_async_copy(x_refs[a], blk(a, me), local_sems.at[a])
            cp.start()
            mine.append(cp)
            first = [copy(a, 0, me, sibling, src=x_refs[a])]
            first += [copy(a, 1 + j, me, (*chip, c), src=x_refs[a]) for j, chip in enumerate(chips)]
            for cp in first:
                cp.start()
            started += first
        for a in range(n):
            for j, chip in enumerate(chips):
                copy(a, 1 + j, (*chip, c), me).wait_recv()
                fwd = copy(a, 4 + j, (*chip, c), sibling)
                fwd.start()
                started.append(fwd)
        for a in range(n):
            copy(a, 0, sibling, me).wait_recv()
            for j, chip in enumerate(chips):
                copy(a, 4 + j, (*chip, 1 - c), me).wait_recv()
        for cp in started:
            cp.wait_send()
        for cp in mine:
            cp.wait()

    any_spec = pl.BlockSpec(memory_space=pl.ANY)
    return pl.pallas_call(
        body, name=name,
        in_specs=[any_spec] * n, out_specs=[any_spec] * n,
        out_shape=[jax.ShapeDtypeStruct((NDEV,) + a.shape, a.dtype) for a in arrs],
        scratch_shapes=[pltpu.SemaphoreType.DMA((7 * n,)), pltpu.SemaphoreType.DMA((7 * n,)),
                        pltpu.SemaphoreType.DMA((n,))],
    )(*arrs)


class _Direct:
    SEMS = (pltpu.SemaphoreType.DMA((7,)), pltpu.SemaphoreType.DMA((7,)), pltpu.SemaphoreType.DMA((1,)))

    def __init__(self, src_ref, recv_ref, sems, gather):
        x, y, c = lax.axis_index("x"), lax.axis_index("y"), lax.axis_index("c")
        me = 4 * x + 2 * y + c
        send_sems, recv_sems, local_sem = sems
        src = (lambda pid: src_ref) if gather else (lambda pid: src_ref.at[pid])
        self.mine = pltpu.make_async_copy(src(me), recv_ref.at[me], local_sem.at[0])
        self.copies = []
        for k in range(1, NDEV):
            px = 1 - x if k & 4 else x
            py = 1 - y if k & 2 else y
            pc = 1 - c if k & 1 else c
            self.copies.append(pltpu.make_async_remote_copy(
                src_ref=src(4 * px + 2 * py + pc), dst_ref=recv_ref.at[me],
                send_sem=send_sems.at[k - 1], recv_sem=recv_sems.at[k - 1],
                device_id=(px, py, pc), device_id_type=MESH))

    def start(self):
        self.mine.start()
        for cp in self.copies:
            cp.start()

    def wait(self):
        for cp in self.copies:
            cp.wait_recv()
        for cp in self.copies:
            cp.wait_send()
        self.mine.wait()


def _recv_shape(src, gather):
    return jax.ShapeDtypeStruct(((NDEV,) + src.shape) if gather else src.shape, src.dtype)


class _Rides:
    def __init__(self, rides, gathers):
        self.n = len(rides)
        self.gathers = list(gathers)
        any_spec = pl.BlockSpec(memory_space=pl.ANY)
        self.in_specs = [any_spec] * self.n
        self.out_specs = [any_spec] * self.n
        self.out_shape = [_recv_shape(a, g) for a, g in zip(rides, gathers)]
        self.scratch = list(_Direct.SEMS) * self.n

    def split(self, refs, n_in, n_out, n_scratch):
        n = self.n
        ins, refs = refs[:n_in], refs[n_in:]
        rides, refs = refs[:n], refs[n:]
        outs, refs = refs[:n_out], refs[n_out:]
        gots, refs = refs[:n], refs[n:]
        scratch, sems = refs[:n_scratch], refs[n_scratch:]
        return ins, outs, scratch, (rides, gots, sems)

    def run(self, handles, first, last):
        rides, gots, sems = handles

        def all_of():
            return [_Direct(rides[a], gots[a], sems[3 * a:3 * a + 3], self.gathers[a]) for a in range(self.n)]

        @pl.when(first)
        def _():
            for e in all_of():
                e.start()

        @pl.when(last)
        def _():
            for e in all_of():
                e.wait()


def _exchange(arrs, gathers, *, name):
    rd = _Rides(arrs, gathers)

    def body(*refs):
        _, _, _, handles = rd.split(refs, 0, 0, 0)
        rd.run(handles, True, True)

    return pl.pallas_call(
        body, name=name, in_specs=rd.in_specs, out_specs=rd.out_specs, out_shape=rd.out_shape,
        scratch_shapes=rd.scratch,
    )(*arrs)


def _adamw(parts, w, m, v, *, name):
    r, cols = w.shape
    tm = _tile(r, 256, PACK_ALIGN)
    c1 = 1.0 - ADAM_B1 ** ADAM_STEP
    c2 = 1.0 - ADAM_B2 ** ADAM_STEP

    def body(p_ref, w_ref, m_ref, v_ref, g_ref, d_ref, nm_ref, nv_ref):
        g = p_ref[0].astype(F32)
        for i in range(1, NDEV):
            g = g + p_ref[i].astype(F32)
        nm = ADAM_B1 * m_ref[...] + (1.0 - ADAM_B1) * g
        nv = ADAM_B2 * v_ref[...] + (1.0 - ADAM_B2) * (g * g)
        d_ref[...] = -ADAM_LR * ((nm / c1) / (jnp.sqrt(nv / c2) + ADAM_EPS) + ADAM_WD * w_ref[...])
        g_ref[...] = g
        nm_ref[...] = nm
        nv_ref[...] = nv

    row = pl.BlockSpec((tm, cols), lambda i: (i, 0))
    return pl.pallas_call(
        body, name=name, grid=(r // tm,),
        in_specs=[pl.BlockSpec((NDEV, tm, cols), lambda i: (0, i, 0)), row, row, row],
        out_specs=[row] * 4, out_shape=[jax.ShapeDtypeStruct((r, cols), F32)] * 4,
        compiler_params=_cp("parallel"),
    )(parts, w, m, v)


PACK_ALIGN = 16


def _part_rows(shape):
    n = -(-math.prod(shape) // D)
    return n + (-n) % PACK_ALIGN


def _rows(a):
    flat = a.reshape(-1)
    pad = _part_rows(a.shape) * D - flat.shape[0]
    if pad:
        flat = jnp.concatenate([flat, jnp.zeros((pad,), flat.dtype)])
    return flat.reshape(-1, D)


def _pack(parts, total_rows):
    if all(math.prod(p.shape) % (PACK_ALIGN * D) for p in parts):
        return _pack_small(parts, total_rows)
    rows = [_rows(p) for p in parts]
    used = sum(r.shape[0] for r in rows)
    if total_rows > used:
        rows.append(jnp.zeros((total_rows - used, D), rows[0].dtype))
    return jnp.concatenate(rows, axis=0)


def _pack_small(parts, total_rows):
    flat, used = [], 0
    for p in parts:
        n, nr = math.prod(p.shape), _part_rows(p.shape)
        flat += [p.reshape(-1), jnp.zeros((nr * D - n,), p.dtype)]
        used += nr
    flat.append(jnp.zeros(((total_rows - used) * D,), parts[0].dtype))
    return jnp.concatenate(flat).reshape(total_rows, D)


def _unpack(buf, shapes, part_rows=_part_rows):
    out, r0 = [], 0
    for shp in shapes:
        n = math.prod(shp)
        out.append(buf[r0:r0 + part_rows(shp)].reshape(-1)[:n].reshape(shp))
        r0 += part_rows(shp)
    return out


def _tight_rows(shape):
    return -(-math.prod(shape) // D)


def _pack_tight(parts, total_rows):
    flat, used = [], 0
    for p in parts:
        n, nr = math.prod(p.shape), _tight_rows(p.shape)
        flat += [p.reshape(-1), jnp.zeros((nr * D - n,), p.dtype)]
        used += nr
    flat.append(jnp.zeros(((total_rows - used) * D,), parts[0].dtype))
    return jnp.concatenate(flat).reshape(total_rows, D)


SHARD = {"w_in": (D, 706), "w_out": (256, D), "w_mq": (128, D), "w_mk": (128, D), "w_mv": (128, D),
         "w_mo": (128, D), "w_up": (D, 704), "w_down": (352, D), "conv_ssd_w": (4, 192), "conv_ffn_w": (3, 704)}
GATHER_MID = ["w_out", "w_mq", "w_mk", "w_mv", "w_mo"]
GATHER_FFN = ["w_down"]
CONV_TAPS = ["conv_ssd_w", "conv_ffn_w"]
GRADS_PACKED = ["w_out", "w_mq", "w_mk", "w_mv", "w_mo", "w_down", "conv_ffn_w", "conv_ssd_w"]


def _layout(names):
    row0, r = {}, 0
    for n in names:
        row0[n] = r
        r += _part_rows(SHARD[n])
    return row0, r + (-r) % 128


SMALL = [("norm_mix_w", (1, D)), ("conv_ssd_b", (1, 1536)), ("dt_bias", (1, 16)), ("a_log", (1, 16)),
         ("d_skip", (1, 16)), ("ssd_norm_w", (1, D)), ("sb_norm_w", (1, D)), ("norm_mem_w", (1, D)),
         ("norm_memkv_w", (1, D)), ("norm_ffn_w", (1, D)), ("conv_ffn_b", (1, 5632)), ("norm_final_w", (D,))]
LOSS_ROW = sum(_tight_rows(_shp) for _, _shp in SMALL)
SMALL_ROWS = LOSS_ROW + 1 + (-(LOSS_ROW + 1)) % 8
ORDER = ["norm_mix_w", "w_in", "conv_ssd_w", "conv_ssd_b", "dt_bias", "a_log", "d_skip", "ssd_norm_w",
         "sb_norm_w", "w_out", "norm_mem_w", "norm_memkv_w", "w_mq", "w_mk", "w_mv", "w_mo", "norm_ffn_w",
         "w_up", "conv_ffn_w", "conv_ffn_b", "w_down", "norm_final_w"]


def _pad_rows(a, nr):
    n = a.shape[1]
    return jnp.concatenate([a, jnp.zeros((NDEV, nr * D - n), a.dtype)], axis=1).reshape(NDEV, nr, D)


def _group_sum(lanes):
    return lanes.reshape(16, 64).sum(axis=1).reshape(1, 16)


def kernel(x, mem, norm_mix_w, w_in, conv_ssd_w, conv_ssd_b, dt_bias, a_log, d_skip, ssd_norm_w, sb_norm_w, w_out, norm_mem_w, norm_memkv_w, w_mq, w_mk, w_mv, w_mo, norm_ffn_w, w_up, conv_ffn_w, conv_ffn_b, w_down, norm_final_w, loss_target, m_norm_mix_w, m_w_in, m_conv_ssd_w, m_conv_ssd_b, m_dt_bias, m_a_log, m_d_skip, m_ssd_norm_w, m_sb_norm_w, m_w_out, m_norm_mem_w, m_norm_memkv_w, m_w_mq, m_w_mk, m_w_mv, m_w_mo, m_norm_ffn_w, m_w_up, m_conv_ffn_w, m_conv_ffn_b, m_w_down, m_norm_final_w, v_norm_mix_w, v_w_in, v_conv_ssd_w, v_conv_ssd_b, v_dt_bias, v_a_log, v_d_skip, v_ssd_norm_w, v_sb_norm_w, v_w_out, v_norm_mem_w, v_norm_memkv_w, v_w_mq, v_w_mk, v_w_mv, v_w_mo, v_norm_ffn_w, v_w_up, v_conv_ffn_w, v_conv_ffn_b, v_w_down, v_norm_final_w):
    P = dict(norm_mix_w=norm_mix_w, w_in=w_in, conv_ssd_w=conv_ssd_w, conv_ssd_b=conv_ssd_b, dt_bias=dt_bias, a_log=a_log, d_skip=d_skip, ssd_norm_w=ssd_norm_w, sb_norm_w=sb_norm_w, w_out=w_out, norm_mem_w=norm_mem_w, norm_memkv_w=norm_memkv_w, w_mq=w_mq, w_mk=w_mk, w_mv=w_mv, w_mo=w_mo, norm_ffn_w=norm_ffn_w, w_up=w_up, conv_ffn_w=conv_ffn_w, conv_ffn_b=conv_ffn_b, w_down=w_down, norm_final_w=norm_final_w)
    M = dict(norm_mix_w=m_norm_mix_w, w_in=m_w_in, conv_ssd_w=m_conv_ssd_w, conv_ssd_b=m_conv_ssd_b, dt_bias=m_dt_bias, a_log=m_a_log, d_skip=m_d_skip, ssd_norm_w=m_ssd_norm_w, sb_norm_w=m_sb_norm_w, w_out=m_w_out, norm_mem_w=m_norm_mem_w, norm_memkv_w=m_norm_memkv_w, w_mq=m_w_mq, w_mk=m_w_mk, w_mv=m_w_mv, w_mo=m_w_mo, norm_ffn_w=m_norm_ffn_w, w_up=m_w_up, conv_ffn_w=m_conv_ffn_w, conv_ffn_b=m_conv_ffn_b, w_down=m_w_down, norm_final_w=m_norm_final_w)
    V = dict(norm_mix_w=v_norm_mix_w, w_in=v_w_in, conv_ssd_w=v_conv_ssd_w, conv_ssd_b=v_conv_ssd_b, dt_bias=v_dt_bias, a_log=v_a_log, d_skip=v_d_skip, ssd_norm_w=v_ssd_norm_w, sb_norm_w=v_sb_norm_w, w_out=v_w_out, norm_mem_w=v_norm_mem_w, norm_memkv_w=v_norm_memkv_w, w_mq=v_w_mq, w_mk=v_w_mk, w_mv=v_w_mv, w_mo=v_w_mo, norm_ffn_w=v_norm_ffn_w, w_up=v_w_up, conv_ffn_w=v_conv_ffn_w, conv_ffn_b=v_conv_ffn_b, w_down=v_w_down, norm_final_w=v_norm_final_w)
    small_shapes = [shp for _, shp in SMALL]

    def packed(src, names, dtype=F32):
        return _pack([src[n][0] for n in names], _layout(names)[1]).astype(dtype)

    def columns(g):
        return g.transpose(1, 0, 2).reshape(g.shape[1], NDEV * g.shape[2])

    g_in, g_taps = _all_gather([w_in[0].astype(BF16), packed(P, CONV_TAPS)], name="gather_w_in")
    W_in = columns(g_in)
    cw_ssd = g_taps[:, 0].reshape(NDEV, -1)[:, :768].reshape(NDEV, 4, 192).transpose(1, 0, 2).reshape(4, XBC)
    cw_ffn = (g_taps[:, PACK_ALIGN:PACK_ALIGN + 3].reshape(NDEV, -1)[:, :2112].reshape(NDEV, 3, 704)
              .transpose(1, 0, 2).reshape(3, 2 * DFF))
    W_z, W_xbc, W_dt, W_qkv = W_in[:, :D], W_in[:, D:D + XBC], W_in[:, D + XBC:D + XBC + 16], W_in[:, D + XBC + 16:]
    W_dtr = jnp.repeat(W_dt, 64, axis=1)
    cwg, cwv = cw_ffn[:, :DFF], cw_ffn[:, DFF:]
    cbg, cbv = conv_ffn_b[:, :DFF], conv_ffn_b[:, DFF:]
    rep = lambda p: jnp.repeat(p, 64, axis=1)
    lanes = jnp.concatenate([rep(dt_bias), rep(a_log), rep(d_skip), ssd_norm_w, jnp.zeros((4, D), F32)], axis=0)

    xs, tgt, mm = x[0], loss_target[0], mem[0]

    h1 = _norm_fwd(xs, norm_mix_w, name="norm_mix")
    z = _mm(h1, W_z, name="proj_z")
    xbc = _mm(h1, W_xbc, name="proj_xbc")
    dtr = _mm(h1, W_dtr, name="proj_dt")
    qkv = _mm(h1, W_qkv, name="proj_qkv", out_dtype=BF16)
    w_up16 = w_up[0].astype(BF16)
    y_ssd, states, g_ffn, g_up0 = _ssd_fwd(z, xbc, dtr, cw_ssd, conv_ssd_b, lanes,
                                           [packed(P, GATHER_FFN, BF16), w_up16[:D // 2]], name="ssd_fwd")
    o_sb, g_mid, g_up1 = _sb_fwd(qkv, [packed(P, GATHER_MID, BF16), w_up16[D // 2:]], name="sb_fwd")
    g_up = jnp.concatenate([g_up0, g_up1], axis=1)
    r_mid = _layout(GATHER_MID)[0]
    W_out = g_mid[:, r_mid["w_out"]:r_mid["w_out"] + 256].reshape(2 * D, D)
    W_mq, W_mk, W_mv, W_mo = [g_mid[:, r_mid[n]:r_mid[n] + 128].reshape(D, D)
                              for n in ("w_mq", "w_mk", "w_mv", "w_mo")]
    W_up = columns(g_up)
    W_down = g_ffn[:, 0:352].reshape(DFF, D)
    W_upg, W_upv = W_up[:, :DFF], W_up[:, DFF:]
    y_sb = _head_norm_fwd(o_sb, sb_norm_w, name="sb_norm")
    ymix = jnp.concatenate([y_ssd, y_sb], axis=1)
    x1 = _mm(ymix, W_out, add=xs, name="proj_out")
    h2 = _norm_fwd(x1, norm_mem_w, name="norm_mem")
    mn = _norm_fwd(mm, norm_memkv_w, name="norm_memkv")
    qm = _mm(h2, W_mq, name="mem_q", out_dtype=BF16)
    km = _mm(mn, W_mk, name="mem_k", out_dtype=BF16)
    vm = _mm(mn, W_mv, name="mem_v", out_dtype=BF16)
    om = _mem_attn_fwd(qm, km, vm, name="mem_attn")
    x2 = _mm(om, W_mo, add=x1, name="mem_o")
    h3 = _norm_fwd(x2, norm_ffn_w, name="norm_ffn")
    ug = _mm(h3, W_upg, name="ffn_up_g")
    uv = _mm(h3, W_upv, name="ffn_up_v")
    f = _glu_fwd(ug, uv, cwg, cwv, cbg, cbv, name="ffn_glu")
    x3 = _mm(f, W_down, add=x2, name="ffn_down")
    dx3, g_nfinal, loss_part = _final(x3, norm_final_w.reshape(1, D), tgt, name="final_loss")

    G = {}
    G["w_down"] = _mm(f, dx3, trans_a=True, name="g_w_down")
    df = _mm(dx3, W_down, trans_b=True, name="d_f")
    dupg, dupv, dcg, dcv = _ffn_bwd(ug, uv, df, cwg, cwv, cbg, cbv, name="ffn_glu_bwd")
    G["w_up"] = jnp.concatenate([_mm(h3, dupg, trans_a=True, name="g_w_up_g"),
                                 _mm(h3, dupv, trans_a=True, name="g_w_up_v")], axis=1)
    G["conv_ffn_w"] = jnp.concatenate([dcg[0:3], dcv[0:3]], axis=1)
    G["conv_ffn_b"] = jnp.concatenate([dcg[3:4], dcv[3:4]], axis=1)
    dh3 = _mm(dupg, W_upg, trans_b=True, name="d_h3_g")
    dh3 = _mm(dupv, W_upv, trans_b=True, add=dh3, name="d_h3_v")
    dx2, G["norm_ffn_w"] = _norm_bwd(x2, norm_ffn_w, dh3, dx3, name="norm_ffn_bwd")
    G["w_mo"] = _mm(om, dx2, trans_a=True, name="g_w_mo")
    dom = _mm(dx2, W_mo, trans_b=True, name="d_om")
    dqm, dkm, dvm = _mem_attn_bwd(qm, km, vm, dom, name="mem_attn_bwd")
    G["w_mq"] = _mm(h2, dqm, trans_a=True, name="g_w_mq")
    G["w_mk"] = _mm(mn, dkm, trans_a=True, name="g_w_mk")
    G["w_mv"] = _mm(mn, dvm, trans_a=True, name="g_w_mv")
    dh2 = _mm(dqm, W_mq, trans_b=True, name="d_h2")
    dmn = _mm(dkm, W_mk, trans_b=True, name="d_mn_k")
    dmn = _mm(dvm, W_mv, trans_b=True, add=dmn, name="d_mn_v")
    _, G["norm_memkv_w"] = _norm_bwd(mm, norm_memkv_w, dmn, None, name="norm_memkv_bwd")
    dx1, G["norm_mem_w"] = _norm_bwd(x1, norm_mem_w, dh2, dx2, name="norm_mem_bwd")
    G["w_out"] = _mm(ymix, dx1, trans_a=True, name="g_w_out")
    dymix = _mm(dx1, W_out, trans_b=True, name="d_ymix")
    do_sb, G["sb_norm_w"] = _head_norm_bwd(o_sb, sb_norm_w, dymix, name="sb_norm_bwd")

    dz, dxbc, ddtr, dlanes, dconv = _ssd_bwd(z, xbc, dtr, states, dymix, cw_ssd, conv_ssd_b, lanes, name="ssd_bwd")
    G["dt_bias"], G["a_log"], G["d_skip"] = [_group_sum(dlanes[i:i + 1]) for i in range(3)]
    G["ssd_norm_w"] = dlanes[3:4]
    G["conv_ssd_w"], G["conv_ssd_b"] = dconv[0:4], dconv[4:5]

    def col_slabs(g, cols):
        return g.reshape(g.shape[0], NDEV, cols).transpose(1, 0, 2).astype(BF16)

    def packed_slabs(names):
        parts = []
        for n in names:
            shp = SHARD[n]
            if shp[-1] == D:
                t = G[n].reshape((NDEV,) + shp)
            else:
                t = _pad_rows(G[n].reshape(shp[0], NDEV, shp[1]).transpose(1, 0, 2).reshape(NDEV, -1),
                              _part_rows(shp))
            parts.append(jnp.pad(t, ((0, 0), (0, _part_rows(shp) - t.shape[1]), (0, 0))))
        used = sum(t.shape[1] for t in parts)
        parts.append(jnp.zeros((NDEV, _layout(names)[1] - used, D), F32))
        return jnp.concatenate(parts, axis=1).astype(BF16)

    dq, dk, dv, recv_packed, recv_up = _sb_bwd(qkv, o_sb, do_sb, [packed_slabs(GRADS_PACKED), col_slabs(G["w_up"], 704)],
                                               name="sb_bwd")
    dproj = jnp.concatenate([dz, dxbc, ddtr, dq, dk, dv], axis=1)
    g_proj = _mm(h1, dproj, trans_a=True, name="g_w_in")
    c_dt = D + XBC
    G["w_in"] = jnp.concatenate([g_proj[:, :c_dt], g_proj[:, c_dt:c_dt + D].reshape(D, 16, 64).sum(axis=2),
                                 g_proj[:, c_dt + D:]], axis=1)
    W_proj = jnp.concatenate([W_z, W_xbc, W_dtr, W_qkv], axis=1)
    dh1, recv_in = _mm(dproj, W_proj, trans_b=True, rides=[col_slabs(G["w_in"], 706)], name="d_h1")
    dx, G["norm_mix_w"] = _norm_bwd(xs, norm_mix_w, dh1, dx1, name="norm_mix_bwd")
    G["norm_final_w"] = g_nfinal.reshape(D)

    small_g = _pack_tight([G[n] for n, _ in SMALL] + [loss_part], SMALL_ROWS)
    (parts_small,) = _exchange([small_g], [True], name="exchange_grads")
    outs_packed = _adamw(recv_packed, packed(P, GRADS_PACKED), packed(M, GRADS_PACKED), packed(V, GRADS_PACKED),
                         name="adamw_packed")
    outs_up = _adamw(recv_up, w_up[0], m_w_up[0], v_w_up[0], name="adamw_w_up")
    outs_in = _adamw(recv_in, w_in[0], m_w_in[0], v_w_in[0], name="adamw_w_in")
    outs_small = _adamw(parts_small, _pack_tight([P[n] for n, _ in SMALL], SMALL_ROWS),
                        _pack_tight([M[n] for n, _ in SMALL], SMALL_ROWS),
                        _pack_tight([V[n] for n, _ in SMALL], SMALL_ROWS), name="adamw_replicated")

    res = {}
    for i, kind in enumerate(("grad", "delta", "new_m", "new_v")):
        for n, val in zip(GRADS_PACKED, _unpack(outs_packed[i], [SHARD[n] for n in GRADS_PACKED])):
            res[kind, n] = val.reshape((1,) + SHARD[n])
        res[kind, "w_up"] = outs_up[i].reshape((1,) + SHARD["w_up"])
        res[kind, "w_in"] = outs_in[i].reshape((1,) + SHARD["w_in"])
        for (n, shp), val in zip(SMALL, _unpack(outs_small[i], small_shapes, _tight_rows)):
            res[kind, n] = val
    loss = outs_small[0][LOSS_ROW, 0]
    out = [loss, dx.reshape(1, -1, D)]
    for kind in ("grad", "delta", "new_m", "new_v"):
        out += [res[kind, n] for n in ORDER]
    return tuple(out)
```
